```python
import math
import jax, jax.numpy as jnp
from jax import lax
import numpy as np

D_MODEL = 2048
BATCH = 2
SEQ = 4096
DEPTH = 2
DEC_BATCH = 32
DEC_SEQ = 4
PAST_LEN = 16384
PAGE_SIZE = 128

N_A_LAYERS = DEPTH // 2
N_B_LAYERS = DEPTH - N_A_LAYERS
LRU_WIDTH = D_MODEL
LRU_BLOCKS = 8
LRU_BLOCK = LRU_WIDTH // LRU_BLOCKS
CONV_WIDTH = 4
LRU_C = 8.0
N_HEADS = 32
HEAD_DIM = 64
N_KV_HEADS = 8
GROUP = N_HEADS // N_KV_HEADS
WINDOW = 128
N_BUCKETS = 32
MAX_DISTANCE = 128
N_EXPERTS = 16
N_GROUPS = 4
EXPERTS_PER_GROUP = N_EXPERTS // N_GROUPS
TOP_K = 2
D_EXPERT = 1024
ALPHA = (2 * DEPTH) ** 0.25
BETA = (8 * DEPTH) ** -0.25
LN_EPS = 1e-5

kernel_name = 'hawk_yoco_swa_sink_grouped_moe_step'


def layer_norm(x, g, b):
    xf = x.astype(jnp.float32)
    mu = jnp.mean(xf, axis=-1, keepdims=True)
    var = jnp.mean(jnp.square(xf - mu), axis=-1, keepdims=True)
    y = (xf - mu) * lax.rsqrt(var + LN_EPS) * g.astype(jnp.float32) + b.astype(jnp.float32)
    return y.astype(x.dtype)


def linear_scan(a, b, h0):
    def step(h, ab):
        a_t, b_t = ab
        h = a_t * h + b_t
        return h, h
    h_last, hs = lax.scan(step, h0, (jnp.swapaxes(a, 0, 1), jnp.swapaxes(b, 0, 1)))
    return jnp.swapaxes(hs, 0, 1), h_last


def rglru_block(x, conv_buf, h0, w_x, b_x, w_y, b_y, conv_w, conv_b,
                w_rg, b_rg, w_ig, b_ig, lam, w_out, b_out):
    B, T, _ = x.shape
    xb = x @ w_x + b_x
    yb = jax.nn.gelu(x @ w_y + b_y)
    xpad = jnp.concatenate([conv_buf.astype(xb.dtype), xb], axis=1)
    new_conv = xpad[:, -(CONV_WIDTH - 1):]
    xc = lax.conv_general_dilated(
        xpad, conv_w[:, None, :].astype(xpad.dtype), window_strides=(1,), padding='VALID',
        dimension_numbers=('NWC', 'WIO', 'NWC'), feature_group_count=LRU_WIDTH) + conv_b
    xblk = xc.reshape(B, T, LRU_BLOCKS, LRU_BLOCK)
    r = jax.nn.sigmoid(jnp.einsum('btnc,ncd->btnd', xblk, w_rg).reshape(B, T, LRU_WIDTH) + b_rg)
    i = jax.nn.sigmoid(jnp.einsum('btnc,ncd->btnd', xblk, w_ig).reshape(B, T, LRU_WIDTH) + b_ig)
    log_a = LRU_C * r.astype(jnp.float32) * jax.nn.log_sigmoid(lam.astype(jnp.float32))
    a = jnp.exp(log_a)
    u = xc.astype(jnp.float32) * i.astype(jnp.float32) * jnp.sqrt(-jnp.expm1(2.0 * log_a))
    hs, h_last = linear_scan(a, u, h0.astype(jnp.float32))
    out = (hs.astype(x.dtype) * yb) @ w_out + b_out
    return out, new_conv, h_last


def rel_bucket(dist):
    n = jnp.maximum(dist, 0)
    max_exact = N_BUCKETS // 2
    nf = jnp.maximum(n, 1).astype(jnp.float32)
    large = max_exact + (jnp.log(nf / max_exact) / math.log(MAX_DISTANCE / max_exact)
                         * (N_BUCKETS - max_exact)).astype(jnp.int32)
    large = jnp.minimum(large, N_BUCKETS - 1)
    return jnp.where(n < max_exact, n, large)


def sink_window_attention(q, k, v, dist, valid, rel_table, sinks):
    B, NB, Lq = q.shape[:3]
    Lk = k.shape[2]
    qg = q.reshape(B, NB, Lq, N_KV_HEADS, GROUP, HEAD_DIM)
    s = jnp.einsum('bnqkgd,bnskd->bnkgqs', qg, k,
                   preferred_element_type=jnp.float32) * (HEAD_DIM ** -0.5)
    bias = rel_table[rel_bucket(dist)].astype(jnp.float32)
    bias = jnp.transpose(bias, (2, 0, 1)).reshape(N_KV_HEADS, GROUP, Lq, Lk)
    s = jnp.where(valid[None, :, None, None], s + bias, -jnp.inf)
    sink = jnp.broadcast_to(sinks.astype(jnp.float32).reshape(N_KV_HEADS, GROUP, 1, 1),
                            s.shape[:-1] + (1,))
    p = jax.nn.softmax(jnp.concatenate([s, sink], axis=-1), axis=-1)[..., :-1]
    o = jnp.einsum('bnkgqs,bnskd->bnqkgd', p.astype(v.dtype), v)
    return o.reshape(B, NB, Lq, N_HEADS * HEAD_DIM)


def attn_prompt(x, k, v, w_q, w_o, rel_table, sinks):
    B, T, _ = x.shape
    NB = T // WINDOW
    q = (x @ w_q).reshape(B, NB, WINDOW, N_HEADS, HEAD_DIM)
    pad = jnp.zeros((B, WINDOW, N_KV_HEADS, HEAD_DIM), k.dtype)

    def band(t):
        tp = jnp.concatenate([pad, t], axis=1)
        prev = tp[:, :T].reshape(B, NB, WINDOW, N_KV_HEADS, HEAD_DIM)
        cur = t.reshape(B, NB, WINDOW, N_KV_HEADS, HEAD_DIM)
        return jnp.concatenate([prev, cur], axis=2)

    qi = jnp.arange(WINDOW)[:, None]
    kj = jnp.arange(2 * WINDOW)[None, :]
    dist = qi + WINDOW - kj
    key_pos = (jnp.arange(NB)[:, None, None] - 1) * WINDOW + kj[None]
    valid = (dist >= 0) & (dist < WINDOW) & (key_pos >= 0)
    o = sink_window_attention(q, band(k), band(v), dist, valid, rel_table, sinks)
    return o.reshape(B, T, N_HEADS * HEAD_DIM) @ w_o


def attn_sample(x, k_all, v_all, w_q, w_o, rel_table, sinks):
    B, T, _ = x.shape
    q = (x @ w_q).reshape(B, 1, T, N_HEADS, HEAD_DIM)
    dist = jnp.arange(T)[:, None] + WINDOW - jnp.arange(WINDOW + T)[None, :]
    valid = ((dist >= 0) & (dist < WINDOW))[None]
    o = sink_window_attention(q, k_all[:, None], v_all[:, None], dist, valid, rel_table, sinks)
    return o.reshape(B, T, N_HEADS * HEAD_DIM) @ w_o


def shared_kv(x, w_kv):
    B, T, _ = x.shape
    kv = (x @ w_kv).reshape(B, T, 2, N_KV_HEADS, HEAD_DIM)
    return kv[:, :, 0], kv[:, :, 1]


def grouped_moe(x, w_router, b_router, w_gate, w_up, w_down):
    shp = x.shape
    xt = x.reshape(-1, D_MODEL)
    n_tok = xt.shape[0]
    aff = jax.nn.sigmoid((xt @ w_router).astype(jnp.float32))
    sel = aff + b_router.astype(jnp.float32)
    grp_score = lax.top_k(sel.reshape(n_tok, N_GROUPS, EXPERTS_PER_GROUP), TOP_K)[0].sum(-1)
    g_idx = jnp.argmax(grp_score, axis=-1)
    in_group = g_idx[:, None] == (jnp.arange(N_EXPERTS) // EXPERTS_PER_GROUP)[None, :]
    _, e_idx = lax.top_k(jnp.where(in_group, sel, -jnp.inf), TOP_K)
    w = jnp.take_along_axis(aff, e_idx, axis=-1)
    w = w / jnp.sum(w, axis=-1, keepdims=True)
    gates = jnp.sum(jax.nn.one_hot(e_idx, N_EXPERTS, dtype=jnp.float32) * w[..., None], axis=1)
    h = jax.nn.silu(jnp.einsum('nd,edf->nef', xt, w_gate)) * jnp.einsum('nd,edf->nef', xt, w_up)
    y = jnp.einsum('nef,efd->nd', h * gates[..., None].astype(h.dtype), w_down)
    return y.reshape(shp)


def setup_inputs(seed: int = 0) -> dict:
    key = jax.random.key(seed)
    ks = iter(jax.random.split(key, 40))

    def nrm(shape, scale):
        return jax.random.normal(next(ks), shape, jnp.float32) * scale

    u = jax.random.uniform(next(ks), (N_A_LAYERS, LRU_WIDTH), jnp.float32, 0.9, 0.999)
    s = u ** (1.0 / LRU_C)
    lam = jnp.log(s) - jnp.log1p(-s)
    hd_all = N_HEADS * HEAD_DIM
    return {
        'x_prompt': nrm((BATCH, SEQ, D_MODEL), 1.0),
        'x_sample': nrm((DEC_BATCH, DEC_SEQ, D_MODEL), 1.0),
        'state_conv': nrm((N_A_LAYERS, DEC_BATCH, CONV_WIDTH - 1, LRU_WIDTH), 1.0),
        'state_rnn': nrm((N_A_LAYERS, DEC_BATCH, LRU_WIDTH), 1.0),
        'cache_k_win': nrm((DEC_BATCH, WINDOW, N_KV_HEADS, HEAD_DIM), 1.0),
        'cache_v_win': nrm((DEC_BATCH, WINDOW, N_KV_HEADS, HEAD_DIM), 1.0),
        'ln_g': 1.0 + nrm((DEPTH, 2, D_MODEL), 0.02),
        'ln_b': nrm((DEPTH, 2, D_MODEL), 0.02),
        'lru_w_x': nrm((N_A_LAYERS, D_MODEL, LRU_WIDTH), D_MODEL ** -0.5),
        'lru_b_x': nrm((N_A_LAYERS, LRU_WIDTH), 0.01),
        'lru_w_y': nrm((N_A_LAYERS, D_MODEL, LRU_WIDTH), D_MODEL ** -0.5),
        'lru_b_y': nrm((N_A_LAYERS, LRU_WIDTH), 0.01),
        'lru_conv_w': nrm((N_A_LAYERS, CONV_WIDTH, LRU_WIDTH), CONV_WIDTH ** -0.5),
        'lru_conv_b': nrm((N_A_LAYERS, LRU_WIDTH), 0.01),
        'lru_w_rg': nrm((N_A_LAYERS, LRU_BLOCKS, LRU_BLOCK, LRU_BLOCK), LRU_BLOCK ** -0.5),
        'lru_b_rg': nrm((N_A_LAYERS, LRU_WIDTH), 0.01),
        'lru_w_ig': nrm((N_A_LAYERS, LRU_BLOCKS, LRU_BLOCK, LRU_BLOCK), LRU_BLOCK ** -0.5),
        'lru_b_ig': nrm((N_A_LAYERS, LRU_WIDTH), 0.01),
        'lru_lam': lam,
        'lru_w_out': nrm((N_A_LAYERS, LRU_WIDTH, D_MODEL), LRU_WIDTH ** -0.5 * BETA),
        'lru_b_out': nrm((N_A_LAYERS, D_MODEL), 0.01),
        'attn_w_kv': nrm((D_MODEL, 2 * N_KV_HEADS * HEAD_DIM), D_MODEL ** -0.5),
        'attn_w_q': nrm((N_B_LAYERS, D_MODEL, hd_all), D_MODEL ** -0.5),
        'attn_w_o': nrm((N_B_LAYERS, hd_all, D_MODEL), hd_all ** -0.5 * BETA),
        'attn_sinks': nrm((N_B_LAYERS, N_HEADS), 0.5),
        'rel_bias': nrm((N_BUCKETS, N_HEADS), 0.5),
        'moe_w_router': nrm((D_MODEL, N_EXPERTS), D_MODEL ** -0.5),
        'moe_b_router': nrm((N_EXPERTS,), 0.01),
        'moe_w_gate': nrm((DEPTH, N_EXPERTS, D_MODEL, D_EXPERT), D_MODEL ** -0.5),
        'moe_w_up': nrm((DEPTH, N_EXPERTS, D_MODEL, D_EXPERT), D_MODEL ** -0.5),
        'moe_w_down': nrm((DEPTH, N_EXPERTS, D_EXPERT, D_MODEL), D_EXPERT ** -0.5 * BETA),
    }


def reference(x_prompt, x_sample, state_conv, state_rnn, cache_k_win, cache_v_win,
              ln_g, ln_b,
              lru_w_x, lru_b_x, lru_w_y, lru_b_y, lru_conv_w, lru_conv_b,
              lru_w_rg, lru_b_rg, lru_w_ig, lru_b_ig, lru_lam, lru_w_out, lru_b_out,
              attn_w_kv, attn_w_q, attn_w_o, attn_sinks, rel_bias,
              moe_w_router, moe_b_router, moe_w_gate, moe_w_up, moe_w_down):

    def trunk(x, conv0, h0, k_past, v_past):
        prompt = k_past is None
        new_conv, new_h = [], []
        k_all = v_all = k_win = v_win = None
        for l in range(DEPTH):
            if l < N_A_LAYERS:
                mix, c, h = rglru_block(
                    x, conv0[l], h0[l], lru_w_x[l], lru_b_x[l], lru_w_y[l], lru_b_y[l],
                    lru_conv_w[l], lru_conv_b[l], lru_w_rg[l], lru_b_rg[l],
                    lru_w_ig[l], lru_b_ig[l], lru_lam[l], lru_w_out[l], lru_b_out[l])
                new_conv.append(c)
                new_h.append(h)
            else:
                j = l - N_A_LAYERS
                if prompt:
                    mix = attn_prompt(x, k_all, v_all, attn_w_q[j], attn_w_o[j], rel_bias, attn_sinks[j])
                else:
                    mix = attn_sample(x, k_all, v_all, attn_w_q[j], attn_w_o[j], rel_bias, attn_sinks[j])
            x = layer_norm(ALPHA * x + mix, ln_g[l, 0], ln_b[l, 0])
            ffn = grouped_moe(x, moe_w_router, moe_b_router, moe_w_gate[l], moe_w_up[l], moe_w_down[l])
            x = layer_norm(ALPHA * x + ffn, ln_g[l, 1], ln_b[l, 1])
            if l == N_A_LAYERS - 1:
                k_sh, v_sh = shared_kv(x, attn_w_kv)
                if prompt:
                    k_all, v_all = k_sh, v_sh
                else:
                    k_all = jnp.concatenate([k_past.astype(k_sh.dtype), k_sh], axis=1)
                    v_all = jnp.concatenate([v_past.astype(v_sh.dtype), v_sh], axis=1)
                k_win = k_all[:, -WINDOW:]
                v_win = v_all[:, -WINDOW:]
        return x, jnp.stack(new_conv), jnp.stack(new_h), k_win, v_win

    bp = x_prompt.shape[0]
    conv_zero = jnp.zeros((N_A_LAYERS, bp, CONV_WIDTH - 1, LRU_WIDTH), x_prompt.dtype)
    h_zero = jnp.zeros((N_A_LAYERS, bp, LRU_WIDTH), jnp.float32)
    y_prompt, conv_p, rnn_p, k_p, v_p = trunk(x_prompt, conv_zero, h_zero, None, None)
    y_sample, conv_s, rnn_s, k_s, v_s = trunk(x_sample, state_conv, state_rnn, cache_k_win, cache_v_win)
    return (y_prompt, y_sample, conv_p, rnn_p, k_p, v_p, conv_s, rnn_s, k_s, v_s)
```

```python
import functools
import math

import jax
import jax.numpy as jnp
from jax import lax
from jax.experimental import pallas as pl
from jax.experimental.pallas import tpu as pltpu

D_MODEL = 2048
DEPTH = 2
LRU_BLOCKS = 8
LRU_BLOCK = D_MODEL // LRU_BLOCKS
CONV_WIDTH = 4
LRU_C = 8.0
N_HEADS = 32
HEAD_DIM = 64
N_KV_HEADS = 8
GROUP = N_HEADS // N_KV_HEADS
KV_DIM = N_KV_HEADS * HEAD_DIM
WINDOW = 128
N_BUCKETS = 32
MAX_DISTANCE = 128
N_EXPERTS = 16
N_GROUPS = 4
EXPERTS_PER_GROUP = N_EXPERTS // N_GROUPS
D_EXPERT = 1024
ALPHA = (2 * DEPTH) ** 0.25
LN_EPS = 1e-5

LANES = 128
MOE_TILE = 256
CAST_ROWS = 256
BF16 = jnp.bfloat16
F32 = jnp.float32
NEG_INF = float("-inf")


def _params(sem, vmem_mb):
    return pltpu.CompilerParams(dimension_semantics=sem, vmem_limit_bytes=vmem_mb * 1024 * 1024)


def _cast_rows(src_ref, dst_ref):
    n = src_ref.shape[0] // CAST_ROWS

    def body(i, c):
        r = pl.multiple_of(i * CAST_ROWS, CAST_ROWS)
        dst_ref[pl.ds(r, CAST_ROWS), :] = src_ref[pl.ds(r, CAST_ROWS), :].astype(BF16)
        return c

    lax.fori_loop(0, n, body, 0)


def _layer_norm(z, g, b):
    mu = jnp.mean(z, axis=-1, keepdims=True)
    zc = z - mu
    var = jnp.mean(zc * zc, axis=-1, keepdims=True)
    return zc * lax.rsqrt(var + LN_EPS) * g + b


def _linear_kernel(x_ref, w_ref, b_ref, o_ref, wbf_ref, *, act, scale):
    @pl.when(pl.program_id(0) == 0)
    def _():
        _cast_rows(w_ref, wbf_ref)

    y = jnp.dot(x_ref[...].astype(BF16), wbf_ref[...], preferred_element_type=F32)
    y = y + b_ref[...]
    if act == "gelu":
        y = jax.nn.gelu(y)
    if scale != 1.0:
        y = y * scale
    o_ref[...] = y.astype(o_ref.dtype)


def _linear(x, w, w_index, b, out_dtype, *, act=None, scale=1.0, tm=320):
    n, k = x.shape
    nout = w.shape[-1]
    lead = len(w_index)
    w_block = (None,) * lead + (k, nout)
    return pl.pallas_call(
        functools.partial(_linear_kernel, act=act, scale=scale),
        grid=(n // tm,),
        in_specs=[
            pl.BlockSpec((tm, k), lambda i: (i, 0)),
            pl.BlockSpec(w_block, lambda i: w_index + (0, 0), pipeline_mode=pl.Buffered(1)),
            pl.BlockSpec((1, nout), lambda i: (0, 0)),
        ],
        out_specs=pl.BlockSpec((tm, nout), lambda i: (i, 0)),
        out_shape=jax.ShapeDtypeStruct((n, nout), out_dtype),
        scratch_shapes=[pltpu.VMEM((k, nout), BF16)],
        compiler_params=_params(("arbitrary",), 56),
    )(x, w, b)


def _route(logits_t, b_router):
    aff = jax.nn.sigmoid(logits_t)
    sel = aff + b_router
    srow = [sel[e:e + 1, :] for e in range(N_EXPERTS)]
    arow = [aff[e:e + 1, :] for e in range(N_EXPERTS)]

    def top2_sum(v):
        pairs = [v[i] + v[j] for i in range(4) for j in range(i + 1, 4)]
        return functools.reduce(jnp.maximum, pairs)

    scores = [top2_sum(srow[4 * g:4 * g + 4]) for g in range(N_GROUPS)]
    best = scores[0]
    gi = jnp.zeros_like(best, dtype=jnp.int32)
    for g in range(1, N_GROUPS):
        upd = scores[g] > best
        best = jnp.where(upd, scores[g], best)
        gi = jnp.where(upd, g, gi)

    def pick_group(rows, j):
        out = rows[j]
        for g in range(1, N_GROUPS):
            out = jnp.where(gi == g, rows[4 * g + j], out)
        return out

    v = [pick_group(srow, j) for j in range(EXPERTS_PER_GROUP)]
    a = [pick_group(arow, j) for j in range(EXPERTS_PER_GROUP)]

    m1, i1 = v[0], jnp.zeros_like(gi)
    for j in range(1, EXPERTS_PER_GROUP):
        upd = v[j] > m1
        m1 = jnp.where(upd, v[j], m1)
        i1 = jnp.where(upd, j, i1)
    m2 = jnp.full_like(m1, NEG_INF)
    i2 = jnp.zeros_like(gi)
    for j in range(EXPERTS_PER_GROUP):
        cand = jnp.where(i1 == j, NEG_INF, v[j])
        upd = cand > m2
        m2 = jnp.where(upd, cand, m2)
        i2 = jnp.where(upd, j, i2)

    def pick_idx(rows, idx):
        out = rows[0]
        for j in range(1, EXPERTS_PER_GROUP):
            out = jnp.where(idx == j, rows[j], out)
        return out

    a1 = pick_idx(a, i1)
    a2 = pick_idx(a, i2)
    tot = a1 + a2
    e_idx = jnp.concatenate([gi * EXPERTS_PER_GROUP + i1, gi * EXPERTS_PER_GROUP + i2], axis=0)
    gates = jnp.concatenate([a1 / tot, a2 / tot], axis=0)
    return e_idx, gates


def _proj_ln_kernel(m_ref, w_ref, b_ref, res_ref, g_ref, beta_ref, wr_ref, br_ref,
                    x_ref, e_ref, gate_ref, wbf_ref):
    @pl.when(pl.program_id(0) == 0)
    def _():
        _cast_rows(w_ref, wbf_ref)

    y = jnp.dot(m_ref[...], wbf_ref[...], preferred_element_type=F32) + b_ref[...]
    x = _layer_norm(ALPHA * res_ref[...] + y, g_ref[...], beta_ref[...])
    x_ref[...] = x
    logits_t = lax.dot_general(wr_ref[...].astype(BF16), x.astype(BF16),
                               (((1,), (1,)), ((), ())), preferred_element_type=F32)
    e_idx, gates = _route(logits_t, br_ref[...])
    e_ref[...] = e_idx
    gate_ref[...] = gates


def _proj_ln(m, w, w_index, b, res, g, beta, wr_t, br, *, tm=320):
    n, k = m.shape
    lead = len(w_index)
    nt = n // tm
    row = lambda i: (i, 0)
    const = lambda i: (0, 0)
    x, e_idx, gates = pl.pallas_call(
        _proj_ln_kernel,
        grid=(nt,),
        in_specs=[
            pl.BlockSpec((tm, k), row),
            pl.BlockSpec((None,) * lead + (k, D_MODEL), lambda i: w_index + (0, 0),
                         pipeline_mode=pl.Buffered(1)),
            pl.BlockSpec((1, D_MODEL), const),
            pl.BlockSpec((tm, D_MODEL), row),
            pl.BlockSpec((1, D_MODEL), const),
            pl.BlockSpec((1, D_MODEL), const),
            pl.BlockSpec((N_EXPERTS, D_MODEL), const),
            pl.BlockSpec((N_EXPERTS, 1), const),
        ],
        out_specs=[
            pl.BlockSpec((tm, D_MODEL), row),
            pl.BlockSpec((None, 2, tm), lambda i: (i, 0, 0)),
            pl.BlockSpec((None, 2, tm), lambda i: (i, 0, 0)),
        ],
        out_shape=[
            jax.ShapeDtypeStruct((n, D_MODEL), F32),
            jax.ShapeDtypeStruct((nt, 2, tm), jnp.int32),
            jax.ShapeDtypeStruct((nt, 2, tm), F32),
        ],
        scratch_shapes=[pltpu.VMEM((k, D_MODEL), BF16)],
        compiler_params=_params(("arbitrary",), 56),
    )(m, w, b, res, g, beta, wr_t, br)
    e_idx = e_idx.transpose(1, 0, 2).reshape(2, n)
    gates = gates.transpose(1, 0, 2).reshape(2, n)
    return x, e_idx, gates


def _plan_kernel(e_ref, pos_ref, meta_ref, rank_ref):
    nrow = e_ref.shape[0]
    ri = lax.broadcasted_iota(jnp.int32, (LANES, LANES), 0)
    ci = lax.broadcasted_iota(jnp.int32, (LANES, LANES), 1)
    tri = jnp.where(ri <= ci, 1.0, 0.0).astype(BF16)
    sub = lax.broadcasted_iota(jnp.int32, (N_EXPERTS, LANES), 0)

    def count_body(r, base):
        onehot = sub == e_ref[pl.ds(r, 1), :]
        loc = jnp.dot(jnp.where(onehot, 1.0, 0.0).astype(BF16), tri, preferred_element_type=F32)
        rank_ref[pl.ds(r, 1), :] = jnp.sum(jnp.where(onehot, base + loc - 1.0, 0.0),
                                           axis=0, keepdims=True)
        return base + jnp.broadcast_to(loc[:, LANES - 1:LANES], (N_EXPERTS, LANES))

    count = lax.fori_loop(0, nrow, count_body, jnp.zeros((N_EXPERTS, LANES), F32))
    ntile = jnp.floor((count + (MOE_TILE - 1.0)) * (1.0 / MOE_TILE))
    offs = []
    acc = jnp.zeros((1, LANES), F32)
    for e in range(N_EXPERTS):
        offs.append(acc)
        acc = acc + ntile[e:e + 1, :]
    tile_off = jnp.concatenate(offs, axis=0)
    tile_end = tile_off + ntile
    lane = lax.broadcasted_iota(jnp.int32, (N_EXPERTS, LANES), 1).astype(F32)
    tile_expert = jnp.sum(jnp.where(tile_end <= lane, 1.0, 0.0), axis=0, keepdims=True)
    tile_expert = jnp.minimum(tile_expert, N_EXPERTS - 1.0)
    meta = jnp.concatenate([tile_expert, acc, jnp.zeros((6, LANES), F32)], axis=0)
    meta_ref[...] = meta.astype(jnp.int32)
    row_off = tile_off * float(MOE_TILE)

    def pos_body(r, c):
        onehot = sub == e_ref[pl.ds(r, 1), :]
        p = jnp.sum(jnp.where(onehot, row_off, 0.0), axis=0, keepdims=True) + rank_ref[pl.ds(r, 1), :]
        pos_ref[pl.ds(r, 1), :] = p.astype(jnp.int32)
        return c

    lax.fori_loop(0, nrow, pos_body, 0)


def _plan(e_idx):
    n2 = e_idx.shape[0] * e_idx.shape[1]
    e2d = e_idx.reshape(n2 // LANES, LANES)
    pos, meta = pl.pallas_call(
        _plan_kernel,
        out_shape=[jax.ShapeDtypeStruct(e2d.shape, jnp.int32),
                   jax.ShapeDtypeStruct((8, LANES), jnp.int32)],
        scratch_shapes=[pltpu.VMEM(e2d.shape, F32)],
    )(e2d)
    return pos.reshape(n2), meta[0], meta[1, :1]


def _scatter_kernel(pos_ref, te_ref, nu_ref, x_ref, xs_hbm, zero_ref, sem, zsem, *, n_tok, n_tiles):
    i = pl.program_id(0)
    tm = x_ref.shape[0]

    def tile_copy(t):
        return pltpu.make_async_copy(zero_ref, xs_hbm.at[pl.ds(t * MOE_TILE, MOE_TILE), :], zsem)

    @pl.when(i == 0)
    def _():
        zero_ref[...] = jnp.zeros_like(zero_ref)
        last_used = nu_ref[0] - 1

        def is_last(t):
            return jnp.logical_or(t >= last_used, te_ref[t + 1] != te_ref[t])

        def start_body(t, c):
            @pl.when(is_last(t))
            def _():
                tile_copy(t).start()
            return c

        def wait_body(t, c):
            @pl.when(is_last(t))
            def _():
                tile_copy(t).wait()
            return c

        lax.fori_loop(0, n_tiles, start_body, 0)
        lax.fori_loop(0, n_tiles, wait_body, 0)

    def row_body(r, c):
        t = i * tm + r
        for k in range(2):
            p = pos_ref[k * n_tok + t]
            pltpu.make_async_copy(x_ref.at[pl.ds(r, 1), :], xs_hbm.at[pl.ds(p, 1), :], sem).start()
        return c

    lax.fori_loop(0, tm, row_body, 0)
    for k in range(2):
        pltpu.make_async_copy(x_ref, xs_hbm.at[pl.ds(0, tm), :], sem).wait()


def _scatter(x, pos, tile_expert, n_used, n_rows, *, tm=320):
    n = x.shape[0]
    return pl.pallas_call(
        functools.partial(_scatter_kernel, n_tok=n, n_tiles=n_rows // MOE_TILE),
        grid_spec=pltpu.PrefetchScalarGridSpec(
            num_scalar_prefetch=3,
            grid=(n // tm,),
            in_specs=[pl.BlockSpec((tm, D_MODEL), lambda i, *_: (i, 0))],
            out_specs=pl.BlockSpec(memory_space=pl.ANY),
            scratch_shapes=[pltpu.VMEM((MOE_TILE, D_MODEL), x.dtype),
                            pltpu.SemaphoreType.DMA(()), pltpu.SemaphoreType.DMA(())],
        ),
        out_shape=jax.ShapeDtypeStruct((n_rows, D_MODEL), x.dtype),
        compiler_params=_params(("arbitrary",), 32),
    )(pos, tile_expert, n_used, x)


def _expert_changed(te_ref, i):
    return jnp.logical_or(i == 0, te_ref[i] != te_ref[jnp.maximum(i - 1, 0)])


def _moe_up_kernel(te_ref, nu_ref, xs_ref, wg_ref, wu_ref, h_ref, wg_bf, wu_bf):
    i = pl.program_id(0)

    @pl.when(i < nu_ref[0])
    def _():
        @pl.when(_expert_changed(te_ref, i))
        def _():
            _cast_rows(wg_ref, wg_bf)
            _cast_rows(wu_ref, wu_bf)

        x = xs_ref[...].astype(BF16)
        a = jnp.dot(x, wg_bf[...], preferred_element_type=F32)
        b = jnp.dot(x, wu_bf[...], preferred_element_type=F32)
        h_ref[...] = (jax.nn.silu(a) * b).astype(BF16)

    @pl.when(i >= nu_ref[0])
    def _():
        h_ref[...] = jnp.zeros_like(h_ref)


def _moe_down_kernel(te_ref, nu_ref, h_ref, wd_ref, y_ref, wd_bf):
    i = pl.program_id(0)

    @pl.when(i < nu_ref[0])
    def _():
        @pl.when(_expert_changed(te_ref, i))
        def _():
            _cast_rows(wd_ref, wd_bf)

        y_ref[...] = jnp.dot(h_ref[...], wd_bf[...], preferred_element_type=F32)

    @pl.when(i >= nu_ref[0])
    def _():
        y_ref[...] = jnp.zeros_like(y_ref)


def _moe_ffn(xs, tile_expert, n_used, w_gate, w_up, w_down, layer):
    n_rows = xs.shape[0]
    n_tiles = n_rows // MOE_TILE

    def tile(i, te, nu):
        return (jnp.minimum(i, nu[0] - 1), 0)

    def expert(i, te, nu):
        return (layer, te[jnp.minimum(i, nu[0] - 1)], 0, 0)

    h = pl.pallas_call(
        _moe_up_kernel,
        grid_spec=pltpu.PrefetchScalarGridSpec(
            num_scalar_prefetch=2,
            grid=(n_tiles,),
            in_specs=[pl.BlockSpec((MOE_TILE, D_MODEL), tile),
                      pl.BlockSpec((None, None, D_MODEL, D_EXPERT), expert),
                      pl.BlockSpec((None, None, D_MODEL, D_EXPERT), expert)],
            out_specs=pl.BlockSpec((MOE_TILE, D_EXPERT), lambda i, te, nu: (i, 0)),
            scratch_shapes=[pltpu.VMEM((D_MODEL, D_EXPERT), BF16),
                            pltpu.VMEM((D_MODEL, D_EXPERT), BF16)],
        ),
        out_shape=jax.ShapeDtypeStruct((n_rows, D_EXPERT), BF16),
        compiler_params=_params(("arbitrary",), 56),
    )(tile_expert, n_used, xs, w_gate, w_up)
    return pl.pallas_call(
        _moe_down_kernel,
        grid_spec=pltpu.PrefetchScalarGridSpec(
            num_scalar_prefetch=2,
            grid=(n_tiles,),
            in_specs=[pl.BlockSpec((MOE_TILE, D_EXPERT), tile),
                      pl.BlockSpec((None, None, D_EXPERT, D_MODEL), expert)],
            out_specs=pl.BlockSpec((MOE_TILE, D_MODEL), lambda i, te, nu: (i, 0)),
            scratch_shapes=[pltpu.VMEM((D_EXPERT, D_MODEL), BF16)],
        ),
        out_shape=jax.ShapeDtypeStruct((n_rows, D_MODEL), F32),
        compiler_params=_params(("arbitrary",), 40),
    )(tile_expert, n_used, h, w_down)


def _combine_ln_kernel(pos_ref, ys_hbm, res_ref, gate_ref, g_ref, beta_ref, *rest, n_tok, n_out):
    outs, (buf0, buf1, sem) = rest[:n_out], rest[n_out:]
    i = pl.program_id(0)
    tm = res_ref.shape[0]
    bufs = (buf0, buf1)

    def row_body(r, c):
        t = i * tm + r
        for k in range(2):
            p = pos_ref[k * n_tok + t]
            pltpu.make_async_copy(ys_hbm.at[pl.ds(p, 1), :], bufs[k].at[pl.ds(r, 1), :], sem).start()
        return c

    lax.fori_loop(0, tm, row_body, 0)
    for k in range(2):
        pltpu.make_async_copy(ys_hbm.at[pl.ds(0, tm), :], bufs[k], sem).wait()

    gate = gate_ref[...]
    ffn = gate[:, 0:1] * buf0[...] + gate[:, 1:2] * buf1[...]
    x = _layer_norm(ALPHA * res_ref[...] + ffn, g_ref[...], beta_ref[...])
    for o_ref in outs:
        o_ref[...] = x.astype(o_ref.dtype)


def _combine_ln(ys, pos, res, gates_col, g, beta, out_dtypes, *, tm=320):
    n = res.shape[0]
    row = lambda i, *_: (i, 0)
    const = lambda i, *_: (0, 0)
    return pl.pallas_call(
        functools.partial(_combine_ln_kernel, n_tok=n, n_out=len(out_dtypes)),
        grid_spec=pltpu.PrefetchScalarGridSpec(
            num_scalar_prefetch=1,
            grid=(n // tm,),
            in_specs=[pl.BlockSpec(memory_space=pl.ANY),
                      pl.BlockSpec((tm, D_MODEL), row),
                      pl.BlockSpec((tm, 2), row),
                      pl.BlockSpec((1, D_MODEL), const),
                      pl.BlockSpec((1, D_MODEL), const)],
            out_specs=[pl.BlockSpec((tm, D_MODEL), row) for _ in out_dtypes],
            scratch_shapes=[pltpu.VMEM((tm, D_MODEL), F32), pltpu.VMEM((tm, D_MODEL), F32),
                            pltpu.SemaphoreType.DMA(())],
        ),
        out_shape=[jax.ShapeDtypeStruct((n, D_MODEL), dt) for dt in out_dtypes],
        compiler_params=_params(("arbitrary",), 40),
    )(pos, ys, res, gates_col, g, beta)


def _moe_block(x, e_idx, gates, w_gate, w_up, w_down, layer, g, beta, out_dtypes):
    n = x.shape[0]
    n_tiles = -(-(2 * n + N_EXPERTS * (MOE_TILE - 1)) // MOE_TILE)
    pos, tile_expert, n_used = _plan(e_idx)
    xs = _scatter(x, pos, tile_expert, n_used, n_tiles * MOE_TILE)
    ys = _moe_ffn(xs, tile_expert, n_used, w_gate, w_up, w_down, layer)
    return _combine_ln(ys, pos, x, gates.T, g, beta, out_dtypes)


def _log_sigmoid(x):
    return -(jnp.maximum(-x, 0.0) + jnp.log1p(jnp.exp(-jnp.abs(x))))


def _lru_gate_block(xc, n, wrg_bf, wig_bf, brg_ref, big_ref, lam_ref):
    cols = slice(n * LRU_BLOCK, (n + 1) * LRU_BLOCK)
    xb = xc.astype(BF16)
    r = jax.nn.sigmoid(jnp.dot(xb, wrg_bf[n], preferred_element_type=F32) + brg_ref[:, cols])
    i = jax.nn.sigmoid(jnp.dot(xb, wig_bf[n], preferred_element_type=F32) + big_ref[:, cols])
    log_a = LRU_C * r * _log_sigmoid(lam_ref[:, cols])
    a = jnp.exp(log_a)
    u = xc * i * jnp.sqrt(-jnp.tanh(log_a) * (a * a + 1.0))
    return a, u


def _cast_gate_weights(wrg_ref, wig_ref, wrg_bf, wig_bf):
    for n in range(LRU_BLOCKS):
        wrg_bf[n] = wrg_ref[n].astype(BF16)
        wig_bf[n] = wig_ref[n].astype(BF16)


def _lru_prompt_kernel(xb_ref, yb_ref, cw_ref, cb_ref, wrg_ref, wig_ref, brg_ref, big_ref, lam_ref,
                       m_ref, conv_ref, hlast_ref, xpad, a_s, u_s, h_s, wrg_bf, wig_bf):
    b = pl.program_id(0)
    j = pl.program_id(1)
    tt = xb_ref.shape[0]

    @pl.when(jnp.logical_and(b == 0, j == 0))
    def _():
        _cast_gate_weights(wrg_ref, wig_ref, wrg_bf, wig_bf)

    @pl.when(j == 0)
    def _():
        xpad[0:8, :] = jnp.zeros((8, D_MODEL), F32)
        h_s[...] = jnp.zeros_like(h_s)

    xpad[8:8 + tt, :] = xb_ref[...]
    for n in range(LRU_BLOCKS):
        cols = slice(n * LRU_BLOCK, (n + 1) * LRU_BLOCK)
        xc = cb_ref[:, cols] + cw_ref[0:1, cols] * xpad[5:5 + tt, cols]
        for k in range(1, CONV_WIDTH):
            xc = xc + cw_ref[k:k + 1, cols] * xpad[5 + k:5 + k + tt, cols]
        a, u = _lru_gate_block(xc, n, wrg_bf, wig_bf, brg_ref, big_ref, lam_ref)
        a_s[:, cols] = a
        u_s[:, cols] = u

    def scan_body(gidx, h):
        base = pl.multiple_of(gidx * 8, 8)
        a8 = a_s[pl.ds(base, 8), :]
        u8 = u_s[pl.ds(base, 8), :]
        rows = []
        for s in range(8):
            h = a8[s:s + 1, :] * h + u8[s:s + 1, :]
            rows.append(h)
        a_s[pl.ds(base, 8), :] = jnp.concatenate(rows, axis=0)
        return h

    h = lax.fori_loop(0, tt // 8, scan_body, h_s[...])
    h_s[...] = h
    m_ref[...] = (a_s[...] * yb_ref[...].astype(F32)).astype(BF16)
    xpad[0:8, :] = xpad[tt:tt + 8, :]

    @pl.when(j == pl.num_programs(1) - 1)
    def _():
        conv_ref[...] = xpad[5:8, :]
        hlast_ref[...] = h


def _lru_prompt(xb, yb, batch, seq, cw, cb, wrg, wig, brg, big, lam, *, tt=256):
    nj = seq // tt
    row = lambda b, j: (b * nj + j, 0)
    const2 = lambda b, j: (0, 0)
    const3 = lambda b, j: (0, 0, 0)
    return pl.pallas_call(
        _lru_prompt_kernel,
        grid=(batch, nj),
        in_specs=[
            pl.BlockSpec((tt, D_MODEL), row),
            pl.BlockSpec((tt, D_MODEL), row),
            pl.BlockSpec((CONV_WIDTH, D_MODEL), const2),
            pl.BlockSpec((1, D_MODEL), const2),
            pl.BlockSpec((LRU_BLOCKS, LRU_BLOCK, LRU_BLOCK), const3),
            pl.BlockSpec((LRU_BLOCKS, LRU_BLOCK, LRU_BLOCK), const3),
            pl.BlockSpec((1, D_MODEL), const2),
            pl.BlockSpec((1, D_MODEL), const2),
            pl.BlockSpec((1, D_MODEL), const2),
        ],
        out_specs=[
            pl.BlockSpec((tt, D_MODEL), row),
            pl.BlockSpec((None, CONV_WIDTH - 1, D_MODEL), lambda b, j: (b, 0, 0)),
            pl.BlockSpec((None, 1, D_MODEL), lambda b, j: (b, 0, 0)),
        ],
        out_shape=[
            jax.ShapeDtypeStruct((batch * seq, D_MODEL), BF16),
            jax.ShapeDtypeStruct((batch, CONV_WIDTH - 1, D_MODEL), F32),
            jax.ShapeDtypeStruct((batch, 1, D_MODEL), F32),
        ],
        scratch_shapes=[
            pltpu.VMEM((tt + 8, D_MODEL), F32),
            pltpu.VMEM((tt, D_MODEL), F32),
            pltpu.VMEM((tt, D_MODEL), F32),
            pltpu.VMEM((1, D_MODEL), F32),
            pltpu.VMEM((LRU_BLOCKS, LRU_BLOCK, LRU_BLOCK), BF16),
            pltpu.VMEM((LRU_BLOCKS, LRU_BLOCK, LRU_BLOCK), BF16),
        ],
        compiler_params=_params(("arbitrary", "arbitrary"), 40),
    )(xb, yb, cw, cb, wrg, wig, brg, big, lam)


def _lru_sample_kernel(xb_ref, yb_ref, cs_ref, h0_ref, cw_ref, cb_ref, wrg_ref, wig_ref,
                       brg_ref, big_ref, lam_ref, m_ref, conv_ref, hlast_ref, wrg_bf, wig_bf):
    steps = xb_ref.shape[0]
    _cast_gate_weights(wrg_ref, wig_ref, wrg_bf, wig_bf)
    slabs = [cs_ref[k] for k in range(CONV_WIDTH - 1)] + [xb_ref[t] for t in range(steps)]
    for n in range(LRU_BLOCKS):
        cols = slice(n * LRU_BLOCK, (n + 1) * LRU_BLOCK)
        h = h0_ref[:, cols]
        for t in range(steps):
            xc = cb_ref[:, cols] + cw_ref[0:1, cols] * slabs[t][:, cols]
            for k in range(1, CONV_WIDTH):
                xc = xc + cw_ref[k:k + 1, cols] * slabs[t + k][:, cols]
            a, u = _lru_gate_block(xc, n, wrg_bf, wig_bf, brg_ref, big_ref, lam_ref)
            h = a * h + u
            m_ref[t, :, cols] = (h * yb_ref[t, :, cols].astype(F32)).astype(BF16)
        hlast_ref[:, cols] = h
    for k in range(CONV_WIDTH - 1):
        conv_ref[k] = slabs[steps + k]


def _lru_sample(xb, yb, conv_state, h0, cw, cb, wrg, wig, brg, big, lam):
    steps, batch, _ = xb.shape
    return pl.pallas_call(
        _lru_sample_kernel,
        out_shape=[
            jax.ShapeDtypeStruct((steps, batch, D_MODEL), BF16),
            jax.ShapeDtypeStruct((CONV_WIDTH - 1, batch, D_MODEL), F32),
            jax.ShapeDtypeStruct((batch, D_MODEL), F32),
        ],
        scratch_shapes=[
            pltpu.VMEM((LRU_BLOCKS, LRU_BLOCK, LRU_BLOCK), BF16),
            pltpu.VMEM((LRU_BLOCKS, LRU_BLOCK, LRU_BLOCK), BF16),
        ],
        compiler_params=_params((), 32),
    )(xb, yb, conv_state, h0, cw, cb, wrg, wig, brg, big, lam)


def _rel_bucket(dist):
    n = jnp.maximum(dist, 0)
    max_exact = N_BUCKETS // 2
    nf = jnp.maximum(n, 1).astype(F32)
    large = max_exact + (jnp.log(nf / max_exact) / math.log(MAX_DISTANCE / max_exact)
                         * (N_BUCKETS - max_exact)).astype(jnp.int32)
    large = jnp.minimum(large, N_BUCKETS - 1)
    return jnp.where(n < max_exact, n, large)


def _masked_buckets(dist):
    valid = (dist >= 0) & (dist < WINDOW)
    return jnp.where(valid, _rel_bucket(dist), -1).astype(jnp.int32)


def _build_bias(bucket, tab_ref, head):
    def body(bi, acc):
        return jnp.where(bucket == bi, tab_ref[bi * N_HEADS + head], acc)
    return lax.fori_loop(0, N_BUCKETS, body, jnp.full(bucket.shape, NEG_INF, F32))


def _softmax_pv(s, sink, v):
    m = jnp.maximum(jnp.max(s, axis=-1, keepdims=True), sink)
    p = jnp.exp(s - m)
    den = jnp.sum(p, axis=-1, keepdims=True) + jnp.exp(sink - m)
    return jnp.dot(p.astype(BF16), v, preferred_element_type=F32) / den


def _attn_prompt_kernel(q_ref, kvp_ref, kvc_ref, bucket_ref, tab_ref, sink_ref, o_ref, bias_s):
    b = pl.program_id(0)
    n = pl.program_id(1)

    @pl.when(jnp.logical_and(b == 0, n == 0))
    def _():
        bucket = bucket_ref[...]

        def head_body(h, c):
            bias_s[h] = _build_bias(bucket, tab_ref, h)
            return c

        lax.fori_loop(0, N_HEADS, head_body, 0)

    kv = jnp.concatenate([kvp_ref[...], kvc_ref[...]], axis=0).astype(BF16)
    col = lax.broadcasted_iota(jnp.int32, (WINDOW, 2 * WINDOW), 1)
    no_prev = jnp.logical_and(n == 0, col < WINDOW)
    for g in range(N_KV_HEADS):
        kg = kv[:, g * HEAD_DIM:(g + 1) * HEAD_DIM]
        vg = kv[:, KV_DIM + g * HEAD_DIM:KV_DIM + (g + 1) * HEAD_DIM]
        for hh in range(GROUP):
            h = g * GROUP + hh
            qh = q_ref[:, h * HEAD_DIM:(h + 1) * HEAD_DIM]
            s = lax.dot_general(qh, kg, (((1,), (1,)), ((), ())), preferred_element_type=F32)
            s = jnp.where(no_prev, NEG_INF, s + bias_s[h])
            o = _softmax_pv(s, sink_ref[h], vg)
            o_ref[:, h * HEAD_DIM:(h + 1) * HEAD_DIM] = o.astype(BF16)


def _attn_prompt(q, kv, batch, seq, bucket, tab, sinks):
    nb = seq // WINDOW
    smem = pl.BlockSpec(memory_space=pltpu.SMEM)
    return pl.pallas_call(
        _attn_prompt_kernel,
        grid=(batch, nb),
        in_specs=[
            pl.BlockSpec((WINDOW, D_MODEL), lambda b, n: (b * nb + n, 0)),
            pl.BlockSpec((WINDOW, 2 * KV_DIM), lambda b, n: (jnp.maximum(b * nb + n - 1, 0), 0)),
            pl.BlockSpec((WINDOW, 2 * KV_DIM), lambda b, n: (b * nb + n, 0)),
            pl.BlockSpec((WINDOW, 2 * WINDOW), lambda b, n: (0, 0)),
            smem, smem,
        ],
        out_specs=pl.BlockSpec((WINDOW, D_MODEL), lambda b, n: (b * nb + n, 0)),
        out_shape=jax.ShapeDtypeStruct((batch * seq, D_MODEL), BF16),
        scratch_shapes=[pltpu.VMEM((N_HEADS, WINDOW, 2 * WINDOW), F32)],
        compiler_params=_params(("arbitrary", "arbitrary"), 32),
    )(q, kv, kv, bucket, tab, sinks)


def _attn_sample_kernel(q_ref, k_ref, v_ref, bucket_ref, tab_ref, sink_ref, o_ref, bias_s):
    steps = q_ref.shape[0]

    @pl.when(pl.program_id(0) == 0)
    def _():
        bucket = bucket_ref[...]

        def head_body(h, c):
            bias_s[h] = _build_bias(bucket, tab_ref, h)
            return c

        lax.fori_loop(0, N_HEADS, head_body, 0)

    rows = lax.broadcasted_iota(jnp.int32, (GROUP * steps, 1), 0)
    for g in range(N_KV_HEADS):
        kg = k_ref[:, g * HEAD_DIM:(g + 1) * HEAD_DIM].astype(BF16)
        vg = v_ref[:, g * HEAD_DIM:(g + 1) * HEAD_DIM].astype(BF16)
        heads = range(g * GROUP, (g + 1) * GROUP)
        qg = jnp.concatenate([q_ref[:, h * HEAD_DIM:(h + 1) * HEAD_DIM] for h in heads], axis=0)
        bias = jnp.concatenate([bias_s[h] for h in heads], axis=0)
        sink = jnp.full((GROUP * steps, 1), sink_ref[g * GROUP], F32)
        for hh in range(1, GROUP):
            sink = jnp.where(rows >= hh * steps, sink_ref[g * GROUP + hh], sink)
        s = lax.dot_general(qg, kg, (((1,), (1,)), ((), ())), preferred_element_type=F32) + bias
        o = _softmax_pv(s, sink, vg)
        for hh, h in enumerate(heads):
            o_ref[:, h * HEAD_DIM:(h + 1) * HEAD_DIM] = o[hh * steps:(hh + 1) * steps].astype(BF16)


def _attn_sample(q, k_all, v_all, bucket, tab, sinks):
    batch, steps, _ = q.shape
    lk = k_all.shape[1]
    smem = pl.BlockSpec(memory_space=pltpu.SMEM)
    return pl.pallas_call(
        _attn_sample_kernel,
        grid=(batch,),
        in_specs=[
            pl.BlockSpec((None, steps, D_MODEL), lambda b: (b, 0, 0)),
            pl.BlockSpec((None, lk, KV_DIM), lambda b: (b, 0, 0)),
            pl.BlockSpec((None, lk, KV_DIM), lambda b: (b, 0, 0)),
            pl.BlockSpec((steps, lk), lambda b: (0, 0)),
            smem, smem,
        ],
        out_specs=pl.BlockSpec((None, steps, D_MODEL), lambda b: (b, 0, 0)),
        out_shape=jax.ShapeDtypeStruct((batch, steps, D_MODEL), BF16),
        scratch_shapes=[pltpu.VMEM((N_HEADS, steps, lk), F32)],
        compiler_params=_params(("arbitrary",), 32),
    )(q, k_all, v_all, bucket, tab, sinks)


def kernel(x_prompt, x_sample, state_conv, state_rnn, cache_k_win, cache_v_win, ln_g, ln_b, lru_w_x, lru_b_x, lru_w_y, lru_b_y, lru_conv_w, lru_conv_b, lru_w_rg, lru_b_rg, lru_w_ig, lru_b_ig, lru_lam, lru_w_out, lru_b_out, attn_w_kv, attn_w_q, attn_w_o, attn_sinks, rel_bias, moe_w_router, moe_b_router, moe_w_gate, moe_w_up, moe_w_down):
    bp, seq, _ = x_prompt.shape
    bs, steps, _ = x_sample.shape
    n_p = bp * seq
    n_s = bs * steps

    x0 = jnp.concatenate([x_prompt.reshape(n_p, D_MODEL),
                          x_sample.transpose(1, 0, 2).reshape(n_s, D_MODEL)], axis=0)
    wr_t = moe_w_router.T
    br = moe_b_router.reshape(N_EXPERTS, 1)
    vec = lambda a: a.reshape(1, -1)

    xb = _linear(x0, lru_w_x, (0,), vec(lru_b_x[0]), F32)
    yb = _linear(x0, lru_w_y, (0,), vec(lru_b_y[0]), BF16, act="gelu")
    lru_args = (lru_conv_w[0], vec(lru_conv_b[0]), lru_w_rg[0], lru_w_ig[0],
                vec(lru_b_rg[0]), vec(lru_b_ig[0]), vec(lru_lam[0]))
    m_p, conv_p, rnn_p = _lru_prompt(xb[:n_p], yb[:n_p], bp, seq, *lru_args)
    m_s, conv_s, rnn_s = _lru_sample(xb[n_p:].reshape(steps, bs, D_MODEL),
                                     yb[n_p:].reshape(steps, bs, D_MODEL),
                                     state_conv[0].transpose(1, 0, 2), state_rnn[0], *lru_args)
    m = jnp.concatenate([m_p, m_s.reshape(n_s, D_MODEL)], axis=0)
    x1, e_idx, gates = _proj_ln(m, lru_w_out, (0,), vec(lru_b_out[0]), x0,
                                vec(ln_g[0, 0]), vec(ln_b[0, 0]), wr_t, br)
    x2, x2_bf = _moe_block(x1, e_idx, gates, moe_w_gate, moe_w_up, moe_w_down, 0,
                           vec(ln_g[0, 1]), vec(ln_b[0, 1]), (F32, BF16))

    kv = _linear(x2_bf, attn_w_kv, (), jnp.zeros((1, 2 * KV_DIM), F32), F32)
    q = _linear(x2_bf, attn_w_q, (0,), jnp.zeros((1, D_MODEL), F32), BF16, scale=HEAD_DIM ** -0.5)
    tab = rel_bias.reshape(-1)
    sinks = attn_sinks[0]
    qi = jnp.arange(WINDOW)[:, None]
    kj = jnp.arange(2 * WINDOW)[None, :]
    o_p = _attn_prompt(q[:n_p], kv[:n_p], bp, seq, _masked_buckets(qi + WINDOW - kj), tab, sinks)
    kv_s = kv[n_p:].reshape(steps, bs, 2, KV_DIM).transpose(2, 1, 0, 3)
    k_all = jnp.concatenate([cache_k_win.reshape(bs, WINDOW, KV_DIM), kv_s[0]], axis=1)
    v_all = jnp.concatenate([cache_v_win.reshape(bs, WINDOW, KV_DIM), kv_s[1]], axis=1)
    dist_s = jnp.arange(steps)[:, None] + WINDOW - jnp.arange(WINDOW + steps)[None, :]
    q_s = q[n_p:].reshape(steps, bs, D_MODEL).transpose(1, 0, 2)
    o_s = _attn_sample(q_s, k_all, v_all, _masked_buckets(dist_s), tab, sinks)
    o = jnp.concatenate([o_p, o_s.transpose(1, 0, 2).reshape(n_s, D_MODEL)], axis=0)
    x3, e_idx, gates = _proj_ln(o, attn_w_o, (0,), jnp.zeros((1, D_MODEL), F32), x2,
                                vec(ln_g[1, 0]), vec(ln_b[1, 0]), wr_t, br)
    (x4,) = _moe_block(x3, e_idx, gates, moe_w_gate, moe_w_up, moe_w_down, 1,
                       vec(ln_g[1, 1]), vec(ln_b[1, 1]), (F32,))

    y_prompt = x4[:n_p].reshape(bp, seq, D_MODEL)
    y_sample = x4[n_p:].reshape(steps, bs, D_MODEL).transpose(1, 0, 2)
    kv_p = kv[:n_p].reshape(bp, seq, 2, N_KV_HEADS, HEAD_DIM)[:, seq - WINDOW:]
    k_win_s = k_all[:, steps:].reshape(bs, WINDOW, N_KV_HEADS, HEAD_DIM)
    v_win_s = v_all[:, steps:].reshape(bs, WINDOW, N_KV_HEADS, HEAD_DIM)
    return (y_prompt, y_sample,
            conv_p[None], rnn_p.reshape(1, bp, D_MODEL),
            kv_p[:, :, 0], kv_p[:, :, 1],
            conv_s.transpose(1, 0, 2)[None], rnn_s[None],
            k_win_s, v_win_s)
```

```python
import functools
import math

import jax
import jax.numpy as jnp
from jax import lax
from jax.experimental import pallas as pl
from jax.experimental.pallas import tpu as pltpu

D_MODEL = 2048
DEPTH = 2
LRU_BLOCKS = 8
LRU_BLOCK = D_MODEL // LRU_BLOCKS
CONV_WIDTH = 4
LRU_C = 8.0
N_HEADS = 32
HEAD_DIM = 64
N_KV_HEADS = 8
GROUP = N_HEADS // N_KV_HEADS
KV_DIM = N_KV_HEADS * HEAD_DIM
WINDOW = 128
N_BUCKETS = 32
MAX_DISTANCE = 128
N_EXPERTS = 16
N_GROUPS = 4
EXPERTS_PER_GROUP = N_EXPERTS // N_GROUPS
D_EXPERT = 1024
ALPHA = (2 * DEPTH) ** 0.25
LN_EPS = 1e-5

LANES = 128
MOE_TILE = 256
TOK_TILE = 256
DMA_UNROLL = 8
CAST_ROWS = 256
BF16 = jnp.bfloat16
F32 = jnp.float32
NEG_INF = float("-inf")


def _params(sem, vmem_mb):
    return pltpu.CompilerParams(dimension_semantics=sem, vmem_limit_bytes=vmem_mb * 1024 * 1024)


def _cast_rows(src_ref, dst_ref):
    n = src_ref.shape[0] // CAST_ROWS

    def body(i, c):
        r = pl.multiple_of(i * CAST_ROWS, CAST_ROWS)
        dst_ref[pl.ds(r, CAST_ROWS), :] = src_ref[pl.ds(r, CAST_ROWS), :].astype(BF16)
        return c

    lax.fori_loop(0, n, body, 0)


def _layer_norm(z, g, b):
    mu = jnp.mean(z, axis=-1, keepdims=True)
    zc = z - mu
    var = jnp.mean(zc * zc, axis=-1, keepdims=True)
    return zc * lax.rsqrt(var + LN_EPS) * g + b


def _tok_operands(x):
    if isinstance(x, tuple):
        xp, xs = x
        d = xp.shape[1]
        last_p = xp.shape[0] // TOK_TILE - 1
        specs = [pl.BlockSpec((TOK_TILE, d), lambda i, *_: (jnp.minimum(i, last_p), 0)),
                 pl.BlockSpec((TOK_TILE, d), lambda i, *_: (0, 0))]
        return [xp, xs], specs, last_p + 2
    return [x], [pl.BlockSpec((TOK_TILE, x.shape[1]), lambda i, *_: (i, 0))], x.shape[0] // TOK_TILE


def _tok_load(refs):
    if len(refs) == 1:
        return refs[0][...]
    return jnp.where(pl.program_id(0) < pl.num_programs(0) - 1, refs[0][...], refs[1][...])


def _linear_kernel(*refs, n_x, act, scale):
    x_refs, (w_ref, b_ref, o_ref, wbf_ref) = refs[:n_x], refs[n_x:]

    @pl.when(pl.program_id(0) == 0)
    def _():
        _cast_rows(w_ref, wbf_ref)

    y = jnp.dot(_tok_load(x_refs).astype(BF16), wbf_ref[...], preferred_element_type=F32)
    y = y + b_ref[...]
    if act == "gelu":
        y = jax.nn.gelu(y)
    if scale != 1.0:
        y = y * scale
    o_ref[...] = y.astype(o_ref.dtype)


def _linear(x, w, w_index, b, out_dtype, *, name, act=None, scale=1.0):
    arrays, specs, nt = _tok_operands(x)
    k, nout = w.shape[-2:]
    w_block = (None,) * len(w_index) + (k, nout)
    return pl.pallas_call(
        functools.partial(_linear_kernel, n_x=len(arrays), act=act, scale=scale),
        grid=(nt,),
        in_specs=specs + [
            pl.BlockSpec(w_block, lambda i: w_index + (0, 0), pipeline_mode=pl.Buffered(1)),
            pl.BlockSpec((1, nout), lambda i: (0, 0)),
        ],
        out_specs=pl.BlockSpec((TOK_TILE, nout), lambda i: (i, 0)),
        out_shape=jax.ShapeDtypeStruct((nt * TOK_TILE, nout), out_dtype),
        scratch_shapes=[pltpu.VMEM((k, nout), BF16)],
        compiler_params=_params(("arbitrary",), 48),
        name=name,
    )(*arrays, w, b)


def _route(logits_t, b_router):
    aff = jax.nn.sigmoid(logits_t)
    sel = aff + b_router
    srow = [sel[e:e + 1, :] for e in range(N_EXPERTS)]
    arow = [aff[e:e + 1, :] for e in range(N_EXPERTS)]

    def top2_sum(v):
        pairs = [v[i] + v[j] for i in range(4) for j in range(i + 1, 4)]
        return functools.reduce(jnp.maximum, pairs)

    scores = [top2_sum(srow[4 * g:4 * g + 4]) for g in range(N_GROUPS)]
    best = scores[0]
    gi = jnp.zeros_like(best, dtype=jnp.int32)
    for g in range(1, N_GROUPS):
        upd = scores[g] > best
        best = jnp.where(upd, scores[g], best)
        gi = jnp.where(upd, g, gi)

    def pick_group(rows, j):
        out = rows[j]
        for g in range(1, N_GROUPS):
            out = jnp.where(gi == g, rows[4 * g + j], out)
        return out

    v = [pick_group(srow, j) for j in range(EXPERTS_PER_GROUP)]
    a = [pick_group(arow, j) for j in range(EXPERTS_PER_GROUP)]

    m1, i1 = v[0], jnp.zeros_like(gi)
    for j in range(1, EXPERTS_PER_GROUP):
        upd = v[j] > m1
        m1 = jnp.where(upd, v[j], m1)
        i1 = jnp.where(upd, j, i1)
    m2 = jnp.full_like(m1, NEG_INF)
    i2 = jnp.zeros_like(gi)
    for j in range(EXPERTS_PER_GROUP):
        cand = jnp.where(i1 == j, NEG_INF, v[j])
        upd = cand > m2
        m2 = jnp.where(upd, cand, m2)
        i2 = jnp.where(upd, j, i2)

    def pick_idx(rows, idx):
        out = rows[0]
        for j in range(1, EXPERTS_PER_GROUP):
            out = jnp.where(idx == j, rows[j], out)
        return out

    a1 = pick_idx(a, i1)
    a2 = pick_idx(a, i2)
    tot = a1 + a2
    e_idx = jnp.concatenate([gi * EXPERTS_PER_GROUP + i1, gi * EXPERTS_PER_GROUP + i2], axis=0)
    gates = jnp.concatenate([a1 / tot, a2 / tot], axis=0)
    return e_idx, gates


def _proj_ln_kernel(*refs, n_m, n_res):
    m_refs = refs[:n_m]
    w_ref, b_ref = refs[n_m:n_m + 2]
    res_refs = refs[n_m + 2:n_m + 2 + n_res]
    g_ref, beta_ref, wr_ref, br_ref, x_ref, e_ref, gate_ref, wbf_ref = refs[n_m + 2 + n_res:]

    @pl.when(pl.program_id(0) == 0)
    def _():
        _cast_rows(w_ref, wbf_ref)

    y = jnp.dot(_tok_load(m_refs), wbf_ref[...], preferred_element_type=F32) + b_ref[...]
    x = _layer_norm(ALPHA * _tok_load(res_refs) + y, g_ref[...], beta_ref[...])
    x_ref[...] = x
    logits_t = lax.dot_general(wr_ref[...].astype(BF16), x.astype(BF16),
                               (((1,), (1,)), ((), ())), preferred_element_type=F32)
    e_idx, gates = _route(logits_t, br_ref[...])
    e_ref[...] = e_idx
    gate_ref[...] = gates


def _proj_ln(m, w, w_index, b, res, g, beta, wr_t, br, *, name):
    m_arrays, m_specs, nt = _tok_operands(m)
    res_arrays, res_specs, _ = _tok_operands(res)
    k = w.shape[-2]
    tm = TOK_TILE
    n = nt * tm
    row = lambda i: (i, 0)
    const = lambda i: (0, 0)
    x, e_idx, gates = pl.pallas_call(
        functools.partial(_proj_ln_kernel, n_m=len(m_arrays), n_res=len(res_arrays)),
        grid=(nt,),
        in_specs=m_specs + [
            pl.BlockSpec((None,) * len(w_index) + (k, D_MODEL), lambda i: w_index + (0, 0),
                         pipeline_mode=pl.Buffered(1)),
            pl.BlockSpec((1, D_MODEL), const),
        ] + res_specs + [
            pl.BlockSpec((1, D_MODEL), const),
            pl.BlockSpec((1, D_MODEL), const),
            pl.BlockSpec((N_EXPERTS, D_MODEL), const),
            pl.BlockSpec((N_EXPERTS, 1), const),
        ],
        out_specs=[
            pl.BlockSpec((tm, D_MODEL), row),
            pl.BlockSpec((None, 2, tm), lambda i: (i, 0, 0)),
            pl.BlockSpec((None, 2, tm), lambda i: (i, 0, 0)),
        ],
        out_shape=[
            jax.ShapeDtypeStruct((n, D_MODEL), F32),
            jax.ShapeDtypeStruct((nt, 2, tm), jnp.int32),
            jax.ShapeDtypeStruct((nt, 2, tm), F32),
        ],
        scratch_shapes=[pltpu.VMEM((k, D_MODEL), BF16)],
        compiler_params=_params(("arbitrary",), 48),
        name=name,
    )(*m_arrays, w, b, *res_arrays, g, beta, wr_t, br)
    e_idx = e_idx.transpose(1, 0, 2).reshape(2, n)
    gates = gates.transpose(1, 0, 2).reshape(2, n)
    return x, e_idx, gates


def _plan_kernel(e_ref, pos_ref, meta_ref, rank_ref):
    nrow = e_ref.shape[0]
    ri = lax.broadcasted_iota(jnp.int32, (LANES, LANES), 0)
    ci = lax.broadcasted_iota(jnp.int32, (LANES, LANES), 1)
    tri = jnp.where(ri <= ci, 1.0, 0.0).astype(BF16)
    sub = lax.broadcasted_iota(jnp.int32, (N_EXPERTS, LANES), 0)

    def count_body(r, base):
        onehot = sub == e_ref[pl.ds(r, 1), :]
        loc = jnp.dot(jnp.where(onehot, 1.0, 0.0).astype(BF16), tri, preferred_element_type=F32)
        rank_ref[pl.ds(r, 1), :] = jnp.sum(jnp.where(onehot, base + loc - 1.0, 0.0),
                                           axis=0, keepdims=True)
        return base + jnp.broadcast_to(loc[:, LANES - 1:LANES], (N_EXPERTS, LANES))

    count = lax.fori_loop(0, nrow, count_body, jnp.zeros((N_EXPERTS, LANES), F32))
    ntile = jnp.floor((count + (MOE_TILE - 1.0)) * (1.0 / MOE_TILE))
    offs = []
    acc = jnp.zeros((1, LANES), F32)
    for e in range(N_EXPERTS):
        offs.append(acc)
        acc = acc + ntile[e:e + 1, :]
    tile_off = jnp.concatenate(offs, axis=0)
    tile_end = tile_off + ntile
    lane = lax.broadcasted_iota(jnp.int32, (N_EXPERTS, LANES), 1).astype(F32)
    tile_expert = jnp.sum(jnp.where(tile_end <= lane, 1.0, 0.0), axis=0, keepdims=True)
    tile_expert = jnp.minimum(tile_expert, N_EXPERTS - 1.0)
    meta = jnp.concatenate([tile_expert, acc, jnp.zeros((6, LANES), F32)], axis=0)
    meta_ref[...] = meta.astype(jnp.int32)
    row_off = tile_off * float(MOE_TILE)

    def pos_body(r, c):
        onehot = sub == e_ref[pl.ds(r, 1), :]
        p = jnp.sum(jnp.where(onehot, row_off, 0.0), axis=0, keepdims=True) + rank_ref[pl.ds(r, 1), :]
        pos_ref[pl.ds(r, 1), :] = p.astype(jnp.int32)
        return c

    lax.fori_loop(0, nrow, pos_body, 0)


def _plan(e_idx, *, name):
    n2 = e_idx.shape[0] * e_idx.shape[1]
    e2d = e_idx.reshape(n2 // LANES, LANES)
    pos, meta = pl.pallas_call(
        _plan_kernel,
        out_shape=[jax.ShapeDtypeStruct(e2d.shape, jnp.int32),
                   jax.ShapeDtypeStruct((8, LANES), jnp.int32)],
        scratch_shapes=[pltpu.VMEM(e2d.shape, F32)],
        name=name,
    )(e2d)
    return pos.reshape(n2), meta[0], meta[1, :1]


def _scatter_kernel(pos_ref, te_ref, nu_ref, x_ref, xs_hbm, zero_ref, sem, zsem, *, n_tok, n_tiles):
    i = pl.program_id(0)
    tm = x_ref.shape[0]

    def tile_copy(t):
        return pltpu.make_async_copy(zero_ref, xs_hbm.at[pl.ds(t * MOE_TILE, MOE_TILE), :], zsem)

    @pl.when(i == 0)
    def _():
        zero_ref[...] = jnp.zeros_like(zero_ref)
        last_used = nu_ref[0] - 1

        def is_last(t):
            return jnp.logical_or(t >= last_used, te_ref[t + 1] != te_ref[t])

        def start_body(t, c):
            @pl.when(is_last(t))
            def _():
                tile_copy(t).start()
            return c

        def wait_body(t, c):
            @pl.when(is_last(t))
            def _():
                tile_copy(t).wait()
            return c

        lax.fori_loop(0, n_tiles, start_body, 0)
        lax.fori_loop(0, n_tiles, wait_body, 0)

    def row_body(rb, c):
        for u in range(DMA_UNROLL):
            r = rb * DMA_UNROLL + u
            for k in range(2):
                p = pos_ref[k * n_tok + i * tm + r]
                pltpu.make_async_copy(x_ref.at[pl.ds(r, 1), :], xs_hbm.at[pl.ds(p, 1), :], sem).start()
        return c

    lax.fori_loop(0, tm // DMA_UNROLL, row_body, 0)
    for k in range(2):
        pltpu.make_async_copy(x_ref, xs_hbm.at[pl.ds(0, tm), :], sem).wait()


def _scatter(x, pos, tile_expert, n_used, n_rows, *, name):
    n = x.shape[0]
    tm = TOK_TILE
    return pl.pallas_call(
        functools.partial(_scatter_kernel, n_tok=n, n_tiles=n_rows // MOE_TILE),
        grid_spec=pltpu.PrefetchScalarGridSpec(
            num_scalar_prefetch=3,
            grid=(n // tm,),
            in_specs=[pl.BlockSpec((tm, D_MODEL), lambda i, *_: (i, 0))],
            out_specs=pl.BlockSpec(memory_space=pl.ANY),
            scratch_shapes=[pltpu.VMEM((MOE_TILE, D_MODEL), x.dtype),
                            pltpu.SemaphoreType.DMA(()), pltpu.SemaphoreType.DMA(())],
        ),
        out_shape=jax.ShapeDtypeStruct((n_rows, D_MODEL), x.dtype),
        compiler_params=_params(("arbitrary",), 32),
        name=name,
    )(pos, tile_expert, n_used, x)


def _expert_changed(te_ref, i):
    return jnp.logical_or(i == 0, te_ref[i] != te_ref[jnp.maximum(i - 1, 0)])


def _moe_up_kernel(te_ref, nu_ref, xs_ref, wg_ref, wu_ref, h_ref, wg_bf, wu_bf):
    i = pl.program_id(0)

    @pl.when(i < nu_ref[0])
    def _():
        @pl.when(_expert_changed(te_ref, i))
        def _():
            _cast_rows(wg_ref, wg_bf)
            _cast_rows(wu_ref, wu_bf)

        x = xs_ref[...].astype(BF16)
        a = jnp.dot(x, wg_bf[...], preferred_element_type=F32)
        b = jnp.dot(x, wu_bf[...], preferred_element_type=F32)
        h_ref[...] = (jax.nn.silu(a) * b).astype(BF16)

    @pl.when(i >= nu_ref[0])
    def _():
        h_ref[...] = jnp.zeros_like(h_ref)


def _moe_down_kernel(te_ref, nu_ref, h_ref, wd_ref, y_ref, wd_bf):
    i = pl.program_id(0)

    @pl.when(i < nu_ref[0])
    def _():
        @pl.when(_expert_changed(te_ref, i))
        def _():
            _cast_rows(wd_ref, wd_bf)

        y_ref[...] = jnp.dot(h_ref[...], wd_bf[...], preferred_element_type=F32)

    @pl.when(i >= nu_ref[0])
    def _():
        y_ref[...] = jnp.zeros_like(y_ref)


def _moe_ffn(xs, tile_expert, n_used, w_gate, w_up, w_down, layer):
    n_rows = xs.shape[0]
    n_tiles = n_rows // MOE_TILE

    def tile(i, te, nu):
        return (jnp.minimum(i, nu[0] - 1), 0)

    def expert(i, te, nu):
        return (layer, te[jnp.minimum(i, nu[0] - 1)], 0, 0)

    def weight_spec(k, n):
        return pl.BlockSpec((None, None, k, n), expert)

    h = pl.pallas_call(
        _moe_up_kernel,
        grid_spec=pltpu.PrefetchScalarGridSpec(
            num_scalar_prefetch=2,
            grid=(n_tiles,),
            in_specs=[pl.BlockSpec((MOE_TILE, D_MODEL), tile),
                      weight_spec(D_MODEL, D_EXPERT),
                      weight_spec(D_MODEL, D_EXPERT)],
            out_specs=pl.BlockSpec((MOE_TILE, D_EXPERT), lambda i, te, nu: (i, 0)),
            scratch_shapes=[pltpu.VMEM((D_MODEL, D_EXPERT), BF16),
                            pltpu.VMEM((D_MODEL, D_EXPERT), BF16)],
        ),
        out_shape=jax.ShapeDtypeStruct((n_rows, D_EXPERT), BF16),
        compiler_params=_params(("arbitrary",), 56),
        name=f"moe_up_{layer}",
    )(tile_expert, n_used, xs, w_gate, w_up)
    return pl.pallas_call(
        _moe_down_kernel,
        grid_spec=pltpu.PrefetchScalarGridSpec(
            num_scalar_prefetch=2,
            grid=(n_tiles,),
            in_specs=[pl.BlockSpec((MOE_TILE, D_EXPERT), tile),
                      weight_spec(D_EXPERT, D_MODEL)],
            out_specs=pl.BlockSpec((MOE_TILE, D_MODEL), lambda i, te, nu: (i, 0)),
            scratch_shapes=[pltpu.VMEM((D_EXPERT, D_MODEL), BF16)],
        ),
        out_shape=jax.ShapeDtypeStruct((n_rows, D_MODEL), F32),
        compiler_params=_params(("arbitrary",), 40),
        name=f"moe_down_{layer}",
    )(tile_expert, n_used, h, w_down)


def _combine_ln_kernel(pos_ref, ys_hbm, res_ref, gate_ref, g_ref, beta_ref, *rest, n_tok, split):
    outs, (buf0, buf1, sem) = rest[:-3], rest[-3:]
    i = pl.program_id(0)
    tm = res_ref.shape[0]
    bufs = (buf0, buf1)

    def row_body(rb, c):
        for u in range(DMA_UNROLL):
            r = rb * DMA_UNROLL + u
            for k in range(2):
                p = pos_ref[k * n_tok + i * tm + r]
                pltpu.make_async_copy(ys_hbm.at[pl.ds(p, 1), :], bufs[k].at[pl.ds(r, 1), :], sem).start()
        return c

    lax.fori_loop(0, tm // DMA_UNROLL, row_body, 0)
    for k in range(2):
        pltpu.make_async_copy(ys_hbm.at[pl.ds(0, tm), :], bufs[k], sem).wait()

    gate = gate_ref[...]
    ffn = gate[:, 0:1] * buf0[...] + gate[:, 1:2] * buf1[...]
    x = _layer_norm(ALPHA * res_ref[...] + ffn, g_ref[...], beta_ref[...])
    if split:
        prompt_ref, sample_ref = outs
        is_sample = i == pl.num_programs(0) - 1

        @pl.when(jnp.logical_not(is_sample))
        def _():
            prompt_ref[...] = x

        @pl.when(is_sample)
        def _():
            sample_ref[...] = x
    else:
        for o_ref in outs:
            o_ref[...] = x.astype(o_ref.dtype)


def _combine_ln(ys, pos, res, gates_col, g, beta, out_dtypes, *, split, name):
    n = res.shape[0]
    tm = TOK_TILE
    nt = n // tm
    row = lambda i, *_: (i, 0)
    const = lambda i, *_: (0, 0)
    if split:
        out_specs = [pl.BlockSpec((tm, D_MODEL), lambda i, *_: (jnp.minimum(i, nt - 2), 0)),
                     pl.BlockSpec((tm, D_MODEL), const)]
        out_shape = [jax.ShapeDtypeStruct((n - tm, D_MODEL), F32),
                     jax.ShapeDtypeStruct((tm, D_MODEL), F32)]
    else:
        out_specs = [pl.BlockSpec((tm, D_MODEL), row) for _ in out_dtypes]
        out_shape = [jax.ShapeDtypeStruct((n, D_MODEL), dt) for dt in out_dtypes]
    return pl.pallas_call(
        functools.partial(_combine_ln_kernel, n_tok=n, split=split),
        grid_spec=pltpu.PrefetchScalarGridSpec(
            num_scalar_prefetch=1,
            grid=(nt,),
            in_specs=[pl.BlockSpec(memory_space=pl.ANY),
                      pl.BlockSpec((tm, D_MODEL), row),
                      pl.BlockSpec((tm, 2), row),
                      pl.BlockSpec((1, D_MODEL), const),
                      pl.BlockSpec((1, D_MODEL), const)],
            out_specs=out_specs,
            scratch_shapes=[pltpu.VMEM((tm, D_MODEL), F32), pltpu.VMEM((tm, D_MODEL), F32),
                            pltpu.SemaphoreType.DMA(())],
        ),
        out_shape=out_shape,
        compiler_params=_params(("arbitrary",), 40),
        name=name,
    )(pos, ys, res, gates_col, g, beta)


def _moe_block(x, e_idx, gates, w_gate, w_up, w_down, layer, g, beta, out_dtypes, split=False):
    n = x.shape[0]
    n_tiles = -(-(2 * n + N_EXPERTS * (MOE_TILE - 1)) // MOE_TILE)
    pos, tile_expert, n_used = _plan(e_idx, name=f"moe_plan_{layer}")
    xs = _scatter(x, pos, tile_expert, n_used, n_tiles * MOE_TILE, name=f"moe_scatter_{layer}")
    ys = _moe_ffn(xs, tile_expert, n_used, w_gate, w_up, w_down, layer)
    return _combine_ln(ys, pos, x, gates.T, g, beta, out_dtypes, split=split,
                       name=f"moe_combine_{layer}")


def _log_sigmoid(x):
    return -(jnp.maximum(-x, 0.0) + jnp.log1p(jnp.exp(-jnp.abs(x))))


def _lru_gate_block(xc, n, wrg_bf, wig_bf, brg_ref, big_ref, lam_ref):
    cols = slice(n * LRU_BLOCK, (n + 1) * LRU_BLOCK)
    xb = xc.astype(BF16)
    r = jax.nn.sigmoid(jnp.dot(xb, wrg_bf[n], preferred_element_type=F32) + brg_ref[:, cols])
    i = jax.nn.sigmoid(jnp.dot(xb, wig_bf[n], preferred_element_type=F32) + big_ref[:, cols])
    log_a = LRU_C * r * _log_sigmoid(lam_ref[:, cols])
    a = jnp.exp(log_a)
    u = xc * i * jnp.sqrt(-jnp.tanh(log_a) * (a * a + 1.0))
    return a, u


def _cast_gate_weights(wrg_ref, wig_ref, wrg_bf, wig_bf):
    for n in range(LRU_BLOCKS):
        wrg_bf[n] = wrg_ref[n].astype(BF16)
        wig_bf[n] = wig_ref[n].astype(BF16)


def _lru_prompt_kernel(xb_ref, yb_ref, cw_ref, cb_ref, wrg_ref, wig_ref, brg_ref, big_ref, lam_ref,
                       m_ref, conv_ref, hlast_ref, xpad, a_s, u_s, h_s, wrg_bf, wig_bf):
    b = pl.program_id(0)
    j = pl.program_id(1)
    tt = xb_ref.shape[0]

    @pl.when(jnp.logical_and(b == 0, j == 0))
    def _():
        _cast_gate_weights(wrg_ref, wig_ref, wrg_bf, wig_bf)

    @pl.when(j == 0)
    def _():
        xpad[0:8, :] = jnp.zeros((8, D_MODEL), F32)
        h_s[...] = jnp.zeros_like(h_s)

    xpad[8:8 + tt, :] = xb_ref[...]
    for n in range(LRU_BLOCKS):
        cols = slice(n * LRU_BLOCK, (n + 1) * LRU_BLOCK)
        xc = cb_ref[:, cols] + cw_ref[0:1, cols] * xpad[5:5 + tt, cols]
        for k in range(1, CONV_WIDTH):
            xc = xc + cw_ref[k:k + 1, cols] * xpad[5 + k:5 + k + tt, cols]
        a, u = _lru_gate_block(xc, n, wrg_bf, wig_bf, brg_ref, big_ref, lam_ref)
        a_s[:, cols] = a
        u_s[:, cols] = u

    def scan_body(gidx, h):
        base = pl.multiple_of(gidx * 8, 8)
        a8 = a_s[pl.ds(base, 8), :]
        u8 = u_s[pl.ds(base, 8), :]
        rows = []
        for s in range(8):
            h = a8[s:s + 1, :] * h + u8[s:s + 1, :]
            rows.append(h)
        a_s[pl.ds(base, 8), :] = jnp.concatenate(rows, axis=0)
        return h

    h = lax.fori_loop(0, tt // 8, scan_body, h_s[...])
    h_s[...] = h
    m_ref[...] = (a_s[...] * yb_ref[...].astype(F32)).astype(BF16)
    xpad[0:8, :] = xpad[tt:tt + 8, :]

    @pl.when(j == pl.num_programs(1) - 1)
    def _():
        conv_ref[...] = xpad[5:8, :]
        hlast_ref[...] = h


def _lru_prompt(xb, yb, batch, seq, cw, cb, wrg, wig, brg, big, lam, *, tt=256):
    nj = seq // tt
    row = lambda b, j: (b * nj + j, 0)
    const2 = lambda b, j: (0, 0)
    const3 = lambda b, j: (0, 0, 0)
    return pl.pallas_call(
        _lru_prompt_kernel,
        grid=(batch, nj),
        in_specs=[
            pl.BlockSpec((tt, D_MODEL), row),
            pl.BlockSpec((tt, D_MODEL), row),
            pl.BlockSpec((CONV_WIDTH, D_MODEL), const2),
            pl.BlockSpec((1, D_MODEL), const2),
            pl.BlockSpec((LRU_BLOCKS, LRU_BLOCK, LRU_BLOCK), const3),
            pl.BlockSpec((LRU_BLOCKS, LRU_BLOCK, LRU_BLOCK), const3),
            pl.BlockSpec((1, D_MODEL), const2),
            pl.BlockSpec((1, D_MODEL), const2),
            pl.BlockSpec((1, D_MODEL), const2),
        ],
        out_specs=[
            pl.BlockSpec((tt, D_MODEL), row),
            pl.BlockSpec((None, CONV_WIDTH - 1, D_MODEL), lambda b, j: (b, 0, 0)),
            pl.BlockSpec((None, 1, D_MODEL), lambda b, j: (b, 0, 0)),
        ],
        out_shape=[
            jax.ShapeDtypeStruct((batch * seq, D_MODEL), BF16),
            jax.ShapeDtypeStruct((batch, CONV_WIDTH - 1, D_MODEL), F32),
            jax.ShapeDtypeStruct((batch, 1, D_MODEL), F32),
        ],
        scratch_shapes=[
            pltpu.VMEM((tt + 8, D_MODEL), F32),
            pltpu.VMEM((tt, D_MODEL), F32),
            pltpu.VMEM((tt, D_MODEL), F32),
            pltpu.VMEM((1, D_MODEL), F32),
            pltpu.VMEM((LRU_BLOCKS, LRU_BLOCK, LRU_BLOCK), BF16),
            pltpu.VMEM((LRU_BLOCKS, LRU_BLOCK, LRU_BLOCK), BF16),
        ],
        compiler_params=_params(("arbitrary", "arbitrary"), 40),
        name="lru_prompt",
    )(xb, yb, cw, cb, wrg, wig, brg, big, lam)


def _lru_sample_kernel(xb_ref, yb_ref, cs_ref, h0_ref, cw_ref, cb_ref, wrg_ref, wig_ref,
                       brg_ref, big_ref, lam_ref, m_ref, conv_ref, hlast_ref, wrg_bf, wig_bf, *, steps):
    batch = h0_ref.shape[0]
    _cast_gate_weights(wrg_ref, wig_ref, wrg_bf, wig_bf)
    m_ref[steps * batch:, :] = jnp.zeros((m_ref.shape[0] - steps * batch, D_MODEL), BF16)

    def slab(t, cols):
        if t < CONV_WIDTH - 1:
            return cs_ref[t, :, cols]
        t -= CONV_WIDTH - 1
        return xb_ref[t * batch:(t + 1) * batch, cols]

    for n in range(LRU_BLOCKS):
        cols = slice(n * LRU_BLOCK, (n + 1) * LRU_BLOCK)
        h = h0_ref[:, cols]
        for t in range(steps):
            xc = cb_ref[:, cols] + cw_ref[0:1, cols] * slab(t, cols)
            for k in range(1, CONV_WIDTH):
                xc = xc + cw_ref[k:k + 1, cols] * slab(t + k, cols)
            a, u = _lru_gate_block(xc, n, wrg_bf, wig_bf, brg_ref, big_ref, lam_ref)
            h = a * h + u
            rows = slice(t * batch, (t + 1) * batch)
            m_ref[rows, cols] = (h * yb_ref[rows, cols].astype(F32)).astype(BF16)
        hlast_ref[:, cols] = h
    for k in range(CONV_WIDTH - 1):
        conv_ref[k] = slab(steps + k, slice(None))


def _lru_sample(xb, yb, tile, steps, conv_state, h0, cw, cb, wrg, wig, brg, big, lam):
    batch = h0.shape[0]
    tok = pl.BlockSpec((TOK_TILE, D_MODEL), lambda i: (tile, 0))
    full = lambda a: pl.BlockSpec(a.shape, lambda i: (0,) * a.ndim)
    small = (conv_state, h0, cw, cb, wrg, wig, brg, big, lam)
    return pl.pallas_call(
        functools.partial(_lru_sample_kernel, steps=steps),
        grid=(1,),
        in_specs=[tok, tok] + [full(a) for a in small],
        out_specs=[
            pl.BlockSpec((TOK_TILE, D_MODEL), lambda i: (0, 0)),
            pl.BlockSpec((CONV_WIDTH - 1, batch, D_MODEL), lambda i: (0, 0, 0)),
            pl.BlockSpec((batch, D_MODEL), lambda i: (0, 0)),
        ],
        out_shape=[
            jax.ShapeDtypeStruct((TOK_TILE, D_MODEL), BF16),
            jax.ShapeDtypeStruct((CONV_WIDTH - 1, batch, D_MODEL), F32),
            jax.ShapeDtypeStruct((batch, D_MODEL), F32),
        ],
        scratch_shapes=[
            pltpu.VMEM((LRU_BLOCKS, LRU_BLOCK, LRU_BLOCK), BF16),
            pltpu.VMEM((LRU_BLOCKS, LRU_BLOCK, LRU_BLOCK), BF16),
        ],
        compiler_params=_params(("arbitrary",), 32),
        name="lru_sample",
    )(xb, yb, *small)


def _rel_bucket(dist):
    n = jnp.maximum(dist, 0)
    max_exact = N_BUCKETS // 2
    nf = jnp.maximum(n, 1).astype(F32)
    large = max_exact + (jnp.log(nf / max_exact) / math.log(MAX_DISTANCE / max_exact)
                         * (N_BUCKETS - max_exact)).astype(jnp.int32)
    large = jnp.minimum(large, N_BUCKETS - 1)
    return jnp.where(n < max_exact, n, large)


def _masked_buckets(dist):
    valid = (dist >= 0) & (dist < WINDOW)
    return jnp.where(valid, _rel_bucket(dist), -1).astype(jnp.int32)


def _build_bias(bucket, tab_ref, head):
    def body(bi, acc):
        return jnp.where(bucket == bi, tab_ref[bi * N_HEADS + head], acc)
    return lax.fori_loop(0, N_BUCKETS, body, jnp.full(bucket.shape, NEG_INF, F32))


def _softmax_pv(s, sink, v):
    m = jnp.maximum(jnp.max(s, axis=-1, keepdims=True), sink)
    p = jnp.exp(s - m)
    den = jnp.sum(p, axis=-1, keepdims=True) + jnp.exp(sink - m)
    return jnp.dot(p.astype(BF16), v, preferred_element_type=F32) / den


def _attn_prompt_kernel(q_ref, kvp_ref, kvc_ref, bucket_ref, tab_ref, sink_ref, o_ref, bias_s):
    b = pl.program_id(0)
    n = pl.program_id(1)

    @pl.when(jnp.logical_and(b == 0, n == 0))
    def _():
        bucket = bucket_ref[...]

        col = lax.broadcasted_iota(jnp.int32, (WINDOW, 2 * WINDOW), 1)

        def head_body(h, c):
            bias = _build_bias(bucket, tab_ref, h)
            sink = sink_ref[h]
            g = h // GROUP
            r0 = pl.multiple_of((h % GROUP) * WINDOW, WINDOW)
            bias_s[0, g, pl.ds(r0, WINDOW), :] = jnp.where(col == 0, sink, bias)
            bias_s[1, g, pl.ds(r0, WINDOW), :] = jnp.where(
                col == 0, sink, jnp.where(col < WINDOW, NEG_INF, bias))
            return c

        lax.fori_loop(0, N_HEADS, head_body, 0)

    first = (n == 0).astype(jnp.int32)
    row = lax.broadcasted_iota(jnp.int32, kvp_ref.shape, 0)
    kv_prev = jnp.where(row == 0, 0.0, kvp_ref[...])
    kv = jnp.concatenate([kv_prev, kvc_ref[...]], axis=0).astype(BF16)
    ones = jnp.ones((2 * WINDOW, HEAD_DIM), BF16)
    lane = lax.broadcasted_iota(jnp.int32, (WINDOW, 2 * HEAD_DIM), 1)
    for g in range(N_KV_HEADS):
        heads = range(g * GROUP, (g + 1) * GROUP)
        kg = kv[:, g * HEAD_DIM:(g + 1) * HEAD_DIM]
        vg = kv[:, KV_DIM + g * HEAD_DIM:KV_DIM + (g + 1) * HEAD_DIM]
        v_ext = jnp.concatenate([vg, ones], axis=1)
        qg = jnp.concatenate([q_ref[:, h * HEAD_DIM:(h + 1) * HEAD_DIM] for h in heads], axis=0)
        s = lax.dot_general(qg, kg, (((1,), (1,)), ((), ())), preferred_element_type=F32)
        s = s + bias_s[first, g]
        p = jnp.exp(s - jnp.max(s, axis=-1, keepdims=True)).astype(BF16)
        o_ext = jnp.dot(p, v_ext, preferred_element_type=F32)
        o_rot = pltpu.roll(o_ext, HEAD_DIM, axis=1)
        for pair in range(GROUP // 2):
            r0 = slice(2 * pair * WINDOW, (2 * pair + 1) * WINDOW)
            r1 = slice((2 * pair + 1) * WINDOW, (2 * pair + 2) * WINDOW)
            even = o_ext[r0] * (1.0 / o_rot[r0])
            odd = o_rot[r1] * (1.0 / o_ext[r1])
            c0 = (g * GROUP + 2 * pair) * HEAD_DIM
            o_ref[:, c0:c0 + 2 * HEAD_DIM] = jnp.where(lane < HEAD_DIM, even, odd).astype(BF16)


def _attn_prompt(q, kv, batch, seq, bucket, tab, sinks):
    nb = seq // WINDOW
    smem = pl.BlockSpec(memory_space=pltpu.SMEM)
    return pl.pallas_call(
        _attn_prompt_kernel,
        grid=(batch, nb),
        in_specs=[
            pl.BlockSpec((WINDOW, D_MODEL), lambda b, n: (b * nb + n, 0)),
            pl.BlockSpec((WINDOW, 2 * KV_DIM), lambda b, n: (jnp.maximum(b * nb + n - 1, 0), 0)),
            pl.BlockSpec((WINDOW, 2 * KV_DIM), lambda b, n: (b * nb + n, 0)),
            pl.BlockSpec((WINDOW, 2 * WINDOW), lambda b, n: (0, 0)),
            smem, smem,
        ],
        out_specs=pl.BlockSpec((WINDOW, D_MODEL), lambda b, n: (b * nb + n, 0)),
        out_shape=jax.ShapeDtypeStruct((batch * seq, D_MODEL), BF16),
        scratch_shapes=[pltpu.VMEM((2, N_KV_HEADS, GROUP * WINDOW, 2 * WINDOW), F32)],
        compiler_params=_params(("arbitrary", "arbitrary"), 32),
        name="attn_prompt",
    )(q, kv, kv, bucket, tab, sinks)


def _attn_sample_kernel(q_ref, k_ref, v_ref, bucket_ref, tab_ref, sink_ref, o_ref, bias_s):
    steps = q_ref.shape[0]

    @pl.when(pl.program_id(0) == 0)
    def _():
        bucket = bucket_ref[...]

        def head_body(h, c):
            bias_s[h] = _build_bias(bucket, tab_ref, h)
            return c

        lax.fori_loop(0, N_HEADS, head_body, 0)

    rows = lax.broadcasted_iota(jnp.int32, (GROUP * steps, 1), 0)
    for g in range(N_KV_HEADS):
        kg = k_ref[:, g * HEAD_DIM:(g + 1) * HEAD_DIM].astype(BF16)
        vg = v_ref[:, g * HEAD_DIM:(g + 1) * HEAD_DIM].astype(BF16)
        heads = range(g * GROUP, (g + 1) * GROUP)
        qg = jnp.concatenate([q_ref[:, h * HEAD_DIM:(h + 1) * HEAD_DIM] for h in heads], axis=0)
        bias = jnp.concatenate([bias_s[h] for h in heads], axis=0)
        sink = jnp.full((GROUP * steps, 1), sink_ref[g * GROUP], F32)
        for hh in range(1, GROUP):
            sink = jnp.where(rows >= hh * steps, sink_ref[g * GROUP + hh], sink)
        s = lax.dot_general(qg, kg, (((1,), (1,)), ((), ())), preferred_element_type=F32) + bias
        o = _softmax_pv(s, sink, vg)
        for hh, h in enumerate(heads):
            o_ref[:, h * HEAD_DIM:(h + 1) * HEAD_DIM] = o[hh * steps:(hh + 1) * steps].astype(BF16)


def _attn_sample(q, k_all, v_all, bucket, tab, sinks):
    batch, steps, _ = q.shape
    lk = k_all.shape[1]
    smem = pl.BlockSpec(memory_space=pltpu.SMEM)
    return pl.pallas_call(
        _attn_sample_kernel,
        grid=(batch,),
        in_specs=[
            pl.BlockSpec((None, steps, D_MODEL), lambda b: (b, 0, 0)),
            pl.BlockSpec((None, lk, KV_DIM), lambda b: (b, 0, 0)),
            pl.BlockSpec((None, lk, KV_DIM), lambda b: (b, 0, 0)),
            pl.BlockSpec((steps, lk), lambda b: (0, 0)),
            smem, smem,
        ],
        out_specs=pl.BlockSpec((None, steps, D_MODEL), lambda b: (b, 0, 0)),
        out_shape=jax.ShapeDtypeStruct((batch, steps, D_MODEL), BF16),
        scratch_shapes=[pltpu.VMEM((N_HEADS, steps, lk), F32)],
        compiler_params=_params(("arbitrary",), 32),
        name="attn_sample",
    )(q, k_all, v_all, bucket, tab, sinks)


def kernel(x_prompt, x_sample, state_conv, state_rnn, cache_k_win, cache_v_win, ln_g, ln_b, lru_w_x, lru_b_x, lru_w_y, lru_b_y, lru_conv_w, lru_conv_b, lru_w_rg, lru_b_rg, lru_w_ig, lru_b_ig, lru_lam, lru_w_out, lru_b_out, attn_w_kv, attn_w_q, attn_w_o, attn_sinks, rel_bias, moe_w_router, moe_b_router, moe_w_gate, moe_w_up, moe_w_down):
    bp, seq, _ = x_prompt.shape
    bs, steps, _ = x_sample.shape
    n_p = bp * seq
    n_s = bs * steps

    assert n_p % TOK_TILE == 0 and n_s <= TOK_TILE
    sample_tile = n_p // TOK_TILE

    def pad_tile(rows):
        return jnp.pad(rows, ((0, TOK_TILE - n_s), (0, 0)))

    x0 = (x_prompt.reshape(n_p, D_MODEL),
          pad_tile(x_sample.transpose(1, 0, 2).reshape(n_s, D_MODEL)))
    wr_t = moe_w_router.T
    br = moe_b_router.reshape(N_EXPERTS, 1)
    vec = lambda a: a.reshape(1, -1)

    xb = _linear(x0, lru_w_x, (0,), vec(lru_b_x[0]), F32, name="lru_in_x")
    yb = _linear(x0, lru_w_y, (0,), vec(lru_b_y[0]), BF16, act="gelu", name="lru_in_y")
    lru_args = (lru_conv_w[0], vec(lru_conv_b[0]), lru_w_rg[0], lru_w_ig[0],
                vec(lru_b_rg[0]), vec(lru_b_ig[0]), vec(lru_lam[0]))
    m_p, conv_p, rnn_p = _lru_prompt(xb, yb, bp, seq, *lru_args)
    m_s, conv_s, rnn_s = _lru_sample(xb, yb, sample_tile, steps,
                                     state_conv[0].transpose(1, 0, 2), state_rnn[0], *lru_args)
    x1, e_idx, gates = _proj_ln((m_p, m_s), lru_w_out, (0,), vec(lru_b_out[0]), x0,
                                vec(ln_g[0, 0]), vec(ln_b[0, 0]), wr_t, br, name="lru_out_ln")
    x2, x2_bf = _moe_block(x1, e_idx, gates, moe_w_gate, moe_w_up, moe_w_down, 0,
                           vec(ln_g[0, 1]), vec(ln_b[0, 1]), (F32, BF16))

    kv = _linear(x2_bf, attn_w_kv, (), jnp.zeros((1, 2 * KV_DIM), F32), F32, name="attn_kv")
    q = _linear(x2_bf, attn_w_q, (0,), jnp.zeros((1, D_MODEL), F32), BF16, scale=HEAD_DIM ** -0.5,
                name="attn_q")
    tab = rel_bias.reshape(-1)
    sinks = attn_sinks[0]
    qi = jnp.arange(WINDOW)[:, None]
    kj = jnp.arange(2 * WINDOW)[None, :]
    o_p = _attn_prompt(q, kv, bp, seq, _masked_buckets(qi + WINDOW - kj), tab, sinks)
    kv_s = kv[n_p:n_p + n_s].reshape(steps, bs, 2, KV_DIM).transpose(2, 1, 0, 3)
    k_all = jnp.concatenate([cache_k_win.reshape(bs, WINDOW, KV_DIM), kv_s[0]], axis=1)
    v_all = jnp.concatenate([cache_v_win.reshape(bs, WINDOW, KV_DIM), kv_s[1]], axis=1)
    dist_s = jnp.arange(steps)[:, None] + WINDOW - jnp.arange(WINDOW + steps)[None, :]
    q_s = q[n_p:n_p + n_s].reshape(steps, bs, D_MODEL).transpose(1, 0, 2)
    o_s = _attn_sample(q_s, k_all, v_all, _masked_buckets(dist_s), tab, sinks)
    o_s = pad_tile(o_s.transpose(1, 0, 2).reshape(n_s, D_MODEL))
    x3, e_idx, gates = _proj_ln((o_p, o_s), attn_w_o, (0,), jnp.zeros((1, D_MODEL), F32), x2,
                                vec(ln_g[1, 0]), vec(ln_b[1, 0]), wr_t, br, name="attn_out_ln")
    y_p, y_s = _moe_block(x3, e_idx, gates, moe_w_gate, moe_w_up, moe_w_down, 1,
                          vec(ln_g[1, 1]), vec(ln_b[1, 1]), (F32,), split=True)

    y_prompt = y_p.reshape(bp, seq, D_MODEL)
    y_sample = y_s[:n_s].reshape(steps, bs, D_MODEL).transpose(1, 0, 2)
    kv_p = kv[:n_p].reshape(bp, seq, 2, N_KV_HEADS, HEAD_DIM)[:, seq - WINDOW:]
    k_win_s = k_all[:, steps:].reshape(bs, WINDOW, N_KV_HEADS, HEAD_DIM)
    v_win_s = v_all[:, steps:].reshape(bs, WINDOW, N_KV_HEADS, HEAD_DIM)
    return (y_prompt, y_sample,
            conv_p[None], rnn_p.reshape(1, bp, D_MODEL),
            kv_p[:, :, 0], kv_p[:, :, 1],
            conv_s.transpose(1, 0, 2)[None], rnn_s[None],
            k_win_s, v_win_s)
```

```python
import functools
import math

import jax
import jax.numpy as jnp
from jax import lax
from jax.experimental import pallas as pl
from jax.experimental.pallas import tpu as pltpu

D_MODEL = 2048
DEPTH = 2
LRU_BLOCKS = 8
LRU_BLOCK = D_MODEL // LRU_BLOCKS
CONV_WIDTH = 4
LRU_C = 8.0
N_HEADS = 32
HEAD_DIM = 64
N_KV_HEADS = 8
GROUP = N_HEADS // N_KV_HEADS
KV_DIM = N_KV_HEADS * HEAD_DIM
WINDOW = 128
N_BUCKETS = 32
MAX_DISTANCE = 128
N_EXPERTS = 16
N_GROUPS = 4
EXPERTS_PER_GROUP = N_EXPERTS // N_GROUPS
D_EXPERT = 1024
ALPHA = (2 * DEPTH) ** 0.25
LN_EPS = 1e-5

LANES = 128
MOE_TILE = 256
TOK_TILE = 256
DMA_UNROLL = 8
CAST_ROWS = 256
BF16 = jnp.bfloat16
F32 = jnp.float32
NEG_INF = float("-inf")


def _params(sem, vmem_mb):
    return pltpu.CompilerParams(dimension_semantics=sem, vmem_limit_bytes=vmem_mb * 1024 * 1024)


def _cast_rows(src_ref, dst_ref):
    n = src_ref.shape[0] // CAST_ROWS

    def body(i, c):
        r = pl.multiple_of(i * CAST_ROWS, CAST_ROWS)
        dst_ref[pl.ds(r, CAST_ROWS), :] = src_ref[pl.ds(r, CAST_ROWS), :].astype(BF16)
        return c

    lax.fori_loop(0, n, body, 0)


def _layer_norm(z, g, b):
    mu = jnp.mean(z, axis=-1, keepdims=True)
    zc = z - mu
    var = jnp.mean(zc * zc, axis=-1, keepdims=True)
    return zc * lax.rsqrt(var + LN_EPS) * g + b


def _tok_operands(x):
    if isinstance(x, tuple):
        xp, xs = x
        d = xp.shape[1]
        last_p = xp.shape[0] // TOK_TILE - 1
        specs = [pl.BlockSpec((TOK_TILE, d), lambda i, *_: (jnp.minimum(i, last_p), 0)),
                 pl.BlockSpec((TOK_TILE, d), lambda i, *_: (0, 0))]
        return [xp, xs], specs, last_p + 2
    return [x], [pl.BlockSpec((TOK_TILE, x.shape[1]), lambda i, *_: (i, 0))], x.shape[0] // TOK_TILE


def _tok_load(refs):
    if len(refs) == 1:
        return refs[0][...]
    return jnp.where(pl.program_id(0) < pl.num_programs(0) - 1, refs[0][...], refs[1][...])


def _linear_kernel(*refs, n_x, act, scale):
    x_refs, (w_ref, b_ref, o_ref, wbf_ref) = refs[:n_x], refs[n_x:]

    @pl.when(pl.program_id(0) == 0)
    def _():
        _cast_rows(w_ref, wbf_ref)

    y = jnp.dot(_tok_load(x_refs).astype(BF16), wbf_ref[...], preferred_element_type=F32)
    y = y + b_ref[...]
    if act == "gelu":
        y = jax.nn.gelu(y)
    if scale != 1.0:
        y = y * scale
    o_ref[...] = y.astype(o_ref.dtype)


def _linear(x, w, w_index, b, out_dtype, *, name, act=None, scale=1.0):
    arrays, specs, nt = _tok_operands(x)
    k, nout = w.shape[-2:]
    w_block = (None,) * len(w_index) + (k, nout)
    return pl.pallas_call(
        functools.partial(_linear_kernel, n_x=len(arrays), act=act, scale=scale),
        grid=(nt,),
        in_specs=specs + [
            pl.BlockSpec(w_block, lambda i: w_index + (0, 0), pipeline_mode=pl.Buffered(1)),
            pl.BlockSpec((1, nout), lambda i: (0, 0)),
        ],
        out_specs=pl.BlockSpec((TOK_TILE, nout), lambda i: (i, 0)),
        out_shape=jax.ShapeDtypeStruct((nt * TOK_TILE, nout), out_dtype),
        scratch_shapes=[pltpu.VMEM((k, nout), BF16)],
        compiler_params=_params(("arbitrary",), 48),
        name=name,
    )(*arrays, w, b)


def _route(logits_t, b_router):
    aff = jax.nn.sigmoid(logits_t)
    sel = aff + b_router
    srow = [sel[e:e + 1, :] for e in range(N_EXPERTS)]
    arow = [aff[e:e + 1, :] for e in range(N_EXPERTS)]

    def top2_sum(v):
        pairs = [v[i] + v[j] for i in range(4) for j in range(i + 1, 4)]
        return functools.reduce(jnp.maximum, pairs)

    scores = [top2_sum(srow[4 * g:4 * g + 4]) for g in range(N_GROUPS)]
    best = scores[0]
    gi = jnp.zeros_like(best, dtype=jnp.int32)
    for g in range(1, N_GROUPS):
        upd = scores[g] > best
        best = jnp.where(upd, scores[g], best)
        gi = jnp.where(upd, g, gi)

    def pick_group(rows, j):
        out = rows[j]
        for g in range(1, N_GROUPS):
            out = jnp.where(gi == g, rows[4 * g + j], out)
        return out

    v = [pick_group(srow, j) for j in range(EXPERTS_PER_GROUP)]
    a = [pick_group(arow, j) for j in range(EXPERTS_PER_GROUP)]

    m1, i1 = v[0], jnp.zeros_like(gi)
    for j in range(1, EXPERTS_PER_GROUP):
        upd = v[j] > m1
        m1 = jnp.where(upd, v[j], m1)
        i1 = jnp.where(upd, j, i1)
    m2 = jnp.full_like(m1, NEG_INF)
    i2 = jnp.zeros_like(gi)
    for j in range(EXPERTS_PER_GROUP):
        cand = jnp.where(i1 == j, NEG_INF, v[j])
        upd = cand > m2
        m2 = jnp.where(upd, cand, m2)
        i2 = jnp.where(upd, j, i2)

    def pick_idx(rows, idx):
        out = rows[0]
        for j in range(1, EXPERTS_PER_GROUP):
            out = jnp.where(idx == j, rows[j], out)
        return out

    a1 = pick_idx(a, i1)
    a2 = pick_idx(a, i2)
    tot = a1 + a2
    e_idx = jnp.concatenate([gi * EXPERTS_PER_GROUP + i1, gi * EXPERTS_PER_GROUP + i2], axis=0)
    gates = jnp.concatenate([a1 / tot, a2 / tot], axis=0)
    return e_idx, gates


def _proj_ln_kernel(*refs, n_m, n_res):
    m_refs = refs[:n_m]
    w_ref, b_ref = refs[n_m:n_m + 2]
    res_refs = refs[n_m + 2:n_m + 2 + n_res]
    g_ref, beta_ref, wr_ref, br_ref, x_ref, e_ref, gate_ref, wbf_ref = refs[n_m + 2 + n_res:]

    @pl.when(pl.program_id(0) == 0)
    def _():
        _cast_rows(w_ref, wbf_ref)

    y = jnp.dot(_tok_load(m_refs), wbf_ref[...], preferred_element_type=F32) + b_ref[...]
    x = _layer_norm(ALPHA * _tok_load(res_refs) + y, g_ref[...], beta_ref[...])
    x_ref[...] = x
    logits_t = lax.dot_general(wr_ref[...].astype(BF16), x.astype(BF16),
                               (((1,), (1,)), ((), ())), preferred_element_type=F32)
    e_idx, gates = _route(logits_t, br_ref[...])
    e_ref[...] = e_idx
    gate_ref[...] = gates


def _proj_ln(m, w, w_index, b, res, g, beta, wr_t, br, *, name):
    m_arrays, m_specs, nt = _tok_operands(m)
    res_arrays, res_specs, _ = _tok_operands(res)
    k = w.shape[-2]
    tm = TOK_TILE
    n = nt * tm
    row = lambda i: (i, 0)
    const = lambda i: (0, 0)
    x, e_idx, gates = pl.pallas_call(
        functools.partial(_proj_ln_kernel, n_m=len(m_arrays), n_res=len(res_arrays)),
        grid=(nt,),
        in_specs=m_specs + [
            pl.BlockSpec((None,) * len(w_index) + (k, D_MODEL), lambda i: w_index + (0, 0),
                         pipeline_mode=pl.Buffered(1)),
            pl.BlockSpec((1, D_MODEL), const),
        ] + res_specs + [
            pl.BlockSpec((1, D_MODEL), const),
            pl.BlockSpec((1, D_MODEL), const),
            pl.BlockSpec((N_EXPERTS, D_MODEL), const),
            pl.BlockSpec((N_EXPERTS, 1), const),
        ],
        out_specs=[
            pl.BlockSpec((tm, D_MODEL), row),
            pl.BlockSpec((None, 2, tm), lambda i: (i, 0, 0)),
            pl.BlockSpec((None, 2, tm), lambda i: (i, 0, 0)),
        ],
        out_shape=[
            jax.ShapeDtypeStruct((n, D_MODEL), F32),
            jax.ShapeDtypeStruct((nt, 2, tm), jnp.int32),
            jax.ShapeDtypeStruct((nt, 2, tm), F32),
        ],
        scratch_shapes=[pltpu.VMEM((k, D_MODEL), BF16)],
        compiler_params=_params(("arbitrary",), 48),
        name=name,
    )(*m_arrays, w, b, *res_arrays, g, beta, wr_t, br)
    e_idx = e_idx.transpose(1, 0, 2).reshape(2, n)
    gates = gates.transpose(1, 0, 2).reshape(2, n)
    return x, e_idx, gates


def _plan_kernel(e_ref, pos_ref, meta_ref, rank_ref):
    nrow = e_ref.shape[0]
    ri = lax.broadcasted_iota(jnp.int32, (LANES, LANES), 0)
    ci = lax.broadcasted_iota(jnp.int32, (LANES, LANES), 1)
    tri = jnp.where(ri <= ci, 1.0, 0.0).astype(BF16)
    sub = lax.broadcasted_iota(jnp.int32, (N_EXPERTS, LANES), 0)

    def count_body(r, base):
        onehot = sub == e_ref[pl.ds(r, 1), :]
        loc = jnp.dot(jnp.where(onehot, 1.0, 0.0).astype(BF16), tri, preferred_element_type=F32)
        rank_ref[pl.ds(r, 1), :] = jnp.sum(jnp.where(onehot, base + loc - 1.0, 0.0),
                                           axis=0, keepdims=True)
        return base + jnp.broadcast_to(loc[:, LANES - 1:LANES], (N_EXPERTS, LANES))

    count = lax.fori_loop(0, nrow, count_body, jnp.zeros((N_EXPERTS, LANES), F32))
    ntile = jnp.floor((count + (MOE_TILE - 1.0)) * (1.0 / MOE_TILE))
    offs = []
    acc = jnp.zeros((1, LANES), F32)
    for e in range(N_EXPERTS):
        offs.append(acc)
        acc = acc + ntile[e:e + 1, :]
    tile_off = jnp.concatenate(offs, axis=0)
    tile_end = tile_off + ntile
    lane = lax.broadcasted_iota(jnp.int32, (N_EXPERTS, LANES), 1).astype(F32)
    tile_expert = jnp.sum(jnp.where(tile_end <= lane, 1.0, 0.0), axis=0, keepdims=True)
    tile_expert = jnp.minimum(tile_expert, N_EXPERTS - 1.0)
    meta = jnp.concatenate([tile_expert, acc, jnp.zeros((6, LANES), F32)], axis=0)
    meta_ref[...] = meta.astype(jnp.int32)
    row_off = tile_off * float(MOE_TILE)

    def pos_body(r, c):
        onehot = sub == e_ref[pl.ds(r, 1), :]
        p = jnp.sum(jnp.where(onehot, row_off, 0.0), axis=0, keepdims=True) + rank_ref[pl.ds(r, 1), :]
        pos_ref[pl.ds(r, 1), :] = p.astype(jnp.int32)
        return c

    lax.fori_loop(0, nrow, pos_body, 0)


def _plan(e_idx, *, name):
    n2 = e_idx.shape[0] * e_idx.shape[1]
    e2d = e_idx.reshape(n2 // LANES, LANES)
    pos, meta = pl.pallas_call(
        _plan_kernel,
        out_shape=[jax.ShapeDtypeStruct(e2d.shape, jnp.int32),
                   jax.ShapeDtypeStruct((8, LANES), jnp.int32)],
        scratch_shapes=[pltpu.VMEM(e2d.shape, F32)],
        name=name,
    )(e2d)
    return pos.reshape(n2), meta[0], meta[1, :1]


def _invert_kernel(pos_ref, pair_ref):
    n_rows = pair_ref.shape[0]
    n_pairs = pos_ref.shape[0]

    def fill_body(b, c):
        for u in range(DMA_UNROLL):
            pair_ref[b * DMA_UNROLL + u] = -1
        return c

    def pair_body(b, c):
        for u in range(DMA_UNROLL):
            p = b * DMA_UNROLL + u
            pair_ref[pos_ref[p]] = p
        return c

    lax.fori_loop(0, n_rows // DMA_UNROLL, fill_body, 0)
    lax.fori_loop(0, n_pairs // DMA_UNROLL, pair_body, 0)


def _invert(pos, n_rows, *, name):
    return pl.pallas_call(
        _invert_kernel,
        grid_spec=pltpu.PrefetchScalarGridSpec(
            num_scalar_prefetch=1,
            grid=(1,),
            in_specs=[],
            out_specs=pl.BlockSpec(memory_space=pltpu.SMEM),
        ),
        out_shape=jax.ShapeDtypeStruct((n_rows,), jnp.int32),
        name=name,
    )(pos)


def _expert_changed(te_ref, i):
    return jnp.logical_or(i == 0, te_ref[i] != te_ref[jnp.maximum(i - 1, 0)])


def _moe_up_kernel(pair_ref, te_ref, nu_ref, x_hbm, wg_ref, wu_ref, h_ref,
                   wg_bf, wu_bf, xa, xb, sems, *, n_tok, n_tiles):
    i = pl.program_id(0)
    nu = nu_ref[0]
    bufs = (xa, xb)

    def row_copy(tile, r, buf, sem):
        p = pair_ref[tile * MOE_TILE + r]
        tok = jnp.where(p >= n_tok, p - n_tok, jnp.maximum(p, 0))
        return pltpu.make_async_copy(x_hbm.at[pl.ds(tok, 1), :], buf.at[pl.ds(r, 1), :], sem)

    def wait_tile(buf, sem):
        pltpu.make_async_copy(x_hbm.at[pl.ds(0, MOE_TILE), :], buf, sem).wait()

    @pl.when(i == 0)
    def _():
        def body(rb, c):
            for u in range(DMA_UNROLL):
                row_copy(0, rb * DMA_UNROLL + u, xa, sems.at[0]).start()
            return c
        lax.fori_loop(0, MOE_TILE // DMA_UNROLL, body, 0)

    for parity in range(2):
        cur, nxt = bufs[parity], bufs[1 - parity]
        cur_sem, nxt_sem = sems.at[parity], sems.at[1 - parity]
        mine = i % 2 == parity

        @pl.when(jnp.logical_and(mine, i <= nu))
        def _():
            wait_tile(cur, cur_sem)

        @pl.when(jnp.logical_and(mine, i < nu))
        def _():
            @pl.when(_expert_changed(te_ref, i))
            def _():
                _cast_rows(wg_ref, wg_bf)
                _cast_rows(wu_ref, wu_bf)

            next_tile = jnp.minimum(i + 1, n_tiles - 1)
            for r in range(MOE_TILE):
                row_copy(next_tile, r, nxt, nxt_sem).start()
            x = cur[...].astype(BF16)
            a = jnp.dot(x, wg_bf[...], preferred_element_type=F32)
            b = jnp.dot(x, wu_bf[...], preferred_element_type=F32)
            h_ref[...] = (jax.nn.silu(a) * b).astype(BF16)

        @pl.when(jnp.logical_and(mine, jnp.logical_and(i == n_tiles - 1, i < nu)))
        def _():
            wait_tile(nxt, nxt_sem)

    @pl.when(i >= nu)
    def _():
        h_ref[...] = jnp.zeros_like(h_ref)


def _moe_down_kernel(pair_ref, te_ref, nu_ref, h_ref, wd_ref, out_hbm,
                     wd_bf, ya, yb, sems, tsem, *, n_tok, n_tiles):
    i = pl.program_id(0)
    nu = nu_ref[0]
    bufs = (ya, yb)
    trash = 2 * n_tok

    def row_copy(tile, r, buf, sem):
        p = pair_ref[tile * MOE_TILE + r]
        dst = jnp.where(p < 0, trash + r, p)
        return pltpu.make_async_copy(buf.at[pl.ds(r, 1), :], out_hbm.at[pl.ds(dst, 1), :], sem)

    def wait_tile(buf, sem):
        pltpu.make_async_copy(buf, out_hbm.at[pl.ds(0, MOE_TILE), :], sem).wait()

    @pl.when(i == 0)
    def _():
        yb[...] = jnp.zeros_like(yb)
        fill = pltpu.make_async_copy(yb, out_hbm.at[pl.ds(trash, MOE_TILE), :], tsem)
        fill.start()
        fill.wait()

    for parity in range(2):
        cur, prev = bufs[parity], bufs[1 - parity]
        cur_sem, prev_sem = sems.at[parity], sems.at[1 - parity]
        mine = i % 2 == parity

        @pl.when(jnp.logical_and(mine, jnp.logical_and(i >= 1, i - 1 <= nu)))
        def _():
            wait_tile(cur, cur_sem)

        @pl.when(jnp.logical_and(mine, i < nu))
        def _():
            @pl.when(_expert_changed(te_ref, i))
            def _():
                _cast_rows(wd_ref, wd_bf)

            prev_tile = jnp.maximum(i - 1, 0)
            for r in range(MOE_TILE):
                row_copy(prev_tile, r, prev, prev_sem).start()
            cur[...] = jnp.dot(h_ref[...], wd_bf[...], preferred_element_type=F32)

        @pl.when(jnp.logical_and(mine, i == nu))
        def _():
            def body(rb, c):
                for u in range(DMA_UNROLL):
                    row_copy(i - 1, rb * DMA_UNROLL + u, prev, prev_sem).start()
                return c
            lax.fori_loop(0, MOE_TILE // DMA_UNROLL, body, 0)

            @pl.when(i == n_tiles)
            def _():
                wait_tile(prev, prev_sem)


def _moe_ffn(x, pair, tile_expert, n_used, w_gate, w_up, w_down, layer):
    n_tok = x.shape[0]
    n_tiles = pair.shape[0] // MOE_TILE

    def tile(i, pr, te, nu):
        return (jnp.minimum(i, nu[0] - 1), 0)

    def expert(i, pr, te, nu):
        return (layer, te[jnp.minimum(i, nu[0] - 1)], 0, 0)

    def weight_spec(k, n):
        return pl.BlockSpec((None, None, k, n), expert)

    row_bufs = [pltpu.VMEM((MOE_TILE, D_MODEL), F32), pltpu.VMEM((MOE_TILE, D_MODEL), F32),
                pltpu.SemaphoreType.DMA((2,))]
    h = pl.pallas_call(
        functools.partial(_moe_up_kernel, n_tok=n_tok, n_tiles=n_tiles),
        grid_spec=pltpu.PrefetchScalarGridSpec(
            num_scalar_prefetch=3,
            grid=(n_tiles,),
            in_specs=[pl.BlockSpec(memory_space=pl.ANY),
                      weight_spec(D_MODEL, D_EXPERT),
                      weight_spec(D_MODEL, D_EXPERT)],
            out_specs=pl.BlockSpec((MOE_TILE, D_EXPERT), lambda i, pr, te, nu: (i, 0)),
            scratch_shapes=[pltpu.VMEM((D_MODEL, D_EXPERT), BF16),
                            pltpu.VMEM((D_MODEL, D_EXPERT), BF16)] + row_bufs,
        ),
        out_shape=jax.ShapeDtypeStruct((n_tiles * MOE_TILE, D_EXPERT), BF16),
        compiler_params=_params(("arbitrary",), 56),
        name=f"moe_up_{layer}",
    )(pair, tile_expert, n_used, x, w_gate, w_up)
    return pl.pallas_call(
        functools.partial(_moe_down_kernel, n_tok=n_tok, n_tiles=n_tiles),
        grid_spec=pltpu.PrefetchScalarGridSpec(
            num_scalar_prefetch=3,
            grid=(n_tiles + 1,),
            in_specs=[pl.BlockSpec((MOE_TILE, D_EXPERT), tile),
                      weight_spec(D_EXPERT, D_MODEL)],
            out_specs=pl.BlockSpec(memory_space=pl.ANY),
            scratch_shapes=[pltpu.VMEM((D_EXPERT, D_MODEL), BF16)] + row_bufs
            + [pltpu.SemaphoreType.DMA(())],
        ),
        out_shape=jax.ShapeDtypeStruct((2 * n_tok + MOE_TILE, D_MODEL), F32),
        compiler_params=_params(("arbitrary",), 40),
        name=f"moe_down_{layer}",
    )(pair, tile_expert, n_used, h, w_down)


def _combine_ln_kernel(y0_ref, y1_ref, res_ref, gate_ref, g_ref, beta_ref, *outs, split):
    i = pl.program_id(0)
    gate = gate_ref[...]
    ffn = gate[:, 0:1] * y0_ref[...] + gate[:, 1:2] * y1_ref[...]
    x = _layer_norm(ALPHA * res_ref[...] + ffn, g_ref[...], beta_ref[...])
    if split:
        prompt_ref, sample_ref = outs
        is_sample = i == pl.num_programs(0) - 1

        @pl.when(jnp.logical_not(is_sample))
        def _():
            prompt_ref[...] = x

        @pl.when(is_sample)
        def _():
            sample_ref[...] = x
    else:
        for o_ref in outs:
            o_ref[...] = x.astype(o_ref.dtype)


def _combine_ln(ys, res, gates_col, g, beta, out_dtypes, *, split, name):
    n = res.shape[0]
    tm = TOK_TILE
    nt = n // tm
    row = lambda i: (i, 0)
    const = lambda i: (0, 0)
    if split:
        out_specs = [pl.BlockSpec((tm, D_MODEL), lambda i: (jnp.minimum(i, nt - 2), 0)),
                     pl.BlockSpec((tm, D_MODEL), const)]
        out_shape = [jax.ShapeDtypeStruct((n - tm, D_MODEL), F32),
                     jax.ShapeDtypeStruct((tm, D_MODEL), F32)]
    else:
        out_specs = [pl.BlockSpec((tm, D_MODEL), row) for _ in out_dtypes]
        out_shape = [jax.ShapeDtypeStruct((n, D_MODEL), dt) for dt in out_dtypes]
    return pl.pallas_call(
        functools.partial(_combine_ln_kernel, split=split),
        grid=(nt,),
        in_specs=[pl.BlockSpec((tm, D_MODEL), row),
                  pl.BlockSpec((tm, D_MODEL), lambda i: (i + nt, 0)),
                  pl.BlockSpec((tm, D_MODEL), row),
                  pl.BlockSpec((tm, 2), row),
                  pl.BlockSpec((1, D_MODEL), const),
                  pl.BlockSpec((1, D_MODEL), const)],
        out_specs=out_specs,
        out_shape=out_shape,
        compiler_params=_params(("arbitrary",), 40),
        name=name,
    )(ys, ys, res, gates_col, g, beta)


def _moe_block(x, e_idx, gates, w_gate, w_up, w_down, layer, g, beta, out_dtypes, split=False):
    n = x.shape[0]
    n_tiles = -(-(2 * n + N_EXPERTS * (MOE_TILE - 1)) // MOE_TILE)
    pos, tile_expert, n_used = _plan(e_idx, name=f"moe_plan_{layer}")
    pair = _invert(pos, n_tiles * MOE_TILE, name=f"moe_invert_{layer}")
    ys = _moe_ffn(x, pair, tile_expert, n_used, w_gate, w_up, w_down, layer)
    return _combine_ln(ys, x, gates.T, g, beta, out_dtypes, split=split,
                       name=f"moe_combine_{layer}")


def _log_sigmoid(x):
    return -(jnp.maximum(-x, 0.0) + jnp.log1p(jnp.exp(-jnp.abs(x))))


def _lru_gate_block(xc, n, wrg_bf, wig_bf, brg_ref, big_ref, lam_ref):
    cols = slice(n * LRU_BLOCK, (n + 1) * LRU_BLOCK)
    xb = xc.astype(BF16)
    r = jax.nn.sigmoid(jnp.dot(xb, wrg_bf[n], preferred_element_type=F32) + brg_ref[:, cols])
    i = jax.nn.sigmoid(jnp.dot(xb, wig_bf[n], preferred_element_type=F32) + big_ref[:, cols])
    log_a = LRU_C * r * _log_sigmoid(lam_ref[:, cols])
    a = jnp.exp(log_a)
    u = xc * i * jnp.sqrt(-jnp.tanh(log_a) * (a * a + 1.0))
    return a, u


def _cast_gate_weights(wrg_ref, wig_ref, wrg_bf, wig_bf):
    for n in range(LRU_BLOCKS):
        wrg_bf[n] = wrg_ref[n].astype(BF16)
        wig_bf[n] = wig_ref[n].astype(BF16)


def _lru_prompt_kernel(xb_ref, yb_ref, cw_ref, cb_ref, wrg_ref, wig_ref, brg_ref, big_ref, lam_ref,
                       m_ref, conv_ref, hlast_ref, xpad, a_s, u_s, h_s, wrg_bf, wig_bf):
    b = pl.program_id(0)
    j = pl.program_id(1)
    tt = xb_ref.shape[0]

    @pl.when(jnp.logical_and(b == 0, j == 0))
    def _():
        _cast_gate_weights(wrg_ref, wig_ref, wrg_bf, wig_bf)

    @pl.when(j == 0)
    def _():
        xpad[0:8, :] = jnp.zeros((8, D_MODEL), F32)
        h_s[...] = jnp.zeros_like(h_s)

    xpad[8:8 + tt, :] = xb_ref[...]
    for n in range(LRU_BLOCKS):
        cols = slice(n * LRU_BLOCK, (n + 1) * LRU_BLOCK)
        xc = cb_ref[:, cols] + cw_ref[0:1, cols] * xpad[5:5 + tt, cols]
        for k in range(1, CONV_WIDTH):
            xc = xc + cw_ref[k:k + 1, cols] * xpad[5 + k:5 + k + tt, cols]
        a, u = _lru_gate_block(xc, n, wrg_bf, wig_bf, brg_ref, big_ref, lam_ref)
        a_s[:, cols] = a
        u_s[:, cols] = u

    def scan_body(gidx, h):
        base = pl.multiple_of(gidx * 8, 8)
        a8 = a_s[pl.ds(base, 8), :]
        u8 = u_s[pl.ds(base, 8), :]
        rows = []
        for s in range(8):
            h = a8[s:s + 1, :] * h + u8[s:s + 1, :]
            rows.append(h)
        a_s[pl.ds(base, 8), :] = jnp.concatenate(rows, axis=0)
        return h

    h = lax.fori_loop(0, tt // 8, scan_body, h_s[...])
    h_s[...] = h
    m_ref[...] = (a_s[...] * yb_ref[...].astype(F32)).astype(BF16)
    xpad[0:8, :] = xpad[tt:tt + 8, :]

    @pl.when(j == pl.num_programs(1) - 1)
    def _():
        conv_ref[...] = xpad[5:8, :]
        hlast_ref[...] = h


def _lru_prompt(xb, yb, batch, seq, cw, cb, wrg, wig, brg, big, lam, *, tt=256):
    nj = seq // tt
    row = lambda b, j: (b * nj + j, 0)
    const2 = lambda b, j: (0, 0)
    const3 = lambda b, j: (0, 0, 0)
    return pl.pallas_call(
        _lru_prompt_kernel,
        grid=(batch, nj),
        in_specs=[
            pl.BlockSpec((tt, D_MODEL), row),
            pl.BlockSpec((tt, D_MODEL), row),
            pl.BlockSpec((CONV_WIDTH, D_MODEL), const2),
            pl.BlockSpec((1, D_MODEL), const2),
            pl.BlockSpec((LRU_BLOCKS, LRU_BLOCK, LRU_BLOCK), const3),
            pl.BlockSpec((LRU_BLOCKS, LRU_BLOCK, LRU_BLOCK), const3),
            pl.BlockSpec((1, D_MODEL), const2),
            pl.BlockSpec((1, D_MODEL), const2),
            pl.BlockSpec((1, D_MODEL), const2),
        ],
        out_specs=[
            pl.BlockSpec((tt, D_MODEL), row),
            pl.BlockSpec((None, CONV_WIDTH - 1, D_MODEL), lambda b, j: (b, 0, 0)),
            pl.BlockSpec((None, 1, D_MODEL), lambda b, j: (b, 0, 0)),
        ],
        out_shape=[
            jax.ShapeDtypeStruct((batch * seq, D_MODEL), BF16),
            jax.ShapeDtypeStruct((batch, CONV_WIDTH - 1, D_MODEL), F32),
            jax.ShapeDtypeStruct((batch, 1, D_MODEL), F32),
        ],
        scratch_shapes=[
            pltpu.VMEM((tt + 8, D_MODEL), F32),
            pltpu.VMEM((tt, D_MODEL), F32),
            pltpu.VMEM((tt, D_MODEL), F32),
            pltpu.VMEM((1, D_MODEL), F32),
            pltpu.VMEM((LRU_BLOCKS, LRU_BLOCK, LRU_BLOCK), BF16),
            pltpu.VMEM((LRU_BLOCKS, LRU_BLOCK, LRU_BLOCK), BF16),
        ],
        compiler_params=_params(("arbitrary", "arbitrary"), 40),
        name="lru_prompt",
    )(xb, yb, cw, cb, wrg, wig, brg, big, lam)


def _lru_sample_kernel(xb_ref, yb_ref, cs_ref, h0_ref, cw_ref, cb_ref, wrg_ref, wig_ref,
                       brg_ref, big_ref, lam_ref, m_ref, conv_ref, hlast_ref, wrg_bf, wig_bf, *, steps):
    batch = h0_ref.shape[0]
    _cast_gate_weights(wrg_ref, wig_ref, wrg_bf, wig_bf)
    m_ref[steps * batch:, :] = jnp.zeros((m_ref.shape[0] - steps * batch, D_MODEL), BF16)

    def slab(t, cols):
        if t < CONV_WIDTH - 1:
            return cs_ref[t, :, cols]
        t -= CONV_WIDTH - 1
        return xb_ref[t * batch:(t + 1) * batch, cols]

    for n in range(LRU_BLOCKS):
        cols = slice(n * LRU_BLOCK, (n + 1) * LRU_BLOCK)
        h = h0_ref[:, cols]
        for t in range(steps):
            xc = cb_ref[:, cols] + cw_ref[0:1, cols] * slab(t, cols)
            for k in range(1, CONV_WIDTH):
                xc = xc + cw_ref[k:k + 1, cols] * slab(t + k, cols)
            a, u = _lru_gate_block(xc, n, wrg_bf, wig_bf, brg_ref, big_ref, lam_ref)
            h = a * h + u
            rows = slice(t * batch, (t + 1) * batch)
            m_ref[rows, cols] = (h * yb_ref[rows, cols].astype(F32)).astype(BF16)
        hlast_ref[:, cols] = h
    for k in range(CONV_WIDTH - 1):
        conv_ref[k] = slab(steps + k, slice(None))


def _lru_sample(xb, yb, tile, steps, conv_state, h0, cw, cb, wrg, wig, brg, big, lam):
    batch = h0.shape[0]
    tok = pl.BlockSpec((TOK_TILE, D_MODEL), lambda i: (tile, 0))
    full = lambda a: pl.BlockSpec(a.shape, lambda i: (0,) * a.ndim)
    small = (conv_state, h0, cw, cb, wrg, wig, brg, big, lam)
    return pl.pallas_call(
        functools.partial(_lru_sample_kernel, steps=steps),
        grid=(1,),
        in_specs=[tok, tok] + [full(a) for a in small],
        out_specs=[
            pl.BlockSpec((TOK_TILE, D_MODEL), lambda i: (0, 0)),
            pl.BlockSpec((CONV_WIDTH - 1, batch, D_MODEL), lambda i: (0, 0, 0)),
            pl.BlockSpec((batch, D_MODEL), lambda i: (0, 0)),
        ],
        out_shape=[
            jax.ShapeDtypeStruct((TOK_TILE, D_MODEL), BF16),
            jax.ShapeDtypeStruct((CONV_WIDTH - 1, batch, D_MODEL), F32),
            jax.ShapeDtypeStruct((batch, D_MODEL), F32),
        ],
        scratch_shapes=[
            pltpu.VMEM((LRU_BLOCKS, LRU_BLOCK, LRU_BLOCK), BF16),
            pltpu.VMEM((LRU_BLOCKS, LRU_BLOCK, LRU_BLOCK), BF16),
        ],
        compiler_params=_params(("arbitrary",), 32),
        name="lru_sample",
    )(xb, yb, *small)


def _rel_bucket(dist):
    n = jnp.maximum(dist, 0)
    max_exact = N_BUCKETS // 2
    nf = jnp.maximum(n, 1).astype(F32)
    large = max_exact + (jnp.log(nf / max_exact) / math.log(MAX_DISTANCE / max_exact)
                         * (N_BUCKETS - max_exact)).astype(jnp.int32)
    large = jnp.minimum(large, N_BUCKETS - 1)
    return jnp.where(n < max_exact, n, large)


def _masked_buckets(dist):
    valid = (dist >= 0) & (dist < WINDOW)
    return jnp.where(valid, _rel_bucket(dist), -1).astype(jnp.int32)


def _build_bias(bucket, tab_ref, head):
    def body(bi, acc):
        return jnp.where(bucket == bi, tab_ref[bi * N_HEADS + head], acc)
    return lax.fori_loop(0, N_BUCKETS, body, jnp.full(bucket.shape, NEG_INF, F32))


def _softmax_pv(s, sink, v):
    m = jnp.maximum(jnp.max(s, axis=-1, keepdims=True), sink)
    p = jnp.exp(s - m)
    den = jnp.sum(p, axis=-1, keepdims=True) + jnp.exp(sink - m)
    return jnp.dot(p.astype(BF16), v, preferred_element_type=F32) / den


def _attn_prompt_kernel(q_ref, kvp_ref, kvc_ref, bucket_ref, tab_ref, sink_ref, o_ref, bias_s):
    b = pl.program_id(0)
    n = pl.program_id(1)

    @pl.when(jnp.logical_and(b == 0, n == 0))
    def _():
        bucket = bucket_ref[...]

        col = lax.broadcasted_iota(jnp.int32, (WINDOW, 2 * WINDOW), 1)

        def head_body(h, c):
            bias = _build_bias(bucket, tab_ref, h)
            sink = sink_ref[h]
            g = h // GROUP
            r0 = pl.multiple_of((h % GROUP) * WINDOW, WINDOW)
            bias_s[0, g, pl.ds(r0, WINDOW), :] = jnp.where(col == 0, sink, bias)
            bias_s[1, g, pl.ds(r0, WINDOW), :] = jnp.where(
                col == 0, sink, jnp.where(col < WINDOW, NEG_INF, bias))
            return c

        lax.fori_loop(0, N_HEADS, head_body, 0)

    first = (n == 0).astype(jnp.int32)
    row = lax.broadcasted_iota(jnp.int32, kvp_ref.shape, 0)
    kv_prev = jnp.where(row == 0, 0.0, kvp_ref[...])
    kv = jnp.concatenate([kv_prev, kvc_ref[...]], axis=0).astype(BF16)
    ones = jnp.ones((2 * WINDOW, HEAD_DIM), BF16)
    lane = lax.broadcasted_iota(jnp.int32, (WINDOW, 2 * HEAD_DIM), 1)
    for g in range(N_KV_HEADS):
        heads = range(g * GROUP, (g + 1) * GROUP)
        kg = kv[:, g * HEAD_DIM:(g + 1) * HEAD_DIM]
        vg = kv[:, KV_DIM + g * HEAD_DIM:KV_DIM + (g + 1) * HEAD_DIM]
        v_ext = jnp.concatenate([vg, ones], axis=1)
        qg = jnp.concatenate([q_ref[:, h * HEAD_DIM:(h + 1) * HEAD_DIM] for h in heads], axis=0)
        s = lax.dot_general(qg, kg, (((1,), (1,)), ((), ())), preferred_element_type=F32)
        s = s + bias_s[first, g]
        p = jnp.exp(s - jnp.max(s, axis=-1, keepdims=True)).astype(BF16)
        o_ext = jnp.dot(p, v_ext, preferred_element_type=F32)
        o_rot = pltpu.roll(o_ext, HEAD_DIM, axis=1)
        for pair in range(GROUP // 2):
            r0 = slice(2 * pair * WINDOW, (2 * pair + 1) * WINDOW)
            r1 = slice((2 * pair + 1) * WINDOW, (2 * pair + 2) * WINDOW)
            even = o_ext[r0] * (1.0 / o_rot[r0])
            odd = o_rot[r1] * (1.0 / o_ext[r1])
            c0 = (g * GROUP + 2 * pair) * HEAD_DIM
            o_ref[:, c0:c0 + 2 * HEAD_DIM] = jnp.where(lane < HEAD_DIM, even, odd).astype(BF16)


def _attn_prompt(q, kv, batch, seq, bucket, tab, sinks):
    nb = seq // WINDOW
    smem = pl.BlockSpec(memory_space=pltpu.SMEM)
    return pl.pallas_call(
        _attn_prompt_kernel,
        grid=(batch, nb),
        in_specs=[
            pl.BlockSpec((WINDOW, D_MODEL), lambda b, n: (b * nb + n, 0)),
            pl.BlockSpec((WINDOW, 2 * KV_DIM), lambda b, n: (jnp.maximum(b * nb + n - 1, 0), 0)),
            pl.BlockSpec((WINDOW, 2 * KV_DIM), lambda b, n: (b * nb + n, 0)),
            pl.BlockSpec((WINDOW, 2 * WINDOW), lambda b, n: (0, 0)),
            smem, smem,
        ],
        out_specs=pl.BlockSpec((WINDOW, D_MODEL), lambda b, n: (b * nb + n, 0)),
        out_shape=jax.ShapeDtypeStruct((batch * seq, D_MODEL), BF16),
        scratch_shapes=[pltpu.VMEM((2, N_KV_HEADS, GROUP * WINDOW, 2 * WINDOW), F32)],
        compiler_params=_params(("arbitrary", "arbitrary"), 32),
        name="attn_prompt",
    )(q, kv, kv, bucket, tab, sinks)


def _attn_sample_kernel(q_ref, k_ref, v_ref, bucket_ref, tab_ref, sink_ref, o_ref, bias_s):
    steps = q_ref.shape[0]

    @pl.when(pl.program_id(0) == 0)
    def _():
        bucket = bucket_ref[...]

        def head_body(h, c):
            bias_s[h] = _build_bias(bucket, tab_ref, h)
            return c

        lax.fori_loop(0, N_HEADS, head_body, 0)

    rows = lax.broadcasted_iota(jnp.int32, (GROUP * steps, 1), 0)
    for g in range(N_KV_HEADS):
        kg = k_ref[:, g * HEAD_DIM:(g + 1) * HEAD_DIM].astype(BF16)
        vg = v_ref[:, g * HEAD_DIM:(g + 1) * HEAD_DIM].astype(BF16)
        heads = range(g * GROUP, (g + 1) * GROUP)
        qg = jnp.concatenate([q_ref[:, h * HEAD_DIM:(h + 1) * HEAD_DIM] for h in heads], axis=0)
        bias = jnp.concatenate([bias_s[h] for h in heads], axis=0)
        sink = jnp.full((GROUP * steps, 1), sink_ref[g * GROUP], F32)
        for hh in range(1, GROUP):
            sink = jnp.where(rows >= hh * steps, sink_ref[g * GROUP + hh], sink)
        s = lax.dot_general(qg, kg, (((1,), (1,)), ((), ())), preferred_element_type=F32) + bias
        o = _softmax_pv(s, sink, vg)
        for hh, h in enumerate(heads):
            o_ref[:, h * HEAD_DIM:(h + 1) * HEAD_DIM] = o[hh * steps:(hh + 1) * steps].astype(BF16)


def _attn_sample(q, k_all, v_all, bucket, tab, sinks):
    batch, steps, _ = q.shape
    lk = k_all.shape[1]
    smem = pl.BlockSpec(memory_space=pltpu.SMEM)
    return pl.pallas_call(
        _attn_sample_kernel,
        grid=(batch,),
        in_specs=[
            pl.BlockSpec((None, steps, D_MODEL), lambda b: (b, 0, 0)),
            pl.BlockSpec((None, lk, KV_DIM), lambda b: (b, 0, 0)),
            pl.BlockSpec((None, lk, KV_DIM), lambda b: (b, 0, 0)),
            pl.BlockSpec((steps, lk), lambda b: (0, 0)),
            smem, smem,
        ],
        out_specs=pl.BlockSpec((None, steps, D_MODEL), lambda b: (b, 0, 0)),
        out_shape=jax.ShapeDtypeStruct((batch, steps, D_MODEL), BF16),
        scratch_shapes=[pltpu.VMEM((N_HEADS, steps, lk), F32)],
        compiler_params=_params(("arbitrary",), 32),
        name="attn_sample",
    )(q, k_all, v_all, bucket, tab, sinks)


def kernel(x_prompt, x_sample, state_conv, state_rnn, cache_k_win, cache_v_win, ln_g, ln_b, lru_w_x, lru_b_x, lru_w_y, lru_b_y, lru_conv_w, lru_conv_b, lru_w_rg, lru_b_rg, lru_w_ig, lru_b_ig, lru_lam, lru_w_out, lru_b_out, attn_w_kv, attn_w_q, attn_w_o, attn_sinks, rel_bias, moe_w_router, moe_b_router, moe_w_gate, moe_w_up, moe_w_down):
    bp, seq, _ = x_prompt.shape
    bs, steps, _ = x_sample.shape
    n_p = bp * seq
    n_s = bs * steps

    assert n_p % TOK_TILE == 0 and n_s <= TOK_TILE
    sample_tile = n_p // TOK_TILE

    def pad_tile(rows):
        return jnp.pad(rows, ((0, TOK_TILE - n_s), (0, 0)))

    x0 = (x_prompt.reshape(n_p, D_MODEL),
          pad_tile(x_sample.transpose(1, 0, 2).reshape(n_s, D_MODEL)))
    wr_t = moe_w_router.T
    br = moe_b_router.reshape(N_EXPERTS, 1)
    vec = lambda a: a.reshape(1, -1)

    xb = _linear(x0, lru_w_x, (0,), vec(lru_b_x[0]), F32, name="lru_in_x")
    yb = _linear(x0, lru_w_y, (0,), vec(lru_b_y[0]), BF16, act="gelu", name="lru_in_y")
    lru_args = (lru_conv_w[0], vec(lru_conv_b[0]), lru_w_rg[0], lru_w_ig[0],
                vec(lru_b_rg[0]), vec(lru_b_ig[0]), vec(lru_lam[0]))
    m_p, conv_p, rnn_p = _lru_prompt(xb, yb, bp, seq, *lru_args)
    m_s, conv_s, rnn_s = _lru_sample(xb, yb, sample_tile, steps,
                                     state_conv[0].transpose(1, 0, 2), state_rnn[0], *lru_args)
    x1, e_idx, gates = _proj_ln((m_p, m_s), lru_w_out, (0,), vec(lru_b_out[0]), x0,
                                vec(ln_g[0, 0]), vec(ln_b[0, 0]), wr_t, br, name="lru_out_ln")
    x2, x2_bf = _moe_block(x1, e_idx, gates, moe_w_gate, moe_w_up, moe_w_down, 0,
                           vec(ln_g[0, 1]), vec(ln_b[0, 1]), (F32, BF16))

    kv = _linear(x2_bf, attn_w_kv, (), jnp.zeros((1, 2 * KV_DIM), F32), F32, name="attn_kv")
    q = _linear(x2_bf, attn_w_q, (0,), jnp.zeros((1, D_MODEL), F32), BF16, scale=HEAD_DIM ** -0.5,
                name="attn_q")
    tab = rel_bias.reshape(-1)
    sinks = attn_sinks[0]
    qi = jnp.arange(WINDOW)[:, None]
    kj = jnp.arange(2 * WINDOW)[None, :]
    o_p = _attn_prompt(q, kv, bp, seq, _masked_buckets(qi + WINDOW - kj), tab, sinks)
    kv_s = kv[n_p:n_p + n_s].reshape(steps, bs, 2, KV_DIM).transpose(2, 1, 0, 3)
    k_all = jnp.concatenate([cache_k_win.reshape(bs, WINDOW, KV_DIM), kv_s[0]], axis=1)
    v_all = jnp.concatenate([cache_v_win.reshape(bs, WINDOW, KV_DIM), kv_s[1]], axis=1)
    dist_s = jnp.arange(steps)[:, None] + WINDOW - jnp.arange(WINDOW + steps)[None, :]
    q_s = q[n_p:n_p + n_s].reshape(steps, bs, D_MODEL).transpose(1, 0, 2)
    o_s = _attn_sample(q_s, k_all, v_all, _masked_buckets(dist_s), tab, sinks)
    o_s = pad_tile(o_s.transpose(1, 0, 2).reshape(n_s, D_MODEL))
    x3, e_idx, gates = _proj_ln((o_p, o_s), attn_w_o, (0,), jnp.zeros((1, D_MODEL), F32), x2,
                                vec(ln_g[1, 0]), vec(ln_b[1, 0]), wr_t, br, name="attn_out_ln")
    y_p, y_s = _moe_block(x3, e_idx, gates, moe_w_gate, moe_w_up, moe_w_down, 1,
                          vec(ln_g[1, 1]), vec(ln_b[1, 1]), (F32,), split=True)

    y_prompt = y_p.reshape(bp, seq, D_MODEL)
    y_sample = y_s[:n_s].reshape(steps, bs, D_MODEL).transpose(1, 0, 2)
    kv_p = jnp.stack([kv[(b + 1) * seq - WINDOW:(b + 1) * seq] for b in range(bp)])
    kv_p = kv_p.reshape(bp, WINDOW, 2, N_KV_HEADS, HEAD_DIM)
    k_win_s = k_all[:, steps:].reshape(bs, WINDOW, N_KV_HEADS, HEAD_DIM)
    v_win_s = v_all[:, steps:].reshape(bs, WINDOW, N_KV_HEADS, HEAD_DIM)
    return (y_prompt, y_sample,
            conv_p[None], rnn_p.reshape(1, bp, D_MODEL),
            kv_p[:, :, 0], kv_p[:, :, 1],
            conv_s.transpose(1, 0, 2)[None], rnn_s[None],
            k_win_s, v_win_s)
```

```python
import functools
import math

import jax
import jax.numpy as jnp
from jax import lax
from jax.experimental import pallas as pl
from jax.experimental.pallas import tpu as pltpu

D_MODEL = 2048
DEPTH = 2
LRU_BLOCKS = 8
LRU_BLOCK = D_MODEL // LRU_BLOCKS
CONV_WIDTH = 4
LRU_C = 8.0
N_HEADS = 32
HEAD_DIM = 64
N_KV_HEADS = 8
GROUP = N_HEADS // N_KV_HEADS
KV_DIM = N_KV_HEADS * HEAD_DIM
WINDOW = 128
N_BUCKETS = 32
MAX_DISTANCE = 128
N_EXPERTS = 16
N_GROUPS = 4
EXPERTS_PER_GROUP = N_EXPERTS // N_GROUPS
D_EXPERT = 1024
ALPHA = (2 * DEPTH) ** 0.25
LN_EPS = 1e-5

LANES = 128
MOE_TILE = 256
TOK_TILE = 256
DMA_UNROLL = 8
SCALAR_UNROLL = 32
PLAN_UNROLL = 4
CAST_ROWS = 256
BF16 = jnp.bfloat16
F32 = jnp.float32
NEG_INF = float("-inf")


def _params(sem, vmem_mb):
    return pltpu.CompilerParams(dimension_semantics=sem, vmem_limit_bytes=vmem_mb * 1024 * 1024)


def _cast_rows(src_ref, dst_ref):
    n = src_ref.shape[0] // CAST_ROWS

    def body(i, c):
        r = pl.multiple_of(i * CAST_ROWS, CAST_ROWS)
        dst_ref[pl.ds(r, CAST_ROWS), :] = src_ref[pl.ds(r, CAST_ROWS), :].astype(BF16)
        return c

    lax.fori_loop(0, n, body, 0)


def _layer_norm(z, g, b):
    mu = jnp.mean(z, axis=-1, keepdims=True)
    zc = z - mu
    var = jnp.mean(zc * zc, axis=-1, keepdims=True)
    return zc * lax.rsqrt(var + LN_EPS) * g + b


def _tok_operands(x):
    if isinstance(x, tuple):
        xp, xs = x
        d = xp.shape[1]
        last_p = xp.shape[0] // TOK_TILE - 1
        specs = [pl.BlockSpec((TOK_TILE, d), lambda i, *_: (jnp.minimum(i, last_p), 0)),
                 pl.BlockSpec((TOK_TILE, d), lambda i, *_: (0, 0))]
        return [xp, xs], specs, last_p + 2
    return [x], [pl.BlockSpec((TOK_TILE, x.shape[1]), lambda i, *_: (i, 0))], x.shape[0] // TOK_TILE


def _tok_load(refs):
    if len(refs) == 1:
        return refs[0][...]
    return jnp.where(pl.program_id(0) < pl.num_programs(0) - 1, refs[0][...], refs[1][...])


def _linear_kernel(*refs, n_x, act, scale):
    x_refs, (w_ref, b_ref, o_ref, wbf_ref) = refs[:n_x], refs[n_x:]

    @pl.when(pl.program_id(0) == 0)
    def _():
        _cast_rows(w_ref, wbf_ref)

    y = jnp.dot(_tok_load(x_refs).astype(BF16), wbf_ref[...], preferred_element_type=F32)
    y = y + b_ref[...]
    if act == "gelu":
        y = jax.nn.gelu(y)
    if scale != 1.0:
        y = y * scale
    o_ref[...] = y.astype(o_ref.dtype)


def _linear(x, w, w_index, b, out_dtype, *, name, act=None, scale=1.0):
    arrays, specs, nt = _tok_operands(x)
    k, nout = w.shape[-2:]
    w_block = (None,) * len(w_index) + (k, nout)
    return pl.pallas_call(
        functools.partial(_linear_kernel, n_x=len(arrays), act=act, scale=scale),
        grid=(nt,),
        in_specs=specs + [
            pl.BlockSpec(w_block, lambda i: w_index + (0, 0), pipeline_mode=pl.Buffered(1)),
            pl.BlockSpec((1, nout), lambda i: (0, 0)),
        ],
        out_specs=pl.BlockSpec((TOK_TILE, nout), lambda i: (i, 0)),
        out_shape=jax.ShapeDtypeStruct((nt * TOK_TILE, nout), out_dtype),
        scratch_shapes=[pltpu.VMEM((k, nout), BF16)],
        compiler_params=_params(("arbitrary",), 48),
        name=name,
    )(*arrays, w, b)


def _route(logits_t, b_router):
    aff = jax.nn.sigmoid(logits_t)
    sel = aff + b_router
    srow = [sel[e:e + 1, :] for e in range(N_EXPERTS)]
    arow = [aff[e:e + 1, :] for e in range(N_EXPERTS)]

    def top2_sum(v):
        pairs = [v[i] + v[j] for i in range(4) for j in range(i + 1, 4)]
        return functools.reduce(jnp.maximum, pairs)

    scores = [top2_sum(srow[4 * g:4 * g + 4]) for g in range(N_GROUPS)]
    best = scores[0]
    gi = jnp.zeros_like(best, dtype=jnp.int32)
    for g in range(1, N_GROUPS):
        upd = scores[g] > best
        best = jnp.where(upd, scores[g], best)
        gi = jnp.where(upd, g, gi)

    def pick_group(rows, j):
        out = rows[j]
        for g in range(1, N_GROUPS):
            out = jnp.where(gi == g, rows[4 * g + j], out)
        return out

    v = [pick_group(srow, j) for j in range(EXPERTS_PER_GROUP)]
    a = [pick_group(arow, j) for j in range(EXPERTS_PER_GROUP)]

    m1, i1 = v[0], jnp.zeros_like(gi)
    for j in range(1, EXPERTS_PER_GROUP):
        upd = v[j] > m1
        m1 = jnp.where(upd, v[j], m1)
        i1 = jnp.where(upd, j, i1)
    m2 = jnp.full_like(m1, NEG_INF)
    i2 = jnp.zeros_like(gi)
    for j in range(EXPERTS_PER_GROUP):
        cand = jnp.where(i1 == j, NEG_INF, v[j])
        upd = cand > m2
        m2 = jnp.where(upd, cand, m2)
        i2 = jnp.where(upd, j, i2)

    def pick_idx(rows, idx):
        out = rows[0]
        for j in range(1, EXPERTS_PER_GROUP):
            out = jnp.where(idx == j, rows[j], out)
        return out

    a1 = pick_idx(a, i1)
    a2 = pick_idx(a, i2)
    tot = a1 + a2
    e_idx = jnp.concatenate([gi * EXPERTS_PER_GROUP + i1, gi * EXPERTS_PER_GROUP + i2], axis=0)
    gates = jnp.concatenate([a1 / tot, a2 / tot], axis=0)
    return e_idx, gates


def _proj_ln_kernel(*refs, n_m, n_res):
    m_refs = refs[:n_m]
    w_ref, b_ref = refs[n_m:n_m + 2]
    res_refs = refs[n_m + 2:n_m + 2 + n_res]
    g_ref, beta_ref, wr_ref, br_ref, x_ref, e_ref, gate_ref, wbf_ref = refs[n_m + 2 + n_res:]

    @pl.when(pl.program_id(0) == 0)
    def _():
        _cast_rows(w_ref, wbf_ref)

    y = jnp.dot(_tok_load(m_refs), wbf_ref[...], preferred_element_type=F32) + b_ref[...]
    x = _layer_norm(ALPHA * _tok_load(res_refs) + y, g_ref[...], beta_ref[...])
    x_ref[...] = x
    logits_t = lax.dot_general(wr_ref[...].astype(BF16), x.astype(BF16),
                               (((1,), (1,)), ((), ())), preferred_element_type=F32)
    e_idx, gates = _route(logits_t, br_ref[...])
    e_ref[...] = e_idx
    gate_ref[...] = gates


def _proj_ln(m, w, w_index, b, res, g, beta, wr_t, br, *, name):
    m_arrays, m_specs, nt = _tok_operands(m)
    res_arrays, res_specs, _ = _tok_operands(res)
    k = w.shape[-2]
    tm = TOK_TILE
    n = nt * tm
    row = lambda i: (i, 0)
    const = lambda i: (0, 0)
    x, e_idx, gates = pl.pallas_call(
        functools.partial(_proj_ln_kernel, n_m=len(m_arrays), n_res=len(res_arrays)),
        grid=(nt,),
        in_specs=m_specs + [
            pl.BlockSpec((None,) * len(w_index) + (k, D_MODEL), lambda i: w_index + (0, 0),
                         pipeline_mode=pl.Buffered(1)),
            pl.BlockSpec((1, D_MODEL), const),
        ] + res_specs + [
            pl.BlockSpec((1, D_MODEL), const),
            pl.BlockSpec((1, D_MODEL), const),
            pl.BlockSpec((N_EXPERTS, D_MODEL), const),
            pl.BlockSpec((N_EXPERTS, 1), const),
        ],
        out_specs=[
            pl.BlockSpec((tm, D_MODEL), row),
            pl.BlockSpec((None, 2, tm), lambda i: (i, 0, 0)),
            pl.BlockSpec((None, 2, tm), lambda i: (i, 0, 0)),
        ],
        out_shape=[
            jax.ShapeDtypeStruct((n, D_MODEL), F32),
            jax.ShapeDtypeStruct((nt, 2, tm), jnp.int32),
            jax.ShapeDtypeStruct((nt, 2, tm), F32),
        ],
        scratch_shapes=[pltpu.VMEM((k, D_MODEL), BF16)],
        compiler_params=_params(("arbitrary",), 48),
        name=name,
    )(*m_arrays, w, b, *res_arrays, g, beta, wr_t, br)
    e_idx = e_idx.transpose(1, 0, 2).reshape(2, n)
    gates = gates.transpose(1, 0, 2).reshape(2, n)
    return x, e_idx, gates


def _plan_kernel(e_ref, pos_ref, meta_ref, rank_ref):
    nrow = e_ref.shape[0]
    ri = lax.broadcasted_iota(jnp.int32, (LANES, LANES), 0)
    ci = lax.broadcasted_iota(jnp.int32, (LANES, LANES), 1)
    tri = jnp.where(ri <= ci, 1.0, 0.0).astype(BF16)
    sub = lax.broadcasted_iota(jnp.int32, (N_EXPERTS, LANES), 0)

    def count_body(b, base):
        rows = [b * PLAN_UNROLL + u for u in range(PLAN_UNROLL)]
        onehots = [sub == e_ref[pl.ds(r, 1), :] for r in rows]
        locs = [jnp.dot(jnp.where(oh, 1.0, 0.0).astype(BF16), tri, preferred_element_type=F32)
                for oh in onehots]
        for r, onehot, loc in zip(rows, onehots, locs):
            rank_ref[pl.ds(r, 1), :] = jnp.sum(jnp.where(onehot, base + loc - 1.0, 0.0),
                                               axis=0, keepdims=True)
            base = base + jnp.broadcast_to(loc[:, LANES - 1:LANES], (N_EXPERTS, LANES))
        return base

    count = lax.fori_loop(0, nrow // PLAN_UNROLL, count_body, jnp.zeros((N_EXPERTS, LANES), F32))
    ntile = jnp.floor((count + (MOE_TILE - 1.0)) * (1.0 / MOE_TILE))
    offs = []
    acc = jnp.zeros((1, LANES), F32)
    for e in range(N_EXPERTS):
        offs.append(acc)
        acc = acc + ntile[e:e + 1, :]
    tile_off = jnp.concatenate(offs, axis=0)
    tile_end = tile_off + ntile
    lane = lax.broadcasted_iota(jnp.int32, (N_EXPERTS, LANES), 1).astype(F32)
    tile_expert = jnp.sum(jnp.where(tile_end <= lane, 1.0, 0.0), axis=0, keepdims=True)
    tile_expert = jnp.minimum(tile_expert, N_EXPERTS - 1.0)
    meta = jnp.concatenate([tile_expert, acc, jnp.zeros((6, LANES), F32)], axis=0)
    meta_ref[...] = meta.astype(jnp.int32)
    row_off = tile_off * float(MOE_TILE)

    def pos_body(r, c):
        onehot = sub == e_ref[pl.ds(r, 1), :]
        p = jnp.sum(jnp.where(onehot, row_off, 0.0), axis=0, keepdims=True) + rank_ref[pl.ds(r, 1), :]
        pos_ref[pl.ds(r, 1), :] = p.astype(jnp.int32)
        return c

    lax.fori_loop(0, nrow, pos_body, 0)


def _plan(e_idx, *, name):
    n2 = e_idx.shape[0] * e_idx.shape[1]
    assert n2 % (LANES * PLAN_UNROLL) == 0
    e2d = e_idx.reshape(n2 // LANES, LANES)
    pos, meta = pl.pallas_call(
        _plan_kernel,
        out_shape=[jax.ShapeDtypeStruct(e2d.shape, jnp.int32),
                   jax.ShapeDtypeStruct((8, LANES), jnp.int32)],
        scratch_shapes=[pltpu.VMEM(e2d.shape, F32)],
        name=name,
    )(e2d)
    return pos.reshape(n2), meta[0], meta[1, :1]


def _invert_kernel(pos_ref, pair_ref):
    n_rows = pair_ref.shape[0]
    n_pairs = pos_ref.shape[0]

    def fill_body(b, c):
        for u in range(SCALAR_UNROLL):
            pair_ref[b * SCALAR_UNROLL + u] = -1
        return c

    def pair_body(b, c):
        rows = [pos_ref[b * SCALAR_UNROLL + u] for u in range(SCALAR_UNROLL)]
        for u in range(SCALAR_UNROLL):
            pair_ref[rows[u]] = b * SCALAR_UNROLL + u
        return c

    lax.fori_loop(0, n_rows // SCALAR_UNROLL, fill_body, 0)
    lax.fori_loop(0, n_pairs // SCALAR_UNROLL, pair_body, 0)


def _invert(pos, n_rows, *, name):
    return pl.pallas_call(
        _invert_kernel,
        grid_spec=pltpu.PrefetchScalarGridSpec(
            num_scalar_prefetch=1,
            grid=(1,),
            in_specs=[],
            out_specs=pl.BlockSpec(memory_space=pltpu.SMEM),
        ),
        out_shape=jax.ShapeDtypeStruct((n_rows,), jnp.int32),
        name=name,
    )(pos)


def _expert_changed(te_ref, i):
    return jnp.logical_or(i == 0, te_ref[i] != te_ref[jnp.maximum(i - 1, 0)])


def _moe_up_kernel(pair_ref, te_ref, nu_ref, x_hbm, wg_ref, wu_ref, h_ref,
                   wg_bf, wu_bf, xa, xb, sems, *, n_tok, n_tiles):
    i = pl.program_id(0)
    nu = nu_ref[0]
    bufs = (xa, xb)

    def row_copy(tile, r, buf, sem):
        p = pair_ref[tile * MOE_TILE + r]
        tok = jnp.where(p >= n_tok, p - n_tok, jnp.maximum(p, 0))
        return pltpu.make_async_copy(x_hbm.at[pl.ds(tok, 1), :], buf.at[pl.ds(r, 1), :], sem)

    def wait_tile(buf, sem):
        pltpu.make_async_copy(x_hbm.at[pl.ds(0, MOE_TILE), :], buf, sem).wait()

    @pl.when(i == 0)
    def _():
        def body(rb, c):
            for u in range(DMA_UNROLL):
                row_copy(0, rb * DMA_UNROLL + u, xa, sems.at[0]).start()
            return c
        lax.fori_loop(0, MOE_TILE // DMA_UNROLL, body, 0)

    for parity in range(2):
        cur, nxt = bufs[parity], bufs[1 - parity]
        cur_sem, nxt_sem = sems.at[parity], sems.at[1 - parity]
        mine = i % 2 == parity

        @pl.when(jnp.logical_and(mine, i <= nu))
        def _():
            wait_tile(cur, cur_sem)

        @pl.when(jnp.logical_and(mine, i < nu))
        def _():
            @pl.when(_expert_changed(te_ref, i))
            def _():
                _cast_rows(wg_ref, wg_bf)
                _cast_rows(wu_ref, wu_bf)

            next_tile = jnp.minimum(i + 1, n_tiles - 1)
            for r in range(MOE_TILE):
                row_copy(next_tile, r, nxt, nxt_sem).start(priority=r % 2)
            x = cur[...].astype(BF16)
            a = jnp.dot(x, wg_bf[...], preferred_element_type=F32)
            b = jnp.dot(x, wu_bf[...], preferred_element_type=F32)
            h_ref[...] = (jax.nn.silu(a) * b).astype(BF16)

        @pl.when(jnp.logical_and(mine, jnp.logical_and(i == n_tiles - 1, i < nu)))
        def _():
            wait_tile(nxt, nxt_sem)

    @pl.when(i >= nu)
    def _():
        h_ref[...] = jnp.zeros_like(h_ref)


def _moe_down_kernel(pair_ref, te_ref, nu_ref, h_ref, wd_ref, out_hbm,
                     wd_bf, ya, yb, sems, tsem, *, n_tok, n_tiles):
    i = pl.program_id(0)
    nu = nu_ref[0]
    bufs = (ya, yb)
    trash = 2 * n_tok

    def row_copy(tile, r, buf, sem):
        p = pair_ref[tile * MOE_TILE + r]
        dst = jnp.where(p < 0, trash + r, p)
        return pltpu.make_async_copy(buf.at[pl.ds(r, 1), :], out_hbm.at[pl.ds(dst, 1), :], sem)

    def wait_tile(buf, sem):
        pltpu.make_async_copy(buf, out_hbm.at[pl.ds(0, MOE_TILE), :], sem).wait()

    @pl.when(i == 0)
    def _():
        yb[...] = jnp.zeros_like(yb)
        fill = pltpu.make_async_copy(yb, out_hbm.at[pl.ds(trash, MOE_TILE), :], tsem)
        fill.start()
        fill.wait()

    for parity in range(2):
        cur, prev = bufs[parity], bufs[1 - parity]
        cur_sem, prev_sem = sems.at[parity], sems.at[1 - parity]
        mine = i % 2 == parity

        @pl.when(jnp.logical_and(mine, jnp.logical_and(i >= 1, i - 1 <= nu)))
        def _():
            wait_tile(cur, cur_sem)

        @pl.when(jnp.logical_and(mine, i < nu))
        def _():
            @pl.when(_expert_changed(te_ref, i))
            def _():
                _cast_rows(wd_ref, wd_bf)

            prev_tile = jnp.maximum(i - 1, 0)
            for r in range(MOE_TILE):
                row_copy(prev_tile, r, prev, prev_sem).start(priority=r % 2)
            cur[...] = jnp.dot(h_ref[...], wd_bf[...], preferred_element_type=F32)

        @pl.when(jnp.logical_and(mine, i == nu))
        def _():
            def body(rb, c):
                for u in range(DMA_UNROLL):
                    row_copy(i - 1, rb * DMA_UNROLL + u, prev, prev_sem).start()
                return c
            lax.fori_loop(0, MOE_TILE // DMA_UNROLL, body, 0)

            @pl.when(i == n_tiles)
            def _():
                wait_tile(prev, prev_sem)


def _moe_ffn(x, pair, tile_expert, n_used, w_gate, w_up, w_down, layer):
    n_tok = x.shape[0]
    n_tiles = pair.shape[0] // MOE_TILE

    def tile(i, pr, te, nu):
        return (jnp.minimum(i, nu[0] - 1), 0)

    def expert(i, pr, te, nu):
        return (layer, te[jnp.minimum(i, nu[0] - 1)], 0, 0)

    def weight_spec(k, n):
        return pl.BlockSpec((None, None, k, n), expert)

    row_bufs = [pltpu.VMEM((MOE_TILE, D_MODEL), F32), pltpu.VMEM((MOE_TILE, D_MODEL), F32),
                pltpu.SemaphoreType.DMA((2,))]
    h = pl.pallas_call(
        functools.partial(_moe_up_kernel, n_tok=n_tok, n_tiles=n_tiles),
        grid_spec=pltpu.PrefetchScalarGridSpec(
            num_scalar_prefetch=3,
            grid=(n_tiles,),
            in_specs=[pl.BlockSpec(memory_space=pl.ANY),
                      weight_spec(D_MODEL, D_EXPERT),
                      weight_spec(D_MODEL, D_EXPERT)],
            out_specs=pl.BlockSpec((MOE_TILE, D_EXPERT), lambda i, pr, te, nu: (i, 0)),
            scratch_shapes=[pltpu.VMEM((D_MODEL, D_EXPERT), BF16),
                            pltpu.VMEM((D_MODEL, D_EXPERT), BF16)] + row_bufs,
        ),
        out_shape=jax.ShapeDtypeStruct((n_tiles * MOE_TILE, D_EXPERT), BF16),
        compiler_params=_params(("arbitrary",), 56),
        name=f"moe_up_{layer}",
    )(pair, tile_expert, n_used, x, w_gate, w_up)
    return pl.pallas_call(
        functools.partial(_moe_down_kernel, n_tok=n_tok, n_tiles=n_tiles),
        grid_spec=pltpu.PrefetchScalarGridSpec(
            num_scalar_prefetch=3,
            grid=(n_tiles + 1,),
            in_specs=[pl.BlockSpec((MOE_TILE, D_EXPERT), tile),
                      weight_spec(D_EXPERT, D_MODEL)],
            out_specs=pl.BlockSpec(memory_space=pl.ANY),
            scratch_shapes=[pltpu.VMEM((D_EXPERT, D_MODEL), BF16)] + row_bufs
            + [pltpu.SemaphoreType.DMA(())],
        ),
        out_shape=jax.ShapeDtypeStruct((2 * n_tok + MOE_TILE, D_MODEL), F32),
        compiler_params=_params(("arbitrary",), 40),
        name=f"moe_down_{layer}",
    )(pair, tile_expert, n_used, h, w_down)


def _combine_ln_kernel(y0_ref, y1_ref, res_ref, gate_ref, g_ref, beta_ref, *outs, split):
    i = pl.program_id(0)
    gate = gate_ref[...]
    ffn = gate[:, 0:1] * y0_ref[...] + gate[:, 1:2] * y1_ref[...]
    x = _layer_norm(ALPHA * res_ref[...] + ffn, g_ref[...], beta_ref[...])
    if split:
        prompt_ref, sample_ref = outs
        is_sample = i == pl.num_programs(0) - 1

        @pl.when(jnp.logical_not(is_sample))
        def _():
            prompt_ref[...] = x

        @pl.when(is_sample)
        def _():
            sample_ref[...] = x
    else:
        for o_ref in outs:
            o_ref[...] = x.astype(o_ref.dtype)


def _combine_ln(ys, res, gates_col, g, beta, out_dtypes, *, split, name):
    n = res.shape[0]
    tm = TOK_TILE
    nt = n // tm
    row = lambda i: (i, 0)
    const = lambda i: (0, 0)
    if split:
        out_specs = [pl.BlockSpec((tm, D_MODEL), lambda i: (jnp.minimum(i, nt - 2), 0)),
                     pl.BlockSpec((tm, D_MODEL), const)]
        out_shape = [jax.ShapeDtypeStruct((n - tm, D_MODEL), F32),
                     jax.ShapeDtypeStruct((tm, D_MODEL), F32)]
    else:
        out_specs = [pl.BlockSpec((tm, D_MODEL), row) for _ in out_dtypes]
        out_shape = [jax.ShapeDtypeStruct((n, D_MODEL), dt) for dt in out_dtypes]
    return pl.pallas_call(
        functools.partial(_combine_ln_kernel, split=split),
        grid=(nt,),
        in_specs=[pl.BlockSpec((tm, D_MODEL), row),
                  pl.BlockSpec((tm, D_MODEL), lambda i: (i + nt, 0)),
                  pl.BlockSpec((tm, D_MODEL), row),
                  pl.BlockSpec((tm, 2), row),
                  pl.BlockSpec((1, D_MODEL), const),
                  pl.BlockSpec((1, D_MODEL), const)],
        out_specs=out_specs,
        out_shape=out_shape,
        compiler_params=_params(("arbitrary",), 40),
        name=name,
    )(ys, ys, res, gates_col, g, beta)


def _moe_block(x, e_idx, gates, w_gate, w_up, w_down, layer, g, beta, out_dtypes, split=False):
    n = x.shape[0]
    n_tiles = -(-(2 * n + N_EXPERTS * (MOE_TILE - 1)) // MOE_TILE)
    pos, tile_expert, n_used = _plan(e_idx, name=f"moe_plan_{layer}")
    pair = _invert(pos, n_tiles * MOE_TILE, name=f"moe_invert_{layer}")
    ys = _moe_ffn(x, pair, tile_expert, n_used, w_gate, w_up, w_down, layer)
    return _combine_ln(ys, x, gates.T, g, beta, out_dtypes, split=split,
                       name=f"moe_combine_{layer}")


def _log_sigmoid(x):
    return -(jnp.maximum(-x, 0.0) + jnp.log1p(jnp.exp(-jnp.abs(x))))


def _lru_gate_block(xc, n, wrg_bf, wig_bf, brg_ref, big_ref, lam_ref):
    cols = slice(n * LRU_BLOCK, (n + 1) * LRU_BLOCK)
    xb = xc.astype(BF16)
    r = jax.nn.sigmoid(jnp.dot(xb, wrg_bf[n], preferred_element_type=F32) + brg_ref[:, cols])
    i = jax.nn.sigmoid(jnp.dot(xb, wig_bf[n], preferred_element_type=F32) + big_ref[:, cols])
    log_a = LRU_C * r * _log_sigmoid(lam_ref[:, cols])
    a = jnp.exp(log_a)
    u = xc * i * jnp.sqrt(-jnp.tanh(log_a) * (a * a + 1.0))
    return a, u


def _cast_gate_weights(wrg_ref, wig_ref, wrg_bf, wig_bf):
    for n in range(LRU_BLOCKS):
        wrg_bf[n] = wrg_ref[n].astype(BF16)
        wig_bf[n] = wig_ref[n].astype(BF16)


def _lru_prompt_kernel(xb_ref, yb_ref, cw_ref, cb_ref, wrg_ref, wig_ref, brg_ref, big_ref, lam_ref,
                       m_ref, conv_ref, hlast_ref, xpad, a_s, u_s, h_s, wrg_bf, wig_bf):
    b = pl.program_id(0)
    j = pl.program_id(1)
    tt = xb_ref.shape[0]

    @pl.when(jnp.logical_and(b == 0, j == 0))
    def _():
        _cast_gate_weights(wrg_ref, wig_ref, wrg_bf, wig_bf)

    @pl.when(j == 0)
    def _():
        xpad[0:8, :] = jnp.zeros((8, D_MODEL), F32)
        h_s[...] = jnp.zeros_like(h_s)

    xpad[8:8 + tt, :] = xb_ref[...]
    for n in range(LRU_BLOCKS):
        cols = slice(n * LRU_BLOCK, (n + 1) * LRU_BLOCK)
        xc = cb_ref[:, cols] + cw_ref[0:1, cols] * xpad[5:5 + tt, cols]
        for k in range(1, CONV_WIDTH):
            xc = xc + cw_ref[k:k + 1, cols] * xpad[5 + k:5 + k + tt, cols]
        a, u = _lru_gate_block(xc, n, wrg_bf, wig_bf, brg_ref, big_ref, lam_ref)
        a_s[:, cols] = a
        u_s[:, cols] = u

    def scan_body(gidx, h):
        base = pl.multiple_of(gidx * 8, 8)
        a8 = a_s[pl.ds(base, 8), :]
        u8 = u_s[pl.ds(base, 8), :]
        rows = []
        for s in range(8):
            h = a8[s:s + 1, :] * h + u8[s:s + 1, :]
            rows.append(h)
        a_s[pl.ds(base, 8), :] = jnp.concatenate(rows, axis=0)
        return h

    h = lax.fori_loop(0, tt // 8, scan_body, h_s[...])
    h_s[...] = h
    m_ref[...] = (a_s[...] * yb_ref[...].astype(F32)).astype(BF16)
    xpad[0:8, :] = xpad[tt:tt + 8, :]

    @pl.when(j == pl.num_programs(1) - 1)
    def _():
        conv_ref[...] = xpad[5:8, :]
        hlast_ref[...] = h


def _lru_prompt(xb, yb, batch, seq, cw, cb, wrg, wig, brg, big, lam, *, tt=256):
    nj = seq // tt
    row = lambda b, j: (b * nj + j, 0)
    const2 = lambda b, j: (0, 0)
    const3 = lambda b, j: (0, 0, 0)
    return pl.pallas_call(
        _lru_prompt_kernel,
        grid=(batch, nj),
        in_specs=[
            pl.BlockSpec((tt, D_MODEL), row),
            pl.BlockSpec((tt, D_MODEL), row),
            pl.BlockSpec((CONV_WIDTH, D_MODEL), const2),
            pl.BlockSpec((1, D_MODEL), const2),
            pl.BlockSpec((LRU_BLOCKS, LRU_BLOCK, LRU_BLOCK), const3),
            pl.BlockSpec((LRU_BLOCKS, LRU_BLOCK, LRU_BLOCK), const3),
            pl.BlockSpec((1, D_MODEL), const2),
            pl.BlockSpec((1, D_MODEL), const2),
            pl.BlockSpec((1, D_MODEL), const2),
        ],
        out_specs=[
            pl.BlockSpec((tt, D_MODEL), row),
            pl.BlockSpec((None, CONV_WIDTH - 1, D_MODEL), lambda b, j: (b, 0, 0)),
            pl.BlockSpec((None, 1, D_MODEL), lambda b, j: (b, 0, 0)),
        ],
        out_shape=[
            jax.ShapeDtypeStruct((batch * seq, D_MODEL), BF16),
            jax.ShapeDtypeStruct((batch, CONV_WIDTH - 1, D_MODEL), F32),
            jax.ShapeDtypeStruct((batch, 1, D_MODEL), F32),
        ],
        scratch_shapes=[
            pltpu.VMEM((tt + 8, D_MODEL), F32),
            pltpu.VMEM((tt, D_MODEL), F32),
            pltpu.VMEM((tt, D_MODEL), F32),
            pltpu.VMEM((1, D_MODEL), F32),
            pltpu.VMEM((LRU_BLOCKS, LRU_BLOCK, LRU_BLOCK), BF16),
            pltpu.VMEM((LRU_BLOCKS, LRU_BLOCK, LRU_BLOCK), BF16),
        ],
        compiler_params=_params(("arbitrary", "arbitrary"), 40),
        name="lru_prompt",
    )(xb, yb, cw, cb, wrg, wig, brg, big, lam)


def _lru_sample_kernel(xb_ref, yb_ref, cs_ref, h0_ref, cw_ref, cb_ref, wrg_ref, wig_ref,
                       brg_ref, big_ref, lam_ref, m_ref, conv_ref, hlast_ref, wrg_bf, wig_bf, *, steps):
    batch = h0_ref.shape[0]
    _cast_gate_weights(wrg_ref, wig_ref, wrg_bf, wig_bf)
    m_ref[steps * batch:, :] = jnp.zeros((m_ref.shape[0] - steps * batch, D_MODEL), BF16)

    def slab(t, cols):
        if t < CONV_WIDTH - 1:
            return cs_ref[t, :, cols]
        t -= CONV_WIDTH - 1
        return xb_ref[t * batch:(t + 1) * batch, cols]

    for n in range(LRU_BLOCKS):
        cols = slice(n * LRU_BLOCK, (n + 1) * LRU_BLOCK)
        h = h0_ref[:, cols]
        for t in range(steps):
            xc = cb_ref[:, cols] + cw_ref[0:1, cols] * slab(t, cols)
            for k in range(1, CONV_WIDTH):
                xc = xc + cw_ref[k:k + 1, cols] * slab(t + k, cols)
            a, u = _lru_gate_block(xc, n, wrg_bf, wig_bf, brg_ref, big_ref, lam_ref)
            h = a * h + u
            rows = slice(t * batch, (t + 1) * batch)
            m_ref[rows, cols] = (h * yb_ref[rows, cols].astype(F32)).astype(BF16)
        hlast_ref[:, cols] = h
    for k in range(CONV_WIDTH - 1):
        conv_ref[k] = slab(steps + k, slice(None))


def _lru_sample(xb, yb, tile, steps, conv_state, h0, cw, cb, wrg, wig, brg, big, lam):
    batch = h0.shape[0]
    tok = pl.BlockSpec((TOK_TILE, D_MODEL), lambda i: (tile, 0))
    full = lambda a: pl.BlockSpec(a.shape, lambda i: (0,) * a.ndim)
    small = (conv_state, h0, cw, cb, wrg, wig, brg, big, lam)
    return pl.pallas_call(
        functools.partial(_lru_sample_kernel, steps=steps),
        grid=(1,),
        in_specs=[tok, tok] + [full(a) for a in small],
        out_specs=[
            pl.BlockSpec((TOK_TILE, D_MODEL), lambda i: (0, 0)),
            pl.BlockSpec((CONV_WIDTH - 1, batch, D_MODEL), lambda i: (0, 0, 0)),
            pl.BlockSpec((batch, D_MODEL), lambda i: (0, 0)),
        ],
        out_shape=[
            jax.ShapeDtypeStruct((TOK_TILE, D_MODEL), BF16),
            jax.ShapeDtypeStruct((CONV_WIDTH - 1, batch, D_MODEL), F32),
            jax.ShapeDtypeStruct((batch, D_MODEL), F32),
        ],
        scratch_shapes=[
            pltpu.VMEM((LRU_BLOCKS, LRU_BLOCK, LRU_BLOCK), BF16),
            pltpu.VMEM((LRU_BLOCKS, LRU_BLOCK, LRU_BLOCK), BF16),
        ],
        compiler_params=_params(("arbitrary",), 32),
        name="lru_sample",
    )(xb, yb, *small)


def _rel_bucket(dist):
    n = jnp.maximum(dist, 0)
    max_exact = N_BUCKETS // 2
    nf = jnp.maximum(n, 1).astype(F32)
    large = max_exact + (jnp.log(nf / max_exact) / math.log(MAX_DISTANCE / max_exact)
                         * (N_BUCKETS - max_exact)).astype(jnp.int32)
    large = jnp.minimum(large, N_BUCKETS - 1)
    return jnp.where(n < max_exact, n, large)


def _masked_buckets(dist):
    valid = (dist >= 0) & (dist < WINDOW)
    return jnp.where(valid, _rel_bucket(dist), -1).astype(jnp.int32)


def _build_bias(bucket, tab_ref, head):
    def body(bi, acc):
        return jnp.where(bucket == bi, tab_ref[bi * N_HEADS + head], acc)
    return lax.fori_loop(0, N_BUCKETS, body, jnp.full(bucket.shape, NEG_INF, F32))


def _softmax_pv(s, sink, v):
    m = jnp.maximum(jnp.max(s, axis=-1, keepdims=True), sink)
    p = jnp.exp(s - m)
    den = jnp.sum(p, axis=-1, keepdims=True) + jnp.exp(sink - m)
    return jnp.dot(p.astype(BF16), v, preferred_element_type=F32) / den


def _attn_prompt_kernel(q_ref, kvp_ref, kvc_ref, bucket_ref, tab_ref, sink_ref, o_ref, bias_s):
    b = pl.program_id(0)
    n = pl.program_id(1)

    @pl.when(jnp.logical_and(b == 0, n == 0))
    def _():
        bucket = bucket_ref[...]

        col = lax.broadcasted_iota(jnp.int32, (WINDOW, 2 * WINDOW), 1)

        def head_body(h, c):
            bias = _build_bias(bucket, tab_ref, h)
            sink = sink_ref[h]
            g = h // GROUP
            r0 = pl.multiple_of((h % GROUP) * WINDOW, WINDOW)
            bias_s[0, g, pl.ds(r0, WINDOW), :] = jnp.where(col == 0, sink, bias)
            bias_s[1, g, pl.ds(r0, WINDOW), :] = jnp.where(
                col == 0, sink, jnp.where(col < WINDOW, NEG_INF, bias))
            return c

        lax.fori_loop(0, N_HEADS, head_body, 0)

    first = (n == 0).astype(jnp.int32)
    row = lax.broadcasted_iota(jnp.int32, kvp_ref.shape, 0)
    kv_prev = jnp.where(row == 0, 0.0, kvp_ref[...])
    kv = jnp.concatenate([kv_prev, kvc_ref[...]], axis=0).astype(BF16)
    ones = jnp.ones((2 * WINDOW, HEAD_DIM), BF16)
    lane = lax.broadcasted_iota(jnp.int32, (WINDOW, 2 * HEAD_DIM), 1)
    for g in range(N_KV_HEADS):
        heads = range(g * GROUP, (g + 1) * GROUP)
        kg = kv[:, g * HEAD_DIM:(g + 1) * HEAD_DIM]
        vg = kv[:, KV_DIM + g * HEAD_DIM:KV_DIM + (g + 1) * HEAD_DIM]
        v_ext = jnp.concatenate([vg, ones], axis=1)
        qg = jnp.concatenate([q_ref[:, h * HEAD_DIM:(h + 1) * HEAD_DIM] for h in heads], axis=0)
        s = lax.dot_general(qg, kg, (((1,), (1,)), ((), ())), preferred_element_type=F32)
        s = s + bias_s[first, g]
        p = jnp.exp(s - jnp.max(s, axis=-1, keepdims=True)).astype(BF16)
        o_ext = jnp.dot(p, v_ext, preferred_element_type=F32)
        o_rot = pltpu.roll(o_ext, HEAD_DIM, axis=1)
        for pair in range(GROUP // 2):
            r0 = slice(2 * pair * WINDOW, (2 * pair + 1) * WINDOW)
            r1 = slice((2 * pair + 1) * WINDOW, (2 * pair + 2) * WINDOW)
            even = o_ext[r0] * (1.0 / o_rot[r0])
            odd = o_rot[r1] * (1.0 / o_ext[r1])
            c0 = (g * GROUP + 2 * pair) * HEAD_DIM
            o_ref[:, c0:c0 + 2 * HEAD_DIM] = jnp.where(lane < HEAD_DIM, even, odd).astype(BF16)


def _attn_prompt(q, kv, batch, seq, bucket, tab, sinks):
    nb = seq // WINDOW
    smem = pl.BlockSpec(memory_space=pltpu.SMEM)
    return pl.pallas_call(
        _attn_prompt_kernel,
        grid=(batch, nb),
        in_specs=[
            pl.BlockSpec((WINDOW, D_MODEL), lambda b, n: (b * nb + n, 0)),
            pl.BlockSpec((WINDOW, 2 * KV_DIM), lambda b, n: (jnp.maximum(b * nb + n - 1, 0), 0)),
            pl.BlockSpec((WINDOW, 2 * KV_DIM), lambda b, n: (b * nb + n, 0)),
            pl.BlockSpec((WINDOW, 2 * WINDOW), lambda b, n: (0, 0)),
            smem, smem,
        ],
        out_specs=pl.BlockSpec((WINDOW, D_MODEL), lambda b, n: (b * nb + n, 0)),
        out_shape=jax.ShapeDtypeStruct((batch * seq, D_MODEL), BF16),
        scratch_shapes=[pltpu.VMEM((2, N_KV_HEADS, GROUP * WINDOW, 2 * WINDOW), F32)],
        compiler_params=_params(("arbitrary", "arbitrary"), 32),
        name="attn_prompt",
    )(q, kv, kv, bucket, tab, sinks)


def _attn_sample_kernel(q_ref, k_ref, v_ref, bucket_ref, tab_ref, sink_ref, o_ref, bias_s):
    steps = q_ref.shape[0]

    @pl.when(pl.program_id(0) == 0)
    def _():
        bucket = bucket_ref[...]

        def head_body(h, c):
            bias_s[h] = _build_bias(bucket, tab_ref, h)
            return c

        lax.fori_loop(0, N_HEADS, head_body, 0)

    rows = lax.broadcasted_iota(jnp.int32, (GROUP * steps, 1), 0)
    for g in range(N_KV_HEADS):
        kg = k_ref[:, g * HEAD_DIM:(g + 1) * HEAD_DIM].astype(BF16)
        vg = v_ref[:, g * HEAD_DIM:(g + 1) * HEAD_DIM].astype(BF16)
        heads = range(g * GROUP, (g + 1) * GROUP)
        qg = jnp.concatenate([q_ref[:, h * HEAD_DIM:(h + 1) * HEAD_DIM] for h in heads], axis=0)
        bias = jnp.concatenate([bias_s[h] for h in heads], axis=0)
        sink = jnp.full((GROUP * steps, 1), sink_ref[g * GROUP], F32)
        for hh in range(1, GROUP):
            sink = jnp.where(rows >= hh * steps, sink_ref[g * GROUP + hh], sink)
        s = lax.dot_general(qg, kg, (((1,), (1,)), ((), ())), preferred_element_type=F32) + bias
        o = _softmax_pv(s, sink, vg)
        for hh, h in enumerate(heads):
            o_ref[:, h * HEAD_DIM:(h + 1) * HEAD_DIM] = o[hh * steps:(hh + 1) * steps].astype(BF16)


def _attn_sample(q, k_all, v_all, bucket, tab, sinks):
    batch, steps, _ = q.shape
    lk = k_all.shape[1]
    smem = pl.BlockSpec(memory_space=pltpu.SMEM)
    return pl.pallas_call(
        _attn_sample_kernel,
        grid=(batch,),
        in_specs=[
            pl.BlockSpec((None, steps, D_MODEL), lambda b: (b, 0, 0)),
            pl.BlockSpec((None, lk, KV_DIM), lambda b: (b, 0, 0)),
            pl.BlockSpec((None, lk, KV_DIM), lambda b: (b, 0, 0)),
            pl.BlockSpec((steps, lk), lambda b: (0, 0)),
            smem, smem,
        ],
        out_specs=pl.BlockSpec((None, steps, D_MODEL), lambda b: (b, 0, 0)),
        out_shape=jax.ShapeDtypeStruct((batch, steps, D_MODEL), BF16),
        scratch_shapes=[pltpu.VMEM((N_HEADS, steps, lk), F32)],
        compiler_params=_params(("arbitrary",), 32),
        name="attn_sample",
    )(q, k_all, v_all, bucket, tab, sinks)


def kernel(x_prompt, x_sample, state_conv, state_rnn, cache_k_win, cache_v_win, ln_g, ln_b, lru_w_x, lru_b_x, lru_w_y, lru_b_y, lru_conv_w, lru_conv_b, lru_w_rg, lru_b_rg, lru_w_ig, lru_b_ig, lru_lam, lru_w_out, lru_b_out, attn_w_kv, attn_w_q, attn_w_o, attn_sinks, rel_bias, moe_w_router, moe_b_router, moe_w_gate, moe_w_up, moe_w_down):
    bp, seq, _ = x_prompt.shape
    bs, steps, _ = x_sample.shape
    n_p = bp * seq
    n_s = bs * steps

    assert n_p % TOK_TILE == 0 and n_s <= TOK_TILE
    sample_tile = n_p // TOK_TILE

    def pad_tile(rows):
        return jnp.pad(rows, ((0, TOK_TILE - n_s), (0, 0)))

    x0 = (x_prompt.reshape(n_p, D_MODEL),
          pad_tile(x_sample.transpose(1, 0, 2).reshape(n_s, D_MODEL)))
    wr_t = moe_w_router.T
    br = moe_b_router.reshape(N_EXPERTS, 1)
    vec = lambda a: a.reshape(1, -1)

    xb = _linear(x0, lru_w_x, (0,), vec(lru_b_x[0]), F32, name="lru_in_x")
    yb = _linear(x0, lru_w_y, (0,), vec(lru_b_y[0]), BF16, act="gelu", name="lru_in_y")
    lru_args = (lru_conv_w[0], vec(lru_conv_b[0]), lru_w_rg[0], lru_w_ig[0],
                vec(lru_b_rg[0]), vec(lru_b_ig[0]), vec(lru_lam[0]))
    m_p, conv_p, rnn_p = _lru_prompt(xb, yb, bp, seq, *lru_args)
    m_s, conv_s, rnn_s = _lru_sample(xb, yb, sample_tile, steps,
                                     state_conv[0].transpose(1, 0, 2), state_rnn[0], *lru_args)
    x1, e_idx, gates = _proj_ln((m_p, m_s), lru_w_out, (0,), vec(lru_b_out[0]), x0,
                                vec(ln_g[0, 0]), vec(ln_b[0, 0]), wr_t, br, name="lru_out_ln")
    x2, x2_bf = _moe_block(x1, e_idx, gates, moe_w_gate, moe_w_up, moe_w_down, 0,
                           vec(ln_g[0, 1]), vec(ln_b[0, 1]), (F32, BF16))

    kv = _linear(x2_bf, attn_w_kv, (), jnp.zeros((1, 2 * KV_DIM), F32), F32, name="attn_kv")
    q = _linear(x2_bf, attn_w_q, (0,), jnp.zeros((1, D_MODEL), F32), BF16, scale=HEAD_DIM ** -0.5,
                name="attn_q")
    tab = rel_bias.reshape(-1)
    sinks = attn_sinks[0]
    qi = jnp.arange(WINDOW)[:, None]
    kj = jnp.arange(2 * WINDOW)[None, :]
    o_p = _attn_prompt(q, kv, bp, seq, _masked_buckets(qi + WINDOW - kj), tab, sinks)
    kv_s = kv[n_p:n_p + n_s].reshape(steps, bs, 2, KV_DIM).transpose(2, 1, 0, 3)
    k_all = jnp.concatenate([cache_k_win.reshape(bs, WINDOW, KV_DIM), kv_s[0]], axis=1)
    v_all = jnp.concatenate([cache_v_win.reshape(bs, WINDOW, KV_DIM), kv_s[1]], axis=1)
    dist_s = jnp.arange(steps)[:, None] + WINDOW - jnp.arange(WINDOW + steps)[None, :]
    q_s = q[n_p:n_p + n_s].reshape(steps, bs, D_MODEL).transpose(1, 0, 2)
    o_s = _attn_sample(q_s, k_all, v_all, _masked_buckets(dist_s), tab, sinks)
    o_s = pad_tile(o_s.transpose(1, 0, 2).reshape(n_s, D_MODEL))
    x3, e_idx, gates = _proj_ln((o_p, o_s), attn_w_o, (0,), jnp.zeros((1, D_MODEL), F32), x2,
                                vec(ln_g[1, 0]), vec(ln_b[1, 0]), wr_t, br, name="attn_out_ln")
    y_p, y_s = _moe_block(x3, e_idx, gates, moe_w_gate, moe_w_up, moe_w_down, 1,
                          vec(ln_g[1, 1]), vec(ln_b[1, 1]), (F32,), split=True)

    y_prompt = y_p.reshape(bp, seq, D_MODEL)
    y_sample = y_s[:n_s].reshape(steps, bs, D_MODEL).transpose(1, 0, 2)
    kv_p = jnp.stack([kv[(b + 1) * seq - WINDOW:(b + 1) * seq] for b in range(bp)])
    kv_p = kv_p.reshape(bp, WINDOW, 2, N_KV_HEADS, HEAD_DIM)
    k_win_s = k_all[:, steps:].reshape(bs, WINDOW, N_KV_HEADS, HEAD_DIM)
    v_win_s = v_all[:, steps:].reshape(bs, WINDOW, N_KV_HEADS, HEAD_DIM)
    return (y_prompt, y_sample,
            conv_p[None], rnn_p.reshape(1, bp, D_MODEL),
            kv_p[:, :, 0], kv_p[:, :, 1],
            conv_s.transpose(1, 0, 2)[None], rnn_s[None],
            k_win_s, v_win_s)
```

```python
import functools
import math

import jax
import jax.numpy as jnp
from jax import lax
from jax.experimental import pallas as pl
from jax.experimental.pallas import tpu as pltpu

D_MODEL = 2048
DEPTH = 2
LRU_BLOCKS = 8
LRU_BLOCK = D_MODEL // LRU_BLOCKS
CONV_WIDTH = 4
LRU_C = 8.0
N_HEADS = 32
HEAD_DIM = 64
N_KV_HEADS = 8
GROUP = N_HEADS // N_KV_HEADS
KV_DIM = N_KV_HEADS * HEAD_DIM
WINDOW = 128
N_BUCKETS = 32
MAX_DISTANCE = 128
N_EXPERTS = 16
N_GROUPS = 4
EXPERTS_PER_GROUP = N_EXPERTS // N_GROUPS
D_EXPERT = 1024
ALPHA = (2 * DEPTH) ** 0.25
LN_EPS = 1e-5

LANES = 128
MOE_TILE = 256
TOK_TILE = 256
DMA_UNROLL = 8
SCALAR_UNROLL = 32
PLAN_UNROLL = 4
CAST_ROWS = 256
BF16 = jnp.bfloat16
F32 = jnp.float32
NEG_INF = float("-inf")


def _params(sem, vmem_mb):
    return pltpu.CompilerParams(dimension_semantics=sem, vmem_limit_bytes=vmem_mb * 1024 * 1024)


def _cast_rows(src_ref, dst_ref):
    n = src_ref.shape[0] // CAST_ROWS

    def body(i, c):
        r = pl.multiple_of(i * CAST_ROWS, CAST_ROWS)
        dst_ref[pl.ds(r, CAST_ROWS), :] = src_ref[pl.ds(r, CAST_ROWS), :].astype(BF16)
        return c

    lax.fori_loop(0, n, body, 0)


def _layer_norm(z, g, b):
    mu = jnp.mean(z, axis=-1, keepdims=True)
    zc = z - mu
    var = jnp.mean(zc * zc, axis=-1, keepdims=True)
    return zc * lax.rsqrt(var + LN_EPS) * g + b


def _tok_operands(x):
    if isinstance(x, tuple):
        xp, xs = x
        d = xp.shape[1]
        last_p = xp.shape[0] // TOK_TILE - 1
        specs = [pl.BlockSpec((TOK_TILE, d), lambda i, *_: (jnp.minimum(i, last_p), 0)),
                 pl.BlockSpec((TOK_TILE, d), lambda i, *_: (0, 0))]
        return [xp, xs], specs, last_p + 2
    return [x], [pl.BlockSpec((TOK_TILE, x.shape[1]), lambda i, *_: (i, 0))], x.shape[0] // TOK_TILE


def _tok_load(refs):
    if len(refs) == 1:
        return refs[0][...]
    return jnp.where(pl.program_id(0) < pl.num_programs(0) - 1, refs[0][...], refs[1][...])


def _linear_kernel(*refs, n_x, act, scale):
    x_refs, (w_ref, b_ref, o_ref, wbf_ref) = refs[:n_x], refs[n_x:]

    @pl.when(pl.program_id(0) == 0)
    def _():
        _cast_rows(w_ref, wbf_ref)

    y = jnp.dot(_tok_load(x_refs).astype(BF16), wbf_ref[...], preferred_element_type=F32)
    y = y + b_ref[...]
    if act == "gelu":
        y = jax.nn.gelu(y)
    if scale != 1.0:
        y = y * scale
    o_ref[...] = y.astype(o_ref.dtype)


def _linear(x, w, w_index, b, out_dtype, *, name, act=None, scale=1.0):
    arrays, specs, nt = _tok_operands(x)
    k, nout = w.shape[-2:]
    w_block = (None,) * len(w_index) + (k, nout)
    return pl.pallas_call(
        functools.partial(_linear_kernel, n_x=len(arrays), act=act, scale=scale),
        grid=(nt,),
        in_specs=specs + [
            pl.BlockSpec(w_block, lambda i: w_index + (0, 0), pipeline_mode=pl.Buffered(1)),
            pl.BlockSpec((1, nout), lambda i: (0, 0)),
        ],
        out_specs=pl.BlockSpec((TOK_TILE, nout), lambda i: (i, 0)),
        out_shape=jax.ShapeDtypeStruct((nt * TOK_TILE, nout), out_dtype),
        scratch_shapes=[pltpu.VMEM((k, nout), BF16)],
        compiler_params=_params(("arbitrary",), 48),
        name=name,
    )(*arrays, w, b)


def _route(logits_t, b_router):
    aff = jax.nn.sigmoid(logits_t)
    sel = aff + b_router
    srow = [sel[e:e + 1, :] for e in range(N_EXPERTS)]
    arow = [aff[e:e + 1, :] for e in range(N_EXPERTS)]

    def top2_sum(v):
        pairs = [v[i] + v[j] for i in range(4) for j in range(i + 1, 4)]
        return functools.reduce(jnp.maximum, pairs)

    scores = [top2_sum(srow[4 * g:4 * g + 4]) for g in range(N_GROUPS)]
    best = scores[0]
    gi = jnp.zeros_like(best, dtype=jnp.int32)
    for g in range(1, N_GROUPS):
        upd = scores[g] > best
        best = jnp.where(upd, scores[g], best)
        gi = jnp.where(upd, g, gi)

    def pick_group(rows, j):
        out = rows[j]
        for g in range(1, N_GROUPS):
            out = jnp.where(gi == g, rows[4 * g + j], out)
        return out

    v = [pick_group(srow, j) for j in range(EXPERTS_PER_GROUP)]
    a = [pick_group(arow, j) for j in range(EXPERTS_PER_GROUP)]

    m1, i1 = v[0], jnp.zeros_like(gi)
    for j in range(1, EXPERTS_PER_GROUP):
        upd = v[j] > m1
        m1 = jnp.where(upd, v[j], m1)
        i1 = jnp.where(upd, j, i1)
    m2 = jnp.full_like(m1, NEG_INF)
    i2 = jnp.zeros_like(gi)
    for j in range(EXPERTS_PER_GROUP):
        cand = jnp.where(i1 == j, NEG_INF, v[j])
        upd = cand > m2
        m2 = jnp.where(upd, cand, m2)
        i2 = jnp.where(upd, j, i2)

    def pick_idx(rows, idx):
        out = rows[0]
        for j in range(1, EXPERTS_PER_GROUP):
            out = jnp.where(idx == j, rows[j], out)
        return out

    a1 = pick_idx(a, i1)
    a2 = pick_idx(a, i2)
    tot = a1 + a2
    e_idx = jnp.concatenate([gi * EXPERTS_PER_GROUP + i1, gi * EXPERTS_PER_GROUP + i2], axis=0)
    gates = jnp.concatenate([a1 / tot, a2 / tot], axis=0)
    return e_idx, gates


def _proj_ln_kernel(*refs, n_m, n_res):
    m_refs = refs[:n_m]
    w_ref, b_ref = refs[n_m:n_m + 2]
    res_refs = refs[n_m + 2:n_m + 2 + n_res]
    g_ref, beta_ref, wr_ref, br_ref, x_ref, e_ref, gate_ref, wbf_ref = refs[n_m + 2 + n_res:]

    @pl.when(pl.program_id(0) == 0)
    def _():
        _cast_rows(w_ref, wbf_ref)

    y = jnp.dot(_tok_load(m_refs), wbf_ref[...], preferred_element_type=F32) + b_ref[...]
    x = _layer_norm(ALPHA * _tok_load(res_refs) + y, g_ref[...], beta_ref[...])
    x_ref[...] = x
    logits_t = lax.dot_general(wr_ref[...].astype(BF16), x.astype(BF16),
                               (((1,), (1,)), ((), ())), preferred_element_type=F32)
    e_idx, gates = _route(logits_t, br_ref[...])
    e_ref[...] = e_idx
    gate_ref[...] = gates


def _proj_ln(m, w, w_index, b, res, g, beta, wr_t, br, *, name):
    m_arrays, m_specs, nt = _tok_operands(m)
    res_arrays, res_specs, _ = _tok_operands(res)
    k = w.shape[-2]
    tm = TOK_TILE
    n = nt * tm
    row = lambda i: (i, 0)
    const = lambda i: (0, 0)
    x, e_idx, gates = pl.pallas_call(
        functools.partial(_proj_ln_kernel, n_m=len(m_arrays), n_res=len(res_arrays)),
        grid=(nt,),
        in_specs=m_specs + [
            pl.BlockSpec((None,) * len(w_index) + (k, D_MODEL), lambda i: w_index + (0, 0),
                         pipeline_mode=pl.Buffered(1)),
            pl.BlockSpec((1, D_MODEL), const),
        ] + res_specs + [
            pl.BlockSpec((1, D_MODEL), const),
            pl.BlockSpec((1, D_MODEL), const),
            pl.BlockSpec((N_EXPERTS, D_MODEL), const),
            pl.BlockSpec((N_EXPERTS, 1), const),
        ],
        out_specs=[
            pl.BlockSpec((tm, D_MODEL), row),
            pl.BlockSpec((None, 2, tm), lambda i: (i, 0, 0)),
            pl.BlockSpec((None, 2, tm), lambda i: (i, 0, 0)),
        ],
        out_shape=[
            jax.ShapeDtypeStruct((n, D_MODEL), F32),
            jax.ShapeDtypeStruct((nt, 2, tm), jnp.int32),
            jax.ShapeDtypeStruct((nt, 2, tm), F32),
        ],
        scratch_shapes=[pltpu.VMEM((k, D_MODEL), BF16)],
        compiler_params=_params(("arbitrary",), 48),
        name=name,
    )(*m_arrays, w, b, *res_arrays, g, beta, wr_t, br)
    e_idx = e_idx.transpose(1, 0, 2).reshape(2, n)
    gates = gates.transpose(1, 0, 2).reshape(2, n)
    return x, e_idx, gates


def _plan_kernel(e_ref, pos_ref, meta_ref, rank_ref):
    nrow = e_ref.shape[0]
    ri = lax.broadcasted_iota(jnp.int32, (LANES, LANES), 0)
    ci = lax.broadcasted_iota(jnp.int32, (LANES, LANES), 1)
    tri = jnp.where(ri <= ci, 1.0, 0.0).astype(BF16)
    sub = lax.broadcasted_iota(jnp.int32, (N_EXPERTS, LANES), 0)

    def count_body(b, base):
        rows = [b * PLAN_UNROLL + u for u in range(PLAN_UNROLL)]
        onehots = [sub == e_ref[pl.ds(r, 1), :] for r in rows]
        locs = [jnp.dot(jnp.where(oh, 1.0, 0.0).astype(BF16), tri, preferred_element_type=F32)
                for oh in onehots]
        for r, onehot, loc in zip(rows, onehots, locs):
            rank_ref[pl.ds(r, 1), :] = jnp.sum(jnp.where(onehot, base + loc - 1.0, 0.0),
                                               axis=0, keepdims=True)
            base = base + jnp.broadcast_to(loc[:, LANES - 1:LANES], (N_EXPERTS, LANES))
        return base

    count = lax.fori_loop(0, nrow // PLAN_UNROLL, count_body, jnp.zeros((N_EXPERTS, LANES), F32))
    ntile = jnp.floor((count + (MOE_TILE - 1.0)) * (1.0 / MOE_TILE))
    offs = []
    acc = jnp.zeros((1, LANES), F32)
    for e in range(N_EXPERTS):
        offs.append(acc)
        acc = acc + ntile[e:e + 1, :]
    tile_off = jnp.concatenate(offs, axis=0)
    tile_end = tile_off + ntile
    lane = lax.broadcasted_iota(jnp.int32, (N_EXPERTS, LANES), 1).astype(F32)
    tile_expert = jnp.sum(jnp.where(tile_end <= lane, 1.0, 0.0), axis=0, keepdims=True)
    tile_expert = jnp.minimum(tile_expert, N_EXPERTS - 1.0)
    meta = jnp.concatenate([tile_expert, acc, jnp.zeros((6, LANES), F32)], axis=0)
    meta_ref[...] = meta.astype(jnp.int32)
    row_off = tile_off * float(MOE_TILE)

    def pos_body(r, c):
        onehot = sub == e_ref[pl.ds(r, 1), :]
        p = jnp.sum(jnp.where(onehot, row_off, 0.0), axis=0, keepdims=True) + rank_ref[pl.ds(r, 1), :]
        pos_ref[pl.ds(r, 1), :] = p.astype(jnp.int32)
        return c

    lax.fori_loop(0, nrow, pos_body, 0)


def _plan(e_idx, *, name):
    n2 = e_idx.shape[0] * e_idx.shape[1]
    assert n2 % (LANES * PLAN_UNROLL) == 0
    e2d = e_idx.reshape(n2 // LANES, LANES)
    pos, meta = pl.pallas_call(
        _plan_kernel,
        out_shape=[jax.ShapeDtypeStruct(e2d.shape, jnp.int32),
                   jax.ShapeDtypeStruct((8, LANES), jnp.int32)],
        scratch_shapes=[pltpu.VMEM(e2d.shape, F32)],
        name=name,
    )(e2d)
    return pos.reshape(n2), meta[0], meta[1, :1]


def _invert_kernel(pos_ref, pair_ref):
    n_rows = pair_ref.shape[0]
    n_pairs = pos_ref.shape[0]

    def fill_body(b, c):
        for u in range(SCALAR_UNROLL):
            pair_ref[b * SCALAR_UNROLL + u] = -1
        return c

    def pair_body(b, c):
        rows = [pos_ref[b * SCALAR_UNROLL + u] for u in range(SCALAR_UNROLL)]
        for u in range(SCALAR_UNROLL):
            pair_ref[rows[u]] = b * SCALAR_UNROLL + u
        return c

    lax.fori_loop(0, n_rows // SCALAR_UNROLL, fill_body, 0)
    lax.fori_loop(0, n_pairs // SCALAR_UNROLL, pair_body, 0)


def _invert(pos, n_rows, *, name):
    return pl.pallas_call(
        _invert_kernel,
        grid_spec=pltpu.PrefetchScalarGridSpec(
            num_scalar_prefetch=1,
            grid=(1,),
            in_specs=[],
            out_specs=pl.BlockSpec(memory_space=pltpu.SMEM),
        ),
        out_shape=jax.ShapeDtypeStruct((n_rows,), jnp.int32),
        name=name,
    )(pos)


def _expert_changed(te_ref, i):
    return jnp.logical_or(i == 0, te_ref[i] != te_ref[jnp.maximum(i - 1, 0)])


def _moe_up_kernel(pair_ref, te_ref, nu_ref, x_hbm, wg_ref, wu_ref, h_ref,
                   wg_bf, wu_bf, x_f32, x_bf, sem, *, n_tok, n_tiles):
    i = pl.program_id(0)
    nu = nu_ref[0]

    def row_copy(tile, r):
        p = pair_ref[tile * MOE_TILE + r]
        tok = jnp.where(p >= n_tok, p - n_tok, jnp.maximum(p, 0))
        return pltpu.make_async_copy(x_hbm.at[pl.ds(tok, 1), :], x_f32.at[pl.ds(r, 1), :], sem)

    def wait_tile():
        pltpu.make_async_copy(x_hbm.at[pl.ds(0, MOE_TILE), :], x_f32, sem).wait()

    @pl.when(i == 0)
    def _():
        def body(rb, c):
            for u in range(DMA_UNROLL):
                row_copy(0, rb * DMA_UNROLL + u).start()
            return c
        lax.fori_loop(0, MOE_TILE // DMA_UNROLL, body, 0)

    @pl.when(i <= nu)
    def _():
        wait_tile()

    @pl.when(i < nu)
    def _():
        @pl.when(_expert_changed(te_ref, i))
        def _():
            _cast_rows(wg_ref, wg_bf)
            _cast_rows(wu_ref, wu_bf)

        x_bf[...] = x_f32[...].astype(BF16)
        next_tile = jnp.minimum(i + 1, n_tiles - 1)
        for r in range(MOE_TILE):
            row_copy(next_tile, r).start(priority=r % 2)
        x = x_bf[...]
        a = jnp.dot(x, wg_bf[...], preferred_element_type=F32)
        b = jnp.dot(x, wu_bf[...], preferred_element_type=F32)
        h_ref[...] = (jax.nn.silu(a) * b).astype(BF16)

    @pl.when(jnp.logical_and(i == n_tiles - 1, i < nu))
    def _():
        wait_tile()

    @pl.when(i >= nu)
    def _():
        h_ref[...] = jnp.zeros_like(h_ref)


def _moe_down_kernel(pair_ref, te_ref, nu_ref, h_ref, wd_ref, out_hbm,
                     wd_bf, ya, yb, sems, tsem, *, n_tok, n_tiles):
    i = pl.program_id(0)
    nu = nu_ref[0]
    bufs = (ya, yb)
    trash = 2 * n_tok

    def row_copy(tile, r, buf, sem):
        p = pair_ref[tile * MOE_TILE + r]
        dst = jnp.where(p < 0, trash + r, p)
        return pltpu.make_async_copy(buf.at[pl.ds(r, 1), :], out_hbm.at[pl.ds(dst, 1), :], sem)

    def wait_tile(buf, sem):
        pltpu.make_async_copy(buf, out_hbm.at[pl.ds(0, MOE_TILE), :], sem).wait()

    @pl.when(i == 0)
    def _():
        yb[...] = jnp.zeros_like(yb)
        fill = pltpu.make_async_copy(yb, out_hbm.at[pl.ds(trash, MOE_TILE), :], tsem)
        fill.start()
        fill.wait()

    for parity in range(2):
        cur, prev = bufs[parity], bufs[1 - parity]
        cur_sem, prev_sem = sems.at[parity], sems.at[1 - parity]
        mine = i % 2 == parity

        @pl.when(jnp.logical_and(mine, jnp.logical_and(i >= 1, i - 1 <= nu)))
        def _():
            wait_tile(cur, cur_sem)

        @pl.when(jnp.logical_and(mine, i < nu))
        def _():
            @pl.when(_expert_changed(te_ref, i))
            def _():
                _cast_rows(wd_ref, wd_bf)

            prev_tile = jnp.maximum(i - 1, 0)
            for r in range(MOE_TILE):
                row_copy(prev_tile, r, prev, prev_sem).start(priority=r % 2)
            cur[...] = jnp.dot(h_ref[...], wd_bf[...], preferred_element_type=F32)

        @pl.when(jnp.logical_and(mine, i == nu))
        def _():
            def body(rb, c):
                for u in range(DMA_UNROLL):
                    row_copy(i - 1, rb * DMA_UNROLL + u, prev, prev_sem).start()
                return c
            lax.fori_loop(0, MOE_TILE // DMA_UNROLL, body, 0)

            @pl.when(i == n_tiles)
            def _():
                wait_tile(prev, prev_sem)


def _moe_ffn(x, pair, tile_expert, n_used, w_gate, w_up, w_down, layer):
    n_tok = x.shape[0]
    n_tiles = pair.shape[0] // MOE_TILE

    def tile(i, pr, te, nu):
        return (jnp.minimum(i, nu[0] - 1), 0)

    def expert(i, pr, te, nu):
        return (layer, te[jnp.minimum(i, nu[0] - 1)], 0, 0)

    def weight_spec(k, n):
        return pl.BlockSpec((None, None, k, n), expert)

    row_bufs = [pltpu.VMEM((MOE_TILE, D_MODEL), F32), pltpu.VMEM((MOE_TILE, D_MODEL), F32),
                pltpu.SemaphoreType.DMA((2,))]
    h = pl.pallas_call(
        functools.partial(_moe_up_kernel, n_tok=n_tok, n_tiles=n_tiles),
        grid_spec=pltpu.PrefetchScalarGridSpec(
            num_scalar_prefetch=3,
            grid=(n_tiles,),
            in_specs=[pl.BlockSpec(memory_space=pl.ANY),
                      weight_spec(D_MODEL, D_EXPERT),
                      weight_spec(D_MODEL, D_EXPERT)],
            out_specs=pl.BlockSpec((MOE_TILE, D_EXPERT), lambda i, pr, te, nu: (i, 0)),
            scratch_shapes=[pltpu.VMEM((D_MODEL, D_EXPERT), BF16),
                            pltpu.VMEM((D_MODEL, D_EXPERT), BF16),
                            pltpu.VMEM((MOE_TILE, D_MODEL), F32),
                            pltpu.VMEM((MOE_TILE, D_MODEL), BF16),
                            pltpu.SemaphoreType.DMA(())],
        ),
        out_shape=jax.ShapeDtypeStruct((n_tiles * MOE_TILE, D_EXPERT), BF16),
        compiler_params=_params(("arbitrary",), 56),
        name=f"moe_up_{layer}",
    )(pair, tile_expert, n_used, x, w_gate, w_up)
    return pl.pallas_call(
        functools.partial(_moe_down_kernel, n_tok=n_tok, n_tiles=n_tiles),
        grid_spec=pltpu.PrefetchScalarGridSpec(
            num_scalar_prefetch=3,
            grid=(n_tiles + 1,),
            in_specs=[pl.BlockSpec((MOE_TILE, D_EXPERT), tile),
                      weight_spec(D_EXPERT, D_MODEL)],
            out_specs=pl.BlockSpec(memory_space=pl.ANY),
            scratch_shapes=[pltpu.VMEM((D_EXPERT, D_MODEL), BF16)] + row_bufs
            + [pltpu.SemaphoreType.DMA(())],
        ),
        out_shape=jax.ShapeDtypeStruct((2 * n_tok + MOE_TILE, D_MODEL), F32),
        compiler_params=_params(("arbitrary",), 40),
        name=f"moe_down_{layer}",
    )(pair, tile_expert, n_used, h, w_down)


def _combine_ln_kernel(y0_ref, y1_ref, res_ref, gate_ref, g_ref, beta_ref, *outs, split):
    i = pl.program_id(0)
    gate = gate_ref[...]
    ffn = gate[:, 0:1] * y0_ref[...] + gate[:, 1:2] * y1_ref[...]
    x = _layer_norm(ALPHA * res_ref[...] + ffn, g_ref[...], beta_ref[...])
    if split:
        prompt_ref, sample_ref = outs
        is_sample = i == pl.num_programs(0) - 1

        @pl.when(jnp.logical_not(is_sample))
        def _():
            prompt_ref[...] = x

        @pl.when(is_sample)
        def _():
            sample_ref[...] = x
    else:
        for o_ref in outs:
            o_ref[...] = x.astype(o_ref.dtype)


def _combine_ln(ys, res, gates_col, g, beta, out_dtypes, *, split, name):
    n = res.shape[0]
    tm = TOK_TILE
    nt = n // tm
    row = lambda i: (i, 0)
    const = lambda i: (0, 0)
    if split:
        out_specs = [pl.BlockSpec((tm, D_MODEL), lambda i: (jnp.minimum(i, nt - 2), 0)),
                     pl.BlockSpec((tm, D_MODEL), const)]
        out_shape = [jax.ShapeDtypeStruct((n - tm, D_MODEL), F32),
                     jax.ShapeDtypeStruct((tm, D_MODEL), F32)]
    else:
        out_specs = [pl.BlockSpec((tm, D_MODEL), row) for _ in out_dtypes]
        out_shape = [jax.ShapeDtypeStruct((n, D_MODEL), dt) for dt in out_dtypes]
    return pl.pallas_call(
        functools.partial(_combine_ln_kernel, split=split),
        grid=(nt,),
        in_specs=[pl.BlockSpec((tm, D_MODEL), row),
                  pl.BlockSpec((tm, D_MODEL), lambda i: (i + nt, 0)),
                  pl.BlockSpec((tm, D_MODEL), row),
                  pl.BlockSpec((tm, 2), row),
                  pl.BlockSpec((1, D_MODEL), const),
                  pl.BlockSpec((1, D_MODEL), const)],
        out_specs=out_specs,
        out_shape=out_shape,
        compiler_params=_params(("arbitrary",), 40),
        name=name,
    )(ys, ys, res, gates_col, g, beta)


def _moe_block(x, e_idx, gates, w_gate, w_up, w_down, layer, g, beta, out_dtypes, split=False):
    n = x.shape[0]
    n_tiles = -(-(2 * n + N_EXPERTS * (MOE_TILE - 1)) // MOE_TILE)
    pos, tile_expert, n_used = _plan(e_idx, name=f"moe_plan_{layer}")
    pair = _invert(pos, n_tiles * MOE_TILE, name=f"moe_invert_{layer}")
    ys = _moe_ffn(x, pair, tile_expert, n_used, w_gate, w_up, w_down, layer)
    return _combine_ln(ys, x, gates.T, g, beta, out_dtypes, split=split,
                       name=f"moe_combine_{layer}")


def _sigmoid(x):
    return 0.5 * jnp.tanh(0.5 * x) + 0.5


def _log_sigmoid(x):
    return -(jnp.maximum(-x, 0.0) + jnp.log1p(jnp.exp(-jnp.abs(x))))


def _lru_gate_block(xc, n, wrg_bf, wig_bf, brg_ref, big_ref, lam_ref):
    cols = slice(n * LRU_BLOCK, (n + 1) * LRU_BLOCK)
    xb = xc.astype(BF16)
    r = _sigmoid(jnp.dot(xb, wrg_bf[n], preferred_element_type=F32) + brg_ref[:, cols])
    i = _sigmoid(jnp.dot(xb, wig_bf[n], preferred_element_type=F32) + big_ref[:, cols])
    log_a = LRU_C * r * _log_sigmoid(lam_ref[:, cols])
    a = jnp.exp(log_a)
    u = xc * i * jnp.sqrt(-jnp.tanh(log_a) * (a * a + 1.0))
    return a, u


def _cast_gate_weights(wrg_ref, wig_ref, wrg_bf, wig_bf):
    for n in range(LRU_BLOCKS):
        wrg_bf[n] = wrg_ref[n].astype(BF16)
        wig_bf[n] = wig_ref[n].astype(BF16)


def _lru_prompt_kernel(xb_ref, yb_ref, cw_ref, cb_ref, wrg_ref, wig_ref, brg_ref, big_ref, lam_ref,
                       m_ref, conv_ref, hlast_ref, xpad, a_s, u_s, h_s, wrg_bf, wig_bf):
    b = pl.program_id(0)
    j = pl.program_id(1)
    tt = xb_ref.shape[0]

    @pl.when(jnp.logical_and(b == 0, j == 0))
    def _():
        _cast_gate_weights(wrg_ref, wig_ref, wrg_bf, wig_bf)

    @pl.when(j == 0)
    def _():
        xpad[0:8, :] = jnp.zeros((8, D_MODEL), F32)
        h_s[...] = jnp.zeros_like(h_s)

    xpad[8:8 + tt, :] = xb_ref[...]
    for n in range(LRU_BLOCKS):
        cols = slice(n * LRU_BLOCK, (n + 1) * LRU_BLOCK)
        xc = cb_ref[:, cols] + cw_ref[0:1, cols] * xpad[5:5 + tt, cols]
        for k in range(1, CONV_WIDTH):
            xc = xc + cw_ref[k:k + 1, cols] * xpad[5 + k:5 + k + tt, cols]
        a, u = _lru_gate_block(xc, n, wrg_bf, wig_bf, brg_ref, big_ref, lam_ref)
        a_s[:, cols] = a
        u_s[:, cols] = u

    def scan_body(gidx, h):
        base = pl.multiple_of(gidx * 8, 8)
        a8 = a_s[pl.ds(base, 8), :]
        u8 = u_s[pl.ds(base, 8), :]
        rows = []
        for s in range(8):
            h = a8[s:s + 1, :] * h + u8[s:s + 1, :]
            rows.append(h)
        a_s[pl.ds(base, 8), :] = jnp.concatenate(rows, axis=0)
        return h

    h = lax.fori_loop(0, tt // 8, scan_body, h_s[...])
    h_s[...] = h
    m_ref[...] = (a_s[...] * yb_ref[...].astype(F32)).astype(BF16)
    xpad[0:8, :] = xpad[tt:tt + 8, :]

    @pl.when(j == pl.num_programs(1) - 1)
    def _():
        conv_ref[...] = xpad[5:8, :]
        hlast_ref[...] = h


def _lru_prompt(xb, yb, batch, seq, cw, cb, wrg, wig, brg, big, lam, *, tt=256):
    nj = seq // tt
    row = lambda b, j: (b * nj + j, 0)
    const2 = lambda b, j: (0, 0)
    const3 = lambda b, j: (0, 0, 0)
    return pl.pallas_call(
        _lru_prompt_kernel,
        grid=(batch, nj),
        in_specs=[
            pl.BlockSpec((tt, D_MODEL), row),
            pl.BlockSpec((tt, D_MODEL), row),
            pl.BlockSpec((CONV_WIDTH, D_MODEL), const2),
            pl.BlockSpec((1, D_MODEL), const2),
            pl.BlockSpec((LRU_BLOCKS, LRU_BLOCK, LRU_BLOCK), const3),
            pl.BlockSpec((LRU_BLOCKS, LRU_BLOCK, LRU_BLOCK), const3),
            pl.BlockSpec((1, D_MODEL), const2),
            pl.BlockSpec((1, D_MODEL), const2),
            pl.BlockSpec((1, D_MODEL), const2),
        ],
        out_specs=[
            pl.BlockSpec((tt, D_MODEL), row),
            pl.BlockSpec((None, CONV_WIDTH - 1, D_MODEL), lambda b, j: (b, 0, 0)),
            pl.BlockSpec((None, 1, D_MODEL), lambda b, j: (b, 0, 0)),
        ],
        out_shape=[
            jax.ShapeDtypeStruct((batch * seq, D_MODEL), BF16),
            jax.ShapeDtypeStruct((batch, CONV_WIDTH - 1, D_MODEL), F32),
            jax.ShapeDtypeStruct((batch, 1, D_MODEL), F32),
        ],
        scratch_shapes=[
            pltpu.VMEM((tt + 8, D_MODEL), F32),
            pltpu.VMEM((tt, D_MODEL), F32),
            pltpu.VMEM((tt, D_MODEL), F32),
            pltpu.VMEM((1, D_MODEL), F32),
            pltpu.VMEM((LRU_BLOCKS, LRU_BLOCK, LRU_BLOCK), BF16),
            pltpu.VMEM((LRU_BLOCKS, LRU_BLOCK, LRU_BLOCK), BF16),
        ],
        compiler_params=_params(("arbitrary", "arbitrary"), 40),
        name="lru_prompt",
    )(xb, yb, cw, cb, wrg, wig, brg, big, lam)


def _lru_sample_kernel(xb_ref, yb_ref, cs_ref, h0_ref, cw_ref, cb_ref, wrg_ref, wig_ref,
                       brg_ref, big_ref, lam_ref, m_ref, conv_ref, hlast_ref, wrg_bf, wig_bf, *, steps):
    batch = h0_ref.shape[0]
    _cast_gate_weights(wrg_ref, wig_ref, wrg_bf, wig_bf)
    m_ref[steps * batch:, :] = jnp.zeros((m_ref.shape[0] - steps * batch, D_MODEL), BF16)

    def slab(t, cols):
        if t < CONV_WIDTH - 1:
            return cs_ref[t, :, cols]
        t -= CONV_WIDTH - 1
        return xb_ref[t * batch:(t + 1) * batch, cols]

    for n in range(LRU_BLOCKS):
        cols = slice(n * LRU_BLOCK, (n + 1) * LRU_BLOCK)
        h = h0_ref[:, cols]
        for t in range(steps):
            xc = cb_ref[:, cols] + cw_ref[0:1, cols] * slab(t, cols)
            for k in range(1, CONV_WIDTH):
                xc = xc + cw_ref[k:k + 1, cols] * slab(t + k, cols)
            a, u = _lru_gate_block(xc, n, wrg_bf, wig_bf, brg_ref, big_ref, lam_ref)
            h = a * h + u
            rows = slice(t * batch, (t + 1) * batch)
            m_ref[rows, cols] = (h * yb_ref[rows, cols].astype(F32)).astype(BF16)
        hlast_ref[:, cols] = h
    for k in range(CONV_WIDTH - 1):
        conv_ref[k] = slab(steps + k, slice(None))


def _lru_sample(xb, yb, tile, steps, conv_state, h0, cw, cb, wrg, wig, brg, big, lam):
    batch = h0.shape[0]
    tok = pl.BlockSpec((TOK_TILE, D_MODEL), lambda i: (tile, 0))
    full = lambda a: pl.BlockSpec(a.shape, lambda i: (0,) * a.ndim)
    small = (conv_state, h0, cw, cb, wrg, wig, brg, big, lam)
    return pl.pallas_call(
        functools.partial(_lru_sample_kernel, steps=steps),
        grid=(1,),
        in_specs=[tok, tok] + [full(a) for a in small],
        out_specs=[
            pl.BlockSpec((TOK_TILE, D_MODEL), lambda i: (0, 0)),
            pl.BlockSpec((CONV_WIDTH - 1, batch, D_MODEL), lambda i: (0, 0, 0)),
            pl.BlockSpec((batch, D_MODEL), lambda i: (0, 0)),
        ],
        out_shape=[
            jax.ShapeDtypeStruct((TOK_TILE, D_MODEL), BF16),
            jax.ShapeDtypeStruct((CONV_WIDTH - 1, batch, D_MODEL), F32),
            jax.ShapeDtypeStruct((batch, D_MODEL), F32),
        ],
        scratch_shapes=[
            pltpu.VMEM((LRU_BLOCKS, LRU_BLOCK, LRU_BLOCK), BF16),
            pltpu.VMEM((LRU_BLOCKS, LRU_BLOCK, LRU_BLOCK), BF16),
        ],
        compiler_params=_params(("arbitrary",), 32),
        name="lru_sample",
    )(xb, yb, *small)


def _rel_bucket(dist):
    n = jnp.maximum(dist, 0)
    max_exact = N_BUCKETS // 2
    nf = jnp.maximum(n, 1).astype(F32)
    large = max_exact + (jnp.log(nf / max_exact) / math.log(MAX_DISTANCE / max_exact)
                         * (N_BUCKETS - max_exact)).astype(jnp.int32)
    large = jnp.minimum(large, N_BUCKETS - 1)
    return jnp.where(n < max_exact, n, large)


def _masked_buckets(dist):
    valid = (dist >= 0) & (dist < WINDOW)
    return jnp.where(valid, _rel_bucket(dist), -1).astype(jnp.int32)


def _build_bias(bucket, tab_ref, head):
    def body(bi, acc):
        return jnp.where(bucket == bi, tab_ref[bi * N_HEADS + head], acc)
    return lax.fori_loop(0, N_BUCKETS, body, jnp.full(bucket.shape, NEG_INF, F32))


def _softmax_pv(s, sink, v):
    m = jnp.maximum(jnp.max(s, axis=-1, keepdims=True), sink)
    p = jnp.exp(s - m)
    den = jnp.sum(p, axis=-1, keepdims=True) + jnp.exp(sink - m)
    return jnp.dot(p.astype(BF16), v, preferred_element_type=F32) / den


def _attn_prompt_kernel(q_ref, kvp_ref, kvc_ref, bucket_ref, tab_ref, sink_ref, o_ref, bias_s):
    b = pl.program_id(0)
    n = pl.program_id(1)

    @pl.when(jnp.logical_and(b == 0, n == 0))
    def _():
        bucket = bucket_ref[...]

        col = lax.broadcasted_iota(jnp.int32, (WINDOW, 2 * WINDOW), 1)

        def head_body(h, c):
            bias = _build_bias(bucket, tab_ref, h)
            sink = sink_ref[h]
            g = h // GROUP
            r0 = pl.multiple_of((h % GROUP) * WINDOW, WINDOW)
            bias_s[0, g, pl.ds(r0, WINDOW), :] = jnp.where(col == 0, sink, bias)
            bias_s[1, g, pl.ds(r0, WINDOW), :] = jnp.where(
                col == 0, sink, jnp.where(col < WINDOW, NEG_INF, bias))
            return c

        lax.fori_loop(0, N_HEADS, head_body, 0)

    first = (n == 0).astype(jnp.int32)
    row = lax.broadcasted_iota(jnp.int32, kvp_ref.shape, 0)
    kv_prev = jnp.where(row == 0, 0.0, kvp_ref[...])
    kv = jnp.concatenate([kv_prev, kvc_ref[...]], axis=0).astype(BF16)
    ones = jnp.ones((2 * WINDOW, HEAD_DIM), BF16)
    lane = lax.broadcasted_iota(jnp.int32, (WINDOW, 2 * HEAD_DIM), 1)
    for g in range(N_KV_HEADS):
        heads = range(g * GROUP, (g + 1) * GROUP)
        kg = kv[:, g * HEAD_DIM:(g + 1) * HEAD_DIM]
        vg = kv[:, KV_DIM + g * HEAD_DIM:KV_DIM + (g + 1) * HEAD_DIM]
        v_ext = jnp.concatenate([vg, ones], axis=1)
        qg = jnp.concatenate([q_ref[:, h * HEAD_DIM:(h + 1) * HEAD_DIM] for h in heads], axis=0)
        s = lax.dot_general(qg, kg, (((1,), (1,)), ((), ())), preferred_element_type=F32)
        s = s + bias_s[first, g]
        p = jnp.exp(s - jnp.max(s, axis=-1, keepdims=True)).astype(BF16)
        o_ext = jnp.dot(p, v_ext, preferred_element_type=F32)
        o_rot = pltpu.roll(o_ext, HEAD_DIM, axis=1)
        for pair in range(GROUP // 2):
            r0 = slice(2 * pair * WINDOW, (2 * pair + 1) * WINDOW)
            r1 = slice((2 * pair + 1) * WINDOW, (2 * pair + 2) * WINDOW)
            even = o_ext[r0] * (1.0 / o_rot[r0])
            odd = o_rot[r1] * (1.0 / o_ext[r1])
            c0 = (g * GROUP + 2 * pair) * HEAD_DIM
            o_ref[:, c0:c0 + 2 * HEAD_DIM] = jnp.where(lane < HEAD_DIM, even, odd).astype(BF16)


def _attn_prompt(q, kv, batch, seq, bucket, tab, sinks):
    nb = seq // WINDOW
    smem = pl.BlockSpec(memory_space=pltpu.SMEM)
    return pl.pallas_call(
        _attn_prompt_kernel,
        grid=(batch, nb),
        in_specs=[
            pl.BlockSpec((WINDOW, D_MODEL), lambda b, n: (b * nb + n, 0)),
            pl.BlockSpec((WINDOW, 2 * KV_DIM), lambda b, n: (jnp.maximum(b * nb + n - 1, 0), 0)),
            pl.BlockSpec((WINDOW, 2 * KV_DIM), lambda b, n: (b * nb + n, 0)),
            pl.BlockSpec((WINDOW, 2 * WINDOW), lambda b, n: (0, 0)),
            smem, smem,
        ],
        out_specs=pl.BlockSpec((WINDOW, D_MODEL), lambda b, n: (b * nb + n, 0)),
        out_shape=jax.ShapeDtypeStruct((batch * seq, D_MODEL), BF16),
        scratch_shapes=[pltpu.VMEM((2, N_KV_HEADS, GROUP * WINDOW, 2 * WINDOW), F32)],
        compiler_params=_params(("arbitrary", "arbitrary"), 32),
        name="attn_prompt",
    )(q, kv, kv, bucket, tab, sinks)


def _attn_sample_kernel(q_ref, k_ref, v_ref, bucket_ref, tab_ref, sink_ref, o_ref, bias_s):
    steps = q_ref.shape[0]

    @pl.when(pl.program_id(0) == 0)
    def _():
        bucket = bucket_ref[...]

        def head_body(h, c):
            bias_s[h] = _build_bias(bucket, tab_ref, h)
            return c

        lax.fori_loop(0, N_HEADS, head_body, 0)

    rows = lax.broadcasted_iota(jnp.int32, (GROUP * steps, 1), 0)
    for g in range(N_KV_HEADS):
        kg = k_ref[:, g * HEAD_DIM:(g + 1) * HEAD_DIM].astype(BF16)
        vg = v_ref[:, g * HEAD_DIM:(g + 1) * HEAD_DIM].astype(BF16)
        heads = range(g * GROUP, (g + 1) * GROUP)
        qg = jnp.concatenate([q_ref[:, h * HEAD_DIM:(h + 1) * HEAD_DIM] for h in heads], axis=0)
        bias = jnp.concatenate([bias_s[h] for h in heads], axis=0)
        sink = jnp.full((GROUP * steps, 1), sink_ref[g * GROUP], F32)
        for hh in range(1, GROUP):
            sink = jnp.where(rows >= hh * steps, sink_ref[g * GROUP + hh], sink)
        s = lax.dot_general(qg, kg, (((1,), (1,)), ((), ())), preferred_element_type=F32) + bias
        o = _softmax_pv(s, sink, vg)
        for hh, h in enumerate(heads):
            o_ref[:, h * HEAD_DIM:(h + 1) * HEAD_DIM] = o[hh * steps:(hh + 1) * steps].astype(BF16)


def _attn_sample(q, k_all, v_all, bucket, tab, sinks):
    batch, steps, _ = q.shape
    lk = k_all.shape[1]
    smem = pl.BlockSpec(memory_space=pltpu.SMEM)
    return pl.pallas_call(
        _attn_sample_kernel,
        grid=(batch,),
        in_specs=[
            pl.BlockSpec((None, steps, D_MODEL), lambda b: (b, 0, 0)),
            pl.BlockSpec((None, lk, KV_DIM), lambda b: (b, 0, 0)),
            pl.BlockSpec((None, lk, KV_DIM), lambda b: (b, 0, 0)),
            pl.BlockSpec((steps, lk), lambda b: (0, 0)),
            smem, smem,
        ],
        out_specs=pl.BlockSpec((None, steps, D_MODEL), lambda b: (b, 0, 0)),
        out_shape=jax.ShapeDtypeStruct((batch, steps, D_MODEL), BF16),
        scratch_shapes=[pltpu.VMEM((N_HEADS, steps, lk), F32)],
        compiler_params=_params(("arbitrary",), 32),
        name="attn_sample",
    )(q, k_all, v_all, bucket, tab, sinks)


def kernel(x_prompt, x_sample, state_conv, state_rnn, cache_k_win, cache_v_win, ln_g, ln_b, lru_w_x, lru_b_x, lru_w_y, lru_b_y, lru_conv_w, lru_conv_b, lru_w_rg, lru_b_rg, lru_w_ig, lru_b_ig, lru_lam, lru_w_out, lru_b_out, attn_w_kv, attn_w_q, attn_w_o, attn_sinks, rel_bias, moe_w_router, moe_b_router, moe_w_gate, moe_w_up, moe_w_down):
    bp, seq, _ = x_prompt.shape
    bs, steps, _ = x_sample.shape
    n_p = bp * seq
    n_s = bs * steps

    assert n_p % TOK_TILE == 0 and n_s <= TOK_TILE
    sample_tile = n_p // TOK_TILE

    def pad_tile(rows):
        return jnp.pad(rows, ((0, TOK_TILE - n_s), (0, 0)))

    x0 = (x_prompt.reshape(n_p, D_MODEL),
          pad_tile(x_sample.transpose(1, 0, 2).reshape(n_s, D_MODEL)))
    wr_t = moe_w_router.T
    br = moe_b_router.reshape(N_EXPERTS, 1)
    vec = lambda a: a.reshape(1, -1)

    xb = _linear(x0, lru_w_x, (0,), vec(lru_b_x[0]), F32, name="lru_in_x")
    yb = _linear(x0, lru_w_y, (0,), vec(lru_b_y[0]), BF16, act="gelu", name="lru_in_y")
    lru_args = (lru_conv_w[0], vec(lru_conv_b[0]), lru_w_rg[0], lru_w_ig[0],
                vec(lru_b_rg[0]), vec(lru_b_ig[0]), vec(lru_lam[0]))
    m_p, conv_p, rnn_p = _lru_prompt(xb, yb, bp, seq, *lru_args)
    m_s, conv_s, rnn_s = _lru_sample(xb, yb, sample_tile, steps,
                                     state_conv[0].transpose(1, 0, 2), state_rnn[0], *lru_args)
    x1, e_idx, gates = _proj_ln((m_p, m_s), lru_w_out, (0,), vec(lru_b_out[0]), x0,
                                vec(ln_g[0, 0]), vec(ln_b[0, 0]), wr_t, br, name="lru_out_ln")
    x2, x2_bf = _moe_block(x1, e_idx, gates, moe_w_gate, moe_w_up, moe_w_down, 0,
                           vec(ln_g[0, 1]), vec(ln_b[0, 1]), (F32, BF16))

    kv = _linear(x2_bf, attn_w_kv, (), jnp.zeros((1, 2 * KV_DIM), F32), F32, name="attn_kv")
    q = _linear(x2_bf, attn_w_q, (0,), jnp.zeros((1, D_MODEL), F32), BF16, scale=HEAD_DIM ** -0.5,
                name="attn_q")
    tab = rel_bias.reshape(-1)
    sinks = attn_sinks[0]
    qi = jnp.arange(WINDOW)[:, None]
    kj = jnp.arange(2 * WINDOW)[None, :]
    o_p = _attn_prompt(q, kv, bp, seq, _masked_buckets(qi + WINDOW - kj), tab, sinks)
    kv_s = kv[n_p:n_p + n_s].reshape(steps, bs, 2, KV_DIM).transpose(2, 1, 0, 3)
    k_all = jnp.concatenate([cache_k_win.reshape(bs, WINDOW, KV_DIM), kv_s[0]], axis=1)
    v_all = jnp.concatenate([cache_v_win.reshape(bs, WINDOW, KV_DIM), kv_s[1]], axis=1)
    dist_s = jnp.arange(steps)[:, None] + WINDOW - jnp.arange(WINDOW + steps)[None, :]
    q_s = q[n_p:n_p + n_s].reshape(steps, bs, D_MODEL).transpose(1, 0, 2)
    o_s = _attn_sample(q_s, k_all, v_all, _masked_buckets(dist_s), tab, sinks)
    o_s = pad_tile(o_s.transpose(1, 0, 2).reshape(n_s, D_MODEL))
    x3, e_idx, gates = _proj_ln((o_p, o_s), attn_w_o, (0,), jnp.zeros((1, D_MODEL), F32), x2,
                                vec(ln_g[1, 0]), vec(ln_b[1, 0]), wr_t, br, name="attn_out_ln")
    y_p, y_s = _moe_block(x3, e_idx, gates, moe_w_gate, moe_w_up, moe_w_down, 1,
                          vec(ln_g[1, 1]), vec(ln_b[1, 1]), (F32,), split=True)

    y_prompt = y_p.reshape(bp, seq, D_MODEL)
    y_sample = y_s[:n_s].reshape(steps, bs, D_MODEL).transpose(1, 0, 2)
    kv_p = jnp.stack([kv[(b + 1) * seq - WINDOW:(b + 1) * seq] for b in range(bp)])
    kv_p = kv_p.reshape(bp, WINDOW, 2, N_KV_HEADS, HEAD_DIM)
    k_win_s = k_all[:, steps:].reshape(bs, WINDOW, N_KV_HEADS, HEAD_DIM)
    v_win_s = v_all[:, steps:].reshape(bs, WINDOW, N_KV_HEADS, HEAD_DIM)
    return (y_prompt, y_sample,
            conv_p[None], rnn_p.reshape(1, bp, D_MODEL),
            kv_p[:, :, 0], kv_p[:, :, 1],
            conv_s.transpose(1, 0, 2)[None], rnn_s[None],
            k_win_s, v_win_s)
```

```python
import functools
import math

import jax
import jax.numpy as jnp
from jax import lax
from jax.experimental import pallas as pl
from jax.experimental.pallas import tpu as pltpu

D_MODEL = 2048
DEPTH = 2
LRU_BLOCKS = 8
LRU_BLOCK = D_MODEL // LRU_BLOCKS
CONV_WIDTH = 4
LRU_C = 8.0
N_HEADS = 32
HEAD_DIM = 64
N_KV_HEADS = 8
GROUP = N_HEADS // N_KV_HEADS
KV_DIM = N_KV_HEADS * HEAD_DIM
WINDOW = 128
N_BUCKETS = 32
MAX_DISTANCE = 128
N_EXPERTS = 16
N_GROUPS = 4
EXPERTS_PER_GROUP = N_EXPERTS // N_GROUPS
D_EXPERT = 1024
ALPHA = (2 * DEPTH) ** 0.25
LN_EPS = 1e-5

LANES = 128
ROW_SUBLANES = D_MODEL // (2 * LANES)
MOE_TILE = 256
TOK_TILE = 256
DMA_UNROLL = 8
SCALAR_UNROLL = 32
PLAN_UNROLL = 4
CAST_ROWS = 256
BF16 = jnp.bfloat16
F32 = jnp.float32
NEG_INF = float("-inf")


def _params(sem, vmem_mb):
    return pltpu.CompilerParams(dimension_semantics=sem, vmem_limit_bytes=vmem_mb * 1024 * 1024)


def _cast_rows(src_ref, dst_ref):
    n = src_ref.shape[0] // CAST_ROWS

    def body(i, c):
        r = pl.multiple_of(i * CAST_ROWS, CAST_ROWS)
        dst_ref[pl.ds(r, CAST_ROWS), :] = src_ref[pl.ds(r, CAST_ROWS), :].astype(BF16)
        return c

    lax.fori_loop(0, n, body, 0)


def _layer_norm(z, g, b):
    mu = jnp.mean(z, axis=-1, keepdims=True)
    zc = z - mu
    var = jnp.mean(zc * zc, axis=-1, keepdims=True)
    return zc * lax.rsqrt(var + LN_EPS) * g + b


def _store_packed_rows(x_bf, rows_ref):
    n = x_bf.shape[0]
    bits = pltpu.bitcast(x_bf.astype(F32), jnp.uint32)
    packed = bits[:, D_MODEL // 2:] | (bits[:, :D_MODEL // 2] >> 16)
    for c in range(ROW_SUBLANES):
        rows_ref[pl.ds(c, n, stride=ROW_SUBLANES), :] = packed[:, c * LANES:(c + 1) * LANES]


def _load_packed_rows(rows_ref, x_bf_ref):
    n = x_bf_ref.shape[0]
    for c in range(ROW_SUBLANES):
        words = rows_ref[pl.ds(c, n, stride=ROW_SUBLANES), :]
        low = pltpu.bitcast(words << 16, F32).astype(BF16)
        high = pltpu.bitcast(words & jnp.uint32(0xFFFF0000), F32).astype(BF16)
        x_bf_ref[:, c * LANES:(c + 1) * LANES] = low
        x_bf_ref[:, D_MODEL // 2 + c * LANES:D_MODEL // 2 + (c + 1) * LANES] = high


def _tok_operands(x):
    if isinstance(x, tuple):
        xp, xs = x
        d = xp.shape[1]
        last_p = xp.shape[0] // TOK_TILE - 1
        specs = [pl.BlockSpec((TOK_TILE, d), lambda i, *_: (jnp.minimum(i, last_p), 0)),
                 pl.BlockSpec((TOK_TILE, d), lambda i, *_: (0, 0))]
        return [xp, xs], specs, last_p + 2
    return [x], [pl.BlockSpec((TOK_TILE, x.shape[1]), lambda i, *_: (i, 0))], x.shape[0] // TOK_TILE


def _tok_load(refs):
    if len(refs) == 1:
        return refs[0][...]
    return jnp.where(pl.program_id(0) < pl.num_programs(0) - 1, refs[0][...], refs[1][...])


def _linear_kernel(*refs, n_x, act, scale):
    x_refs, (w_ref, b_ref, o_ref, wbf_ref) = refs[:n_x], refs[n_x:]

    @pl.when(pl.program_id(0) == 0)
    def _():
        _cast_rows(w_ref, wbf_ref)

    y = jnp.dot(_tok_load(x_refs).astype(BF16), wbf_ref[...], preferred_element_type=F32)
    y = y + b_ref[...]
    if act == "gelu":
        y = jax.nn.gelu(y)
    if scale != 1.0:
        y = y * scale
    o_ref[...] = y.astype(o_ref.dtype)


def _linear(x, w, w_index, b, out_dtype, *, name, act=None, scale=1.0):
    arrays, specs, nt = _tok_operands(x)
    k, nout = w.shape[-2:]
    w_block = (None,) * len(w_index) + (k, nout)
    return pl.pallas_call(
        functools.partial(_linear_kernel, n_x=len(arrays), act=act, scale=scale),
        grid=(nt,),
        in_specs=specs + [
            pl.BlockSpec(w_block, lambda i: w_index + (0, 0), pipeline_mode=pl.Buffered(1)),
            pl.BlockSpec((1, nout), lambda i: (0, 0)),
        ],
        out_specs=pl.BlockSpec((TOK_TILE, nout), lambda i: (i, 0)),
        out_shape=jax.ShapeDtypeStruct((nt * TOK_TILE, nout), out_dtype),
        scratch_shapes=[pltpu.VMEM((k, nout), BF16)],
        compiler_params=_params(("arbitrary",), 48),
        name=name,
    )(*arrays, w, b)


def _route(logits_t, b_router):
    aff = jax.nn.sigmoid(logits_t)
    sel = aff + b_router
    srow = [sel[e:e + 1, :] for e in range(N_EXPERTS)]
    arow = [aff[e:e + 1, :] for e in range(N_EXPERTS)]

    def top2_sum(v):
        pairs = [v[i] + v[j] for i in range(4) for j in range(i + 1, 4)]
        return functools.reduce(jnp.maximum, pairs)

    scores = [top2_sum(srow[4 * g:4 * g + 4]) for g in range(N_GROUPS)]
    best = scores[0]
    gi = jnp.zeros_like(best, dtype=jnp.int32)
    for g in range(1, N_GROUPS):
        upd = scores[g] > best
        best = jnp.where(upd, scores[g], best)
        gi = jnp.where(upd, g, gi)

    def pick_group(rows, j):
        out = rows[j]
        for g in range(1, N_GROUPS):
            out = jnp.where(gi == g, rows[4 * g + j], out)
        return out

    v = [pick_group(srow, j) for j in range(EXPERTS_PER_GROUP)]
    a = [pick_group(arow, j) for j in range(EXPERTS_PER_GROUP)]

    m1, i1 = v[0], jnp.zeros_like(gi)
    for j in range(1, EXPERTS_PER_GROUP):
        upd = v[j] > m1
        m1 = jnp.where(upd, v[j], m1)
        i1 = jnp.where(upd, j, i1)
    m2 = jnp.full_like(m1, NEG_INF)
    i2 = jnp.zeros_like(gi)
    for j in range(EXPERTS_PER_GROUP):
        cand = jnp.where(i1 == j, NEG_INF, v[j])
        upd = cand > m2
        m2 = jnp.where(upd, cand, m2)
        i2 = jnp.where(upd, j, i2)

    def pick_idx(rows, idx):
        out = rows[0]
        for j in range(1, EXPERTS_PER_GROUP):
            out = jnp.where(idx == j, rows[j], out)
        return out

    a1 = pick_idx(a, i1)
    a2 = pick_idx(a, i2)
    tot = a1 + a2
    e_idx = jnp.concatenate([gi * EXPERTS_PER_GROUP + i1, gi * EXPERTS_PER_GROUP + i2], axis=0)
    gates = jnp.concatenate([a1 / tot, a2 / tot], axis=0)
    return e_idx, gates


def _proj_ln_kernel(*refs, n_m, n_res):
    m_refs = refs[:n_m]
    w_ref, b_ref = refs[n_m:n_m + 2]
    res_refs = refs[n_m + 2:n_m + 2 + n_res]
    g_ref, beta_ref, wr_ref, br_ref, x_ref, xrow_ref, e_ref, gate_ref, wbf_ref = refs[n_m + 2 + n_res:]

    @pl.when(pl.program_id(0) == 0)
    def _():
        _cast_rows(w_ref, wbf_ref)

    y = jnp.dot(_tok_load(m_refs), wbf_ref[...], preferred_element_type=F32) + b_ref[...]
    x = _layer_norm(ALPHA * _tok_load(res_refs) + y, g_ref[...], beta_ref[...])
    x_ref[...] = x
    x_bf = x.astype(BF16)
    _store_packed_rows(x_bf, xrow_ref)
    logits_t = lax.dot_general(wr_ref[...].astype(BF16), x_bf,
                               (((1,), (1,)), ((), ())), preferred_element_type=F32)
    e_idx, gates = _route(logits_t, br_ref[...])
    e_ref[...] = e_idx
    gate_ref[...] = gates


def _proj_ln(m, w, w_index, b, res, g, beta, wr_t, br, *, name):
    m_arrays, m_specs, nt = _tok_operands(m)
    res_arrays, res_specs, _ = _tok_operands(res)
    k = w.shape[-2]
    tm = TOK_TILE
    n = nt * tm
    row = lambda i: (i, 0)
    const = lambda i: (0, 0)
    x, x_rows, e_idx, gates = pl.pallas_call(
        functools.partial(_proj_ln_kernel, n_m=len(m_arrays), n_res=len(res_arrays)),
        grid=(nt,),
        in_specs=m_specs + [
            pl.BlockSpec((None,) * len(w_index) + (k, D_MODEL), lambda i: w_index + (0, 0),
                         pipeline_mode=pl.Buffered(1)),
            pl.BlockSpec((1, D_MODEL), const),
        ] + res_specs + [
            pl.BlockSpec((1, D_MODEL), const),
            pl.BlockSpec((1, D_MODEL), const),
            pl.BlockSpec((N_EXPERTS, D_MODEL), const),
            pl.BlockSpec((N_EXPERTS, 1), const),
        ],
        out_specs=[
            pl.BlockSpec((tm, D_MODEL), row),
            pl.BlockSpec((tm * ROW_SUBLANES, LANES), row),
            pl.BlockSpec((None, 2, tm), lambda i: (i, 0, 0)),
            pl.BlockSpec((None, 2, tm), lambda i: (i, 0, 0)),
        ],
        out_shape=[
            jax.ShapeDtypeStruct((n, D_MODEL), F32),
            jax.ShapeDtypeStruct((n * ROW_SUBLANES, LANES), jnp.uint32),
            jax.ShapeDtypeStruct((nt, 2, tm), jnp.int32),
            jax.ShapeDtypeStruct((nt, 2, tm), F32),
        ],
        scratch_shapes=[pltpu.VMEM((k, D_MODEL), BF16)],
        compiler_params=_params(("arbitrary",), 48),
        name=name,
    )(*m_arrays, w, b, *res_arrays, g, beta, wr_t, br)
    e_idx = e_idx.transpose(1, 0, 2).reshape(2, n)
    gates = gates.transpose(1, 0, 2).reshape(2, n)
    return x, x_rows, e_idx, gates


def _plan_kernel(e_ref, pos_ref, meta_ref, rank_ref):
    nrow = e_ref.shape[0]
    ri = lax.broadcasted_iota(jnp.int32, (LANES, LANES), 0)
    ci = lax.broadcasted_iota(jnp.int32, (LANES, LANES), 1)
    tri = jnp.where(ri <= ci, 1.0, 0.0).astype(BF16)
    sub = lax.broadcasted_iota(jnp.int32, (N_EXPERTS, LANES), 0)

    def count_body(b, base):
        rows = [b * PLAN_UNROLL + u for u in range(PLAN_UNROLL)]
        onehots = [sub == e_ref[pl.ds(r, 1), :] for r in rows]
        locs = [jnp.dot(jnp.where(oh, 1.0, 0.0).astype(BF16), tri, preferred_element_type=F32)
                for oh in onehots]
        for r, onehot, loc in zip(rows, onehots, locs):
            rank_ref[pl.ds(r, 1), :] = jnp.sum(jnp.where(onehot, base + loc - 1.0, 0.0),
                                               axis=0, keepdims=True)
            base = base + jnp.broadcast_to(loc[:, LANES - 1:LANES], (N_EXPERTS, LANES))
        return base

    count = lax.fori_loop(0, nrow // PLAN_UNROLL, count_body, jnp.zeros((N_EXPERTS, LANES), F32))
    ntile = jnp.floor((count + (MOE_TILE - 1.0)) * (1.0 / MOE_TILE))
    offs = []
    acc = jnp.zeros((1, LANES), F32)
    for e in range(N_EXPERTS):
        offs.append(acc)
        acc = acc + ntile[e:e + 1, :]
    tile_off = jnp.concatenate(offs, axis=0)
    tile_end = tile_off + ntile
    lane = lax.broadcasted_iota(jnp.int32, (N_EXPERTS, LANES), 1).astype(F32)
    tile_expert = jnp.sum(jnp.where(tile_end <= lane, 1.0, 0.0), axis=0, keepdims=True)
    tile_expert = jnp.minimum(tile_expert, N_EXPERTS - 1.0)
    meta = jnp.concatenate([tile_expert, acc, jnp.zeros((6, LANES), F32)], axis=0)
    meta_ref[...] = meta.astype(jnp.int32)
    row_off = tile_off * float(MOE_TILE)

    def pos_body(r, c):
        onehot = sub == e_ref[pl.ds(r, 1), :]
        p = jnp.sum(jnp.where(onehot, row_off, 0.0), axis=0, keepdims=True) + rank_ref[pl.ds(r, 1), :]
        pos_ref[pl.ds(r, 1), :] = p.astype(jnp.int32)
        return c

    lax.fori_loop(0, nrow, pos_body, 0)


def _plan(e_idx, *, name):
    n2 = e_idx.shape[0] * e_idx.shape[1]
    assert n2 % (LANES * PLAN_UNROLL) == 0
    e2d = e_idx.reshape(n2 // LANES, LANES)
    pos, meta = pl.pallas_call(
        _plan_kernel,
        out_shape=[jax.ShapeDtypeStruct(e2d.shape, jnp.int32),
                   jax.ShapeDtypeStruct((8, LANES), jnp.int32)],
        scratch_shapes=[pltpu.VMEM(e2d.shape, F32)],
        name=name,
    )(e2d)
    return pos.reshape(n2), meta[0], meta[1, :1]


def _invert_kernel(pos_ref, pair_ref):
    n_rows = pair_ref.shape[0]
    n_pairs = pos_ref.shape[0]

    def fill_body(b, c):
        for u in range(SCALAR_UNROLL):
            pair_ref[b * SCALAR_UNROLL + u] = -1
        return c

    def pair_body(b, c):
        rows = [pos_ref[b * SCALAR_UNROLL + u] for u in range(SCALAR_UNROLL)]
        for u in range(SCALAR_UNROLL):
            pair_ref[rows[u]] = b * SCALAR_UNROLL + u
        return c

    lax.fori_loop(0, n_rows // SCALAR_UNROLL, fill_body, 0)
    lax.fori_loop(0, n_pairs // SCALAR_UNROLL, pair_body, 0)


def _invert(pos, n_rows, *, name):
    return pl.pallas_call(
        _invert_kernel,
        grid_spec=pltpu.PrefetchScalarGridSpec(
            num_scalar_prefetch=1,
            grid=(1,),
            in_specs=[],
            out_specs=pl.BlockSpec(memory_space=pltpu.SMEM),
        ),
        out_shape=jax.ShapeDtypeStruct((n_rows,), jnp.int32),
        name=name,
    )(pos)


def _expert_changed(te_ref, i):
    return jnp.logical_or(i == 0, te_ref[i] != te_ref[jnp.maximum(i - 1, 0)])


def _moe_up_kernel(pair_ref, te_ref, nu_ref, x_hbm, wg_ref, wu_ref, h_ref,
                   wg_bf, wu_bf, x_rows, x_bf, sem, *, n_tok, n_tiles):
    i = pl.program_id(0)
    nu = nu_ref[0]

    def row_copy(tile, r):
        p = pair_ref[tile * MOE_TILE + r]
        tok = jnp.where(p >= n_tok, p - n_tok, jnp.maximum(p, 0))
        src = pl.ds(pl.multiple_of(tok * ROW_SUBLANES, ROW_SUBLANES), ROW_SUBLANES)
        dst = pl.ds(r * ROW_SUBLANES, ROW_SUBLANES)
        return pltpu.make_async_copy(x_hbm.at[src, :], x_rows.at[dst, :], sem)

    def wait_tile():
        pltpu.make_async_copy(x_hbm.at[pl.ds(0, MOE_TILE * ROW_SUBLANES), :], x_rows, sem).wait()

    @pl.when(i == 0)
    def _():
        def body(rb, c):
            for u in range(DMA_UNROLL):
                row_copy(0, rb * DMA_UNROLL + u).start()
            return c
        lax.fori_loop(0, MOE_TILE // DMA_UNROLL, body, 0)

    @pl.when(i <= nu)
    def _():
        wait_tile()

    @pl.when(i < nu)
    def _():
        @pl.when(_expert_changed(te_ref, i))
        def _():
            _cast_rows(wg_ref, wg_bf)
            _cast_rows(wu_ref, wu_bf)

        _load_packed_rows(x_rows, x_bf)
        next_tile = jnp.minimum(i + 1, n_tiles - 1)
        for r in range(MOE_TILE):
            row_copy(next_tile, r).start(priority=r % 2)
        x = x_bf[...]
        a = jnp.dot(x, wg_bf[...], preferred_element_type=F32)
        b = jnp.dot(x, wu_bf[...], preferred_element_type=F32)
        h_ref[...] = (jax.nn.silu(a) * b).astype(BF16)

    @pl.when(jnp.logical_and(i == n_tiles - 1, i < nu))
    def _():
        wait_tile()

    @pl.when(i >= nu)
    def _():
        h_ref[...] = jnp.zeros_like(h_ref)


def _moe_down_kernel(pair_ref, te_ref, nu_ref, h_ref, wd_ref, out_hbm,
                     wd_bf, ya, yb, sems, tsem, *, n_tok, n_tiles):
    i = pl.program_id(0)
    nu = nu_ref[0]
    bufs = (ya, yb)
    trash = 2 * n_tok

    def row_copy(tile, r, buf, sem):
        p = pair_ref[tile * MOE_TILE + r]
        dst = jnp.where(p < 0, trash + r, p)
        return pltpu.make_async_copy(buf.at[pl.ds(r, 1), :], out_hbm.at[pl.ds(dst, 1), :], sem)

    def wait_tile(buf, sem):
        pltpu.make_async_copy(buf, out_hbm.at[pl.ds(0, MOE_TILE), :], sem).wait()

    @pl.when(i == 0)
    def _():
        yb[...] = jnp.zeros_like(yb)
        fill = pltpu.make_async_copy(yb, out_hbm.at[pl.ds(trash, MOE_TILE), :], tsem)
        fill.start()
        fill.wait()

    for parity in range(2):
        cur, prev = bufs[parity], bufs[1 - parity]
        cur_sem, prev_sem = sems.at[parity], sems.at[1 - parity]
        mine = i % 2 == parity

        @pl.when(jnp.logical_and(mine, jnp.logical_and(i >= 1, i - 1 <= nu)))
        def _():
            wait_tile(cur, cur_sem)

        @pl.when(jnp.logical_and(mine, i < nu))
        def _():
            @pl.when(_expert_changed(te_ref, i))
            def _():
                _cast_rows(wd_ref, wd_bf)

            prev_tile = jnp.maximum(i - 1, 0)
            for r in range(MOE_TILE):
                row_copy(prev_tile, r, prev, prev_sem).start(priority=r % 2)
            cur[...] = jnp.dot(h_ref[...], wd_bf[...], preferred_element_type=F32)

        @pl.when(jnp.logical_and(mine, i == nu))
        def _():
            def body(rb, c):
                for u in range(DMA_UNROLL):
                    row_copy(i - 1, rb * DMA_UNROLL + u, prev, prev_sem).start()
                return c
            lax.fori_loop(0, MOE_TILE // DMA_UNROLL, body, 0)

            @pl.when(i == n_tiles)
            def _():
                wait_tile(prev, prev_sem)


def _moe_ffn(x, pair, tile_expert, n_used, w_gate, w_up, w_down, layer):
    n_tok = x.shape[0] // ROW_SUBLANES
    n_tiles = pair.shape[0] // MOE_TILE

    def tile(i, pr, te, nu):
        return (jnp.minimum(i, nu[0] - 1), 0)

    def expert(i, pr, te, nu):
        return (layer, te[jnp.minimum(i, nu[0] - 1)], 0, 0)

    def weight_spec(k, n):
        return pl.BlockSpec((None, None, k, n), expert)

    row_bufs = [pltpu.VMEM((MOE_TILE, D_MODEL), F32), pltpu.VMEM((MOE_TILE, D_MODEL), F32),
                pltpu.SemaphoreType.DMA((2,))]
    h = pl.pallas_call(
        functools.partial(_moe_up_kernel, n_tok=n_tok, n_tiles=n_tiles),
        grid_spec=pltpu.PrefetchScalarGridSpec(
            num_scalar_prefetch=3,
            grid=(n_tiles,),
            in_specs=[pl.BlockSpec(memory_space=pl.ANY),
                      weight_spec(D_MODEL, D_EXPERT),
                      weight_spec(D_MODEL, D_EXPERT)],
            out_specs=pl.BlockSpec((MOE_TILE, D_EXPERT), lambda i, pr, te, nu: (i, 0)),
            scratch_shapes=[pltpu.VMEM((D_MODEL, D_EXPERT), BF16),
                            pltpu.VMEM((D_MODEL, D_EXPERT), BF16),
                            pltpu.VMEM((MOE_TILE * ROW_SUBLANES, LANES), jnp.uint32),
                            pltpu.VMEM((MOE_TILE, D_MODEL), BF16),
                            pltpu.SemaphoreType.DMA(())],
        ),
        out_shape=jax.ShapeDtypeStruct((n_tiles * MOE_TILE, D_EXPERT), BF16),
        compiler_params=_params(("arbitrary",), 56),
        name=f"moe_up_{layer}",
    )(pair, tile_expert, n_used, x, w_gate, w_up)
    return pl.pallas_call(
        functools.partial(_moe_down_kernel, n_tok=n_tok, n_tiles=n_tiles),
        grid_spec=pltpu.PrefetchScalarGridSpec(
            num_scalar_prefetch=3,
            grid=(n_tiles + 1,),
            in_specs=[pl.BlockSpec((MOE_TILE, D_EXPERT), tile),
                      weight_spec(D_EXPERT, D_MODEL)],
            out_specs=pl.BlockSpec(memory_space=pl.ANY),
            scratch_shapes=[pltpu.VMEM((D_EXPERT, D_MODEL), BF16)] + row_bufs
            + [pltpu.SemaphoreType.DMA(())],
        ),
        out_shape=jax.ShapeDtypeStruct((2 * n_tok + MOE_TILE, D_MODEL), F32),
        compiler_params=_params(("arbitrary",), 40),
        name=f"moe_down_{layer}",
    )(pair, tile_expert, n_used, h, w_down)


def _combine_ln_kernel(y0_ref, y1_ref, res_ref, gate_ref, g_ref, beta_ref, *outs, split):
    i = pl.program_id(0)
    gate = gate_ref[...]
    ffn = gate[:, 0:1] * y0_ref[...] + gate[:, 1:2] * y1_ref[...]
    x = _layer_norm(ALPHA * res_ref[...] + ffn, g_ref[...], beta_ref[...])
    if split:
        prompt_ref, sample_ref = outs
        is_sample = i == pl.num_programs(0) - 1

        @pl.when(jnp.logical_not(is_sample))
        def _():
            prompt_ref[...] = x

        @pl.when(is_sample)
        def _():
            sample_ref[...] = x
    else:
        for o_ref in outs:
            o_ref[...] = x.astype(o_ref.dtype)


def _combine_ln(ys, res, gates_col, g, beta, out_dtypes, *, split, name):
    n = res.shape[0]
    tm = TOK_TILE
    nt = n // tm
    row = lambda i: (i, 0)
    const = lambda i: (0, 0)
    if split:
        out_specs = [pl.BlockSpec((tm, D_MODEL), lambda i: (jnp.minimum(i, nt - 2), 0)),
                     pl.BlockSpec((tm, D_MODEL), const)]
        out_shape = [jax.ShapeDtypeStruct((n - tm, D_MODEL), F32),
                     jax.ShapeDtypeStruct((tm, D_MODEL), F32)]
    else:
        out_specs = [pl.BlockSpec((tm, D_MODEL), row) for _ in out_dtypes]
        out_shape = [jax.ShapeDtypeStruct((n, D_MODEL), dt) for dt in out_dtypes]
    return pl.pallas_call(
        functools.partial(_combine_ln_kernel, split=split),
        grid=(nt,),
        in_specs=[pl.BlockSpec((tm, D_MODEL), row),
                  pl.BlockSpec((tm, D_MODEL), lambda i: (i + nt, 0)),
                  pl.BlockSpec((tm, D_MODEL), row),
                  pl.BlockSpec((tm, 2), row),
                  pl.BlockSpec((1, D_MODEL), const),
                  pl.BlockSpec((1, D_MODEL), const)],
        out_specs=out_specs,
        out_shape=out_shape,
        compiler_params=_params(("arbitrary",), 40),
        name=name,
    )(ys, ys, res, gates_col, g, beta)


def _moe_block(x, x_rows, e_idx, gates, w_gate, w_up, w_down, layer, g, beta, out_dtypes, split=False):
    n = x.shape[0]
    n_tiles = -(-(2 * n + N_EXPERTS * (MOE_TILE - 1)) // MOE_TILE)
    pos, tile_expert, n_used = _plan(e_idx, name=f"moe_plan_{layer}")
    pair = _invert(pos, n_tiles * MOE_TILE, name=f"moe_invert_{layer}")
    ys = _moe_ffn(x_rows, pair, tile_expert, n_used, w_gate, w_up, w_down, layer)
    return _combine_ln(ys, x, gates.T, g, beta, out_dtypes, split=split,
                       name=f"moe_combine_{layer}")


def _sigmoid(x):
    return 0.5 * jnp.tanh(0.5 * x) + 0.5


def _log_sigmoid(x):
    return -(jnp.maximum(-x, 0.0) + jnp.log1p(jnp.exp(-jnp.abs(x))))


def _lru_gate_block(xc, n, wrg_bf, wig_bf, brg_ref, big_ref, lam_ref):
    cols = slice(n * LRU_BLOCK, (n + 1) * LRU_BLOCK)
    xb = xc.astype(BF16)
    r = _sigmoid(jnp.dot(xb, wrg_bf[n], preferred_element_type=F32) + brg_ref[:, cols])
    i = _sigmoid(jnp.dot(xb, wig_bf[n], preferred_element_type=F32) + big_ref[:, cols])
    log_a = LRU_C * r * _log_sigmoid(lam_ref[:, cols])
    a = jnp.exp(log_a)
    u = xc * i * jnp.sqrt(-jnp.tanh(log_a) * (a * a + 1.0))
    return a, u


def _cast_gate_weights(wrg_ref, wig_ref, wrg_bf, wig_bf):
    for n in range(LRU_BLOCKS):
        wrg_bf[n] = wrg_ref[n].astype(BF16)
        wig_bf[n] = wig_ref[n].astype(BF16)


def _lru_prompt_kernel(xb_ref, yb_ref, cw_ref, cb_ref, wrg_ref, wig_ref, brg_ref, big_ref, lam_ref,
                       m_ref, conv_ref, hlast_ref, xpad, a_s, u_s, h_s, wrg_bf, wig_bf):
    b = pl.program_id(0)
    j = pl.program_id(1)
    tt = xb_ref.shape[0]

    @pl.when(jnp.logical_and(b == 0, j == 0))
    def _():
        _cast_gate_weights(wrg_ref, wig_ref, wrg_bf, wig_bf)

    @pl.when(j == 0)
    def _():
        xpad[0:8, :] = jnp.zeros((8, D_MODEL), F32)
        h_s[...] = jnp.zeros_like(h_s)

    xpad[8:8 + tt, :] = xb_ref[...]
    for n in range(LRU_BLOCKS):
        cols = slice(n * LRU_BLOCK, (n + 1) * LRU_BLOCK)
        xc = cb_ref[:, cols] + cw_ref[0:1, cols] * xpad[5:5 + tt, cols]
        for k in range(1, CONV_WIDTH):
            xc = xc + cw_ref[k:k + 1, cols] * xpad[5 + k:5 + k + tt, cols]
        a, u = _lru_gate_block(xc, n, wrg_bf, wig_bf, brg_ref, big_ref, lam_ref)
        a_s[:, cols] = a
        u_s[:, cols] = u

    def scan_body(gidx, h):
        base = pl.multiple_of(gidx * 8, 8)
        a8 = a_s[pl.ds(base, 8), :]
        u8 = u_s[pl.ds(base, 8), :]
        rows = []
        for s in range(8):
            h = a8[s:s + 1, :] * h + u8[s:s + 1, :]
            rows.append(h)
        a_s[pl.ds(base, 8), :] = jnp.concatenate(rows, axis=0)
        return h

    h = lax.fori_loop(0, tt // 8, scan_body, h_s[...])
    h_s[...] = h
    m_ref[...] = (a_s[...] * yb_ref[...].astype(F32)).astype(BF16)
    xpad[0:8, :] = xpad[tt:tt + 8, :]

    @pl.when(j == pl.num_programs(1) - 1)
    def _():
        conv_ref[...] = xpad[5:8, :]
        hlast_ref[...] = h


def _lru_prompt(xb, yb, batch, seq, cw, cb, wrg, wig, brg, big, lam, *, tt=256):
    nj = seq // tt
    row = lambda b, j: (b * nj + j, 0)
    const2 = lambda b, j: (0, 0)
    const3 = lambda b, j: (0, 0, 0)
    return pl.pallas_call(
        _lru_prompt_kernel,
        grid=(batch, nj),
        in_specs=[
            pl.BlockSpec((tt, D_MODEL), row),
            pl.BlockSpec((tt, D_MODEL), row),
            pl.BlockSpec((CONV_WIDTH, D_MODEL), const2),
            pl.BlockSpec((1, D_MODEL), const2),
            pl.BlockSpec((LRU_BLOCKS, LRU_BLOCK, LRU_BLOCK), const3),
            pl.BlockSpec((LRU_BLOCKS, LRU_BLOCK, LRU_BLOCK), const3),
            pl.BlockSpec((1, D_MODEL), const2),
            pl.BlockSpec((1, D_MODEL), const2),
            pl.BlockSpec((1, D_MODEL), const2),
        ],
        out_specs=[
            pl.BlockSpec((tt, D_MODEL), row),
            pl.BlockSpec((None, CONV_WIDTH - 1, D_MODEL), lambda b, j: (b, 0, 0)),
            pl.BlockSpec((None, 1, D_MODEL), lambda b, j: (b, 0, 0)),
        ],
        out_shape=[
            jax.ShapeDtypeStruct((batch * seq, D_MODEL), BF16),
            jax.ShapeDtypeStruct((batch, CONV_WIDTH - 1, D_MODEL), F32),
            jax.ShapeDtypeStruct((batch, 1, D_MODEL), F32),
        ],
        scratch_shapes=[
            pltpu.VMEM((tt + 8, D_MODEL), F32),
            pltpu.VMEM((tt, D_MODEL), F32),
            pltpu.VMEM((tt, D_MODEL), F32),
            pltpu.VMEM((1, D_MODEL), F32),
            pltpu.VMEM((LRU_BLOCKS, LRU_BLOCK, LRU_BLOCK), BF16),
            pltpu.VMEM((LRU_BLOCKS, LRU_BLOCK, LRU_BLOCK), BF16),
        ],
        compiler_params=_params(("arbitrary", "arbitrary"), 40),
        name="lru_prompt",
    )(xb, yb, cw, cb, wrg, wig, brg, big, lam)


def _lru_sample_kernel(xb_ref, yb_ref, cs_ref, h0_ref, cw_ref, cb_ref, wrg_ref, wig_ref,
                       brg_ref, big_ref, lam_ref, m_ref, conv_ref, hlast_ref, wrg_bf, wig_bf, *, steps):
    batch = h0_ref.shape[0]
    _cast_gate_weights(wrg_ref, wig_ref, wrg_bf, wig_bf)
    m_ref[steps * batch:, :] = jnp.zeros((m_ref.shape[0] - steps * batch, D_MODEL), BF16)

    def slab(t, cols):
        if t < CONV_WIDTH - 1:
            return cs_ref[t, :, cols]
        t -= CONV_WIDTH - 1
        return xb_ref[t * batch:(t + 1) * batch, cols]

    for n in range(LRU_BLOCKS):
        cols = slice(n * LRU_BLOCK, (n + 1) * LRU_BLOCK)
        h = h0_ref[:, cols]
        for t in range(steps):
            xc = cb_ref[:, cols] + cw_ref[0:1, cols] * slab(t, cols)
            for k in range(1, CONV_WIDTH):
                xc = xc + cw_ref[k:k + 1, cols] * slab(t + k, cols)
            a, u = _lru_gate_block(xc, n, wrg_bf, wig_bf, brg_ref, big_ref, lam_ref)
            h = a * h + u
            rows = slice(t * batch, (t + 1) * batch)
            m_ref[rows, cols] = (h * yb_ref[rows, cols].astype(F32)).astype(BF16)
        hlast_ref[:, cols] = h
    for k in range(CONV_WIDTH - 1):
        conv_ref[k] = slab(steps + k, slice(None))


def _lru_sample(xb, yb, tile, steps, conv_state, h0, cw, cb, wrg, wig, brg, big, lam):
    batch = h0.shape[0]
    tok = pl.BlockSpec((TOK_TILE, D_MODEL), lambda i: (tile, 0))
    full = lambda a: pl.BlockSpec(a.shape, lambda i: (0,) * a.ndim)
    small = (conv_state, h0, cw, cb, wrg, wig, brg, big, lam)
    return pl.pallas_call(
        functools.partial(_lru_sample_kernel, steps=steps),
        grid=(1,),
        in_specs=[tok, tok] + [full(a) for a in small],
        out_specs=[
            pl.BlockSpec((TOK_TILE, D_MODEL), lambda i: (0, 0)),
            pl.BlockSpec((CONV_WIDTH - 1, batch, D_MODEL), lambda i: (0, 0, 0)),
            pl.BlockSpec((batch, D_MODEL), lambda i: (0, 0)),
        ],
        out_shape=[
            jax.ShapeDtypeStruct((TOK_TILE, D_MODEL), BF16),
            jax.ShapeDtypeStruct((CONV_WIDTH - 1, batch, D_MODEL), F32),
            jax.ShapeDtypeStruct((batch, D_MODEL), F32),
        ],
        scratch_shapes=[
            pltpu.VMEM((LRU_BLOCKS, LRU_BLOCK, LRU_BLOCK), BF16),
            pltpu.VMEM((LRU_BLOCKS, LRU_BLOCK, LRU_BLOCK), BF16),
        ],
        compiler_params=_params(("arbitrary",), 32),
        name="lru_sample",
    )(xb, yb, *small)


def _rel_bucket(dist):
    n = jnp.maximum(dist, 0)
    max_exact = N_BUCKETS // 2
    nf = jnp.maximum(n, 1).astype(F32)
    large = max_exact + (jnp.log(nf / max_exact) / math.log(MAX_DISTANCE / max_exact)
                         * (N_BUCKETS - max_exact)).astype(jnp.int32)
    large = jnp.minimum(large, N_BUCKETS - 1)
    return jnp.where(n < max_exact, n, large)


def _masked_buckets(dist):
    valid = (dist >= 0) & (dist < WINDOW)
    return jnp.where(valid, _rel_bucket(dist), -1).astype(jnp.int32)


def _build_bias(bucket, tab_ref, head):
    def body(bi, acc):
        return jnp.where(bucket == bi, tab_ref[bi * N_HEADS + head], acc)
    return lax.fori_loop(0, N_BUCKETS, body, jnp.full(bucket.shape, NEG_INF, F32))


def _softmax_pv(s, sink, v):
    m = jnp.maximum(jnp.max(s, axis=-1, keepdims=True), sink)
    p = jnp.exp(s - m)
    den = jnp.sum(p, axis=-1, keepdims=True) + jnp.exp(sink - m)
    return jnp.dot(p.astype(BF16), v, preferred_element_type=F32) / den


def _attn_prompt_kernel(q_ref, kvp_ref, kvc_ref, bucket_ref, tab_ref, sink_ref, o_ref, bias_s):
    b = pl.program_id(0)
    n = pl.program_id(1)

    @pl.when(jnp.logical_and(b == 0, n == 0))
    def _():
        bucket = bucket_ref[...]

        col = lax.broadcasted_iota(jnp.int32, (WINDOW, 2 * WINDOW), 1)

        def head_body(h, c):
            bias = _build_bias(bucket, tab_ref, h)
            sink = sink_ref[h]
            g = h // GROUP
            r0 = pl.multiple_of((h % GROUP) * WINDOW, WINDOW)
            bias_s[0, g, pl.ds(r0, WINDOW), :] = jnp.where(col == 0, sink, bias)
            bias_s[1, g, pl.ds(r0, WINDOW), :] = jnp.where(
                col == 0, sink, jnp.where(col < WINDOW, NEG_INF, bias))
            return c

        lax.fori_loop(0, N_HEADS, head_body, 0)

    first = (n == 0).astype(jnp.int32)
    row = lax.broadcasted_iota(jnp.int32, kvp_ref.shape, 0)
    kv_prev = jnp.where(row == 0, 0.0, kvp_ref[...])
    kv = jnp.concatenate([kv_prev, kvc_ref[...]], axis=0).astype(BF16)
    ones = jnp.ones((2 * WINDOW, HEAD_DIM), BF16)
    lane = lax.broadcasted_iota(jnp.int32, (WINDOW, 2 * HEAD_DIM), 1)
    for g in range(N_KV_HEADS):
        heads = range(g * GROUP, (g + 1) * GROUP)
        kg = kv[:, g * HEAD_DIM:(g + 1) * HEAD_DIM]
        vg = kv[:, KV_DIM + g * HEAD_DIM:KV_DIM + (g + 1) * HEAD_DIM]
        v_ext = jnp.concatenate([vg, ones], axis=1)
        qg = jnp.concatenate([q_ref[:, h * HEAD_DIM:(h + 1) * HEAD_DIM] for h in heads], axis=0)
        s = lax.dot_general(qg, kg, (((1,), (1,)), ((), ())), preferred_element_type=F32)
        s = s + bias_s[first, g]
        p = jnp.exp(s - jnp.max(s, axis=-1, keepdims=True)).astype(BF16)
        o_ext = jnp.dot(p, v_ext, preferred_element_type=F32)
        o_rot = pltpu.roll(o_ext, HEAD_DIM, axis=1)
        for pair in range(GROUP // 2):
            r0 = slice(2 * pair * WINDOW, (2 * pair + 1) * WINDOW)
            r1 = slice((2 * pair + 1) * WINDOW, (2 * pair + 2) * WINDOW)
            even = o_ext[r0] * (1.0 / o_rot[r0])
            odd = o_rot[r1] * (1.0 / o_ext[r1])
            c0 = (g * GROUP + 2 * pair) * HEAD_DIM
            o_ref[:, c0:c0 + 2 * HEAD_DIM] = jnp.where(lane < HEAD_DIM, even, odd).astype(BF16)


def _attn_prompt(q, kv, batch, seq, bucket, tab, sinks):
    nb = seq // WINDOW
    smem = pl.BlockSpec(memory_space=pltpu.SMEM)
    return pl.pallas_call(
        _attn_prompt_kernel,
        grid=(batch, nb),
        in_specs=[
            pl.BlockSpec((WINDOW, D_MODEL), lambda b, n: (b * nb + n, 0)),
            pl.BlockSpec((WINDOW, 2 * KV_DIM), lambda b, n: (jnp.maximum(b * nb + n - 1, 0), 0)),
            pl.BlockSpec((WINDOW, 2 * KV_DIM), lambda b, n: (b * nb + n, 0)),
            pl.BlockSpec((WINDOW, 2 * WINDOW), lambda b, n: (0, 0)),
            smem, smem,
        ],
        out_specs=pl.BlockSpec((WINDOW, D_MODEL), lambda b, n: (b * nb + n, 0)),
        out_shape=jax.ShapeDtypeStruct((batch * seq, D_MODEL), BF16),
        scratch_shapes=[pltpu.VMEM((2, N_KV_HEADS, GROUP * WINDOW, 2 * WINDOW), F32)],
        compiler_params=_params(("arbitrary", "arbitrary"), 32),
        name="attn_prompt",
    )(q, kv, kv, bucket, tab, sinks)


def _attn_sample_kernel(q_ref, k_ref, v_ref, bucket_ref, tab_ref, sink_ref, o_ref, bias_s):
    steps = q_ref.shape[0]

    @pl.when(pl.program_id(0) == 0)
    def _():
        bucket = bucket_ref[...]

        def head_body(h, c):
            bias_s[h] = _build_bias(bucket, tab_ref, h)
            return c

        lax.fori_loop(0, N_HEADS, head_body, 0)

    rows = lax.broadcasted_iota(jnp.int32, (GROUP * steps, 1), 0)
    for g in range(N_KV_HEADS):
        kg = k_ref[:, g * HEAD_DIM:(g + 1) * HEAD_DIM].astype(BF16)
        vg = v_ref[:, g * HEAD_DIM:(g + 1) * HEAD_DIM].astype(BF16)
        heads = range(g * GROUP, (g + 1) * GROUP)
        qg = jnp.concatenate([q_ref[:, h * HEAD_DIM:(h + 1) * HEAD_DIM] for h in heads], axis=0)
        bias = jnp.concatenate([bias_s[h] for h in heads], axis=0)
        sink = jnp.full((GROUP * steps, 1), sink_ref[g * GROUP], F32)
        for hh in range(1, GROUP):
            sink = jnp.where(rows >= hh * steps, sink_ref[g * GROUP + hh], sink)
        s = lax.dot_general(qg, kg, (((1,), (1,)), ((), ())), preferred_element_type=F32) + bias
        o = _softmax_pv(s, sink, vg)
        for hh, h in enumerate(heads):
            o_ref[:, h * HEAD_DIM:(h + 1) * HEAD_DIM] = o[hh * steps:(hh + 1) * steps].astype(BF16)


def _attn_sample(q, k_all, v_all, bucket, tab, sinks):
    batch, steps, _ = q.shape
    lk = k_all.shape[1]
    smem = pl.BlockSpec(memory_space=pltpu.SMEM)
    return pl.pallas_call(
        _attn_sample_kernel,
        grid=(batch,),
        in_specs=[
            pl.BlockSpec((None, steps, D_MODEL), lambda b: (b, 0, 0)),
            pl.BlockSpec((None, lk, KV_DIM), lambda b: (b, 0, 0)),
            pl.BlockSpec((None, lk, KV_DIM), lambda b: (b, 0, 0)),
            pl.BlockSpec((steps, lk), lambda b: (0, 0)),
            smem, smem,
        ],
        out_specs=pl.BlockSpec((None, steps, D_MODEL), lambda b: (b, 0, 0)),
        out_shape=jax.ShapeDtypeStruct((batch, steps, D_MODEL), BF16),
        scratch_shapes=[pltpu.VMEM((N_HEADS, steps, lk), F32)],
        compiler_params=_params(("arbitrary",), 32),
        name="attn_sample",
    )(q, k_all, v_all, bucket, tab, sinks)


def kernel(x_prompt, x_sample, state_conv, state_rnn, cache_k_win, cache_v_win, ln_g, ln_b, lru_w_x, lru_b_x, lru_w_y, lru_b_y, lru_conv_w, lru_conv_b, lru_w_rg, lru_b_rg, lru_w_ig, lru_b_ig, lru_lam, lru_w_out, lru_b_out, attn_w_kv, attn_w_q, attn_w_o, attn_sinks, rel_bias, moe_w_router, moe_b_router, moe_w_gate, moe_w_up, moe_w_down):
    bp, seq, _ = x_prompt.shape
    bs, steps, _ = x_sample.shape
    n_p = bp * seq
    n_s = bs * steps

    assert n_p % TOK_TILE == 0 and n_s <= TOK_TILE
    sample_tile = n_p // TOK_TILE

    def pad_tile(rows):
        return jnp.pad(rows, ((0, TOK_TILE - n_s), (0, 0)))

    x0 = (x_prompt.reshape(n_p, D_MODEL),
          pad_tile(x_sample.transpose(1, 0, 2).reshape(n_s, D_MODEL)))
    wr_t = moe_w_router.T
    br = moe_b_router.reshape(N_EXPERTS, 1)
    vec = lambda a: a.reshape(1, -1)

    xb = _linear(x0, lru_w_x, (0,), vec(lru_b_x[0]), F32, name="lru_in_x")
    yb = _linear(x0, lru_w_y, (0,), vec(lru_b_y[0]), BF16, act="gelu", name="lru_in_y")
    lru_args = (lru_conv_w[0], vec(lru_conv_b[0]), lru_w_rg[0], lru_w_ig[0],
                vec(lru_b_rg[0]), vec(lru_b_ig[0]), vec(lru_lam[0]))
    m_p, conv_p, rnn_p = _lru_prompt(xb, yb, bp, seq, *lru_args)
    m_s, conv_s, rnn_s = _lru_sample(xb, yb, sample_tile, steps,
                                     state_conv[0].transpose(1, 0, 2), state_rnn[0], *lru_args)
    x1, x1_rows, e_idx, gates = _proj_ln((m_p, m_s), lru_w_out, (0,), vec(lru_b_out[0]), x0,
                                vec(ln_g[0, 0]), vec(ln_b[0, 0]), wr_t, br, name="lru_out_ln")
    x2, x2_bf = _moe_block(x1, x1_rows, e_idx, gates, moe_w_gate, moe_w_up, moe_w_down, 0,
                           vec(ln_g[0, 1]), vec(ln_b[0, 1]), (F32, BF16))

    kv = _linear(x2_bf, attn_w_kv, (), jnp.zeros((1, 2 * KV_DIM), F32), F32, name="attn_kv")
    q = _linear(x2_bf, attn_w_q, (0,), jnp.zeros((1, D_MODEL), F32), BF16, scale=HEAD_DIM ** -0.5,
                name="attn_q")
    tab = rel_bias.reshape(-1)
    sinks = attn_sinks[0]
    qi = jnp.arange(WINDOW)[:, None]
    kj = jnp.arange(2 * WINDOW)[None, :]
    o_p = _attn_prompt(q, kv, bp, seq, _masked_buckets(qi + WINDOW - kj), tab, sinks)
    kv_s = kv[n_p:n_p + n_s].reshape(steps, bs, 2, KV_DIM).transpose(2, 1, 0, 3)
    k_all = jnp.concatenate([cache_k_win.reshape(bs, WINDOW, KV_DIM), kv_s[0]], axis=1)
    v_all = jnp.concatenate([cache_v_win.reshape(bs, WINDOW, KV_DIM), kv_s[1]], axis=1)
    dist_s = jnp.arange(steps)[:, None] + WINDOW - jnp.arange(WINDOW + steps)[None, :]
    q_s = q[n_p:n_p + n_s].reshape(steps, bs, D_MODEL).transpose(1, 0, 2)
    o_s = _attn_sample(q_s, k_all, v_all, _masked_buckets(dist_s), tab, sinks)
    o_s = pad_tile(o_s.transpose(1, 0, 2).reshape(n_s, D_MODEL))
    x3, x3_rows, e_idx, gates = _proj_ln((o_p, o_s), attn_w_o, (0,), jnp.zeros((1, D_MODEL), F32), x2,
                                vec(ln_g[1, 0]), vec(ln_b[1, 0]), wr_t, br, name="attn_out_ln")
    y_p, y_s = _moe_block(x3, x3_rows, e_idx, gates, moe_w_gate, moe_w_up, moe_w_down, 1,
                          vec(ln_g[1, 1]), vec(ln_b[1, 1]), (F32,), split=True)

    y_prompt = y_p.reshape(bp, seq, D_MODEL)
    y_sample = y_s[:n_s].reshape(steps, bs, D_MODEL).transpose(1, 0, 2)
    kv_p = jnp.stack([kv[(b + 1) * seq - WINDOW:(b + 1) * seq] for b in range(bp)])
    kv_p = kv_p.reshape(bp, WINDOW, 2, N_KV_HEADS, HEAD_DIM)
    k_win_s = k_all[:, steps:].reshape(bs, WINDOW, N_KV_HEADS, HEAD_DIM)
    v_win_s = v_all[:, steps:].reshape(bs, WINDOW, N_KV_HEADS, HEAD_DIM)
    return (y_prompt, y_sample,
            conv_p[None], rnn_p.reshape(1, bp, D_MODEL),
            kv_p[:, :, 0], kv_p[:, :, 1],
            conv_s.transpose(1, 0, 2)[None], rnn_s[None],
            k_win_s, v_win_s)
```

```python
import functools
import math

import jax
import jax.numpy as jnp
from jax import lax
from jax.experimental import pallas as pl
from jax.experimental.pallas import tpu as pltpu

D_MODEL = 2048
DEPTH = 2
LRU_BLOCKS = 8
LRU_BLOCK = D_MODEL // LRU_BLOCKS
CONV_WIDTH = 4
LRU_C = 8.0
N_HEADS = 32
HEAD_DIM = 64
N_KV_HEADS = 8
GROUP = N_HEADS // N_KV_HEADS
KV_DIM = N_KV_HEADS * HEAD_DIM
WINDOW = 128
N_BUCKETS = 32
MAX_DISTANCE = 128
N_EXPERTS = 16
N_GROUPS = 4
EXPERTS_PER_GROUP = N_EXPERTS // N_GROUPS
D_EXPERT = 1024
ALPHA = (2 * DEPTH) ** 0.25
LN_EPS = 1e-5

LANES = 128
ROW_SUBLANES = D_MODEL // (2 * LANES)
MOE_TILE = 256
TOK_TILE = 256
DMA_UNROLL = 8
SCALAR_UNROLL = 32
PLAN_UNROLL = 4
CAST_ROWS = 256
BF16 = jnp.bfloat16
F32 = jnp.float32
NEG_INF = float("-inf")


def _params(sem, vmem_mb):
    return pltpu.CompilerParams(dimension_semantics=sem, vmem_limit_bytes=vmem_mb * 1024 * 1024)


def _cast_rows(src_ref, dst_ref):
    n = src_ref.shape[0] // CAST_ROWS

    def body(i, c):
        r = pl.multiple_of(i * CAST_ROWS, CAST_ROWS)
        dst_ref[pl.ds(r, CAST_ROWS), :] = src_ref[pl.ds(r, CAST_ROWS), :].astype(BF16)
        return c

    lax.fori_loop(0, n, body, 0)


def _layer_norm(z, g, b):
    mu = jnp.mean(z, axis=-1, keepdims=True)
    zc = z - mu
    var = jnp.mean(zc * zc, axis=-1, keepdims=True)
    return zc * lax.rsqrt(var + LN_EPS) * g + b


def _store_packed_rows(x_bf, rows_ref):
    n = x_bf.shape[0]
    bits = pltpu.bitcast(x_bf.astype(F32), jnp.uint32)
    packed = bits[:, D_MODEL // 2:] | (bits[:, :D_MODEL // 2] >> 16)
    for c in range(ROW_SUBLANES):
        rows_ref[pl.ds(c, n, stride=ROW_SUBLANES), :] = packed[:, c * LANES:(c + 1) * LANES]


def _load_packed_rows(rows_ref, x_bf_ref):
    n = x_bf_ref.shape[0]
    for c in range(ROW_SUBLANES):
        words = rows_ref[pl.ds(c, n, stride=ROW_SUBLANES), :]
        low = pltpu.bitcast(words << 16, F32).astype(BF16)
        high = pltpu.bitcast(words & jnp.uint32(0xFFFF0000), F32).astype(BF16)
        x_bf_ref[:, c * LANES:(c + 1) * LANES] = low
        x_bf_ref[:, D_MODEL // 2 + c * LANES:D_MODEL // 2 + (c + 1) * LANES] = high


def _tok_operands(x):
    if isinstance(x, tuple):
        xp, xs = x
        d = xp.shape[1]
        last_p = xp.shape[0] // TOK_TILE - 1
        specs = [pl.BlockSpec((TOK_TILE, d), lambda i, *_: (jnp.minimum(i, last_p), 0)),
                 pl.BlockSpec((TOK_TILE, d), lambda i, *_: (0, 0))]
        return [xp, xs], specs, last_p + 2
    return [x], [pl.BlockSpec((TOK_TILE, x.shape[1]), lambda i, *_: (i, 0))], x.shape[0] // TOK_TILE


def _tok_load(refs):
    if len(refs) == 1:
        return refs[0][...]
    return jnp.where(pl.program_id(0) < pl.num_programs(0) - 1, refs[0][...], refs[1][...])


def _linear_kernel(*refs, n_x, act, scale):
    x_refs, (w_ref, b_ref, o_ref, wbf_ref) = refs[:n_x], refs[n_x:]

    @pl.when(pl.program_id(0) == 0)
    def _():
        _cast_rows(w_ref, wbf_ref)

    y = jnp.dot(_tok_load(x_refs).astype(BF16), wbf_ref[...], preferred_element_type=F32)
    y = y + b_ref[...]
    if act == "gelu":
        y = jax.nn.gelu(y)
    if scale != 1.0:
        y = y * scale
    o_ref[...] = y.astype(o_ref.dtype)


def _linear(x, w, w_index, b, out_dtype, *, name, act=None, scale=1.0):
    arrays, specs, nt = _tok_operands(x)
    k, nout = w.shape[-2:]
    w_block = (None,) * len(w_index) + (k, nout)
    return pl.pallas_call(
        functools.partial(_linear_kernel, n_x=len(arrays), act=act, scale=scale),
        grid=(nt,),
        in_specs=specs + [
            pl.BlockSpec(w_block, lambda i: w_index + (0, 0), pipeline_mode=pl.Buffered(1)),
            pl.BlockSpec((1, nout), lambda i: (0, 0)),
        ],
        out_specs=pl.BlockSpec((TOK_TILE, nout), lambda i: (i, 0)),
        out_shape=jax.ShapeDtypeStruct((nt * TOK_TILE, nout), out_dtype),
        scratch_shapes=[pltpu.VMEM((k, nout), BF16)],
        compiler_params=_params(("arbitrary",), 48),
        name=name,
    )(*arrays, w, b)


def _route(logits_t, b_router):
    aff = jax.nn.sigmoid(logits_t)
    sel = aff + b_router
    srow = [sel[e:e + 1, :] for e in range(N_EXPERTS)]
    arow = [aff[e:e + 1, :] for e in range(N_EXPERTS)]

    def top2_sum(v):
        pairs = [v[i] + v[j] for i in range(4) for j in range(i + 1, 4)]
        return functools.reduce(jnp.maximum, pairs)

    scores = [top2_sum(srow[4 * g:4 * g + 4]) for g in range(N_GROUPS)]
    best = scores[0]
    gi = jnp.zeros_like(best, dtype=jnp.int32)
    for g in range(1, N_GROUPS):
        upd = scores[g] > best
        best = jnp.where(upd, scores[g], best)
        gi = jnp.where(upd, g, gi)

    def pick_group(rows, j):
        out = rows[j]
        for g in range(1, N_GROUPS):
            out = jnp.where(gi == g, rows[4 * g + j], out)
        return out

    v = [pick_group(srow, j) for j in range(EXPERTS_PER_GROUP)]
    a = [pick_group(arow, j) for j in range(EXPERTS_PER_GROUP)]

    m1, i1 = v[0], jnp.zeros_like(gi)
    for j in range(1, EXPERTS_PER_GROUP):
        upd = v[j] > m1
        m1 = jnp.where(upd, v[j], m1)
        i1 = jnp.where(upd, j, i1)
    m2 = jnp.full_like(m1, NEG_INF)
    i2 = jnp.zeros_like(gi)
    for j in range(EXPERTS_PER_GROUP):
        cand = jnp.where(i1 == j, NEG_INF, v[j])
        upd = cand > m2
        m2 = jnp.where(upd, cand, m2)
        i2 = jnp.where(upd, j, i2)

    def pick_idx(rows, idx):
        out = rows[0]
        for j in range(1, EXPERTS_PER_GROUP):
            out = jnp.where(idx == j, rows[j], out)
        return out

    a1 = pick_idx(a, i1)
    a2 = pick_idx(a, i2)
    tot = a1 + a2
    e_idx = jnp.concatenate([gi * EXPERTS_PER_GROUP + i1, gi * EXPERTS_PER_GROUP + i2], axis=0)
    gates = jnp.concatenate([a1 / tot, a2 / tot], axis=0)
    return e_idx, gates


def _proj_ln_kernel(*refs, n_m, n_res):
    m_refs = refs[:n_m]
    w_ref, b_ref = refs[n_m:n_m + 2]
    res_refs = refs[n_m + 2:n_m + 2 + n_res]
    g_ref, beta_ref, wr_ref, br_ref, x_ref, xrow_ref, e_ref, gate_ref, wbf_ref = refs[n_m + 2 + n_res:]

    @pl.when(pl.program_id(0) == 0)
    def _():
        _cast_rows(w_ref, wbf_ref)

    y = jnp.dot(_tok_load(m_refs), wbf_ref[...], preferred_element_type=F32) + b_ref[...]
    x = _layer_norm(ALPHA * _tok_load(res_refs) + y, g_ref[...], beta_ref[...])
    x_ref[...] = x
    x_bf = x.astype(BF16)
    _store_packed_rows(x_bf, xrow_ref)
    logits_t = lax.dot_general(wr_ref[...].astype(BF16), x_bf,
                               (((1,), (1,)), ((), ())), preferred_element_type=F32)
    e_idx, gates = _route(logits_t, br_ref[...])
    e_ref[...] = e_idx
    gate_ref[...] = gates


def _proj_ln(m, w, w_index, b, res, g, beta, wr_t, br, *, name):
    m_arrays, m_specs, nt = _tok_operands(m)
    res_arrays, res_specs, _ = _tok_operands(res)
    k = w.shape[-2]
    tm = TOK_TILE
    n = nt * tm
    row = lambda i: (i, 0)
    const = lambda i: (0, 0)
    x, x_rows, e_idx, gates = pl.pallas_call(
        functools.partial(_proj_ln_kernel, n_m=len(m_arrays), n_res=len(res_arrays)),
        grid=(nt,),
        in_specs=m_specs + [
            pl.BlockSpec((None,) * len(w_index) + (k, D_MODEL), lambda i: w_index + (0, 0),
                         pipeline_mode=pl.Buffered(1)),
            pl.BlockSpec((1, D_MODEL), const),
        ] + res_specs + [
            pl.BlockSpec((1, D_MODEL), const),
            pl.BlockSpec((1, D_MODEL), const),
            pl.BlockSpec((N_EXPERTS, D_MODEL), const),
            pl.BlockSpec((N_EXPERTS, 1), const),
        ],
        out_specs=[
            pl.BlockSpec((tm, D_MODEL), row),
            pl.BlockSpec((tm * ROW_SUBLANES, LANES), row),
            pl.BlockSpec((None, 2, tm), lambda i: (i, 0, 0)),
            pl.BlockSpec((None, 2, tm), lambda i: (i, 0, 0)),
        ],
        out_shape=[
            jax.ShapeDtypeStruct((n, D_MODEL), F32),
            jax.ShapeDtypeStruct((n * ROW_SUBLANES, LANES), jnp.uint32),
            jax.ShapeDtypeStruct((nt, 2, tm), jnp.int32),
            jax.ShapeDtypeStruct((nt, 2, tm), F32),
        ],
        scratch_shapes=[pltpu.VMEM((k, D_MODEL), BF16)],
        compiler_params=_params(("arbitrary",), 48),
        name=name,
    )(*m_arrays, w, b, *res_arrays, g, beta, wr_t, br)
    e_idx = e_idx.transpose(1, 0, 2).reshape(2, n)
    gates = gates.transpose(1, 0, 2).reshape(2, n)
    return x, x_rows, e_idx, gates


def _plan_kernel(e_ref, pos_ref, meta_ref, rank_ref):
    nrow = e_ref.shape[0]
    ri = lax.broadcasted_iota(jnp.int32, (LANES, LANES), 0)
    ci = lax.broadcasted_iota(jnp.int32, (LANES, LANES), 1)
    tri = jnp.where(ri <= ci, 1.0, 0.0).astype(BF16)
    sub = lax.broadcasted_iota(jnp.int32, (N_EXPERTS, LANES), 0)

    def count_body(b, base):
        rows = [b * PLAN_UNROLL + u for u in range(PLAN_UNROLL)]
        onehots = [sub == e_ref[pl.ds(r, 1), :] for r in rows]
        locs = [jnp.dot(jnp.where(oh, 1.0, 0.0).astype(BF16), tri, preferred_element_type=F32)
                for oh in onehots]
        for r, onehot, loc in zip(rows, onehots, locs):
            rank_ref[pl.ds(r, 1), :] = jnp.sum(jnp.where(onehot, base + loc - 1.0, 0.0),
                                               axis=0, keepdims=True)
            base = base + jnp.broadcast_to(loc[:, LANES - 1:LANES], (N_EXPERTS, LANES))
        return base

    count = lax.fori_loop(0, nrow // PLAN_UNROLL, count_body, jnp.zeros((N_EXPERTS, LANES), F32))
    ntile = jnp.floor((count + (MOE_TILE - 1.0)) * (1.0 / MOE_TILE))
    offs = []
    acc = jnp.zeros((1, LANES), F32)
    for e in range(N_EXPERTS):
        offs.append(acc)
        acc = acc + ntile[e:e + 1, :]
    tile_off = jnp.concatenate(offs, axis=0)
    tile_end = tile_off + ntile
    lane = lax.broadcasted_iota(jnp.int32, (N_EXPERTS, LANES), 1).astype(F32)
    tile_expert = jnp.sum(jnp.where(tile_end <= lane, 1.0, 0.0), axis=0, keepdims=True)
    tile_expert = jnp.minimum(tile_expert, N_EXPERTS - 1.0)
    meta = jnp.concatenate([tile_expert, acc, jnp.zeros((6, LANES), F32)], axis=0)
    meta_ref[...] = meta.astype(jnp.int32)
    row_off = tile_off * float(MOE_TILE)

    def pos_body(r, c):
        onehot = sub == e_ref[pl.ds(r, 1), :]
        p = jnp.sum(jnp.where(onehot, row_off, 0.0), axis=0, keepdims=True) + rank_ref[pl.ds(r, 1), :]
        pos_ref[pl.ds(r, 1), :] = p.astype(jnp.int32)
        return c

    lax.fori_loop(0, nrow, pos_body, 0)


def _plan(e_idx, *, name):
    n2 = e_idx.shape[0] * e_idx.shape[1]
    assert n2 % (LANES * PLAN_UNROLL) == 0
    e2d = e_idx.reshape(n2 // LANES, LANES)
    pos, meta = pl.pallas_call(
        _plan_kernel,
        out_shape=[jax.ShapeDtypeStruct(e2d.shape, jnp.int32),
                   jax.ShapeDtypeStruct((8, LANES), jnp.int32)],
        scratch_shapes=[pltpu.VMEM(e2d.shape, F32)],
        name=name,
    )(e2d)
    return pos.reshape(n2), meta[0], meta[1, :1]


def _invert_kernel(pos_ref, pair_ref):
    n_rows = pair_ref.shape[0]
    n_pairs = pos_ref.shape[0]

    def fill_body(b, c):
        for u in range(SCALAR_UNROLL):
            pair_ref[b * SCALAR_UNROLL + u] = -1
        return c

    def pair_body(b, c):
        rows = [pos_ref[b * SCALAR_UNROLL + u] for u in range(SCALAR_UNROLL)]
        for u in range(SCALAR_UNROLL):
            pair_ref[rows[u]] = b * SCALAR_UNROLL + u
        return c

    lax.fori_loop(0, n_rows // SCALAR_UNROLL, fill_body, 0)
    lax.fori_loop(0, n_pairs // SCALAR_UNROLL, pair_body, 0)


def _invert(pos, n_rows, *, name):
    return pl.pallas_call(
        _invert_kernel,
        grid_spec=pltpu.PrefetchScalarGridSpec(
            num_scalar_prefetch=1,
            grid=(1,),
            in_specs=[],
            out_specs=pl.BlockSpec(memory_space=pltpu.SMEM),
        ),
        out_shape=jax.ShapeDtypeStruct((n_rows,), jnp.int32),
        name=name,
    )(pos)


def _expert_changed(te_ref, i):
    return jnp.logical_or(i == 0, te_ref[i] != te_ref[jnp.maximum(i - 1, 0)])


def _moe_up_kernel(pair_ref, te_ref, nu_ref, x_hbm, wg_ref, wu_ref, h_ref,
                   wg_bf, wu_bf, x_rows, x_bf, sem, *, n_tok, n_tiles):
    i = pl.program_id(0)
    nu = nu_ref[0]

    def row_copy(tile, r):
        p = pair_ref[tile * MOE_TILE + r]
        tok = jnp.where(p >= n_tok, p - n_tok, jnp.maximum(p, 0))
        src = pl.ds(pl.multiple_of(tok * ROW_SUBLANES, ROW_SUBLANES), ROW_SUBLANES)
        dst = pl.ds(r * ROW_SUBLANES, ROW_SUBLANES)
        return pltpu.make_async_copy(x_hbm.at[src, :], x_rows.at[dst, :], sem)

    def wait_tile():
        pltpu.make_async_copy(x_hbm.at[pl.ds(0, MOE_TILE * ROW_SUBLANES), :], x_rows, sem).wait()

    @pl.when(i == 0)
    def _():
        def body(rb, c):
            for u in range(DMA_UNROLL):
                row_copy(0, rb * DMA_UNROLL + u).start()
            return c
        lax.fori_loop(0, MOE_TILE // DMA_UNROLL, body, 0)

    @pl.when(i <= nu)
    def _():
        wait_tile()

    @pl.when(i < nu)
    def _():
        @pl.when(_expert_changed(te_ref, i))
        def _():
            _cast_rows(wg_ref, wg_bf)
            _cast_rows(wu_ref, wu_bf)

        _load_packed_rows(x_rows, x_bf)
        next_tile = jnp.minimum(i + 1, n_tiles - 1)
        for r in range(MOE_TILE):
            row_copy(next_tile, r).start(priority=r % 2)
        x = x_bf[...]
        a = jnp.dot(x, wg_bf[...], preferred_element_type=F32)
        b = jnp.dot(x, wu_bf[...], preferred_element_type=F32)
        h_ref[...] = (jax.nn.silu(a) * b).astype(BF16)

    @pl.when(jnp.logical_and(i == n_tiles - 1, i < nu))
    def _():
        wait_tile()

    @pl.when(i >= nu)
    def _():
        h_ref[...] = jnp.zeros_like(h_ref)


def _moe_down_kernel(pair_ref, te_ref, nu_ref, h_ref, wd_ref, out_hbm,
                     wd_bf, ya, yb, sems, tsem, *, n_tok, n_tiles):
    i = pl.program_id(0)
    nu = nu_ref[0]
    bufs = (ya, yb)
    trash = 2 * n_tok

    def row_copy(tile, r, buf, sem):
        p = pair_ref[tile * MOE_TILE + r]
        dst = jnp.where(p < 0, trash + r, p)
        return pltpu.make_async_copy(buf.at[pl.ds(r, 1), :], out_hbm.at[pl.ds(dst, 1), :], sem)

    def wait_tile(buf, sem):
        pltpu.make_async_copy(buf, out_hbm.at[pl.ds(0, MOE_TILE), :], sem).wait()

    @pl.when(i == 0)
    def _():
        yb[...] = jnp.zeros_like(yb)
        fill = pltpu.make_async_copy(yb, out_hbm.at[pl.ds(trash, MOE_TILE), :], tsem)
        fill.start()
        fill.wait()

    for parity in range(2):
        cur, prev = bufs[parity], bufs[1 - parity]
        cur_sem, prev_sem = sems.at[parity], sems.at[1 - parity]
        mine = i % 2 == parity

        @pl.when(jnp.logical_and(mine, jnp.logical_and(i >= 1, i - 1 <= nu)))
        def _():
            wait_tile(cur, cur_sem)

        @pl.when(jnp.logical_and(mine, i < nu))
        def _():
            @pl.when(_expert_changed(te_ref, i))
            def _():
                _cast_rows(wd_ref, wd_bf)

            prev_tile = jnp.maximum(i - 1, 0)
            for r in range(MOE_TILE):
                row_copy(prev_tile, r, prev, prev_sem).start(priority=r % 2)
            cur[...] = jnp.dot(h_ref[...], wd_bf[...], preferred_element_type=F32)

        @pl.when(jnp.logical_and(mine, i == nu))
        def _():
            def body(rb, c):
                for u in range(DMA_UNROLL):
                    row_copy(i - 1, rb * DMA_UNROLL + u, prev, prev_sem).start()
                return c
            lax.fori_loop(0, MOE_TILE // DMA_UNROLL, body, 0)

            @pl.when(i == n_tiles)
            def _():
                wait_tile(prev, prev_sem)


def _moe_ffn(x, pair, tile_expert, n_used, w_gate, w_up, w_down, layer):
    n_tok = x.shape[0] // ROW_SUBLANES
    n_tiles = pair.shape[0] // MOE_TILE

    def tile(i, pr, te, nu):
        return (jnp.minimum(i, nu[0] - 1), 0)

    def expert(i, pr, te, nu):
        return (layer, te[jnp.minimum(i, nu[0] - 1)], 0, 0)

    def weight_spec(k, n):
        return pl.BlockSpec((None, None, k, n), expert)

    row_bufs = [pltpu.VMEM((MOE_TILE, D_MODEL), F32), pltpu.VMEM((MOE_TILE, D_MODEL), F32),
                pltpu.SemaphoreType.DMA((2,))]
    h = pl.pallas_call(
        functools.partial(_moe_up_kernel, n_tok=n_tok, n_tiles=n_tiles),
        grid_spec=pltpu.PrefetchScalarGridSpec(
            num_scalar_prefetch=3,
            grid=(n_tiles,),
            in_specs=[pl.BlockSpec(memory_space=pl.ANY),
                      weight_spec(D_MODEL, D_EXPERT),
                      weight_spec(D_MODEL, D_EXPERT)],
            out_specs=pl.BlockSpec((MOE_TILE, D_EXPERT), lambda i, pr, te, nu: (i, 0)),
            scratch_shapes=[pltpu.VMEM((D_MODEL, D_EXPERT), BF16),
                            pltpu.VMEM((D_MODEL, D_EXPERT), BF16),
                            pltpu.VMEM((MOE_TILE * ROW_SUBLANES, LANES), jnp.uint32),
                            pltpu.VMEM((MOE_TILE, D_MODEL), BF16),
                            pltpu.SemaphoreType.DMA(())],
        ),
        out_shape=jax.ShapeDtypeStruct((n_tiles * MOE_TILE, D_EXPERT), BF16),
        compiler_params=_params(("arbitrary",), 56),
        name=f"moe_up_{layer}",
    )(pair, tile_expert, n_used, x, w_gate, w_up)
    return pl.pallas_call(
        functools.partial(_moe_down_kernel, n_tok=n_tok, n_tiles=n_tiles),
        grid_spec=pltpu.PrefetchScalarGridSpec(
            num_scalar_prefetch=3,
            grid=(n_tiles + 1,),
            in_specs=[pl.BlockSpec((MOE_TILE, D_EXPERT), tile),
                      weight_spec(D_EXPERT, D_MODEL)],
            out_specs=pl.BlockSpec(memory_space=pl.ANY),
            scratch_shapes=[pltpu.VMEM((D_EXPERT, D_MODEL), BF16)] + row_bufs
            + [pltpu.SemaphoreType.DMA(())],
        ),
        out_shape=jax.ShapeDtypeStruct((2 * n_tok + MOE_TILE, D_MODEL), F32),
        compiler_params=_params(("arbitrary",), 40),
        name=f"moe_down_{layer}",
    )(pair, tile_expert, n_used, h, w_down)


def _combine(y0_ref, y1_ref, res_ref, gate_ref, g_ref, beta_ref):
    gate = gate_ref[...]
    ffn = gate[:, 0:1] * y0_ref[...] + gate[:, 1:2] * y1_ref[...]
    return _layer_norm(ALPHA * res_ref[...] + ffn, g_ref[...], beta_ref[...])


def _combine_specs(n):
    nt = n // TOK_TILE
    row = lambda i: (i, 0)
    const = lambda i: (0, 0)
    return [pl.BlockSpec((TOK_TILE, D_MODEL), row),
            pl.BlockSpec((TOK_TILE, D_MODEL), lambda i: (i + nt, 0)),
            pl.BlockSpec((TOK_TILE, D_MODEL), row),
            pl.BlockSpec((TOK_TILE, 2), row),
            pl.BlockSpec((1, D_MODEL), const),
            pl.BlockSpec((1, D_MODEL), const)]


def _combine_split_kernel(y0_ref, y1_ref, res_ref, gate_ref, g_ref, beta_ref, prompt_ref, sample_ref):
    x = _combine(y0_ref, y1_ref, res_ref, gate_ref, g_ref, beta_ref)
    is_sample = pl.program_id(0) == pl.num_programs(0) - 1

    @pl.when(jnp.logical_not(is_sample))
    def _():
        prompt_ref[...] = x

    @pl.when(is_sample)
    def _():
        sample_ref[...] = x


def _combine_split(ys, res, gates_col, g, beta, *, name):
    n = res.shape[0]
    tm = TOK_TILE
    nt = n // tm
    return pl.pallas_call(
        _combine_split_kernel,
        grid=(nt,),
        in_specs=_combine_specs(n),
        out_specs=[pl.BlockSpec((tm, D_MODEL), lambda i: (jnp.minimum(i, nt - 2), 0)),
                   pl.BlockSpec((tm, D_MODEL), lambda i: (0, 0))],
        out_shape=[jax.ShapeDtypeStruct((n - tm, D_MODEL), F32),
                   jax.ShapeDtypeStruct((tm, D_MODEL), F32)],
        compiler_params=_params(("arbitrary",), 40),
        name=name,
    )(ys, ys, res, gates_col, g, beta)


def _load_weight(w_hbm, wbf_ref, stage_ref, sems):
    rows = stage_ref.shape[1]
    n_chunks = wbf_ref.shape[0] // rows

    def chunk_copy(c):
        return pltpu.make_async_copy(w_hbm.at[pl.ds(c * rows, rows), :], stage_ref.at[c % 2],
                                     sems.at[c % 2])

    chunk_copy(0).start()
    for c in range(n_chunks):
        if c + 1 < n_chunks:
            chunk_copy(c + 1).start()
        chunk_copy(c).wait()
        wbf_ref[c * rows:(c + 1) * rows, :] = stage_ref[c % 2].astype(BF16)


def _combine_qkv_kernel(y0_ref, y1_ref, res_ref, gate_ref, g_ref, beta_ref, wq_hbm, wkv_hbm,
                        x_ref, q_ref, kv_ref, wq_bf, wkv_bf, stage_q, stage_kv, sems, *, wq_index):
    @pl.when(pl.program_id(0) == 0)
    def _():
        wq = wq_hbm
        for k in wq_index:
            wq = wq.at[k]
        _load_weight(wq, wq_bf, stage_q, sems)
        _load_weight(wkv_hbm, wkv_bf, stage_kv, sems)

    x = _combine(y0_ref, y1_ref, res_ref, gate_ref, g_ref, beta_ref)
    x_ref[...] = x
    x_bf = x.astype(BF16)
    q = jnp.dot(x_bf, wq_bf[...], preferred_element_type=F32) * (HEAD_DIM ** -0.5)
    q_ref[...] = q.astype(BF16)
    kv_ref[...] = jnp.dot(x_bf, wkv_bf[...], preferred_element_type=F32)


def _combine_qkv(ys, res, gates_col, g, beta, w_q, wq_index, w_kv, *, name):
    n = res.shape[0]
    tm = TOK_TILE
    row = lambda i: (i, 0)
    hbm = pl.BlockSpec(memory_space=pl.ANY)
    return pl.pallas_call(
        functools.partial(_combine_qkv_kernel, wq_index=wq_index),
        grid=(n // tm,),
        in_specs=_combine_specs(n) + [hbm, hbm],
        out_specs=[pl.BlockSpec((tm, D_MODEL), row),
                   pl.BlockSpec((tm, D_MODEL), row),
                   pl.BlockSpec((tm, 2 * KV_DIM), row)],
        out_shape=[jax.ShapeDtypeStruct((n, D_MODEL), F32),
                   jax.ShapeDtypeStruct((n, D_MODEL), BF16),
                   jax.ShapeDtypeStruct((n, 2 * KV_DIM), F32)],
        scratch_shapes=[pltpu.VMEM((D_MODEL, D_MODEL), BF16),
                        pltpu.VMEM((D_MODEL, 2 * KV_DIM), BF16),
                        pltpu.VMEM((2, CAST_ROWS, D_MODEL), F32),
                        pltpu.VMEM((2, CAST_ROWS, 2 * KV_DIM), F32),
                        pltpu.SemaphoreType.DMA((2,))],
        compiler_params=_params(("arbitrary",), 52),
        name=name,
    )(ys, ys, res, gates_col, g, beta, w_q, w_kv)


def _moe_block(x_rows, e_idx, w_gate, w_up, w_down, layer):
    n = x_rows.shape[0] // ROW_SUBLANES
    n_tiles = -(-(2 * n + N_EXPERTS * (MOE_TILE - 1)) // MOE_TILE)
    pos, tile_expert, n_used = _plan(e_idx, name=f"moe_plan_{layer}")
    pair = _invert(pos, n_tiles * MOE_TILE, name=f"moe_invert_{layer}")
    return _moe_ffn(x_rows, pair, tile_expert, n_used, w_gate, w_up, w_down, layer)


def _sigmoid(x):
    return 0.5 * jnp.tanh(0.5 * x) + 0.5


def _log_sigmoid(x):
    return -(jnp.maximum(-x, 0.0) + jnp.log1p(jnp.exp(-jnp.abs(x))))


def _lru_gate_block(xc, n, wrg_bf, wig_bf, brg_ref, big_ref, lam_ref):
    cols = slice(n * LRU_BLOCK, (n + 1) * LRU_BLOCK)
    xb = xc.astype(BF16)
    r = _sigmoid(jnp.dot(xb, wrg_bf[n], preferred_element_type=F32) + brg_ref[:, cols])
    i = _sigmoid(jnp.dot(xb, wig_bf[n], preferred_element_type=F32) + big_ref[:, cols])
    log_a = LRU_C * r * _log_sigmoid(lam_ref[:, cols])
    a = jnp.exp(log_a)
    u = xc * i * jnp.sqrt(-jnp.tanh(log_a) * (a * a + 1.0))
    return a, u


def _cast_gate_weights(wrg_ref, wig_ref, wrg_bf, wig_bf):
    for n in range(LRU_BLOCKS):
        wrg_bf[n] = wrg_ref[n].astype(BF16)
        wig_bf[n] = wig_ref[n].astype(BF16)


def _lru_prompt_kernel(xb_ref, yb_ref, cw_ref, cb_ref, wrg_ref, wig_ref, brg_ref, big_ref, lam_ref,
                       m_ref, conv_ref, hlast_ref, xpad, a_s, u_s, h_s, wrg_bf, wig_bf):
    b = pl.program_id(0)
    j = pl.program_id(1)
    tt = xb_ref.shape[0]

    @pl.when(jnp.logical_and(b == 0, j == 0))
    def _():
        _cast_gate_weights(wrg_ref, wig_ref, wrg_bf, wig_bf)

    @pl.when(j == 0)
    def _():
        xpad[0:8, :] = jnp.zeros((8, D_MODEL), F32)
        h_s[...] = jnp.zeros_like(h_s)

    xpad[8:8 + tt, :] = xb_ref[...]
    for n in range(LRU_BLOCKS):
        cols = slice(n * LRU_BLOCK, (n + 1) * LRU_BLOCK)
        xc = cb_ref[:, cols] + cw_ref[0:1, cols] * xpad[5:5 + tt, cols]
        for k in range(1, CONV_WIDTH):
            xc = xc + cw_ref[k:k + 1, cols] * xpad[5 + k:5 + k + tt, cols]
        a, u = _lru_gate_block(xc, n, wrg_bf, wig_bf, brg_ref, big_ref, lam_ref)
        a_s[:, cols] = a
        u_s[:, cols] = u

    def scan_body(gidx, h):
        base = pl.multiple_of(gidx * 8, 8)
        a8 = a_s[pl.ds(base, 8), :]
        u8 = u_s[pl.ds(base, 8), :]
        rows = []
        for s in range(8):
            h = a8[s:s + 1, :] * h + u8[s:s + 1, :]
            rows.append(h)
        a_s[pl.ds(base, 8), :] = jnp.concatenate(rows, axis=0)
        return h

    h = lax.fori_loop(0, tt // 8, scan_body, h_s[...])
    h_s[...] = h
    m_ref[...] = (a_s[...] * yb_ref[...].astype(F32)).astype(BF16)
    xpad[0:8, :] = xpad[tt:tt + 8, :]

    @pl.when(j == pl.num_programs(1) - 1)
    def _():
        conv_ref[...] = xpad[5:8, :]
        hlast_ref[...] = h


def _lru_prompt(xb, yb, batch, seq, cw, cb, wrg, wig, brg, big, lam, *, tt=256):
    nj = seq // tt
    row = lambda b, j: (b * nj + j, 0)
    const2 = lambda b, j: (0, 0)
    const3 = lambda b, j: (0, 0, 0)
    return pl.pallas_call(
        _lru_prompt_kernel,
        grid=(batch, nj),
        in_specs=[
            pl.BlockSpec((tt, D_MODEL), row),
            pl.BlockSpec((tt, D_MODEL), row),
            pl.BlockSpec((CONV_WIDTH, D_MODEL), const2),
            pl.BlockSpec((1, D_MODEL), const2),
            pl.BlockSpec((LRU_BLOCKS, LRU_BLOCK, LRU_BLOCK), const3),
            pl.BlockSpec((LRU_BLOCKS, LRU_BLOCK, LRU_BLOCK), const3),
            pl.BlockSpec((1, D_MODEL), const2),
            pl.BlockSpec((1, D_MODEL), const2),
            pl.BlockSpec((1, D_MODEL), const2),
        ],
        out_specs=[
            pl.BlockSpec((tt, D_MODEL), row),
            pl.BlockSpec((None, CONV_WIDTH - 1, D_MODEL), lambda b, j: (b, 0, 0)),
            pl.BlockSpec((None, 1, D_MODEL), lambda b, j: (b, 0, 0)),
        ],
        out_shape=[
            jax.ShapeDtypeStruct((batch * seq, D_MODEL), BF16),
            jax.ShapeDtypeStruct((batch, CONV_WIDTH - 1, D_MODEL), F32),
            jax.ShapeDtypeStruct((batch, 1, D_MODEL), F32),
        ],
        scratch_shapes=[
            pltpu.VMEM((tt + 8, D_MODEL), F32),
            pltpu.VMEM((tt, D_MODEL), F32),
            pltpu.VMEM((tt, D_MODEL), F32),
            pltpu.VMEM((1, D_MODEL), F32),
            pltpu.VMEM((LRU_BLOCKS, LRU_BLOCK, LRU_BLOCK), BF16),
            pltpu.VMEM((LRU_BLOCKS, LRU_BLOCK, LRU_BLOCK), BF16),
        ],
        compiler_params=_params(("arbitrary", "arbitrary"), 40),
        name="lru_prompt",
    )(xb, yb, cw, cb, wrg, wig, brg, big, lam)


def _lru_sample_kernel(xb_ref, yb_ref, cs_ref, h0_ref, cw_ref, cb_ref, wrg_ref, wig_ref,
                       brg_ref, big_ref, lam_ref, m_ref, conv_ref, hlast_ref, wrg_bf, wig_bf, *, steps):
    batch = h0_ref.shape[0]
    _cast_gate_weights(wrg_ref, wig_ref, wrg_bf, wig_bf)
    m_ref[steps * batch:, :] = jnp.zeros((m_ref.shape[0] - steps * batch, D_MODEL), BF16)

    def slab(t, cols):
        if t < CONV_WIDTH - 1:
            return cs_ref[t, :, cols]
        t -= CONV_WIDTH - 1
        return xb_ref[t * batch:(t + 1) * batch, cols]

    for n in range(LRU_BLOCKS):
        cols = slice(n * LRU_BLOCK, (n + 1) * LRU_BLOCK)
        h = h0_ref[:, cols]
        for t in range(steps):
            xc = cb_ref[:, cols] + cw_ref[0:1, cols] * slab(t, cols)
            for k in range(1, CONV_WIDTH):
                xc = xc + cw_ref[k:k + 1, cols] * slab(t + k, cols)
            a, u = _lru_gate_block(xc, n, wrg_bf, wig_bf, brg_ref, big_ref, lam_ref)
            h = a * h + u
            rows = slice(t * batch, (t + 1) * batch)
            m_ref[rows, cols] = (h * yb_ref[rows, cols].astype(F32)).astype(BF16)
        hlast_ref[:, cols] = h
    for k in range(CONV_WIDTH - 1):
        conv_ref[k] = slab(steps + k, slice(None))


def _lru_sample(xb, yb, tile, steps, conv_state, h0, cw, cb, wrg, wig, brg, big, lam):
    batch = h0.shape[0]
    tok = pl.BlockSpec((TOK_TILE, D_MODEL), lambda i: (tile, 0))
    full = lambda a: pl.BlockSpec(a.shape, lambda i: (0,) * a.ndim)
    small = (conv_state, h0, cw, cb, wrg, wig, brg, big, lam)
    return pl.pallas_call(
        functools.partial(_lru_sample_kernel, steps=steps),
        grid=(1,),
        in_specs=[tok, tok] + [full(a) for a in small],
        out_specs=[
            pl.BlockSpec((TOK_TILE, D_MODEL), lambda i: (0, 0)),
            pl.BlockSpec((CONV_WIDTH - 1, batch, D_MODEL), lambda i: (0, 0, 0)),
            pl.BlockSpec((batch, D_MODEL), lambda i: (0, 0)),
        ],
        out_shape=[
            jax.ShapeDtypeStruct((TOK_TILE, D_MODEL), BF16),
            jax.ShapeDtypeStruct((CONV_WIDTH - 1, batch, D_MODEL), F32),
            jax.ShapeDtypeStruct((batch, D_MODEL), F32),
        ],
        scratch_shapes=[
            pltpu.VMEM((LRU_BLOCKS, LRU_BLOCK, LRU_BLOCK), BF16),
            pltpu.VMEM((LRU_BLOCKS, LRU_BLOCK, LRU_BLOCK), BF16),
        ],
        compiler_params=_params(("arbitrary",), 32),
        name="lru_sample",
    )(xb, yb, *small)


def _rel_bucket(dist):
    n = jnp.maximum(dist, 0)
    max_exact = N_BUCKETS // 2
    nf = jnp.maximum(n, 1).astype(F32)
    large = max_exact + (jnp.log(nf / max_exact) / math.log(MAX_DISTANCE / max_exact)
                         * (N_BUCKETS - max_exact)).astype(jnp.int32)
    large = jnp.minimum(large, N_BUCKETS - 1)
    return jnp.where(n < max_exact, n, large)


def _masked_buckets(dist):
    valid = (dist >= 0) & (dist < WINDOW)
    return jnp.where(valid, _rel_bucket(dist), -1).astype(jnp.int32)


def _build_bias(bucket, tab_ref, head):
    def body(bi, acc):
        return jnp.where(bucket == bi, tab_ref[bi * N_HEADS + head], acc)
    return lax.fori_loop(0, N_BUCKETS, body, jnp.full(bucket.shape, NEG_INF, F32))


def _softmax_pv(s, sink, v):
    m = jnp.maximum(jnp.max(s, axis=-1, keepdims=True), sink)
    p = jnp.exp(s - m)
    den = jnp.sum(p, axis=-1, keepdims=True) + jnp.exp(sink - m)
    return jnp.dot(p.astype(BF16), v, preferred_element_type=F32) / den


def _attn_prompt_kernel(q_ref, kvp_ref, kvc_ref, bucket_ref, tab_ref, sink_ref, o_ref, bias_s):
    b = pl.program_id(0)
    n = pl.program_id(1)

    @pl.when(jnp.logical_and(b == 0, n == 0))
    def _():
        bucket = bucket_ref[...]

        col = lax.broadcasted_iota(jnp.int32, (WINDOW, 2 * WINDOW), 1)

        def head_body(h, c):
            bias = _build_bias(bucket, tab_ref, h)
            sink = sink_ref[h]
            g = h // GROUP
            r0 = pl.multiple_of((h % GROUP) * WINDOW, WINDOW)
            bias_s[0, g, pl.ds(r0, WINDOW), :] = jnp.where(col == 0, sink, bias)
            bias_s[1, g, pl.ds(r0, WINDOW), :] = jnp.where(
                col == 0, sink, jnp.where(col < WINDOW, NEG_INF, bias))
            return c

        lax.fori_loop(0, N_HEADS, head_body, 0)

    first = (n == 0).astype(jnp.int32)
    row = lax.broadcasted_iota(jnp.int32, kvp_ref.shape, 0)
    kv_prev = jnp.where(row == 0, 0.0, kvp_ref[...])
    kv = jnp.concatenate([kv_prev, kvc_ref[...]], axis=0).astype(BF16)
    ones = jnp.ones((2 * WINDOW, HEAD_DIM), BF16)
    lane = lax.broadcasted_iota(jnp.int32, (WINDOW, 2 * HEAD_DIM), 1)
    for g in range(N_KV_HEADS):
        heads = range(g * GROUP, (g + 1) * GROUP)
        kg = kv[:, g * HEAD_DIM:(g + 1) * HEAD_DIM]
        vg = kv[:, KV_DIM + g * HEAD_DIM:KV_DIM + (g + 1) * HEAD_DIM]
        v_ext = jnp.concatenate([vg, ones], axis=1)
        qg = jnp.concatenate([q_ref[:, h * HEAD_DIM:(h + 1) * HEAD_DIM] for h in heads], axis=0)
        s = lax.dot_general(qg, kg, (((1,), (1,)), ((), ())), preferred_element_type=F32)
        s = s + bias_s[first, g]
        p = jnp.exp(s - jnp.max(s, axis=-1, keepdims=True)).astype(BF16)
        o_ext = jnp.dot(p, v_ext, preferred_element_type=F32)
        o_rot = pltpu.roll(o_ext, HEAD_DIM, axis=1)
        for pair in range(GROUP // 2):
            r0 = slice(2 * pair * WINDOW, (2 * pair + 1) * WINDOW)
            r1 = slice((2 * pair + 1) * WINDOW, (2 * pair + 2) * WINDOW)
            even = o_ext[r0] * (1.0 / o_rot[r0])
            odd = o_rot[r1] * (1.0 / o_ext[r1])
            c0 = (g * GROUP + 2 * pair) * HEAD_DIM
            o_ref[:, c0:c0 + 2 * HEAD_DIM] = jnp.where(lane < HEAD_DIM, even, odd).astype(BF16)


def _attn_prompt(q, kv, batch, seq, bucket, tab, sinks):
    nb = seq // WINDOW
    smem = pl.BlockSpec(memory_space=pltpu.SMEM)
    return pl.pallas_call(
        _attn_prompt_kernel,
        grid=(batch, nb),
        in_specs=[
            pl.BlockSpec((WINDOW, D_MODEL), lambda b, n: (b * nb + n, 0)),
            pl.BlockSpec((WINDOW, 2 * KV_DIM), lambda b, n: (jnp.maximum(b * nb + n - 1, 0), 0)),
            pl.BlockSpec((WINDOW, 2 * KV_DIM), lambda b, n: (b * nb + n, 0)),
            pl.BlockSpec((WINDOW, 2 * WINDOW), lambda b, n: (0, 0)),
            smem, smem,
        ],
        out_specs=pl.BlockSpec((WINDOW, D_MODEL), lambda b, n: (b * nb + n, 0)),
        out_shape=jax.ShapeDtypeStruct((batch * seq, D_MODEL), BF16),
        scratch_shapes=[pltpu.VMEM((2, N_KV_HEADS, GROUP * WINDOW, 2 * WINDOW), F32)],
        compiler_params=_params(("arbitrary", "arbitrary"), 32),
        name="attn_prompt",
    )(q, kv, kv, bucket, tab, sinks)


def _attn_sample_kernel(q_ref, k_ref, v_ref, bucket_ref, tab_ref, sink_ref, o_ref, bias_s):
    steps = q_ref.shape[0]

    @pl.when(pl.program_id(0) == 0)
    def _():
        bucket = bucket_ref[...]

        def head_body(h, c):
            bias_s[h] = _build_bias(bucket, tab_ref, h)
            return c

        lax.fori_loop(0, N_HEADS, head_body, 0)

    rows = lax.broadcasted_iota(jnp.int32, (GROUP * steps, 1), 0)
    for g in range(N_KV_HEADS):
        kg = k_ref[:, g * HEAD_DIM:(g + 1) * HEAD_DIM].astype(BF16)
        vg = v_ref[:, g * HEAD_DIM:(g + 1) * HEAD_DIM].astype(BF16)
        heads = range(g * GROUP, (g + 1) * GROUP)
        qg = jnp.concatenate([q_ref[:, h * HEAD_DIM:(h + 1) * HEAD_DIM] for h in heads], axis=0)
        bias = jnp.concatenate([bias_s[h] for h in heads], axis=0)
        sink = jnp.full((GROUP * steps, 1), sink_ref[g * GROUP], F32)
        for hh in range(1, GROUP):
            sink = jnp.where(rows >= hh * steps, sink_ref[g * GROUP + hh], sink)
        s = lax.dot_general(qg, kg, (((1,), (1,)), ((), ())), preferred_element_type=F32) + bias
        o = _softmax_pv(s, sink, vg)
        for hh, h in enumerate(heads):
            o_ref[:, h * HEAD_DIM:(h + 1) * HEAD_DIM] = o[hh * steps:(hh + 1) * steps].astype(BF16)


def _attn_sample(q, k_all, v_all, bucket, tab, sinks):
    batch, steps, _ = q.shape
    lk = k_all.shape[1]
    smem = pl.BlockSpec(memory_space=pltpu.SMEM)
    return pl.pallas_call(
        _attn_sample_kernel,
        grid=(batch,),
        in_specs=[
            pl.BlockSpec((None, steps, D_MODEL), lambda b: (b, 0, 0)),
            pl.BlockSpec((None, lk, KV_DIM), lambda b: (b, 0, 0)),
            pl.BlockSpec((None, lk, KV_DIM), lambda b: (b, 0, 0)),
            pl.BlockSpec((steps, lk), lambda b: (0, 0)),
            smem, smem,
        ],
        out_specs=pl.BlockSpec((None, steps, D_MODEL), lambda b: (b, 0, 0)),
        out_shape=jax.ShapeDtypeStruct((batch, steps, D_MODEL), BF16),
        scratch_shapes=[pltpu.VMEM((N_HEADS, steps, lk), F32)],
        compiler_params=_params(("arbitrary",), 32),
        name="attn_sample",
    )(q, k_all, v_all, bucket, tab, sinks)


def kernel(x_prompt, x_sample, state_conv, state_rnn, cache_k_win, cache_v_win, ln_g, ln_b, lru_w_x, lru_b_x, lru_w_y, lru_b_y, lru_conv_w, lru_conv_b, lru_w_rg, lru_b_rg, lru_w_ig, lru_b_ig, lru_lam, lru_w_out, lru_b_out, attn_w_kv, attn_w_q, attn_w_o, attn_sinks, rel_bias, moe_w_router, moe_b_router, moe_w_gate, moe_w_up, moe_w_down):
    bp, seq, _ = x_prompt.shape
    bs, steps, _ = x_sample.shape
    n_p = bp * seq
    n_s = bs * steps

    assert n_p % TOK_TILE == 0 and n_s <= TOK_TILE
    sample_tile = n_p // TOK_TILE

    def pad_tile(rows):
        return jnp.pad(rows, ((0, TOK_TILE - n_s), (0, 0)))

    x0 = (x_prompt.reshape(n_p, D_MODEL),
          pad_tile(x_sample.transpose(1, 0, 2).reshape(n_s, D_MODEL)))
    wr_t = moe_w_router.T
    br = moe_b_router.reshape(N_EXPERTS, 1)
    vec = lambda a: a.reshape(1, -1)

    xb = _linear(x0, lru_w_x, (0,), vec(lru_b_x[0]), F32, name="lru_in_x")
    yb = _linear(x0, lru_w_y, (0,), vec(lru_b_y[0]), BF16, act="gelu", name="lru_in_y")
    lru_args = (lru_conv_w[0], vec(lru_conv_b[0]), lru_w_rg[0], lru_w_ig[0],
                vec(lru_b_rg[0]), vec(lru_b_ig[0]), vec(lru_lam[0]))
    m_p, conv_p, rnn_p = _lru_prompt(xb, yb, bp, seq, *lru_args)
    m_s, conv_s, rnn_s = _lru_sample(xb, yb, sample_tile, steps,
                                     state_conv[0].transpose(1, 0, 2), state_rnn[0], *lru_args)
    x1, x1_rows, e_idx, gates = _proj_ln((m_p, m_s), lru_w_out, (0,), vec(lru_b_out[0]), x0,
                                vec(ln_g[0, 0]), vec(ln_b[0, 0]), wr_t, br, name="lru_out_ln")
    ys = _moe_block(x1_rows, e_idx, moe_w_gate, moe_w_up, moe_w_down, 0)

    x2, q, kv = _combine_qkv(ys, x1, gates.T, vec(ln_g[0, 1]), vec(ln_b[0, 1]),
                             attn_w_q, (0,), attn_w_kv, name="moe_combine_qkv")
    tab = rel_bias.reshape(-1)
    sinks = attn_sinks[0]
    qi = jnp.arange(WINDOW)[:, None]
    kj = jnp.arange(2 * WINDOW)[None, :]
    o_p = _attn_prompt(q, kv, bp, seq, _masked_buckets(qi + WINDOW - kj), tab, sinks)
    kv_s = kv[n_p:n_p + n_s].reshape(steps, bs, 2, KV_DIM).transpose(2, 1, 0, 3)
    k_all = jnp.concatenate([cache_k_win.reshape(bs, WINDOW, KV_DIM), kv_s[0]], axis=1)
    v_all = jnp.concatenate([cache_v_win.reshape(bs, WINDOW, KV_DIM), kv_s[1]], axis=1)
    dist_s = jnp.arange(steps)[:, None] + WINDOW - jnp.arange(WINDOW + steps)[None, :]
    q_s = q[n_p:n_p + n_s].reshape(steps, bs, D_MODEL).transpose(1, 0, 2)
    o_s = _attn_sample(q_s, k_all, v_all, _masked_buckets(dist_s), tab, sinks)
    o_s = pad_tile(o_s.transpose(1, 0, 2).reshape(n_s, D_MODEL))
    x3, x3_rows, e_idx, gates = _proj_ln((o_p, o_s), attn_w_o, (0,), jnp.zeros((1, D_MODEL), F32), x2,
                                vec(ln_g[1, 0]), vec(ln_b[1, 0]), wr_t, br, name="attn_out_ln")
    ys = _moe_block(x3_rows, e_idx, moe_w_gate, moe_w_up, moe_w_down, 1)
    y_p, y_s = _combine_split(ys, x3, gates.T, vec(ln_g[1, 1]), vec(ln_b[1, 1]), name="moe_combine_1")

    y_prompt = y_p.reshape(bp, seq, D_MODEL)
    y_sample = y_s[:n_s].reshape(steps, bs, D_MODEL).transpose(1, 0, 2)
    kv_p = jnp.stack([kv[(b + 1) * seq - WINDOW:(b + 1) * seq] for b in range(bp)])
    kv_p = kv_p.reshape(bp, WINDOW, 2, N_KV_HEADS, HEAD_DIM)
    k_win_s = k_all[:, steps:].reshape(bs, WINDOW, N_KV_HEADS, HEAD_DIM)
    v_win_s = v_all[:, steps:].reshape(bs, WINDOW, N_KV_HEADS, HEAD_DIM)
    return (y_prompt, y_sample,
            conv_p[None], rnn_p.reshape(1, bp, D_MODEL),
            kv_p[:, :, 0], kv_p[:, :, 1],
            conv_s.transpose(1, 0, 2)[None], rnn_s[None],
            k_win_s, v_win_s)
```

```python
import functools
import math

import jax
import jax.numpy as jnp
from jax import lax
from jax.experimental import pallas as pl
from jax.experimental.pallas import tpu as pltpu

D_MODEL = 2048
DEPTH = 2
LRU_BLOCKS = 8
LRU_BLOCK = D_MODEL // LRU_BLOCKS
CONV_WIDTH = 4
LRU_C = 8.0
N_HEADS = 32
HEAD_DIM = 64
N_KV_HEADS = 8
GROUP = N_HEADS // N_KV_HEADS
KV_DIM = N_KV_HEADS * HEAD_DIM
WINDOW = 128
N_BUCKETS = 32
MAX_DISTANCE = 128
N_EXPERTS = 16
N_GROUPS = 4
EXPERTS_PER_GROUP = N_EXPERTS // N_GROUPS
D_EXPERT = 1024
ALPHA = (2 * DEPTH) ** 0.25
LN_EPS = 1e-5

LANES = 128
SEGS = 8
CHUNKS = D_MODEL // LANES
ROW_SUBLANES = D_MODEL // (2 * LANES)
MOE_TILE = 256
TOK_TILE = 256
DMA_UNROLL = 8
SCALAR_UNROLL = 32
PLAN_UNROLL = 4
CAST_ROWS = 256
BF16 = jnp.bfloat16
F32 = jnp.float32
NEG_INF = float("-inf")


def _params(sem, vmem_mb):
    return pltpu.CompilerParams(dimension_semantics=sem, vmem_limit_bytes=vmem_mb * 1024 * 1024)


def _cast_rows(src_ref, dst_ref):
    n = src_ref.shape[0] // CAST_ROWS

    def body(i, c):
        r = pl.multiple_of(i * CAST_ROWS, CAST_ROWS)
        dst_ref[pl.ds(r, CAST_ROWS), :] = src_ref[pl.ds(r, CAST_ROWS), :].astype(BF16)
        return c

    lax.fori_loop(0, n, body, 0)


def _layer_norm(z, g, b):
    mu = jnp.mean(z, axis=-1, keepdims=True)
    zc = z - mu
    var = jnp.mean(zc * zc, axis=-1, keepdims=True)
    return zc * lax.rsqrt(var + LN_EPS) * g + b


def _store_packed_rows(x_bf, rows_ref):
    n = x_bf.shape[0]
    bits = pltpu.bitcast(x_bf.astype(F32), jnp.uint32)
    packed = bits[:, D_MODEL // 2:] | (bits[:, :D_MODEL // 2] >> 16)
    for c in range(ROW_SUBLANES):
        rows_ref[pl.ds(c, n, stride=ROW_SUBLANES), :] = packed[:, c * LANES:(c + 1) * LANES]


def _load_packed_rows(rows_ref, x_bf_ref):
    n = x_bf_ref.shape[0]
    for c in range(ROW_SUBLANES):
        words = rows_ref[pl.ds(c, n, stride=ROW_SUBLANES), :]
        low = pltpu.bitcast(words << 16, F32).astype(BF16)
        high = pltpu.bitcast(words & jnp.uint32(0xFFFF0000), F32).astype(BF16)
        x_bf_ref[:, c * LANES:(c + 1) * LANES] = low
        x_bf_ref[:, D_MODEL // 2 + c * LANES:D_MODEL // 2 + (c + 1) * LANES] = high


def _tok_operands(x, tile_of=lambda i: i):
    if isinstance(x, tuple):
        xp, xs = x
        d = xp.shape[1]
        last_p = xp.shape[0] // TOK_TILE - 1
        specs = [pl.BlockSpec((TOK_TILE, d), lambda i, *_: (jnp.minimum(tile_of(i), last_p), 0)),
                 pl.BlockSpec((TOK_TILE, d), lambda i, *_: (0, 0))]
        return [xp, xs], specs, last_p + 2
    return ([x], [pl.BlockSpec((TOK_TILE, x.shape[1]), lambda i, *_: (tile_of(i), 0))],
            x.shape[0] // TOK_TILE)


def _tok_load(refs, is_sample=None):
    if len(refs) == 1:
        return refs[0][...]
    if is_sample is None:
        is_sample = pl.program_id(0) == pl.num_programs(0) - 1
    return jnp.where(is_sample, refs[1][...], refs[0][...])


def _linear_kernel(*refs, n_x, act, scale, chunk_rows):
    x_refs, (w_ref, b_ref, o_ref, wbf_ref) = refs[:n_x], refs[n_x:]

    @pl.when(pl.program_id(0) == 0)
    def _():
        _cast_rows(w_ref, wbf_ref)

    y = jnp.dot(_tok_load(x_refs).astype(BF16), wbf_ref[...], preferred_element_type=F32)
    y = y + b_ref[...]
    if act == "gelu":
        y = jax.nn.gelu(y)
    if scale != 1.0:
        y = y * scale
    if chunk_rows:
        n_chunks = y.shape[1] // LANES
        for c in range(n_chunks):
            o_ref[pl.ds(c, y.shape[0], stride=n_chunks), :] = y[:, c * LANES:(c + 1) * LANES]
    else:
        o_ref[...] = y.astype(o_ref.dtype)


def _linear(x, w, w_index, b, out_dtype, *, name, act=None, scale=1.0, chunk_rows=False):
    arrays, specs, nt = _tok_operands(x)
    k, nout = w.shape[-2:]
    w_block = (None,) * len(w_index) + (k, nout)
    if chunk_rows:
        out_block, out_rows, out_cols = (TOK_TILE * (nout // LANES), LANES), nt * TOK_TILE * (nout // LANES), LANES
    else:
        out_block, out_rows, out_cols = (TOK_TILE, nout), nt * TOK_TILE, nout
    return pl.pallas_call(
        functools.partial(_linear_kernel, n_x=len(arrays), act=act, scale=scale,
                          chunk_rows=chunk_rows),
        grid=(nt,),
        in_specs=specs + [
            pl.BlockSpec(w_block, lambda i: w_index + (0, 0), pipeline_mode=pl.Buffered(1)),
            pl.BlockSpec((1, nout), lambda i: (0, 0)),
        ],
        out_specs=pl.BlockSpec(out_block, lambda i: (i, 0)),
        out_shape=jax.ShapeDtypeStruct((out_rows, out_cols), out_dtype),
        scratch_shapes=[pltpu.VMEM((k, nout), BF16)],
        compiler_params=_params(("arbitrary",), 48),
        name=name,
    )(*arrays, w, b)


def _route(logits_t, b_router):
    aff = jax.nn.sigmoid(logits_t)
    sel = aff + b_router
    srow = [sel[e:e + 1, :] for e in range(N_EXPERTS)]
    arow = [aff[e:e + 1, :] for e in range(N_EXPERTS)]

    def top2_sum(v):
        pairs = [v[i] + v[j] for i in range(4) for j in range(i + 1, 4)]
        return functools.reduce(jnp.maximum, pairs)

    scores = [top2_sum(srow[4 * g:4 * g + 4]) for g in range(N_GROUPS)]
    best = scores[0]
    gi = jnp.zeros_like(best, dtype=jnp.int32)
    for g in range(1, N_GROUPS):
        upd = scores[g] > best
        best = jnp.where(upd, scores[g], best)
        gi = jnp.where(upd, g, gi)

    def pick_group(rows, j):
        out = rows[j]
        for g in range(1, N_GROUPS):
            out = jnp.where(gi == g, rows[4 * g + j], out)
        return out

    v = [pick_group(srow, j) for j in range(EXPERTS_PER_GROUP)]
    a = [pick_group(arow, j) for j in range(EXPERTS_PER_GROUP)]

    m1, i1 = v[0], jnp.zeros_like(gi)
    for j in range(1, EXPERTS_PER_GROUP):
        upd = v[j] > m1
        m1 = jnp.where(upd, v[j], m1)
        i1 = jnp.where(upd, j, i1)
    m2 = jnp.full_like(m1, NEG_INF)
    i2 = jnp.zeros_like(gi)
    for j in range(EXPERTS_PER_GROUP):
        cand = jnp.where(i1 == j, NEG_INF, v[j])
        upd = cand > m2
        m2 = jnp.where(upd, cand, m2)
        i2 = jnp.where(upd, j, i2)

    def pick_idx(rows, idx):
        out = rows[0]
        for j in range(1, EXPERTS_PER_GROUP):
            out = jnp.where(idx == j, rows[j], out)
        return out

    a1 = pick_idx(a, i1)
    a2 = pick_idx(a, i2)
    tot = a1 + a2
    e_idx = jnp.concatenate([gi * EXPERTS_PER_GROUP + i1, gi * EXPERTS_PER_GROUP + i2], axis=0)
    gates = jnp.concatenate([a1 / tot, a2 / tot], axis=0)
    return e_idx, gates


def _proj_ln_kernel(*refs, n_m, n_res):
    m_refs = refs[:n_m]
    w_ref, b_ref = refs[n_m:n_m + 2]
    res_refs = refs[n_m + 2:n_m + 2 + n_res]
    (g_ref, beta_ref, wr_ref, br_ref, x_ref, xrow_ref, e_ref, gate_ref,
     wbf_ref, ya, yb) = refs[n_m + 2 + n_res:]
    i = pl.program_id(0)
    n_tiles = pl.num_programs(0) - 1

    @pl.when(i == 0)
    def _():
        _cast_rows(w_ref, wbf_ref)
        yb[...] = jnp.zeros_like(yb)

    for parity, (cur, prev) in enumerate(((ya, yb), (yb, ya))):
        @pl.when(i % 2 == parity)
        def _():
            cur[...] = jnp.dot(_tok_load(m_refs, i >= n_tiles - 1), wbf_ref[...],
                               preferred_element_type=F32)
            y = prev[...] + b_ref[...]
            x = _layer_norm(ALPHA * _tok_load(res_refs, i == n_tiles) + y, g_ref[...], beta_ref[...])
            x_ref[...] = x
            x_bf = x.astype(BF16)
            _store_packed_rows(x_bf, xrow_ref)
            logits_t = lax.dot_general(wr_ref[...].astype(BF16), x_bf,
                                       (((1,), (1,)), ((), ())), preferred_element_type=F32)
            e_idx, gates = _route(logits_t, br_ref[...])
            e_ref[...] = e_idx
            gate_ref[...] = gates


def _proj_ln(m, w, w_index, b, res, g, beta, wr_t, br, *, name):
    nt = _tok_operands(m)[2]
    m_arrays, m_specs, _ = _tok_operands(m, lambda i: jnp.minimum(i, nt - 1))
    res_arrays, res_specs, _ = _tok_operands(res, lambda i: jnp.maximum(i - 1, 0))
    k = w.shape[-2]
    tm = TOK_TILE
    n = nt * tm
    row = lambda i: (jnp.maximum(i - 1, 0), 0)
    const = lambda i: (0, 0)
    x, x_rows, e_idx, gates = pl.pallas_call(
        functools.partial(_proj_ln_kernel, n_m=len(m_arrays), n_res=len(res_arrays)),
        grid=(nt + 1,),
        in_specs=m_specs + [
            pl.BlockSpec((None,) * len(w_index) + (k, D_MODEL), lambda i: w_index + (0, 0),
                         pipeline_mode=pl.Buffered(1)),
            pl.BlockSpec((1, D_MODEL), const),
        ] + res_specs + [
            pl.BlockSpec((1, D_MODEL), const),
            pl.BlockSpec((1, D_MODEL), const),
            pl.BlockSpec((N_EXPERTS, D_MODEL), const),
            pl.BlockSpec((N_EXPERTS, 1), const),
        ],
        out_specs=[
            pl.BlockSpec((tm, D_MODEL), row),
            pl.BlockSpec((tm * ROW_SUBLANES, LANES), row),
            pl.BlockSpec((None, 2, tm), lambda i: (jnp.maximum(i - 1, 0), 0, 0)),
            pl.BlockSpec((None, 2, tm), lambda i: (jnp.maximum(i - 1, 0), 0, 0)),
        ],
        out_shape=[
            jax.ShapeDtypeStruct((n, D_MODEL), F32),
            jax.ShapeDtypeStruct((n * ROW_SUBLANES, LANES), jnp.uint32),
            jax.ShapeDtypeStruct((nt, 2, tm), jnp.int32),
            jax.ShapeDtypeStruct((nt, 2, tm), F32),
        ],
        scratch_shapes=[pltpu.VMEM((k, D_MODEL), BF16),
                        pltpu.VMEM((tm, D_MODEL), F32), pltpu.VMEM((tm, D_MODEL), F32)],
        compiler_params=_params(("arbitrary",), 52),
        name=name,
    )(*m_arrays, w, b, *res_arrays, g, beta, wr_t, br)
    e_idx = e_idx.transpose(1, 0, 2).reshape(2, n)
    gates = gates.transpose(1, 0, 2).reshape(2, n)
    return x, x_rows, e_idx, gates


def _plan_kernel(e_ref, pos_ref, meta_ref, rank_ref):
    nrow = e_ref.shape[0]
    ri = lax.broadcasted_iota(jnp.int32, (LANES, LANES), 0)
    ci = lax.broadcasted_iota(jnp.int32, (LANES, LANES), 1)
    tri = jnp.where(ri <= ci, 1.0, 0.0).astype(BF16)
    sub = lax.broadcasted_iota(jnp.int32, (N_EXPERTS, LANES), 0)

    def count_body(b, base):
        rows = [b * PLAN_UNROLL + u for u in range(PLAN_UNROLL)]
        onehots = [sub == e_ref[pl.ds(r, 1), :] for r in rows]
        locs = [jnp.dot(jnp.where(oh, 1.0, 0.0).astype(BF16), tri, preferred_element_type=F32)
                for oh in onehots]
        for r, onehot, loc in zip(rows, onehots, locs):
            rank_ref[pl.ds(r, 1), :] = jnp.sum(jnp.where(onehot, base + loc - 1.0, 0.0),
                                               axis=0, keepdims=True)
            base = base + jnp.broadcast_to(loc[:, LANES - 1:LANES], (N_EXPERTS, LANES))
        return base

    count = lax.fori_loop(0, nrow // PLAN_UNROLL, count_body, jnp.zeros((N_EXPERTS, LANES), F32))
    ntile = jnp.floor((count + (MOE_TILE - 1.0)) * (1.0 / MOE_TILE))
    offs = []
    acc = jnp.zeros((1, LANES), F32)
    for e in range(N_EXPERTS):
        offs.append(acc)
        acc = acc + ntile[e:e + 1, :]
    tile_off = jnp.concatenate(offs, axis=0)
    tile_end = tile_off + ntile
    lane = lax.broadcasted_iota(jnp.int32, (N_EXPERTS, LANES), 1).astype(F32)
    tile_expert = jnp.sum(jnp.where(tile_end <= lane, 1.0, 0.0), axis=0, keepdims=True)
    tile_expert = jnp.minimum(tile_expert, N_EXPERTS - 1.0)
    meta = jnp.concatenate([tile_expert, acc, jnp.zeros((6, LANES), F32)], axis=0)
    meta_ref[...] = meta.astype(jnp.int32)
    row_off = tile_off * float(MOE_TILE)

    def pos_body(r, c):
        onehot = sub == e_ref[pl.ds(r, 1), :]
        p = jnp.sum(jnp.where(onehot, row_off, 0.0), axis=0, keepdims=True) + rank_ref[pl.ds(r, 1), :]
        pos_ref[pl.ds(r, 1), :] = p.astype(jnp.int32)
        return c

    lax.fori_loop(0, nrow, pos_body, 0)


def _plan(e_idx, *, name):
    n2 = e_idx.shape[0] * e_idx.shape[1]
    assert n2 % (LANES * PLAN_UNROLL) == 0
    e2d = e_idx.reshape(n2 // LANES, LANES)
    pos, meta = pl.pallas_call(
        _plan_kernel,
        out_shape=[jax.ShapeDtypeStruct(e2d.shape, jnp.int32),
                   jax.ShapeDtypeStruct((8, LANES), jnp.int32)],
        scratch_shapes=[pltpu.VMEM(e2d.shape, F32)],
        name=name,
    )(e2d)
    return pos.reshape(n2), meta[0], meta[1, :1]


def _invert_kernel(pos_ref, pair_ref):
    n_rows = pair_ref.shape[0]
    n_pairs = pos_ref.shape[0]

    def fill_body(b, c):
        for u in range(SCALAR_UNROLL):
            pair_ref[b * SCALAR_UNROLL + u] = -1
        return c

    def pair_body(b, c):
        rows = [pos_ref[b * SCALAR_UNROLL + u] for u in range(SCALAR_UNROLL)]
        for u in range(SCALAR_UNROLL):
            pair_ref[rows[u]] = b * SCALAR_UNROLL + u
        return c

    lax.fori_loop(0, n_rows // SCALAR_UNROLL, fill_body, 0)
    lax.fori_loop(0, n_pairs // SCALAR_UNROLL, pair_body, 0)


def _invert(pos, n_rows, *, name):
    return pl.pallas_call(
        _invert_kernel,
        grid_spec=pltpu.PrefetchScalarGridSpec(
            num_scalar_prefetch=1,
            grid=(1,),
            in_specs=[],
            out_specs=pl.BlockSpec(memory_space=pltpu.SMEM),
        ),
        out_shape=jax.ShapeDtypeStruct((n_rows,), jnp.int32),
        name=name,
    )(pos)


def _expert_changed(te_ref, i):
    return jnp.logical_or(i == 0, te_ref[i] != te_ref[jnp.maximum(i - 1, 0)])


def _moe_up_kernel(pair_ref, te_ref, nu_ref, x_hbm, wg_ref, wu_ref, h_ref,
                   wg_bf, wu_bf, x_rows, x_bf, sem, *, n_tok, n_tiles):
    i = pl.program_id(0)
    nu = nu_ref[0]

    def row_copy(tile, r):
        p = pair_ref[tile * MOE_TILE + r]
        tok = jnp.where(p >= n_tok, p - n_tok, jnp.maximum(p, 0))
        src = pl.ds(pl.multiple_of(tok * ROW_SUBLANES, ROW_SUBLANES), ROW_SUBLANES)
        dst = pl.ds(r * ROW_SUBLANES, ROW_SUBLANES)
        return pltpu.make_async_copy(x_hbm.at[src, :], x_rows.at[dst, :], sem)

    def wait_tile():
        pltpu.make_async_copy(x_hbm.at[pl.ds(0, MOE_TILE * ROW_SUBLANES), :], x_rows, sem).wait()

    @pl.when(i == 0)
    def _():
        def body(rb, c):
            for u in range(DMA_UNROLL):
                row_copy(0, rb * DMA_UNROLL + u).start()
            return c
        lax.fori_loop(0, MOE_TILE // DMA_UNROLL, body, 0)

    @pl.when(i <= nu)
    def _():
        wait_tile()

    @pl.when(i < nu)
    def _():
        @pl.when(_expert_changed(te_ref, i))
        def _():
            _cast_rows(wg_ref, wg_bf)
            _cast_rows(wu_ref, wu_bf)

        _load_packed_rows(x_rows, x_bf)
        next_tile = jnp.minimum(i + 1, n_tiles - 1)
        for r in range(MOE_TILE):
            row_copy(next_tile, r).start(priority=r % 2)
        x = x_bf[...]
        a = jnp.dot(x, wg_bf[...], preferred_element_type=F32)
        b = jnp.dot(x, wu_bf[...], preferred_element_type=F32)
        h_ref[...] = (jax.nn.silu(a) * b).astype(BF16)

    @pl.when(jnp.logical_and(i == n_tiles - 1, i < nu))
    def _():
        wait_tile()

    @pl.when(i >= nu)
    def _():
        h_ref[...] = jnp.zeros_like(h_ref)


def _moe_down_kernel(pair_ref, te_ref, nu_ref, h_ref, wd_ref, out_hbm,
                     wd_bf, ya, yb, sems, tsem, *, n_tok, n_tiles):
    i = pl.program_id(0)
    nu = nu_ref[0]
    bufs = (ya, yb)
    trash = 2 * n_tok

    def row_copy(tile, r, buf, sem):
        p = pair_ref[tile * MOE_TILE + r]
        dst = jnp.where(p < 0, trash + r, p)
        return pltpu.make_async_copy(buf.at[pl.ds(r, 1), :], out_hbm.at[pl.ds(dst, 1), :], sem)

    def wait_tile(buf, sem):
        pltpu.make_async_copy(buf, out_hbm.at[pl.ds(0, MOE_TILE), :], sem).wait()

    @pl.when(i == 0)
    def _():
        yb[...] = jnp.zeros_like(yb)
        fill = pltpu.make_async_copy(yb, out_hbm.at[pl.ds(trash, MOE_TILE), :], tsem)
        fill.start()
        fill.wait()

    for parity in range(2):
        cur, prev = bufs[parity], bufs[1 - parity]
        cur_sem, prev_sem = sems.at[parity], sems.at[1 - parity]
        mine = i % 2 == parity

        @pl.when(jnp.logical_and(mine, jnp.logical_and(i >= 1, i - 1 <= nu)))
        def _():
            wait_tile(cur, cur_sem)

        @pl.when(jnp.logical_and(mine, i < nu))
        def _():
            @pl.when(_expert_changed(te_ref, i))
            def _():
                _cast_rows(wd_ref, wd_bf)

            prev_tile = jnp.maximum(i - 1, 0)
            for r in range(MOE_TILE):
                row_copy(prev_tile, r, prev, prev_sem).start(priority=r % 2)
            cur[...] = jnp.dot(h_ref[...], wd_bf[...], preferred_element_type=F32)

        @pl.when(jnp.logical_and(mine, i == nu))
        def _():
            def body(rb, c):
                for u in range(DMA_UNROLL):
                    row_copy(i - 1, rb * DMA_UNROLL + u, prev, prev_sem).start()
                return c
            lax.fori_loop(0, MOE_TILE // DMA_UNROLL, body, 0)

            @pl.when(i == n_tiles)
            def _():
                wait_tile(prev, prev_sem)


def _moe_ffn(x, pair, tile_expert, n_used, w_gate, w_up, w_down, layer):
    n_tok = x.shape[0] // ROW_SUBLANES
    n_tiles = pair.shape[0] // MOE_TILE

    def tile(i, pr, te, nu):
        return (jnp.minimum(i, nu[0] - 1), 0)

    def expert(i, pr, te, nu):
        return (layer, te[jnp.minimum(i, nu[0] - 1)], 0, 0)

    def weight_spec(k, n):
        return pl.BlockSpec((None, None, k, n), expert)

    row_bufs = [pltpu.VMEM((MOE_TILE, D_MODEL), F32), pltpu.VMEM((MOE_TILE, D_MODEL), F32),
                pltpu.SemaphoreType.DMA((2,))]
    h = pl.pallas_call(
        functools.partial(_moe_up_kernel, n_tok=n_tok, n_tiles=n_tiles),
        grid_spec=pltpu.PrefetchScalarGridSpec(
            num_scalar_prefetch=3,
            grid=(n_tiles,),
            in_specs=[pl.BlockSpec(memory_space=pl.ANY),
                      weight_spec(D_MODEL, D_EXPERT),
                      weight_spec(D_MODEL, D_EXPERT)],
            out_specs=pl.BlockSpec((MOE_TILE, D_EXPERT), lambda i, pr, te, nu: (i, 0)),
            scratch_shapes=[pltpu.VMEM((D_MODEL, D_EXPERT), BF16),
                            pltpu.VMEM((D_MODEL, D_EXPERT), BF16),
                            pltpu.VMEM((MOE_TILE * ROW_SUBLANES, LANES), jnp.uint32),
                            pltpu.VMEM((MOE_TILE, D_MODEL), BF16),
                            pltpu.SemaphoreType.DMA(())],
        ),
        out_shape=jax.ShapeDtypeStruct((n_tiles * MOE_TILE, D_EXPERT), BF16),
        compiler_params=_params(("arbitrary",), 56),
        name=f"moe_up_{layer}",
    )(pair, tile_expert, n_used, x, w_gate, w_up)
    return pl.pallas_call(
        functools.partial(_moe_down_kernel, n_tok=n_tok, n_tiles=n_tiles),
        grid_spec=pltpu.PrefetchScalarGridSpec(
            num_scalar_prefetch=3,
            grid=(n_tiles + 1,),
            in_specs=[pl.BlockSpec((MOE_TILE, D_EXPERT), tile),
                      weight_spec(D_EXPERT, D_MODEL)],
            out_specs=pl.BlockSpec(memory_space=pl.ANY),
            scratch_shapes=[pltpu.VMEM((D_EXPERT, D_MODEL), BF16)] + row_bufs
            + [pltpu.SemaphoreType.DMA(())],
        ),
        out_shape=jax.ShapeDtypeStruct((2 * n_tok + MOE_TILE, D_MODEL), F32),
        compiler_params=_params(("arbitrary",), 40),
        name=f"moe_down_{layer}",
    )(pair, tile_expert, n_used, h, w_down)


def _combine(y0_ref, y1_ref, res_ref, gate_ref, g_ref, beta_ref):
    gate = gate_ref[...]
    ffn = gate[:, 0:1] * y0_ref[...] + gate[:, 1:2] * y1_ref[...]
    return _layer_norm(ALPHA * res_ref[...] + ffn, g_ref[...], beta_ref[...])


def _combine_specs(n):
    nt = n // TOK_TILE
    row = lambda i: (i, 0)
    const = lambda i: (0, 0)
    return [pl.BlockSpec((TOK_TILE, D_MODEL), row),
            pl.BlockSpec((TOK_TILE, D_MODEL), lambda i: (i + nt, 0)),
            pl.BlockSpec((TOK_TILE, D_MODEL), row),
            pl.BlockSpec((TOK_TILE, 2), row),
            pl.BlockSpec((1, D_MODEL), const),
            pl.BlockSpec((1, D_MODEL), const)]


def _combine_split_kernel(y0_ref, y1_ref, res_ref, gate_ref, g_ref, beta_ref, prompt_ref, sample_ref):
    x = _combine(y0_ref, y1_ref, res_ref, gate_ref, g_ref, beta_ref)
    is_sample = pl.program_id(0) == pl.num_programs(0) - 1

    @pl.when(jnp.logical_not(is_sample))
    def _():
        prompt_ref[...] = x

    @pl.when(is_sample)
    def _():
        sample_ref[...] = x


def _combine_split(ys, res, gates_col, g, beta, *, name):
    n = res.shape[0]
    tm = TOK_TILE
    nt = n // tm
    return pl.pallas_call(
        _combine_split_kernel,
        grid=(nt,),
        in_specs=_combine_specs(n),
        out_specs=[pl.BlockSpec((tm, D_MODEL), lambda i: (jnp.minimum(i, nt - 2), 0)),
                   pl.BlockSpec((tm, D_MODEL), lambda i: (0, 0))],
        out_shape=[jax.ShapeDtypeStruct((n - tm, D_MODEL), F32),
                   jax.ShapeDtypeStruct((tm, D_MODEL), F32)],
        compiler_params=_params(("arbitrary",), 40),
        name=name,
    )(ys, ys, res, gates_col, g, beta)


def _load_weight(w_hbm, wbf_ref, stage_ref, sems):
    rows = stage_ref.shape[1]
    n_chunks = wbf_ref.shape[0] // rows

    def chunk_copy(c):
        return pltpu.make_async_copy(w_hbm.at[pl.ds(c * rows, rows), :], stage_ref.at[c % 2],
                                     sems.at[c % 2])

    chunk_copy(0).start()
    for c in range(n_chunks):
        if c + 1 < n_chunks:
            chunk_copy(c + 1).start()
        chunk_copy(c).wait()
        wbf_ref[c * rows:(c + 1) * rows, :] = stage_ref[c % 2].astype(BF16)


def _combine_qkv_kernel(y0_ref, y1_ref, res_ref, gate_ref, g_ref, beta_ref, wq_hbm, wkv_hbm,
                        x_ref, q_ref, kv_ref, wq_bf, wkv_bf, stage_q, stage_kv, sems, *, wq_index):
    @pl.when(pl.program_id(0) == 0)
    def _():
        wq = wq_hbm
        for k in wq_index:
            wq = wq.at[k]
        _load_weight(wq, wq_bf, stage_q, sems)
        _load_weight(wkv_hbm, wkv_bf, stage_kv, sems)

    x = _combine(y0_ref, y1_ref, res_ref, gate_ref, g_ref, beta_ref)
    x_ref[...] = x
    x_bf = x.astype(BF16)
    q = jnp.dot(x_bf, wq_bf[...], preferred_element_type=F32) * (HEAD_DIM ** -0.5)
    q_ref[...] = q.astype(BF16)
    kv_ref[...] = jnp.dot(x_bf, wkv_bf[...], preferred_element_type=F32)


def _combine_qkv(ys, res, gates_col, g, beta, w_q, wq_index, w_kv, *, name):
    n = res.shape[0]
    tm = TOK_TILE
    row = lambda i: (i, 0)
    hbm = pl.BlockSpec(memory_space=pl.ANY)
    return pl.pallas_call(
        functools.partial(_combine_qkv_kernel, wq_index=wq_index),
        grid=(n // tm,),
        in_specs=_combine_specs(n) + [hbm, hbm],
        out_specs=[pl.BlockSpec((tm, D_MODEL), row),
                   pl.BlockSpec((tm, D_MODEL), row),
                   pl.BlockSpec((tm, 2 * KV_DIM), row)],
        out_shape=[jax.ShapeDtypeStruct((n, D_MODEL), F32),
                   jax.ShapeDtypeStruct((n, D_MODEL), BF16),
                   jax.ShapeDtypeStruct((n, 2 * KV_DIM), F32)],
        scratch_shapes=[pltpu.VMEM((D_MODEL, D_MODEL), BF16),
                        pltpu.VMEM((D_MODEL, 2 * KV_DIM), BF16),
                        pltpu.VMEM((2, CAST_ROWS, D_MODEL), F32),
                        pltpu.VMEM((2, CAST_ROWS, 2 * KV_DIM), F32),
                        pltpu.SemaphoreType.DMA((2,))],
        compiler_params=_params(("arbitrary",), 52),
        name=name,
    )(ys, ys, res, gates_col, g, beta, w_q, w_kv)


def _moe_block(x_rows, e_idx, w_gate, w_up, w_down, layer):
    n = x_rows.shape[0] // ROW_SUBLANES
    n_tiles = -(-(2 * n + N_EXPERTS * (MOE_TILE - 1)) // MOE_TILE)
    pos, tile_expert, n_used = _plan(e_idx, name=f"moe_plan_{layer}")
    pair = _invert(pos, n_tiles * MOE_TILE, name=f"moe_invert_{layer}")
    return _moe_ffn(x_rows, pair, tile_expert, n_used, w_gate, w_up, w_down, layer)


def _sigmoid(x):
    return 0.5 * jnp.tanh(0.5 * x) + 0.5


def _log_sigmoid(x):
    return -(jnp.maximum(-x, 0.0) + jnp.log1p(jnp.exp(-jnp.abs(x))))


def _lru_gate_block(xc, n, wrg_bf, wig_bf, brg_ref, big_ref, lam_ref):
    cols = slice(n * LRU_BLOCK, (n + 1) * LRU_BLOCK)
    xb = xc.astype(BF16)
    r = _sigmoid(jnp.dot(xb, wrg_bf[n], preferred_element_type=F32) + brg_ref[:, cols])
    i = _sigmoid(jnp.dot(xb, wig_bf[n], preferred_element_type=F32) + big_ref[:, cols])
    log_a = LRU_C * r * _log_sigmoid(lam_ref[:, cols])
    a = jnp.exp(log_a)
    u = xc * i * jnp.sqrt(-jnp.tanh(log_a) * (a * a + 1.0))
    return a, u


def _cast_gate_weights(wrg_ref, wig_ref, wrg_bf, wig_bf):
    for n in range(LRU_BLOCKS):
        wrg_bf[n] = wrg_ref[n].astype(BF16)
        wig_bf[n] = wig_ref[n].astype(BF16)


def _lru_prompt_kernel(xb_ref, yb_ref, cw_ref, cb_ref, wrg_ref, wig_ref, brg_ref, big_ref, lam_ref,
                       m_ref, conv_ref, hlast_ref, xs, tail, a_s, u_s, hs_t, h_s, wrg_bf, wig_bf):
    b = pl.program_id(0)
    j = pl.program_id(1)
    tt = m_ref.shape[0]
    seg_len = tt // SEGS
    taps = CONV_WIDTH - 1
    head = SEGS * taps

    @pl.when(jnp.logical_and(b == 0, j == 0))
    def _():
        _cast_gate_weights(wrg_ref, wig_ref, wrg_bf, wig_bf)

    @pl.when(j == 0)
    def _():
        tail[...] = jnp.zeros_like(tail)
        h_s[...] = jnp.zeros_like(h_s)

    for q in range(seg_len):
        xs[head + SEGS * q:head + SEGS * (q + 1), :] = jnp.concatenate(
            [xb_ref[pl.ds(CHUNKS * q + c, SEGS, stride=CHUNKS * seg_len), :] for c in range(CHUNKS)],
            axis=1)
    sub = lax.broadcasted_iota(jnp.int32, (SEGS, D_MODEL), 0)
    for k in range(taps):
        last = head + SEGS * (seg_len - taps + k)
        joined = jnp.where(sub == SEGS - 1, tail[SEGS * k:SEGS * (k + 1), :], xs[last:last + SEGS, :])
        xs[SEGS * k:SEGS * (k + 1), :] = pltpu.roll(joined, 1, axis=0)
    tail[...] = xs[head + SEGS * (seg_len - taps):head + SEGS * seg_len, :]

    for n in range(LRU_BLOCKS):
        cols = slice(n * LRU_BLOCK, (n + 1) * LRU_BLOCK)
        xc = cb_ref[:, cols] + cw_ref[0:1, cols] * xs[0:tt, cols]
        for k in range(1, CONV_WIDTH):
            xc = xc + cw_ref[k:k + 1, cols] * xs[SEGS * k:SEGS * k + tt, cols]
        a, u = _lru_gate_block(xc, n, wrg_bf, wig_bf, brg_ref, big_ref, lam_ref)
        a_s[:, cols] = a
        u_s[:, cols] = u

    def scan_body(q, carry):
        h, prod = carry
        rows = pl.ds(pl.multiple_of(q * SEGS, SEGS), SEGS)
        a = a_s[rows, :]
        h = a * h + u_s[rows, :]
        prod = a * prod
        u_s[rows, :] = h
        a_s[rows, :] = prod
        return h, prod

    h_end, prod_end = lax.fori_loop(
        0, seg_len, scan_body,
        (jnp.zeros((SEGS, D_MODEL), F32), jnp.ones((SEGS, D_MODEL), F32)))
    state = h_s[...]
    entering = []
    for s in range(SEGS):
        entering.append(state)
        state = h_end[s:s + 1, :] + prod_end[s:s + 1, :] * state
    h_s[...] = state
    enter = jnp.concatenate(entering, axis=0)

    def fix_body(q, carry):
        rows = pl.ds(pl.multiple_of(q * SEGS, SEGS), SEGS)
        h = u_s[rows, :] + a_s[rows, :] * enter
        for c in range(CHUNKS):
            hs_t[pl.ds(CHUNKS * q + c, SEGS, stride=CHUNKS * seg_len), :] = h[:, c * LANES:(c + 1) * LANES]
        return carry

    lax.fori_loop(0, seg_len, fix_body, 0)
    hs = jnp.concatenate([hs_t[pl.ds(c, tt, stride=CHUNKS), :] for c in range(CHUNKS)], axis=1)
    m_ref[...] = (hs * yb_ref[...].astype(F32)).astype(BF16)

    @pl.when(j == pl.num_programs(1) - 1)
    def _():
        for k in range(taps):
            conv_ref[k:k + 1, :] = tail[SEGS * k + SEGS - 1:SEGS * (k + 1), :]
        hlast_ref[...] = state


def _lru_prompt(xb, yb, batch, seq, cw, cb, wrg, wig, brg, big, lam, *, tt=256):
    nj = seq // tt
    row = lambda b, j: (b * nj + j, 0)
    const2 = lambda b, j: (0, 0)
    const3 = lambda b, j: (0, 0, 0)
    return pl.pallas_call(
        _lru_prompt_kernel,
        grid=(batch, nj),
        in_specs=[
            pl.BlockSpec((tt * CHUNKS, LANES), row),
            pl.BlockSpec((tt, D_MODEL), row),
            pl.BlockSpec((CONV_WIDTH, D_MODEL), const2),
            pl.BlockSpec((1, D_MODEL), const2),
            pl.BlockSpec((LRU_BLOCKS, LRU_BLOCK, LRU_BLOCK), const3),
            pl.BlockSpec((LRU_BLOCKS, LRU_BLOCK, LRU_BLOCK), const3),
            pl.BlockSpec((1, D_MODEL), const2),
            pl.BlockSpec((1, D_MODEL), const2),
            pl.BlockSpec((1, D_MODEL), const2),
        ],
        out_specs=[
            pl.BlockSpec((tt, D_MODEL), row),
            pl.BlockSpec((None, CONV_WIDTH - 1, D_MODEL), lambda b, j: (b, 0, 0)),
            pl.BlockSpec((None, 1, D_MODEL), lambda b, j: (b, 0, 0)),
        ],
        out_shape=[
            jax.ShapeDtypeStruct((batch * seq, D_MODEL), BF16),
            jax.ShapeDtypeStruct((batch, CONV_WIDTH - 1, D_MODEL), F32),
            jax.ShapeDtypeStruct((batch, 1, D_MODEL), F32),
        ],
        scratch_shapes=[
            pltpu.VMEM((tt + SEGS * (CONV_WIDTH - 1), D_MODEL), F32),
            pltpu.VMEM((SEGS * (CONV_WIDTH - 1), D_MODEL), F32),
            pltpu.VMEM((tt, D_MODEL), F32),
            pltpu.VMEM((tt, D_MODEL), F32),
            pltpu.VMEM((tt * CHUNKS, LANES), F32),
            pltpu.VMEM((1, D_MODEL), F32),
            pltpu.VMEM((LRU_BLOCKS, LRU_BLOCK, LRU_BLOCK), BF16),
            pltpu.VMEM((LRU_BLOCKS, LRU_BLOCK, LRU_BLOCK), BF16),
        ],
        compiler_params=_params(("arbitrary", "arbitrary"), 40),
        name="lru_prompt",
    )(xb, yb, cw, cb, wrg, wig, brg, big, lam)


def _lru_sample_kernel(xb_ref, yb_ref, cs_ref, h0_ref, cw_ref, cb_ref, wrg_ref, wig_ref,
                       brg_ref, big_ref, lam_ref, m_ref, conv_ref, hlast_ref, wrg_bf, wig_bf, *, steps):
    batch = h0_ref.shape[0]
    _cast_gate_weights(wrg_ref, wig_ref, wrg_bf, wig_bf)
    m_ref[steps * batch:, :] = jnp.zeros((m_ref.shape[0] - steps * batch, D_MODEL), BF16)

    def slab(t, cols):
        if t < CONV_WIDTH - 1:
            return cs_ref[t, :, cols]
        t -= CONV_WIDTH - 1
        first, stop, _ = cols.indices(D_MODEL)
        chunks = range(first // LANES, stop // LANES)
        return jnp.concatenate(
            [xb_ref[pl.ds(t * batch * CHUNKS + c, batch, stride=CHUNKS), :] for c in chunks], axis=1)

    for n in range(LRU_BLOCKS):
        cols = slice(n * LRU_BLOCK, (n + 1) * LRU_BLOCK)
        h = h0_ref[:, cols]
        for t in range(steps):
            xc = cb_ref[:, cols] + cw_ref[0:1, cols] * slab(t, cols)
            for k in range(1, CONV_WIDTH):
                xc = xc + cw_ref[k:k + 1, cols] * slab(t + k, cols)
            a, u = _lru_gate_block(xc, n, wrg_bf, wig_bf, brg_ref, big_ref, lam_ref)
            h = a * h + u
            rows = slice(t * batch, (t + 1) * batch)
            m_ref[rows, cols] = (h * yb_ref[rows, cols].astype(F32)).astype(BF16)
        hlast_ref[:, cols] = h
    for k in range(CONV_WIDTH - 1):
        conv_ref[k] = slab(steps + k, slice(None))


def _lru_sample(xb, yb, tile, steps, conv_state, h0, cw, cb, wrg, wig, brg, big, lam):
    batch = h0.shape[0]
    tok = pl.BlockSpec((TOK_TILE, D_MODEL), lambda i: (tile, 0))
    tok_chunks = pl.BlockSpec((TOK_TILE * CHUNKS, LANES), lambda i: (tile, 0))
    full = lambda a: pl.BlockSpec(a.shape, lambda i: (0,) * a.ndim)
    small = (conv_state, h0, cw, cb, wrg, wig, brg, big, lam)
    return pl.pallas_call(
        functools.partial(_lru_sample_kernel, steps=steps),
        grid=(1,),
        in_specs=[tok_chunks, tok] + [full(a) for a in small],
        out_specs=[
            pl.BlockSpec((TOK_TILE, D_MODEL), lambda i: (0, 0)),
            pl.BlockSpec((CONV_WIDTH - 1, batch, D_MODEL), lambda i: (0, 0, 0)),
            pl.BlockSpec((batch, D_MODEL), lambda i: (0, 0)),
        ],
        out_shape=[
            jax.ShapeDtypeStruct((TOK_TILE, D_MODEL), BF16),
            jax.ShapeDtypeStruct((CONV_WIDTH - 1, batch, D_MODEL), F32),
            jax.ShapeDtypeStruct((batch, D_MODEL), F32),
        ],
        scratch_shapes=[
            pltpu.VMEM((LRU_BLOCKS, LRU_BLOCK, LRU_BLOCK), BF16),
            pltpu.VMEM((LRU_BLOCKS, LRU_BLOCK, LRU_BLOCK), BF16),
        ],
        compiler_params=_params(("arbitrary",), 32),
        name="lru_sample",
    )(xb, yb, *small)


def _rel_bucket(dist):
    n = jnp.maximum(dist, 0)
    max_exact = N_BUCKETS // 2
    nf = jnp.maximum(n, 1).astype(F32)
    large = max_exact + (jnp.log(nf / max_exact) / math.log(MAX_DISTANCE / max_exact)
                         * (N_BUCKETS - max_exact)).astype(jnp.int32)
    large = jnp.minimum(large, N_BUCKETS - 1)
    return jnp.where(n < max_exact, n, large)


def _masked_buckets(dist):
    valid = (dist >= 0) & (dist < WINDOW)
    return jnp.where(valid, _rel_bucket(dist), -1).astype(jnp.int32)


def _build_bias(bucket, tab_ref, head):
    def body(bi, acc):
        return jnp.where(bucket == bi, tab_ref[bi * N_HEADS + head], acc)
    return lax.fori_loop(0, N_BUCKETS, body, jnp.full(bucket.shape, NEG_INF, F32))


def _softmax_pv(s, sink, v):
    m = jnp.maximum(jnp.max(s, axis=-1, keepdims=True), sink)
    p = jnp.exp(s - m)
    den = jnp.sum(p, axis=-1, keepdims=True) + jnp.exp(sink - m)
    return jnp.dot(p.astype(BF16), v, preferred_element_type=F32) / den


def _attn_prompt_kernel(q_ref, kvp_ref, kvc_ref, bucket_ref, tab_ref, sink_ref, o_ref, bias_s):
    b = pl.program_id(0)
    n = pl.program_id(1)

    @pl.when(jnp.logical_and(b == 0, n == 0))
    def _():
        bucket = bucket_ref[...]

        col = lax.broadcasted_iota(jnp.int32, (WINDOW, 2 * WINDOW), 1)

        def head_body(h, c):
            bias = _build_bias(bucket, tab_ref, h)
            sink = sink_ref[h]
            g = h // GROUP
            r0 = pl.multiple_of((h % GROUP) * WINDOW, WINDOW)
            bias_s[0, g, pl.ds(r0, WINDOW), :] = jnp.where(col == 0, sink, bias)
            bias_s[1, g, pl.ds(r0, WINDOW), :] = jnp.where(
                col == 0, sink, jnp.where(col < WINDOW, NEG_INF, bias))
            return c

        lax.fori_loop(0, N_HEADS, head_body, 0)

    first = (n == 0).astype(jnp.int32)
    row = lax.broadcasted_iota(jnp.int32, kvp_ref.shape, 0)
    kv_prev = jnp.where(row == 0, 0.0, kvp_ref[...])
    kv = jnp.concatenate([kv_prev, kvc_ref[...]], axis=0).astype(BF16)
    ones = jnp.ones((2 * WINDOW, HEAD_DIM), BF16)
    lane = lax.broadcasted_iota(jnp.int32, (WINDOW, 2 * HEAD_DIM), 1)
    for g in range(N_KV_HEADS):
        heads = range(g * GROUP, (g + 1) * GROUP)
        kg = kv[:, g * HEAD_DIM:(g + 1) * HEAD_DIM]
        vg = kv[:, KV_DIM + g * HEAD_DIM:KV_DIM + (g + 1) * HEAD_DIM]
        v_ext = jnp.concatenate([vg, ones], axis=1)
        qg = jnp.concatenate([q_ref[:, h * HEAD_DIM:(h + 1) * HEAD_DIM] for h in heads], axis=0)
        s = lax.dot_general(qg, kg, (((1,), (1,)), ((), ())), preferred_element_type=F32)
        s = s + bias_s[first, g]
        p = jnp.exp(s - jnp.max(s, axis=-1, keepdims=True)).astype(BF16)
        o_ext = jnp.dot(p, v_ext, preferred_element_type=F32)
        o_rot = pltpu.roll(o_ext, HEAD_DIM, axis=1)
        for pair in range(GROUP // 2):
            r0 = slice(2 * pair * WINDOW, (2 * pair + 1) * WINDOW)
            r1 = slice((2 * pair + 1) * WINDOW, (2 * pair + 2) * WINDOW)
            even = o_ext[r0] * (1.0 / o_rot[r0])
            odd = o_rot[r1] * (1.0 / o_ext[r1])
            c0 = (g * GROUP + 2 * pair) * HEAD_DIM
            o_ref[:, c0:c0 + 2 * HEAD_DIM] = jnp.where(lane < HEAD_DIM, even, odd).astype(BF16)


def _attn_prompt(q, kv, batch, seq, bucket, tab, sinks):
    nb = seq // WINDOW
    smem = pl.BlockSpec(memory_space=pltpu.SMEM)
    return pl.pallas_call(
        _attn_prompt_kernel,
        grid=(batch, nb),
        in_specs=[
            pl.BlockSpec((WINDOW, D_MODEL), lambda b, n: (b * nb + n, 0)),
            pl.BlockSpec((WINDOW, 2 * KV_DIM), lambda b, n: (jnp.maximum(b * nb + n - 1, 0), 0)),
            pl.BlockSpec((WINDOW, 2 * KV_DIM), lambda b, n: (b * nb + n, 0)),
            pl.BlockSpec((WINDOW, 2 * WINDOW), lambda b, n: (0, 0)),
            smem, smem,
        ],
        out_specs=pl.BlockSpec((WINDOW, D_MODEL), lambda b, n: (b * nb + n, 0)),
        out_shape=jax.ShapeDtypeStruct((batch * seq, D_MODEL), BF16),
        scratch_shapes=[pltpu.VMEM((2, N_KV_HEADS, GROUP * WINDOW, 2 * WINDOW), F32)],
        compiler_params=_params(("arbitrary", "arbitrary"), 32),
        name="attn_prompt",
    )(q, kv, kv, bucket, tab, sinks)


def _attn_sample_kernel(q_ref, k_ref, v_ref, bucket_ref, tab_ref, sink_ref, o_ref, bias_s):
    steps = q_ref.shape[0]

    @pl.when(pl.program_id(0) == 0)
    def _():
        bucket = bucket_ref[...]

        def head_body(h, c):
            bias_s[h] = _build_bias(bucket, tab_ref, h)
            return c

        lax.fori_loop(0, N_HEADS, head_body, 0)

    rows = lax.broadcasted_iota(jnp.int32, (GROUP * steps, 1), 0)
    for g in range(N_KV_HEADS):
        kg = k_ref[:, g * HEAD_DIM:(g + 1) * HEAD_DIM].astype(BF16)
        vg = v_ref[:, g * HEAD_DIM:(g + 1) * HEAD_DIM].astype(BF16)
        heads = range(g * GROUP, (g + 1) * GROUP)
        qg = jnp.concatenate([q_ref[:, h * HEAD_DIM:(h + 1) * HEAD_DIM] for h in heads], axis=0)
        bias = jnp.concatenate([bias_s[h] for h in heads], axis=0)
        sink = jnp.full((GROUP * steps, 1), sink_ref[g * GROUP], F32)
        for hh in range(1, GROUP):
            sink = jnp.where(rows >= hh * steps, sink_ref[g * GROUP + hh], sink)
        s = lax.dot_general(qg, kg, (((1,), (1,)), ((), ())), preferred_element_type=F32) + bias
        o = _softmax_pv(s, sink, vg)
        for hh, h in enumerate(heads):
            o_ref[:, h * HEAD_DIM:(h + 1) * HEAD_DIM] = o[hh * steps:(hh + 1) * steps].astype(BF16)


def _attn_sample(q, k_all, v_all, bucket, tab, sinks):
    batch, steps, _ = q.shape
    lk = k_all.shape[1]
    smem = pl.BlockSpec(memory_space=pltpu.SMEM)
    return pl.pallas_call(
        _attn_sample_kernel,
        grid=(batch,),
        in_specs=[
            pl.BlockSpec((None, steps, D_MODEL), lambda b: (b, 0, 0)),
            pl.BlockSpec((None, lk, KV_DIM), lambda b: (b, 0, 0)),
            pl.BlockSpec((None, lk, KV_DIM), lambda b: (b, 0, 0)),
            pl.BlockSpec((steps, lk), lambda b: (0, 0)),
            smem, smem,
        ],
        out_specs=pl.BlockSpec((None, steps, D_MODEL), lambda b: (b, 0, 0)),
        out_shape=jax.ShapeDtypeStruct((batch, steps, D_MODEL), BF16),
        scratch_shapes=[pltpu.VMEM((N_HEADS, steps, lk), F32)],
        compiler_params=_params(("arbitrary",), 32),
        name="attn_sample",
    )(q, k_all, v_all, bucket, tab, sinks)


def kernel(x_prompt, x_sample, state_conv, state_rnn, cache_k_win, cache_v_win, ln_g, ln_b, lru_w_x, lru_b_x, lru_w_y, lru_b_y, lru_conv_w, lru_conv_b, lru_w_rg, lru_b_rg, lru_w_ig, lru_b_ig, lru_lam, lru_w_out, lru_b_out, attn_w_kv, attn_w_q, attn_w_o, attn_sinks, rel_bias, moe_w_router, moe_b_router, moe_w_gate, moe_w_up, moe_w_down):
    bp, seq, _ = x_prompt.shape
    bs, steps, _ = x_sample.shape
    n_p = bp * seq
    n_s = bs * steps

    assert n_p % TOK_TILE == 0 and n_s <= TOK_TILE
    sample_tile = n_p // TOK_TILE

    def pad_tile(rows):
        return jnp.pad(rows, ((0, TOK_TILE - n_s), (0, 0)))

    x0 = (x_prompt.reshape(n_p, D_MODEL),
          pad_tile(x_sample.transpose(1, 0, 2).reshape(n_s, D_MODEL)))
    wr_t = moe_w_router.T
    br = moe_b_router.reshape(N_EXPERTS, 1)
    vec = lambda a: a.reshape(1, -1)

    xb = _linear(x0, lru_w_x, (0,), vec(lru_b_x[0]), F32, name="lru_in_x", chunk_rows=True)
    yb = _linear(x0, lru_w_y, (0,), vec(lru_b_y[0]), BF16, act="gelu", name="lru_in_y")
    lru_args = (lru_conv_w[0], vec(lru_conv_b[0]), lru_w_rg[0], lru_w_ig[0],
                vec(lru_b_rg[0]), vec(lru_b_ig[0]), vec(lru_lam[0]))
    m_p, conv_p, rnn_p = _lru_prompt(xb, yb, bp, seq, *lru_args)
    m_s, conv_s, rnn_s = _lru_sample(xb, yb, sample_tile, steps,
                                     state_conv[0].transpose(1, 0, 2), state_rnn[0], *lru_args)
    x1, x1_rows, e_idx, gates = _proj_ln((m_p, m_s), lru_w_out, (0,), vec(lru_b_out[0]), x0,
                                vec(ln_g[0, 0]), vec(ln_b[0, 0]), wr_t, br, name="lru_out_ln")
    ys = _moe_block(x1_rows, e_idx, moe_w_gate, moe_w_up, moe_w_down, 0)

    x2, q, kv = _combine_qkv(ys, x1, gates.T, vec(ln_g[0, 1]), vec(ln_b[0, 1]),
                             attn_w_q, (0,), attn_w_kv, name="moe_combine_qkv")
    tab = rel_bias.reshape(-1)
    sinks = attn_sinks[0]
    qi = jnp.arange(WINDOW)[:, None]
    kj = jnp.arange(2 * WINDOW)[None, :]
    o_p = _attn_prompt(q, kv, bp, seq, _masked_buckets(qi + WINDOW - kj), tab, sinks)
    kv_s = kv[n_p:n_p + n_s].reshape(steps, bs, 2, KV_DIM).transpose(2, 1, 0, 3)
    k_all = jnp.concatenate([cache_k_win.reshape(bs, WINDOW, KV_DIM), kv_s[0]], axis=1)
    v_all = jnp.concatenate([cache_v_win.reshape(bs, WINDOW, KV_DIM), kv_s[1]], axis=1)
    dist_s = jnp.arange(steps)[:, None] + WINDOW - jnp.arange(WINDOW + steps)[None, :]
    q_s = q[n_p:n_p + n_s].reshape(steps, bs, D_MODEL).transpose(1, 0, 2)
    o_s = _attn_sample(q_s, k_all, v_all, _masked_buckets(dist_s), tab, sinks)
    o_s = pad_tile(o_s.transpose(1, 0, 2).reshape(n_s, D_MODEL))
    x3, x3_rows, e_idx, gates = _proj_ln((o_p, o_s), attn_w_o, (0,), jnp.zeros((1, D_MODEL), F32), x2,
                                vec(ln_g[1, 0]), vec(ln_b[1, 0]), wr_t, br, name="attn_out_ln")
    ys = _moe_block(x3_rows, e_idx, moe_w_gate, moe_w_up, moe_w_down, 1)
    y_p, y_s = _combine_split(ys, x3, gates.T, vec(ln_g[1, 1]), vec(ln_b[1, 1]), name="moe_combine_1")

    y_prompt = y_p.reshape(bp, seq, D_MODEL)
    y_sample = y_s[:n_s].reshape(steps, bs, D_MODEL).transpose(1, 0, 2)
    kv_p = jnp.stack([kv[(b + 1) * seq - WINDOW:(b + 1) * seq] for b in range(bp)])
    kv_p = kv_p.reshape(bp, WINDOW, 2, N_KV_HEADS, HEAD_DIM)
    k_win_s = k_all[:, steps:].reshape(bs, WINDOW, N_KV_HEADS, HEAD_DIM)
    v_win_s = v_all[:, steps:].reshape(bs, WINDOW, N_KV_HEADS, HEAD_DIM)
    return (y_prompt, y_sample,
            conv_p[None], rnn_p.reshape(1, bp, D_MODEL),
            kv_p[:, :, 0], kv_p[:, :, 1],
            conv_s.transpose(1, 0, 2)[None], rnn_s[None],
            k_win_s, v_win_s)
```

```python
import functools
import math

import jax
import jax.numpy as jnp
from jax import lax
from jax.experimental import pallas as pl
from jax.experimental.pallas import tpu as pltpu

D_MODEL = 2048
DEPTH = 2
LRU_BLOCKS = 8
LRU_BLOCK = D_MODEL // LRU_BLOCKS
CONV_WIDTH = 4
LRU_C = 8.0
N_HEADS = 32
HEAD_DIM = 64
N_KV_HEADS = 8
GROUP = N_HEADS // N_KV_HEADS
KV_DIM = N_KV_HEADS * HEAD_DIM
WINDOW = 128
N_BUCKETS = 32
MAX_DISTANCE = 128
N_EXPERTS = 16
N_GROUPS = 4
EXPERTS_PER_GROUP = N_EXPERTS // N_GROUPS
D_EXPERT = 1024
ALPHA = (2 * DEPTH) ** 0.25
LN_EPS = 1e-5

LANES = 128
SEGS = 8
CHUNKS = D_MODEL // LANES
ROW_SUBLANES = D_MODEL // (2 * LANES)
MOE_TILE = 256
TOK_TILE = 256
DMA_UNROLL = 8
SCALAR_UNROLL = 32
PLAN_UNROLL = 4
CAST_ROWS = 256
BF16 = jnp.bfloat16
F32 = jnp.float32
NEG_INF = float("-inf")


def _params(sem, vmem_mb):
    return pltpu.CompilerParams(dimension_semantics=sem, vmem_limit_bytes=vmem_mb * 1024 * 1024)


def _cast_rows(src_ref, dst_ref):
    n = src_ref.shape[0] // CAST_ROWS

    def body(i, c):
        r = pl.multiple_of(i * CAST_ROWS, CAST_ROWS)
        dst_ref[pl.ds(r, CAST_ROWS), :] = src_ref[pl.ds(r, CAST_ROWS), :].astype(BF16)
        return c

    lax.fori_loop(0, n, body, 0)


def _layer_norm(z, g, b):
    mu = jnp.mean(z, axis=-1, keepdims=True)
    zc = z - mu
    var = jnp.mean(zc * zc, axis=-1, keepdims=True)
    return zc * lax.rsqrt(var + LN_EPS) * g + b


def _store_packed_rows(x_bf, rows_ref):
    n = x_bf.shape[0]
    bits = pltpu.bitcast(x_bf.astype(F32), jnp.uint32)
    packed = bits[:, D_MODEL // 2:] | (bits[:, :D_MODEL // 2] >> 16)
    for c in range(ROW_SUBLANES):
        rows_ref[pl.ds(c, n, stride=ROW_SUBLANES), :] = packed[:, c * LANES:(c + 1) * LANES]


def _load_packed_rows(rows_ref, x_bf_ref):
    n = x_bf_ref.shape[0]
    for c in range(ROW_SUBLANES):
        words = rows_ref[pl.ds(c, n, stride=ROW_SUBLANES), :]
        low = pltpu.bitcast(words << 16, F32).astype(BF16)
        high = pltpu.bitcast(words & jnp.uint32(0xFFFF0000), F32).astype(BF16)
        x_bf_ref[:, c * LANES:(c + 1) * LANES] = low
        x_bf_ref[:, D_MODEL // 2 + c * LANES:D_MODEL // 2 + (c + 1) * LANES] = high


def _tok_operands(x, tile_of=lambda i: i):
    if isinstance(x, tuple):
        xp, xs = x
        d = xp.shape[1]
        last_p = xp.shape[0] // TOK_TILE - 1
        specs = [pl.BlockSpec((TOK_TILE, d), lambda i, *_: (jnp.minimum(tile_of(i), last_p), 0)),
                 pl.BlockSpec((TOK_TILE, d), lambda i, *_: (0, 0))]
        return [xp, xs], specs, last_p + 2
    return ([x], [pl.BlockSpec((TOK_TILE, x.shape[1]), lambda i, *_: (tile_of(i), 0))],
            x.shape[0] // TOK_TILE)


def _tok_load(refs, is_sample=None):
    if len(refs) == 1:
        return refs[0][...]
    if is_sample is None:
        is_sample = pl.program_id(0) == pl.num_programs(0) - 1
    return jnp.where(is_sample, refs[1][...], refs[0][...])


def _linear_kernel(*refs, n_x, act, scale, chunk_rows):
    x_refs, (w_ref, b_ref, o_ref, wbf_ref) = refs[:n_x], refs[n_x:]

    @pl.when(pl.program_id(0) == 0)
    def _():
        _cast_rows(w_ref, wbf_ref)

    y = jnp.dot(_tok_load(x_refs).astype(BF16), wbf_ref[...], preferred_element_type=F32)
    y = y + b_ref[...]
    if act == "gelu":
        y = jax.nn.gelu(y)
    if scale != 1.0:
        y = y * scale
    if chunk_rows:
        n_chunks = y.shape[1] // LANES
        for c in range(n_chunks):
            o_ref[pl.ds(c, y.shape[0], stride=n_chunks), :] = y[:, c * LANES:(c + 1) * LANES]
    else:
        o_ref[...] = y.astype(o_ref.dtype)


def _linear(x, w, w_index, b, out_dtype, *, name, act=None, scale=1.0, chunk_rows=False):
    arrays, specs, nt = _tok_operands(x)
    k, nout = w.shape[-2:]
    w_block = (None,) * len(w_index) + (k, nout)
    if chunk_rows:
        out_block, out_rows, out_cols = (TOK_TILE * (nout // LANES), LANES), nt * TOK_TILE * (nout // LANES), LANES
    else:
        out_block, out_rows, out_cols = (TOK_TILE, nout), nt * TOK_TILE, nout
    return pl.pallas_call(
        functools.partial(_linear_kernel, n_x=len(arrays), act=act, scale=scale,
                          chunk_rows=chunk_rows),
        grid=(nt,),
        in_specs=specs + [
            pl.BlockSpec(w_block, lambda i: w_index + (0, 0), pipeline_mode=pl.Buffered(1)),
            pl.BlockSpec((1, nout), lambda i: (0, 0)),
        ],
        out_specs=pl.BlockSpec(out_block, lambda i: (i, 0)),
        out_shape=jax.ShapeDtypeStruct((out_rows, out_cols), out_dtype),
        scratch_shapes=[pltpu.VMEM((k, nout), BF16)],
        compiler_params=_params(("arbitrary",), 48),
        name=name,
    )(*arrays, w, b)


def _route(logits_t, b_router):
    aff = jax.nn.sigmoid(logits_t)
    sel = aff + b_router
    srow = [sel[e:e + 1, :] for e in range(N_EXPERTS)]
    arow = [aff[e:e + 1, :] for e in range(N_EXPERTS)]

    def top2_sum(v):
        pairs = [v[i] + v[j] for i in range(4) for j in range(i + 1, 4)]
        return functools.reduce(jnp.maximum, pairs)

    scores = [top2_sum(srow[4 * g:4 * g + 4]) for g in range(N_GROUPS)]
    best = scores[0]
    gi = jnp.zeros_like(best, dtype=jnp.int32)
    for g in range(1, N_GROUPS):
        upd = scores[g] > best
        best = jnp.where(upd, scores[g], best)
        gi = jnp.where(upd, g, gi)

    def pick_group(rows, j):
        out = rows[j]
        for g in range(1, N_GROUPS):
            out = jnp.where(gi == g, rows[4 * g + j], out)
        return out

    v = [pick_group(srow, j) for j in range(EXPERTS_PER_GROUP)]
    a = [pick_group(arow, j) for j in range(EXPERTS_PER_GROUP)]

    m1, i1 = v[0], jnp.zeros_like(gi)
    for j in range(1, EXPERTS_PER_GROUP):
        upd = v[j] > m1
        m1 = jnp.where(upd, v[j], m1)
        i1 = jnp.where(upd, j, i1)
    m2 = jnp.full_like(m1, NEG_INF)
    i2 = jnp.zeros_like(gi)
    for j in range(EXPERTS_PER_GROUP):
        cand = jnp.where(i1 == j, NEG_INF, v[j])
        upd = cand > m2
        m2 = jnp.where(upd, cand, m2)
        i2 = jnp.where(upd, j, i2)

    def pick_idx(rows, idx):
        out = rows[0]
        for j in range(1, EXPERTS_PER_GROUP):
            out = jnp.where(idx == j, rows[j], out)
        return out

    a1 = pick_idx(a, i1)
    a2 = pick_idx(a, i2)
    tot = a1 + a2
    e_idx = jnp.concatenate([gi * EXPERTS_PER_GROUP + i1, gi * EXPERTS_PER_GROUP + i2], axis=0)
    gates = jnp.concatenate([a1 / tot, a2 / tot], axis=0)
    return e_idx, gates


def _proj_ln_kernel(*refs, n_m, n_res):
    m_refs = refs[:n_m]
    w_ref, b_ref = refs[n_m:n_m + 2]
    res_refs = refs[n_m + 2:n_m + 2 + n_res]
    (g_ref, beta_ref, wr_ref, br_ref, x_ref, xrow_ref, e_ref, gate_ref,
     wbf_ref, ya, yb) = refs[n_m + 2 + n_res:]
    i = pl.program_id(0)
    n_tiles = pl.num_programs(0) - 1

    @pl.when(i == 0)
    def _():
        _cast_rows(w_ref, wbf_ref)
        yb[...] = jnp.zeros_like(yb)

    for parity, (cur, prev) in enumerate(((ya, yb), (yb, ya))):
        @pl.when(i % 2 == parity)
        def _():
            cur[...] = jnp.dot(_tok_load(m_refs, i >= n_tiles - 1), wbf_ref[...],
                               preferred_element_type=F32)
            y = prev[...] + b_ref[...]
            x = _layer_norm(ALPHA * _tok_load(res_refs, i == n_tiles) + y, g_ref[...], beta_ref[...])
            x_ref[...] = x
            x_bf = x.astype(BF16)
            _store_packed_rows(x_bf, xrow_ref)
            logits_t = lax.dot_general(wr_ref[...].astype(BF16), x_bf,
                                       (((1,), (1,)), ((), ())), preferred_element_type=F32)
            e_idx, gates = _route(logits_t, br_ref[...])
            e_ref[...] = e_idx
            gate_ref[...] = gates


def _proj_ln(m, w, w_index, b, res, g, beta, wr_t, br, *, name):
    nt = _tok_operands(m)[2]
    m_arrays, m_specs, _ = _tok_operands(m, lambda i: jnp.minimum(i, nt - 1))
    res_arrays, res_specs, _ = _tok_operands(res, lambda i: jnp.maximum(i - 1, 0))
    k = w.shape[-2]
    tm = TOK_TILE
    n = nt * tm
    row = lambda i: (jnp.maximum(i - 1, 0), 0)
    const = lambda i: (0, 0)
    x, x_rows, e_idx, gates = pl.pallas_call(
        functools.partial(_proj_ln_kernel, n_m=len(m_arrays), n_res=len(res_arrays)),
        grid=(nt + 1,),
        in_specs=m_specs + [
            pl.BlockSpec((None,) * len(w_index) + (k, D_MODEL), lambda i: w_index + (0, 0),
                         pipeline_mode=pl.Buffered(1)),
            pl.BlockSpec((1, D_MODEL), const),
        ] + res_specs + [
            pl.BlockSpec((1, D_MODEL), const),
            pl.BlockSpec((1, D_MODEL), const),
            pl.BlockSpec((N_EXPERTS, D_MODEL), const),
            pl.BlockSpec((N_EXPERTS, 1), const),
        ],
        out_specs=[
            pl.BlockSpec((tm, D_MODEL), row),
            pl.BlockSpec((tm * ROW_SUBLANES, LANES), row),
            pl.BlockSpec((None, 2, tm), lambda i: (jnp.maximum(i - 1, 0), 0, 0)),
            pl.BlockSpec((None, 2, tm), lambda i: (jnp.maximum(i - 1, 0), 0, 0)),
        ],
        out_shape=[
            jax.ShapeDtypeStruct((n, D_MODEL), F32),
            jax.ShapeDtypeStruct((n * ROW_SUBLANES, LANES), jnp.uint32),
            jax.ShapeDtypeStruct((nt, 2, tm), jnp.int32),
            jax.ShapeDtypeStruct((nt, 2, tm), F32),
        ],
        scratch_shapes=[pltpu.VMEM((k, D_MODEL), BF16),
                        pltpu.VMEM((tm, D_MODEL), F32), pltpu.VMEM((tm, D_MODEL), F32)],
        compiler_params=_params(("arbitrary",), 52),
        name=name,
    )(*m_arrays, w, b, *res_arrays, g, beta, wr_t, br)
    e_idx = e_idx.transpose(1, 0, 2).reshape(2, n)
    gates = gates.transpose(1, 0, 2).reshape(2, n)
    return x, x_rows, e_idx, gates


def _plan_kernel(e_ref, pos_ref, meta_ref, rank_ref):
    nrow = e_ref.shape[0]
    ri = lax.broadcasted_iota(jnp.int32, (LANES, LANES), 0)
    ci = lax.broadcasted_iota(jnp.int32, (LANES, LANES), 1)
    tri = jnp.where(ri <= ci, 1.0, 0.0).astype(BF16)
    sub = lax.broadcasted_iota(jnp.int32, (N_EXPERTS, LANES), 0)

    def count_body(b, base):
        rows = [b * PLAN_UNROLL + u for u in range(PLAN_UNROLL)]
        onehots = [sub == e_ref[pl.ds(r, 1), :] for r in rows]
        locs = [jnp.dot(jnp.where(oh, 1.0, 0.0).astype(BF16), tri, preferred_element_type=F32)
                for oh in onehots]
        for r, onehot, loc in zip(rows, onehots, locs):
            rank_ref[pl.ds(r, 1), :] = jnp.sum(jnp.where(onehot, base + loc - 1.0, 0.0),
                                               axis=0, keepdims=True)
            base = base + jnp.broadcast_to(loc[:, LANES - 1:LANES], (N_EXPERTS, LANES))
        return base

    count = lax.fori_loop(0, nrow // PLAN_UNROLL, count_body, jnp.zeros((N_EXPERTS, LANES), F32))
    ntile = jnp.floor((count + (MOE_TILE - 1.0)) * (1.0 / MOE_TILE))
    offs = []
    acc = jnp.zeros((1, LANES), F32)
    for e in range(N_EXPERTS):
        offs.append(acc)
        acc = acc + ntile[e:e + 1, :]
    tile_off = jnp.concatenate(offs, axis=0)
    tile_end = tile_off + ntile
    lane = lax.broadcasted_iota(jnp.int32, (N_EXPERTS, LANES), 1).astype(F32)
    tile_expert = jnp.sum(jnp.where(tile_end <= lane, 1.0, 0.0), axis=0, keepdims=True)
    tile_expert = jnp.minimum(tile_expert, N_EXPERTS - 1.0)
    meta = jnp.concatenate([tile_expert, acc, jnp.zeros((6, LANES), F32)], axis=0)
    meta_ref[...] = meta.astype(jnp.int32)
    row_off = tile_off * float(MOE_TILE)

    def pos_body(r, c):
        onehot = sub == e_ref[pl.ds(r, 1), :]
        p = jnp.sum(jnp.where(onehot, row_off, 0.0), axis=0, keepdims=True) + rank_ref[pl.ds(r, 1), :]
        pos_ref[pl.ds(r, 1), :] = p.astype(jnp.int32)
        return c

    lax.fori_loop(0, nrow, pos_body, 0)


def _plan(e_idx, *, name):
    n2 = e_idx.shape[0] * e_idx.shape[1]
    assert n2 % (LANES * PLAN_UNROLL) == 0
    e2d = e_idx.reshape(n2 // LANES, LANES)
    pos, meta = pl.pallas_call(
        _plan_kernel,
        out_shape=[jax.ShapeDtypeStruct(e2d.shape, jnp.int32),
                   jax.ShapeDtypeStruct((8, LANES), jnp.int32)],
        scratch_shapes=[pltpu.VMEM(e2d.shape, F32)],
        name=name,
    )(e2d)
    return pos.reshape(n2), meta[0], meta[1, :1]


def _invert_kernel(pos_ref, pair_ref):
    n_rows = pair_ref.shape[0]
    n_pairs = pos_ref.shape[0]

    def fill_body(b, c):
        for u in range(SCALAR_UNROLL):
            pair_ref[b * SCALAR_UNROLL + u] = -1
        return c

    def pair_body(b, c):
        rows = [pos_ref[b * SCALAR_UNROLL + u] for u in range(SCALAR_UNROLL)]
        for u in range(SCALAR_UNROLL):
            pair_ref[rows[u]] = b * SCALAR_UNROLL + u
        return c

    lax.fori_loop(0, n_rows // SCALAR_UNROLL, fill_body, 0)
    lax.fori_loop(0, n_pairs // SCALAR_UNROLL, pair_body, 0)


def _invert(pos, n_rows, *, name):
    return pl.pallas_call(
        _invert_kernel,
        grid_spec=pltpu.PrefetchScalarGridSpec(
            num_scalar_prefetch=1,
            grid=(1,),
            in_specs=[],
            out_specs=pl.BlockSpec(memory_space=pltpu.SMEM),
        ),
        out_shape=jax.ShapeDtypeStruct((n_rows,), jnp.int32),
        name=name,
    )(pos)


def _expert_changed(te_ref, i):
    return jnp.logical_or(i == 0, te_ref[i] != te_ref[jnp.maximum(i - 1, 0)])


def _moe_up_kernel(pair_ref, te_ref, nu_ref, x_hbm, wg_ref, wu_ref, h_ref,
                   wg_bf, wu_bf, x_rows, x_bf, sem, *, n_tok, n_tiles):
    i = pl.program_id(0)
    nu = nu_ref[0]

    def row_copy(tile, r):
        p = pair_ref[tile * MOE_TILE + r]
        tok = jnp.where(p >= n_tok, p - n_tok, jnp.maximum(p, 0))
        src = pl.ds(pl.multiple_of(tok * ROW_SUBLANES, ROW_SUBLANES), ROW_SUBLANES)
        dst = pl.ds(r * ROW_SUBLANES, ROW_SUBLANES)
        return pltpu.make_async_copy(x_hbm.at[src, :], x_rows.at[dst, :], sem)

    def wait_tile():
        pltpu.make_async_copy(x_hbm.at[pl.ds(0, MOE_TILE * ROW_SUBLANES), :], x_rows, sem).wait()

    @pl.when(i == 0)
    def _():
        def body(rb, c):
            for u in range(DMA_UNROLL):
                row_copy(0, rb * DMA_UNROLL + u).start()
            return c
        lax.fori_loop(0, MOE_TILE // DMA_UNROLL, body, 0)

    @pl.when(i <= nu)
    def _():
        wait_tile()

    @pl.when(i < nu)
    def _():
        @pl.when(_expert_changed(te_ref, i))
        def _():
            _cast_rows(wg_ref, wg_bf)
            _cast_rows(wu_ref, wu_bf)

        _load_packed_rows(x_rows, x_bf)
        next_tile = jnp.minimum(i + 1, n_tiles - 1)
        for r in range(MOE_TILE):
            row_copy(next_tile, r).start(priority=r % 2)
        x = x_bf[...]
        a = jnp.dot(x, wg_bf[...], preferred_element_type=F32)
        b = jnp.dot(x, wu_bf[...], preferred_element_type=F32)
        h_ref[...] = (jax.nn.silu(a) * b).astype(BF16)

    @pl.when(jnp.logical_and(i == n_tiles - 1, i < nu))
    def _():
        wait_tile()

    @pl.when(i >= nu)
    def _():
        h_ref[...] = jnp.zeros_like(h_ref)


def _moe_down_kernel(pair_ref, te_ref, nu_ref, h_ref, wd_ref, out_hbm,
                     wd_bf, ya, yb, sems, tsem, *, n_tok, n_tiles):
    i = pl.program_id(0)
    nu = nu_ref[0]
    bufs = (ya, yb)
    trash = 2 * n_tok

    def row_copy(tile, r, buf, sem):
        p = pair_ref[tile * MOE_TILE + r]
        dst = jnp.where(p < 0, trash + r, p)
        return pltpu.make_async_copy(buf.at[pl.ds(r, 1), :], out_hbm.at[pl.ds(dst, 1), :], sem)

    def wait_tile(buf, sem):
        pltpu.make_async_copy(buf, out_hbm.at[pl.ds(0, MOE_TILE), :], sem).wait()

    @pl.when(i == 0)
    def _():
        yb[...] = jnp.zeros_like(yb)
        fill = pltpu.make_async_copy(yb, out_hbm.at[pl.ds(trash, MOE_TILE), :], tsem)
        fill.start()
        fill.wait()

    for parity in range(2):
        cur, prev = bufs[parity], bufs[1 - parity]
        cur_sem, prev_sem = sems.at[parity], sems.at[1 - parity]
        mine = i % 2 == parity

        @pl.when(jnp.logical_and(mine, jnp.logical_and(i >= 1, i - 1 <= nu)))
        def _():
            wait_tile(cur, cur_sem)

        @pl.when(jnp.logical_and(mine, i < nu))
        def _():
            @pl.when(_expert_changed(te_ref, i))
            def _():
                _cast_rows(wd_ref, wd_bf)

            prev_tile = jnp.maximum(i - 1, 0)
            for r in range(MOE_TILE):
                row_copy(prev_tile, r, prev, prev_sem).start(priority=r % 2)
            cur[...] = jnp.dot(h_ref[...], wd_bf[...], preferred_element_type=F32)

        @pl.when(jnp.logical_and(mine, i == nu))
        def _():
            def body(rb, c):
                for u in range(DMA_UNROLL):
                    row_copy(i - 1, rb * DMA_UNROLL + u, prev, prev_sem).start()
                return c
            lax.fori_loop(0, MOE_TILE // DMA_UNROLL, body, 0)

            @pl.when(i == n_tiles)
            def _():
                wait_tile(prev, prev_sem)


def _moe_ffn(x, pair, tile_expert, n_used, w_gate, w_up, w_down, layer):
    n_tok = x.shape[0] // ROW_SUBLANES
    n_tiles = pair.shape[0] // MOE_TILE

    def tile(i, pr, te, nu):
        return (jnp.minimum(i, nu[0] - 1), 0)

    def expert(i, pr, te, nu):
        return (layer, te[jnp.minimum(i, nu[0] - 1)], 0, 0)

    def weight_spec(k, n):
        return pl.BlockSpec((None, None, k, n), expert)

    row_bufs = [pltpu.VMEM((MOE_TILE, D_MODEL), F32), pltpu.VMEM((MOE_TILE, D_MODEL), F32),
                pltpu.SemaphoreType.DMA((2,))]
    h = pl.pallas_call(
        functools.partial(_moe_up_kernel, n_tok=n_tok, n_tiles=n_tiles),
        grid_spec=pltpu.PrefetchScalarGridSpec(
            num_scalar_prefetch=3,
            grid=(n_tiles,),
            in_specs=[pl.BlockSpec(memory_space=pl.ANY),
                      weight_spec(D_MODEL, D_EXPERT),
                      weight_spec(D_MODEL, D_EXPERT)],
            out_specs=pl.BlockSpec((MOE_TILE, D_EXPERT), lambda i, pr, te, nu: (i, 0)),
            scratch_shapes=[pltpu.VMEM((D_MODEL, D_EXPERT), BF16),
                            pltpu.VMEM((D_MODEL, D_EXPERT), BF16),
                            pltpu.VMEM((MOE_TILE * ROW_SUBLANES, LANES), jnp.uint32),
                            pltpu.VMEM((MOE_TILE, D_MODEL), BF16),
                            pltpu.SemaphoreType.DMA(())],
        ),
        out_shape=jax.ShapeDtypeStruct((n_tiles * MOE_TILE, D_EXPERT), BF16),
        compiler_params=_params(("arbitrary",), 56),
        name=f"moe_up_{layer}",
    )(pair, tile_expert, n_used, x, w_gate, w_up)
    return pl.pallas_call(
        functools.partial(_moe_down_kernel, n_tok=n_tok, n_tiles=n_tiles),
        grid_spec=pltpu.PrefetchScalarGridSpec(
            num_scalar_prefetch=3,
            grid=(n_tiles + 1,),
            in_specs=[pl.BlockSpec((MOE_TILE, D_EXPERT), tile),
                      weight_spec(D_EXPERT, D_MODEL)],
            out_specs=pl.BlockSpec(memory_space=pl.ANY),
            scratch_shapes=[pltpu.VMEM((D_EXPERT, D_MODEL), BF16)] + row_bufs
            + [pltpu.SemaphoreType.DMA(())],
        ),
        out_shape=jax.ShapeDtypeStruct((2 * n_tok + MOE_TILE, D_MODEL), F32),
        compiler_params=_params(("arbitrary",), 40),
        name=f"moe_down_{layer}",
    )(pair, tile_expert, n_used, h, w_down)


def _combine(y0_ref, y1_ref, res_ref, gate_ref, g_ref, beta_ref):
    gate = gate_ref[...]
    ffn = gate[:, 0:1] * y0_ref[...] + gate[:, 1:2] * y1_ref[...]
    return _layer_norm(ALPHA * res_ref[...] + ffn, g_ref[...], beta_ref[...])


def _combine_specs(n):
    nt = n // TOK_TILE
    row = lambda i: (i, 0)
    const = lambda i: (0, 0)
    return [pl.BlockSpec((TOK_TILE, D_MODEL), row),
            pl.BlockSpec((TOK_TILE, D_MODEL), lambda i: (i + nt, 0)),
            pl.BlockSpec((TOK_TILE, D_MODEL), row),
            pl.BlockSpec((TOK_TILE, 2), row),
            pl.BlockSpec((1, D_MODEL), const),
            pl.BlockSpec((1, D_MODEL), const)]


def _combine_split_kernel(y0_ref, y1_ref, res_ref, gate_ref, g_ref, beta_ref, prompt_ref, sample_ref):
    x = _combine(y0_ref, y1_ref, res_ref, gate_ref, g_ref, beta_ref)
    is_sample = pl.program_id(0) == pl.num_programs(0) - 1

    @pl.when(jnp.logical_not(is_sample))
    def _():
        prompt_ref[...] = x

    @pl.when(is_sample)
    def _():
        sample_ref[...] = x


def _combine_split(ys, res, gates_col, g, beta, *, name):
    n = res.shape[0]
    tm = TOK_TILE
    nt = n // tm
    return pl.pallas_call(
        _combine_split_kernel,
        grid=(nt,),
        in_specs=_combine_specs(n),
        out_specs=[pl.BlockSpec((tm, D_MODEL), lambda i: (jnp.minimum(i, nt - 2), 0)),
                   pl.BlockSpec((tm, D_MODEL), lambda i: (0, 0))],
        out_shape=[jax.ShapeDtypeStruct((n - tm, D_MODEL), F32),
                   jax.ShapeDtypeStruct((tm, D_MODEL), F32)],
        compiler_params=_params(("arbitrary",), 40),
        name=name,
    )(ys, ys, res, gates_col, g, beta)


def _load_weight(w_hbm, wbf_ref, stage_ref, sems):
    rows = stage_ref.shape[1]
    n_chunks = wbf_ref.shape[0] // rows

    def chunk_copy(c):
        return pltpu.make_async_copy(w_hbm.at[pl.ds(c * rows, rows), :], stage_ref.at[c % 2],
                                     sems.at[c % 2])

    chunk_copy(0).start()
    for c in range(n_chunks):
        if c + 1 < n_chunks:
            chunk_copy(c + 1).start()
        chunk_copy(c).wait()
        wbf_ref[c * rows:(c + 1) * rows, :] = stage_ref[c % 2].astype(BF16)


def _combine_qkv_kernel(y0_ref, y1_ref, res_ref, gate_ref, g_ref, beta_ref, wq_hbm, wkv_hbm,
                        x_ref, q_ref, kv_ref, wq_bf, wkv_bf, stage_q, stage_kv, sems, *, wq_index):
    @pl.when(pl.program_id(0) == 0)
    def _():
        wq = wq_hbm
        for k in wq_index:
            wq = wq.at[k]
        _load_weight(wq, wq_bf, stage_q, sems)
        _load_weight(wkv_hbm, wkv_bf, stage_kv, sems)

    x = _combine(y0_ref, y1_ref, res_ref, gate_ref, g_ref, beta_ref)
    x_ref[...] = x
    x_bf = x.astype(BF16)
    q = jnp.dot(x_bf, wq_bf[...], preferred_element_type=F32) * (HEAD_DIM ** -0.5)
    q_ref[...] = q.astype(BF16)
    kv_ref[...] = jnp.dot(x_bf, wkv_bf[...], preferred_element_type=F32)


def _combine_qkv(ys, res, gates_col, g, beta, w_q, wq_index, w_kv, *, name):
    n = res.shape[0]
    tm = TOK_TILE
    row = lambda i: (i, 0)
    hbm = pl.BlockSpec(memory_space=pl.ANY)
    return pl.pallas_call(
        functools.partial(_combine_qkv_kernel, wq_index=wq_index),
        grid=(n // tm,),
        in_specs=_combine_specs(n) + [hbm, hbm],
        out_specs=[pl.BlockSpec((tm, D_MODEL), row),
                   pl.BlockSpec((tm, D_MODEL), row),
                   pl.BlockSpec((tm, 2 * KV_DIM), row)],
        out_shape=[jax.ShapeDtypeStruct((n, D_MODEL), F32),
                   jax.ShapeDtypeStruct((n, D_MODEL), BF16),
                   jax.ShapeDtypeStruct((n, 2 * KV_DIM), F32)],
        scratch_shapes=[pltpu.VMEM((D_MODEL, D_MODEL), BF16),
                        pltpu.VMEM((D_MODEL, 2 * KV_DIM), BF16),
                        pltpu.VMEM((2, CAST_ROWS, D_MODEL), F32),
                        pltpu.VMEM((2, CAST_ROWS, 2 * KV_DIM), F32),
                        pltpu.SemaphoreType.DMA((2,))],
        compiler_params=_params(("arbitrary",), 52),
        name=name,
    )(ys, ys, res, gates_col, g, beta, w_q, w_kv)


def _moe_block(x_rows, e_idx, w_gate, w_up, w_down, layer):
    n = x_rows.shape[0] // ROW_SUBLANES
    n_tiles = -(-(2 * n + N_EXPERTS * (MOE_TILE - 1)) // MOE_TILE)
    pos, tile_expert, n_used = _plan(e_idx, name=f"moe_plan_{layer}")
    pair = _invert(pos, n_tiles * MOE_TILE, name=f"moe_invert_{layer}")
    return _moe_ffn(x_rows, pair, tile_expert, n_used, w_gate, w_up, w_down, layer)


def _sigmoid(x):
    return 0.5 * jnp.tanh(0.5 * x) + 0.5


def _log_sigmoid(x):
    return -(jnp.maximum(-x, 0.0) + jnp.log1p(jnp.exp(-jnp.abs(x))))


def _lru_gate_block(xc, n, wrg_bf, wig_bf, brg_ref, big_ref, lam_ref):
    cols = slice(n * LRU_BLOCK, (n + 1) * LRU_BLOCK)
    xb = xc.astype(BF16)
    r = _sigmoid(jnp.dot(xb, wrg_bf[n], preferred_element_type=F32) + brg_ref[:, cols])
    i = _sigmoid(jnp.dot(xb, wig_bf[n], preferred_element_type=F32) + big_ref[:, cols])
    log_a = LRU_C * r * _log_sigmoid(lam_ref[:, cols])
    a = jnp.exp(log_a)
    u = xc * i * jnp.sqrt(-jnp.tanh(log_a) * (a * a + 1.0))
    return a, u


def _cast_gate_weights(wrg_ref, wig_ref, wrg_bf, wig_bf):
    for n in range(LRU_BLOCKS):
        wrg_bf[n] = wrg_ref[n].astype(BF16)
        wig_bf[n] = wig_ref[n].astype(BF16)


def _lru_prompt_kernel(xb_ref, yb_ref, cw_ref, cb_ref, wrg_ref, wig_ref, brg_ref, big_ref, lam_ref,
                       m_ref, conv_ref, hlast_ref, xs, tail, a_s, u_s, hs_t, h_s, wrg_bf, wig_bf):
    b = pl.program_id(0)
    j = pl.program_id(1)
    tt = m_ref.shape[0]
    seg_len = tt // SEGS
    taps = CONV_WIDTH - 1
    head = SEGS * taps

    @pl.when(jnp.logical_and(b == 0, j == 0))
    def _():
        _cast_gate_weights(wrg_ref, wig_ref, wrg_bf, wig_bf)

    @pl.when(j == 0)
    def _():
        tail[...] = jnp.zeros_like(tail)
        h_s[...] = jnp.zeros_like(h_s)

    for q in range(seg_len):
        xs[head + SEGS * q:head + SEGS * (q + 1), :] = jnp.concatenate(
            [xb_ref[pl.ds(CHUNKS * q + c, SEGS, stride=CHUNKS * seg_len), :] for c in range(CHUNKS)],
            axis=1)
    sub = lax.broadcasted_iota(jnp.int32, (SEGS, D_MODEL), 0)
    for k in range(taps):
        last = head + SEGS * (seg_len - taps + k)
        joined = jnp.where(sub == SEGS - 1, tail[SEGS * k:SEGS * (k + 1), :], xs[last:last + SEGS, :])
        xs[SEGS * k:SEGS * (k + 1), :] = pltpu.roll(joined, 1, axis=0)
    tail[...] = xs[head + SEGS * (seg_len - taps):head + SEGS * seg_len, :]

    for n in range(LRU_BLOCKS):
        cols = slice(n * LRU_BLOCK, (n + 1) * LRU_BLOCK)
        xc = cb_ref[:, cols] + cw_ref[0:1, cols] * xs[0:tt, cols]
        for k in range(1, CONV_WIDTH):
            xc = xc + cw_ref[k:k + 1, cols] * xs[SEGS * k:SEGS * k + tt, cols]
        a, u = _lru_gate_block(xc, n, wrg_bf, wig_bf, brg_ref, big_ref, lam_ref)
        a_s[:, cols] = a
        u_s[:, cols] = u

    def scan_body(q, carry):
        h, prod = carry
        rows = pl.ds(pl.multiple_of(q * SEGS, SEGS), SEGS)
        a = a_s[rows, :]
        h = a * h + u_s[rows, :]
        prod = a * prod
        u_s[rows, :] = h
        a_s[rows, :] = prod
        return h, prod

    h_end, prod_end = lax.fori_loop(
        0, seg_len, scan_body,
        (jnp.zeros((SEGS, D_MODEL), F32), jnp.ones((SEGS, D_MODEL), F32)))
    state = h_s[...]
    entering = []
    for s in range(SEGS):
        entering.append(state)
        state = h_end[s:s + 1, :] + prod_end[s:s + 1, :] * state
    h_s[...] = state
    enter = jnp.concatenate(entering, axis=0)

    def fix_body(q, carry):
        rows = pl.ds(pl.multiple_of(q * SEGS, SEGS), SEGS)
        h = u_s[rows, :] + a_s[rows, :] * enter
        for c in range(CHUNKS):
            hs_t[pl.ds(CHUNKS * q + c, SEGS, stride=CHUNKS * seg_len), :] = h[:, c * LANES:(c + 1) * LANES]
        return carry

    lax.fori_loop(0, seg_len, fix_body, 0)
    hs = jnp.concatenate([hs_t[pl.ds(c, tt, stride=CHUNKS), :] for c in range(CHUNKS)], axis=1)
    m_ref[...] = (hs * yb_ref[...].astype(F32)).astype(BF16)

    @pl.when(j == pl.num_programs(1) - 1)
    def _():
        for k in range(taps):
            conv_ref[k:k + 1, :] = tail[SEGS * k + SEGS - 1:SEGS * (k + 1), :]
        hlast_ref[...] = state


def _lru_prompt(xb, yb, batch, seq, cw, cb, wrg, wig, brg, big, lam, *, tt=256):
    nj = seq // tt
    row = lambda b, j: (b * nj + j, 0)
    const2 = lambda b, j: (0, 0)
    const3 = lambda b, j: (0, 0, 0)
    return pl.pallas_call(
        _lru_prompt_kernel,
        grid=(batch, nj),
        in_specs=[
            pl.BlockSpec((tt * CHUNKS, LANES), row),
            pl.BlockSpec((tt, D_MODEL), row),
            pl.BlockSpec((CONV_WIDTH, D_MODEL), const2),
            pl.BlockSpec((1, D_MODEL), const2),
            pl.BlockSpec((LRU_BLOCKS, LRU_BLOCK, LRU_BLOCK), const3),
            pl.BlockSpec((LRU_BLOCKS, LRU_BLOCK, LRU_BLOCK), const3),
            pl.BlockSpec((1, D_MODEL), const2),
            pl.BlockSpec((1, D_MODEL), const2),
            pl.BlockSpec((1, D_MODEL), const2),
        ],
        out_specs=[
            pl.BlockSpec((tt, D_MODEL), row),
            pl.BlockSpec((None, CONV_WIDTH - 1, D_MODEL), lambda b, j: (b, 0, 0)),
            pl.BlockSpec((None, 1, D_MODEL), lambda b, j: (b, 0, 0)),
        ],
        out_shape=[
            jax.ShapeDtypeStruct((batch * seq, D_MODEL), BF16),
            jax.ShapeDtypeStruct((batch, CONV_WIDTH - 1, D_MODEL), F32),
            jax.ShapeDtypeStruct((batch, 1, D_MODEL), F32),
        ],
        scratch_shapes=[
            pltpu.VMEM((tt + SEGS * (CONV_WIDTH - 1), D_MODEL), F32),
            pltpu.VMEM((SEGS * (CONV_WIDTH - 1), D_MODEL), F32),
            pltpu.VMEM((tt, D_MODEL), F32),
            pltpu.VMEM((tt, D_MODEL), F32),
            pltpu.VMEM((tt * CHUNKS, LANES), F32),
            pltpu.VMEM((1, D_MODEL), F32),
            pltpu.VMEM((LRU_BLOCKS, LRU_BLOCK, LRU_BLOCK), BF16),
            pltpu.VMEM((LRU_BLOCKS, LRU_BLOCK, LRU_BLOCK), BF16),
        ],
        compiler_params=_params(("arbitrary", "arbitrary"), 40),
        name="lru_prompt",
    )(xb, yb, cw, cb, wrg, wig, brg, big, lam)


def _lru_sample_kernel(xb_ref, yb_ref, cs_ref, h0_ref, cw_ref, cb_ref, wrg_ref, wig_ref,
                       brg_ref, big_ref, lam_ref, m_ref, conv_ref, hlast_ref, wrg_bf, wig_bf, *, steps):
    batch = h0_ref.shape[0]
    _cast_gate_weights(wrg_ref, wig_ref, wrg_bf, wig_bf)
    m_ref[steps * batch:, :] = jnp.zeros((m_ref.shape[0] - steps * batch, D_MODEL), BF16)

    def slab(t, cols):
        if t < CONV_WIDTH - 1:
            return cs_ref[t, :, cols]
        t -= CONV_WIDTH - 1
        first, stop, _ = cols.indices(D_MODEL)
        chunks = range(first // LANES, stop // LANES)
        return jnp.concatenate(
            [xb_ref[pl.ds(t * batch * CHUNKS + c, batch, stride=CHUNKS), :] for c in chunks], axis=1)

    for n in range(LRU_BLOCKS):
        cols = slice(n * LRU_BLOCK, (n + 1) * LRU_BLOCK)
        h = h0_ref[:, cols]
        for t in range(steps):
            xc = cb_ref[:, cols] + cw_ref[0:1, cols] * slab(t, cols)
            for k in range(1, CONV_WIDTH):
                xc = xc + cw_ref[k:k + 1, cols] * slab(t + k, cols)
            a, u = _lru_gate_block(xc, n, wrg_bf, wig_bf, brg_ref, big_ref, lam_ref)
            h = a * h + u
            rows = slice(t * batch, (t + 1) * batch)
            m_ref[rows, cols] = (h * yb_ref[rows, cols].astype(F32)).astype(BF16)
        hlast_ref[:, cols] = h
    for k in range(CONV_WIDTH - 1):
        conv_ref[k] = slab(steps + k, slice(None))


def _lru_sample(xb, yb, tile, steps, conv_state, h0, cw, cb, wrg, wig, brg, big, lam):
    batch = h0.shape[0]
    tok = pl.BlockSpec((TOK_TILE, D_MODEL), lambda i: (tile, 0))
    tok_chunks = pl.BlockSpec((TOK_TILE * CHUNKS, LANES), lambda i: (tile, 0))
    full = lambda a: pl.BlockSpec(a.shape, lambda i: (0,) * a.ndim)
    small = (conv_state, h0, cw, cb, wrg, wig, brg, big, lam)
    return pl.pallas_call(
        functools.partial(_lru_sample_kernel, steps=steps),
        grid=(1,),
        in_specs=[tok_chunks, tok] + [full(a) for a in small],
        out_specs=[
            pl.BlockSpec((TOK_TILE, D_MODEL), lambda i: (0, 0)),
            pl.BlockSpec((CONV_WIDTH - 1, batch, D_MODEL), lambda i: (0, 0, 0)),
            pl.BlockSpec((batch, D_MODEL), lambda i: (0, 0)),
        ],
        out_shape=[
            jax.ShapeDtypeStruct((TOK_TILE, D_MODEL), BF16),
            jax.ShapeDtypeStruct((CONV_WIDTH - 1, batch, D_MODEL), F32),
            jax.ShapeDtypeStruct((batch, D_MODEL), F32),
        ],
        scratch_shapes=[
            pltpu.VMEM((LRU_BLOCKS, LRU_BLOCK, LRU_BLOCK), BF16),
            pltpu.VMEM((LRU_BLOCKS, LRU_BLOCK, LRU_BLOCK), BF16),
        ],
        compiler_params=_params(("arbitrary",), 32),
        name="lru_sample",
    )(xb, yb, *small)


def _rel_bucket(dist):
    n = jnp.maximum(dist, 0)
    max_exact = N_BUCKETS // 2
    nf = jnp.maximum(n, 1).astype(F32)
    large = max_exact + (jnp.log(nf / max_exact) / math.log(MAX_DISTANCE / max_exact)
                         * (N_BUCKETS - max_exact)).astype(jnp.int32)
    large = jnp.minimum(large, N_BUCKETS - 1)
    return jnp.where(n < max_exact, n, large)


def _masked_buckets(dist):
    valid = (dist >= 0) & (dist < WINDOW)
    return jnp.where(valid, _rel_bucket(dist), -1).astype(jnp.int32)


def _build_bias(bucket, tab_ref, head):
    def body(bi, acc):
        return jnp.where(bucket == bi, tab_ref[bi * N_HEADS + head], acc)
    return lax.fori_loop(0, N_BUCKETS, body, jnp.full(bucket.shape, NEG_INF, F32))


def _softmax_pv(s, sink, v):
    m = jnp.maximum(jnp.max(s, axis=-1, keepdims=True), sink)
    p = jnp.exp(s - m)
    den = jnp.sum(p, axis=-1, keepdims=True) + jnp.exp(sink - m)
    return jnp.dot(p.astype(BF16), v, preferred_element_type=F32) / den


def _attn_prompt_kernel(q_ref, kvp_ref, kvc_ref, bucket_ref, tab_ref, sink_ref, o_ref, bias_s):
    b = pl.program_id(0)
    n = pl.program_id(1)

    @pl.when(jnp.logical_and(b == 0, n == 0))
    def _():
        bucket = bucket_ref[...]

        col = lax.broadcasted_iota(jnp.int32, (WINDOW, 2 * WINDOW), 1)

        def head_body(h, c):
            bias = _build_bias(bucket, tab_ref, h)
            sink = sink_ref[h]
            g = h // GROUP
            r0 = pl.multiple_of((h % GROUP) * WINDOW, WINDOW)
            bias_s[0, g, pl.ds(r0, WINDOW), :] = jnp.where(col == 0, sink, bias)
            bias_s[1, g, pl.ds(r0, WINDOW), :] = jnp.where(
                col == 0, sink, jnp.where(col < WINDOW, NEG_INF, bias))
            return c

        lax.fori_loop(0, N_HEADS, head_body, 0)

    first = (n == 0).astype(jnp.int32)
    row = lax.broadcasted_iota(jnp.int32, kvp_ref.shape, 0)
    kv_prev = jnp.where(row == 0, 0.0, kvp_ref[...])
    kv = jnp.concatenate([kv_prev, kvc_ref[...]], axis=0).astype(BF16)
    ones = jnp.ones((2 * WINDOW, 2 * HEAD_DIM), BF16)
    lane = lax.broadcasted_iota(jnp.int32, (WINDOW, 2 * HEAD_DIM), 1)
    for g in range(N_KV_HEADS):
        heads = range(g * GROUP, (g + 1) * GROUP)
        kg = kv[:, g * HEAD_DIM:(g + 1) * HEAD_DIM]
        vg = kv[:, KV_DIM + g * HEAD_DIM:KV_DIM + (g + 1) * HEAD_DIM]
        v_ext = jnp.concatenate([vg, vg, ones], axis=1)
        qg = jnp.concatenate([q_ref[:, h * HEAD_DIM:(h + 1) * HEAD_DIM] for h in heads], axis=0)
        s = lax.dot_general(qg, kg, (((1,), (1,)), ((), ())), preferred_element_type=F32)
        s = s + bias_s[first, g]
        p = jnp.exp(s - jnp.max(s, axis=-1, keepdims=True)).astype(BF16)
        o_ext = jnp.dot(p, v_ext, preferred_element_type=F32)
        o = o_ext[:, :2 * HEAD_DIM] * (1.0 / o_ext[:, 2 * HEAD_DIM:])
        for pair in range(GROUP // 2):
            even = o[2 * pair * WINDOW:(2 * pair + 1) * WINDOW]
            odd = o[(2 * pair + 1) * WINDOW:(2 * pair + 2) * WINDOW]
            c0 = (g * GROUP + 2 * pair) * HEAD_DIM
            o_ref[:, c0:c0 + 2 * HEAD_DIM] = jnp.where(lane < HEAD_DIM, even, odd).astype(BF16)


def _attn_prompt(q, kv, batch, seq, bucket, tab, sinks):
    nb = seq // WINDOW
    smem = pl.BlockSpec(memory_space=pltpu.SMEM)
    return pl.pallas_call(
        _attn_prompt_kernel,
        grid=(batch, nb),
        in_specs=[
            pl.BlockSpec((WINDOW, D_MODEL), lambda b, n: (b * nb + n, 0)),
            pl.BlockSpec((WINDOW, 2 * KV_DIM), lambda b, n: (jnp.maximum(b * nb + n - 1, 0), 0)),
            pl.BlockSpec((WINDOW, 2 * KV_DIM), lambda b, n: (b * nb + n, 0)),
            pl.BlockSpec((WINDOW, 2 * WINDOW), lambda b, n: (0, 0)),
            smem, smem,
        ],
        out_specs=pl.BlockSpec((WINDOW, D_MODEL), lambda b, n: (b * nb + n, 0)),
        out_shape=jax.ShapeDtypeStruct((batch * seq, D_MODEL), BF16),
        scratch_shapes=[pltpu.VMEM((2, N_KV_HEADS, GROUP * WINDOW, 2 * WINDOW), F32)],
        compiler_params=_params(("arbitrary", "arbitrary"), 32),
        name="attn_prompt",
    )(q, kv, kv, bucket, tab, sinks)


def _attn_sample_kernel(q_ref, k_ref, v_ref, bucket_ref, tab_ref, sink_ref, o_ref, bias_s):
    steps = q_ref.shape[0]

    @pl.when(pl.program_id(0) == 0)
    def _():
        bucket = bucket_ref[...]

        def head_body(h, c):
            bias_s[h] = _build_bias(bucket, tab_ref, h)
            return c

        lax.fori_loop(0, N_HEADS, head_body, 0)

    rows = lax.broadcasted_iota(jnp.int32, (GROUP * steps, 1), 0)
    for g in range(N_KV_HEADS):
        kg = k_ref[:, g * HEAD_DIM:(g + 1) * HEAD_DIM].astype(BF16)
        vg = v_ref[:, g * HEAD_DIM:(g + 1) * HEAD_DIM].astype(BF16)
        heads = range(g * GROUP, (g + 1) * GROUP)
        qg = jnp.concatenate([q_ref[:, h * HEAD_DIM:(h + 1) * HEAD_DIM] for h in heads], axis=0)
        bias = jnp.concatenate([bias_s[h] for h in heads], axis=0)
        sink = jnp.full((GROUP * steps, 1), sink_ref[g * GROUP], F32)
        for hh in range(1, GROUP):
            sink = jnp.where(rows >= hh * steps, sink_ref[g * GROUP + hh], sink)
        s = lax.dot_general(qg, kg, (((1,), (1,)), ((), ())), preferred_element_type=F32) + bias
        o = _softmax_pv(s, sink, vg)
        for hh, h in enumerate(heads):
            o_ref[:, h * HEAD_DIM:(h + 1) * HEAD_DIM] = o[hh * steps:(hh + 1) * steps].astype(BF16)


def _attn_sample(q, k_all, v_all, bucket, tab, sinks):
    batch, steps, _ = q.shape
    lk = k_all.shape[1]
    smem = pl.BlockSpec(memory_space=pltpu.SMEM)
    return pl.pallas_call(
        _attn_sample_kernel,
        grid=(batch,),
        in_specs=[
            pl.BlockSpec((None, steps, D_MODEL), lambda b: (b, 0, 0)),
            pl.BlockSpec((None, lk, KV_DIM), lambda b: (b, 0, 0)),
            pl.BlockSpec((None, lk, KV_DIM), lambda b: (b, 0, 0)),
            pl.BlockSpec((steps, lk), lambda b: (0, 0)),
            smem, smem,
        ],
        out_specs=pl.BlockSpec((None, steps, D_MODEL), lambda b: (b, 0, 0)),
        out_shape=jax.ShapeDtypeStruct((batch, steps, D_MODEL), BF16),
        scratch_shapes=[pltpu.VMEM((N_HEADS, steps, lk), F32)],
        compiler_params=_params(("arbitrary",), 32),
        name="attn_sample",
    )(q, k_all, v_all, bucket, tab, sinks)


def kernel(x_prompt, x_sample, state_conv, state_rnn, cache_k_win, cache_v_win, ln_g, ln_b, lru_w_x, lru_b_x, lru_w_y, lru_b_y, lru_conv_w, lru_conv_b, lru_w_rg, lru_b_rg, lru_w_ig, lru_b_ig, lru_lam, lru_w_out, lru_b_out, attn_w_kv, attn_w_q, attn_w_o, attn_sinks, rel_bias, moe_w_router, moe_b_router, moe_w_gate, moe_w_up, moe_w_down):
    bp, seq, _ = x_prompt.shape
    bs, steps, _ = x_sample.shape
    n_p = bp * seq
    n_s = bs * steps

    assert n_p % TOK_TILE == 0 and n_s <= TOK_TILE
    sample_tile = n_p // TOK_TILE

    def pad_tile(rows):
        return jnp.pad(rows, ((0, TOK_TILE - n_s), (0, 0)))

    x0 = (x_prompt.reshape(n_p, D_MODEL),
          pad_tile(x_sample.transpose(1, 0, 2).reshape(n_s, D_MODEL)))
    wr_t = moe_w_router.T
    br = moe_b_router.reshape(N_EXPERTS, 1)
    vec = lambda a: a.reshape(1, -1)

    xb = _linear(x0, lru_w_x, (0,), vec(lru_b_x[0]), F32, name="lru_in_x", chunk_rows=True)
    yb = _linear(x0, lru_w_y, (0,), vec(lru_b_y[0]), BF16, act="gelu", name="lru_in_y")
    lru_args = (lru_conv_w[0], vec(lru_conv_b[0]), lru_w_rg[0], lru_w_ig[0],
                vec(lru_b_rg[0]), vec(lru_b_ig[0]), vec(lru_lam[0]))
    m_p, conv_p, rnn_p = _lru_prompt(xb, yb, bp, seq, *lru_args)
    m_s, conv_s, rnn_s = _lru_sample(xb, yb, sample_tile, steps,
                                     state_conv[0].transpose(1, 0, 2), state_rnn[0], *lru_args)
    x1, x1_rows, e_idx, gates = _proj_ln((m_p, m_s), lru_w_out, (0,), vec(lru_b_out[0]), x0,
                                vec(ln_g[0, 0]), vec(ln_b[0, 0]), wr_t, br, name="lru_out_ln")
    ys = _moe_block(x1_rows, e_idx, moe_w_gate, moe_w_up, moe_w_down, 0)

    x2, q, kv = _combine_qkv(ys, x1, gates.T, vec(ln_g[0, 1]), vec(ln_b[0, 1]),
                             attn_w_q, (0,), attn_w_kv, name="moe_combine_qkv")
    tab = rel_bias.reshape(-1)
    sinks = attn_sinks[0]
    qi = jnp.arange(WINDOW)[:, None]
    kj = jnp.arange(2 * WINDOW)[None, :]
    o_p = _attn_prompt(q, kv, bp, seq, _masked_buckets(qi + WINDOW - kj), tab, sinks)
    kv_s = kv[n_p:n_p + n_s].reshape(steps, bs, 2, KV_DIM).transpose(2, 1, 0, 3)
    k_all = jnp.concatenate([cache_k_win.reshape(bs, WINDOW, KV_DIM), kv_s[0]], axis=1)
    v_all = jnp.concatenate([cache_v_win.reshape(bs, WINDOW, KV_DIM), kv_s[1]], axis=1)
    dist_s = jnp.arange(steps)[:, None] + WINDOW - jnp.arange(WINDOW + steps)[None, :]
    q_s = q[n_p:n_p + n_s].reshape(steps, bs, D_MODEL).transpose(1, 0, 2)
    o_s = _attn_sample(q_s, k_all, v_all, _masked_buckets(dist_s), tab, sinks)
    o_s = pad_tile(o_s.transpose(1, 0, 2).reshape(n_s, D_MODEL))
    x3, x3_rows, e_idx, gates = _proj_ln((o_p, o_s), attn_w_o, (0,), jnp.zeros((1, D_MODEL), F32), x2,
                                vec(ln_g[1, 0]), vec(ln_b[1, 0]), wr_t, br, name="attn_out_ln")
    ys = _moe_block(x3_rows, e_idx, moe_w_gate, moe_w_up, moe_w_down, 1)
    y_p, y_s = _combine_split(ys, x3, gates.T, vec(ln_g[1, 1]), vec(ln_b[1, 1]), name="moe_combine_1")

    y_prompt = y_p.reshape(bp, seq, D_MODEL)
    y_sample = y_s[:n_s].reshape(steps, bs, D_MODEL).transpose(1, 0, 2)
    kv_p = jnp.stack([kv[(b + 1) * seq - WINDOW:(b + 1) * seq] for b in range(bp)])
    kv_p = kv_p.reshape(bp, WINDOW, 2, N_KV_HEADS, HEAD_DIM)
    k_win_s = k_all[:, steps:].reshape(bs, WINDOW, N_KV_HEADS, HEAD_DIM)
    v_win_s = v_all[:, steps:].reshape(bs, WINDOW, N_KV_HEADS, HEAD_DIM)
    return (y_prompt, y_sample,
            conv_p[None], rnn_p.reshape(1, bp, D_MODEL),
            kv_p[:, :, 0], kv_p[:, :, 1],
            conv_s.transpose(1, 0, 2)[None], rnn_s[None],
            k_win_s, v_win_s)
```

```python
import functools
import math

import jax
import jax.numpy as jnp
from jax import lax
from jax.experimental import pallas as pl
from jax.experimental.pallas import tpu as pltpu

D_MODEL = 2048
DEPTH = 2
LRU_BLOCKS = 8
LRU_BLOCK = D_MODEL // LRU_BLOCKS
CONV_WIDTH = 4
LRU_C = 8.0
N_HEADS = 32
HEAD_DIM = 64
N_KV_HEADS = 8
GROUP = N_HEADS // N_KV_HEADS
KV_DIM = N_KV_HEADS * HEAD_DIM
WINDOW = 128
N_BUCKETS = 32
MAX_DISTANCE = 128
N_EXPERTS = 16
N_GROUPS = 4
EXPERTS_PER_GROUP = N_EXPERTS // N_GROUPS
D_EXPERT = 1024
ALPHA = (2 * DEPTH) ** 0.25
LN_EPS = 1e-5

LANES = 128
SEGS = 8
CHUNKS = D_MODEL // LANES
ROW_SUBLANES = D_MODEL // (2 * LANES)
MOE_TILE = 256
TOK_TILE = 256
DMA_UNROLL = 8
SCALAR_UNROLL = 32
PLAN_UNROLL = 4
SCORE_AHEAD = 1
CAST_ROWS = 256
BF16 = jnp.bfloat16
F32 = jnp.float32
NEG_INF = float("-inf")


def _params(sem, vmem_mb):
    return pltpu.CompilerParams(dimension_semantics=sem, vmem_limit_bytes=vmem_mb * 1024 * 1024)


def _cast_rows(src_ref, dst_ref):
    n = src_ref.shape[0] // CAST_ROWS

    def body(i, c):
        r = pl.multiple_of(i * CAST_ROWS, CAST_ROWS)
        dst_ref[pl.ds(r, CAST_ROWS), :] = src_ref[pl.ds(r, CAST_ROWS), :].astype(BF16)
        return c

    lax.fori_loop(0, n, body, 0)


def _layer_norm(z, g, b):
    mu = jnp.mean(z, axis=-1, keepdims=True)
    zc = z - mu
    var = jnp.mean(zc * zc, axis=-1, keepdims=True)
    return zc * lax.rsqrt(var + LN_EPS) * g + b


def _store_packed_rows(x_bf, rows_ref):
    n = x_bf.shape[0]
    bits = pltpu.bitcast(x_bf.astype(F32), jnp.uint32)
    packed = bits[:, D_MODEL // 2:] | (bits[:, :D_MODEL // 2] >> 16)
    for c in range(ROW_SUBLANES):
        rows_ref[pl.ds(c, n, stride=ROW_SUBLANES), :] = packed[:, c * LANES:(c + 1) * LANES]


def _load_packed_rows(rows_ref, x_bf_ref):
    n = x_bf_ref.shape[0]
    for c in range(ROW_SUBLANES):
        words = rows_ref[pl.ds(c, n, stride=ROW_SUBLANES), :]
        low = pltpu.bitcast(words << 16, F32).astype(BF16)
        high = pltpu.bitcast(words & jnp.uint32(0xFFFF0000), F32).astype(BF16)
        x_bf_ref[:, c * LANES:(c + 1) * LANES] = low
        x_bf_ref[:, D_MODEL // 2 + c * LANES:D_MODEL // 2 + (c + 1) * LANES] = high


def _tok_operands(x, tile_of=lambda i: i):
    if isinstance(x, tuple):
        xp, xs = x
        d = xp.shape[1]
        last_p = xp.shape[0] // TOK_TILE - 1
        specs = [pl.BlockSpec((TOK_TILE, d), lambda i, *_: (jnp.minimum(tile_of(i), last_p), 0)),
                 pl.BlockSpec((TOK_TILE, d), lambda i, *_: (0, 0))]
        return [xp, xs], specs, last_p + 2
    return ([x], [pl.BlockSpec((TOK_TILE, x.shape[1]), lambda i, *_: (tile_of(i), 0))],
            x.shape[0] // TOK_TILE)


def _tok_load(refs, is_sample=None):
    if len(refs) == 1:
        return refs[0][...]
    if is_sample is None:
        is_sample = pl.program_id(0) == pl.num_programs(0) - 1
    return jnp.where(is_sample, refs[1][...], refs[0][...])


def _linear_kernel(*refs, n_x, act, scale, chunk_rows):
    x_refs, (w_ref, b_ref, o_ref, wbf_ref) = refs[:n_x], refs[n_x:]

    @pl.when(pl.program_id(0) == 0)
    def _():
        _cast_rows(w_ref, wbf_ref)

    y = jnp.dot(_tok_load(x_refs).astype(BF16), wbf_ref[...], preferred_element_type=F32)
    y = y + b_ref[...]
    if act == "gelu":
        y = jax.nn.gelu(y)
    if scale != 1.0:
        y = y * scale
    if chunk_rows:
        n_chunks = y.shape[1] // LANES
        for c in range(n_chunks):
            o_ref[pl.ds(c, y.shape[0], stride=n_chunks), :] = y[:, c * LANES:(c + 1) * LANES]
    else:
        o_ref[...] = y.astype(o_ref.dtype)


def _linear(x, w, w_index, b, out_dtype, *, name, act=None, scale=1.0, chunk_rows=False):
    arrays, specs, nt = _tok_operands(x)
    k, nout = w.shape[-2:]
    w_block = (None,) * len(w_index) + (k, nout)
    if chunk_rows:
        out_block, out_rows, out_cols = (TOK_TILE * (nout // LANES), LANES), nt * TOK_TILE * (nout // LANES), LANES
    else:
        out_block, out_rows, out_cols = (TOK_TILE, nout), nt * TOK_TILE, nout
    return pl.pallas_call(
        functools.partial(_linear_kernel, n_x=len(arrays), act=act, scale=scale,
                          chunk_rows=chunk_rows),
        grid=(nt,),
        in_specs=specs + [
            pl.BlockSpec(w_block, lambda i: w_index + (0, 0), pipeline_mode=pl.Buffered(1)),
            pl.BlockSpec((1, nout), lambda i: (0, 0)),
        ],
        out_specs=pl.BlockSpec(out_block, lambda i: (i, 0)),
        out_shape=jax.ShapeDtypeStruct((out_rows, out_cols), out_dtype),
        scratch_shapes=[pltpu.VMEM((k, nout), BF16)],
        compiler_params=_params(("arbitrary",), 48),
        name=name,
    )(*arrays, w, b)


def _route(logits_t, b_router):
    aff = jax.nn.sigmoid(logits_t)
    sel = aff + b_router
    srow = [sel[e:e + 1, :] for e in range(N_EXPERTS)]
    arow = [aff[e:e + 1, :] for e in range(N_EXPERTS)]

    def top2_sum(v):
        pairs = [v[i] + v[j] for i in range(4) for j in range(i + 1, 4)]
        return functools.reduce(jnp.maximum, pairs)

    scores = [top2_sum(srow[4 * g:4 * g + 4]) for g in range(N_GROUPS)]
    best = scores[0]
    gi = jnp.zeros_like(best, dtype=jnp.int32)
    for g in range(1, N_GROUPS):
        upd = scores[g] > best
        best = jnp.where(upd, scores[g], best)
        gi = jnp.where(upd, g, gi)

    def pick_group(rows, j):
        out = rows[j]
        for g in range(1, N_GROUPS):
            out = jnp.where(gi == g, rows[4 * g + j], out)
        return out

    v = [pick_group(srow, j) for j in range(EXPERTS_PER_GROUP)]
    a = [pick_group(arow, j) for j in range(EXPERTS_PER_GROUP)]

    m1, i1 = v[0], jnp.zeros_like(gi)
    for j in range(1, EXPERTS_PER_GROUP):
        upd = v[j] > m1
        m1 = jnp.where(upd, v[j], m1)
        i1 = jnp.where(upd, j, i1)
    m2 = jnp.full_like(m1, NEG_INF)
    i2 = jnp.zeros_like(gi)
    for j in range(EXPERTS_PER_GROUP):
        cand = jnp.where(i1 == j, NEG_INF, v[j])
        upd = cand > m2
        m2 = jnp.where(upd, cand, m2)
        i2 = jnp.where(upd, j, i2)

    def pick_idx(rows, idx):
        out = rows[0]
        for j in range(1, EXPERTS_PER_GROUP):
            out = jnp.where(idx == j, rows[j], out)
        return out

    a1 = pick_idx(a, i1)
    a2 = pick_idx(a, i2)
    tot = a1 + a2
    e_idx = jnp.concatenate([gi * EXPERTS_PER_GROUP + i1, gi * EXPERTS_PER_GROUP + i2], axis=0)
    gates = jnp.concatenate([a1 / tot, a2 / tot], axis=0)
    return e_idx, gates


def _proj_ln_kernel(*refs, n_m, n_res):
    m_refs = refs[:n_m]
    w_ref, b_ref = refs[n_m:n_m + 2]
    res_refs = refs[n_m + 2:n_m + 2 + n_res]
    (g_ref, beta_ref, wr_ref, br_ref, x_ref, xrow_ref, e_ref, gate_ref,
     wbf_ref, ya, yb) = refs[n_m + 2 + n_res:]
    i = pl.program_id(0)
    n_tiles = pl.num_programs(0) - 1

    @pl.when(i == 0)
    def _():
        _cast_rows(w_ref, wbf_ref)
        yb[...] = jnp.zeros_like(yb)

    for parity, (cur, prev) in enumerate(((ya, yb), (yb, ya))):
        @pl.when(i % 2 == parity)
        def _():
            cur[...] = jnp.dot(_tok_load(m_refs, i >= n_tiles - 1), wbf_ref[...],
                               preferred_element_type=F32)
            y = prev[...] + b_ref[...]
            x = _layer_norm(ALPHA * _tok_load(res_refs, i == n_tiles) + y, g_ref[...], beta_ref[...])
            x_ref[...] = x
            x_bf = x.astype(BF16)
            _store_packed_rows(x_bf, xrow_ref)
            logits_t = lax.dot_general(wr_ref[...].astype(BF16), x_bf,
                                       (((1,), (1,)), ((), ())), preferred_element_type=F32)
            e_idx, gates = _route(logits_t, br_ref[...])
            e_ref[...] = e_idx
            gate_ref[...] = gates


def _proj_ln(m, w, w_index, b, res, g, beta, wr_t, br, *, name):
    nt = _tok_operands(m)[2]
    m_arrays, m_specs, _ = _tok_operands(m, lambda i: jnp.minimum(i, nt - 1))
    res_arrays, res_specs, _ = _tok_operands(res, lambda i: jnp.maximum(i - 1, 0))
    k = w.shape[-2]
    tm = TOK_TILE
    n = nt * tm
    row = lambda i: (jnp.maximum(i - 1, 0), 0)
    const = lambda i: (0, 0)
    x, x_rows, e_idx, gates = pl.pallas_call(
        functools.partial(_proj_ln_kernel, n_m=len(m_arrays), n_res=len(res_arrays)),
        grid=(nt + 1,),
        in_specs=m_specs + [
            pl.BlockSpec((None,) * len(w_index) + (k, D_MODEL), lambda i: w_index + (0, 0),
                         pipeline_mode=pl.Buffered(1)),
            pl.BlockSpec((1, D_MODEL), const),
        ] + res_specs + [
            pl.BlockSpec((1, D_MODEL), const),
            pl.BlockSpec((1, D_MODEL), const),
            pl.BlockSpec((N_EXPERTS, D_MODEL), const),
            pl.BlockSpec((N_EXPERTS, 1), const),
        ],
        out_specs=[
            pl.BlockSpec((tm, D_MODEL), row),
            pl.BlockSpec((tm * ROW_SUBLANES, LANES), row),
            pl.BlockSpec((None, 2, tm), lambda i: (jnp.maximum(i - 1, 0), 0, 0)),
            pl.BlockSpec((None, 2, tm), lambda i: (jnp.maximum(i - 1, 0), 0, 0)),
        ],
        out_shape=[
            jax.ShapeDtypeStruct((n, D_MODEL), F32),
            jax.ShapeDtypeStruct((n * ROW_SUBLANES, LANES), jnp.uint32),
            jax.ShapeDtypeStruct((nt, 2, tm), jnp.int32),
            jax.ShapeDtypeStruct((nt, 2, tm), F32),
        ],
        scratch_shapes=[pltpu.VMEM((k, D_MODEL), BF16),
                        pltpu.VMEM((tm, D_MODEL), F32), pltpu.VMEM((tm, D_MODEL), F32)],
        compiler_params=_params(("arbitrary",), 52),
        name=name,
    )(*m_arrays, w, b, *res_arrays, g, beta, wr_t, br)
    e_idx = e_idx.transpose(1, 0, 2).reshape(2, n)
    gates = gates.transpose(1, 0, 2).reshape(2, n)
    return x, x_rows, e_idx, gates


def _plan_kernel(e_ref, pos_ref, meta_ref, rank_ref):
    nrow = e_ref.shape[0]
    ri = lax.broadcasted_iota(jnp.int32, (LANES, LANES), 0)
    ci = lax.broadcasted_iota(jnp.int32, (LANES, LANES), 1)
    tri = jnp.where(ri <= ci, 1.0, 0.0).astype(BF16)
    sub = lax.broadcasted_iota(jnp.int32, (N_EXPERTS, LANES), 0)

    def count_body(b, base):
        rows = [b * PLAN_UNROLL + u for u in range(PLAN_UNROLL)]
        onehots = [sub == e_ref[pl.ds(r, 1), :] for r in rows]
        locs = [jnp.dot(jnp.where(oh, 1.0, 0.0).astype(BF16), tri, preferred_element_type=F32)
                for oh in onehots]
        for r, onehot, loc in zip(rows, onehots, locs):
            rank_ref[pl.ds(r, 1), :] = jnp.sum(jnp.where(onehot, base + loc - 1.0, 0.0),
                                               axis=0, keepdims=True)
            base = base + jnp.broadcast_to(loc[:, LANES - 1:LANES], (N_EXPERTS, LANES))
        return base

    count = lax.fori_loop(0, nrow // PLAN_UNROLL, count_body, jnp.zeros((N_EXPERTS, LANES), F32))
    ntile = jnp.floor((count + (MOE_TILE - 1.0)) * (1.0 / MOE_TILE))
    offs = []
    acc = jnp.zeros((1, LANES), F32)
    for e in range(N_EXPERTS):
        offs.append(acc)
        acc = acc + ntile[e:e + 1, :]
    tile_off = jnp.concatenate(offs, axis=0)
    tile_end = tile_off + ntile
    lane = lax.broadcasted_iota(jnp.int32, (N_EXPERTS, LANES), 1).astype(F32)
    tile_expert = jnp.sum(jnp.where(tile_end <= lane, 1.0, 0.0), axis=0, keepdims=True)
    tile_expert = jnp.minimum(tile_expert, N_EXPERTS - 1.0)
    meta = jnp.concatenate([tile_expert, acc, jnp.zeros((6, LANES), F32)], axis=0)
    meta_ref[...] = meta.astype(jnp.int32)
    row_off = tile_off * float(MOE_TILE)

    def pos_body(r, c):
        onehot = sub == e_ref[pl.ds(r, 1), :]
        p = jnp.sum(jnp.where(onehot, row_off, 0.0), axis=0, keepdims=True) + rank_ref[pl.ds(r, 1), :]
        pos_ref[pl.ds(r, 1), :] = p.astype(jnp.int32)
        return c

    lax.fori_loop(0, nrow, pos_body, 0)


def _plan(e_idx, *, name):
    n2 = e_idx.shape[0] * e_idx.shape[1]
    assert n2 % (LANES * PLAN_UNROLL) == 0
    e2d = e_idx.reshape(n2 // LANES, LANES)
    pos, meta = pl.pallas_call(
        _plan_kernel,
        out_shape=[jax.ShapeDtypeStruct(e2d.shape, jnp.int32),
                   jax.ShapeDtypeStruct((8, LANES), jnp.int32)],
        scratch_shapes=[pltpu.VMEM(e2d.shape, F32)],
        name=name,
    )(e2d)
    return pos.reshape(n2), meta[0], meta[1, :1]


def _invert_kernel(pos_ref, pair_ref):
    n_rows = pair_ref.shape[0]
    n_pairs = pos_ref.shape[0]

    def fill_body(b, c):
        for u in range(SCALAR_UNROLL):
            pair_ref[b * SCALAR_UNROLL + u] = -1
        return c

    def pair_body(b, c):
        rows = [pos_ref[b * SCALAR_UNROLL + u] for u in range(SCALAR_UNROLL)]
        for u in range(SCALAR_UNROLL):
            pair_ref[rows[u]] = b * SCALAR_UNROLL + u
        return c

    lax.fori_loop(0, n_rows // SCALAR_UNROLL, fill_body, 0)
    lax.fori_loop(0, n_pairs // SCALAR_UNROLL, pair_body, 0)


def _invert(pos, n_rows, *, name):
    return pl.pallas_call(
        _invert_kernel,
        grid_spec=pltpu.PrefetchScalarGridSpec(
            num_scalar_prefetch=1,
            grid=(1,),
            in_specs=[],
            out_specs=pl.BlockSpec(memory_space=pltpu.SMEM),
        ),
        out_shape=jax.ShapeDtypeStruct((n_rows,), jnp.int32),
        name=name,
    )(pos)


def _expert_changed(te_ref, i):
    return jnp.logical_or(i == 0, te_ref[i] != te_ref[jnp.maximum(i - 1, 0)])


def _moe_up_kernel(pair_ref, te_ref, nu_ref, x_hbm, wg_ref, wu_ref, h_ref,
                   wg_bf, wu_bf, x_rows, x_bf, sem, *, n_tok, n_tiles):
    i = pl.program_id(0)
    nu = nu_ref[0]

    def row_copy(tile, r):
        p = pair_ref[tile * MOE_TILE + r]
        tok = jnp.where(p >= n_tok, p - n_tok, jnp.maximum(p, 0))
        src = pl.ds(pl.multiple_of(tok * ROW_SUBLANES, ROW_SUBLANES), ROW_SUBLANES)
        dst = pl.ds(r * ROW_SUBLANES, ROW_SUBLANES)
        return pltpu.make_async_copy(x_hbm.at[src, :], x_rows.at[dst, :], sem)

    def wait_tile():
        pltpu.make_async_copy(x_hbm.at[pl.ds(0, MOE_TILE * ROW_SUBLANES), :], x_rows, sem).wait()

    @pl.when(i == 0)
    def _():
        def body(rb, c):
            for u in range(DMA_UNROLL):
                row_copy(0, rb * DMA_UNROLL + u).start()
            return c
        lax.fori_loop(0, MOE_TILE // DMA_UNROLL, body, 0)

    @pl.when(i <= nu)
    def _():
        wait_tile()

    @pl.when(i < nu)
    def _():
        @pl.when(_expert_changed(te_ref, i))
        def _():
            _cast_rows(wg_ref, wg_bf)
            _cast_rows(wu_ref, wu_bf)

        _load_packed_rows(x_rows, x_bf)
        next_tile = jnp.minimum(i + 1, n_tiles - 1)
        for r in range(MOE_TILE):
            row_copy(next_tile, r).start(priority=r % 2)
        x = x_bf[...]
        a = jnp.dot(x, wg_bf[...], preferred_element_type=F32)
        b = jnp.dot(x, wu_bf[...], preferred_element_type=F32)
        h_ref[...] = (jax.nn.silu(a) * b).astype(BF16)

    @pl.when(jnp.logical_and(i == n_tiles - 1, i < nu))
    def _():
        wait_tile()

    @pl.when(i >= nu)
    def _():
        h_ref[...] = jnp.zeros_like(h_ref)


def _moe_down_kernel(pair_ref, te_ref, nu_ref, h_ref, wd_ref, out_hbm,
                     wd_bf, ya, yb, sems, tsem, *, n_tok, n_tiles):
    i = pl.program_id(0)
    nu = nu_ref[0]
    bufs = (ya, yb)
    trash = 2 * n_tok

    def row_copy(tile, r, buf, sem):
        p = pair_ref[tile * MOE_TILE + r]
        dst = jnp.where(p < 0, trash + r, p)
        return pltpu.make_async_copy(buf.at[pl.ds(r, 1), :], out_hbm.at[pl.ds(dst, 1), :], sem)

    def wait_tile(buf, sem):
        pltpu.make_async_copy(buf, out_hbm.at[pl.ds(0, MOE_TILE), :], sem).wait()

    @pl.when(i == 0)
    def _():
        yb[...] = jnp.zeros_like(yb)
        fill = pltpu.make_async_copy(yb, out_hbm.at[pl.ds(trash, MOE_TILE), :], tsem)
        fill.start()
        fill.wait()

    for parity in range(2):
        cur, prev = bufs[parity], bufs[1 - parity]
        cur_sem, prev_sem = sems.at[parity], sems.at[1 - parity]
        mine = i % 2 == parity

        @pl.when(jnp.logical_and(mine, jnp.logical_and(i >= 1, i - 1 <= nu)))
        def _():
            wait_tile(cur, cur_sem)

        @pl.when(jnp.logical_and(mine, i < nu))
        def _():
            @pl.when(_expert_changed(te_ref, i))
            def _():
                _cast_rows(wd_ref, wd_bf)

            prev_tile = jnp.maximum(i - 1, 0)
            for r in range(MOE_TILE):
                row_copy(prev_tile, r, prev, prev_sem).start(priority=r % 2)
            cur[...] = jnp.dot(h_ref[...], wd_bf[...], preferred_element_type=F32)

        @pl.when(jnp.logical_and(mine, i == nu))
        def _():
            def body(rb, c):
                for u in range(DMA_UNROLL):
                    row_copy(i - 1, rb * DMA_UNROLL + u, prev, prev_sem).start()
                return c
            lax.fori_loop(0, MOE_TILE // DMA_UNROLL, body, 0)

            @pl.when(i == n_tiles)
            def _():
                wait_tile(prev, prev_sem)


def _moe_ffn(x, pair, tile_expert, n_used, w_gate, w_up, w_down, layer):
    n_tok = x.shape[0] // ROW_SUBLANES
    n_tiles = pair.shape[0] // MOE_TILE

    def tile(i, pr, te, nu):
        return (jnp.minimum(i, nu[0] - 1), 0)

    def expert(i, pr, te, nu):
        return (layer, te[jnp.minimum(i, nu[0] - 1)], 0, 0)

    def weight_spec(k, n):
        return pl.BlockSpec((None, None, k, n), expert)

    row_bufs = [pltpu.VMEM((MOE_TILE, D_MODEL), F32), pltpu.VMEM((MOE_TILE, D_MODEL), F32),
                pltpu.SemaphoreType.DMA((2,))]
    h = pl.pallas_call(
        functools.partial(_moe_up_kernel, n_tok=n_tok, n_tiles=n_tiles),
        grid_spec=pltpu.PrefetchScalarGridSpec(
            num_scalar_prefetch=3,
            grid=(n_tiles,),
            in_specs=[pl.BlockSpec(memory_space=pl.ANY),
                      weight_spec(D_MODEL, D_EXPERT),
                      weight_spec(D_MODEL, D_EXPERT)],
            out_specs=pl.BlockSpec((MOE_TILE, D_EXPERT), lambda i, pr, te, nu: (i, 0)),
            scratch_shapes=[pltpu.VMEM((D_MODEL, D_EXPERT), BF16),
                            pltpu.VMEM((D_MODEL, D_EXPERT), BF16),
                            pltpu.VMEM((MOE_TILE * ROW_SUBLANES, LANES), jnp.uint32),
                            pltpu.VMEM((MOE_TILE, D_MODEL), BF16),
                            pltpu.SemaphoreType.DMA(())],
        ),
        out_shape=jax.ShapeDtypeStruct((n_tiles * MOE_TILE, D_EXPERT), BF16),
        compiler_params=_params(("arbitrary",), 56),
        name=f"moe_up_{layer}",
    )(pair, tile_expert, n_used, x, w_gate, w_up)
    return pl.pallas_call(
        functools.partial(_moe_down_kernel, n_tok=n_tok, n_tiles=n_tiles),
        grid_spec=pltpu.PrefetchScalarGridSpec(
            num_scalar_prefetch=3,
            grid=(n_tiles + 1,),
            in_specs=[pl.BlockSpec((MOE_TILE, D_EXPERT), tile),
                      weight_spec(D_EXPERT, D_MODEL)],
            out_specs=pl.BlockSpec(memory_space=pl.ANY),
            scratch_shapes=[pltpu.VMEM((D_EXPERT, D_MODEL), BF16)] + row_bufs
            + [pltpu.SemaphoreType.DMA(())],
        ),
        out_shape=jax.ShapeDtypeStruct((2 * n_tok + MOE_TILE, D_MODEL), F32),
        compiler_params=_params(("arbitrary",), 40),
        name=f"moe_down_{layer}",
    )(pair, tile_expert, n_used, h, w_down)


def _combine(y0_ref, y1_ref, res_ref, gate_ref, g_ref, beta_ref):
    gate = gate_ref[...]
    ffn = gate[:, 0:1] * y0_ref[...] + gate[:, 1:2] * y1_ref[...]
    return _layer_norm(ALPHA * res_ref[...] + ffn, g_ref[...], beta_ref[...])


def _combine_specs(n):
    nt = n // TOK_TILE
    row = lambda i: (i, 0)
    const = lambda i: (0, 0)
    return [pl.BlockSpec((TOK_TILE, D_MODEL), row),
            pl.BlockSpec((TOK_TILE, D_MODEL), lambda i: (i + nt, 0)),
            pl.BlockSpec((TOK_TILE, D_MODEL), row),
            pl.BlockSpec((TOK_TILE, 2), row),
            pl.BlockSpec((1, D_MODEL), const),
            pl.BlockSpec((1, D_MODEL), const)]


def _combine_split_kernel(y0_ref, y1_ref, res_ref, gate_ref, g_ref, beta_ref, prompt_ref, sample_ref):
    x = _combine(y0_ref, y1_ref, res_ref, gate_ref, g_ref, beta_ref)
    is_sample = pl.program_id(0) == pl.num_programs(0) - 1

    @pl.when(jnp.logical_not(is_sample))
    def _():
        prompt_ref[...] = x

    @pl.when(is_sample)
    def _():
        sample_ref[...] = x


def _combine_split(ys, res, gates_col, g, beta, *, name):
    n = res.shape[0]
    tm = TOK_TILE
    nt = n // tm
    return pl.pallas_call(
        _combine_split_kernel,
        grid=(nt,),
        in_specs=_combine_specs(n),
        out_specs=[pl.BlockSpec((tm, D_MODEL), lambda i: (jnp.minimum(i, nt - 2), 0)),
                   pl.BlockSpec((tm, D_MODEL), lambda i: (0, 0))],
        out_shape=[jax.ShapeDtypeStruct((n - tm, D_MODEL), F32),
                   jax.ShapeDtypeStruct((tm, D_MODEL), F32)],
        compiler_params=_params(("arbitrary",), 40),
        name=name,
    )(ys, ys, res, gates_col, g, beta)


def _load_weight(w_hbm, wbf_ref, stage_ref, sems):
    rows = stage_ref.shape[1]
    n_chunks = wbf_ref.shape[0] // rows

    def chunk_copy(c):
        return pltpu.make_async_copy(w_hbm.at[pl.ds(c * rows, rows), :], stage_ref.at[c % 2],
                                     sems.at[c % 2])

    chunk_copy(0).start()
    for c in range(n_chunks):
        if c + 1 < n_chunks:
            chunk_copy(c + 1).start()
        chunk_copy(c).wait()
        wbf_ref[c * rows:(c + 1) * rows, :] = stage_ref[c % 2].astype(BF16)


def _combine_qkv_kernel(y0_ref, y1_ref, res_ref, gate_ref, g_ref, beta_ref, wq_hbm, wkv_hbm,
                        x_ref, q_ref, kv_ref, wq_bf, wkv_bf, stage_q, stage_kv, sems, *, wq_index):
    @pl.when(pl.program_id(0) == 0)
    def _():
        wq = wq_hbm
        for k in wq_index:
            wq = wq.at[k]
        _load_weight(wq, wq_bf, stage_q, sems)
        _load_weight(wkv_hbm, wkv_bf, stage_kv, sems)

    x = _combine(y0_ref, y1_ref, res_ref, gate_ref, g_ref, beta_ref)
    x_ref[...] = x
    x_bf = x.astype(BF16)
    q = jnp.dot(x_bf, wq_bf[...], preferred_element_type=F32) * (HEAD_DIM ** -0.5)
    q_ref[...] = q.astype(BF16)
    kv_ref[...] = jnp.dot(x_bf, wkv_bf[...], preferred_element_type=F32)


def _combine_qkv(ys, res, gates_col, g, beta, w_q, wq_index, w_kv, *, name):
    n = res.shape[0]
    tm = TOK_TILE
    row = lambda i: (i, 0)
    hbm = pl.BlockSpec(memory_space=pl.ANY)
    return pl.pallas_call(
        functools.partial(_combine_qkv_kernel, wq_index=wq_index),
        grid=(n // tm,),
        in_specs=_combine_specs(n) + [hbm, hbm],
        out_specs=[pl.BlockSpec((tm, D_MODEL), row),
                   pl.BlockSpec((tm, D_MODEL), row),
                   pl.BlockSpec((tm, 2 * KV_DIM), row)],
        out_shape=[jax.ShapeDtypeStruct((n, D_MODEL), F32),
                   jax.ShapeDtypeStruct((n, D_MODEL), BF16),
                   jax.ShapeDtypeStruct((n, 2 * KV_DIM), F32)],
        scratch_shapes=[pltpu.VMEM((D_MODEL, D_MODEL), BF16),
                        pltpu.VMEM((D_MODEL, 2 * KV_DIM), BF16),
                        pltpu.VMEM((2, CAST_ROWS, D_MODEL), F32),
                        pltpu.VMEM((2, CAST_ROWS, 2 * KV_DIM), F32),
                        pltpu.SemaphoreType.DMA((2,))],
        compiler_params=_params(("arbitrary",), 52),
        name=name,
    )(ys, ys, res, gates_col, g, beta, w_q, w_kv)


def _moe_block(x_rows, e_idx, w_gate, w_up, w_down, layer):
    n = x_rows.shape[0] // ROW_SUBLANES
    n_tiles = -(-(2 * n + N_EXPERTS * (MOE_TILE - 1)) // MOE_TILE)
    pos, tile_expert, n_used = _plan(e_idx, name=f"moe_plan_{layer}")
    pair = _invert(pos, n_tiles * MOE_TILE, name=f"moe_invert_{layer}")
    return _moe_ffn(x_rows, pair, tile_expert, n_used, w_gate, w_up, w_down, layer)


def _sigmoid(x):
    return 0.5 * jnp.tanh(0.5 * x) + 0.5


def _log_sigmoid(x):
    return -(jnp.maximum(-x, 0.0) + jnp.log1p(jnp.exp(-jnp.abs(x))))


def _lru_gate_block(xc, n, wrg_bf, wig_bf, brg_ref, big_ref, lam_ref):
    cols = slice(n * LRU_BLOCK, (n + 1) * LRU_BLOCK)
    xb = xc.astype(BF16)
    r = _sigmoid(jnp.dot(xb, wrg_bf[n], preferred_element_type=F32) + brg_ref[:, cols])
    i = _sigmoid(jnp.dot(xb, wig_bf[n], preferred_element_type=F32) + big_ref[:, cols])
    log_a = LRU_C * r * _log_sigmoid(lam_ref[:, cols])
    a = jnp.exp(log_a)
    u = xc * i * jnp.sqrt(-jnp.tanh(log_a) * (a * a + 1.0))
    return a, u


def _cast_gate_weights(wrg_ref, wig_ref, wrg_bf, wig_bf):
    for n in range(LRU_BLOCKS):
        wrg_bf[n] = wrg_ref[n].astype(BF16)
        wig_bf[n] = wig_ref[n].astype(BF16)


def _lru_prompt_kernel(xb_ref, yb_ref, cw_ref, cb_ref, wrg_ref, wig_ref, brg_ref, big_ref, lam_ref,
                       m_ref, conv_ref, hlast_ref, xs, tail, a_s, u_s, hs_t, h_s, wrg_bf, wig_bf):
    b = pl.program_id(0)
    j = pl.program_id(1)
    tt = m_ref.shape[0]
    seg_len = tt // SEGS
    taps = CONV_WIDTH - 1
    head = SEGS * taps

    @pl.when(jnp.logical_and(b == 0, j == 0))
    def _():
        _cast_gate_weights(wrg_ref, wig_ref, wrg_bf, wig_bf)

    @pl.when(j == 0)
    def _():
        tail[...] = jnp.zeros_like(tail)
        h_s[...] = jnp.zeros_like(h_s)

    for q in range(seg_len):
        xs[head + SEGS * q:head + SEGS * (q + 1), :] = jnp.concatenate(
            [xb_ref[pl.ds(CHUNKS * q + c, SEGS, stride=CHUNKS * seg_len), :] for c in range(CHUNKS)],
            axis=1)
    sub = lax.broadcasted_iota(jnp.int32, (SEGS, D_MODEL), 0)
    for k in range(taps):
        last = head + SEGS * (seg_len - taps + k)
        joined = jnp.where(sub == SEGS - 1, tail[SEGS * k:SEGS * (k + 1), :], xs[last:last + SEGS, :])
        xs[SEGS * k:SEGS * (k + 1), :] = pltpu.roll(joined, 1, axis=0)
    tail[...] = xs[head + SEGS * (seg_len - taps):head + SEGS * seg_len, :]

    for n in range(LRU_BLOCKS):
        cols = slice(n * LRU_BLOCK, (n + 1) * LRU_BLOCK)
        xc = cb_ref[:, cols] + cw_ref[0:1, cols] * xs[0:tt, cols]
        for k in range(1, CONV_WIDTH):
            xc = xc + cw_ref[k:k + 1, cols] * xs[SEGS * k:SEGS * k + tt, cols]
        a, u = _lru_gate_block(xc, n, wrg_bf, wig_bf, brg_ref, big_ref, lam_ref)
        a_s[:, cols] = a
        u_s[:, cols] = u

    def scan_body(q, carry):
        h, prod = carry
        rows = pl.ds(pl.multiple_of(q * SEGS, SEGS), SEGS)
        a = a_s[rows, :]
        h = a * h + u_s[rows, :]
        prod = a * prod
        u_s[rows, :] = h
        a_s[rows, :] = prod
        return h, prod

    h_end, prod_end = lax.fori_loop(
        0, seg_len, scan_body,
        (jnp.zeros((SEGS, D_MODEL), F32), jnp.ones((SEGS, D_MODEL), F32)))
    state = h_s[...]
    entering = []
    for s in range(SEGS):
        entering.append(state)
        state = h_end[s:s + 1, :] + prod_end[s:s + 1, :] * state
    h_s[...] = state
    enter = jnp.concatenate(entering, axis=0)

    def fix_body(q, carry):
        rows = pl.ds(pl.multiple_of(q * SEGS, SEGS), SEGS)
        h = u_s[rows, :] + a_s[rows, :] * enter
        for c in range(CHUNKS):
            hs_t[pl.ds(CHUNKS * q + c, SEGS, stride=CHUNKS * seg_len), :] = h[:, c * LANES:(c + 1) * LANES]
        return carry

    lax.fori_loop(0, seg_len, fix_body, 0)
    hs = jnp.concatenate([hs_t[pl.ds(c, tt, stride=CHUNKS), :] for c in range(CHUNKS)], axis=1)
    m_ref[...] = (hs * yb_ref[...].astype(F32)).astype(BF16)

    @pl.when(j == pl.num_programs(1) - 1)
    def _():
        for k in range(taps):
            conv_ref[k:k + 1, :] = tail[SEGS * k + SEGS - 1:SEGS * (k + 1), :]
        hlast_ref[...] = state


def _lru_prompt(xb, yb, batch, seq, cw, cb, wrg, wig, brg, big, lam, *, tt=256):
    nj = seq // tt
    row = lambda b, j: (b * nj + j, 0)
    const2 = lambda b, j: (0, 0)
    const3 = lambda b, j: (0, 0, 0)
    return pl.pallas_call(
        _lru_prompt_kernel,
        grid=(batch, nj),
        in_specs=[
            pl.BlockSpec((tt * CHUNKS, LANES), row),
            pl.BlockSpec((tt, D_MODEL), row),
            pl.BlockSpec((CONV_WIDTH, D_MODEL), const2),
            pl.BlockSpec((1, D_MODEL), const2),
            pl.BlockSpec((LRU_BLOCKS, LRU_BLOCK, LRU_BLOCK), const3),
            pl.BlockSpec((LRU_BLOCKS, LRU_BLOCK, LRU_BLOCK), const3),
            pl.BlockSpec((1, D_MODEL), const2),
            pl.BlockSpec((1, D_MODEL), const2),
            pl.BlockSpec((1, D_MODEL), const2),
        ],
        out_specs=[
            pl.BlockSpec((tt, D_MODEL), row),
            pl.BlockSpec((None, CONV_WIDTH - 1, D_MODEL), lambda b, j: (b, 0, 0)),
            pl.BlockSpec((None, 1, D_MODEL), lambda b, j: (b, 0, 0)),
        ],
        out_shape=[
            jax.ShapeDtypeStruct((batch * seq, D_MODEL), BF16),
            jax.ShapeDtypeStruct((batch, CONV_WIDTH - 1, D_MODEL), F32),
            jax.ShapeDtypeStruct((batch, 1, D_MODEL), F32),
        ],
        scratch_shapes=[
            pltpu.VMEM((tt + SEGS * (CONV_WIDTH - 1), D_MODEL), F32),
            pltpu.VMEM((SEGS * (CONV_WIDTH - 1), D_MODEL), F32),
            pltpu.VMEM((tt, D_MODEL), F32),
            pltpu.VMEM((tt, D_MODEL), F32),
            pltpu.VMEM((tt * CHUNKS, LANES), F32),
            pltpu.VMEM((1, D_MODEL), F32),
            pltpu.VMEM((LRU_BLOCKS, LRU_BLOCK, LRU_BLOCK), BF16),
            pltpu.VMEM((LRU_BLOCKS, LRU_BLOCK, LRU_BLOCK), BF16),
        ],
        compiler_params=_params(("arbitrary", "arbitrary"), 40),
        name="lru_prompt",
    )(xb, yb, cw, cb, wrg, wig, brg, big, lam)


def _lru_sample_kernel(xb_ref, yb_ref, cs_ref, h0_ref, cw_ref, cb_ref, wrg_ref, wig_ref,
                       brg_ref, big_ref, lam_ref, m_ref, conv_ref, hlast_ref, wrg_bf, wig_bf, *, steps):
    batch = h0_ref.shape[0]
    _cast_gate_weights(wrg_ref, wig_ref, wrg_bf, wig_bf)
    m_ref[steps * batch:, :] = jnp.zeros((m_ref.shape[0] - steps * batch, D_MODEL), BF16)

    def slab(t, cols):
        if t < CONV_WIDTH - 1:
            return cs_ref[t, :, cols]
        t -= CONV_WIDTH - 1
        first, stop, _ = cols.indices(D_MODEL)
        chunks = range(first // LANES, stop // LANES)
        return jnp.concatenate(
            [xb_ref[pl.ds(t * batch * CHUNKS + c, batch, stride=CHUNKS), :] for c in chunks], axis=1)

    for n in range(LRU_BLOCKS):
        cols = slice(n * LRU_BLOCK, (n + 1) * LRU_BLOCK)
        h = h0_ref[:, cols]
        for t in range(steps):
            xc = cb_ref[:, cols] + cw_ref[0:1, cols] * slab(t, cols)
            for k in range(1, CONV_WIDTH):
                xc = xc + cw_ref[k:k + 1, cols] * slab(t + k, cols)
            a, u = _lru_gate_block(xc, n, wrg_bf, wig_bf, brg_ref, big_ref, lam_ref)
            h = a * h + u
            rows = slice(t * batch, (t + 1) * batch)
            m_ref[rows, cols] = (h * yb_ref[rows, cols].astype(F32)).astype(BF16)
        hlast_ref[:, cols] = h
    for k in range(CONV_WIDTH - 1):
        conv_ref[k] = slab(steps + k, slice(None))


def _lru_sample(xb, yb, tile, steps, conv_state, h0, cw, cb, wrg, wig, brg, big, lam):
    batch = h0.shape[0]
    tok = pl.BlockSpec((TOK_TILE, D_MODEL), lambda i: (tile, 0))
    tok_chunks = pl.BlockSpec((TOK_TILE * CHUNKS, LANES), lambda i: (tile, 0))
    full = lambda a: pl.BlockSpec(a.shape, lambda i: (0,) * a.ndim)
    small = (conv_state, h0, cw, cb, wrg, wig, brg, big, lam)
    return pl.pallas_call(
        functools.partial(_lru_sample_kernel, steps=steps),
        grid=(1,),
        in_specs=[tok_chunks, tok] + [full(a) for a in small],
        out_specs=[
            pl.BlockSpec((TOK_TILE, D_MODEL), lambda i: (0, 0)),
            pl.BlockSpec((CONV_WIDTH - 1, batch, D_MODEL), lambda i: (0, 0, 0)),
            pl.BlockSpec((batch, D_MODEL), lambda i: (0, 0)),
        ],
        out_shape=[
            jax.ShapeDtypeStruct((TOK_TILE, D_MODEL), BF16),
            jax.ShapeDtypeStruct((CONV_WIDTH - 1, batch, D_MODEL), F32),
            jax.ShapeDtypeStruct((batch, D_MODEL), F32),
        ],
        scratch_shapes=[
            pltpu.VMEM((LRU_BLOCKS, LRU_BLOCK, LRU_BLOCK), BF16),
            pltpu.VMEM((LRU_BLOCKS, LRU_BLOCK, LRU_BLOCK), BF16),
        ],
        compiler_params=_params(("arbitrary",), 32),
        name="lru_sample",
    )(xb, yb, *small)


def _rel_bucket(dist):
    n = jnp.maximum(dist, 0)
    max_exact = N_BUCKETS // 2
    nf = jnp.maximum(n, 1).astype(F32)
    large = max_exact + (jnp.log(nf / max_exact) / math.log(MAX_DISTANCE / max_exact)
                         * (N_BUCKETS - max_exact)).astype(jnp.int32)
    large = jnp.minimum(large, N_BUCKETS - 1)
    return jnp.where(n < max_exact, n, large)


def _masked_buckets(dist):
    valid = (dist >= 0) & (dist < WINDOW)
    return jnp.where(valid, _rel_bucket(dist), -1).astype(jnp.int32)


def _build_bias(bucket, tab_ref, head):
    def body(bi, acc):
        return jnp.where(bucket == bi, tab_ref[bi * N_HEADS + head], acc)
    return lax.fori_loop(0, N_BUCKETS, body, jnp.full(bucket.shape, NEG_INF, F32))


def _softmax_pv(s, sink, v):
    m = jnp.maximum(jnp.max(s, axis=-1, keepdims=True), sink)
    p = jnp.exp(s - m)
    den = jnp.sum(p, axis=-1, keepdims=True) + jnp.exp(sink - m)
    return jnp.dot(p.astype(BF16), v, preferred_element_type=F32) / den


def _attn_prompt_kernel(q_ref, kvp_ref, kvc_ref, bucket_ref, tab_ref, sink_ref, o_ref, bias_s):
    b = pl.program_id(0)
    n = pl.program_id(1)

    @pl.when(jnp.logical_and(b == 0, n == 0))
    def _():
        bucket = bucket_ref[...]

        col = lax.broadcasted_iota(jnp.int32, (WINDOW, 2 * WINDOW), 1)

        def head_body(h, c):
            bias = _build_bias(bucket, tab_ref, h)
            sink = sink_ref[h]
            g = h // GROUP
            r0 = pl.multiple_of((h % GROUP) * WINDOW, WINDOW)
            bias_s[0, g, pl.ds(r0, WINDOW), :] = jnp.where(col == 0, sink, bias)
            bias_s[1, g, pl.ds(r0, WINDOW), :] = jnp.where(
                col == 0, sink, jnp.where(col < WINDOW, NEG_INF, bias))
            return c

        lax.fori_loop(0, N_HEADS, head_body, 0)

    first = (n == 0).astype(jnp.int32)
    row = lax.broadcasted_iota(jnp.int32, kvp_ref.shape, 0)
    kv_prev = jnp.where(row == 0, 0.0, kvp_ref[...])
    kv = jnp.concatenate([kv_prev, kvc_ref[...]], axis=0).astype(BF16)
    ones = jnp.ones((2 * WINDOW, 2 * HEAD_DIM), BF16)
    lane = lax.broadcasted_iota(jnp.int32, (WINDOW, 2 * HEAD_DIM), 1)
    def scores(idx):
        g, pair = divmod(idx, GROUP // 2)
        h0 = g * GROUP + 2 * pair
        kg = kv[:, g * HEAD_DIM:(g + 1) * HEAD_DIM]
        qp = jnp.concatenate([q_ref[:, h * HEAD_DIM:(h + 1) * HEAD_DIM] for h in (h0, h0 + 1)], axis=0)
        s = lax.dot_general(qp, kg, (((1,), (1,)), ((), ())), preferred_element_type=F32)
        return s + bias_s[first, g, 2 * pair * WINDOW:(2 * pair + 2) * WINDOW, :]

    def finish(idx, o_ext):
        h0 = 2 * idx
        o = o_ext[:, :2 * HEAD_DIM] * (1.0 / o_ext[:, 2 * HEAD_DIM:])
        o_ref[:, h0 * HEAD_DIM:(h0 + 2) * HEAD_DIM] = jnp.where(
            lane < HEAD_DIM, o[:WINDOW], o[WINDOW:]).astype(BF16)

    n_pairs = N_HEADS // 2
    ahead = [scores(k) for k in range(SCORE_AHEAD)]
    pending = None
    for idx in range(n_pairs):
        if idx + SCORE_AHEAD < n_pairs:
            ahead.append(scores(idx + SCORE_AHEAD))
        s = ahead.pop(0)
        g = idx // (GROUP // 2)
        vg = kv[:, KV_DIM + g * HEAD_DIM:KV_DIM + (g + 1) * HEAD_DIM]
        v_ext = jnp.concatenate([vg, vg, ones], axis=1)
        p = jnp.exp(s - jnp.max(s, axis=-1, keepdims=True)).astype(BF16)
        o_ext = jnp.dot(p, v_ext, preferred_element_type=F32)
        if pending is not None:
            finish(*pending)
        pending = (idx, o_ext)
    finish(*pending)


def _attn_prompt(q, kv, batch, seq, bucket, tab, sinks):
    nb = seq // WINDOW
    smem = pl.BlockSpec(memory_space=pltpu.SMEM)
    return pl.pallas_call(
        _attn_prompt_kernel,
        grid=(batch, nb),
        in_specs=[
            pl.BlockSpec((WINDOW, D_MODEL), lambda b, n: (b * nb + n, 0)),
            pl.BlockSpec((WINDOW, 2 * KV_DIM), lambda b, n: (jnp.maximum(b * nb + n - 1, 0), 0)),
            pl.BlockSpec((WINDOW, 2 * KV_DIM), lambda b, n: (b * nb + n, 0)),
            pl.BlockSpec((WINDOW, 2 * WINDOW), lambda b, n: (0, 0)),
            smem, smem,
        ],
        out_specs=pl.BlockSpec((WINDOW, D_MODEL), lambda b, n: (b * nb + n, 0)),
        out_shape=jax.ShapeDtypeStruct((batch * seq, D_MODEL), BF16),
        scratch_shapes=[pltpu.VMEM((2, N_KV_HEADS, GROUP * WINDOW, 2 * WINDOW), F32)],
        compiler_params=_params(("arbitrary", "arbitrary"), 32),
        name="attn_prompt",
    )(q, kv, kv, bucket, tab, sinks)


def _attn_sample_kernel(q_ref, k_ref, v_ref, bucket_ref, tab_ref, sink_ref, o_ref, bias_s):
    steps = q_ref.shape[0]

    @pl.when(pl.program_id(0) == 0)
    def _():
        bucket = bucket_ref[...]

        def head_body(h, c):
            bias_s[h] = _build_bias(bucket, tab_ref, h)
            return c

        lax.fori_loop(0, N_HEADS, head_body, 0)

    rows = lax.broadcasted_iota(jnp.int32, (GROUP * steps, 1), 0)
    k = k_ref[...].astype(BF16)
    v = v_ref[...].astype(BF16)
    groups = range(N_KV_HEADS)

    def heads_of(g):
        return range(g * GROUP, (g + 1) * GROUP)

    scores, sinks = [], []
    for g in groups:
        qg = jnp.concatenate([q_ref[:, h * HEAD_DIM:(h + 1) * HEAD_DIM] for h in heads_of(g)], axis=0)
        bias = jnp.concatenate([bias_s[h] for h in heads_of(g)], axis=0)
        sink = jnp.full((GROUP * steps, 1), sink_ref[g * GROUP], F32)
        for hh in range(1, GROUP):
            sink = jnp.where(rows >= hh * steps, sink_ref[g * GROUP + hh], sink)
        kg = k[:, g * HEAD_DIM:(g + 1) * HEAD_DIM]
        scores.append(lax.dot_general(qg, kg, (((1,), (1,)), ((), ())),
                                      preferred_element_type=F32) + bias)
        sinks.append(sink)
    outs = [_softmax_pv(scores[g], sinks[g], v[:, g * HEAD_DIM:(g + 1) * HEAD_DIM]) for g in groups]
    for g in groups:
        for hh, h in enumerate(heads_of(g)):
            o_ref[:, h * HEAD_DIM:(h + 1) * HEAD_DIM] = outs[g][hh * steps:(hh + 1) * steps].astype(BF16)


def _attn_sample(q, k_all, v_all, bucket, tab, sinks):
    batch, steps, _ = q.shape
    lk = k_all.shape[1]
    smem = pl.BlockSpec(memory_space=pltpu.SMEM)
    return pl.pallas_call(
        _attn_sample_kernel,
        grid=(batch,),
        in_specs=[
            pl.BlockSpec((None, steps, D_MODEL), lambda b: (b, 0, 0)),
            pl.BlockSpec((None, lk, KV_DIM), lambda b: (b, 0, 0)),
            pl.BlockSpec((None, lk, KV_DIM), lambda b: (b, 0, 0)),
            pl.BlockSpec((steps, lk), lambda b: (0, 0)),
            smem, smem,
        ],
        out_specs=pl.BlockSpec((None, steps, D_MODEL), lambda b: (b, 0, 0)),
        out_shape=jax.ShapeDtypeStruct((batch, steps, D_MODEL), BF16),
        scratch_shapes=[pltpu.VMEM((N_HEADS, steps, lk), F32)],
        compiler_params=_params(("arbitrary",), 32),
        name="attn_sample",
    )(q, k_all, v_all, bucket, tab, sinks)


def kernel(x_prompt, x_sample, state_conv, state_rnn, cache_k_win, cache_v_win, ln_g, ln_b, lru_w_x, lru_b_x, lru_w_y, lru_b_y, lru_conv_w, lru_conv_b, lru_w_rg, lru_b_rg, lru_w_ig, lru_b_ig, lru_lam, lru_w_out, lru_b_out, attn_w_kv, attn_w_q, attn_w_o, attn_sinks, rel_bias, moe_w_router, moe_b_router, moe_w_gate, moe_w_up, moe_w_down):
    bp, seq, _ = x_prompt.shape
    bs, steps, _ = x_sample.shape
    n_p = bp * seq
    n_s = bs * steps

    assert n_p % TOK_TILE == 0 and n_s <= TOK_TILE
    sample_tile = n_p // TOK_TILE

    def pad_tile(rows):
        return jnp.pad(rows, ((0, TOK_TILE - n_s), (0, 0)))

    x0 = (x_prompt.reshape(n_p, D_MODEL),
          pad_tile(x_sample.transpose(1, 0, 2).reshape(n_s, D_MODEL)))
    wr_t = moe_w_router.T
    br = moe_b_router.reshape(N_EXPERTS, 1)
    vec = lambda a: a.reshape(1, -1)

    xb = _linear(x0, lru_w_x, (0,), vec(lru_b_x[0]), F32, name="lru_in_x", chunk_rows=True)
    yb = _linear(x0, lru_w_y, (0,), vec(lru_b_y[0]), BF16, act="gelu", name="lru_in_y")
    lru_args = (lru_conv_w[0], vec(lru_conv_b[0]), lru_w_rg[0], lru_w_ig[0],
                vec(lru_b_rg[0]), vec(lru_b_ig[0]), vec(lru_lam[0]))
    m_p, conv_p, rnn_p = _lru_prompt(xb, yb, bp, seq, *lru_args)
    m_s, conv_s, rnn_s = _lru_sample(xb, yb, sample_tile, steps,
                                     state_conv[0].transpose(1, 0, 2), state_rnn[0], *lru_args)
    x1, x1_rows, e_idx, gates = _proj_ln((m_p, m_s), lru_w_out, (0,), vec(lru_b_out[0]), x0,
                                vec(ln_g[0, 0]), vec(ln_b[0, 0]), wr_t, br, name="lru_out_ln")
    ys = _moe_block(x1_rows, e_idx, moe_w_gate, moe_w_up, moe_w_down, 0)

    x2, q, kv = _combine_qkv(ys, x1, gates.T, vec(ln_g[0, 1]), vec(ln_b[0, 1]),
                             attn_w_q, (0,), attn_w_kv, name="moe_combine_qkv")
    tab = rel_bias.reshape(-1)
    sinks = attn_sinks[0]
    qi = jnp.arange(WINDOW)[:, None]
    kj = jnp.arange(2 * WINDOW)[None, :]
    o_p = _attn_prompt(q, kv, bp, seq, _masked_buckets(qi + WINDOW - kj), tab, sinks)
    kv_s = kv[n_p:n_p + n_s].reshape(steps, bs, 2, KV_DIM).transpose(2, 1, 0, 3)
    k_all = jnp.concatenate([cache_k_win.reshape(bs, WINDOW, KV_DIM), kv_s[0]], axis=1)
    v_all = jnp.concatenate([cache_v_win.reshape(bs, WINDOW, KV_DIM), kv_s[1]], axis=1)
    dist_s = jnp.arange(steps)[:, None] + WINDOW - jnp.arange(WINDOW + steps)[None, :]
    q_s = q[n_p:n_p + n_s].reshape(steps, bs, D_MODEL).transpose(1, 0, 2)
    o_s = _attn_sample(q_s, k_all, v_all, _masked_buckets(dist_s), tab, sinks)
    o_s = pad_tile(o_s.transpose(1, 0, 2).reshape(n_s, D_MODEL))
    x3, x3_rows, e_idx, gates = _proj_ln((o_p, o_s), attn_w_o, (0,), jnp.zeros((1, D_MODEL), F32), x2,
                                vec(ln_g[1, 0]), vec(ln_b[1, 0]), wr_t, br, name="attn_out_ln")
    ys = _moe_block(x3_rows, e_idx, moe_w_gate, moe_w_up, moe_w_down, 1)
    y_p, y_s = _combine_split(ys, x3, gates.T, vec(ln_g[1, 1]), vec(ln_b[1, 1]), name="moe_combine_1")

    y_prompt = y_p.reshape(bp, seq, D_MODEL)
    y_sample = y_s[:n_s].reshape(steps, bs, D_MODEL).transpose(1, 0, 2)
    kv_p = jnp.stack([kv[(b + 1) * seq - WINDOW:(b + 1) * seq] for b in range(bp)])
    kv_p = kv_p.reshape(bp, WINDOW, 2, N_KV_HEADS, HEAD_DIM)
    k_win_s = k_all[:, steps:].reshape(bs, WINDOW, N_KV_HEADS, HEAD_DIM)
    v_win_s = v_all[:, steps:].reshape(bs, WINDOW, N_KV_HEADS, HEAD_DIM)
    return (y_prompt, y_sample,
            conv_p[None], rnn_p.reshape(1, bp, D_MODEL),
            kv_p[:, :, 0], kv_p[:, :, 1],
            conv_s.transpose(1, 0, 2)[None], rnn_s[None],
            k_win_s, v_win_s)
```

```python
import functools
import math

import jax
import jax.numpy as jnp
from jax import lax
from jax.experimental import pallas as pl
from jax.experimental.pallas import tpu as pltpu

D_MODEL = 2048
DEPTH = 2
LRU_BLOCKS = 8
LRU_BLOCK = D_MODEL // LRU_BLOCKS
CONV_WIDTH = 4
LRU_C = 8.0
N_HEADS = 32
HEAD_DIM = 64
N_KV_HEADS = 8
GROUP = N_HEADS // N_KV_HEADS
KV_DIM = N_KV_HEADS * HEAD_DIM
WINDOW = 128
N_BUCKETS = 32
MAX_DISTANCE = 128
N_EXPERTS = 16
N_GROUPS = 4
EXPERTS_PER_GROUP = N_EXPERTS // N_GROUPS
D_EXPERT = 1024
ALPHA = (2 * DEPTH) ** 0.25
LN_EPS = 1e-5

LANES = 128
SEGS = 8
CHUNKS = D_MODEL // LANES
ROW_SUBLANES = D_MODEL // (2 * LANES)
MOE_TILE = 256
TOK_TILE = 256
DMA_UNROLL = 8
SCALAR_UNROLL = 32
PLAN_UNROLL = 4
SCORE_AHEAD = 1
W_CHUNKS = 4
CAST_ROWS = 256
BF16 = jnp.bfloat16
F32 = jnp.float32
NEG_INF = float("-inf")


def _params(sem, vmem_mb):
    return pltpu.CompilerParams(dimension_semantics=sem, vmem_limit_bytes=vmem_mb * 1024 * 1024)


def _cast_rows(src_ref, dst_ref):
    n = src_ref.shape[0] // CAST_ROWS

    def body(i, c):
        r = pl.multiple_of(i * CAST_ROWS, CAST_ROWS)
        dst_ref[pl.ds(r, CAST_ROWS), :] = src_ref[pl.ds(r, CAST_ROWS), :].astype(BF16)
        return c

    lax.fori_loop(0, n, body, 0)


def _layer_norm(z, g, b):
    mu = jnp.mean(z, axis=-1, keepdims=True)
    zc = z - mu
    var = jnp.mean(zc * zc, axis=-1, keepdims=True)
    return zc * lax.rsqrt(var + LN_EPS) * g + b


def _store_packed_rows(x_bf, rows_ref):
    n = x_bf.shape[0]
    bits = pltpu.bitcast(x_bf.astype(F32), jnp.uint32)
    packed = bits[:, D_MODEL // 2:] | (bits[:, :D_MODEL // 2] >> 16)
    for c in range(ROW_SUBLANES):
        rows_ref[pl.ds(c, n, stride=ROW_SUBLANES), :] = packed[:, c * LANES:(c + 1) * LANES]


def _load_packed_rows(rows_ref, x_bf_ref):
    n = x_bf_ref.shape[0]
    for c in range(ROW_SUBLANES):
        words = rows_ref[pl.ds(c, n, stride=ROW_SUBLANES), :]
        low = pltpu.bitcast(words << 16, F32).astype(BF16)
        high = pltpu.bitcast(words & jnp.uint32(0xFFFF0000), F32).astype(BF16)
        x_bf_ref[:, c * LANES:(c + 1) * LANES] = low
        x_bf_ref[:, D_MODEL // 2 + c * LANES:D_MODEL // 2 + (c + 1) * LANES] = high


def _tok_operands(x, tile_of=lambda i: i):
    if isinstance(x, tuple):
        xp, xs = x
        d = xp.shape[1]
        last_p = xp.shape[0] // TOK_TILE - 1
        specs = [pl.BlockSpec((TOK_TILE, d), lambda i, *_: (jnp.minimum(tile_of(i), last_p), 0)),
                 pl.BlockSpec((TOK_TILE, d), lambda i, *_: (0, 0))]
        return [xp, xs], specs, last_p + 2
    return ([x], [pl.BlockSpec((TOK_TILE, x.shape[1]), lambda i, *_: (tile_of(i), 0))],
            x.shape[0] // TOK_TILE)


def _tok_load(refs, is_sample=None):
    if len(refs) == 1:
        return refs[0][...]
    if is_sample is None:
        is_sample = pl.program_id(0) == pl.num_programs(0) - 1
    return jnp.where(is_sample, refs[1][...], refs[0][...])


def _linear_kernel(*refs, n_x, act, scale, chunk_rows):
    x_refs, (w_ref, b_ref, o_ref, wbf_ref) = refs[:n_x], refs[n_x:]

    @pl.when(pl.program_id(0) == 0)
    def _():
        _cast_rows(w_ref, wbf_ref)

    y = jnp.dot(_tok_load(x_refs).astype(BF16), wbf_ref[...], preferred_element_type=F32)
    y = y + b_ref[...]
    if act == "gelu":
        y = jax.nn.gelu(y)
    if scale != 1.0:
        y = y * scale
    if chunk_rows:
        n_chunks = y.shape[1] // LANES
        for c in range(n_chunks):
            o_ref[pl.ds(c, y.shape[0], stride=n_chunks), :] = y[:, c * LANES:(c + 1) * LANES]
    else:
        o_ref[...] = y.astype(o_ref.dtype)


def _linear(x, w, w_index, b, out_dtype, *, name, act=None, scale=1.0, chunk_rows=False):
    arrays, specs, nt = _tok_operands(x)
    k, nout = w.shape[-2:]
    w_block = (None,) * len(w_index) + (k, nout)
    if chunk_rows:
        out_block, out_rows, out_cols = (TOK_TILE * (nout // LANES), LANES), nt * TOK_TILE * (nout // LANES), LANES
    else:
        out_block, out_rows, out_cols = (TOK_TILE, nout), nt * TOK_TILE, nout
    return pl.pallas_call(
        functools.partial(_linear_kernel, n_x=len(arrays), act=act, scale=scale,
                          chunk_rows=chunk_rows),
        grid=(nt,),
        in_specs=specs + [
            pl.BlockSpec(w_block, lambda i: w_index + (0, 0), pipeline_mode=pl.Buffered(1)),
            pl.BlockSpec((1, nout), lambda i: (0, 0)),
        ],
        out_specs=pl.BlockSpec(out_block, lambda i: (i, 0)),
        out_shape=jax.ShapeDtypeStruct((out_rows, out_cols), out_dtype),
        scratch_shapes=[pltpu.VMEM((k, nout), BF16)],
        compiler_params=_params(("arbitrary",), 48),
        name=name,
    )(*arrays, w, b)


def _route(logits_t, b_router):
    aff = jax.nn.sigmoid(logits_t)
    sel = aff + b_router
    srow = [sel[e:e + 1, :] for e in range(N_EXPERTS)]
    arow = [aff[e:e + 1, :] for e in range(N_EXPERTS)]

    def top2_sum(v):
        pairs = [v[i] + v[j] for i in range(4) for j in range(i + 1, 4)]
        return functools.reduce(jnp.maximum, pairs)

    scores = [top2_sum(srow[4 * g:4 * g + 4]) for g in range(N_GROUPS)]
    best = scores[0]
    gi = jnp.zeros_like(best, dtype=jnp.int32)
    for g in range(1, N_GROUPS):
        upd = scores[g] > best
        best = jnp.where(upd, scores[g], best)
        gi = jnp.where(upd, g, gi)

    def pick_group(rows, j):
        out = rows[j]
        for g in range(1, N_GROUPS):
            out = jnp.where(gi == g, rows[4 * g + j], out)
        return out

    v = [pick_group(srow, j) for j in range(EXPERTS_PER_GROUP)]
    a = [pick_group(arow, j) for j in range(EXPERTS_PER_GROUP)]

    m1, i1 = v[0], jnp.zeros_like(gi)
    for j in range(1, EXPERTS_PER_GROUP):
        upd = v[j] > m1
        m1 = jnp.where(upd, v[j], m1)
        i1 = jnp.where(upd, j, i1)
    m2 = jnp.full_like(m1, NEG_INF)
    i2 = jnp.zeros_like(gi)
    for j in range(EXPERTS_PER_GROUP):
        cand = jnp.where(i1 == j, NEG_INF, v[j])
        upd = cand > m2
        m2 = jnp.where(upd, cand, m2)
        i2 = jnp.where(upd, j, i2)

    def pick_idx(rows, idx):
        out = rows[0]
        for j in range(1, EXPERTS_PER_GROUP):
            out = jnp.where(idx == j, rows[j], out)
        return out

    a1 = pick_idx(a, i1)
    a2 = pick_idx(a, i2)
    tot = a1 + a2
    e_idx = jnp.concatenate([gi * EXPERTS_PER_GROUP + i1, gi * EXPERTS_PER_GROUP + i2], axis=0)
    gates = jnp.concatenate([a1 / tot, a2 / tot], axis=0)
    return e_idx, gates


def _proj_ln_kernel(*refs, n_m, n_res):
    m_refs = refs[:n_m]
    w_ref, b_ref = refs[n_m:n_m + 2]
    res_refs = refs[n_m + 2:n_m + 2 + n_res]
    (g_ref, beta_ref, wr_ref, br_ref, x_ref, xrow_ref, e_ref, gate_ref,
     wbf_ref, ya, yb) = refs[n_m + 2 + n_res:]
    i = pl.program_id(0)
    n_tiles = pl.num_programs(0) - 1

    @pl.when(i == 0)
    def _():
        _cast_rows(w_ref, wbf_ref)
        yb[...] = jnp.zeros_like(yb)

    for parity, (cur, prev) in enumerate(((ya, yb), (yb, ya))):
        @pl.when(i % 2 == parity)
        def _():
            cur[...] = jnp.dot(_tok_load(m_refs, i >= n_tiles - 1), wbf_ref[...],
                               preferred_element_type=F32)
            y = prev[...] + b_ref[...]
            x = _layer_norm(ALPHA * _tok_load(res_refs, i == n_tiles) + y, g_ref[...], beta_ref[...])
            x_ref[...] = x
            x_bf = x.astype(BF16)
            _store_packed_rows(x_bf, xrow_ref)
            logits_t = lax.dot_general(wr_ref[...].astype(BF16), x_bf,
                                       (((1,), (1,)), ((), ())), preferred_element_type=F32)
            e_idx, gates = _route(logits_t, br_ref[...])
            e_ref[...] = e_idx
            gate_ref[...] = gates


def _proj_ln(m, w, w_index, b, res, g, beta, wr_t, br, *, name):
    nt = _tok_operands(m)[2]
    m_arrays, m_specs, _ = _tok_operands(m, lambda i: jnp.minimum(i, nt - 1))
    res_arrays, res_specs, _ = _tok_operands(res, lambda i: jnp.maximum(i - 1, 0))
    k = w.shape[-2]
    tm = TOK_TILE
    n = nt * tm
    row = lambda i: (jnp.maximum(i - 1, 0), 0)
    const = lambda i: (0, 0)
    x, x_rows, e_idx, gates = pl.pallas_call(
        functools.partial(_proj_ln_kernel, n_m=len(m_arrays), n_res=len(res_arrays)),
        grid=(nt + 1,),
        in_specs=m_specs + [
            pl.BlockSpec((None,) * len(w_index) + (k, D_MODEL), lambda i: w_index + (0, 0),
                         pipeline_mode=pl.Buffered(1)),
            pl.BlockSpec((1, D_MODEL), const),
        ] + res_specs + [
            pl.BlockSpec((1, D_MODEL), const),
            pl.BlockSpec((1, D_MODEL), const),
            pl.BlockSpec((N_EXPERTS, D_MODEL), const),
            pl.BlockSpec((N_EXPERTS, 1), const),
        ],
        out_specs=[
            pl.BlockSpec((tm, D_MODEL), row),
            pl.BlockSpec((tm * ROW_SUBLANES, LANES), row),
            pl.BlockSpec((None, 2, tm), lambda i: (jnp.maximum(i - 1, 0), 0, 0)),
            pl.BlockSpec((None, 2, tm), lambda i: (jnp.maximum(i - 1, 0), 0, 0)),
        ],
        out_shape=[
            jax.ShapeDtypeStruct((n, D_MODEL), F32),
            jax.ShapeDtypeStruct((n * ROW_SUBLANES, LANES), jnp.uint32),
            jax.ShapeDtypeStruct((nt, 2, tm), jnp.int32),
            jax.ShapeDtypeStruct((nt, 2, tm), F32),
        ],
        scratch_shapes=[pltpu.VMEM((k, D_MODEL), BF16),
                        pltpu.VMEM((tm, D_MODEL), F32), pltpu.VMEM((tm, D_MODEL), F32)],
        compiler_params=_params(("arbitrary",), 52),
        name=name,
    )(*m_arrays, w, b, *res_arrays, g, beta, wr_t, br)
    e_idx = e_idx.transpose(1, 0, 2).reshape(2, n)
    gates = gates.transpose(1, 0, 2).reshape(2, n)
    return x, x_rows, e_idx, gates


def _plan_kernel(e_ref, pos_ref, meta_ref, rank_ref):
    nrow = e_ref.shape[0]
    ri = lax.broadcasted_iota(jnp.int32, (LANES, LANES), 0)
    ci = lax.broadcasted_iota(jnp.int32, (LANES, LANES), 1)
    tri = jnp.where(ri <= ci, 1.0, 0.0).astype(BF16)
    sub = lax.broadcasted_iota(jnp.int32, (N_EXPERTS, LANES), 0)

    def count_body(b, base):
        rows = [b * PLAN_UNROLL + u for u in range(PLAN_UNROLL)]
        onehots = [sub == e_ref[pl.ds(r, 1), :] for r in rows]
        locs = [jnp.dot(jnp.where(oh, 1.0, 0.0).astype(BF16), tri, preferred_element_type=F32)
                for oh in onehots]
        for r, onehot, loc in zip(rows, onehots, locs):
            rank_ref[pl.ds(r, 1), :] = jnp.sum(jnp.where(onehot, base + loc - 1.0, 0.0),
                                               axis=0, keepdims=True)
            base = base + jnp.broadcast_to(loc[:, LANES - 1:LANES], (N_EXPERTS, LANES))
        return base

    count = lax.fori_loop(0, nrow // PLAN_UNROLL, count_body, jnp.zeros((N_EXPERTS, LANES), F32))
    ntile = jnp.floor((count + (MOE_TILE - 1.0)) * (1.0 / MOE_TILE))
    offs = []
    acc = jnp.zeros((1, LANES), F32)
    for e in range(N_EXPERTS):
        offs.append(acc)
        acc = acc + ntile[e:e + 1, :]
    tile_off = jnp.concatenate(offs, axis=0)
    tile_end = tile_off + ntile
    lane = lax.broadcasted_iota(jnp.int32, (N_EXPERTS, LANES), 1).astype(F32)
    tile_expert = jnp.sum(jnp.where(tile_end <= lane, 1.0, 0.0), axis=0, keepdims=True)
    tile_expert = jnp.minimum(tile_expert, N_EXPERTS - 1.0)
    own = jnp.logical_and(tile_off <= lane, lane < tile_end)
    run_end = jnp.sum(jnp.where(own, tile_end, 0.0), axis=0, keepdims=True)
    next_expert = jnp.sum(jnp.where(tile_end <= run_end, 1.0, 0.0), axis=0, keepdims=True)
    has_next = jnp.logical_and(lane[0:1, :] < acc, run_end < acc)
    next_expert = jnp.where(has_next, next_expert, -1.0)
    meta = jnp.concatenate([tile_expert, acc, next_expert, jnp.zeros((5, LANES), F32)], axis=0)
    meta_ref[...] = meta.astype(jnp.int32)
    row_off = tile_off * float(MOE_TILE)

    def pos_body(r, c):
        onehot = sub == e_ref[pl.ds(r, 1), :]
        p = jnp.sum(jnp.where(onehot, row_off, 0.0), axis=0, keepdims=True) + rank_ref[pl.ds(r, 1), :]
        pos_ref[pl.ds(r, 1), :] = p.astype(jnp.int32)
        return c

    lax.fori_loop(0, nrow, pos_body, 0)


def _plan(e_idx, *, name):
    n2 = e_idx.shape[0] * e_idx.shape[1]
    assert n2 % (LANES * PLAN_UNROLL) == 0
    e2d = e_idx.reshape(n2 // LANES, LANES)
    pos, meta = pl.pallas_call(
        _plan_kernel,
        out_shape=[jax.ShapeDtypeStruct(e2d.shape, jnp.int32),
                   jax.ShapeDtypeStruct((8, LANES), jnp.int32)],
        scratch_shapes=[pltpu.VMEM(e2d.shape, F32)],
        name=name,
    )(e2d)
    return pos.reshape(n2), meta[0], meta[1, :1], meta[2]


def _invert_kernel(pos_ref, pair_ref):
    n_rows = pair_ref.shape[0]
    n_pairs = pos_ref.shape[0]

    def fill_body(b, c):
        for u in range(SCALAR_UNROLL):
            pair_ref[b * SCALAR_UNROLL + u] = -1
        return c

    def pair_body(b, c):
        rows = [pos_ref[b * SCALAR_UNROLL + u] for u in range(SCALAR_UNROLL)]
        for u in range(SCALAR_UNROLL):
            pair_ref[rows[u]] = b * SCALAR_UNROLL + u
        return c

    lax.fori_loop(0, n_rows // SCALAR_UNROLL, fill_body, 0)
    lax.fori_loop(0, n_pairs // SCALAR_UNROLL, pair_body, 0)


def _invert(pos, n_rows, *, name):
    return pl.pallas_call(
        _invert_kernel,
        grid_spec=pltpu.PrefetchScalarGridSpec(
            num_scalar_prefetch=1,
            grid=(1,),
            in_specs=[],
            out_specs=pl.BlockSpec(memory_space=pltpu.SMEM),
        ),
        out_shape=jax.ShapeDtypeStruct((n_rows,), jnp.int32),
        name=name,
    )(pos)


def _expert_changed(te_ref, i):
    return jnp.logical_or(i == 0, te_ref[i] != te_ref[jnp.maximum(i - 1, 0)])


def _moe_up_kernel(pair_ref, te_ref, nu_ref, nxt_ref, x_hbm, wg_hbm, wu_hbm, h_ref,
                   wg_cur, wg_next, wg_stage, wu_cur, wu_next, wu_stage, x_rows, x_bf,
                   sem, wsems, st, *, n_tok, n_tiles, layer):
    i = pl.program_id(0)
    nu = nu_ref[0]
    weights = _ExpertWeights([(wg_hbm, wg_stage, wg_next, wg_cur), (wu_hbm, wu_stage, wu_next, wu_cur)],
                             layer, st, wsems)

    def row_copy(tile, r):
        p = pair_ref[tile * MOE_TILE + r]
        tok = jnp.where(p >= n_tok, p - n_tok, jnp.maximum(p, 0))
        src = pl.ds(pl.multiple_of(tok * ROW_SUBLANES, ROW_SUBLANES), ROW_SUBLANES)
        dst = pl.ds(r * ROW_SUBLANES, ROW_SUBLANES)
        return pltpu.make_async_copy(x_hbm.at[src, :], x_rows.at[dst, :], sem)

    def wait_tile():
        pltpu.make_async_copy(x_hbm.at[pl.ds(0, MOE_TILE * ROW_SUBLANES), :], x_rows, sem).wait()

    @pl.when(i == 0)
    def _():
        def body(rb, c):
            for u in range(DMA_UNROLL):
                row_copy(0, rb * DMA_UNROLL + u).start()
            return c
        lax.fori_loop(0, MOE_TILE // DMA_UNROLL, body, 0)
        weights.reset()

    @pl.when(i <= nu)
    def _():
        wait_tile()

    running = i < nu

    @pl.when(jnp.logical_and(running, _expert_changed(te_ref, i)))
    def _():
        weights.switch_to(te_ref[i])

    active, done = weights.begin_step(jnp.where(running, nxt_ref[i], -1))

    @pl.when(running)
    def _():
        _load_packed_rows(x_rows, x_bf)
        next_tile = jnp.minimum(i + 1, n_tiles - 1)
        for r in range(MOE_TILE):
            row_copy(next_tile, r).start(priority=r % 2)
        weights.convert_step(active, done)
        x = x_bf[...]
        a = jnp.dot(x, wg_cur[...], preferred_element_type=F32)
        b = jnp.dot(x, wu_cur[...], preferred_element_type=F32)
        h_ref[...] = (jax.nn.silu(a) * b).astype(BF16)

    weights.end_step(active, done)

    @pl.when(jnp.logical_and(i == n_tiles - 1, i < nu))
    def _():
        wait_tile()

    @pl.when(i >= nu)
    def _():
        h_ref[...] = jnp.zeros_like(h_ref)


class _ExpertWeights:
    def __init__(self, mats, layer, st, sems):
        self.mats, self.layer, self.st, self.sems = mats, layer, st, sems

    def _copies(self, expert, c):
        out = []
        for w_hbm, stage, _, _ in self.mats:
            rows = stage.shape[1]
            src = w_hbm.at[self.layer, expert, pl.ds(pl.multiple_of(c * rows, rows), rows), :]
            out.append(pltpu.make_async_copy(src, stage.at[c % 2], self.sems.at[c % 2]))
        return out

    def _start(self, expert, c):
        for cp in self._copies(expert, c):
            cp.start()
        self.st[2] = c + 1

    def _convert(self, c_src, c_dst):
        for _, stage, w_next, _ in self.mats:
            rows = stage.shape[1]
            dst = pl.ds(pl.multiple_of(c_dst * rows, rows), rows)
            w_next[dst, :] = stage[c_src % 2].astype(BF16)

    def reset(self):
        for _, stage, _, _ in self.mats:
            stage[...] = jnp.zeros_like(stage)
        self.st[1] = 0
        self.st[2] = 0

    def switch_to(self, expert):
        st = self.st

        def body(c, carry):
            @pl.when(c >= st[2])
            def _():
                self._start(expert, c)

            @pl.when(jnp.logical_and(c + 1 < W_CHUNKS, c + 1 >= st[2]))
            def _():
                self._start(expert, c + 1)

            for cp in self._copies(expert, c):
                cp.wait()
            self._convert(c, c)
            return carry

        lax.fori_loop(st[1], W_CHUNKS, body, 0)
        for _, _, w_next, w_cur in self.mats:
            _copy_rows(w_next, w_cur)
        st[1] = 0
        st[2] = 0

    def begin_step(self, next_expert):
        st = self.st
        done, issued = st[1], st[2]
        has_next = next_expert >= 0
        active = jnp.logical_and(has_next, done < issued)

        @pl.when(jnp.logical_and(has_next, jnp.logical_and(issued < W_CHUNKS, issued < done + 2)))
        def _():
            self._start(next_expert, issued)

        @pl.when(active)
        def _():
            for cp in self._copies(next_expert, done):
                cp.wait()

        return active, done

    def convert_step(self, active, done):
        self._convert(jnp.where(active, done, done + 1), jnp.where(active, done, W_CHUNKS))

    def end_step(self, active, done):
        @pl.when(active)
        def _():
            self.st[1] = done + 1


def _copy_rows(src_ref, dst_ref):
    n = dst_ref.shape[0] // CAST_ROWS

    def body(i, c):
        r = pl.multiple_of(i * CAST_ROWS, CAST_ROWS)
        dst_ref[pl.ds(r, CAST_ROWS), :] = src_ref[pl.ds(r, CAST_ROWS), :]
        return c

    lax.fori_loop(0, n, body, 0)


def _moe_down_kernel(pair_ref, te_ref, nu_ref, nxt_ref, h_ref, wd_hbm, out_hbm,
                     w_cur, w_next, stage, ya, yb, sems, tsem, wsems, st, *, n_tok, n_tiles, layer):
    i = pl.program_id(0)
    nu = nu_ref[0]
    bufs = (ya, yb)
    trash = 2 * n_tok
    weights = _ExpertWeights([(wd_hbm, stage, w_next, w_cur)], layer, st, wsems)

    def row_copy(tile, r, buf, sem):
        p = pair_ref[tile * MOE_TILE + r]
        dst = jnp.where(p < 0, trash + r, p)
        return pltpu.make_async_copy(buf.at[pl.ds(r, 1), :], out_hbm.at[pl.ds(dst, 1), :], sem)

    def wait_tile(buf, sem):
        pltpu.make_async_copy(buf, out_hbm.at[pl.ds(0, MOE_TILE), :], sem).wait()

    @pl.when(i == 0)
    def _():
        yb[...] = jnp.zeros_like(yb)
        fill = pltpu.make_async_copy(yb, out_hbm.at[pl.ds(trash, MOE_TILE), :], tsem)
        fill.start()
        fill.wait()
        weights.reset()

    running = i < nu

    @pl.when(jnp.logical_and(running, _expert_changed(te_ref, i)))
    def _():
        weights.switch_to(te_ref[i])

    active, done = weights.begin_step(jnp.where(running, nxt_ref[i], -1))

    for parity in range(2):
        cur, prev = bufs[parity], bufs[1 - parity]
        cur_sem, prev_sem = sems.at[parity], sems.at[1 - parity]
        mine = i % 2 == parity

        @pl.when(jnp.logical_and(mine, jnp.logical_and(i >= 1, i - 1 <= nu)))
        def _():
            wait_tile(cur, cur_sem)

        @pl.when(jnp.logical_and(mine, running))
        def _():
            prev_tile = jnp.maximum(i - 1, 0)
            for r in range(MOE_TILE):
                row_copy(prev_tile, r, prev, prev_sem).start(priority=r % 2)
            weights.convert_step(active, done)
            cur[...] = jnp.dot(h_ref[...], w_cur[...], preferred_element_type=F32)

        @pl.when(jnp.logical_and(mine, i == nu))
        def _():
            def body(rb, c):
                for u in range(DMA_UNROLL):
                    row_copy(i - 1, rb * DMA_UNROLL + u, prev, prev_sem).start()
                return c
            lax.fori_loop(0, MOE_TILE // DMA_UNROLL, body, 0)

            @pl.when(i == n_tiles)
            def _():
                wait_tile(prev, prev_sem)

    weights.end_step(active, done)


def _moe_ffn(x, pair, tile_expert, n_used, next_expert, w_gate, w_up, w_down, layer):
    n_tok = x.shape[0] // ROW_SUBLANES
    n_tiles = pair.shape[0] // MOE_TILE

    def tile(i, pr, te, nu):
        return (jnp.minimum(i, nu[0] - 1), 0)

    hbm = pl.BlockSpec(memory_space=pl.ANY)
    row_bufs = [pltpu.VMEM((MOE_TILE, D_MODEL), F32), pltpu.VMEM((MOE_TILE, D_MODEL), F32),
                pltpu.SemaphoreType.DMA((2,))]
    up_chunk = D_MODEL // W_CHUNKS
    up_weight = [pltpu.VMEM((D_MODEL, D_EXPERT), BF16),
                 pltpu.VMEM((D_MODEL + up_chunk, D_EXPERT), BF16),
                 pltpu.VMEM((2, up_chunk, D_EXPERT), F32)]
    h = pl.pallas_call(
        functools.partial(_moe_up_kernel, n_tok=n_tok, n_tiles=n_tiles, layer=layer),
        grid_spec=pltpu.PrefetchScalarGridSpec(
            num_scalar_prefetch=4,
            grid=(n_tiles,),
            in_specs=[hbm, hbm, hbm],
            out_specs=pl.BlockSpec((MOE_TILE, D_EXPERT), lambda i, pr, te, nu, nx: (i, 0)),
            scratch_shapes=up_weight + up_weight + [
                pltpu.VMEM((MOE_TILE * ROW_SUBLANES, LANES), jnp.uint32),
                pltpu.VMEM((MOE_TILE, D_MODEL), BF16),
                pltpu.SemaphoreType.DMA(()), pltpu.SemaphoreType.DMA((2,)),
                pltpu.SMEM((4,), jnp.int32)],
        ),
        out_shape=jax.ShapeDtypeStruct((n_tiles * MOE_TILE, D_EXPERT), BF16),
        compiler_params=_params(("arbitrary",), 48),
        name=f"moe_up_{layer}",
    )(pair, tile_expert, n_used, next_expert, x, w_gate, w_up)
    down_chunk = D_EXPERT // W_CHUNKS
    return pl.pallas_call(
        functools.partial(_moe_down_kernel, n_tok=n_tok, n_tiles=n_tiles, layer=layer),
        grid_spec=pltpu.PrefetchScalarGridSpec(
            num_scalar_prefetch=4,
            grid=(n_tiles + 1,),
            in_specs=[pl.BlockSpec((MOE_TILE, D_EXPERT), lambda i, pr, te, nu, nx: tile(i, pr, te, nu)),
                      pl.BlockSpec(memory_space=pl.ANY)],
            out_specs=pl.BlockSpec(memory_space=pl.ANY),
            scratch_shapes=[pltpu.VMEM((D_EXPERT, D_MODEL), BF16),
                            pltpu.VMEM((D_EXPERT + down_chunk, D_MODEL), BF16),
                            pltpu.VMEM((2, down_chunk, D_MODEL), F32)] + row_bufs
            + [pltpu.SemaphoreType.DMA(()), pltpu.SemaphoreType.DMA((2,)),
               pltpu.SMEM((4,), jnp.int32)],
        ),
        out_shape=jax.ShapeDtypeStruct((2 * n_tok + MOE_TILE, D_MODEL), F32),
        compiler_params=_params(("arbitrary",), 40),
        name=f"moe_down_{layer}",
    )(pair, tile_expert, n_used, next_expert, h, w_down)


def _combine(y0_ref, y1_ref, res_ref, gate_ref, g_ref, beta_ref):
    gate = gate_ref[...]
    ffn = gate[:, 0:1] * y0_ref[...] + gate[:, 1:2] * y1_ref[...]
    return _layer_norm(ALPHA * res_ref[...] + ffn, g_ref[...], beta_ref[...])


def _combine_specs(n):
    nt = n // TOK_TILE
    row = lambda i: (i, 0)
    const = lambda i: (0, 0)
    return [pl.BlockSpec((TOK_TILE, D_MODEL), row),
            pl.BlockSpec((TOK_TILE, D_MODEL), lambda i: (i + nt, 0)),
            pl.BlockSpec((TOK_TILE, D_MODEL), row),
            pl.BlockSpec((TOK_TILE, 2), row),
            pl.BlockSpec((1, D_MODEL), const),
            pl.BlockSpec((1, D_MODEL), const)]


def _combine_split_kernel(y0_ref, y1_ref, res_ref, gate_ref, g_ref, beta_ref, prompt_ref, sample_ref):
    x = _combine(y0_ref, y1_ref, res_ref, gate_ref, g_ref, beta_ref)
    is_sample = pl.program_id(0) == pl.num_programs(0) - 1

    @pl.when(jnp.logical_not(is_sample))
    def _():
        prompt_ref[...] = x

    @pl.when(is_sample)
    def _():
        sample_ref[...] = x


def _combine_split(ys, res, gates_col, g, beta, *, name):
    n = res.shape[0]
    tm = TOK_TILE
    nt = n // tm
    return pl.pallas_call(
        _combine_split_kernel,
        grid=(nt,),
        in_specs=_combine_specs(n),
        out_specs=[pl.BlockSpec((tm, D_MODEL), lambda i: (jnp.minimum(i, nt - 2), 0)),
                   pl.BlockSpec((tm, D_MODEL), lambda i: (0, 0))],
        out_shape=[jax.ShapeDtypeStruct((n - tm, D_MODEL), F32),
                   jax.ShapeDtypeStruct((tm, D_MODEL), F32)],
        compiler_params=_params(("arbitrary",), 40),
        name=name,
    )(ys, ys, res, gates_col, g, beta)


def _load_weight(w_hbm, wbf_ref, stage_ref, sems):
    rows = stage_ref.shape[1]
    n_chunks = wbf_ref.shape[0] // rows

    def chunk_copy(c):
        return pltpu.make_async_copy(w_hbm.at[pl.ds(c * rows, rows), :], stage_ref.at[c % 2],
                                     sems.at[c % 2])

    chunk_copy(0).start()
    for c in range(n_chunks):
        if c + 1 < n_chunks:
            chunk_copy(c + 1).start()
        chunk_copy(c).wait()
        wbf_ref[c * rows:(c + 1) * rows, :] = stage_ref[c % 2].astype(BF16)


def _combine_qkv_kernel(y0_ref, y1_ref, res_ref, gate_ref, g_ref, beta_ref, wq_hbm, wkv_hbm,
                        x_ref, q_ref, kv_ref, wq_bf, wkv_bf, stage_q, stage_kv, sems, *, wq_index):
    @pl.when(pl.program_id(0) == 0)
    def _():
        wq = wq_hbm
        for k in wq_index:
            wq = wq.at[k]
        _load_weight(wq, wq_bf, stage_q, sems)
        _load_weight(wkv_hbm, wkv_bf, stage_kv, sems)

    x = _combine(y0_ref, y1_ref, res_ref, gate_ref, g_ref, beta_ref)
    x_ref[...] = x
    x_bf = x.astype(BF16)
    q = jnp.dot(x_bf, wq_bf[...], preferred_element_type=F32) * (HEAD_DIM ** -0.5)
    q_ref[...] = q.astype(BF16)
    kv_ref[...] = jnp.dot(x_bf, wkv_bf[...], preferred_element_type=F32)


def _combine_qkv(ys, res, gates_col, g, beta, w_q, wq_index, w_kv, *, name):
    n = res.shape[0]
    tm = TOK_TILE
    row = lambda i: (i, 0)
    hbm = pl.BlockSpec(memory_space=pl.ANY)
    return pl.pallas_call(
        functools.partial(_combine_qkv_kernel, wq_index=wq_index),
        grid=(n // tm,),
        in_specs=_combine_specs(n) + [hbm, hbm],
        out_specs=[pl.BlockSpec((tm, D_MODEL), row),
                   pl.BlockSpec((tm, D_MODEL), row),
                   pl.BlockSpec((tm, 2 * KV_DIM), row)],
        out_shape=[jax.ShapeDtypeStruct((n, D_MODEL), F32),
                   jax.ShapeDtypeStruct((n, D_MODEL), BF16),
                   jax.ShapeDtypeStruct((n, 2 * KV_DIM), F32)],
        scratch_shapes=[pltpu.VMEM((D_MODEL, D_MODEL), BF16),
                        pltpu.VMEM((D_MODEL, 2 * KV_DIM), BF16),
                        pltpu.VMEM((2, CAST_ROWS, D_MODEL), F32),
                        pltpu.VMEM((2, CAST_ROWS, 2 * KV_DIM), F32),
                        pltpu.SemaphoreType.DMA((2,))],
        compiler_params=_params(("arbitrary",), 52),
        name=name,
    )(ys, ys, res, gates_col, g, beta, w_q, w_kv)


def _moe_block(x_rows, e_idx, w_gate, w_up, w_down, layer):
    n = x_rows.shape[0] // ROW_SUBLANES
    n_tiles = -(-(2 * n + N_EXPERTS * (MOE_TILE - 1)) // MOE_TILE)
    pos, tile_expert, n_used, next_expert = _plan(e_idx, name=f"moe_plan_{layer}")
    pair = _invert(pos, n_tiles * MOE_TILE, name=f"moe_invert_{layer}")
    return _moe_ffn(x_rows, pair, tile_expert, n_used, next_expert, w_gate, w_up, w_down, layer)


def _sigmoid(x):
    return 0.5 * jnp.tanh(0.5 * x) + 0.5


def _log_sigmoid(x):
    return -(jnp.maximum(-x, 0.0) + jnp.log1p(jnp.exp(-jnp.abs(x))))


def _lru_gate_block(xc, n, wrg_bf, wig_bf, brg_ref, big_ref, lam_ref):
    cols = slice(n * LRU_BLOCK, (n + 1) * LRU_BLOCK)
    xb = xc.astype(BF16)
    r = _sigmoid(jnp.dot(xb, wrg_bf[n], preferred_element_type=F32) + brg_ref[:, cols])
    i = _sigmoid(jnp.dot(xb, wig_bf[n], preferred_element_type=F32) + big_ref[:, cols])
    log_a = LRU_C * r * _log_sigmoid(lam_ref[:, cols])
    a = jnp.exp(log_a)
    u = xc * i * jnp.sqrt(-jnp.tanh(log_a) * (a * a + 1.0))
    return a, u


def _cast_gate_weights(wrg_ref, wig_ref, wrg_bf, wig_bf):
    for n in range(LRU_BLOCKS):
        wrg_bf[n] = wrg_ref[n].astype(BF16)
        wig_bf[n] = wig_ref[n].astype(BF16)


def _lru_prompt_kernel(xb_ref, yb_ref, cw_ref, cb_ref, wrg_ref, wig_ref, brg_ref, big_ref, lam_ref,
                       m_ref, conv_ref, hlast_ref, xs, tail, a_s, u_s, hs_t, h_s, wrg_bf, wig_bf):
    b = pl.program_id(0)
    j = pl.program_id(1)
    tt = m_ref.shape[0]
    seg_len = tt // SEGS
    taps = CONV_WIDTH - 1
    head = SEGS * taps

    @pl.when(jnp.logical_and(b == 0, j == 0))
    def _():
        _cast_gate_weights(wrg_ref, wig_ref, wrg_bf, wig_bf)

    @pl.when(j == 0)
    def _():
        tail[...] = jnp.zeros_like(tail)
        h_s[...] = jnp.zeros_like(h_s)

    for q in range(seg_len):
        xs[head + SEGS * q:head + SEGS * (q + 1), :] = jnp.concatenate(
            [xb_ref[pl.ds(CHUNKS * q + c, SEGS, stride=CHUNKS * seg_len), :] for c in range(CHUNKS)],
            axis=1)
    sub = lax.broadcasted_iota(jnp.int32, (SEGS, D_MODEL), 0)
    for k in range(taps):
        last = head + SEGS * (seg_len - taps + k)
        joined = jnp.where(sub == SEGS - 1, tail[SEGS * k:SEGS * (k + 1), :], xs[last:last + SEGS, :])
        xs[SEGS * k:SEGS * (k + 1), :] = pltpu.roll(joined, 1, axis=0)
    tail[...] = xs[head + SEGS * (seg_len - taps):head + SEGS * seg_len, :]

    for n in range(LRU_BLOCKS):
        cols = slice(n * LRU_BLOCK, (n + 1) * LRU_BLOCK)
        xc = cb_ref[:, cols] + cw_ref[0:1, cols] * xs[0:tt, cols]
        for k in range(1, CONV_WIDTH):
            xc = xc + cw_ref[k:k + 1, cols] * xs[SEGS * k:SEGS * k + tt, cols]
        a, u = _lru_gate_block(xc, n, wrg_bf, wig_bf, brg_ref, big_ref, lam_ref)
        a_s[:, cols] = a
        u_s[:, cols] = u

    def scan_body(q, carry):
        h, prod = carry
        rows = pl.ds(pl.multiple_of(q * SEGS, SEGS), SEGS)
        a = a_s[rows, :]
        h = a * h + u_s[rows, :]
        prod = a * prod
        u_s[rows, :] = h
        a_s[rows, :] = prod
        return h, prod

    h_end, prod_end = lax.fori_loop(
        0, seg_len, scan_body,
        (jnp.zeros((SEGS, D_MODEL), F32), jnp.ones((SEGS, D_MODEL), F32)))
    state = h_s[...]
    entering = []
    for s in range(SEGS):
        entering.append(state)
        state = h_end[s:s + 1, :] + prod_end[s:s + 1, :] * state
    h_s[...] = state
    enter = jnp.concatenate(entering, axis=0)

    def fix_body(q, carry):
        rows = pl.ds(pl.multiple_of(q * SEGS, SEGS), SEGS)
        h = u_s[rows, :] + a_s[rows, :] * enter
        for c in range(CHUNKS):
            hs_t[pl.ds(CHUNKS * q + c, SEGS, stride=CHUNKS * seg_len), :] = h[:, c * LANES:(c + 1) * LANES]
        return carry

    lax.fori_loop(0, seg_len, fix_body, 0)
    hs = jnp.concatenate([hs_t[pl.ds(c, tt, stride=CHUNKS), :] for c in range(CHUNKS)], axis=1)
    m_ref[...] = (hs * yb_ref[...].astype(F32)).astype(BF16)

    @pl.when(j == pl.num_programs(1) - 1)
    def _():
        for k in range(taps):
            conv_ref[k:k + 1, :] = tail[SEGS * k + SEGS - 1:SEGS * (k + 1), :]
        hlast_ref[...] = state


def _lru_prompt(xb, yb, batch, seq, cw, cb, wrg, wig, brg, big, lam, *, tt=256):
    nj = seq // tt
    row = lambda b, j: (b * nj + j, 0)
    const2 = lambda b, j: (0, 0)
    const3 = lambda b, j: (0, 0, 0)
    return pl.pallas_call(
        _lru_prompt_kernel,
        grid=(batch, nj),
        in_specs=[
            pl.BlockSpec((tt * CHUNKS, LANES), row),
            pl.BlockSpec((tt, D_MODEL), row),
            pl.BlockSpec((CONV_WIDTH, D_MODEL), const2),
            pl.BlockSpec((1, D_MODEL), const2),
            pl.BlockSpec((LRU_BLOCKS, LRU_BLOCK, LRU_BLOCK), const3),
            pl.BlockSpec((LRU_BLOCKS, LRU_BLOCK, LRU_BLOCK), const3),
            pl.BlockSpec((1, D_MODEL), const2),
            pl.BlockSpec((1, D_MODEL), const2),
            pl.BlockSpec((1, D_MODEL), const2),
        ],
        out_specs=[
            pl.BlockSpec((tt, D_MODEL), row),
            pl.BlockSpec((None, CONV_WIDTH - 1, D_MODEL), lambda b, j: (b, 0, 0)),
            pl.BlockSpec((None, 1, D_MODEL), lambda b, j: (b, 0, 0)),
        ],
        out_shape=[
            jax.ShapeDtypeStruct((batch * seq, D_MODEL), BF16),
            jax.ShapeDtypeStruct((batch, CONV_WIDTH - 1, D_MODEL), F32),
            jax.ShapeDtypeStruct((batch, 1, D_MODEL), F32),
        ],
        scratch_shapes=[
            pltpu.VMEM((tt + SEGS * (CONV_WIDTH - 1), D_MODEL), F32),
            pltpu.VMEM((SEGS * (CONV_WIDTH - 1), D_MODEL), F32),
            pltpu.VMEM((tt, D_MODEL), F32),
            pltpu.VMEM((tt, D_MODEL), F32),
            pltpu.VMEM((tt * CHUNKS, LANES), F32),
            pltpu.VMEM((1, D_MODEL), F32),
            pltpu.VMEM((LRU_BLOCKS, LRU_BLOCK, LRU_BLOCK), BF16),
            pltpu.VMEM((LRU_BLOCKS, LRU_BLOCK, LRU_BLOCK), BF16),
        ],
        compiler_params=_params(("arbitrary", "arbitrary"), 40),
        name="lru_prompt",
    )(xb, yb, cw, cb, wrg, wig, brg, big, lam)


def _lru_sample_kernel(xb_ref, yb_ref, cs_ref, h0_ref, cw_ref, cb_ref, wrg_ref, wig_ref,
                       brg_ref, big_ref, lam_ref, m_ref, conv_ref, hlast_ref, wrg_bf, wig_bf, *, steps):
    batch = h0_ref.shape[0]
    _cast_gate_weights(wrg_ref, wig_ref, wrg_bf, wig_bf)
    m_ref[steps * batch:, :] = jnp.zeros((m_ref.shape[0] - steps * batch, D_MODEL), BF16)

    def slab(t, cols):
        if t < CONV_WIDTH - 1:
            return cs_ref[t, :, cols]
        t -= CONV_WIDTH - 1
        first, stop, _ = cols.indices(D_MODEL)
        chunks = range(first // LANES, stop // LANES)
        return jnp.concatenate(
            [xb_ref[pl.ds(t * batch * CHUNKS + c, batch, stride=CHUNKS), :] for c in chunks], axis=1)

    for n in range(LRU_BLOCKS):
        cols = slice(n * LRU_BLOCK, (n + 1) * LRU_BLOCK)
        h = h0_ref[:, cols]
        for t in range(steps):
            xc = cb_ref[:, cols] + cw_ref[0:1, cols] * slab(t, cols)
            for k in range(1, CONV_WIDTH):
                xc = xc + cw_ref[k:k + 1, cols] * slab(t + k, cols)
            a, u = _lru_gate_block(xc, n, wrg_bf, wig_bf, brg_ref, big_ref, lam_ref)
            h = a * h + u
            rows = slice(t * batch, (t + 1) * batch)
            m_ref[rows, cols] = (h * yb_ref[rows, cols].astype(F32)).astype(BF16)
        hlast_ref[:, cols] = h
    for k in range(CONV_WIDTH - 1):
        conv_ref[k] = slab(steps + k, slice(None))


def _lru_sample(xb, yb, tile, steps, conv_state, h0, cw, cb, wrg, wig, brg, big, lam):
    batch = h0.shape[0]
    tok = pl.BlockSpec((TOK_TILE, D_MODEL), lambda i: (tile, 0))
    tok_chunks = pl.BlockSpec((TOK_TILE * CHUNKS, LANES), lambda i: (tile, 0))
    full = lambda a: pl.BlockSpec(a.shape, lambda i: (0,) * a.ndim)
    small = (conv_state, h0, cw, cb, wrg, wig, brg, big, lam)
    return pl.pallas_call(
        functools.partial(_lru_sample_kernel, steps=steps),
        grid=(1,),
        in_specs=[tok_chunks, tok] + [full(a) for a in small],
        out_specs=[
            pl.BlockSpec((TOK_TILE, D_MODEL), lambda i: (0, 0)),
            pl.BlockSpec((CONV_WIDTH - 1, batch, D_MODEL), lambda i: (0, 0, 0)),
            pl.BlockSpec((batch, D_MODEL), lambda i: (0, 0)),
        ],
        out_shape=[
            jax.ShapeDtypeStruct((TOK_TILE, D_MODEL), BF16),
            jax.ShapeDtypeStruct((CONV_WIDTH - 1, batch, D_MODEL), F32),
            jax.ShapeDtypeStruct((batch, D_MODEL), F32),
        ],
        scratch_shapes=[
            pltpu.VMEM((LRU_BLOCKS, LRU_BLOCK, LRU_BLOCK), BF16),
            pltpu.VMEM((LRU_BLOCKS, LRU_BLOCK, LRU_BLOCK), BF16),
        ],
        compiler_params=_params(("arbitrary",), 32),
        name="lru_sample",
    )(xb, yb, *small)


def _rel_bucket(dist):
    n = jnp.maximum(dist, 0)
    max_exact = N_BUCKETS // 2
    nf = jnp.maximum(n, 1).astype(F32)
    large = max_exact + (jnp.log(nf / max_exact) / math.log(MAX_DISTANCE / max_exact)
                         * (N_BUCKETS - max_exact)).astype(jnp.int32)
    large = jnp.minimum(large, N_BUCKETS - 1)
    return jnp.where(n < max_exact, n, large)


def _masked_buckets(dist):
    valid = (dist >= 0) & (dist < WINDOW)
    return jnp.where(valid, _rel_bucket(dist), -1).astype(jnp.int32)


def _build_bias(bucket, tab_ref, head):
    def body(bi, acc):
        return jnp.where(bucket == bi, tab_ref[bi * N_HEADS + head], acc)
    return lax.fori_loop(0, N_BUCKETS, body, jnp.full(bucket.shape, NEG_INF, F32))


def _softmax_pv(s, sink, v):
    m = jnp.maximum(jnp.max(s, axis=-1, keepdims=True), sink)
    p = jnp.exp(s - m)
    den = jnp.sum(p, axis=-1, keepdims=True) + jnp.exp(sink - m)
    return jnp.dot(p.astype(BF16), v, preferred_element_type=F32) / den


def _attn_prompt_kernel(q_ref, kvp_ref, kvc_ref, bucket_ref, tab_ref, sink_ref, o_ref, bias_s):
    b = pl.program_id(0)
    n = pl.program_id(1)

    @pl.when(jnp.logical_and(b == 0, n == 0))
    def _():
        bucket = bucket_ref[...]

        col = lax.broadcasted_iota(jnp.int32, (WINDOW, 2 * WINDOW), 1)

        def head_body(h, c):
            bias = _build_bias(bucket, tab_ref, h)
            sink = sink_ref[h]
            g = h // GROUP
            r0 = pl.multiple_of((h % GROUP) * WINDOW, WINDOW)
            bias_s[0, g, pl.ds(r0, WINDOW), :] = jnp.where(col == 0, sink, bias)
            bias_s[1, g, pl.ds(r0, WINDOW), :] = jnp.where(
                col == 0, sink, jnp.where(col < WINDOW, NEG_INF, bias))
            return c

        lax.fori_loop(0, N_HEADS, head_body, 0)

    first = (n == 0).astype(jnp.int32)
    row = lax.broadcasted_iota(jnp.int32, kvp_ref.shape, 0)
    kv_prev = jnp.where(row == 0, 0.0, kvp_ref[...])
    kv = jnp.concatenate([kv_prev, kvc_ref[...]], axis=0).astype(BF16)
    ones = jnp.ones((2 * WINDOW, 2 * HEAD_DIM), BF16)
    lane = lax.broadcasted_iota(jnp.int32, (WINDOW, 2 * HEAD_DIM), 1)
    def scores(idx):
        g, pair = divmod(idx, GROUP // 2)
        h0 = g * GROUP + 2 * pair
        kg = kv[:, g * HEAD_DIM:(g + 1) * HEAD_DIM]
        qp = jnp.concatenate([q_ref[:, h * HEAD_DIM:(h + 1) * HEAD_DIM] for h in (h0, h0 + 1)], axis=0)
        s = lax.dot_general(qp, kg, (((1,), (1,)), ((), ())), preferred_element_type=F32)
        return s + bias_s[first, g, 2 * pair * WINDOW:(2 * pair + 2) * WINDOW, :]

    def finish(idx, o_ext):
        h0 = 2 * idx
        o = o_ext[:, :2 * HEAD_DIM] * (1.0 / o_ext[:, 2 * HEAD_DIM:])
        o_ref[:, h0 * HEAD_DIM:(h0 + 2) * HEAD_DIM] = jnp.where(
            lane < HEAD_DIM, o[:WINDOW], o[WINDOW:]).astype(BF16)

    n_pairs = N_HEADS // 2
    ahead = [scores(k) for k in range(SCORE_AHEAD)]
    pending = None
    for idx in range(n_pairs):
        if idx + SCORE_AHEAD < n_pairs:
            ahead.append(scores(idx + SCORE_AHEAD))
        s = ahead.pop(0)
        g = idx // (GROUP // 2)
        vg = kv[:, KV_DIM + g * HEAD_DIM:KV_DIM + (g + 1) * HEAD_DIM]
        v_ext = jnp.concatenate([vg, vg, ones], axis=1)
        p = jnp.exp(s - jnp.max(s, axis=-1, keepdims=True)).astype(BF16)
        o_ext = jnp.dot(p, v_ext, preferred_element_type=F32)
        if pending is not None:
            finish(*pending)
        pending = (idx, o_ext)
    finish(*pending)


def _attn_prompt(q, kv, batch, seq, bucket, tab, sinks):
    nb = seq // WINDOW
    smem = pl.BlockSpec(memory_space=pltpu.SMEM)
    return pl.pallas_call(
        _attn_prompt_kernel,
        grid=(batch, nb),
        in_specs=[
            pl.BlockSpec((WINDOW, D_MODEL), lambda b, n: (b * nb + n, 0)),
            pl.BlockSpec((WINDOW, 2 * KV_DIM), lambda b, n: (jnp.maximum(b * nb + n - 1, 0), 0)),
            pl.BlockSpec((WINDOW, 2 * KV_DIM), lambda b, n: (b * nb + n, 0)),
            pl.BlockSpec((WINDOW, 2 * WINDOW), lambda b, n: (0, 0)),
            smem, smem,
        ],
        out_specs=pl.BlockSpec((WINDOW, D_MODEL), lambda b, n: (b * nb + n, 0)),
        out_shape=jax.ShapeDtypeStruct((batch * seq, D_MODEL), BF16),
        scratch_shapes=[pltpu.VMEM((2, N_KV_HEADS, GROUP * WINDOW, 2 * WINDOW), F32)],
        compiler_params=_params(("arbitrary", "arbitrary"), 32),
        name="attn_prompt",
    )(q, kv, kv, bucket, tab, sinks)


def _attn_sample_kernel(q_ref, k_ref, v_ref, bucket_ref, tab_ref, sink_ref, o_ref, bias_s):
    steps = q_ref.shape[0]

    @pl.when(pl.program_id(0) == 0)
    def _():
        bucket = bucket_ref[...]

        def head_body(h, c):
            bias_s[h] = _build_bias(bucket, tab_ref, h)
            return c

        lax.fori_loop(0, N_HEADS, head_body, 0)

    rows = lax.broadcasted_iota(jnp.int32, (GROUP * steps, 1), 0)
    k = k_ref[...].astype(BF16)
    v = v_ref[...].astype(BF16)
    groups = range(N_KV_HEADS)

    def heads_of(g):
        return range(g * GROUP, (g + 1) * GROUP)

    scores, sinks = [], []
    for g in groups:
        qg = jnp.concatenate([q_ref[:, h * HEAD_DIM:(h + 1) * HEAD_DIM] for h in heads_of(g)], axis=0)
        bias = jnp.concatenate([bias_s[h] for h in heads_of(g)], axis=0)
        sink = jnp.full((GROUP * steps, 1), sink_ref[g * GROUP], F32)
        for hh in range(1, GROUP):
            sink = jnp.where(rows >= hh * steps, sink_ref[g * GROUP + hh], sink)
        kg = k[:, g * HEAD_DIM:(g + 1) * HEAD_DIM]
        scores.append(lax.dot_general(qg, kg, (((1,), (1,)), ((), ())),
                                      preferred_element_type=F32) + bias)
        sinks.append(sink)
    outs = [_softmax_pv(scores[g], sinks[g], v[:, g * HEAD_DIM:(g + 1) * HEAD_DIM]) for g in groups]
    for g in groups:
        for hh, h in enumerate(heads_of(g)):
            o_ref[:, h * HEAD_DIM:(h + 1) * HEAD_DIM] = outs[g][hh * steps:(hh + 1) * steps].astype(BF16)


def _attn_sample(q, k_all, v_all, bucket, tab, sinks):
    batch, steps, _ = q.shape
    lk = k_all.shape[1]
    smem = pl.BlockSpec(memory_space=pltpu.SMEM)
    return pl.pallas_call(
        _attn_sample_kernel,
        grid=(batch,),
        in_specs=[
            pl.BlockSpec((None, steps, D_MODEL), lambda b: (b, 0, 0)),
            pl.BlockSpec((None, lk, KV_DIM), lambda b: (b, 0, 0)),
            pl.BlockSpec((None, lk, KV_DIM), lambda b: (b, 0, 0)),
            pl.BlockSpec((steps, lk), lambda b: (0, 0)),
            smem, smem,
        ],
        out_specs=pl.BlockSpec((None, steps, D_MODEL), lambda b: (b, 0, 0)),
        out_shape=jax.ShapeDtypeStruct((batch, steps, D_MODEL), BF16),
        scratch_shapes=[pltpu.VMEM((N_HEADS, steps, lk), F32)],
        compiler_params=_params(("arbitrary",), 32),
        name="attn_sample",
    )(q, k_all, v_all, bucket, tab, sinks)


def kernel(x_prompt, x_sample, state_conv, state_rnn, cache_k_win, cache_v_win, ln_g, ln_b, lru_w_x, lru_b_x, lru_w_y, lru_b_y, lru_conv_w, lru_conv_b, lru_w_rg, lru_b_rg, lru_w_ig, lru_b_ig, lru_lam, lru_w_out, lru_b_out, attn_w_kv, attn_w_q, attn_w_o, attn_sinks, rel_bias, moe_w_router, moe_b_router, moe_w_gate, moe_w_up, moe_w_down):
    bp, seq, _ = x_prompt.shape
    bs, steps, _ = x_sample.shape
    n_p = bp * seq
    n_s = bs * steps

    assert n_p % TOK_TILE == 0 and n_s <= TOK_TILE
    sample_tile = n_p // TOK_TILE

    def pad_tile(rows):
        return jnp.pad(rows, ((0, TOK_TILE - n_s), (0, 0)))

    x0 = (x_prompt.reshape(n_p, D_MODEL),
          pad_tile(x_sample.transpose(1, 0, 2).reshape(n_s, D_MODEL)))
    wr_t = moe_w_router.T
    br = moe_b_router.reshape(N_EXPERTS, 1)
    vec = lambda a: a.reshape(1, -1)

    xb = _linear(x0, lru_w_x, (0,), vec(lru_b_x[0]), F32, name="lru_in_x", chunk_rows=True)
    yb = _linear(x0, lru_w_y, (0,), vec(lru_b_y[0]), BF16, act="gelu", name="lru_in_y")
    lru_args = (lru_conv_w[0], vec(lru_conv_b[0]), lru_w_rg[0], lru_w_ig[0],
                vec(lru_b_rg[0]), vec(lru_b_ig[0]), vec(lru_lam[0]))
    m_p, conv_p, rnn_p = _lru_prompt(xb, yb, bp, seq, *lru_args)
    m_s, conv_s, rnn_s = _lru_sample(xb, yb, sample_tile, steps,
                                     state_conv[0].transpose(1, 0, 2), state_rnn[0], *lru_args)
    x1, x1_rows, e_idx, gates = _proj_ln((m_p, m_s), lru_w_out, (0,), vec(lru_b_out[0]), x0,
                                vec(ln_g[0, 0]), vec(ln_b[0, 0]), wr_t, br, name="lru_out_ln")
    ys = _moe_block(x1_rows, e_idx, moe_w_gate, moe_w_up, moe_w_down, 0)

    x2, q, kv = _combine_qkv(ys, x1, gates.T, vec(ln_g[0, 1]), vec(ln_b[0, 1]),
                             attn_w_q, (0,), attn_w_kv, name="moe_combine_qkv")
    tab = rel_bias.reshape(-1)
    sinks = attn_sinks[0]
    qi = jnp.arange(WINDOW)[:, None]
    kj = jnp.arange(2 * WINDOW)[None, :]
    o_p = _attn_prompt(q, kv, bp, seq, _masked_buckets(qi + WINDOW - kj), tab, sinks)
    kv_s = kv[n_p:n_p + n_s].reshape(steps, bs, 2, KV_DIM).transpose(2, 1, 0, 3)
    k_all = jnp.concatenate([cache_k_win.reshape(bs, WINDOW, KV_DIM), kv_s[0]], axis=1)
    v_all = jnp.concatenate([cache_v_win.reshape(bs, WINDOW, KV_DIM), kv_s[1]], axis=1)
    dist_s = jnp.arange(steps)[:, None] + WINDOW - jnp.arange(WINDOW + steps)[None, :]
    q_s = q[n_p:n_p + n_s].reshape(steps, bs, D_MODEL).transpose(1, 0, 2)
    o_s = _attn_sample(q_s, k_all, v_all, _masked_buckets(dist_s), tab, sinks)
    o_s = pad_tile(o_s.transpose(1, 0, 2).reshape(n_s, D_MODEL))
    x3, x3_rows, e_idx, gates = _proj_ln((o_p, o_s), attn_w_o, (0,), jnp.zeros((1, D_MODEL), F32), x2,
                                vec(ln_g[1, 0]), vec(ln_b[1, 0]), wr_t, br, name="attn_out_ln")
    ys = _moe_block(x3_rows, e_idx, moe_w_gate, moe_w_up, moe_w_down, 1)
    y_p, y_s = _combine_split(ys, x3, gates.T, vec(ln_g[1, 1]), vec(ln_b[1, 1]), name="moe_combine_1")

    y_prompt = y_p.reshape(bp, seq, D_MODEL)
    y_sample = y_s[:n_s].reshape(steps, bs, D_MODEL).transpose(1, 0, 2)
    kv_p = jnp.stack([kv[(b + 1) * seq - WINDOW:(b + 1) * seq] for b in range(bp)])
    kv_p = kv_p.reshape(bp, WINDOW, 2, N_KV_HEADS, HEAD_DIM)
    k_win_s = k_all[:, steps:].reshape(bs, WINDOW, N_KV_HEADS, HEAD_DIM)
    v_win_s = v_all[:, steps:].reshape(bs, WINDOW, N_KV_HEADS, HEAD_DIM)
    return (y_prompt, y_sample,
            conv_p[None], rnn_p.reshape(1, bp, D_MODEL),
            kv_p[:, :, 0], kv_p[:, :, 1],
            conv_s.transpose(1, 0, 2)[None], rnn_s[None],
            k_win_s, v_win_s)
```

```python
import functools
import math

import jax
import jax.numpy as jnp
from jax import lax
from jax.experimental import pallas as pl
from jax.experimental.pallas import tpu as pltpu

D_MODEL = 2048
DEPTH = 2
LRU_BLOCKS = 8
LRU_BLOCK = D_MODEL // LRU_BLOCKS
CONV_WIDTH = 4
LRU_C = 8.0
N_HEADS = 32
HEAD_DIM = 64
N_KV_HEADS = 8
GROUP = N_HEADS // N_KV_HEADS
KV_DIM = N_KV_HEADS * HEAD_DIM
WINDOW = 128
N_BUCKETS = 32
MAX_DISTANCE = 128
N_EXPERTS = 16
N_GROUPS = 4
EXPERTS_PER_GROUP = N_EXPERTS // N_GROUPS
D_EXPERT = 1024
ALPHA = (2 * DEPTH) ** 0.25
LN_EPS = 1e-5

LANES = 128
SEGS = 8
CHUNKS = D_MODEL // LANES
ROW_SUBLANES = D_MODEL // (2 * LANES)
MOE_TILE = 256
TOK_TILE = 256
DMA_UNROLL = 8
SCALAR_UNROLL = 32
PLAN_UNROLL = 4
PAIR_BLOCK = 4
W_CHUNKS = 4
CAST_ROWS = 256
BF16 = jnp.bfloat16
F32 = jnp.float32
NEG_INF = float("-inf")


def _params(sem, vmem_mb):
    return pltpu.CompilerParams(dimension_semantics=sem, vmem_limit_bytes=vmem_mb * 1024 * 1024)


def _cast_rows(src_ref, dst_ref):
    n = src_ref.shape[0] // CAST_ROWS

    def body(i, c):
        r = pl.multiple_of(i * CAST_ROWS, CAST_ROWS)
        dst_ref[pl.ds(r, CAST_ROWS), :] = src_ref[pl.ds(r, CAST_ROWS), :].astype(BF16)
        return c

    lax.fori_loop(0, n, body, 0)


def _layer_norm(z, g, b):
    mu = jnp.mean(z, axis=-1, keepdims=True)
    zc = z - mu
    var = jnp.mean(zc * zc, axis=-1, keepdims=True)
    return zc * lax.rsqrt(var + LN_EPS) * g + b


def _store_packed_rows(x_bf, rows_ref):
    n = x_bf.shape[0]
    bits = pltpu.bitcast(x_bf.astype(F32), jnp.uint32)
    packed = bits[:, D_MODEL // 2:] | (bits[:, :D_MODEL // 2] >> 16)
    for c in range(ROW_SUBLANES):
        rows_ref[pl.ds(c, n, stride=ROW_SUBLANES), :] = packed[:, c * LANES:(c + 1) * LANES]


def _load_packed_rows(rows_ref, x_bf_ref):
    n = x_bf_ref.shape[0]
    for c in range(ROW_SUBLANES):
        words = rows_ref[pl.ds(c, n, stride=ROW_SUBLANES), :]
        low = pltpu.bitcast(words << 16, F32).astype(BF16)
        high = pltpu.bitcast(words & jnp.uint32(0xFFFF0000), F32).astype(BF16)
        x_bf_ref[:, c * LANES:(c + 1) * LANES] = low
        x_bf_ref[:, D_MODEL // 2 + c * LANES:D_MODEL // 2 + (c + 1) * LANES] = high


def _tok_operands(x, tile_of=lambda i: i):
    if isinstance(x, tuple):
        xp, xs = x
        d = xp.shape[1]
        last_p = xp.shape[0] // TOK_TILE - 1
        specs = [pl.BlockSpec((TOK_TILE, d), lambda i, *_: (jnp.minimum(tile_of(i), last_p), 0)),
                 pl.BlockSpec((TOK_TILE, d), lambda i, *_: (0, 0))]
        return [xp, xs], specs, last_p + 2
    return ([x], [pl.BlockSpec((TOK_TILE, x.shape[1]), lambda i, *_: (tile_of(i), 0))],
            x.shape[0] // TOK_TILE)


def _tok_load(refs, is_sample=None):
    if len(refs) == 1:
        return refs[0][...]
    if is_sample is None:
        is_sample = pl.program_id(0) == pl.num_programs(0) - 1
    return jnp.where(is_sample, refs[1][...], refs[0][...])


def _linear_kernel(*refs, n_x, act, scale, chunk_rows):
    x_refs, (w_ref, b_ref, o_ref, wbf_ref) = refs[:n_x], refs[n_x:]

    @pl.when(pl.program_id(0) == 0)
    def _():
        _cast_rows(w_ref, wbf_ref)

    y = jnp.dot(_tok_load(x_refs).astype(BF16), wbf_ref[...], preferred_element_type=F32)
    y = y + b_ref[...]
    if act == "gelu":
        y = jax.nn.gelu(y)
    if scale != 1.0:
        y = y * scale
    if chunk_rows:
        n_chunks = y.shape[1] // LANES
        for c in range(n_chunks):
            o_ref[pl.ds(c, y.shape[0], stride=n_chunks), :] = y[:, c * LANES:(c + 1) * LANES]
    else:
        o_ref[...] = y.astype(o_ref.dtype)


def _linear(x, w, w_index, b, out_dtype, *, name, act=None, scale=1.0, chunk_rows=False):
    arrays, specs, nt = _tok_operands(x)
    k, nout = w.shape[-2:]
    w_block = (None,) * len(w_index) + (k, nout)
    if chunk_rows:
        out_block, out_rows, out_cols = (TOK_TILE * (nout // LANES), LANES), nt * TOK_TILE * (nout // LANES), LANES
    else:
        out_block, out_rows, out_cols = (TOK_TILE, nout), nt * TOK_TILE, nout
    return pl.pallas_call(
        functools.partial(_linear_kernel, n_x=len(arrays), act=act, scale=scale,
                          chunk_rows=chunk_rows),
        grid=(nt,),
        in_specs=specs + [
            pl.BlockSpec(w_block, lambda i: w_index + (0, 0), pipeline_mode=pl.Buffered(1)),
            pl.BlockSpec((1, nout), lambda i: (0, 0)),
        ],
        out_specs=pl.BlockSpec(out_block, lambda i: (i, 0)),
        out_shape=jax.ShapeDtypeStruct((out_rows, out_cols), out_dtype),
        scratch_shapes=[pltpu.VMEM((k, nout), BF16)],
        compiler_params=_params(("arbitrary",), 48),
        name=name,
    )(*arrays, w, b)


def _route(logits_t, b_router):
    aff = jax.nn.sigmoid(logits_t)
    sel = aff + b_router
    srow = [sel[e:e + 1, :] for e in range(N_EXPERTS)]
    arow = [aff[e:e + 1, :] for e in range(N_EXPERTS)]

    def top2_sum(v):
        pairs = [v[i] + v[j] for i in range(4) for j in range(i + 1, 4)]
        return functools.reduce(jnp.maximum, pairs)

    scores = [top2_sum(srow[4 * g:4 * g + 4]) for g in range(N_GROUPS)]
    best = scores[0]
    gi = jnp.zeros_like(best, dtype=jnp.int32)
    for g in range(1, N_GROUPS):
        upd = scores[g] > best
        best = jnp.where(upd, scores[g], best)
        gi = jnp.where(upd, g, gi)

    def pick_group(rows, j):
        out = rows[j]
        for g in range(1, N_GROUPS):
            out = jnp.where(gi == g, rows[4 * g + j], out)
        return out

    v = [pick_group(srow, j) for j in range(EXPERTS_PER_GROUP)]
    a = [pick_group(arow, j) for j in range(EXPERTS_PER_GROUP)]

    m1, i1 = v[0], jnp.zeros_like(gi)
    for j in range(1, EXPERTS_PER_GROUP):
        upd = v[j] > m1
        m1 = jnp.where(upd, v[j], m1)
        i1 = jnp.where(upd, j, i1)
    m2 = jnp.full_like(m1, NEG_INF)
    i2 = jnp.zeros_like(gi)
    for j in range(EXPERTS_PER_GROUP):
        cand = jnp.where(i1 == j, NEG_INF, v[j])
        upd = cand > m2
        m2 = jnp.where(upd, cand, m2)
        i2 = jnp.where(upd, j, i2)

    def pick_idx(rows, idx):
        out = rows[0]
        for j in range(1, EXPERTS_PER_GROUP):
            out = jnp.where(idx == j, rows[j], out)
        return out

    a1 = pick_idx(a, i1)
    a2 = pick_idx(a, i2)
    tot = a1 + a2
    e_idx = jnp.concatenate([gi * EXPERTS_PER_GROUP + i1, gi * EXPERTS_PER_GROUP + i2], axis=0)
    gates = jnp.concatenate([a1 / tot, a2 / tot], axis=0)
    return e_idx, gates


def _proj_ln_kernel(*refs, n_m, n_res):
    m_refs = refs[:n_m]
    w_ref, b_ref = refs[n_m:n_m + 2]
    res_refs = refs[n_m + 2:n_m + 2 + n_res]
    (g_ref, beta_ref, wr_ref, br_ref, x_ref, xrow_ref, e_ref, gate_ref,
     wbf_ref, ya, yb) = refs[n_m + 2 + n_res:]
    i = pl.program_id(0)
    n_tiles = pl.num_programs(0) - 1

    @pl.when(i == 0)
    def _():
        _cast_rows(w_ref, wbf_ref)
        yb[...] = jnp.zeros_like(yb)

    for parity, (cur, prev) in enumerate(((ya, yb), (yb, ya))):
        @pl.when(i % 2 == parity)
        def _():
            cur[...] = jnp.dot(_tok_load(m_refs, i >= n_tiles - 1), wbf_ref[...],
                               preferred_element_type=F32)
            y = prev[...] + b_ref[...]
            x = _layer_norm(ALPHA * _tok_load(res_refs, i == n_tiles) + y, g_ref[...], beta_ref[...])
            x_ref[...] = x
            x_bf = x.astype(BF16)
            _store_packed_rows(x_bf, xrow_ref)
            logits_t = lax.dot_general(wr_ref[...].astype(BF16), x_bf,
                                       (((1,), (1,)), ((), ())), preferred_element_type=F32)
            e_idx, gates = _route(logits_t, br_ref[...])
            e_ref[...] = e_idx
            gate_ref[...] = gates


def _proj_ln(m, w, w_index, b, res, g, beta, wr_t, br, *, name):
    nt = _tok_operands(m)[2]
    m_arrays, m_specs, _ = _tok_operands(m, lambda i: jnp.minimum(i, nt - 1))
    res_arrays, res_specs, _ = _tok_operands(res, lambda i: jnp.maximum(i - 1, 0))
    k = w.shape[-2]
    tm = TOK_TILE
    n = nt * tm
    row = lambda i: (jnp.maximum(i - 1, 0), 0)
    const = lambda i: (0, 0)
    x, x_rows, e_idx, gates = pl.pallas_call(
        functools.partial(_proj_ln_kernel, n_m=len(m_arrays), n_res=len(res_arrays)),
        grid=(nt + 1,),
        in_specs=m_specs + [
            pl.BlockSpec((None,) * len(w_index) + (k, D_MODEL), lambda i: w_index + (0, 0),
                         pipeline_mode=pl.Buffered(1)),
            pl.BlockSpec((1, D_MODEL), const),
        ] + res_specs + [
            pl.BlockSpec((1, D_MODEL), const),
            pl.BlockSpec((1, D_MODEL), const),
            pl.BlockSpec((N_EXPERTS, D_MODEL), const),
            pl.BlockSpec((N_EXPERTS, 1), const),
        ],
        out_specs=[
            pl.BlockSpec((tm, D_MODEL), row),
            pl.BlockSpec((tm * ROW_SUBLANES, LANES), row),
            pl.BlockSpec((None, 2, tm), lambda i: (jnp.maximum(i - 1, 0), 0, 0)),
            pl.BlockSpec((None, 2, tm), lambda i: (jnp.maximum(i - 1, 0), 0, 0)),
        ],
        out_shape=[
            jax.ShapeDtypeStruct((n, D_MODEL), F32),
            jax.ShapeDtypeStruct((n * ROW_SUBLANES, LANES), jnp.uint32),
            jax.ShapeDtypeStruct((nt, 2, tm), jnp.int32),
            jax.ShapeDtypeStruct((nt, 2, tm), F32),
        ],
        scratch_shapes=[pltpu.VMEM((k, D_MODEL), BF16),
                        pltpu.VMEM((tm, D_MODEL), F32), pltpu.VMEM((tm, D_MODEL), F32)],
        compiler_params=_params(("arbitrary",), 52),
        name=name,
    )(*m_arrays, w, b, *res_arrays, g, beta, wr_t, br)
    e_idx = e_idx.transpose(1, 0, 2).reshape(2, n)
    gates = gates.transpose(1, 0, 2).reshape(2, n)
    return x, x_rows, e_idx, gates


def _plan_kernel(e_ref, pos_ref, meta_ref, rank_ref):
    nrow = e_ref.shape[0]
    ri = lax.broadcasted_iota(jnp.int32, (LANES, LANES), 0)
    ci = lax.broadcasted_iota(jnp.int32, (LANES, LANES), 1)
    tri = jnp.where(ri <= ci, 1.0, 0.0).astype(BF16)
    sub = lax.broadcasted_iota(jnp.int32, (N_EXPERTS, LANES), 0)

    def count_body(b, base):
        rows = [b * PLAN_UNROLL + u for u in range(PLAN_UNROLL)]
        onehots = [sub == e_ref[pl.ds(r, 1), :] for r in rows]
        locs = [jnp.dot(jnp.where(oh, 1.0, 0.0).astype(BF16), tri, preferred_element_type=F32)
                for oh in onehots]
        for r, onehot, loc in zip(rows, onehots, locs):
            rank_ref[pl.ds(r, 1), :] = jnp.sum(jnp.where(onehot, base + loc - 1.0, 0.0),
                                               axis=0, keepdims=True)
            base = base + jnp.broadcast_to(loc[:, LANES - 1:LANES], (N_EXPERTS, LANES))
        return base

    count = lax.fori_loop(0, nrow // PLAN_UNROLL, count_body, jnp.zeros((N_EXPERTS, LANES), F32))
    ntile = jnp.floor((count + (MOE_TILE - 1.0)) * (1.0 / MOE_TILE))
    offs = []
    acc = jnp.zeros((1, LANES), F32)
    for e in range(N_EXPERTS):
        offs.append(acc)
        acc = acc + ntile[e:e + 1, :]
    tile_off = jnp.concatenate(offs, axis=0)
    tile_end = tile_off + ntile
    lane = lax.broadcasted_iota(jnp.int32, (N_EXPERTS, LANES), 1).astype(F32)
    tile_expert = jnp.sum(jnp.where(tile_end <= lane, 1.0, 0.0), axis=0, keepdims=True)
    tile_expert = jnp.minimum(tile_expert, N_EXPERTS - 1.0)
    own = jnp.logical_and(tile_off <= lane, lane < tile_end)
    run_end = jnp.sum(jnp.where(own, tile_end, 0.0), axis=0, keepdims=True)
    next_expert = jnp.sum(jnp.where(tile_end <= run_end, 1.0, 0.0), axis=0, keepdims=True)
    has_next = jnp.logical_and(lane[0:1, :] < acc, run_end < acc)
    next_expert = jnp.where(has_next, next_expert, -1.0)
    meta = jnp.concatenate([tile_expert, acc, next_expert, jnp.zeros((5, LANES), F32)], axis=0)
    meta_ref[...] = meta.astype(jnp.int32)
    row_off = tile_off * float(MOE_TILE)

    def pos_body(r, c):
        onehot = sub == e_ref[pl.ds(r, 1), :]
        p = jnp.sum(jnp.where(onehot, row_off, 0.0), axis=0, keepdims=True) + rank_ref[pl.ds(r, 1), :]
        pos_ref[pl.ds(r, 1), :] = p.astype(jnp.int32)
        return c

    lax.fori_loop(0, nrow, pos_body, 0)


def _plan(e_idx, *, name):
    n2 = e_idx.shape[0] * e_idx.shape[1]
    assert n2 % (LANES * PLAN_UNROLL) == 0
    e2d = e_idx.reshape(n2 // LANES, LANES)
    pos, meta = pl.pallas_call(
        _plan_kernel,
        out_shape=[jax.ShapeDtypeStruct(e2d.shape, jnp.int32),
                   jax.ShapeDtypeStruct((8, LANES), jnp.int32)],
        scratch_shapes=[pltpu.VMEM(e2d.shape, F32)],
        name=name,
    )(e2d)
    return pos.reshape(n2), meta[0], meta[1, :1], meta[2]


def _invert_kernel(pos_ref, pair_ref):
    n_rows = pair_ref.shape[0]
    n_pairs = pos_ref.shape[0]

    def fill_body(b, c):
        for u in range(SCALAR_UNROLL):
            pair_ref[b * SCALAR_UNROLL + u] = -1
        return c

    def pair_body(b, c):
        rows = [pos_ref[b * SCALAR_UNROLL + u] for u in range(SCALAR_UNROLL)]
        for u in range(SCALAR_UNROLL):
            pair_ref[rows[u]] = b * SCALAR_UNROLL + u
        return c

    lax.fori_loop(0, n_rows // SCALAR_UNROLL, fill_body, 0)
    lax.fori_loop(0, n_pairs // SCALAR_UNROLL, pair_body, 0)


def _invert(pos, n_rows, *, name):
    return pl.pallas_call(
        _invert_kernel,
        grid_spec=pltpu.PrefetchScalarGridSpec(
            num_scalar_prefetch=1,
            grid=(1,),
            in_specs=[],
            out_specs=pl.BlockSpec(memory_space=pltpu.SMEM),
        ),
        out_shape=jax.ShapeDtypeStruct((n_rows,), jnp.int32),
        name=name,
    )(pos)


def _expert_changed(te_ref, i):
    return jnp.logical_or(i == 0, te_ref[i] != te_ref[jnp.maximum(i - 1, 0)])


def _moe_up_kernel(pair_ref, te_ref, nu_ref, nxt_ref, x_hbm, wg_hbm, wu_hbm, h_ref,
                   wg_cur, wg_next, wg_stage, wu_cur, wu_next, wu_stage, x_rows, x_bf,
                   sem, wsems, st, *, n_tok, n_tiles, layer):
    i = pl.program_id(0)
    nu = nu_ref[0]
    weights = _ExpertWeights([(wg_hbm, wg_stage, wg_next, wg_cur), (wu_hbm, wu_stage, wu_next, wu_cur)],
                             layer, st, wsems)

    def row_copy(tile, r):
        p = pair_ref[tile * MOE_TILE + r]
        tok = jnp.where(p >= n_tok, p - n_tok, jnp.maximum(p, 0))
        src = pl.ds(pl.multiple_of(tok * ROW_SUBLANES, ROW_SUBLANES), ROW_SUBLANES)
        dst = pl.ds(r * ROW_SUBLANES, ROW_SUBLANES)
        return pltpu.make_async_copy(x_hbm.at[src, :], x_rows.at[dst, :], sem)

    def wait_tile():
        pltpu.make_async_copy(x_hbm.at[pl.ds(0, MOE_TILE * ROW_SUBLANES), :], x_rows, sem).wait()

    @pl.when(i == 0)
    def _():
        def body(rb, c):
            for u in range(DMA_UNROLL):
                row_copy(0, rb * DMA_UNROLL + u).start()
            return c
        lax.fori_loop(0, MOE_TILE // DMA_UNROLL, body, 0)
        weights.reset()

    @pl.when(i <= nu)
    def _():
        wait_tile()

    running = i < nu

    @pl.when(jnp.logical_and(running, _expert_changed(te_ref, i)))
    def _():
        weights.switch_to(te_ref[i])

    active, done = weights.begin_step(jnp.where(running, nxt_ref[i], -1))

    @pl.when(running)
    def _():
        _load_packed_rows(x_rows, x_bf)
        next_tile = jnp.minimum(i + 1, n_tiles - 1)
        for r in range(MOE_TILE):
            row_copy(next_tile, r).start(priority=r % 2)
        weights.convert_step(active, done)
        x = x_bf[...]
        a = jnp.dot(x, wg_cur[...], preferred_element_type=F32)
        b = jnp.dot(x, wu_cur[...], preferred_element_type=F32)
        h_ref[...] = (jax.nn.silu(a) * b).astype(BF16)

    weights.end_step(active, done)

    @pl.when(jnp.logical_and(i == n_tiles - 1, i < nu))
    def _():
        wait_tile()

    @pl.when(i >= nu)
    def _():
        h_ref[...] = jnp.zeros_like(h_ref)


class _ExpertWeights:
    def __init__(self, mats, layer, st, sems):
        self.mats, self.layer, self.st, self.sems = mats, layer, st, sems

    def _copies(self, expert, c):
        out = []
        for w_hbm, stage, _, _ in self.mats:
            rows = stage.shape[1]
            src = w_hbm.at[self.layer, expert, pl.ds(pl.multiple_of(c * rows, rows), rows), :]
            out.append(pltpu.make_async_copy(src, stage.at[c % 2], self.sems.at[c % 2]))
        return out

    def _start(self, expert, c):
        for cp in self._copies(expert, c):
            cp.start()
        self.st[2] = c + 1

    def _convert(self, c_src, c_dst):
        for _, stage, w_next, _ in self.mats:
            rows = stage.shape[1]
            dst = pl.ds(pl.multiple_of(c_dst * rows, rows), rows)
            w_next[dst, :] = stage[c_src % 2].astype(BF16)

    def reset(self):
        for _, stage, _, _ in self.mats:
            stage[...] = jnp.zeros_like(stage)
        self.st[1] = 0
        self.st[2] = 0

    def switch_to(self, expert):
        st = self.st

        def body(c, carry):
            @pl.when(c >= st[2])
            def _():
                self._start(expert, c)

            @pl.when(jnp.logical_and(c + 1 < W_CHUNKS, c + 1 >= st[2]))
            def _():
                self._start(expert, c + 1)

            for cp in self._copies(expert, c):
                cp.wait()
            self._convert(c, c)
            return carry

        lax.fori_loop(st[1], W_CHUNKS, body, 0)
        for _, _, w_next, w_cur in self.mats:
            _copy_rows(w_next, w_cur)
        st[1] = 0
        st[2] = 0

    def begin_step(self, next_expert):
        st = self.st
        done, issued = st[1], st[2]
        has_next = next_expert >= 0
        active = jnp.logical_and(has_next, done < issued)

        @pl.when(jnp.logical_and(has_next, jnp.logical_and(issued < W_CHUNKS, issued < done + 2)))
        def _():
            self._start(next_expert, issued)

        @pl.when(active)
        def _():
            for cp in self._copies(next_expert, done):
                cp.wait()

        return active, done

    def convert_step(self, active, done):
        self._convert(jnp.where(active, done, done + 1), jnp.where(active, done, W_CHUNKS))

    def end_step(self, active, done):
        @pl.when(active)
        def _():
            self.st[1] = done + 1


def _copy_rows(src_ref, dst_ref):
    n = dst_ref.shape[0] // CAST_ROWS

    def body(i, c):
        r = pl.multiple_of(i * CAST_ROWS, CAST_ROWS)
        dst_ref[pl.ds(r, CAST_ROWS), :] = src_ref[pl.ds(r, CAST_ROWS), :]
        return c

    lax.fori_loop(0, n, body, 0)


def _moe_down_kernel(pair_ref, te_ref, nu_ref, nxt_ref, h_ref, wd_hbm, out_hbm,
                     w_cur, w_next, stage, ya, yb, sems, tsem, wsems, st, *, n_tok, n_tiles, layer):
    i = pl.program_id(0)
    nu = nu_ref[0]
    bufs = (ya, yb)
    trash = 2 * n_tok
    weights = _ExpertWeights([(wd_hbm, stage, w_next, w_cur)], layer, st, wsems)

    def row_copy(tile, r, buf, sem):
        p = pair_ref[tile * MOE_TILE + r]
        dst = jnp.where(p < 0, trash + r, p)
        return pltpu.make_async_copy(buf.at[pl.ds(r, 1), :], out_hbm.at[pl.ds(dst, 1), :], sem)

    def wait_tile(buf, sem):
        pltpu.make_async_copy(buf, out_hbm.at[pl.ds(0, MOE_TILE), :], sem).wait()

    @pl.when(i == 0)
    def _():
        yb[...] = jnp.zeros_like(yb)
        fill = pltpu.make_async_copy(yb, out_hbm.at[pl.ds(trash, MOE_TILE), :], tsem)
        fill.start()
        fill.wait()
        weights.reset()

    running = i < nu

    @pl.when(jnp.logical_and(running, _expert_changed(te_ref, i)))
    def _():
        weights.switch_to(te_ref[i])

    active, done = weights.begin_step(jnp.where(running, nxt_ref[i], -1))

    for parity in range(2):
        cur, prev = bufs[parity], bufs[1 - parity]
        cur_sem, prev_sem = sems.at[parity], sems.at[1 - parity]
        mine = i % 2 == parity

        @pl.when(jnp.logical_and(mine, jnp.logical_and(i >= 1, i - 1 <= nu)))
        def _():
            wait_tile(cur, cur_sem)

        @pl.when(jnp.logical_and(mine, running))
        def _():
            prev_tile = jnp.maximum(i - 1, 0)
            for r in range(MOE_TILE):
                row_copy(prev_tile, r, prev, prev_sem).start(priority=r % 2)
            weights.convert_step(active, done)
            cur[...] = jnp.dot(h_ref[...], w_cur[...], preferred_element_type=F32)

        @pl.when(jnp.logical_and(mine, i == nu))
        def _():
            def body(rb, c):
                for u in range(DMA_UNROLL):
                    row_copy(i - 1, rb * DMA_UNROLL + u, prev, prev_sem).start()
                return c
            lax.fori_loop(0, MOE_TILE // DMA_UNROLL, body, 0)

            @pl.when(i == n_tiles)
            def _():
                wait_tile(prev, prev_sem)

    weights.end_step(active, done)


def _moe_ffn(x, pair, tile_expert, n_used, next_expert, w_gate, w_up, w_down, layer):
    n_tok = x.shape[0] // ROW_SUBLANES
    n_tiles = pair.shape[0] // MOE_TILE

    def tile(i, pr, te, nu):
        return (jnp.minimum(i, nu[0] - 1), 0)

    hbm = pl.BlockSpec(memory_space=pl.ANY)
    row_bufs = [pltpu.VMEM((MOE_TILE, D_MODEL), F32), pltpu.VMEM((MOE_TILE, D_MODEL), F32),
                pltpu.SemaphoreType.DMA((2,))]
    up_chunk = D_MODEL // W_CHUNKS
    up_weight = [pltpu.VMEM((D_MODEL, D_EXPERT), BF16),
                 pltpu.VMEM((D_MODEL + up_chunk, D_EXPERT), BF16),
                 pltpu.VMEM((2, up_chunk, D_EXPERT), F32)]
    h = pl.pallas_call(
        functools.partial(_moe_up_kernel, n_tok=n_tok, n_tiles=n_tiles, layer=layer),
        grid_spec=pltpu.PrefetchScalarGridSpec(
            num_scalar_prefetch=4,
            grid=(n_tiles,),
            in_specs=[hbm, hbm, hbm],
            out_specs=pl.BlockSpec((MOE_TILE, D_EXPERT), lambda i, pr, te, nu, nx: (i, 0)),
            scratch_shapes=up_weight + up_weight + [
                pltpu.VMEM((MOE_TILE * ROW_SUBLANES, LANES), jnp.uint32),
                pltpu.VMEM((MOE_TILE, D_MODEL), BF16),
                pltpu.SemaphoreType.DMA(()), pltpu.SemaphoreType.DMA((2,)),
                pltpu.SMEM((4,), jnp.int32)],
        ),
        out_shape=jax.ShapeDtypeStruct((n_tiles * MOE_TILE, D_EXPERT), BF16),
        compiler_params=_params(("arbitrary",), 48),
        name=f"moe_up_{layer}",
    )(pair, tile_expert, n_used, next_expert, x, w_gate, w_up)
    down_chunk = D_EXPERT // W_CHUNKS
    return pl.pallas_call(
        functools.partial(_moe_down_kernel, n_tok=n_tok, n_tiles=n_tiles, layer=layer),
        grid_spec=pltpu.PrefetchScalarGridSpec(
            num_scalar_prefetch=4,
            grid=(n_tiles + 1,),
            in_specs=[pl.BlockSpec((MOE_TILE, D_EXPERT), lambda i, pr, te, nu, nx: tile(i, pr, te, nu)),
                      pl.BlockSpec(memory_space=pl.ANY)],
            out_specs=pl.BlockSpec(memory_space=pl.ANY),
            scratch_shapes=[pltpu.VMEM((D_EXPERT, D_MODEL), BF16),
                            pltpu.VMEM((D_EXPERT + down_chunk, D_MODEL), BF16),
                            pltpu.VMEM((2, down_chunk, D_MODEL), F32)] + row_bufs
            + [pltpu.SemaphoreType.DMA(()), pltpu.SemaphoreType.DMA((2,)),
               pltpu.SMEM((4,), jnp.int32)],
        ),
        out_shape=jax.ShapeDtypeStruct((2 * n_tok + MOE_TILE, D_MODEL), F32),
        compiler_params=_params(("arbitrary",), 40),
        name=f"moe_down_{layer}",
    )(pair, tile_expert, n_used, next_expert, h, w_down)


def _combine(y0_ref, y1_ref, res_ref, gate_ref, g_ref, beta_ref):
    gate = gate_ref[...]
    ffn = gate[:, 0:1] * y0_ref[...] + gate[:, 1:2] * y1_ref[...]
    return _layer_norm(ALPHA * res_ref[...] + ffn, g_ref[...], beta_ref[...])


def _combine_specs(n):
    nt = n // TOK_TILE
    row = lambda i: (i, 0)
    const = lambda i: (0, 0)
    return [pl.BlockSpec((TOK_TILE, D_MODEL), row),
            pl.BlockSpec((TOK_TILE, D_MODEL), lambda i: (i + nt, 0)),
            pl.BlockSpec((TOK_TILE, D_MODEL), row),
            pl.BlockSpec((TOK_TILE, 2), row),
            pl.BlockSpec((1, D_MODEL), const),
            pl.BlockSpec((1, D_MODEL), const)]


def _combine_split_kernel(y0_ref, y1_ref, res_ref, gate_ref, g_ref, beta_ref, prompt_ref, sample_ref):
    x = _combine(y0_ref, y1_ref, res_ref, gate_ref, g_ref, beta_ref)
    is_sample = pl.program_id(0) == pl.num_programs(0) - 1

    @pl.when(jnp.logical_not(is_sample))
    def _():
        prompt_ref[...] = x

    @pl.when(is_sample)
    def _():
        sample_ref[...] = x


def _combine_split(ys, res, gates_col, g, beta, *, name):
    n = res.shape[0]
    tm = TOK_TILE
    nt = n // tm
    return pl.pallas_call(
        _combine_split_kernel,
        grid=(nt,),
        in_specs=_combine_specs(n),
        out_specs=[pl.BlockSpec((tm, D_MODEL), lambda i: (jnp.minimum(i, nt - 2), 0)),
                   pl.BlockSpec((tm, D_MODEL), lambda i: (0, 0))],
        out_shape=[jax.ShapeDtypeStruct((n - tm, D_MODEL), F32),
                   jax.ShapeDtypeStruct((tm, D_MODEL), F32)],
        compiler_params=_params(("arbitrary",), 40),
        name=name,
    )(ys, ys, res, gates_col, g, beta)


def _load_weight(w_hbm, wbf_ref, stage_ref, sems):
    rows = stage_ref.shape[1]
    n_chunks = wbf_ref.shape[0] // rows

    def chunk_copy(c):
        return pltpu.make_async_copy(w_hbm.at[pl.ds(c * rows, rows), :], stage_ref.at[c % 2],
                                     sems.at[c % 2])

    chunk_copy(0).start()
    for c in range(n_chunks):
        if c + 1 < n_chunks:
            chunk_copy(c + 1).start()
        chunk_copy(c).wait()
        wbf_ref[c * rows:(c + 1) * rows, :] = stage_ref[c % 2].astype(BF16)


def _combine_qkv_kernel(y0_ref, y1_ref, res_ref, gate_ref, g_ref, beta_ref, wq_hbm, wkv_hbm,
                        x_ref, q_ref, kv_ref, wq_bf, wkv_bf, stage_q, stage_kv, sems, *, wq_index):
    @pl.when(pl.program_id(0) == 0)
    def _():
        wq = wq_hbm
        for k in wq_index:
            wq = wq.at[k]
        _load_weight(wq, wq_bf, stage_q, sems)
        _load_weight(wkv_hbm, wkv_bf, stage_kv, sems)

    x = _combine(y0_ref, y1_ref, res_ref, gate_ref, g_ref, beta_ref)
    x_ref[...] = x
    x_bf = x.astype(BF16)
    q = jnp.dot(x_bf, wq_bf[...], preferred_element_type=F32) * (HEAD_DIM ** -0.5)
    q_ref[...] = q.astype(BF16)
    kv_ref[...] = jnp.dot(x_bf, wkv_bf[...], preferred_element_type=F32)


def _combine_qkv(ys, res, gates_col, g, beta, w_q, wq_index, w_kv, *, name):
    n = res.shape[0]
    tm = TOK_TILE
    row = lambda i: (i, 0)
    hbm = pl.BlockSpec(memory_space=pl.ANY)
    return pl.pallas_call(
        functools.partial(_combine_qkv_kernel, wq_index=wq_index),
        grid=(n // tm,),
        in_specs=_combine_specs(n) + [hbm, hbm],
        out_specs=[pl.BlockSpec((tm, D_MODEL), row),
                   pl.BlockSpec((tm, D_MODEL), row),
                   pl.BlockSpec((tm, 2 * KV_DIM), row)],
        out_shape=[jax.ShapeDtypeStruct((n, D_MODEL), F32),
                   jax.ShapeDtypeStruct((n, D_MODEL), BF16),
                   jax.ShapeDtypeStruct((n, 2 * KV_DIM), F32)],
        scratch_shapes=[pltpu.VMEM((D_MODEL, D_MODEL), BF16),
                        pltpu.VMEM((D_MODEL, 2 * KV_DIM), BF16),
                        pltpu.VMEM((2, CAST_ROWS, D_MODEL), F32),
                        pltpu.VMEM((2, CAST_ROWS, 2 * KV_DIM), F32),
                        pltpu.SemaphoreType.DMA((2,))],
        compiler_params=_params(("arbitrary",), 52),
        name=name,
    )(ys, ys, res, gates_col, g, beta, w_q, w_kv)


def _moe_block(x_rows, e_idx, w_gate, w_up, w_down, layer):
    n = x_rows.shape[0] // ROW_SUBLANES
    n_tiles = -(-(2 * n + N_EXPERTS * (MOE_TILE - 1)) // MOE_TILE)
    pos, tile_expert, n_used, next_expert = _plan(e_idx, name=f"moe_plan_{layer}")
    pair = _invert(pos, n_tiles * MOE_TILE, name=f"moe_invert_{layer}")
    return _moe_ffn(x_rows, pair, tile_expert, n_used, next_expert, w_gate, w_up, w_down, layer)


def _sigmoid(x):
    return 0.5 * jnp.tanh(0.5 * x) + 0.5


def _log_sigmoid(x):
    return -(jnp.maximum(-x, 0.0) + jnp.log1p(jnp.exp(-jnp.abs(x))))


def _lru_gate_block(xc, n, wrg_bf, wig_bf, brg_ref, big_ref, lam_ref):
    cols = slice(n * LRU_BLOCK, (n + 1) * LRU_BLOCK)
    xb = xc.astype(BF16)
    r = _sigmoid(jnp.dot(xb, wrg_bf[n], preferred_element_type=F32) + brg_ref[:, cols])
    i = _sigmoid(jnp.dot(xb, wig_bf[n], preferred_element_type=F32) + big_ref[:, cols])
    log_a = LRU_C * r * _log_sigmoid(lam_ref[:, cols])
    a = jnp.exp(log_a)
    u = xc * i * jnp.sqrt(-jnp.tanh(log_a) * (a * a + 1.0))
    return a, u


def _cast_gate_weights(wrg_ref, wig_ref, wrg_bf, wig_bf):
    for n in range(LRU_BLOCKS):
        wrg_bf[n] = wrg_ref[n].astype(BF16)
        wig_bf[n] = wig_ref[n].astype(BF16)


def _lru_prompt_kernel(xb_ref, yb_ref, cw_ref, cb_ref, wrg_ref, wig_ref, brg_ref, big_ref, lam_ref,
                       m_ref, conv_ref, hlast_ref, xs, tail, a_s, u_s, hs_t, h_s, wrg_bf, wig_bf):
    b = pl.program_id(0)
    j = pl.program_id(1)
    tt = m_ref.shape[0]
    seg_len = tt // SEGS
    taps = CONV_WIDTH - 1
    head = SEGS * taps

    @pl.when(jnp.logical_and(b == 0, j == 0))
    def _():
        _cast_gate_weights(wrg_ref, wig_ref, wrg_bf, wig_bf)

    @pl.when(j == 0)
    def _():
        tail[...] = jnp.zeros_like(tail)
        h_s[...] = jnp.zeros_like(h_s)

    for q in range(seg_len):
        xs[head + SEGS * q:head + SEGS * (q + 1), :] = jnp.concatenate(
            [xb_ref[pl.ds(CHUNKS * q + c, SEGS, stride=CHUNKS * seg_len), :] for c in range(CHUNKS)],
            axis=1)
    sub = lax.broadcasted_iota(jnp.int32, (SEGS, D_MODEL), 0)
    for k in range(taps):
        last = head + SEGS * (seg_len - taps + k)
        joined = jnp.where(sub == SEGS - 1, tail[SEGS * k:SEGS * (k + 1), :], xs[last:last + SEGS, :])
        xs[SEGS * k:SEGS * (k + 1), :] = pltpu.roll(joined, 1, axis=0)
    tail[...] = xs[head + SEGS * (seg_len - taps):head + SEGS * seg_len, :]

    for n in range(LRU_BLOCKS):
        cols = slice(n * LRU_BLOCK, (n + 1) * LRU_BLOCK)
        xc = cb_ref[:, cols] + cw_ref[0:1, cols] * xs[0:tt, cols]
        for k in range(1, CONV_WIDTH):
            xc = xc + cw_ref[k:k + 1, cols] * xs[SEGS * k:SEGS * k + tt, cols]
        a, u = _lru_gate_block(xc, n, wrg_bf, wig_bf, brg_ref, big_ref, lam_ref)
        a_s[:, cols] = a
        u_s[:, cols] = u

    def scan_body(q, carry):
        h, prod = carry
        rows = pl.ds(pl.multiple_of(q * SEGS, SEGS), SEGS)
        a = a_s[rows, :]
        h = a * h + u_s[rows, :]
        prod = a * prod
        u_s[rows, :] = h
        a_s[rows, :] = prod
        return h, prod

    h_end, prod_end = lax.fori_loop(
        0, seg_len, scan_body,
        (jnp.zeros((SEGS, D_MODEL), F32), jnp.ones((SEGS, D_MODEL), F32)))
    state = h_s[...]
    entering = []
    for s in range(SEGS):
        entering.append(state)
        state = h_end[s:s + 1, :] + prod_end[s:s + 1, :] * state
    h_s[...] = state
    enter = jnp.concatenate(entering, axis=0)

    def fix_body(q, carry):
        rows = pl.ds(pl.multiple_of(q * SEGS, SEGS), SEGS)
        h = u_s[rows, :] + a_s[rows, :] * enter
        for c in range(CHUNKS):
            hs_t[pl.ds(CHUNKS * q + c, SEGS, stride=CHUNKS * seg_len), :] = h[:, c * LANES:(c + 1) * LANES]
        return carry

    lax.fori_loop(0, seg_len, fix_body, 0)
    hs = jnp.concatenate([hs_t[pl.ds(c, tt, stride=CHUNKS), :] for c in range(CHUNKS)], axis=1)
    m_ref[...] = (hs * yb_ref[...].astype(F32)).astype(BF16)

    @pl.when(j == pl.num_programs(1) - 1)
    def _():
        for k in range(taps):
            conv_ref[k:k + 1, :] = tail[SEGS * k + SEGS - 1:SEGS * (k + 1), :]
        hlast_ref[...] = state


def _lru_prompt(xb, yb, batch, seq, cw, cb, wrg, wig, brg, big, lam, *, tt=256):
    nj = seq // tt
    row = lambda b, j: (b * nj + j, 0)
    const2 = lambda b, j: (0, 0)
    const3 = lambda b, j: (0, 0, 0)
    return pl.pallas_call(
        _lru_prompt_kernel,
        grid=(batch, nj),
        in_specs=[
            pl.BlockSpec((tt * CHUNKS, LANES), row),
            pl.BlockSpec((tt, D_MODEL), row),
            pl.BlockSpec((CONV_WIDTH, D_MODEL), const2),
            pl.BlockSpec((1, D_MODEL), const2),
            pl.BlockSpec((LRU_BLOCKS, LRU_BLOCK, LRU_BLOCK), const3),
            pl.BlockSpec((LRU_BLOCKS, LRU_BLOCK, LRU_BLOCK), const3),
            pl.BlockSpec((1, D_MODEL), const2),
            pl.BlockSpec((1, D_MODEL), const2),
            pl.BlockSpec((1, D_MODEL), const2),
        ],
        out_specs=[
            pl.BlockSpec((tt, D_MODEL), row),
            pl.BlockSpec((None, CONV_WIDTH - 1, D_MODEL), lambda b, j: (b, 0, 0)),
            pl.BlockSpec((None, 1, D_MODEL), lambda b, j: (b, 0, 0)),
        ],
        out_shape=[
            jax.ShapeDtypeStruct((batch * seq, D_MODEL), BF16),
            jax.ShapeDtypeStruct((batch, CONV_WIDTH - 1, D_MODEL), F32),
            jax.ShapeDtypeStruct((batch, 1, D_MODEL), F32),
        ],
        scratch_shapes=[
            pltpu.VMEM((tt + SEGS * (CONV_WIDTH - 1), D_MODEL), F32),
            pltpu.VMEM((SEGS * (CONV_WIDTH - 1), D_MODEL), F32),
            pltpu.VMEM((tt, D_MODEL), F32),
            pltpu.VMEM((tt, D_MODEL), F32),
            pltpu.VMEM((tt * CHUNKS, LANES), F32),
            pltpu.VMEM((1, D_MODEL), F32),
            pltpu.VMEM((LRU_BLOCKS, LRU_BLOCK, LRU_BLOCK), BF16),
            pltpu.VMEM((LRU_BLOCKS, LRU_BLOCK, LRU_BLOCK), BF16),
        ],
        compiler_params=_params(("arbitrary", "arbitrary"), 40),
        name="lru_prompt",
    )(xb, yb, cw, cb, wrg, wig, brg, big, lam)


def _lru_sample_kernel(xb_ref, yb_ref, cs_ref, h0_ref, cw_ref, cb_ref, wrg_ref, wig_ref,
                       brg_ref, big_ref, lam_ref, m_ref, conv_ref, hlast_ref, wrg_bf, wig_bf, *, steps):
    batch = h0_ref.shape[0]
    _cast_gate_weights(wrg_ref, wig_ref, wrg_bf, wig_bf)
    m_ref[steps * batch:, :] = jnp.zeros((m_ref.shape[0] - steps * batch, D_MODEL), BF16)

    def slab(t, cols):
        if t < CONV_WIDTH - 1:
            return cs_ref[t, :, cols]
        t -= CONV_WIDTH - 1
        first, stop, _ = cols.indices(D_MODEL)
        chunks = range(first // LANES, stop // LANES)
        return jnp.concatenate(
            [xb_ref[pl.ds(t * batch * CHUNKS + c, batch, stride=CHUNKS), :] for c in chunks], axis=1)

    for n in range(LRU_BLOCKS):
        cols = slice(n * LRU_BLOCK, (n + 1) * LRU_BLOCK)
        h = h0_ref[:, cols]
        for t in range(steps):
            xc = cb_ref[:, cols] + cw_ref[0:1, cols] * slab(t, cols)
            for k in range(1, CONV_WIDTH):
                xc = xc + cw_ref[k:k + 1, cols] * slab(t + k, cols)
            a, u = _lru_gate_block(xc, n, wrg_bf, wig_bf, brg_ref, big_ref, lam_ref)
            h = a * h + u
            rows = slice(t * batch, (t + 1) * batch)
            m_ref[rows, cols] = (h * yb_ref[rows, cols].astype(F32)).astype(BF16)
        hlast_ref[:, cols] = h
    for k in range(CONV_WIDTH - 1):
        conv_ref[k] = slab(steps + k, slice(None))


def _lru_sample(xb, yb, tile, steps, conv_state, h0, cw, cb, wrg, wig, brg, big, lam):
    batch = h0.shape[0]
    tok = pl.BlockSpec((TOK_TILE, D_MODEL), lambda i: (tile, 0))
    tok_chunks = pl.BlockSpec((TOK_TILE * CHUNKS, LANES), lambda i: (tile, 0))
    full = lambda a: pl.BlockSpec(a.shape, lambda i: (0,) * a.ndim)
    small = (conv_state, h0, cw, cb, wrg, wig, brg, big, lam)
    return pl.pallas_call(
        functools.partial(_lru_sample_kernel, steps=steps),
        grid=(1,),
        in_specs=[tok_chunks, tok] + [full(a) for a in small],
        out_specs=[
            pl.BlockSpec((TOK_TILE, D_MODEL), lambda i: (0, 0)),
            pl.BlockSpec((CONV_WIDTH - 1, batch, D_MODEL), lambda i: (0, 0, 0)),
            pl.BlockSpec((batch, D_MODEL), lambda i: (0, 0)),
        ],
        out_shape=[
            jax.ShapeDtypeStruct((TOK_TILE, D_MODEL), BF16),
            jax.ShapeDtypeStruct((CONV_WIDTH - 1, batch, D_MODEL), F32),
            jax.ShapeDtypeStruct((batch, D_MODEL), F32),
        ],
        scratch_shapes=[
            pltpu.VMEM((LRU_BLOCKS, LRU_BLOCK, LRU_BLOCK), BF16),
            pltpu.VMEM((LRU_BLOCKS, LRU_BLOCK, LRU_BLOCK), BF16),
        ],
        compiler_params=_params(("arbitrary",), 32),
        name="lru_sample",
    )(xb, yb, *small)


def _rel_bucket(dist):
    n = jnp.maximum(dist, 0)
    max_exact = N_BUCKETS // 2
    nf = jnp.maximum(n, 1).astype(F32)
    large = max_exact + (jnp.log(nf / max_exact) / math.log(MAX_DISTANCE / max_exact)
                         * (N_BUCKETS - max_exact)).astype(jnp.int32)
    large = jnp.minimum(large, N_BUCKETS - 1)
    return jnp.where(n < max_exact, n, large)


def _masked_buckets(dist):
    valid = (dist >= 0) & (dist < WINDOW)
    return jnp.where(valid, _rel_bucket(dist), -1).astype(jnp.int32)


def _build_bias(bucket, tab_ref, head):
    def body(bi, acc):
        return jnp.where(bucket == bi, tab_ref[bi * N_HEADS + head], acc)
    return lax.fori_loop(0, N_BUCKETS, body, jnp.full(bucket.shape, NEG_INF, F32))


def _softmax_pv(s, sink, v):
    m = jnp.maximum(jnp.max(s, axis=-1, keepdims=True), sink)
    p = jnp.exp(s - m)
    den = jnp.sum(p, axis=-1, keepdims=True) + jnp.exp(sink - m)
    return jnp.dot(p.astype(BF16), v, preferred_element_type=F32) / den


def _attn_prompt_kernel(q_ref, kvp_ref, kvc_ref, bucket_ref, tab_ref, sink_ref, o_ref, bias_s):
    b = pl.program_id(0)
    n = pl.program_id(1)

    @pl.when(jnp.logical_and(b == 0, n == 0))
    def _():
        bucket = bucket_ref[...]

        col = lax.broadcasted_iota(jnp.int32, (WINDOW, 2 * WINDOW), 1)

        def head_body(h, c):
            bias = _build_bias(bucket, tab_ref, h)
            sink = sink_ref[h]
            g = h // GROUP
            r0 = pl.multiple_of((h % GROUP) * WINDOW, WINDOW)
            bias_s[0, g, pl.ds(r0, WINDOW), :] = jnp.where(col == 0, sink, bias)
            bias_s[1, g, pl.ds(r0, WINDOW), :] = jnp.where(
                col == 0, sink, jnp.where(col < WINDOW, NEG_INF, bias))
            return c

        lax.fori_loop(0, N_HEADS, head_body, 0)

    first = (n == 0).astype(jnp.int32)
    row = lax.broadcasted_iota(jnp.int32, kvp_ref.shape, 0)
    kv_prev = jnp.where(row == 0, 0.0, kvp_ref[...])
    kv = jnp.concatenate([kv_prev, kvc_ref[...]], axis=0).astype(BF16)
    ones = jnp.ones((2 * WINDOW, 2 * HEAD_DIM), BF16)
    lane = lax.broadcasted_iota(jnp.int32, (WINDOW, 2 * HEAD_DIM), 1)
    def scores(idx):
        g, pair = divmod(idx, GROUP // 2)
        h0 = g * GROUP + 2 * pair
        kg = kv[:, g * HEAD_DIM:(g + 1) * HEAD_DIM]
        qp = jnp.concatenate([q_ref[:, h * HEAD_DIM:(h + 1) * HEAD_DIM] for h in (h0, h0 + 1)], axis=0)
        s = lax.dot_general(qp, kg, (((1,), (1,)), ((), ())), preferred_element_type=F32)
        return s + bias_s[first, g, 2 * pair * WINDOW:(2 * pair + 2) * WINDOW, :]

    def finish(idx, o_ext):
        h0 = 2 * idx
        o = o_ext[:, :2 * HEAD_DIM] * (1.0 / o_ext[:, 2 * HEAD_DIM:])
        o_ref[:, h0 * HEAD_DIM:(h0 + 2) * HEAD_DIM] = jnp.where(
            lane < HEAD_DIM, o[:WINDOW], o[WINDOW:]).astype(BF16)

    n_pairs = N_HEADS // 2

    def values(idx, p):
        g = idx // (GROUP // 2)
        vg = kv[:, KV_DIM + g * HEAD_DIM:KV_DIM + (g + 1) * HEAD_DIM]
        v_ext = jnp.concatenate([vg, vg, ones], axis=1)
        return jnp.dot(p, v_ext, preferred_element_type=F32)

    for first_pair in range(0, n_pairs, PAIR_BLOCK):
        block = range(first_pair, first_pair + PAIR_BLOCK)
        ss = [scores(idx) for idx in block]
        ms = [jnp.max(s, axis=-1, keepdims=True) for s in ss]
        ps = [jnp.exp(s - m).astype(BF16) for s, m in zip(ss, ms)]
        os_ = [values(idx, p) for idx, p in zip(block, ps)]
        for idx, o_ext in zip(block, os_):
            finish(idx, o_ext)


def _attn_prompt(q, kv, batch, seq, bucket, tab, sinks):
    nb = seq // WINDOW
    smem = pl.BlockSpec(memory_space=pltpu.SMEM)
    return pl.pallas_call(
        _attn_prompt_kernel,
        grid=(batch, nb),
        in_specs=[
            pl.BlockSpec((WINDOW, D_MODEL), lambda b, n: (b * nb + n, 0)),
            pl.BlockSpec((WINDOW, 2 * KV_DIM), lambda b, n: (jnp.maximum(b * nb + n - 1, 0), 0)),
            pl.BlockSpec((WINDOW, 2 * KV_DIM), lambda b, n: (b * nb + n, 0)),
            pl.BlockSpec((WINDOW, 2 * WINDOW), lambda b, n: (0, 0)),
            smem, smem,
        ],
        out_specs=pl.BlockSpec((WINDOW, D_MODEL), lambda b, n: (b * nb + n, 0)),
        out_shape=jax.ShapeDtypeStruct((batch * seq, D_MODEL), BF16),
        scratch_shapes=[pltpu.VMEM((2, N_KV_HEADS, GROUP * WINDOW, 2 * WINDOW), F32)],
        compiler_params=_params(("arbitrary", "arbitrary"), 32),
        name="attn_prompt",
    )(q, kv, kv, bucket, tab, sinks)


def _attn_sample_kernel(q_ref, k_ref, v_ref, bucket_ref, tab_ref, sink_ref, o_ref, bias_s):
    steps = q_ref.shape[0]

    @pl.when(pl.program_id(0) == 0)
    def _():
        bucket = bucket_ref[...]

        def head_body(h, c):
            bias_s[h] = _build_bias(bucket, tab_ref, h)
            return c

        lax.fori_loop(0, N_HEADS, head_body, 0)

    rows = lax.broadcasted_iota(jnp.int32, (GROUP * steps, 1), 0)
    k = k_ref[...].astype(BF16)
    v = v_ref[...].astype(BF16)
    groups = range(N_KV_HEADS)

    def heads_of(g):
        return range(g * GROUP, (g + 1) * GROUP)

    scores, sinks = [], []
    for g in groups:
        qg = jnp.concatenate([q_ref[:, h * HEAD_DIM:(h + 1) * HEAD_DIM] for h in heads_of(g)], axis=0)
        bias = jnp.concatenate([bias_s[h] for h in heads_of(g)], axis=0)
        sink = jnp.full((GROUP * steps, 1), sink_ref[g * GROUP], F32)
        for hh in range(1, GROUP):
            sink = jnp.where(rows >= hh * steps, sink_ref[g * GROUP + hh], sink)
        kg = k[:, g * HEAD_DIM:(g + 1) * HEAD_DIM]
        scores.append(lax.dot_general(qg, kg, (((1,), (1,)), ((), ())),
                                      preferred_element_type=F32) + bias)
        sinks.append(sink)
    outs = [_softmax_pv(scores[g], sinks[g], v[:, g * HEAD_DIM:(g + 1) * HEAD_DIM]) for g in groups]
    for g in groups:
        for hh, h in enumerate(heads_of(g)):
            o_ref[:, h * HEAD_DIM:(h + 1) * HEAD_DIM] = outs[g][hh * steps:(hh + 1) * steps].astype(BF16)


def _attn_sample(q, k_all, v_all, bucket, tab, sinks):
    batch, steps, _ = q.shape
    lk = k_all.shape[1]
    smem = pl.BlockSpec(memory_space=pltpu.SMEM)
    return pl.pallas_call(
        _attn_sample_kernel,
        grid=(batch,),
        in_specs=[
            pl.BlockSpec((None, steps, D_MODEL), lambda b: (b, 0, 0)),
            pl.BlockSpec((None, lk, KV_DIM), lambda b: (b, 0, 0)),
            pl.BlockSpec((None, lk, KV_DIM), lambda b: (b, 0, 0)),
            pl.BlockSpec((steps, lk), lambda b: (0, 0)),
            smem, smem,
        ],
        out_specs=pl.BlockSpec((None, steps, D_MODEL), lambda b: (b, 0, 0)),
        out_shape=jax.ShapeDtypeStruct((batch, steps, D_MODEL), BF16),
        scratch_shapes=[pltpu.VMEM((N_HEADS, steps, lk), F32)],
        compiler_params=_params(("arbitrary",), 32),
        name="attn_sample",
    )(q, k_all, v_all, bucket, tab, sinks)


def kernel(x_prompt, x_sample, state_conv, state_rnn, cache_k_win, cache_v_win, ln_g, ln_b, lru_w_x, lru_b_x, lru_w_y, lru_b_y, lru_conv_w, lru_conv_b, lru_w_rg, lru_b_rg, lru_w_ig, lru_b_ig, lru_lam, lru_w_out, lru_b_out, attn_w_kv, attn_w_q, attn_w_o, attn_sinks, rel_bias, moe_w_router, moe_b_router, moe_w_gate, moe_w_up, moe_w_down):
    bp, seq, _ = x_prompt.shape
    bs, steps, _ = x_sample.shape
    n_p = bp * seq
    n_s = bs * steps

    assert n_p % TOK_TILE == 0 and n_s <= TOK_TILE
    sample_tile = n_p // TOK_TILE

    def pad_tile(rows):
        return jnp.pad(rows, ((0, TOK_TILE - n_s), (0, 0)))

    x0 = (x_prompt.reshape(n_p, D_MODEL),
          pad_tile(x_sample.transpose(1, 0, 2).reshape(n_s, D_MODEL)))
    wr_t = moe_w_router.T
    br = moe_b_router.reshape(N_EXPERTS, 1)
    vec = lambda a: a.reshape(1, -1)

    xb = _linear(x0, lru_w_x, (0,), vec(lru_b_x[0]), F32, name="lru_in_x", chunk_rows=True)
    yb = _linear(x0, lru_w_y, (0,), vec(lru_b_y[0]), BF16, act="gelu", name="lru_in_y")
    lru_args = (lru_conv_w[0], vec(lru_conv_b[0]), lru_w_rg[0], lru_w_ig[0],
                vec(lru_b_rg[0]), vec(lru_b_ig[0]), vec(lru_lam[0]))
    m_p, conv_p, rnn_p = _lru_prompt(xb, yb, bp, seq, *lru_args)
    m_s, conv_s, rnn_s = _lru_sample(xb, yb, sample_tile, steps,
                                     state_conv[0].transpose(1, 0, 2), state_rnn[0], *lru_args)
    x1, x1_rows, e_idx, gates = _proj_ln((m_p, m_s), lru_w_out, (0,), vec(lru_b_out[0]), x0,
                                vec(ln_g[0, 0]), vec(ln_b[0, 0]), wr_t, br, name="lru_out_ln")
    ys = _moe_block(x1_rows, e_idx, moe_w_gate, moe_w_up, moe_w_down, 0)

    x2, q, kv = _combine_qkv(ys, x1, gates.T, vec(ln_g[0, 1]), vec(ln_b[0, 1]),
                             attn_w_q, (0,), attn_w_kv, name="moe_combine_qkv")
    tab = rel_bias.reshape(-1)
    sinks = attn_sinks[0]
    qi = jnp.arange(WINDOW)[:, None]
    kj = jnp.arange(2 * WINDOW)[None, :]
    o_p = _attn_prompt(q, kv, bp, seq, _masked_buckets(qi + WINDOW - kj), tab, sinks)
    kv_s = kv[n_p:n_p + n_s].reshape(steps, bs, 2, KV_DIM).transpose(2, 1, 0, 3)
    k_all = jnp.concatenate([cache_k_win.reshape(bs, WINDOW, KV_DIM), kv_s[0]], axis=1)
    v_all = jnp.concatenate([cache_v_win.reshape(bs, WINDOW, KV_DIM), kv_s[1]], axis=1)
    dist_s = jnp.arange(steps)[:, None] + WINDOW - jnp.arange(WINDOW + steps)[None, :]
    q_s = q[n_p:n_p + n_s].reshape(steps, bs, D_MODEL).transpose(1, 0, 2)
    o_s = _attn_sample(q_s, k_all, v_all, _masked_buckets(dist_s), tab, sinks)
    o_s = pad_tile(o_s.transpose(1, 0, 2).reshape(n_s, D_MODEL))
    x3, x3_rows, e_idx, gates = _proj_ln((o_p, o_s), attn_w_o, (0,), jnp.zeros((1, D_MODEL), F32), x2,
                                vec(ln_g[1, 0]), vec(ln_b[1, 0]), wr_t, br, name="attn_out_ln")
    ys = _moe_block(x3_rows, e_idx, moe_w_gate, moe_w_up, moe_w_down, 1)
    y_p, y_s = _combine_split(ys, x3, gates.T, vec(ln_g[1, 1]), vec(ln_b[1, 1]), name="moe_combine_1")

    y_prompt = y_p.reshape(bp, seq, D_MODEL)
    y_sample = y_s[:n_s].reshape(steps, bs, D_MODEL).transpose(1, 0, 2)
    kv_p = jnp.stack([kv[(b + 1) * seq - WINDOW:(b + 1) * seq] for b in range(bp)])
    kv_p = kv_p.reshape(bp, WINDOW, 2, N_KV_HEADS, HEAD_DIM)
    k_win_s = k_all[:, steps:].reshape(bs, WINDOW, N_KV_HEADS, HEAD_DIM)
    v_win_s = v_all[:, steps:].reshape(bs, WINDOW, N_KV_HEADS, HEAD_DIM)
    return (y_prompt, y_sample,
            conv_p[None], rnn_p.reshape(1, bp, D_MODEL),
            kv_p[:, :, 0], kv_p[:, :, 1],
            conv_s.transpose(1, 0, 2)[None], rnn_s[None],
            k_win_s, v_win_s)
```

```python
import functools
import math

import jax
import jax.numpy as jnp
from jax import lax
from jax.experimental import pallas as pl
from jax.experimental.pallas import tpu as pltpu

D_MODEL = 2048
DEPTH = 2
LRU_BLOCKS = 8
LRU_BLOCK = D_MODEL // LRU_BLOCKS
CONV_WIDTH = 4
LRU_C = 8.0
N_HEADS = 32
HEAD_DIM = 64
N_KV_HEADS = 8
GROUP = N_HEADS // N_KV_HEADS
KV_DIM = N_KV_HEADS * HEAD_DIM
WINDOW = 128
N_BUCKETS = 32
MAX_DISTANCE = 128
N_EXPERTS = 16
N_GROUPS = 4
EXPERTS_PER_GROUP = N_EXPERTS // N_GROUPS
D_EXPERT = 1024
ALPHA = (2 * DEPTH) ** 0.25
LN_EPS = 1e-5

LANES = 128
SEGS = 8
CHUNKS = D_MODEL // LANES
ROW_SUBLANES = D_MODEL // (2 * LANES)
MOE_TILE = 256
TOK_TILE = 256
CHUNK_TOK_ROWS = CHUNKS + 4
CHUNK_SEG_TOKS = TOK_TILE // SEGS
CHUNK_SEG_ROWS = CHUNK_SEG_TOKS * CHUNK_TOK_ROWS + 4
CHUNK_TILE_ROWS = SEGS * CHUNK_SEG_ROWS
DMA_UNROLL = 8
SCALAR_UNROLL = 32
PLAN_UNROLL = 4
PAIR_BLOCK = 4
GATE_BLOCKS = 4
W_CHUNKS = 4
CAST_ROWS = 256
BF16 = jnp.bfloat16
F32 = jnp.float32
NEG_INF = float("-inf")


def _params(sem, vmem_mb):
    return pltpu.CompilerParams(dimension_semantics=sem, vmem_limit_bytes=vmem_mb * 1024 * 1024)


def _cast_rows(src_ref, dst_ref):
    n = src_ref.shape[0] // CAST_ROWS

    def body(i, c):
        r = pl.multiple_of(i * CAST_ROWS, CAST_ROWS)
        dst_ref[pl.ds(r, CAST_ROWS), :] = src_ref[pl.ds(r, CAST_ROWS), :].astype(BF16)
        return c

    lax.fori_loop(0, n, body, 0)


def _layer_norm(z, g, b):
    mu = jnp.mean(z, axis=-1, keepdims=True)
    zc = z - mu
    var = jnp.mean(zc * zc, axis=-1, keepdims=True)
    return zc * lax.rsqrt(var + LN_EPS) * g + b


def _chunk_row(tok, chunk):
    seg, t = divmod(tok, CHUNK_SEG_TOKS)
    return seg * CHUNK_SEG_ROWS + t * CHUNK_TOK_ROWS + chunk


def _chunk_rows_store(ref, y):
    ref[...] = jnp.zeros_like(ref)
    for seg in range(SEGS):
        rows = slice(seg * CHUNK_SEG_TOKS, (seg + 1) * CHUNK_SEG_TOKS)
        for c in range(CHUNKS):
            dst = pl.ds(_chunk_row(seg * CHUNK_SEG_TOKS, c), CHUNK_SEG_TOKS, stride=CHUNK_TOK_ROWS)
            ref[dst, :] = y[rows, c * LANES:(c + 1) * LANES]


def _chunk_rows_load(ref, tok0, count, chunks=range(CHUNKS)):
    assert tok0 // CHUNK_SEG_TOKS == (tok0 + count - 1) // CHUNK_SEG_TOKS
    return jnp.concatenate(
        [ref[pl.ds(_chunk_row(tok0, c), count, stride=CHUNK_TOK_ROWS), :] for c in chunks], axis=1)


def _store_packed_rows(x_bf, rows_ref):
    n = x_bf.shape[0]
    bits = pltpu.bitcast(x_bf.astype(F32), jnp.uint32)
    packed = bits[:, D_MODEL // 2:] | (bits[:, :D_MODEL // 2] >> 16)
    for c in range(ROW_SUBLANES):
        rows_ref[pl.ds(c, n, stride=ROW_SUBLANES), :] = packed[:, c * LANES:(c + 1) * LANES]


def _load_packed_rows(rows_ref, x_bf_ref):
    n = x_bf_ref.shape[0]
    for c in range(ROW_SUBLANES):
        words = rows_ref[pl.ds(c, n, stride=ROW_SUBLANES), :]
        low = pltpu.bitcast(words << 16, F32).astype(BF16)
        high = pltpu.bitcast(words & jnp.uint32(0xFFFF0000), F32).astype(BF16)
        x_bf_ref[:, c * LANES:(c + 1) * LANES] = low
        x_bf_ref[:, D_MODEL // 2 + c * LANES:D_MODEL // 2 + (c + 1) * LANES] = high


def _tok_operands(x, tile_of=lambda i: i):
    if isinstance(x, tuple):
        xp, xs = x
        d = xp.shape[1]
        last_p = xp.shape[0] // TOK_TILE - 1
        specs = [pl.BlockSpec((TOK_TILE, d), lambda i, *_: (jnp.minimum(tile_of(i), last_p), 0)),
                 pl.BlockSpec((TOK_TILE, d), lambda i, *_: (0, 0))]
        return [xp, xs], specs, last_p + 2
    return ([x], [pl.BlockSpec((TOK_TILE, x.shape[1]), lambda i, *_: (tile_of(i), 0))],
            x.shape[0] // TOK_TILE)


def _tok_load(refs, is_sample=None):
    if len(refs) == 1:
        return refs[0][...]
    if is_sample is None:
        is_sample = pl.program_id(0) == pl.num_programs(0) - 1
    return jnp.where(is_sample, refs[1][...], refs[0][...])


def _linear_kernel(*refs, n_x, act, scale, chunk_rows):
    x_refs, (w_ref, b_ref, o_ref, wbf_ref) = refs[:n_x], refs[n_x:]

    @pl.when(pl.program_id(0) == 0)
    def _():
        _cast_rows(w_ref, wbf_ref)

    y = jnp.dot(_tok_load(x_refs).astype(BF16), wbf_ref[...], preferred_element_type=F32)
    y = y + b_ref[...]
    if act == "gelu":
        y = jax.nn.gelu(y)
    if scale != 1.0:
        y = y * scale
    if chunk_rows:
        _chunk_rows_store(o_ref, y)
    else:
        o_ref[...] = y.astype(o_ref.dtype)


def _linear(x, w, w_index, b, out_dtype, *, name, act=None, scale=1.0, chunk_rows=False):
    arrays, specs, nt = _tok_operands(x)
    k, nout = w.shape[-2:]
    w_block = (None,) * len(w_index) + (k, nout)
    if chunk_rows:
        assert nout == D_MODEL
        out_block, out_rows, out_cols = (CHUNK_TILE_ROWS, LANES), nt * CHUNK_TILE_ROWS, LANES
    else:
        out_block, out_rows, out_cols = (TOK_TILE, nout), nt * TOK_TILE, nout
    return pl.pallas_call(
        functools.partial(_linear_kernel, n_x=len(arrays), act=act, scale=scale,
                          chunk_rows=chunk_rows),
        grid=(nt,),
        in_specs=specs + [
            pl.BlockSpec(w_block, lambda i: w_index + (0, 0), pipeline_mode=pl.Buffered(1)),
            pl.BlockSpec((1, nout), lambda i: (0, 0)),
        ],
        out_specs=pl.BlockSpec(out_block, lambda i: (i, 0)),
        out_shape=jax.ShapeDtypeStruct((out_rows, out_cols), out_dtype),
        scratch_shapes=[pltpu.VMEM((k, nout), BF16)],
        compiler_params=_params(("arbitrary",), 48),
        name=name,
    )(*arrays, w, b)


def _route(logits_t, b_router):
    aff = jax.nn.sigmoid(logits_t)
    sel = aff + b_router
    srow = [sel[e:e + 1, :] for e in range(N_EXPERTS)]
    arow = [aff[e:e + 1, :] for e in range(N_EXPERTS)]

    def top2_sum(v):
        pairs = [v[i] + v[j] for i in range(4) for j in range(i + 1, 4)]
        return functools.reduce(jnp.maximum, pairs)

    scores = [top2_sum(srow[4 * g:4 * g + 4]) for g in range(N_GROUPS)]
    best = scores[0]
    gi = jnp.zeros_like(best, dtype=jnp.int32)
    for g in range(1, N_GROUPS):
        upd = scores[g] > best
        best = jnp.where(upd, scores[g], best)
        gi = jnp.where(upd, g, gi)

    def pick_group(rows, j):
        out = rows[j]
        for g in range(1, N_GROUPS):
            out = jnp.where(gi == g, rows[4 * g + j], out)
        return out

    v = [pick_group(srow, j) for j in range(EXPERTS_PER_GROUP)]
    a = [pick_group(arow, j) for j in range(EXPERTS_PER_GROUP)]

    m1, i1 = v[0], jnp.zeros_like(gi)
    for j in range(1, EXPERTS_PER_GROUP):
        upd = v[j] > m1
        m1 = jnp.where(upd, v[j], m1)
        i1 = jnp.where(upd, j, i1)
    m2 = jnp.full_like(m1, NEG_INF)
    i2 = jnp.zeros_like(gi)
    for j in range(EXPERTS_PER_GROUP):
        cand = jnp.where(i1 == j, NEG_INF, v[j])
        upd = cand > m2
        m2 = jnp.where(upd, cand, m2)
        i2 = jnp.where(upd, j, i2)

    def pick_idx(rows, idx):
        out = rows[0]
        for j in range(1, EXPERTS_PER_GROUP):
            out = jnp.where(idx == j, rows[j], out)
        return out

    a1 = pick_idx(a, i1)
    a2 = pick_idx(a, i2)
    tot = a1 + a2
    e_idx = jnp.concatenate([gi * EXPERTS_PER_GROUP + i1, gi * EXPERTS_PER_GROUP + i2], axis=0)
    gates = jnp.concatenate([a1 / tot, a2 / tot], axis=0)
    return e_idx, gates


def _proj_ln_kernel(*refs, n_m, n_res):
    m_refs = refs[:n_m]
    w_ref, b_ref = refs[n_m:n_m + 2]
    res_refs = refs[n_m + 2:n_m + 2 + n_res]
    (g_ref, beta_ref, wr_ref, br_ref, x_ref, xrow_ref, e_ref, gate_ref,
     wbf_ref, ya, yb) = refs[n_m + 2 + n_res:]
    i = pl.program_id(0)
    n_tiles = pl.num_programs(0) - 1

    @pl.when(i == 0)
    def _():
        _cast_rows(w_ref, wbf_ref)
        yb[...] = jnp.zeros_like(yb)

    for parity, (cur, prev) in enumerate(((ya, yb), (yb, ya))):
        @pl.when(i % 2 == parity)
        def _():
            cur[...] = jnp.dot(_tok_load(m_refs, i >= n_tiles - 1), wbf_ref[...],
                               preferred_element_type=F32)
            y = prev[...] + b_ref[...]
            x = _layer_norm(ALPHA * _tok_load(res_refs, i == n_tiles) + y, g_ref[...], beta_ref[...])
            x_ref[...] = x
            x_bf = x.astype(BF16)
            _store_packed_rows(x_bf, xrow_ref)
            logits_t = lax.dot_general(wr_ref[...].astype(BF16), x_bf,
                                       (((1,), (1,)), ((), ())), preferred_element_type=F32)
            e_idx, gates = _route(logits_t, br_ref[...])
            e_ref[...] = e_idx
            gate_ref[...] = gates


def _proj_ln(m, w, w_index, b, res, g, beta, wr_t, br, *, name):
    nt = _tok_operands(m)[2]
    m_arrays, m_specs, _ = _tok_operands(m, lambda i: jnp.minimum(i, nt - 1))
    res_arrays, res_specs, _ = _tok_operands(res, lambda i: jnp.maximum(i - 1, 0))
    k = w.shape[-2]
    tm = TOK_TILE
    n = nt * tm
    row = lambda i: (jnp.maximum(i - 1, 0), 0)
    const = lambda i: (0, 0)
    x, x_rows, e_idx, gates = pl.pallas_call(
        functools.partial(_proj_ln_kernel, n_m=len(m_arrays), n_res=len(res_arrays)),
        grid=(nt + 1,),
        in_specs=m_specs + [
            pl.BlockSpec((None,) * len(w_index) + (k, D_MODEL), lambda i: w_index + (0, 0),
                         pipeline_mode=pl.Buffered(1)),
            pl.BlockSpec((1, D_MODEL), const),
        ] + res_specs + [
            pl.BlockSpec((1, D_MODEL), const),
            pl.BlockSpec((1, D_MODEL), const),
            pl.BlockSpec((N_EXPERTS, D_MODEL), const),
            pl.BlockSpec((N_EXPERTS, 1), const),
        ],
        out_specs=[
            pl.BlockSpec((tm, D_MODEL), row),
            pl.BlockSpec((tm * ROW_SUBLANES, LANES), row),
            pl.BlockSpec((None, 2, tm), lambda i: (jnp.maximum(i - 1, 0), 0, 0)),
            pl.BlockSpec((None, 2, tm), lambda i: (jnp.maximum(i - 1, 0), 0, 0)),
        ],
        out_shape=[
            jax.ShapeDtypeStruct((n, D_MODEL), F32),
            jax.ShapeDtypeStruct((n * ROW_SUBLANES, LANES), jnp.uint32),
            jax.ShapeDtypeStruct((nt, 2, tm), jnp.int32),
            jax.ShapeDtypeStruct((nt, 2, tm), F32),
        ],
        scratch_shapes=[pltpu.VMEM((k, D_MODEL), BF16),
                        pltpu.VMEM((tm, D_MODEL), F32), pltpu.VMEM((tm, D_MODEL), F32)],
        compiler_params=_params(("arbitrary",), 52),
        name=name,
    )(*m_arrays, w, b, *res_arrays, g, beta, wr_t, br)
    e_idx = e_idx.transpose(1, 0, 2).reshape(2, n)
    gates = gates.transpose(1, 0, 2).reshape(2, n)
    return x, x_rows, e_idx, gates


def _plan_kernel(e_ref, pos_ref, meta_ref, rank_ref):
    nrow = e_ref.shape[0]
    ri = lax.broadcasted_iota(jnp.int32, (LANES, LANES), 0)
    ci = lax.broadcasted_iota(jnp.int32, (LANES, LANES), 1)
    tri = jnp.where(ri <= ci, 1.0, 0.0).astype(BF16)
    sub = lax.broadcasted_iota(jnp.int32, (N_EXPERTS, LANES), 0)

    def count_body(b, base):
        rows = [b * PLAN_UNROLL + u for u in range(PLAN_UNROLL)]
        onehots = [sub == e_ref[pl.ds(r, 1), :] for r in rows]
        locs = [jnp.dot(jnp.where(oh, 1.0, 0.0).astype(BF16), tri, preferred_element_type=F32)
                for oh in onehots]
        for r, onehot, loc in zip(rows, onehots, locs):
            rank_ref[pl.ds(r, 1), :] = jnp.sum(jnp.where(onehot, base + loc - 1.0, 0.0),
                                               axis=0, keepdims=True)
            base = base + jnp.broadcast_to(loc[:, LANES - 1:LANES], (N_EXPERTS, LANES))
        return base

    count = lax.fori_loop(0, nrow // PLAN_UNROLL, count_body, jnp.zeros((N_EXPERTS, LANES), F32))
    ntile = jnp.floor((count + (MOE_TILE - 1.0)) * (1.0 / MOE_TILE))
    offs = []
    acc = jnp.zeros((1, LANES), F32)
    for e in range(N_EXPERTS):
        offs.append(acc)
        acc = acc + ntile[e:e + 1, :]
    tile_off = jnp.concatenate(offs, axis=0)
    tile_end = tile_off + ntile
    lane = lax.broadcasted_iota(jnp.int32, (N_EXPERTS, LANES), 1).astype(F32)
    tile_expert = jnp.sum(jnp.where(tile_end <= lane, 1.0, 0.0), axis=0, keepdims=True)
    tile_expert = jnp.minimum(tile_expert, N_EXPERTS - 1.0)
    own = jnp.logical_and(tile_off <= lane, lane < tile_end)
    run_end = jnp.sum(jnp.where(own, tile_end, 0.0), axis=0, keepdims=True)
    next_expert = jnp.sum(jnp.where(tile_end <= run_end, 1.0, 0.0), axis=0, keepdims=True)
    has_next = jnp.logical_and(lane[0:1, :] < acc, run_end < acc)
    next_expert = jnp.where(has_next, next_expert, -1.0)
    meta = jnp.concatenate([tile_expert, acc, next_expert, jnp.zeros((5, LANES), F32)], axis=0)
    meta_ref[...] = meta.astype(jnp.int32)
    row_off = tile_off * float(MOE_TILE)

    def pos_body(r, c):
        onehot = sub == e_ref[pl.ds(r, 1), :]
        p = jnp.sum(jnp.where(onehot, row_off, 0.0), axis=0, keepdims=True) + rank_ref[pl.ds(r, 1), :]
        pos_ref[pl.ds(r, 1), :] = p.astype(jnp.int32)
        return c

    lax.fori_loop(0, nrow, pos_body, 0)


def _plan(e_idx, *, name):
    n2 = e_idx.shape[0] * e_idx.shape[1]
    assert n2 % (LANES * PLAN_UNROLL) == 0
    e2d = e_idx.reshape(n2 // LANES, LANES)
    pos, meta = pl.pallas_call(
        _plan_kernel,
        out_shape=[jax.ShapeDtypeStruct(e2d.shape, jnp.int32),
                   jax.ShapeDtypeStruct((8, LANES), jnp.int32)],
        scratch_shapes=[pltpu.VMEM(e2d.shape, F32)],
        name=name,
    )(e2d)
    return pos.reshape(n2), meta[0], meta[1, :1], meta[2]


def _invert_kernel(pos_ref, pair_ref):
    n_rows = pair_ref.shape[0]
    n_pairs = pos_ref.shape[0]

    def fill_body(b, c):
        for u in range(SCALAR_UNROLL):
            pair_ref[b * SCALAR_UNROLL + u] = -1
        return c

    def pair_body(b, c):
        rows = [pos_ref[b * SCALAR_UNROLL + u] for u in range(SCALAR_UNROLL)]
        for u in range(SCALAR_UNROLL):
            pair_ref[rows[u]] = b * SCALAR_UNROLL + u
        return c

    lax.fori_loop(0, n_rows // SCALAR_UNROLL, fill_body, 0)
    lax.fori_loop(0, n_pairs // SCALAR_UNROLL, pair_body, 0)


def _invert(pos, n_rows, *, name):
    return pl.pallas_call(
        _invert_kernel,
        grid_spec=pltpu.PrefetchScalarGridSpec(
            num_scalar_prefetch=1,
            grid=(1,),
            in_specs=[],
            out_specs=pl.BlockSpec(memory_space=pltpu.SMEM),
        ),
        out_shape=jax.ShapeDtypeStruct((n_rows,), jnp.int32),
        name=name,
    )(pos)


def _expert_changed(te_ref, i):
    return jnp.logical_or(i == 0, te_ref[i] != te_ref[jnp.maximum(i - 1, 0)])


def _moe_up_kernel(pair_ref, te_ref, nu_ref, nxt_ref, x_hbm, wg_hbm, wu_hbm, h_ref,
                   wg_cur, wg_next, wg_stage, wu_cur, wu_next, wu_stage, x_rows, x_bf,
                   sem, wsems, st, *, n_tok, n_tiles, layer):
    i = pl.program_id(0)
    nu = nu_ref[0]
    weights = _ExpertWeights([(wg_hbm, wg_stage, wg_next, wg_cur), (wu_hbm, wu_stage, wu_next, wu_cur)],
                             layer, st, wsems)

    def row_copy(tile, r):
        p = pair_ref[tile * MOE_TILE + r]
        tok = jnp.where(p >= n_tok, p - n_tok, jnp.maximum(p, 0))
        src = pl.ds(pl.multiple_of(tok * ROW_SUBLANES, ROW_SUBLANES), ROW_SUBLANES)
        dst = pl.ds(r * ROW_SUBLANES, ROW_SUBLANES)
        return pltpu.make_async_copy(x_hbm.at[src, :], x_rows.at[dst, :], sem)

    def wait_tile():
        pltpu.make_async_copy(x_hbm.at[pl.ds(0, MOE_TILE * ROW_SUBLANES), :], x_rows, sem).wait()

    @pl.when(i == 0)
    def _():
        def body(rb, c):
            for u in range(DMA_UNROLL):
                row_copy(0, rb * DMA_UNROLL + u).start()
            return c
        lax.fori_loop(0, MOE_TILE // DMA_UNROLL, body, 0)
        weights.reset()

    @pl.when(i <= nu)
    def _():
        wait_tile()

    running = i < nu

    @pl.when(jnp.logical_and(running, _expert_changed(te_ref, i)))
    def _():
        weights.switch_to(te_ref[i])

    active, done = weights.begin_step(jnp.where(running, nxt_ref[i], -1))

    @pl.when(running)
    def _():
        _load_packed_rows(x_rows, x_bf)
        next_tile = jnp.minimum(i + 1, n_tiles - 1)
        for r in range(MOE_TILE):
            row_copy(next_tile, r).start(priority=r % 2)
        weights.convert_step(active, done)
        x = x_bf[...]
        a = jnp.dot(x, wg_cur[...], preferred_element_type=F32)
        b = jnp.dot(x, wu_cur[...], preferred_element_type=F32)
        h_ref[...] = (jax.nn.silu(a) * b).astype(BF16)

    weights.end_step(active, done)

    @pl.when(jnp.logical_and(i == n_tiles - 1, i < nu))
    def _():
        wait_tile()

    @pl.when(i >= nu)
    def _():
        h_ref[...] = jnp.zeros_like(h_ref)


class _ExpertWeights:
    def __init__(self, mats, layer, st, sems):
        self.mats, self.layer, self.st, self.sems = mats, layer, st, sems

    def _copies(self, expert, c):
        out = []
        for w_hbm, stage, _, _ in self.mats:
            rows = stage.shape[1]
            src = w_hbm.at[self.layer, expert, pl.ds(pl.multiple_of(c * rows, rows), rows), :]
            out.append(pltpu.make_async_copy(src, stage.at[c % 2], self.sems.at[c % 2]))
        return out

    def _start(self, expert, c):
        for cp in self._copies(expert, c):
            cp.start()
        self.st[2] = c + 1

    def _convert(self, c_src, c_dst):
        for _, stage, w_next, _ in self.mats:
            rows = stage.shape[1]
            dst = pl.ds(pl.multiple_of(c_dst * rows, rows), rows)
            w_next[dst, :] = stage[c_src % 2].astype(BF16)

    def reset(self):
        for _, stage, _, _ in self.mats:
            stage[...] = jnp.zeros_like(stage)
        self.st[1] = 0
        self.st[2] = 0

    def switch_to(self, expert):
        st = self.st

        def body(c, carry):
            @pl.when(c >= st[2])
            def _():
                self._start(expert, c)

            @pl.when(jnp.logical_and(c + 1 < W_CHUNKS, c + 1 >= st[2]))
            def _():
                self._start(expert, c + 1)

            for cp in self._copies(expert, c):
                cp.wait()
            self._convert(c, c)
            return carry

        lax.fori_loop(st[1], W_CHUNKS, body, 0)
        for _, _, w_next, w_cur in self.mats:
            _copy_rows(w_next, w_cur)
        st[1] = 0
        st[2] = 0

    def begin_step(self, next_expert):
        st = self.st
        done, issued = st[1], st[2]
        has_next = next_expert >= 0
        active = jnp.logical_and(has_next, done < issued)

        @pl.when(jnp.logical_and(has_next, jnp.logical_and(issued < W_CHUNKS, issued < done + 2)))
        def _():
            self._start(next_expert, issued)

        @pl.when(active)
        def _():
            for cp in self._copies(next_expert, done):
                cp.wait()

        return active, done

    def convert_step(self, active, done):
        self._convert(jnp.where(active, done, done + 1), jnp.where(active, done, W_CHUNKS))

    def end_step(self, active, done):
        @pl.when(active)
        def _():
            self.st[1] = done + 1


def _copy_rows(src_ref, dst_ref):
    n = dst_ref.shape[0] // CAST_ROWS

    def body(i, c):
        r = pl.multiple_of(i * CAST_ROWS, CAST_ROWS)
        dst_ref[pl.ds(r, CAST_ROWS), :] = src_ref[pl.ds(r, CAST_ROWS), :]
        return c

    lax.fori_loop(0, n, body, 0)


def _moe_down_kernel(pair_ref, te_ref, nu_ref, nxt_ref, h_ref, wd_hbm, out_hbm,
                     w_cur, w_next, stage, ya, yb, sems, tsem, wsems, st, *, n_tok, n_tiles, layer):
    i = pl.program_id(0)
    nu = nu_ref[0]
    bufs = (ya, yb)
    trash = 2 * n_tok
    weights = _ExpertWeights([(wd_hbm, stage, w_next, w_cur)], layer, st, wsems)

    def row_copy(tile, r, buf, sem):
        p = pair_ref[tile * MOE_TILE + r]
        dst = jnp.where(p < 0, trash + r, p)
        return pltpu.make_async_copy(buf.at[pl.ds(r, 1), :], out_hbm.at[pl.ds(dst, 1), :], sem)

    def wait_tile(buf, sem):
        pltpu.make_async_copy(buf, out_hbm.at[pl.ds(0, MOE_TILE), :], sem).wait()

    @pl.when(i == 0)
    def _():
        yb[...] = jnp.zeros_like(yb)
        fill = pltpu.make_async_copy(yb, out_hbm.at[pl.ds(trash, MOE_TILE), :], tsem)
        fill.start()
        fill.wait()
        weights.reset()

    running = i < nu

    @pl.when(jnp.logical_and(running, _expert_changed(te_ref, i)))
    def _():
        weights.switch_to(te_ref[i])

    active, done = weights.begin_step(jnp.where(running, nxt_ref[i], -1))

    for parity in range(2):
        cur, prev = bufs[parity], bufs[1 - parity]
        cur_sem, prev_sem = sems.at[parity], sems.at[1 - parity]
        mine = i % 2 == parity

        @pl.when(jnp.logical_and(mine, jnp.logical_and(i >= 1, i - 1 <= nu)))
        def _():
            wait_tile(cur, cur_sem)

        @pl.when(jnp.logical_and(mine, running))
        def _():
            prev_tile = jnp.maximum(i - 1, 0)
            for r in range(MOE_TILE):
                row_copy(prev_tile, r, prev, prev_sem).start(priority=r % 2)
            weights.convert_step(active, done)
            cur[...] = jnp.dot(h_ref[...], w_cur[...], preferred_element_type=F32)

        @pl.when(jnp.logical_and(mine, i == nu))
        def _():
            def body(rb, c):
                for u in range(DMA_UNROLL):
                    row_copy(i - 1, rb * DMA_UNROLL + u, prev, prev_sem).start()
                return c
            lax.fori_loop(0, MOE_TILE // DMA_UNROLL, body, 0)

            @pl.when(i == n_tiles)
            def _():
                wait_tile(prev, prev_sem)

    weights.end_step(active, done)


def _moe_ffn(x, pair, tile_expert, n_used, next_expert, w_gate, w_up, w_down, layer):
    n_tok = x.shape[0] // ROW_SUBLANES
    n_tiles = pair.shape[0] // MOE_TILE

    def tile(i, pr, te, nu):
        return (jnp.minimum(i, nu[0] - 1), 0)

    hbm = pl.BlockSpec(memory_space=pl.ANY)
    row_bufs = [pltpu.VMEM((MOE_TILE, D_MODEL), F32), pltpu.VMEM((MOE_TILE, D_MODEL), F32),
                pltpu.SemaphoreType.DMA((2,))]
    up_chunk = D_MODEL // W_CHUNKS
    up_weight = [pltpu.VMEM((D_MODEL, D_EXPERT), BF16),
                 pltpu.VMEM((D_MODEL + up_chunk, D_EXPERT), BF16),
                 pltpu.VMEM((2, up_chunk, D_EXPERT), F32)]
    h = pl.pallas_call(
        functools.partial(_moe_up_kernel, n_tok=n_tok, n_tiles=n_tiles, layer=layer),
        grid_spec=pltpu.PrefetchScalarGridSpec(
            num_scalar_prefetch=4,
            grid=(n_tiles,),
            in_specs=[hbm, hbm, hbm],
            out_specs=pl.BlockSpec((MOE_TILE, D_EXPERT), lambda i, pr, te, nu, nx: (i, 0)),
            scratch_shapes=up_weight + up_weight + [
                pltpu.VMEM((MOE_TILE * ROW_SUBLANES, LANES), jnp.uint32),
                pltpu.VMEM((MOE_TILE, D_MODEL), BF16),
                pltpu.SemaphoreType.DMA(()), pltpu.SemaphoreType.DMA((2,)),
                pltpu.SMEM((4,), jnp.int32)],
        ),
        out_shape=jax.ShapeDtypeStruct((n_tiles * MOE_TILE, D_EXPERT), BF16),
        compiler_params=_params(("arbitrary",), 48),
        name=f"moe_up_{layer}",
    )(pair, tile_expert, n_used, next_expert, x, w_gate, w_up)
    down_chunk = D_EXPERT // W_CHUNKS
    return pl.pallas_call(
        functools.partial(_moe_down_kernel, n_tok=n_tok, n_tiles=n_tiles, layer=layer),
        grid_spec=pltpu.PrefetchScalarGridSpec(
            num_scalar_prefetch=4,
            grid=(n_tiles + 1,),
            in_specs=[pl.BlockSpec((MOE_TILE, D_EXPERT), lambda i, pr, te, nu, nx: tile(i, pr, te, nu)),
                      pl.BlockSpec(memory_space=pl.ANY)],
            out_specs=pl.BlockSpec(memory_space=pl.ANY),
            scratch_shapes=[pltpu.VMEM((D_EXPERT, D_MODEL), BF16),
                            pltpu.VMEM((D_EXPERT + down_chunk, D_MODEL), BF16),
                            pltpu.VMEM((2, down_chunk, D_MODEL), F32)] + row_bufs
            + [pltpu.SemaphoreType.DMA(()), pltpu.SemaphoreType.DMA((2,)),
               pltpu.SMEM((4,), jnp.int32)],
        ),
        out_shape=jax.ShapeDtypeStruct((2 * n_tok + MOE_TILE, D_MODEL), F32),
        compiler_params=_params(("arbitrary",), 40),
        name=f"moe_down_{layer}",
    )(pair, tile_expert, n_used, next_expert, h, w_down)


def _combine(y0_ref, y1_ref, res_ref, gate_ref, g_ref, beta_ref):
    gate = gate_ref[...]
    ffn = gate[:, 0:1] * y0_ref[...] + gate[:, 1:2] * y1_ref[...]
    return _layer_norm(ALPHA * res_ref[...] + ffn, g_ref[...], beta_ref[...])


def _combine_specs(n):
    nt = n // TOK_TILE
    row = lambda i: (i, 0)
    const = lambda i: (0, 0)
    return [pl.BlockSpec((TOK_TILE, D_MODEL), row),
            pl.BlockSpec((TOK_TILE, D_MODEL), lambda i: (i + nt, 0)),
            pl.BlockSpec((TOK_TILE, D_MODEL), row),
            pl.BlockSpec((TOK_TILE, 2), row),
            pl.BlockSpec((1, D_MODEL), const),
            pl.BlockSpec((1, D_MODEL), const)]


def _combine_split_kernel(y0_ref, y1_ref, res_ref, gate_ref, g_ref, beta_ref, prompt_ref, sample_ref):
    x = _combine(y0_ref, y1_ref, res_ref, gate_ref, g_ref, beta_ref)
    is_sample = pl.program_id(0) == pl.num_programs(0) - 1

    @pl.when(jnp.logical_not(is_sample))
    def _():
        prompt_ref[...] = x

    @pl.when(is_sample)
    def _():
        sample_ref[...] = x


def _combine_split(ys, res, gates_col, g, beta, *, name):
    n = res.shape[0]
    tm = TOK_TILE
    nt = n // tm
    return pl.pallas_call(
        _combine_split_kernel,
        grid=(nt,),
        in_specs=_combine_specs(n),
        out_specs=[pl.BlockSpec((tm, D_MODEL), lambda i: (jnp.minimum(i, nt - 2), 0)),
                   pl.BlockSpec((tm, D_MODEL), lambda i: (0, 0))],
        out_shape=[jax.ShapeDtypeStruct((n - tm, D_MODEL), F32),
                   jax.ShapeDtypeStruct((tm, D_MODEL), F32)],
        compiler_params=_params(("arbitrary",), 40),
        name=name,
    )(ys, ys, res, gates_col, g, beta)


def _load_weight(w_hbm, wbf_ref, stage_ref, sems):
    rows = stage_ref.shape[1]
    n_chunks = wbf_ref.shape[0] // rows

    def chunk_copy(c):
        return pltpu.make_async_copy(w_hbm.at[pl.ds(c * rows, rows), :], stage_ref.at[c % 2],
                                     sems.at[c % 2])

    chunk_copy(0).start()
    for c in range(n_chunks):
        if c + 1 < n_chunks:
            chunk_copy(c + 1).start()
        chunk_copy(c).wait()
        wbf_ref[c * rows:(c + 1) * rows, :] = stage_ref[c % 2].astype(BF16)


def _combine_qkv_kernel(y0_ref, y1_ref, res_ref, gate_ref, g_ref, beta_ref, wq_hbm, wkv_hbm,
                        x_ref, q_ref, kv_ref, wq_bf, wkv_bf, stage_q, stage_kv, sems, *, wq_index):
    @pl.when(pl.program_id(0) == 0)
    def _():
        wq = wq_hbm
        for k in wq_index:
            wq = wq.at[k]
        _load_weight(wq, wq_bf, stage_q, sems)
        _load_weight(wkv_hbm, wkv_bf, stage_kv, sems)

    x = _combine(y0_ref, y1_ref, res_ref, gate_ref, g_ref, beta_ref)
    x_ref[...] = x
    x_bf = x.astype(BF16)
    q = jnp.dot(x_bf, wq_bf[...], preferred_element_type=F32) * (HEAD_DIM ** -0.5)
    q_ref[...] = q.astype(BF16)
    kv_ref[...] = jnp.dot(x_bf, wkv_bf[...], preferred_element_type=F32)


def _combine_qkv(ys, res, gates_col, g, beta, w_q, wq_index, w_kv, *, name):
    n = res.shape[0]
    tm = TOK_TILE
    row = lambda i: (i, 0)
    hbm = pl.BlockSpec(memory_space=pl.ANY)
    return pl.pallas_call(
        functools.partial(_combine_qkv_kernel, wq_index=wq_index),
        grid=(n // tm,),
        in_specs=_combine_specs(n) + [hbm, hbm],
        out_specs=[pl.BlockSpec((tm, D_MODEL), row),
                   pl.BlockSpec((tm, D_MODEL), row),
                   pl.BlockSpec((tm, 2 * KV_DIM), row)],
        out_shape=[jax.ShapeDtypeStruct((n, D_MODEL), F32),
                   jax.ShapeDtypeStruct((n, D_MODEL), BF16),
                   jax.ShapeDtypeStruct((n, 2 * KV_DIM), F32)],
        scratch_shapes=[pltpu.VMEM((D_MODEL, D_MODEL), BF16),
                        pltpu.VMEM((D_MODEL, 2 * KV_DIM), BF16),
                        pltpu.VMEM((2, CAST_ROWS, D_MODEL), F32),
                        pltpu.VMEM((2, CAST_ROWS, 2 * KV_DIM), F32),
                        pltpu.SemaphoreType.DMA((2,))],
        compiler_params=_params(("arbitrary",), 52),
        name=name,
    )(ys, ys, res, gates_col, g, beta, w_q, w_kv)


def _moe_block(x_rows, e_idx, w_gate, w_up, w_down, layer):
    n = x_rows.shape[0] // ROW_SUBLANES
    n_tiles = -(-(2 * n + N_EXPERTS * (MOE_TILE - 1)) // MOE_TILE)
    pos, tile_expert, n_used, next_expert = _plan(e_idx, name=f"moe_plan_{layer}")
    pair = _invert(pos, n_tiles * MOE_TILE, name=f"moe_invert_{layer}")
    return _moe_ffn(x_rows, pair, tile_expert, n_used, next_expert, w_gate, w_up, w_down, layer)


def _sigmoid(x):
    return 0.5 * jnp.tanh(0.5 * x) + 0.5


def _log_sigmoid(x):
    return -(jnp.maximum(-x, 0.0) + jnp.log1p(jnp.exp(-jnp.abs(x))))


def _lru_gate_blocks(xcs, blocks, wrg_bf, wig_bf, brg_ref, big_ref, lam_ref):
    xbs = [xc.astype(BF16) for xc in xcs]
    r_lin = [jnp.dot(xb, wrg_bf[n], preferred_element_type=F32) for xb, n in zip(xbs, blocks)]
    i_lin = [jnp.dot(xb, wig_bf[n], preferred_element_type=F32) for xb, n in zip(xbs, blocks)]
    out = []
    for xc, n, rl, il in zip(xcs, blocks, r_lin, i_lin):
        cols = slice(n * LRU_BLOCK, (n + 1) * LRU_BLOCK)
        r = _sigmoid(rl + brg_ref[:, cols])
        i = _sigmoid(il + big_ref[:, cols])
        log_a = LRU_C * r * _log_sigmoid(lam_ref[:, cols])
        a = jnp.exp(log_a)
        u = xc * i * jnp.sqrt(-jnp.tanh(log_a) * (a * a + 1.0))
        out.append((a, u))
    return out


def _lru_gate_block(xc, n, wrg_bf, wig_bf, brg_ref, big_ref, lam_ref):
    return _lru_gate_blocks([xc], [n], wrg_bf, wig_bf, brg_ref, big_ref, lam_ref)[0]


def _cast_gate_weights(wrg_ref, wig_ref, wrg_bf, wig_bf):
    for n in range(LRU_BLOCKS):
        wrg_bf[n] = wrg_ref[n].astype(BF16)
        wig_bf[n] = wig_ref[n].astype(BF16)


def _lru_prompt_kernel(xb_ref, yb_ref, cw_ref, cb_ref, wrg_ref, wig_ref, brg_ref, big_ref, lam_ref,
                       m_ref, conv_ref, hlast_ref, xs, tail, a_s, u_s, hs_t, h_s, wrg_bf, wig_bf):
    b = pl.program_id(0)
    j = pl.program_id(1)
    tt = m_ref.shape[0]
    seg_len = tt // SEGS
    taps = CONV_WIDTH - 1
    head = SEGS * taps

    @pl.when(jnp.logical_and(b == 0, j == 0))
    def _():
        _cast_gate_weights(wrg_ref, wig_ref, wrg_bf, wig_bf)

    @pl.when(j == 0)
    def _():
        tail[...] = jnp.zeros_like(tail)
        h_s[...] = jnp.zeros_like(h_s)

    for q in range(seg_len):
        xs[head + SEGS * q:head + SEGS * (q + 1), :] = jnp.concatenate(
            [xb_ref[pl.ds(_chunk_row(q, c), SEGS, stride=CHUNK_SEG_ROWS), :] for c in range(CHUNKS)],
            axis=1)
    sub = lax.broadcasted_iota(jnp.int32, (SEGS, D_MODEL), 0)
    for k in range(taps):
        last = head + SEGS * (seg_len - taps + k)
        joined = jnp.where(sub == SEGS - 1, tail[SEGS * k:SEGS * (k + 1), :], xs[last:last + SEGS, :])
        xs[SEGS * k:SEGS * (k + 1), :] = pltpu.roll(joined, 1, axis=0)
    tail[...] = xs[head + SEGS * (seg_len - taps):head + SEGS * seg_len, :]

    for first_block in range(0, LRU_BLOCKS, GATE_BLOCKS):
        blocks = range(first_block, first_block + GATE_BLOCKS)
        xcs = []
        for n in blocks:
            cols = slice(n * LRU_BLOCK, (n + 1) * LRU_BLOCK)
            xc = cb_ref[:, cols] + cw_ref[0:1, cols] * xs[0:tt, cols]
            for k in range(1, CONV_WIDTH):
                xc = xc + cw_ref[k:k + 1, cols] * xs[SEGS * k:SEGS * k + tt, cols]
            xcs.append(xc)
        gates = _lru_gate_blocks(xcs, blocks, wrg_bf, wig_bf, brg_ref, big_ref, lam_ref)
        for n, (a, u) in zip(blocks, gates):
            cols = slice(n * LRU_BLOCK, (n + 1) * LRU_BLOCK)
            a_s[:, cols] = a
            u_s[:, cols] = u

    def scan_body(q, carry):
        h, prod = carry
        rows = pl.ds(pl.multiple_of(q * SEGS, SEGS), SEGS)
        a = a_s[rows, :]
        h = a * h + u_s[rows, :]
        prod = a * prod
        u_s[rows, :] = h
        a_s[rows, :] = prod
        return h, prod

    h_end, prod_end = lax.fori_loop(
        0, seg_len, scan_body,
        (jnp.zeros((SEGS, D_MODEL), F32), jnp.ones((SEGS, D_MODEL), F32)))
    state = h_s[...]
    entering = []
    for s in range(SEGS):
        entering.append(state)
        state = h_end[s:s + 1, :] + prod_end[s:s + 1, :] * state
    h_s[...] = state
    enter = jnp.concatenate(entering, axis=0)

    def fix_body(q, carry):
        rows = pl.ds(pl.multiple_of(q * SEGS, SEGS), SEGS)
        h = u_s[rows, :] + a_s[rows, :] * enter
        for c in range(CHUNKS):
            hs_t[pl.ds(q * CHUNK_TOK_ROWS + c, SEGS, stride=CHUNK_SEG_ROWS), :] = h[:, c * LANES:(c + 1) * LANES]
        return carry

    lax.fori_loop(0, seg_len, fix_body, 0)
    for s in range(SEGS):
        rows = slice(s * seg_len, (s + 1) * seg_len)
        hs = _chunk_rows_load(hs_t, s * seg_len, seg_len)
        m_ref[rows, :] = (hs * yb_ref[rows, :].astype(F32)).astype(BF16)

    @pl.when(j == pl.num_programs(1) - 1)
    def _():
        for k in range(taps):
            conv_ref[k:k + 1, :] = tail[SEGS * k + SEGS - 1:SEGS * (k + 1), :]
        hlast_ref[...] = state


def _lru_prompt(xb, yb, batch, seq, cw, cb, wrg, wig, brg, big, lam):
    tt = TOK_TILE
    nj = seq // tt
    row = lambda b, j: (b * nj + j, 0)
    const2 = lambda b, j: (0, 0)
    const3 = lambda b, j: (0, 0, 0)
    return pl.pallas_call(
        _lru_prompt_kernel,
        grid=(batch, nj),
        in_specs=[
            pl.BlockSpec((CHUNK_TILE_ROWS, LANES), row),
            pl.BlockSpec((tt, D_MODEL), row),
            pl.BlockSpec((CONV_WIDTH, D_MODEL), const2),
            pl.BlockSpec((1, D_MODEL), const2),
            pl.BlockSpec((LRU_BLOCKS, LRU_BLOCK, LRU_BLOCK), const3),
            pl.BlockSpec((LRU_BLOCKS, LRU_BLOCK, LRU_BLOCK), const3),
            pl.BlockSpec((1, D_MODEL), const2),
            pl.BlockSpec((1, D_MODEL), const2),
            pl.BlockSpec((1, D_MODEL), const2),
        ],
        out_specs=[
            pl.BlockSpec((tt, D_MODEL), row),
            pl.BlockSpec((None, CONV_WIDTH - 1, D_MODEL), lambda b, j: (b, 0, 0)),
            pl.BlockSpec((None, 1, D_MODEL), lambda b, j: (b, 0, 0)),
        ],
        out_shape=[
            jax.ShapeDtypeStruct((batch * seq, D_MODEL), BF16),
            jax.ShapeDtypeStruct((batch, CONV_WIDTH - 1, D_MODEL), F32),
            jax.ShapeDtypeStruct((batch, 1, D_MODEL), F32),
        ],
        scratch_shapes=[
            pltpu.VMEM((tt + SEGS * (CONV_WIDTH - 1), D_MODEL), F32),
            pltpu.VMEM((SEGS * (CONV_WIDTH - 1), D_MODEL), F32),
            pltpu.VMEM((tt, D_MODEL), F32),
            pltpu.VMEM((tt, D_MODEL), F32),
            pltpu.VMEM((CHUNK_TILE_ROWS, LANES), F32),
            pltpu.VMEM((1, D_MODEL), F32),
            pltpu.VMEM((LRU_BLOCKS, LRU_BLOCK, LRU_BLOCK), BF16),
            pltpu.VMEM((LRU_BLOCKS, LRU_BLOCK, LRU_BLOCK), BF16),
        ],
        compiler_params=_params(("arbitrary", "arbitrary"), 40),
        name="lru_prompt",
    )(xb, yb, cw, cb, wrg, wig, brg, big, lam)


def _lru_sample_kernel(xb_ref, yb_ref, cs_ref, h0_ref, cw_ref, cb_ref, wrg_ref, wig_ref,
                       brg_ref, big_ref, lam_ref, m_ref, conv_ref, hlast_ref, wrg_bf, wig_bf, *, steps):
    batch = h0_ref.shape[0]
    _cast_gate_weights(wrg_ref, wig_ref, wrg_bf, wig_bf)
    m_ref[steps * batch:, :] = jnp.zeros((m_ref.shape[0] - steps * batch, D_MODEL), BF16)

    def slab(t, cols):
        if t < CONV_WIDTH - 1:
            return cs_ref[t, :, cols]
        t -= CONV_WIDTH - 1
        first, stop, _ = cols.indices(D_MODEL)
        return _chunk_rows_load(xb_ref, t * batch, batch, range(first // LANES, stop // LANES))

    for n in range(LRU_BLOCKS):
        cols = slice(n * LRU_BLOCK, (n + 1) * LRU_BLOCK)
        h = h0_ref[:, cols]
        for t in range(steps):
            xc = cb_ref[:, cols] + cw_ref[0:1, cols] * slab(t, cols)
            for k in range(1, CONV_WIDTH):
                xc = xc + cw_ref[k:k + 1, cols] * slab(t + k, cols)
            a, u = _lru_gate_block(xc, n, wrg_bf, wig_bf, brg_ref, big_ref, lam_ref)
            h = a * h + u
            rows = slice(t * batch, (t + 1) * batch)
            m_ref[rows, cols] = (h * yb_ref[rows, cols].astype(F32)).astype(BF16)
        hlast_ref[:, cols] = h
    for k in range(CONV_WIDTH - 1):
        conv_ref[k] = slab(steps + k, slice(None))


def _lru_sample(xb, yb, tile, steps, conv_state, h0, cw, cb, wrg, wig, brg, big, lam):
    batch = h0.shape[0]
    tok = pl.BlockSpec((TOK_TILE, D_MODEL), lambda i: (tile, 0))
    tok_chunks = pl.BlockSpec((CHUNK_TILE_ROWS, LANES), lambda i: (tile, 0))
    full = lambda a: pl.BlockSpec(a.shape, lambda i: (0,) * a.ndim)
    small = (conv_state, h0, cw, cb, wrg, wig, brg, big, lam)
    return pl.pallas_call(
        functools.partial(_lru_sample_kernel, steps=steps),
        grid=(1,),
        in_specs=[tok_chunks, tok] + [full(a) for a in small],
        out_specs=[
            pl.BlockSpec((TOK_TILE, D_MODEL), lambda i: (0, 0)),
            pl.BlockSpec((CONV_WIDTH - 1, batch, D_MODEL), lambda i: (0, 0, 0)),
            pl.BlockSpec((batch, D_MODEL), lambda i: (0, 0)),
        ],
        out_shape=[
            jax.ShapeDtypeStruct((TOK_TILE, D_MODEL), BF16),
            jax.ShapeDtypeStruct((CONV_WIDTH - 1, batch, D_MODEL), F32),
            jax.ShapeDtypeStruct((batch, D_MODEL), F32),
        ],
        scratch_shapes=[
            pltpu.VMEM((LRU_BLOCKS, LRU_BLOCK, LRU_BLOCK), BF16),
            pltpu.VMEM((LRU_BLOCKS, LRU_BLOCK, LRU_BLOCK), BF16),
        ],
        compiler_params=_params(("arbitrary",), 32),
        name="lru_sample",
    )(xb, yb, *small)


def _rel_bucket(dist):
    n = jnp.maximum(dist, 0)
    max_exact = N_BUCKETS // 2
    nf = jnp.maximum(n, 1).astype(F32)
    large = max_exact + (jnp.log(nf / max_exact) / math.log(MAX_DISTANCE / max_exact)
                         * (N_BUCKETS - max_exact)).astype(jnp.int32)
    large = jnp.minimum(large, N_BUCKETS - 1)
    return jnp.where(n < max_exact, n, large)


def _masked_buckets(dist):
    valid = (dist >= 0) & (dist < WINDOW)
    return jnp.where(valid, _rel_bucket(dist), -1).astype(jnp.int32)


def _build_bias(bucket, tab_ref, head):
    def body(bi, acc):
        return jnp.where(bucket == bi, tab_ref[bi * N_HEADS + head], acc)
    return lax.fori_loop(0, N_BUCKETS, body, jnp.full(bucket.shape, NEG_INF, F32))


def _softmax_pv(s, sink, v):
    m = jnp.maximum(jnp.max(s, axis=-1, keepdims=True), sink)
    p = jnp.exp(s - m)
    den = jnp.sum(p, axis=-1, keepdims=True) + jnp.exp(sink - m)
    return jnp.dot(p.astype(BF16), v, preferred_element_type=F32) / den


def _attn_prompt_kernel(q_ref, kvp_ref, kvc_ref, bucket_ref, tab_ref, sink_ref, o_ref, bias_s):
    b = pl.program_id(0)
    n = pl.program_id(1)

    @pl.when(jnp.logical_and(b == 0, n == 0))
    def _():
        bucket = bucket_ref[...]

        col = lax.broadcasted_iota(jnp.int32, (WINDOW, 2 * WINDOW), 1)

        def head_body(h, c):
            bias = _build_bias(bucket, tab_ref, h)
            sink = sink_ref[h]
            g = h // GROUP
            r0 = pl.multiple_of((h % GROUP) * WINDOW, WINDOW)
            bias_s[0, g, pl.ds(r0, WINDOW), :] = jnp.where(col == 0, sink, bias)
            bias_s[1, g, pl.ds(r0, WINDOW), :] = jnp.where(
                col == 0, sink, jnp.where(col < WINDOW, NEG_INF, bias))
            return c

        lax.fori_loop(0, N_HEADS, head_body, 0)

    first = (n == 0).astype(jnp.int32)
    row = lax.broadcasted_iota(jnp.int32, kvp_ref.shape, 0)
    kv_prev = jnp.where(row == 0, 0.0, kvp_ref[...])
    kv = jnp.concatenate([kv_prev, kvc_ref[...]], axis=0).astype(BF16)
    ones = jnp.ones((2 * WINDOW, 2 * HEAD_DIM), BF16)
    lane = lax.broadcasted_iota(jnp.int32, (WINDOW, 2 * HEAD_DIM), 1)
    def scores(idx):
        g, pair = divmod(idx, GROUP // 2)
        h0 = g * GROUP + 2 * pair
        kg = kv[:, g * HEAD_DIM:(g + 1) * HEAD_DIM]
        qp = jnp.concatenate([q_ref[:, h * HEAD_DIM:(h + 1) * HEAD_DIM] for h in (h0, h0 + 1)], axis=0)
        s = lax.dot_general(qp, kg, (((1,), (1,)), ((), ())), preferred_element_type=F32)
        return s + bias_s[first, g, 2 * pair * WINDOW:(2 * pair + 2) * WINDOW, :]

    def finish(idx, o_ext):
        h0 = 2 * idx
        o = o_ext[:, :2 * HEAD_DIM] * (1.0 / o_ext[:, 2 * HEAD_DIM:])
        o_ref[:, h0 * HEAD_DIM:(h0 + 2) * HEAD_DIM] = jnp.where(
            lane < HEAD_DIM, o[:WINDOW], o[WINDOW:]).astype(BF16)

    n_pairs = N_HEADS // 2

    def values(idx, p):
        g = idx // (GROUP // 2)
        vg = kv[:, KV_DIM + g * HEAD_DIM:KV_DIM + (g + 1) * HEAD_DIM]
        v_ext = jnp.concatenate([vg, vg, ones], axis=1)
        return jnp.dot(p, v_ext, preferred_element_type=F32)

    for first_pair in range(0, n_pairs, PAIR_BLOCK):
        block = range(first_pair, first_pair + PAIR_BLOCK)
        ss = [scores(idx) for idx in block]
        ms = [jnp.max(s, axis=-1, keepdims=True) for s in ss]
        ps = [jnp.exp(s - m).astype(BF16) for s, m in zip(ss, ms)]
        os_ = [values(idx, p) for idx, p in zip(block, ps)]
        for idx, o_ext in zip(block, os_):
            finish(idx, o_ext)


def _attn_prompt(q, kv, batch, seq, bucket, tab, sinks):
    nb = seq // WINDOW
    smem = pl.BlockSpec(memory_space=pltpu.SMEM)
    return pl.pallas_call(
        _attn_prompt_kernel,
        grid=(batch, nb),
        in_specs=[
            pl.BlockSpec((WINDOW, D_MODEL), lambda b, n: (b * nb + n, 0)),
            pl.BlockSpec((WINDOW, 2 * KV_DIM), lambda b, n: (jnp.maximum(b * nb + n - 1, 0), 0)),
            pl.BlockSpec((WINDOW, 2 * KV_DIM), lambda b, n: (b * nb + n, 0)),
            pl.BlockSpec((WINDOW, 2 * WINDOW), lambda b, n: (0, 0)),
            smem, smem,
        ],
        out_specs=pl.BlockSpec((WINDOW, D_MODEL), lambda b, n: (b * nb + n, 0)),
        out_shape=jax.ShapeDtypeStruct((batch * seq, D_MODEL), BF16),
        scratch_shapes=[pltpu.VMEM((2, N_KV_HEADS, GROUP * WINDOW, 2 * WINDOW), F32)],
        compiler_params=_params(("arbitrary", "arbitrary"), 32),
        name="attn_prompt",
    )(q, kv, kv, bucket, tab, sinks)


def _attn_sample_kernel(q_ref, k_ref, v_ref, bucket_ref, tab_ref, sink_ref, o_ref, bias_s):
    steps = q_ref.shape[0]

    @pl.when(pl.program_id(0) == 0)
    def _():
        bucket = bucket_ref[...]

        def head_body(h, c):
            bias_s[h] = _build_bias(bucket, tab_ref, h)
            return c

        lax.fori_loop(0, N_HEADS, head_body, 0)

    rows = lax.broadcasted_iota(jnp.int32, (GROUP * steps, 1), 0)
    k = k_ref[...].astype(BF16)
    v = v_ref[...].astype(BF16)
    groups = range(N_KV_HEADS)

    def heads_of(g):
        return range(g * GROUP, (g + 1) * GROUP)

    scores, sinks = [], []
    for g in groups:
        qg = jnp.concatenate([q_ref[:, h * HEAD_DIM:(h + 1) * HEAD_DIM] for h in heads_of(g)], axis=0)
        bias = jnp.concatenate([bias_s[h] for h in heads_of(g)], axis=0)
        sink = jnp.full((GROUP * steps, 1), sink_ref[g * GROUP], F32)
        for hh in range(1, GROUP):
            sink = jnp.where(rows >= hh * steps, sink_ref[g * GROUP + hh], sink)
        kg = k[:, g * HEAD_DIM:(g + 1) * HEAD_DIM]
        scores.append(lax.dot_general(qg, kg, (((1,), (1,)), ((), ())),
                                      preferred_element_type=F32) + bias)
        sinks.append(sink)
    outs = [_softmax_pv(scores[g], sinks[g], v[:, g * HEAD_DIM:(g + 1) * HEAD_DIM]) for g in groups]
    for g in groups:
        for hh, h in enumerate(heads_of(g)):
            o_ref[:, h * HEAD_DIM:(h + 1) * HEAD_DIM] = outs[g][hh * steps:(hh + 1) * steps].astype(BF16)


def _attn_sample(q, k_all, v_all, bucket, tab, sinks):
    batch, steps, _ = q.shape
    lk = k_all.shape[1]
    smem = pl.BlockSpec(memory_space=pltpu.SMEM)
    return pl.pallas_call(
        _attn_sample_kernel,
        grid=(batch,),
        in_specs=[
            pl.BlockSpec((None, steps, D_MODEL), lambda b: (b, 0, 0)),
            pl.BlockSpec((None, lk, KV_DIM), lambda b: (b, 0, 0)),
            pl.BlockSpec((None, lk, KV_DIM), lambda b: (b, 0, 0)),
            pl.BlockSpec((steps, lk), lambda b: (0, 0)),
            smem, smem,
        ],
        out_specs=pl.BlockSpec((None, steps, D_MODEL), lambda b: (b, 0, 0)),
        out_shape=jax.ShapeDtypeStruct((batch, steps, D_MODEL), BF16),
        scratch_shapes=[pltpu.VMEM((N_HEADS, steps, lk), F32)],
        compiler_params=_params(("arbitrary",), 32),
        name="attn_sample",
    )(q, k_all, v_all, bucket, tab, sinks)


def kernel(x_prompt, x_sample, state_conv, state_rnn, cache_k_win, cache_v_win, ln_g, ln_b, lru_w_x, lru_b_x, lru_w_y, lru_b_y, lru_conv_w, lru_conv_b, lru_w_rg, lru_b_rg, lru_w_ig, lru_b_ig, lru_lam, lru_w_out, lru_b_out, attn_w_kv, attn_w_q, attn_w_o, attn_sinks, rel_bias, moe_w_router, moe_b_router, moe_w_gate, moe_w_up, moe_w_down):
    bp, seq, _ = x_prompt.shape
    bs, steps, _ = x_sample.shape
    n_p = bp * seq
    n_s = bs * steps

    assert n_p % TOK_TILE == 0 and n_s <= TOK_TILE
    sample_tile = n_p // TOK_TILE

    def pad_tile(rows):
        return jnp.pad(rows, ((0, TOK_TILE - n_s), (0, 0)))

    x0 = (x_prompt.reshape(n_p, D_MODEL),
          pad_tile(x_sample.transpose(1, 0, 2).reshape(n_s, D_MODEL)))
    wr_t = moe_w_router.T
    br = moe_b_router.reshape(N_EXPERTS, 1)
    vec = lambda a: a.reshape(1, -1)

    xb = _linear(x0, lru_w_x, (0,), vec(lru_b_x[0]), F32, name="lru_in_x", chunk_rows=True)
    yb = _linear(x0, lru_w_y, (0,), vec(lru_b_y[0]), BF16, act="gelu", name="lru_in_y")
    lru_args = (lru_conv_w[0], vec(lru_conv_b[0]), lru_w_rg[0], lru_w_ig[0],
                vec(lru_b_rg[0]), vec(lru_b_ig[0]), vec(lru_lam[0]))
    m_p, conv_p, rnn_p = _lru_prompt(xb, yb, bp, seq, *lru_args)
    m_s, conv_s, rnn_s = _lru_sample(xb, yb, sample_tile, steps,
                                     state_conv[0].transpose(1, 0, 2), state_rnn[0], *lru_args)
    x1, x1_rows, e_idx, gates = _proj_ln((m_p, m_s), lru_w_out, (0,), vec(lru_b_out[0]), x0,
                                vec(ln_g[0, 0]), vec(ln_b[0, 0]), wr_t, br, name="lru_out_ln")
    ys = _moe_block(x1_rows, e_idx, moe_w_gate, moe_w_up, moe_w_down, 0)

    x2, q, kv = _combine_qkv(ys, x1, gates.T, vec(ln_g[0, 1]), vec(ln_b[0, 1]),
                             attn_w_q, (0,), attn_w_kv, name="moe_combine_qkv")
    tab = rel_bias.reshape(-1)
    sinks = attn_sinks[0]
    qi = jnp.arange(WINDOW)[:, None]
    kj = jnp.arange(2 * WINDOW)[None, :]
    o_p = _attn_prompt(q, kv, bp, seq, _masked_buckets(qi + WINDOW - kj), tab, sinks)
    kv_s = kv[n_p:n_p + n_s].reshape(steps, bs, 2, KV_DIM).transpose(2, 1, 0, 3)
    k_all = jnp.concatenate([cache_k_win.reshape(bs, WINDOW, KV_DIM), kv_s[0]], axis=1)
    v_all = jnp.concatenate([cache_v_win.reshape(bs, WINDOW, KV_DIM), kv_s[1]], axis=1)
    dist_s = jnp.arange(steps)[:, None] + WINDOW - jnp.arange(WINDOW + steps)[None, :]
    q_s = q[n_p:n_p + n_s].reshape(steps, bs, D_MODEL).transpose(1, 0, 2)
    o_s = _attn_sample(q_s, k_all, v_all, _masked_buckets(dist_s), tab, sinks)
    o_s = pad_tile(o_s.transpose(1, 0, 2).reshape(n_s, D_MODEL))
    x3, x3_rows, e_idx, gates = _proj_ln((o_p, o_s), attn_w_o, (0,), jnp.zeros((1, D_MODEL), F32), x2,
                                vec(ln_g[1, 0]), vec(ln_b[1, 0]), wr_t, br, name="attn_out_ln")
    ys = _moe_block(x3_rows, e_idx, moe_w_gate, moe_w_up, moe_w_down, 1)
    y_p, y_s = _combine_split(ys, x3, gates.T, vec(ln_g[1, 1]), vec(ln_b[1, 1]), name="moe_combine_1")

    y_prompt = y_p.reshape(bp, seq, D_MODEL)
    y_sample = y_s[:n_s].reshape(steps, bs, D_MODEL).transpose(1, 0, 2)
    kv_p = jnp.stack([kv[(b + 1) * seq - WINDOW:(b + 1) * seq] for b in range(bp)])
    kv_p = kv_p.reshape(bp, WINDOW, 2, N_KV_HEADS, HEAD_DIM)
    k_win_s = k_all[:, steps:].reshape(bs, WINDOW, N_KV_HEADS, HEAD_DIM)
    v_win_s = v_all[:, steps:].reshape(bs, WINDOW, N_KV_HEADS, HEAD_DIM)
    return (y_prompt, y_sample,
            conv_p[None], rnn_p.reshape(1, bp, D_MODEL),
            kv_p[:, :, 0], kv_p[:, :, 1],
            conv_s.transpose(1, 0, 2)[None], rnn_s[None],
            k_win_s, v_win_s)
```

```python
import functools
import math

import jax
import jax.numpy as jnp
from jax import lax
from jax.experimental import pallas as pl
from jax.experimental.pallas import tpu as pltpu

D_MODEL = 2048
DEPTH = 2
LRU_BLOCKS = 8
LRU_BLOCK = D_MODEL // LRU_BLOCKS
CONV_WIDTH = 4
LRU_C = 8.0
N_HEADS = 32
HEAD_DIM = 64
N_KV_HEADS = 8
GROUP = N_HEADS // N_KV_HEADS
KV_DIM = N_KV_HEADS * HEAD_DIM
WINDOW = 128
N_BUCKETS = 32
MAX_DISTANCE = 128
N_EXPERTS = 16
N_GROUPS = 4
EXPERTS_PER_GROUP = N_EXPERTS // N_GROUPS
D_EXPERT = 1024
ALPHA = (2 * DEPTH) ** 0.25
LN_EPS = 1e-5

LANES = 128
SEGS = 8
CHUNKS = D_MODEL // LANES
ROW_SUBLANES = D_MODEL // (2 * LANES)
MOE_TILE = 256
TOK_TILE = 256
CHUNK_TOK_ROWS = CHUNKS + 4
CHUNK_SEG_TOKS = TOK_TILE // SEGS
CHUNK_SEG_ROWS = CHUNK_SEG_TOKS * CHUNK_TOK_ROWS + 4
CHUNK_TILE_ROWS = SEGS * CHUNK_SEG_ROWS
DMA_UNROLL = 8
SCALAR_UNROLL = 32
PLAN_UNROLL = 4
PAIR_BLOCK = 4
GATE_BLOCKS = 4
W_CHUNKS = 4
CAST_ROWS = 256
BF16 = jnp.bfloat16
F32 = jnp.float32
NEG_INF = float("-inf")


def _params(sem, vmem_mb):
    return pltpu.CompilerParams(dimension_semantics=sem, vmem_limit_bytes=vmem_mb * 1024 * 1024)


def _cast_rows(src_ref, dst_ref):
    n = src_ref.shape[0] // CAST_ROWS

    def body(i, c):
        r = pl.multiple_of(i * CAST_ROWS, CAST_ROWS)
        dst_ref[pl.ds(r, CAST_ROWS), :] = src_ref[pl.ds(r, CAST_ROWS), :].astype(BF16)
        return c

    lax.fori_loop(0, n, body, 0)


def _layer_norm(z, g, b):
    mu = jnp.mean(z, axis=-1, keepdims=True)
    zc = z - mu
    var = jnp.mean(zc * zc, axis=-1, keepdims=True)
    return zc * lax.rsqrt(var + LN_EPS) * g + b


def _chunk_row(tok, chunk):
    seg, t = divmod(tok, CHUNK_SEG_TOKS)
    return seg * CHUNK_SEG_ROWS + t * CHUNK_TOK_ROWS + chunk


def _chunk_rows_store(ref, y):
    ref[...] = jnp.zeros_like(ref)
    for seg in range(SEGS):
        rows = slice(seg * CHUNK_SEG_TOKS, (seg + 1) * CHUNK_SEG_TOKS)
        for c in range(CHUNKS):
            dst = pl.ds(_chunk_row(seg * CHUNK_SEG_TOKS, c), CHUNK_SEG_TOKS, stride=CHUNK_TOK_ROWS)
            ref[dst, :] = y[rows, c * LANES:(c + 1) * LANES]


def _chunk_rows_load(ref, tok0, count, chunks=range(CHUNKS)):
    assert tok0 // CHUNK_SEG_TOKS == (tok0 + count - 1) // CHUNK_SEG_TOKS
    return jnp.concatenate(
        [ref[pl.ds(_chunk_row(tok0, c), count, stride=CHUNK_TOK_ROWS), :] for c in chunks], axis=1)


def _store_packed_rows(x_bf, rows_ref):
    n = x_bf.shape[0]
    bits = pltpu.bitcast(x_bf.astype(F32), jnp.uint32)
    packed = bits[:, D_MODEL // 2:] | (bits[:, :D_MODEL // 2] >> 16)
    for c in range(ROW_SUBLANES):
        rows_ref[pl.ds(c, n, stride=ROW_SUBLANES), :] = packed[:, c * LANES:(c + 1) * LANES]


def _load_packed_rows(rows_ref, x_bf_ref):
    n = x_bf_ref.shape[0]
    for c in range(ROW_SUBLANES):
        words = rows_ref[pl.ds(c, n, stride=ROW_SUBLANES), :]
        low = pltpu.bitcast(words << 16, F32).astype(BF16)
        high = pltpu.bitcast(words & jnp.uint32(0xFFFF0000), F32).astype(BF16)
        x_bf_ref[:, c * LANES:(c + 1) * LANES] = low
        x_bf_ref[:, D_MODEL // 2 + c * LANES:D_MODEL // 2 + (c + 1) * LANES] = high


def _tok_operands(x, tile_of=lambda i: i):
    if isinstance(x, tuple):
        xp, xs = x
        d = xp.shape[1]
        last_p = xp.shape[0] // TOK_TILE - 1
        specs = [pl.BlockSpec((TOK_TILE, d), lambda i, *_: (jnp.minimum(tile_of(i), last_p), 0)),
                 pl.BlockSpec((TOK_TILE, d), lambda i, *_: (0, 0))]
        return [xp, xs], specs, last_p + 2
    return ([x], [pl.BlockSpec((TOK_TILE, x.shape[1]), lambda i, *_: (tile_of(i), 0))],
            x.shape[0] // TOK_TILE)


def _tok_load(refs, is_sample=None):
    if len(refs) == 1:
        return refs[0][...]
    if is_sample is None:
        is_sample = pl.program_id(0) == pl.num_programs(0) - 1
    return jnp.where(is_sample, refs[1][...], refs[0][...])


def _linear_kernel(*refs, n_x, act, scale, chunk_rows):
    x_refs, (w_ref, b_ref, o_ref, wbf_ref) = refs[:n_x], refs[n_x:]

    @pl.when(pl.program_id(0) == 0)
    def _():
        _cast_rows(w_ref, wbf_ref)

    y = jnp.dot(_tok_load(x_refs).astype(BF16), wbf_ref[...], preferred_element_type=F32)
    y = y + b_ref[...]
    if act == "gelu":
        y = jax.nn.gelu(y)
    if scale != 1.0:
        y = y * scale
    if chunk_rows:
        _chunk_rows_store(o_ref, y)
    else:
        o_ref[...] = y.astype(o_ref.dtype)


def _linear(x, w, w_index, b, out_dtype, *, name, act=None, scale=1.0, chunk_rows=False):
    arrays, specs, nt = _tok_operands(x)
    k, nout = w.shape[-2:]
    w_block = (None,) * len(w_index) + (k, nout)
    if chunk_rows:
        assert nout == D_MODEL
        out_block, out_rows, out_cols = (CHUNK_TILE_ROWS, LANES), nt * CHUNK_TILE_ROWS, LANES
    else:
        out_block, out_rows, out_cols = (TOK_TILE, nout), nt * TOK_TILE, nout
    return pl.pallas_call(
        functools.partial(_linear_kernel, n_x=len(arrays), act=act, scale=scale,
                          chunk_rows=chunk_rows),
        grid=(nt,),
        in_specs=specs + [
            pl.BlockSpec(w_block, lambda i: w_index + (0, 0), pipeline_mode=pl.Buffered(1)),
            pl.BlockSpec((1, nout), lambda i: (0, 0)),
        ],
        out_specs=pl.BlockSpec(out_block, lambda i: (i, 0)),
        out_shape=jax.ShapeDtypeStruct((out_rows, out_cols), out_dtype),
        scratch_shapes=[pltpu.VMEM((k, nout), BF16)],
        compiler_params=_params(("arbitrary",), 48),
        name=name,
    )(*arrays, w, b)


def _route(logits_t, b_router):
    aff = jax.nn.sigmoid(logits_t)
    sel = aff + b_router
    srow = [sel[e:e + 1, :] for e in range(N_EXPERTS)]
    arow = [aff[e:e + 1, :] for e in range(N_EXPERTS)]

    def top2_sum(v):
        pairs = [v[i] + v[j] for i in range(4) for j in range(i + 1, 4)]
        return functools.reduce(jnp.maximum, pairs)

    scores = [top2_sum(srow[4 * g:4 * g + 4]) for g in range(N_GROUPS)]
    best = scores[0]
    gi = jnp.zeros_like(best, dtype=jnp.int32)
    for g in range(1, N_GROUPS):
        upd = scores[g] > best
        best = jnp.where(upd, scores[g], best)
        gi = jnp.where(upd, g, gi)

    def pick_group(rows, j):
        out = rows[j]
        for g in range(1, N_GROUPS):
            out = jnp.where(gi == g, rows[4 * g + j], out)
        return out

    v = [pick_group(srow, j) for j in range(EXPERTS_PER_GROUP)]
    a = [pick_group(arow, j) for j in range(EXPERTS_PER_GROUP)]

    m1, i1 = v[0], jnp.zeros_like(gi)
    for j in range(1, EXPERTS_PER_GROUP):
        upd = v[j] > m1
        m1 = jnp.where(upd, v[j], m1)
        i1 = jnp.where(upd, j, i1)
    m2 = jnp.full_like(m1, NEG_INF)
    i2 = jnp.zeros_like(gi)
    for j in range(EXPERTS_PER_GROUP):
        cand = jnp.where(i1 == j, NEG_INF, v[j])
        upd = cand > m2
        m2 = jnp.where(upd, cand, m2)
        i2 = jnp.where(upd, j, i2)

    def pick_idx(rows, idx):
        out = rows[0]
        for j in range(1, EXPERTS_PER_GROUP):
            out = jnp.where(idx == j, rows[j], out)
        return out

    a1 = pick_idx(a, i1)
    a2 = pick_idx(a, i2)
    tot = a1 + a2
    e_idx = jnp.concatenate([gi * EXPERTS_PER_GROUP + i1, gi * EXPERTS_PER_GROUP + i2], axis=0)
    gates = jnp.concatenate([a1 / tot, a2 / tot], axis=0)
    return e_idx, gates


def _proj_ln_kernel(*refs, n_m, n_res):
    m_refs = refs[:n_m]
    w_ref, b_ref = refs[n_m:n_m + 2]
    res_refs = refs[n_m + 2:n_m + 2 + n_res]
    (g_ref, beta_ref, wr_ref, br_ref, x_ref, xrow_ref, e_ref, gate_ref,
     wbf_ref, ya, yb) = refs[n_m + 2 + n_res:]
    i = pl.program_id(0)
    n_tiles = pl.num_programs(0) - 1

    @pl.when(i == 0)
    def _():
        _cast_rows(w_ref, wbf_ref)
        yb[...] = jnp.zeros_like(yb)

    for parity, (cur, prev) in enumerate(((ya, yb), (yb, ya))):
        @pl.when(i % 2 == parity)
        def _():
            cur[...] = jnp.dot(_tok_load(m_refs, i >= n_tiles - 1), wbf_ref[...],
                               preferred_element_type=F32)
            y = prev[...] + b_ref[...]
            x = _layer_norm(ALPHA * _tok_load(res_refs, i == n_tiles) + y, g_ref[...], beta_ref[...])
            x_ref[...] = x
            x_bf = x.astype(BF16)
            _store_packed_rows(x_bf, xrow_ref)
            logits_t = lax.dot_general(wr_ref[...].astype(BF16), x_bf,
                                       (((1,), (1,)), ((), ())), preferred_element_type=F32)
            e_idx, gates = _route(logits_t, br_ref[...])
            e_ref[...] = e_idx
            gate_ref[...] = gates


def _proj_ln(m, w, w_index, b, res, g, beta, wr_t, br, *, name):
    nt = _tok_operands(m)[2]
    m_arrays, m_specs, _ = _tok_operands(m, lambda i: jnp.minimum(i, nt - 1))
    res_arrays, res_specs, _ = _tok_operands(res, lambda i: jnp.maximum(i - 1, 0))
    k = w.shape[-2]
    tm = TOK_TILE
    n = nt * tm
    row = lambda i: (jnp.maximum(i - 1, 0), 0)
    const = lambda i: (0, 0)
    x, x_rows, e_idx, gates = pl.pallas_call(
        functools.partial(_proj_ln_kernel, n_m=len(m_arrays), n_res=len(res_arrays)),
        grid=(nt + 1,),
        in_specs=m_specs + [
            pl.BlockSpec((None,) * len(w_index) + (k, D_MODEL), lambda i: w_index + (0, 0),
                         pipeline_mode=pl.Buffered(1)),
            pl.BlockSpec((1, D_MODEL), const),
        ] + res_specs + [
            pl.BlockSpec((1, D_MODEL), const),
            pl.BlockSpec((1, D_MODEL), const),
            pl.BlockSpec((N_EXPERTS, D_MODEL), const),
            pl.BlockSpec((N_EXPERTS, 1), const),
        ],
        out_specs=[
            pl.BlockSpec((tm, D_MODEL), row),
            pl.BlockSpec((tm * ROW_SUBLANES, LANES), row),
            pl.BlockSpec((None, 2, tm), lambda i: (jnp.maximum(i - 1, 0), 0, 0)),
            pl.BlockSpec((None, 2, tm), lambda i: (jnp.maximum(i - 1, 0), 0, 0)),
        ],
        out_shape=[
            jax.ShapeDtypeStruct((n, D_MODEL), F32),
            jax.ShapeDtypeStruct((n * ROW_SUBLANES, LANES), jnp.uint32),
            jax.ShapeDtypeStruct((nt, 2, tm), jnp.int32),
            jax.ShapeDtypeStruct((nt, 2, tm), F32),
        ],
        scratch_shapes=[pltpu.VMEM((k, D_MODEL), BF16),
                        pltpu.VMEM((tm, D_MODEL), F32), pltpu.VMEM((tm, D_MODEL), F32)],
        compiler_params=_params(("arbitrary",), 52),
        name=name,
    )(*m_arrays, w, b, *res_arrays, g, beta, wr_t, br)
    e_idx = e_idx.transpose(1, 0, 2).reshape(2, n)
    gates = gates.transpose(1, 0, 2).reshape(2, n)
    return x, x_rows, e_idx, gates


def _plan_kernel(e_ref, pos_ref, meta_ref, rank_ref):
    nrow = e_ref.shape[0]
    ri = lax.broadcasted_iota(jnp.int32, (LANES, LANES), 0)
    ci = lax.broadcasted_iota(jnp.int32, (LANES, LANES), 1)
    tri = jnp.where(ri <= ci, 1.0, 0.0).astype(BF16)
    sub = lax.broadcasted_iota(jnp.int32, (N_EXPERTS, LANES), 0)

    def count_body(b, base):
        rows = [b * PLAN_UNROLL + u for u in range(PLAN_UNROLL)]
        onehots = [sub == e_ref[pl.ds(r, 1), :] for r in rows]
        locs = [jnp.dot(jnp.where(oh, 1.0, 0.0).astype(BF16), tri, preferred_element_type=F32)
                for oh in onehots]
        for r, onehot, loc in zip(rows, onehots, locs):
            rank_ref[pl.ds(r, 1), :] = jnp.sum(jnp.where(onehot, base + loc - 1.0, 0.0),
                                               axis=0, keepdims=True)
            base = base + jnp.broadcast_to(loc[:, LANES - 1:LANES], (N_EXPERTS, LANES))
        return base

    count = lax.fori_loop(0, nrow // PLAN_UNROLL, count_body, jnp.zeros((N_EXPERTS, LANES), F32))
    ntile = jnp.floor((count + (MOE_TILE - 1.0)) * (1.0 / MOE_TILE))
    offs = []
    acc = jnp.zeros((1, LANES), F32)
    for e in range(N_EXPERTS):
        offs.append(acc)
        acc = acc + ntile[e:e + 1, :]
    tile_off = jnp.concatenate(offs, axis=0)
    tile_end = tile_off + ntile
    lane = lax.broadcasted_iota(jnp.int32, (N_EXPERTS, LANES), 1).astype(F32)
    tile_expert = jnp.sum(jnp.where(tile_end <= lane, 1.0, 0.0), axis=0, keepdims=True)
    tile_expert = jnp.minimum(tile_expert, N_EXPERTS - 1.0)
    own = jnp.logical_and(tile_off <= lane, lane < tile_end)
    run_end = jnp.sum(jnp.where(own, tile_end, 0.0), axis=0, keepdims=True)
    next_expert = jnp.sum(jnp.where(tile_end <= run_end, 1.0, 0.0), axis=0, keepdims=True)
    has_next = jnp.logical_and(lane[0:1, :] < acc, run_end < acc)
    next_expert = jnp.where(has_next, next_expert, -1.0)
    meta = jnp.concatenate([tile_expert, acc, next_expert, jnp.zeros((5, LANES), F32)], axis=0)
    meta_ref[...] = meta.astype(jnp.int32)
    row_off = tile_off * float(MOE_TILE)

    def pos_body(r, c):
        onehot = sub == e_ref[pl.ds(r, 1), :]
        p = jnp.sum(jnp.where(onehot, row_off, 0.0), axis=0, keepdims=True) + rank_ref[pl.ds(r, 1), :]
        pos_ref[pl.ds(r, 1), :] = p.astype(jnp.int32)
        return c

    lax.fori_loop(0, nrow, pos_body, 0)


def _plan(e_idx, *, name):
    n2 = e_idx.shape[0] * e_idx.shape[1]
    assert n2 % (LANES * PLAN_UNROLL) == 0
    e2d = e_idx.reshape(n2 // LANES, LANES)
    pos, meta = pl.pallas_call(
        _plan_kernel,
        out_shape=[jax.ShapeDtypeStruct(e2d.shape, jnp.int32),
                   jax.ShapeDtypeStruct((8, LANES), jnp.int32)],
        scratch_shapes=[pltpu.VMEM(e2d.shape, F32)],
        name=name,
    )(e2d)
    return pos.reshape(n2), meta[0], meta[1, :1], meta[2]


def _invert_kernel(pos_ref, pair_ref):
    n_rows = pair_ref.shape[0]
    n_pairs = pos_ref.shape[0]

    def fill_body(b, c):
        for u in range(SCALAR_UNROLL):
            pair_ref[b * SCALAR_UNROLL + u] = -1
        return c

    def pair_body(b, c):
        rows = [pos_ref[b * SCALAR_UNROLL + u] for u in range(SCALAR_UNROLL)]
        for u in range(SCALAR_UNROLL):
            pair_ref[rows[u]] = b * SCALAR_UNROLL + u
        return c

    lax.fori_loop(0, n_rows // SCALAR_UNROLL, fill_body, 0)
    lax.fori_loop(0, n_pairs // SCALAR_UNROLL, pair_body, 0)


def _invert(pos, n_rows, *, name):
    return pl.pallas_call(
        _invert_kernel,
        grid_spec=pltpu.PrefetchScalarGridSpec(
            num_scalar_prefetch=1,
            grid=(1,),
            in_specs=[],
            out_specs=pl.BlockSpec(memory_space=pltpu.SMEM),
        ),
        out_shape=jax.ShapeDtypeStruct((n_rows,), jnp.int32),
        name=name,
    )(pos)


def _expert_changed(te_ref, i):
    return jnp.logical_or(i == 0, te_ref[i] != te_ref[jnp.maximum(i - 1, 0)])


def _moe_up_kernel(pair_ref, te_ref, nu_ref, nxt_ref, x_hbm, wg_hbm, wu_hbm, h_ref,
                   wg_cur, wg_next, wg_stage, wu_cur, wu_next, wu_stage, x_rows, x_bf,
                   sem, wsems, st, *, n_tok, n_tiles, layer):
    i = pl.program_id(0)
    nu = nu_ref[0]
    weights = _ExpertWeights([(wg_hbm, wg_stage, wg_next, wg_cur), (wu_hbm, wu_stage, wu_next, wu_cur)],
                             layer, st, wsems)

    def row_copy(tile, r):
        p = pair_ref[tile * MOE_TILE + r]
        tok = jnp.where(p >= n_tok, p - n_tok, jnp.maximum(p, 0))
        src = pl.ds(pl.multiple_of(tok * ROW_SUBLANES, ROW_SUBLANES), ROW_SUBLANES)
        dst = pl.ds(r * ROW_SUBLANES, ROW_SUBLANES)
        return pltpu.make_async_copy(x_hbm.at[src, :], x_rows.at[dst, :], sem)

    def wait_tile():
        pltpu.make_async_copy(x_hbm.at[pl.ds(0, MOE_TILE * ROW_SUBLANES), :], x_rows, sem).wait()

    @pl.when(i == 0)
    def _():
        def body(rb, c):
            for u in range(DMA_UNROLL):
                row_copy(0, rb * DMA_UNROLL + u).start()
            return c
        lax.fori_loop(0, MOE_TILE // DMA_UNROLL, body, 0)
        weights.reset()

    @pl.when(i <= nu)
    def _():
        wait_tile()

    running = i < nu

    @pl.when(jnp.logical_and(running, _expert_changed(te_ref, i)))
    def _():
        weights.switch_to(te_ref[i])

    active, done = weights.begin_step(jnp.where(running, nxt_ref[i], -1))

    @pl.when(running)
    def _():
        _load_packed_rows(x_rows, x_bf)
        next_tile = jnp.minimum(i + 1, n_tiles - 1)
        for r in range(MOE_TILE):
            row_copy(next_tile, r).start(priority=r % 2)
        weights.convert_step(active, done)
        x = x_bf[...]
        a = jnp.dot(x, wg_cur[...], preferred_element_type=F32)
        b = jnp.dot(x, wu_cur[...], preferred_element_type=F32)
        h_ref[...] = (jax.nn.silu(a) * b).astype(BF16)

    weights.end_step(active, done)

    @pl.when(jnp.logical_and(i == n_tiles - 1, i < nu))
    def _():
        wait_tile()

    @pl.when(i >= nu)
    def _():
        h_ref[...] = jnp.zeros_like(h_ref)


class _ExpertWeights:
    def __init__(self, mats, layer, st, sems):
        self.mats, self.layer, self.st, self.sems = mats, layer, st, sems

    def _copies(self, expert, c):
        out = []
        for w_hbm, stage, _, _ in self.mats:
            rows = stage.shape[1]
            src = w_hbm.at[self.layer, expert, pl.ds(pl.multiple_of(c * rows, rows), rows), :]
            out.append(pltpu.make_async_copy(src, stage.at[c % 2], self.sems.at[c % 2]))
        return out

    def _start(self, expert, c):
        for cp in self._copies(expert, c):
            cp.start()
        self.st[2] = c + 1

    def _convert(self, c_src, c_dst):
        for _, stage, w_next, _ in self.mats:
            rows = stage.shape[1]
            dst = pl.ds(pl.multiple_of(c_dst * rows, rows), rows)
            w_next[dst, :] = stage[c_src % 2].astype(BF16)

    def reset(self):
        for _, stage, _, _ in self.mats:
            stage[...] = jnp.zeros_like(stage)
        self.st[1] = 0
        self.st[2] = 0

    def switch_to(self, expert):
        st = self.st

        def body(c, carry):
            @pl.when(c >= st[2])
            def _():
                self._start(expert, c)

            @pl.when(jnp.logical_and(c + 1 < W_CHUNKS, c + 1 >= st[2]))
            def _():
                self._start(expert, c + 1)

            for cp in self._copies(expert, c):
                cp.wait()
            self._convert(c, c)
            return carry

        lax.fori_loop(st[1], W_CHUNKS, body, 0)
        for _, _, w_next, w_cur in self.mats:
            _copy_rows(w_next, w_cur)
        st[1] = 0
        st[2] = 0

    def begin_step(self, next_expert):
        st = self.st
        done, issued = st[1], st[2]
        has_next = next_expert >= 0
        active = jnp.logical_and(has_next, done < issued)

        @pl.when(jnp.logical_and(has_next, jnp.logical_and(issued < W_CHUNKS, issued < done + 2)))
        def _():
            self._start(next_expert, issued)

        @pl.when(active)
        def _():
            for cp in self._copies(next_expert, done):
                cp.wait()

        return active, done

    def convert_step(self, active, done):
        self._convert(jnp.where(active, done, done + 1), jnp.where(active, done, W_CHUNKS))

    def end_step(self, active, done):
        @pl.when(active)
        def _():
            self.st[1] = done + 1


def _copy_rows(src_ref, dst_ref):
    n = dst_ref.shape[0] // CAST_ROWS

    def body(i, c):
        r = pl.multiple_of(i * CAST_ROWS, CAST_ROWS)
        dst_ref[pl.ds(r, CAST_ROWS), :] = src_ref[pl.ds(r, CAST_ROWS), :]
        return c

    lax.fori_loop(0, n, body, 0)


def _moe_down_kernel(pair_ref, te_ref, nu_ref, nxt_ref, h_ref, wd_hbm, out_hbm,
                     w_cur, w_next, stage, ya, yb, sems, tsem, wsems, st, *, n_tok, n_tiles, layer):
    i = pl.program_id(0)
    nu = nu_ref[0]
    bufs = (ya, yb)
    trash = 2 * n_tok
    weights = _ExpertWeights([(wd_hbm, stage, w_next, w_cur)], layer, st, wsems)

    def row_copy(tile, r, buf, sem):
        p = pair_ref[tile * MOE_TILE + r]
        dst = jnp.where(p < 0, trash + r, p)
        return pltpu.make_async_copy(buf.at[pl.ds(r, 1), :], out_hbm.at[pl.ds(dst, 1), :], sem)

    def wait_tile(buf, sem):
        pltpu.make_async_copy(buf, out_hbm.at[pl.ds(0, MOE_TILE), :], sem).wait()

    @pl.when(i == 0)
    def _():
        yb[...] = jnp.zeros_like(yb)
        fill = pltpu.make_async_copy(yb, out_hbm.at[pl.ds(trash, MOE_TILE), :], tsem)
        fill.start()
        fill.wait()
        weights.reset()

    running = i < nu

    @pl.when(jnp.logical_and(running, _expert_changed(te_ref, i)))
    def _():
        weights.switch_to(te_ref[i])

    active, done = weights.begin_step(jnp.where(running, nxt_ref[i], -1))

    for parity in range(2):
        cur, prev = bufs[parity], bufs[1 - parity]
        cur_sem, prev_sem = sems.at[parity], sems.at[1 - parity]
        mine = i % 2 == parity

        @pl.when(jnp.logical_and(mine, jnp.logical_and(i >= 1, i - 1 <= nu)))
        def _():
            wait_tile(cur, cur_sem)

        @pl.when(jnp.logical_and(mine, running))
        def _():
            prev_tile = jnp.maximum(i - 1, 0)
            for r in range(MOE_TILE):
                row_copy(prev_tile, r, prev, prev_sem).start(priority=r % 2)
            weights.convert_step(active, done)
            cur[...] = jnp.dot(h_ref[...], w_cur[...], preferred_element_type=F32)

        @pl.when(jnp.logical_and(mine, i == nu))
        def _():
            def body(rb, c):
                for u in range(DMA_UNROLL):
                    row_copy(i - 1, rb * DMA_UNROLL + u, prev, prev_sem).start()
                return c
            lax.fori_loop(0, MOE_TILE // DMA_UNROLL, body, 0)

            @pl.when(i == n_tiles)
            def _():
                wait_tile(prev, prev_sem)

    weights.end_step(active, done)


def _moe_ffn(x, pair, tile_expert, n_used, next_expert, w_gate, w_up, w_down, layer):
    n_tok = x.shape[0] // ROW_SUBLANES
    n_tiles = pair.shape[0] // MOE_TILE

    def tile(i, pr, te, nu):
        return (jnp.minimum(i, nu[0] - 1), 0)

    hbm = pl.BlockSpec(memory_space=pl.ANY)
    row_bufs = [pltpu.VMEM((MOE_TILE, D_MODEL), F32), pltpu.VMEM((MOE_TILE, D_MODEL), F32),
                pltpu.SemaphoreType.DMA((2,))]
    up_chunk = D_MODEL // W_CHUNKS
    up_weight = [pltpu.VMEM((D_MODEL, D_EXPERT), BF16),
                 pltpu.VMEM((D_MODEL + up_chunk, D_EXPERT), BF16),
                 pltpu.VMEM((2, up_chunk, D_EXPERT), F32)]
    h = pl.pallas_call(
        functools.partial(_moe_up_kernel, n_tok=n_tok, n_tiles=n_tiles, layer=layer),
        grid_spec=pltpu.PrefetchScalarGridSpec(
            num_scalar_prefetch=4,
            grid=(n_tiles,),
            in_specs=[hbm, hbm, hbm],
            out_specs=pl.BlockSpec((MOE_TILE, D_EXPERT), lambda i, pr, te, nu, nx: (i, 0)),
            scratch_shapes=up_weight + up_weight + [
                pltpu.VMEM((MOE_TILE * ROW_SUBLANES, LANES), jnp.uint32),
                pltpu.VMEM((MOE_TILE, D_MODEL), BF16),
                pltpu.SemaphoreType.DMA(()), pltpu.SemaphoreType.DMA((2,)),
                pltpu.SMEM((4,), jnp.int32)],
        ),
        out_shape=jax.ShapeDtypeStruct((n_tiles * MOE_TILE, D_EXPERT), BF16),
        compiler_params=_params(("arbitrary",), 48),
        name=f"moe_up_{layer}",
    )(pair, tile_expert, n_used, next_expert, x, w_gate, w_up)
    down_chunk = D_EXPERT // W_CHUNKS
    return pl.pallas_call(
        functools.partial(_moe_down_kernel, n_tok=n_tok, n_tiles=n_tiles, layer=layer),
        grid_spec=pltpu.PrefetchScalarGridSpec(
            num_scalar_prefetch=4,
            grid=(n_tiles + 1,),
            in_specs=[pl.BlockSpec((MOE_TILE, D_EXPERT), lambda i, pr, te, nu, nx: tile(i, pr, te, nu)),
                      pl.BlockSpec(memory_space=pl.ANY)],
            out_specs=pl.BlockSpec(memory_space=pl.ANY),
            scratch_shapes=[pltpu.VMEM((D_EXPERT, D_MODEL), BF16),
                            pltpu.VMEM((D_EXPERT + down_chunk, D_MODEL), BF16),
                            pltpu.VMEM((2, down_chunk, D_MODEL), F32)] + row_bufs
            + [pltpu.SemaphoreType.DMA(()), pltpu.SemaphoreType.DMA((2,)),
               pltpu.SMEM((4,), jnp.int32)],
        ),
        out_shape=jax.ShapeDtypeStruct((2 * n_tok + MOE_TILE, D_MODEL), F32),
        compiler_params=_params(("arbitrary",), 40),
        name=f"moe_down_{layer}",
    )(pair, tile_expert, n_used, next_expert, h, w_down)


def _moe_tile_kernel(pair_ref, te_ref, nu_ref, nxt_ref, x_hbm, wg_hbm, wu_hbm, wd_hbm, out_hbm,
                     wg_cur, wg_next, wg_stage, wu_cur, wu_next, wu_stage, wd_cur, wd_next, wd_stage,
                     x_rows, x_bf, ya, yb, gsem, ssems, tsem, wsems, st, *, n_tok, n_tiles, layer):
    i = pl.program_id(0)
    nu = nu_ref[0]
    running = i < nu
    trash = 2 * n_tok
    weights = _ExpertWeights([(wg_hbm, wg_stage, wg_next, wg_cur), (wu_hbm, wu_stage, wu_next, wu_cur),
                              (wd_hbm, wd_stage, wd_next, wd_cur)], layer, st, wsems)

    def gather_copy(tile, r):
        p = pair_ref[tile * MOE_TILE + r]
        tok = jnp.where(p >= n_tok, p - n_tok, jnp.maximum(p, 0))
        src = pl.ds(pl.multiple_of(tok * ROW_SUBLANES, ROW_SUBLANES), ROW_SUBLANES)
        return pltpu.make_async_copy(x_hbm.at[src, :], x_rows.at[pl.ds(r * ROW_SUBLANES, ROW_SUBLANES), :],
                                     gsem)

    def scatter_copy(tile, r, buf, sem):
        p = pair_ref[tile * MOE_TILE + r]
        dst = jnp.where(p < 0, trash + r, p)
        return pltpu.make_async_copy(buf.at[pl.ds(r, 1), :], out_hbm.at[pl.ds(dst, 1), :], sem)

    def wait_scatter(buf, sem):
        pltpu.make_async_copy(buf, out_hbm.at[pl.ds(0, MOE_TILE), :], sem).wait()

    @pl.when(i == 0)
    def _():
        def body(rb, c):
            for u in range(DMA_UNROLL):
                gather_copy(0, rb * DMA_UNROLL + u).start()
            return c
        lax.fori_loop(0, MOE_TILE // DMA_UNROLL, body, 0)
        yb[...] = jnp.zeros_like(yb)
        fill = pltpu.make_async_copy(yb, out_hbm.at[pl.ds(trash, MOE_TILE), :], tsem)
        fill.start()
        fill.wait()
        weights.reset()

    @pl.when(i <= nu)
    def _():
        pltpu.make_async_copy(x_hbm.at[pl.ds(0, MOE_TILE * ROW_SUBLANES), :], x_rows, gsem).wait()

    @pl.when(jnp.logical_and(running, _expert_changed(te_ref, i)))
    def _():
        weights.switch_to(te_ref[i])

    active, done = weights.begin_step(jnp.where(running, nxt_ref[i], -1))

    for parity, (cur, prev) in enumerate(((ya, yb), (yb, ya))):
        cur_sem, prev_sem = ssems.at[parity], ssems.at[1 - parity]
        mine = i % 2 == parity

        @pl.when(jnp.logical_and(mine, jnp.logical_and(i >= 1, i - 1 <= nu)))
        def _():
            wait_scatter(cur, cur_sem)

        @pl.when(jnp.logical_and(mine, running))
        def _():
            _load_packed_rows(x_rows, x_bf)
            next_tile = jnp.minimum(i + 1, n_tiles - 1)
            prev_tile = jnp.maximum(i - 1, 0)
            for r in range(MOE_TILE):
                gather_copy(next_tile, r).start(priority=r % 2)
            for r in range(MOE_TILE):
                scatter_copy(prev_tile, r, prev, prev_sem).start(priority=r % 2)
            weights.convert_step(active, done)
            x = x_bf[...]
            a = jnp.dot(x, wg_cur[...], preferred_element_type=F32)
            b = jnp.dot(x, wu_cur[...], preferred_element_type=F32)
            h = (jax.nn.silu(a) * b).astype(BF16)
            cur[...] = jnp.dot(h, wd_cur[...], preferred_element_type=F32)

        @pl.when(jnp.logical_and(mine, i == nu))
        def _():
            def body(rb, c):
                for u in range(DMA_UNROLL):
                    scatter_copy(i - 1, rb * DMA_UNROLL + u, prev, prev_sem).start()
                return c
            lax.fori_loop(0, MOE_TILE // DMA_UNROLL, body, 0)

            @pl.when(i == n_tiles)
            def _():
                wait_scatter(prev, prev_sem)

    weights.end_step(active, done)


def _moe_tiles(x, pair, tile_expert, n_used, next_expert, w_gate, w_up, w_down, layer):
    n_tok = x.shape[0] // ROW_SUBLANES
    n_tiles = pair.shape[0] // MOE_TILE
    hbm = pl.BlockSpec(memory_space=pl.ANY)

    def weight_bufs(k, n):
        chunk = k // W_CHUNKS
        return [pltpu.VMEM((k, n), BF16), pltpu.VMEM((k + chunk, n), BF16), pltpu.VMEM((2, chunk, n), F32)]

    return pl.pallas_call(
        functools.partial(_moe_tile_kernel, n_tok=n_tok, n_tiles=n_tiles, layer=layer),
        grid_spec=pltpu.PrefetchScalarGridSpec(
            num_scalar_prefetch=4,
            grid=(n_tiles + 1,),
            in_specs=[hbm, hbm, hbm, hbm],
            out_specs=hbm,
            scratch_shapes=weight_bufs(D_MODEL, D_EXPERT) + weight_bufs(D_MODEL, D_EXPERT)
            + weight_bufs(D_EXPERT, D_MODEL) + [
                pltpu.VMEM((MOE_TILE * ROW_SUBLANES, LANES), jnp.uint32),
                pltpu.VMEM((MOE_TILE, D_MODEL), BF16),
                pltpu.VMEM((MOE_TILE, D_MODEL), F32), pltpu.VMEM((MOE_TILE, D_MODEL), F32),
                pltpu.SemaphoreType.DMA(()), pltpu.SemaphoreType.DMA((2,)),
                pltpu.SemaphoreType.DMA(()), pltpu.SemaphoreType.DMA((2,)),
                pltpu.SMEM((4,), jnp.int32)],
        ),
        out_shape=jax.ShapeDtypeStruct((2 * n_tok + MOE_TILE, D_MODEL), F32),
        compiler_params=_params(("arbitrary",), 56),
        name=f"moe_tiles_{layer}",
    )(pair, tile_expert, n_used, next_expert, x, w_gate, w_up, w_down)


def _combine(y0_ref, y1_ref, res_ref, gate_ref, g_ref, beta_ref):
    gate = gate_ref[...]
    ffn = gate[:, 0:1] * y0_ref[...] + gate[:, 1:2] * y1_ref[...]
    return _layer_norm(ALPHA * res_ref[...] + ffn, g_ref[...], beta_ref[...])


def _combine_specs(n):
    nt = n // TOK_TILE
    row = lambda i: (i, 0)
    const = lambda i: (0, 0)
    return [pl.BlockSpec((TOK_TILE, D_MODEL), row),
            pl.BlockSpec((TOK_TILE, D_MODEL), lambda i: (i + nt, 0)),
            pl.BlockSpec((TOK_TILE, D_MODEL), row),
            pl.BlockSpec((TOK_TILE, 2), row),
            pl.BlockSpec((1, D_MODEL), const),
            pl.BlockSpec((1, D_MODEL), const)]


def _combine_split_kernel(y0_ref, y1_ref, res_ref, gate_ref, g_ref, beta_ref, prompt_ref, sample_ref):
    x = _combine(y0_ref, y1_ref, res_ref, gate_ref, g_ref, beta_ref)
    is_sample = pl.program_id(0) == pl.num_programs(0) - 1

    @pl.when(jnp.logical_not(is_sample))
    def _():
        prompt_ref[...] = x

    @pl.when(is_sample)
    def _():
        sample_ref[...] = x


def _combine_split(ys, res, gates_col, g, beta, *, name):
    n = res.shape[0]
    tm = TOK_TILE
    nt = n // tm
    return pl.pallas_call(
        _combine_split_kernel,
        grid=(nt,),
        in_specs=_combine_specs(n),
        out_specs=[pl.BlockSpec((tm, D_MODEL), lambda i: (jnp.minimum(i, nt - 2), 0)),
                   pl.BlockSpec((tm, D_MODEL), lambda i: (0, 0))],
        out_shape=[jax.ShapeDtypeStruct((n - tm, D_MODEL), F32),
                   jax.ShapeDtypeStruct((tm, D_MODEL), F32)],
        compiler_params=_params(("arbitrary",), 40),
        name=name,
    )(ys, ys, res, gates_col, g, beta)


def _load_weight(w_hbm, wbf_ref, stage_ref, sems):
    rows = stage_ref.shape[1]
    n_chunks = wbf_ref.shape[0] // rows

    def chunk_copy(c):
        return pltpu.make_async_copy(w_hbm.at[pl.ds(c * rows, rows), :], stage_ref.at[c % 2],
                                     sems.at[c % 2])

    chunk_copy(0).start()
    for c in range(n_chunks):
        if c + 1 < n_chunks:
            chunk_copy(c + 1).start()
        chunk_copy(c).wait()
        wbf_ref[c * rows:(c + 1) * rows, :] = stage_ref[c % 2].astype(BF16)


def _combine_qkv_kernel(y0_ref, y1_ref, res_ref, gate_ref, g_ref, beta_ref, wq_hbm, wkv_hbm,
                        x_ref, q_ref, kv_ref, wq_bf, wkv_bf, stage_q, stage_kv, sems, *, wq_index):
    @pl.when(pl.program_id(0) == 0)
    def _():
        wq = wq_hbm
        for k in wq_index:
            wq = wq.at[k]
        _load_weight(wq, wq_bf, stage_q, sems)
        _load_weight(wkv_hbm, wkv_bf, stage_kv, sems)

    x = _combine(y0_ref, y1_ref, res_ref, gate_ref, g_ref, beta_ref)
    x_ref[...] = x
    x_bf = x.astype(BF16)
    q = jnp.dot(x_bf, wq_bf[...], preferred_element_type=F32) * (HEAD_DIM ** -0.5)
    q_ref[...] = q.astype(BF16)
    kv_ref[...] = jnp.dot(x_bf, wkv_bf[...], preferred_element_type=F32)


def _combine_qkv(ys, res, gates_col, g, beta, w_q, wq_index, w_kv, *, name):
    n = res.shape[0]
    tm = TOK_TILE
    row = lambda i: (i, 0)
    hbm = pl.BlockSpec(memory_space=pl.ANY)
    return pl.pallas_call(
        functools.partial(_combine_qkv_kernel, wq_index=wq_index),
        grid=(n // tm,),
        in_specs=_combine_specs(n) + [hbm, hbm],
        out_specs=[pl.BlockSpec((tm, D_MODEL), row),
                   pl.BlockSpec((tm, D_MODEL), row),
                   pl.BlockSpec((tm, 2 * KV_DIM), row)],
        out_shape=[jax.ShapeDtypeStruct((n, D_MODEL), F32),
                   jax.ShapeDtypeStruct((n, D_MODEL), BF16),
                   jax.ShapeDtypeStruct((n, 2 * KV_DIM), F32)],
        scratch_shapes=[pltpu.VMEM((D_MODEL, D_MODEL), BF16),
                        pltpu.VMEM((D_MODEL, 2 * KV_DIM), BF16),
                        pltpu.VMEM((2, CAST_ROWS, D_MODEL), F32),
                        pltpu.VMEM((2, CAST_ROWS, 2 * KV_DIM), F32),
                        pltpu.SemaphoreType.DMA((2,))],
        compiler_params=_params(("arbitrary",), 52),
        name=name,
    )(ys, ys, res, gates_col, g, beta, w_q, w_kv)


def _moe_block(x_rows, e_idx, w_gate, w_up, w_down, layer):
    n = x_rows.shape[0] // ROW_SUBLANES
    n_tiles = -(-(2 * n + N_EXPERTS * (MOE_TILE - 1)) // MOE_TILE)
    pos, tile_expert, n_used, next_expert = _plan(e_idx, name=f"moe_plan_{layer}")
    pair = _invert(pos, n_tiles * MOE_TILE, name=f"moe_invert_{layer}")
    return _moe_tiles(x_rows, pair, tile_expert, n_used, next_expert, w_gate, w_up, w_down, layer)


def _sigmoid(x):
    return 0.5 * jnp.tanh(0.5 * x) + 0.5


def _log_sigmoid(x):
    return -(jnp.maximum(-x, 0.0) + jnp.log1p(jnp.exp(-jnp.abs(x))))


def _lru_gate_blocks(xcs, blocks, wrg_bf, wig_bf, brg_ref, big_ref, lam_ref):
    xbs = [xc.astype(BF16) for xc in xcs]
    r_lin = [jnp.dot(xb, wrg_bf[n], preferred_element_type=F32) for xb, n in zip(xbs, blocks)]
    i_lin = [jnp.dot(xb, wig_bf[n], preferred_element_type=F32) for xb, n in zip(xbs, blocks)]
    out = []
    for xc, n, rl, il in zip(xcs, blocks, r_lin, i_lin):
        cols = slice(n * LRU_BLOCK, (n + 1) * LRU_BLOCK)
        r = _sigmoid(rl + brg_ref[:, cols])
        i = _sigmoid(il + big_ref[:, cols])
        log_a = LRU_C * r * _log_sigmoid(lam_ref[:, cols])
        a = jnp.exp(log_a)
        u = xc * i * jnp.sqrt(-jnp.tanh(log_a) * (a * a + 1.0))
        out.append((a, u))
    return out


def _lru_gate_block(xc, n, wrg_bf, wig_bf, brg_ref, big_ref, lam_ref):
    return _lru_gate_blocks([xc], [n], wrg_bf, wig_bf, brg_ref, big_ref, lam_ref)[0]


def _cast_gate_weights(wrg_ref, wig_ref, wrg_bf, wig_bf):
    for n in range(LRU_BLOCKS):
        wrg_bf[n] = wrg_ref[n].astype(BF16)
        wig_bf[n] = wig_ref[n].astype(BF16)


def _lru_prompt_kernel(xb_ref, yb_ref, cw_ref, cb_ref, wrg_ref, wig_ref, brg_ref, big_ref, lam_ref,
                       m_ref, conv_ref, hlast_ref, xs, tail, a_s, u_s, hs_t, h_s, wrg_bf, wig_bf):
    b = pl.program_id(0)
    j = pl.program_id(1)
    tt = m_ref.shape[0]
    seg_len = tt // SEGS
    taps = CONV_WIDTH - 1
    head = SEGS * taps

    @pl.when(jnp.logical_and(b == 0, j == 0))
    def _():
        _cast_gate_weights(wrg_ref, wig_ref, wrg_bf, wig_bf)

    @pl.when(j == 0)
    def _():
        tail[...] = jnp.zeros_like(tail)
        h_s[...] = jnp.zeros_like(h_s)

    for q in range(seg_len):
        xs[head + SEGS * q:head + SEGS * (q + 1), :] = jnp.concatenate(
            [xb_ref[pl.ds(_chunk_row(q, c), SEGS, stride=CHUNK_SEG_ROWS), :] for c in range(CHUNKS)],
            axis=1)
    sub = lax.broadcasted_iota(jnp.int32, (SEGS, D_MODEL), 0)
    for k in range(taps):
        last = head + SEGS * (seg_len - taps + k)
        joined = jnp.where(sub == SEGS - 1, tail[SEGS * k:SEGS * (k + 1), :], xs[last:last + SEGS, :])
        xs[SEGS * k:SEGS * (k + 1), :] = pltpu.roll(joined, 1, axis=0)
    tail[...] = xs[head + SEGS * (seg_len - taps):head + SEGS * seg_len, :]

    for first_block in range(0, LRU_BLOCKS, GATE_BLOCKS):
        blocks = range(first_block, first_block + GATE_BLOCKS)
        xcs = []
        for n in blocks:
            cols = slice(n * LRU_BLOCK, (n + 1) * LRU_BLOCK)
            xc = cb_ref[:, cols] + cw_ref[0:1, cols] * xs[0:tt, cols]
            for k in range(1, CONV_WIDTH):
                xc = xc + cw_ref[k:k + 1, cols] * xs[SEGS * k:SEGS * k + tt, cols]
            xcs.append(xc)
        gates = _lru_gate_blocks(xcs, blocks, wrg_bf, wig_bf, brg_ref, big_ref, lam_ref)
        for n, (a, u) in zip(blocks, gates):
            cols = slice(n * LRU_BLOCK, (n + 1) * LRU_BLOCK)
            a_s[:, cols] = a
            u_s[:, cols] = u

    def scan_body(q, carry):
        h, prod = carry
        rows = pl.ds(pl.multiple_of(q * SEGS, SEGS), SEGS)
        a = a_s[rows, :]
        h = a * h + u_s[rows, :]
        prod = a * prod
        u_s[rows, :] = h
        a_s[rows, :] = prod
        return h, prod

    h_end, prod_end = lax.fori_loop(
        0, seg_len, scan_body,
        (jnp.zeros((SEGS, D_MODEL), F32), jnp.ones((SEGS, D_MODEL), F32)))
    state = h_s[...]
    entering = []
    for s in range(SEGS):
        entering.append(state)
        state = h_end[s:s + 1, :] + prod_end[s:s + 1, :] * state
    h_s[...] = state
    enter = jnp.concatenate(entering, axis=0)

    def fix_body(q, carry):
        rows = pl.ds(pl.multiple_of(q * SEGS, SEGS), SEGS)
        h = u_s[rows, :] + a_s[rows, :] * enter
        for c in range(CHUNKS):
            hs_t[pl.ds(q * CHUNK_TOK_ROWS + c, SEGS, stride=CHUNK_SEG_ROWS), :] = h[:, c * LANES:(c + 1) * LANES]
        return carry

    lax.fori_loop(0, seg_len, fix_body, 0)
    for s in range(SEGS):
        rows = slice(s * seg_len, (s + 1) * seg_len)
        hs = _chunk_rows_load(hs_t, s * seg_len, seg_len)
        m_ref[rows, :] = (hs * yb_ref[rows, :].astype(F32)).astype(BF16)

    @pl.when(j == pl.num_programs(1) - 1)
    def _():
        for k in range(taps):
            conv_ref[k:k + 1, :] = tail[SEGS * k + SEGS - 1:SEGS * (k + 1), :]
        hlast_ref[...] = state


def _lru_prompt(xb, yb, batch, seq, cw, cb, wrg, wig, brg, big, lam):
    tt = TOK_TILE
    nj = seq // tt
    row = lambda b, j: (b * nj + j, 0)
    const2 = lambda b, j: (0, 0)
    const3 = lambda b, j: (0, 0, 0)
    return pl.pallas_call(
        _lru_prompt_kernel,
        grid=(batch, nj),
        in_specs=[
            pl.BlockSpec((CHUNK_TILE_ROWS, LANES), row),
            pl.BlockSpec((tt, D_MODEL), row),
            pl.BlockSpec((CONV_WIDTH, D_MODEL), const2),
            pl.BlockSpec((1, D_MODEL), const2),
            pl.BlockSpec((LRU_BLOCKS, LRU_BLOCK, LRU_BLOCK), const3),
            pl.BlockSpec((LRU_BLOCKS, LRU_BLOCK, LRU_BLOCK), const3),
            pl.BlockSpec((1, D_MODEL), const2),
            pl.BlockSpec((1, D_MODEL), const2),
            pl.BlockSpec((1, D_MODEL), const2),
        ],
        out_specs=[
            pl.BlockSpec((tt, D_MODEL), row),
            pl.BlockSpec((None, CONV_WIDTH - 1, D_MODEL), lambda b, j: (b, 0, 0)),
            pl.BlockSpec((None, 1, D_MODEL), lambda b, j: (b, 0, 0)),
        ],
        out_shape=[
            jax.ShapeDtypeStruct((batch * seq, D_MODEL), BF16),
            jax.ShapeDtypeStruct((batch, CONV_WIDTH - 1, D_MODEL), F32),
            jax.ShapeDtypeStruct((batch, 1, D_MODEL), F32),
        ],
        scratch_shapes=[
            pltpu.VMEM((tt + SEGS * (CONV_WIDTH - 1), D_MODEL), F32),
            pltpu.VMEM((SEGS * (CONV_WIDTH - 1), D_MODEL), F32),
            pltpu.VMEM((tt, D_MODEL), F32),
            pltpu.VMEM((tt, D_MODEL), F32),
            pltpu.VMEM((CHUNK_TILE_ROWS, LANES), F32),
            pltpu.VMEM((1, D_MODEL), F32),
            pltpu.VMEM((LRU_BLOCKS, LRU_BLOCK, LRU_BLOCK), BF16),
            pltpu.VMEM((LRU_BLOCKS, LRU_BLOCK, LRU_BLOCK), BF16),
        ],
        compiler_params=_params(("arbitrary", "arbitrary"), 40),
        name="lru_prompt",
    )(xb, yb, cw, cb, wrg, wig, brg, big, lam)


def _lru_sample_kernel(xb_ref, yb_ref, cs_ref, h0_ref, cw_ref, cb_ref, wrg_ref, wig_ref,
                       brg_ref, big_ref, lam_ref, m_ref, conv_ref, hlast_ref, wrg_bf, wig_bf, *, steps):
    batch = h0_ref.shape[0]
    _cast_gate_weights(wrg_ref, wig_ref, wrg_bf, wig_bf)
    m_ref[steps * batch:, :] = jnp.zeros((m_ref.shape[0] - steps * batch, D_MODEL), BF16)

    def slab(t, cols):
        if t < CONV_WIDTH - 1:
            return cs_ref[t, :, cols]
        t -= CONV_WIDTH - 1
        first, stop, _ = cols.indices(D_MODEL)
        return _chunk_rows_load(xb_ref, t * batch, batch, range(first // LANES, stop // LANES))

    for n in range(LRU_BLOCKS):
        cols = slice(n * LRU_BLOCK, (n + 1) * LRU_BLOCK)
        h = h0_ref[:, cols]
        for t in range(steps):
            xc = cb_ref[:, cols] + cw_ref[0:1, cols] * slab(t, cols)
            for k in range(1, CONV_WIDTH):
                xc = xc + cw_ref[k:k + 1, cols] * slab(t + k, cols)
            a, u = _lru_gate_block(xc, n, wrg_bf, wig_bf, brg_ref, big_ref, lam_ref)
            h = a * h + u
            rows = slice(t * batch, (t + 1) * batch)
            m_ref[rows, cols] = (h * yb_ref[rows, cols].astype(F32)).astype(BF16)
        hlast_ref[:, cols] = h
    for k in range(CONV_WIDTH - 1):
        conv_ref[k] = slab(steps + k, slice(None))


def _lru_sample(xb, yb, tile, steps, conv_state, h0, cw, cb, wrg, wig, brg, big, lam):
    batch = h0.shape[0]
    tok = pl.BlockSpec((TOK_TILE, D_MODEL), lambda i: (tile, 0))
    tok_chunks = pl.BlockSpec((CHUNK_TILE_ROWS, LANES), lambda i: (tile, 0))
    full = lambda a: pl.BlockSpec(a.shape, lambda i: (0,) * a.ndim)
    small = (conv_state, h0, cw, cb, wrg, wig, brg, big, lam)
    return pl.pallas_call(
        functools.partial(_lru_sample_kernel, steps=steps),
        grid=(1,),
        in_specs=[tok_chunks, tok] + [full(a) for a in small],
        out_specs=[
            pl.BlockSpec((TOK_TILE, D_MODEL), lambda i: (0, 0)),
            pl.BlockSpec((CONV_WIDTH - 1, batch, D_MODEL), lambda i: (0, 0, 0)),
            pl.BlockSpec((batch, D_MODEL), lambda i: (0, 0)),
        ],
        out_shape=[
            jax.ShapeDtypeStruct((TOK_TILE, D_MODEL), BF16),
            jax.ShapeDtypeStruct((CONV_WIDTH - 1, batch, D_MODEL), F32),
            jax.ShapeDtypeStruct((batch, D_MODEL), F32),
        ],
        scratch_shapes=[
            pltpu.VMEM((LRU_BLOCKS, LRU_BLOCK, LRU_BLOCK), BF16),
            pltpu.VMEM((LRU_BLOCKS, LRU_BLOCK, LRU_BLOCK), BF16),
        ],
        compiler_params=_params(("arbitrary",), 32),
        name="lru_sample",
    )(xb, yb, *small)


def _rel_bucket(dist):
    n = jnp.maximum(dist, 0)
    max_exact = N_BUCKETS // 2
    nf = jnp.maximum(n, 1).astype(F32)
    large = max_exact + (jnp.log(nf / max_exact) / math.log(MAX_DISTANCE / max_exact)
                         * (N_BUCKETS - max_exact)).astype(jnp.int32)
    large = jnp.minimum(large, N_BUCKETS - 1)
    return jnp.where(n < max_exact, n, large)


def _masked_buckets(dist):
    valid = (dist >= 0) & (dist < WINDOW)
    return jnp.where(valid, _rel_bucket(dist), -1).astype(jnp.int32)


def _build_bias(bucket, tab_ref, head):
    def body(bi, acc):
        return jnp.where(bucket == bi, tab_ref[bi * N_HEADS + head], acc)
    return lax.fori_loop(0, N_BUCKETS, body, jnp.full(bucket.shape, NEG_INF, F32))


def _softmax_pv(s, sink, v):
    m = jnp.maximum(jnp.max(s, axis=-1, keepdims=True), sink)
    p = jnp.exp(s - m)
    den = jnp.sum(p, axis=-1, keepdims=True) + jnp.exp(sink - m)
    return jnp.dot(p.astype(BF16), v, preferred_element_type=F32) / den


def _attn_prompt_kernel(q_ref, kvp_ref, kvc_ref, bucket_ref, tab_ref, sink_ref, o_ref, bias_s):
    b = pl.program_id(0)
    n = pl.program_id(1)

    @pl.when(jnp.logical_and(b == 0, n == 0))
    def _():
        bucket = bucket_ref[...]

        col = lax.broadcasted_iota(jnp.int32, (WINDOW, 2 * WINDOW), 1)

        def head_body(h, c):
            bias = _build_bias(bucket, tab_ref, h)
            sink = sink_ref[h]
            g = h // GROUP
            r0 = pl.multiple_of((h % GROUP) * WINDOW, WINDOW)
            bias_s[0, g, pl.ds(r0, WINDOW), :] = jnp.where(col == 0, sink, bias)
            bias_s[1, g, pl.ds(r0, WINDOW), :] = jnp.where(
                col == 0, sink, jnp.where(col < WINDOW, NEG_INF, bias))
            return c

        lax.fori_loop(0, N_HEADS, head_body, 0)

    first = (n == 0).astype(jnp.int32)
    row = lax.broadcasted_iota(jnp.int32, kvp_ref.shape, 0)
    kv_prev = jnp.where(row == 0, 0.0, kvp_ref[...])
    kv = jnp.concatenate([kv_prev, kvc_ref[...]], axis=0).astype(BF16)
    ones = jnp.ones((2 * WINDOW, 2 * HEAD_DIM), BF16)
    lane = lax.broadcasted_iota(jnp.int32, (WINDOW, 2 * HEAD_DIM), 1)
    def scores(idx):
        g, pair = divmod(idx, GROUP // 2)
        h0 = g * GROUP + 2 * pair
        kg = kv[:, g * HEAD_DIM:(g + 1) * HEAD_DIM]
        qp = jnp.concatenate([q_ref[:, h * HEAD_DIM:(h + 1) * HEAD_DIM] for h in (h0, h0 + 1)], axis=0)
        s = lax.dot_general(qp, kg, (((1,), (1,)), ((), ())), preferred_element_type=F32)
        return s + bias_s[first, g, 2 * pair * WINDOW:(2 * pair + 2) * WINDOW, :]

    def finish(idx, o_ext):
        h0 = 2 * idx
        o = o_ext[:, :2 * HEAD_DIM] * (1.0 / o_ext[:, 2 * HEAD_DIM:])
        o_ref[:, h0 * HEAD_DIM:(h0 + 2) * HEAD_DIM] = jnp.where(
            lane < HEAD_DIM, o[:WINDOW], o[WINDOW:]).astype(BF16)

    n_pairs = N_HEADS // 2

    def values(idx, p):
        g = idx // (GROUP // 2)
        vg = kv[:, KV_DIM + g * HEAD_DIM:KV_DIM + (g + 1) * HEAD_DIM]
        v_ext = jnp.concatenate([vg, vg, ones], axis=1)
        return jnp.dot(p, v_ext, preferred_element_type=F32)

    for first_pair in range(0, n_pairs, PAIR_BLOCK):
        block = range(first_pair, first_pair + PAIR_BLOCK)
        ss = [scores(idx) for idx in block]
        ms = [jnp.max(s, axis=-1, keepdims=True) for s in ss]
        ps = [jnp.exp(s - m).astype(BF16) for s, m in zip(ss, ms)]
        os_ = [values(idx, p) for idx, p in zip(block, ps)]
        for idx, o_ext in zip(block, os_):
            finish(idx, o_ext)


def _attn_prompt(q, kv, batch, seq, bucket, tab, sinks):
    nb = seq // WINDOW
    smem = pl.BlockSpec(memory_space=pltpu.SMEM)
    return pl.pallas_call(
        _attn_prompt_kernel,
        grid=(batch, nb),
        in_specs=[
            pl.BlockSpec((WINDOW, D_MODEL), lambda b, n: (b * nb + n, 0)),
            pl.BlockSpec((WINDOW, 2 * KV_DIM), lambda b, n: (jnp.maximum(b * nb + n - 1, 0), 0)),
            pl.BlockSpec((WINDOW, 2 * KV_DIM), lambda b, n: (b * nb + n, 0)),
            pl.BlockSpec((WINDOW, 2 * WINDOW), lambda b, n: (0, 0)),
            smem, smem,
        ],
        out_specs=pl.BlockSpec((WINDOW, D_MODEL), lambda b, n: (b * nb + n, 0)),
        out_shape=jax.ShapeDtypeStruct((batch * seq, D_MODEL), BF16),
        scratch_shapes=[pltpu.VMEM((2, N_KV_HEADS, GROUP * WINDOW, 2 * WINDOW), F32)],
        compiler_params=_params(("arbitrary", "arbitrary"), 32),
        name="attn_prompt",
    )(q, kv, kv, bucket, tab, sinks)


def _attn_sample_kernel(q_ref, k_ref, v_ref, bucket_ref, tab_ref, sink_ref, o_ref, bias_s):
    steps = q_ref.shape[0]

    @pl.when(pl.program_id(0) == 0)
    def _():
        bucket = bucket_ref[...]

        def head_body(h, c):
            bias_s[h] = _build_bias(bucket, tab_ref, h)
            return c

        lax.fori_loop(0, N_HEADS, head_body, 0)

    rows = lax.broadcasted_iota(jnp.int32, (GROUP * steps, 1), 0)
    k = k_ref[...].astype(BF16)
    v = v_ref[...].astype(BF16)
    groups = range(N_KV_HEADS)

    def heads_of(g):
        return range(g * GROUP, (g + 1) * GROUP)

    scores, sinks = [], []
    for g in groups:
        qg = jnp.concatenate([q_ref[:, h * HEAD_DIM:(h + 1) * HEAD_DIM] for h in heads_of(g)], axis=0)
        bias = jnp.concatenate([bias_s[h] for h in heads_of(g)], axis=0)
        sink = jnp.full((GROUP * steps, 1), sink_ref[g * GROUP], F32)
        for hh in range(1, GROUP):
            sink = jnp.where(rows >= hh * steps, sink_ref[g * GROUP + hh], sink)
        kg = k[:, g * HEAD_DIM:(g + 1) * HEAD_DIM]
        scores.append(lax.dot_general(qg, kg, (((1,), (1,)), ((), ())),
                                      preferred_element_type=F32) + bias)
        sinks.append(sink)
    outs = [_softmax_pv(scores[g], sinks[g], v[:, g * HEAD_DIM:(g + 1) * HEAD_DIM]) for g in groups]
    for g in groups:
        for hh, h in enumerate(heads_of(g)):
            o_ref[:, h * HEAD_DIM:(h + 1) * HEAD_DIM] = outs[g][hh * steps:(hh + 1) * steps].astype(BF16)


def _attn_sample(q, k_all, v_all, bucket, tab, sinks):
    batch, steps, _ = q.shape
    lk = k_all.shape[1]
    smem = pl.BlockSpec(memory_space=pltpu.SMEM)
    return pl.pallas_call(
        _attn_sample_kernel,
        grid=(batch,),
        in_specs=[
            pl.BlockSpec((None, steps, D_MODEL), lambda b: (b, 0, 0)),
            pl.BlockSpec((None, lk, KV_DIM), lambda b: (b, 0, 0)),
            pl.BlockSpec((None, lk, KV_DIM), lambda b: (b, 0, 0)),
            pl.BlockSpec((steps, lk), lambda b: (0, 0)),
            smem, smem,
        ],
        out_specs=pl.BlockSpec((None, steps, D_MODEL), lambda b: (b, 0, 0)),
        out_shape=jax.ShapeDtypeStruct((batch, steps, D_MODEL), BF16),
        scratch_shapes=[pltpu.VMEM((N_HEADS, steps, lk), F32)],
        compiler_params=_params(("arbitrary",), 32),
        name="attn_sample",
    )(q, k_all, v_all, bucket, tab, sinks)


def kernel(x_prompt, x_sample, state_conv, state_rnn, cache_k_win, cache_v_win, ln_g, ln_b, lru_w_x, lru_b_x, lru_w_y, lru_b_y, lru_conv_w, lru_conv_b, lru_w_rg, lru_b_rg, lru_w_ig, lru_b_ig, lru_lam, lru_w_out, lru_b_out, attn_w_kv, attn_w_q, attn_w_o, attn_sinks, rel_bias, moe_w_router, moe_b_router, moe_w_gate, moe_w_up, moe_w_down):
    bp, seq, _ = x_prompt.shape
    bs, steps, _ = x_sample.shape
    n_p = bp * seq
    n_s = bs * steps

    assert n_p % TOK_TILE == 0 and n_s <= TOK_TILE
    sample_tile = n_p // TOK_TILE

    def pad_tile(rows):
        return jnp.pad(rows, ((0, TOK_TILE - n_s), (0, 0)))

    x0 = (x_prompt.reshape(n_p, D_MODEL),
          pad_tile(x_sample.transpose(1, 0, 2).reshape(n_s, D_MODEL)))
    wr_t = moe_w_router.T
    br = moe_b_router.reshape(N_EXPERTS, 1)
    vec = lambda a: a.reshape(1, -1)

    xb = _linear(x0, lru_w_x, (0,), vec(lru_b_x[0]), F32, name="lru_in_x", chunk_rows=True)
    yb = _linear(x0, lru_w_y, (0,), vec(lru_b_y[0]), BF16, act="gelu", name="lru_in_y")
    lru_args = (lru_conv_w[0], vec(lru_conv_b[0]), lru_w_rg[0], lru_w_ig[0],
                vec(lru_b_rg[0]), vec(lru_b_ig[0]), vec(lru_lam[0]))
    m_p, conv_p, rnn_p = _lru_prompt(xb, yb, bp, seq, *lru_args)
    m_s, conv_s, rnn_s = _lru_sample(xb, yb, sample_tile, steps,
                                     state_conv[0].transpose(1, 0, 2), state_rnn[0], *lru_args)
    x1, x1_rows, e_idx, gates = _proj_ln((m_p, m_s), lru_w_out, (0,), vec(lru_b_out[0]), x0,
                                vec(ln_g[0, 0]), vec(ln_b[0, 0]), wr_t, br, name="lru_out_ln")
    ys = _moe_block(x1_rows, e_idx, moe_w_gate, moe_w_up, moe_w_down, 0)

    x2, q, kv = _combine_qkv(ys, x1, gates.T, vec(ln_g[0, 1]), vec(ln_b[0, 1]),
                             attn_w_q, (0,), attn_w_kv, name="moe_combine_qkv")
    tab = rel_bias.reshape(-1)
    sinks = attn_sinks[0]
    qi = jnp.arange(WINDOW)[:, None]
    kj = jnp.arange(2 * WINDOW)[None, :]
    o_p = _attn_prompt(q, kv, bp, seq, _masked_buckets(qi + WINDOW - kj), tab, sinks)
    kv_s = kv[n_p:n_p + n_s].reshape(steps, bs, 2, KV_DIM).transpose(2, 1, 0, 3)
    k_all = jnp.concatenate([cache_k_win.reshape(bs, WINDOW, KV_DIM), kv_s[0]], axis=1)
    v_all = jnp.concatenate([cache_v_win.reshape(bs, WINDOW, KV_DIM), kv_s[1]], axis=1)
    dist_s = jnp.arange(steps)[:, None] + WINDOW - jnp.arange(WINDOW + steps)[None, :]
    q_s = q[n_p:n_p + n_s].reshape(steps, bs, D_MODEL).transpose(1, 0, 2)
    o_s = _attn_sample(q_s, k_all, v_all, _masked_buckets(dist_s), tab, sinks)
    o_s = pad_tile(o_s.transpose(1, 0, 2).reshape(n_s, D_MODEL))
    x3, x3_rows, e_idx, gates = _proj_ln((o_p, o_s), attn_w_o, (0,), jnp.zeros((1, D_MODEL), F32), x2,
                                vec(ln_g[1, 0]), vec(ln_b[1, 0]), wr_t, br, name="attn_out_ln")
    ys = _moe_block(x3_rows, e_idx, moe_w_gate, moe_w_up, moe_w_down, 1)
    y_p, y_s = _combine_split(ys, x3, gates.T, vec(ln_g[1, 1]), vec(ln_b[1, 1]), name="moe_combine_1")

    y_prompt = y_p.reshape(bp, seq, D_MODEL)
    y_sample = y_s[:n_s].reshape(steps, bs, D_MODEL).transpose(1, 0, 2)
    kv_p = jnp.stack([kv[(b + 1) * seq - WINDOW:(b + 1) * seq] for b in range(bp)])
    kv_p = kv_p.reshape(bp, WINDOW, 2, N_KV_HEADS, HEAD_DIM)
    k_win_s = k_all[:, steps:].reshape(bs, WINDOW, N_KV_HEADS, HEAD_DIM)
    v_win_s = v_all[:, steps:].reshape(bs, WINDOW, N_KV_HEADS, HEAD_DIM)
    return (y_prompt, y_sample,
            conv_p[None], rnn_p.reshape(1, bp, D_MODEL),
            kv_p[:, :, 0], kv_p[:, :, 1],
            conv_s.transpose(1, 0, 2)[None], rnn_s[None],
            k_win_s, v_win_s)
```

```python
import functools
import math

import jax
import jax.numpy as jnp
from jax import lax
from jax.experimental import pallas as pl
from jax.experimental.pallas import tpu as pltpu

D_MODEL = 2048
DEPTH = 2
LRU_BLOCKS = 8
LRU_BLOCK = D_MODEL // LRU_BLOCKS
CONV_WIDTH = 4
LRU_C = 8.0
N_HEADS = 32
HEAD_DIM = 64
N_KV_HEADS = 8
GROUP = N_HEADS // N_KV_HEADS
KV_DIM = N_KV_HEADS * HEAD_DIM
WINDOW = 128
N_BUCKETS = 32
MAX_DISTANCE = 128
N_EXPERTS = 16
N_GROUPS = 4
EXPERTS_PER_GROUP = N_EXPERTS // N_GROUPS
D_EXPERT = 1024
ALPHA = (2 * DEPTH) ** 0.25
LN_EPS = 1e-5

LANES = 128
SEGS = 8
CHUNKS = D_MODEL // LANES
ROW_SUBLANES = D_MODEL // (2 * LANES)
MOE_TILE = 256
TOK_TILE = 256
CHUNK_TOK_ROWS = CHUNKS + 4
CHUNK_SEG_TOKS = TOK_TILE // SEGS
CHUNK_SEG_ROWS = CHUNK_SEG_TOKS * CHUNK_TOK_ROWS + 4
CHUNK_TILE_ROWS = SEGS * CHUNK_SEG_ROWS
DMA_UNROLL = 8
SCALAR_UNROLL = 32
PLAN_UNROLL = 4
PAIR_BLOCK = 4
GATE_BLOCKS = 4
W_CHUNKS = 4
CAST_ROWS = 256
BF16 = jnp.bfloat16
F32 = jnp.float32
NEG_INF = float("-inf")


def _params(sem, vmem_mb):
    return pltpu.CompilerParams(dimension_semantics=sem, vmem_limit_bytes=vmem_mb * 1024 * 1024)


def _cast_rows(src_ref, dst_ref):
    n = src_ref.shape[0] // CAST_ROWS

    def body(i, c):
        r = pl.multiple_of(i * CAST_ROWS, CAST_ROWS)
        dst_ref[pl.ds(r, CAST_ROWS), :] = src_ref[pl.ds(r, CAST_ROWS), :].astype(BF16)
        return c

    lax.fori_loop(0, n, body, 0)


def _layer_norm(z, g, b):
    mu = jnp.mean(z, axis=-1, keepdims=True)
    zc = z - mu
    var = jnp.mean(zc * zc, axis=-1, keepdims=True)
    return zc * lax.rsqrt(var + LN_EPS) * g + b


def _chunk_row(tok, chunk):
    seg, t = divmod(tok, CHUNK_SEG_TOKS)
    return seg * CHUNK_SEG_ROWS + t * CHUNK_TOK_ROWS + chunk


def _chunk_rows_store(ref, y):
    ref[...] = jnp.zeros_like(ref)
    for seg in range(SEGS):
        rows = slice(seg * CHUNK_SEG_TOKS, (seg + 1) * CHUNK_SEG_TOKS)
        for c in range(CHUNKS):
            dst = pl.ds(_chunk_row(seg * CHUNK_SEG_TOKS, c), CHUNK_SEG_TOKS, stride=CHUNK_TOK_ROWS)
            ref[dst, :] = y[rows, c * LANES:(c + 1) * LANES]


def _chunk_rows_load(ref, tok0, count, chunks=range(CHUNKS)):
    assert tok0 // CHUNK_SEG_TOKS == (tok0 + count - 1) // CHUNK_SEG_TOKS
    return jnp.concatenate(
        [ref[pl.ds(_chunk_row(tok0, c), count, stride=CHUNK_TOK_ROWS), :] for c in chunks], axis=1)


def _store_packed_rows(x_bf, rows_ref):
    n = x_bf.shape[0]
    bits = pltpu.bitcast(x_bf.astype(F32), jnp.uint32)
    packed = bits[:, D_MODEL // 2:] | (bits[:, :D_MODEL // 2] >> 16)
    for c in range(ROW_SUBLANES):
        rows_ref[pl.ds(c, n, stride=ROW_SUBLANES), :] = packed[:, c * LANES:(c + 1) * LANES]


def _load_packed_rows(rows_ref, x_bf_ref):
    n = x_bf_ref.shape[0]
    for c in range(ROW_SUBLANES):
        words = rows_ref[pl.ds(c, n, stride=ROW_SUBLANES), :]
        low = pltpu.bitcast(words << 16, F32).astype(BF16)
        high = pltpu.bitcast(words & jnp.uint32(0xFFFF0000), F32).astype(BF16)
        x_bf_ref[:, c * LANES:(c + 1) * LANES] = low
        x_bf_ref[:, D_MODEL // 2 + c * LANES:D_MODEL // 2 + (c + 1) * LANES] = high


def _tok_operands(x, tile_of=lambda i: i):
    if isinstance(x, tuple):
        xp, xs = x
        d = xp.shape[1]
        last_p = xp.shape[0] // TOK_TILE - 1
        specs = [pl.BlockSpec((TOK_TILE, d), lambda i, *_: (jnp.minimum(tile_of(i), last_p), 0)),
                 pl.BlockSpec((TOK_TILE, d), lambda i, *_: (0, 0))]
        return [xp, xs], specs, last_p + 2
    return ([x], [pl.BlockSpec((TOK_TILE, x.shape[1]), lambda i, *_: (tile_of(i), 0))],
            x.shape[0] // TOK_TILE)


def _tok_load(refs, is_sample=None):
    if len(refs) == 1:
        return refs[0][...]
    if is_sample is None:
        is_sample = pl.program_id(0) == pl.num_programs(0) - 1
    return jnp.where(is_sample, refs[1][...], refs[0][...])


def _lru_in_kernel(*refs, n_x, layer):
    x_refs = refs[:n_x]
    wx_hbm, wy_hbm, bx_ref, by_ref, xb_ref, yb_ref, wx_bf, wy_bf, stage, sems = refs[n_x:]

    @pl.when(pl.program_id(0) == 0)
    def _():
        _load_weight(wx_hbm.at[layer], wx_bf, stage, sems)
        _load_weight(wy_hbm.at[layer], wy_bf, stage, sems)

    x = _tok_load(x_refs).astype(BF16)
    _chunk_rows_store(xb_ref, jnp.dot(x, wx_bf[...], preferred_element_type=F32) + bx_ref[...])
    y = jnp.dot(x, wy_bf[...], preferred_element_type=F32) + by_ref[...]
    yb_ref[...] = jax.nn.gelu(y).astype(BF16)


def _lru_in(x, w_x, b_x, w_y, b_y, layer):
    arrays, specs, nt = _tok_operands(x)
    hbm = pl.BlockSpec(memory_space=pl.ANY)
    vec_spec = pl.BlockSpec((1, D_MODEL), lambda i: (0, 0))
    return pl.pallas_call(
        functools.partial(_lru_in_kernel, n_x=len(arrays), layer=layer),
        grid=(nt,),
        in_specs=specs + [hbm, hbm, vec_spec, vec_spec],
        out_specs=[pl.BlockSpec((CHUNK_TILE_ROWS, LANES), lambda i: (i, 0)),
                   pl.BlockSpec((TOK_TILE, D_MODEL), lambda i: (i, 0))],
        out_shape=[jax.ShapeDtypeStruct((nt * CHUNK_TILE_ROWS, LANES), F32),
                   jax.ShapeDtypeStruct((nt * TOK_TILE, D_MODEL), BF16)],
        scratch_shapes=[pltpu.VMEM((D_MODEL, D_MODEL), BF16), pltpu.VMEM((D_MODEL, D_MODEL), BF16),
                        pltpu.VMEM((2, CAST_ROWS, D_MODEL), F32), pltpu.SemaphoreType.DMA((2,))],
        compiler_params=_params(("arbitrary",), 48),
        name="lru_in",
    )(*arrays, w_x, w_y, b_x, b_y)


def _route(logits_t, b_router):
    aff = jax.nn.sigmoid(logits_t)
    sel = aff + b_router
    srow = [sel[e:e + 1, :] for e in range(N_EXPERTS)]
    arow = [aff[e:e + 1, :] for e in range(N_EXPERTS)]

    def top2_sum(v):
        pairs = [v[i] + v[j] for i in range(4) for j in range(i + 1, 4)]
        return functools.reduce(jnp.maximum, pairs)

    scores = [top2_sum(srow[4 * g:4 * g + 4]) for g in range(N_GROUPS)]
    best = scores[0]
    gi = jnp.zeros_like(best, dtype=jnp.int32)
    for g in range(1, N_GROUPS):
        upd = scores[g] > best
        best = jnp.where(upd, scores[g], best)
        gi = jnp.where(upd, g, gi)

    def pick_group(rows, j):
        out = rows[j]
        for g in range(1, N_GROUPS):
            out = jnp.where(gi == g, rows[4 * g + j], out)
        return out

    v = [pick_group(srow, j) for j in range(EXPERTS_PER_GROUP)]
    a = [pick_group(arow, j) for j in range(EXPERTS_PER_GROUP)]

    m1, i1 = v[0], jnp.zeros_like(gi)
    for j in range(1, EXPERTS_PER_GROUP):
        upd = v[j] > m1
        m1 = jnp.where(upd, v[j], m1)
        i1 = jnp.where(upd, j, i1)
    m2 = jnp.full_like(m1, NEG_INF)
    i2 = jnp.zeros_like(gi)
    for j in range(EXPERTS_PER_GROUP):
        cand = jnp.where(i1 == j, NEG_INF, v[j])
        upd = cand > m2
        m2 = jnp.where(upd, cand, m2)
        i2 = jnp.where(upd, j, i2)

    def pick_idx(rows, idx):
        out = rows[0]
        for j in range(1, EXPERTS_PER_GROUP):
            out = jnp.where(idx == j, rows[j], out)
        return out

    a1 = pick_idx(a, i1)
    a2 = pick_idx(a, i2)
    tot = a1 + a2
    e_idx = jnp.concatenate([gi * EXPERTS_PER_GROUP + i1, gi * EXPERTS_PER_GROUP + i2], axis=0)
    gates = jnp.concatenate([a1 / tot, a2 / tot], axis=0)
    return e_idx, gates


def _proj_ln_kernel(*refs, n_m, n_res):
    m_refs = refs[:n_m]
    w_ref, b_ref = refs[n_m:n_m + 2]
    res_refs = refs[n_m + 2:n_m + 2 + n_res]
    (g_ref, beta_ref, wr_ref, br_ref, x_ref, xrow_ref, e_ref, gate_ref,
     wbf_ref, ya, yb) = refs[n_m + 2 + n_res:]
    i = pl.program_id(0)
    n_tiles = pl.num_programs(0) - 1

    @pl.when(i == 0)
    def _():
        _cast_rows(w_ref, wbf_ref)
        yb[...] = jnp.zeros_like(yb)

    for parity, (cur, prev) in enumerate(((ya, yb), (yb, ya))):
        @pl.when(i % 2 == parity)
        def _():
            cur[...] = jnp.dot(_tok_load(m_refs, i >= n_tiles - 1), wbf_ref[...],
                               preferred_element_type=F32)
            y = prev[...] + b_ref[...]
            x = _layer_norm(ALPHA * _tok_load(res_refs, i == n_tiles) + y, g_ref[...], beta_ref[...])
            x_ref[...] = x
            x_bf = x.astype(BF16)
            _store_packed_rows(x_bf, xrow_ref)
            logits_t = lax.dot_general(wr_ref[...].astype(BF16), x_bf,
                                       (((1,), (1,)), ((), ())), preferred_element_type=F32)
            e_idx, gates = _route(logits_t, br_ref[...])
            e_ref[...] = e_idx
            gate_ref[...] = gates


def _proj_ln(m, w, w_index, b, res, g, beta, wr_t, br, *, name):
    nt = _tok_operands(m)[2]
    m_arrays, m_specs, _ = _tok_operands(m, lambda i: jnp.minimum(i, nt - 1))
    res_arrays, res_specs, _ = _tok_operands(res, lambda i: jnp.maximum(i - 1, 0))
    k = w.shape[-2]
    tm = TOK_TILE
    n = nt * tm
    row = lambda i: (jnp.maximum(i - 1, 0), 0)
    const = lambda i: (0, 0)
    x, x_rows, e_idx, gates = pl.pallas_call(
        functools.partial(_proj_ln_kernel, n_m=len(m_arrays), n_res=len(res_arrays)),
        grid=(nt + 1,),
        in_specs=m_specs + [
            pl.BlockSpec((None,) * len(w_index) + (k, D_MODEL), lambda i: w_index + (0, 0),
                         pipeline_mode=pl.Buffered(1)),
            pl.BlockSpec((1, D_MODEL), const),
        ] + res_specs + [
            pl.BlockSpec((1, D_MODEL), const),
            pl.BlockSpec((1, D_MODEL), const),
            pl.BlockSpec((N_EXPERTS, D_MODEL), const),
            pl.BlockSpec((N_EXPERTS, 1), const),
        ],
        out_specs=[
            pl.BlockSpec((tm, D_MODEL), row),
            pl.BlockSpec((tm * ROW_SUBLANES, LANES), row),
            pl.BlockSpec((None, 2, tm), lambda i: (jnp.maximum(i - 1, 0), 0, 0)),
            pl.BlockSpec((None, 2, tm), lambda i: (jnp.maximum(i - 1, 0), 0, 0)),
        ],
        out_shape=[
            jax.ShapeDtypeStruct((n, D_MODEL), F32),
            jax.ShapeDtypeStruct((n * ROW_SUBLANES, LANES), jnp.uint32),
            jax.ShapeDtypeStruct((nt, 2, tm), jnp.int32),
            jax.ShapeDtypeStruct((nt, 2, tm), F32),
        ],
        scratch_shapes=[pltpu.VMEM((k, D_MODEL), BF16),
                        pltpu.VMEM((tm, D_MODEL), F32), pltpu.VMEM((tm, D_MODEL), F32)],
        compiler_params=_params(("arbitrary",), 52),
        name=name,
    )(*m_arrays, w, b, *res_arrays, g, beta, wr_t, br)
    e_idx = e_idx.transpose(1, 0, 2).reshape(2, n)
    gates = gates.transpose(1, 0, 2).reshape(2, n)
    return x, x_rows, e_idx, gates


def _plan_kernel(e_ref, pos_ref, meta_ref, rank_ref):
    nrow = e_ref.shape[0]
    ri = lax.broadcasted_iota(jnp.int32, (LANES, LANES), 0)
    ci = lax.broadcasted_iota(jnp.int32, (LANES, LANES), 1)
    tri = jnp.where(ri <= ci, 1.0, 0.0).astype(BF16)
    sub = lax.broadcasted_iota(jnp.int32, (N_EXPERTS, LANES), 0)

    def count_body(b, base):
        rows = [b * PLAN_UNROLL + u for u in range(PLAN_UNROLL)]
        onehots = [sub == e_ref[pl.ds(r, 1), :] for r in rows]
        locs = [jnp.dot(jnp.where(oh, 1.0, 0.0).astype(BF16), tri, preferred_element_type=F32)
                for oh in onehots]
        for r, onehot, loc in zip(rows, onehots, locs):
            rank_ref[pl.ds(r, 1), :] = jnp.sum(jnp.where(onehot, base + loc - 1.0, 0.0),
                                               axis=0, keepdims=True)
            base = base + jnp.broadcast_to(loc[:, LANES - 1:LANES], (N_EXPERTS, LANES))
        return base

    count = lax.fori_loop(0, nrow // PLAN_UNROLL, count_body, jnp.zeros((N_EXPERTS, LANES), F32))
    ntile = jnp.floor((count + (MOE_TILE - 1.0)) * (1.0 / MOE_TILE))
    offs = []
    acc = jnp.zeros((1, LANES), F32)
    for e in range(N_EXPERTS):
        offs.append(acc)
        acc = acc + ntile[e:e + 1, :]
    tile_off = jnp.concatenate(offs, axis=0)
    tile_end = tile_off + ntile
    lane = lax.broadcasted_iota(jnp.int32, (N_EXPERTS, LANES), 1).astype(F32)
    tile_expert = jnp.sum(jnp.where(tile_end <= lane, 1.0, 0.0), axis=0, keepdims=True)
    tile_expert = jnp.minimum(tile_expert, N_EXPERTS - 1.0)
    own = jnp.logical_and(tile_off <= lane, lane < tile_end)
    run_end = jnp.sum(jnp.where(own, tile_end, 0.0), axis=0, keepdims=True)
    next_expert = jnp.sum(jnp.where(tile_end <= run_end, 1.0, 0.0), axis=0, keepdims=True)
    has_next = jnp.logical_and(lane[0:1, :] < acc, run_end < acc)
    next_expert = jnp.where(has_next, next_expert, -1.0)
    meta = jnp.concatenate([tile_expert, acc, next_expert, jnp.zeros((5, LANES), F32)], axis=0)
    meta_ref[...] = meta.astype(jnp.int32)
    row_off = tile_off * float(MOE_TILE)

    def pos_body(r, c):
        onehot = sub == e_ref[pl.ds(r, 1), :]
        p = jnp.sum(jnp.where(onehot, row_off, 0.0), axis=0, keepdims=True) + rank_ref[pl.ds(r, 1), :]
        pos_ref[pl.ds(r, 1), :] = p.astype(jnp.int32)
        return c

    lax.fori_loop(0, nrow, pos_body, 0)


def _plan(e_idx, *, name):
    n2 = e_idx.shape[0] * e_idx.shape[1]
    assert n2 % (LANES * PLAN_UNROLL) == 0
    e2d = e_idx.reshape(n2 // LANES, LANES)
    pos, meta = pl.pallas_call(
        _plan_kernel,
        out_shape=[jax.ShapeDtypeStruct(e2d.shape, jnp.int32),
                   jax.ShapeDtypeStruct((8, LANES), jnp.int32)],
        scratch_shapes=[pltpu.VMEM(e2d.shape, F32)],
        name=name,
    )(e2d)
    return pos.reshape(n2), meta[0], meta[1, :1], meta[2]


def _invert_kernel(pos_ref, pair_ref):
    n_rows = pair_ref.shape[0]
    n_pairs = pos_ref.shape[0]

    def fill_body(b, c):
        for u in range(SCALAR_UNROLL):
            pair_ref[b * SCALAR_UNROLL + u] = -1
        return c

    def pair_body(b, c):
        rows = [pos_ref[b * SCALAR_UNROLL + u] for u in range(SCALAR_UNROLL)]
        for u in range(SCALAR_UNROLL):
            pair_ref[rows[u]] = b * SCALAR_UNROLL + u
        return c

    lax.fori_loop(0, n_rows // SCALAR_UNROLL, fill_body, 0)
    lax.fori_loop(0, n_pairs // SCALAR_UNROLL, pair_body, 0)


def _invert(pos, n_rows, *, name):
    return pl.pallas_call(
        _invert_kernel,
        grid_spec=pltpu.PrefetchScalarGridSpec(
            num_scalar_prefetch=1,
            grid=(1,),
            in_specs=[],
            out_specs=pl.BlockSpec(memory_space=pltpu.SMEM),
        ),
        out_shape=jax.ShapeDtypeStruct((n_rows,), jnp.int32),
        name=name,
    )(pos)


def _expert_changed(te_ref, i):
    return jnp.logical_or(i == 0, te_ref[i] != te_ref[jnp.maximum(i - 1, 0)])


class _ExpertWeights:
    def __init__(self, mats, layer, st, sems):
        self.mats, self.layer, self.st, self.sems = mats, layer, st, sems

    def _copies(self, expert, c):
        out = []
        for w_hbm, stage, _, _ in self.mats:
            rows = stage.shape[1]
            src = w_hbm.at[self.layer, expert, pl.ds(pl.multiple_of(c * rows, rows), rows), :]
            out.append(pltpu.make_async_copy(src, stage.at[c % 2], self.sems.at[c % 2]))
        return out

    def _start(self, expert, c):
        for cp in self._copies(expert, c):
            cp.start()
        self.st[2] = c + 1

    def _convert(self, c_src, c_dst):
        for _, stage, w_next, _ in self.mats:
            rows = stage.shape[1]
            dst = pl.ds(pl.multiple_of(c_dst * rows, rows), rows)
            w_next[dst, :] = stage[c_src % 2].astype(BF16)

    def reset(self):
        for _, stage, _, _ in self.mats:
            stage[...] = jnp.zeros_like(stage)
        self.st[1] = 0
        self.st[2] = 0

    def switch_to(self, expert):
        st = self.st

        def body(c, carry):
            @pl.when(c >= st[2])
            def _():
                self._start(expert, c)

            @pl.when(jnp.logical_and(c + 1 < W_CHUNKS, c + 1 >= st[2]))
            def _():
                self._start(expert, c + 1)

            for cp in self._copies(expert, c):
                cp.wait()
            self._convert(c, c)
            return carry

        lax.fori_loop(st[1], W_CHUNKS, body, 0)
        for _, _, w_next, w_cur in self.mats:
            _copy_rows(w_next, w_cur)
        st[1] = 0
        st[2] = 0

    def begin_step(self, next_expert):
        st = self.st
        done, issued = st[1], st[2]
        has_next = next_expert >= 0
        active = jnp.logical_and(has_next, done < issued)

        @pl.when(jnp.logical_and(has_next, jnp.logical_and(issued < W_CHUNKS, issued < done + 2)))
        def _():
            self._start(next_expert, issued)

        @pl.when(active)
        def _():
            for cp in self._copies(next_expert, done):
                cp.wait()

        return active, done

    def convert_step(self, active, done):
        self._convert(jnp.where(active, done, done + 1), jnp.where(active, done, W_CHUNKS))

    def end_step(self, active, done):
        @pl.when(active)
        def _():
            self.st[1] = done + 1


def _copy_rows(src_ref, dst_ref):
    n = dst_ref.shape[0] // CAST_ROWS

    def body(i, c):
        r = pl.multiple_of(i * CAST_ROWS, CAST_ROWS)
        dst_ref[pl.ds(r, CAST_ROWS), :] = src_ref[pl.ds(r, CAST_ROWS), :]
        return c

    lax.fori_loop(0, n, body, 0)


def _moe_tile_kernel(pair_ref, te_ref, nu_ref, nxt_ref, x_hbm, wg_hbm, wu_hbm, wd_hbm, out_hbm,
                     wg_cur, wg_next, wg_stage, wu_cur, wu_next, wu_stage, wd_cur, wd_next, wd_stage,
                     x_rows, x_bf, ya, yb, gsem, ssems, tsem, wsems, st, *, n_tok, n_tiles, layer):
    i = pl.program_id(0)
    nu = nu_ref[0]
    running = i < nu
    trash = 2 * n_tok
    weights = _ExpertWeights([(wg_hbm, wg_stage, wg_next, wg_cur), (wu_hbm, wu_stage, wu_next, wu_cur),
                              (wd_hbm, wd_stage, wd_next, wd_cur)], layer, st, wsems)

    def gather_copy(tile, r):
        p = pair_ref[tile * MOE_TILE + r]
        tok = jnp.where(p >= n_tok, p - n_tok, jnp.maximum(p, 0))
        src = pl.ds(pl.multiple_of(tok * ROW_SUBLANES, ROW_SUBLANES), ROW_SUBLANES)
        return pltpu.make_async_copy(x_hbm.at[src, :], x_rows.at[pl.ds(r * ROW_SUBLANES, ROW_SUBLANES), :],
                                     gsem)

    def scatter_copy(tile, r, buf, sem):
        p = pair_ref[tile * MOE_TILE + r]
        dst = jnp.where(p < 0, trash + r, p)
        return pltpu.make_async_copy(buf.at[pl.ds(r, 1), :], out_hbm.at[pl.ds(dst, 1), :], sem)

    def wait_scatter(buf, sem):
        pltpu.make_async_copy(buf, out_hbm.at[pl.ds(0, MOE_TILE), :], sem).wait()

    @pl.when(i == 0)
    def _():
        def body(rb, c):
            for u in range(DMA_UNROLL):
                gather_copy(0, rb * DMA_UNROLL + u).start()
            return c
        lax.fori_loop(0, MOE_TILE // DMA_UNROLL, body, 0)
        yb[...] = jnp.zeros_like(yb)
        fill = pltpu.make_async_copy(yb, out_hbm.at[pl.ds(trash, MOE_TILE), :], tsem)
        fill.start()
        fill.wait()
        weights.reset()

    @pl.when(i <= nu)
    def _():
        pltpu.make_async_copy(x_hbm.at[pl.ds(0, MOE_TILE * ROW_SUBLANES), :], x_rows, gsem).wait()

    @pl.when(jnp.logical_and(running, _expert_changed(te_ref, i)))
    def _():
        weights.switch_to(te_ref[i])

    active, done = weights.begin_step(jnp.where(running, nxt_ref[i], -1))

    for parity, (cur, prev) in enumerate(((ya, yb), (yb, ya))):
        cur_sem, prev_sem = ssems.at[parity], ssems.at[1 - parity]
        mine = i % 2 == parity

        @pl.when(jnp.logical_and(mine, jnp.logical_and(i >= 1, i - 1 <= nu)))
        def _():
            wait_scatter(cur, cur_sem)

        @pl.when(jnp.logical_and(mine, running))
        def _():
            _load_packed_rows(x_rows, x_bf)
            next_tile = jnp.minimum(i + 1, n_tiles - 1)
            prev_tile = jnp.maximum(i - 1, 0)
            for r in range(MOE_TILE):
                gather_copy(next_tile, r).start(priority=r % 2)
            for r in range(MOE_TILE):
                scatter_copy(prev_tile, r, prev, prev_sem).start(priority=r % 2)
            weights.convert_step(active, done)
            x = x_bf[...]
            a = jnp.dot(x, wg_cur[...], preferred_element_type=F32)
            b = jnp.dot(x, wu_cur[...], preferred_element_type=F32)
            h = (jax.nn.silu(a) * b).astype(BF16)
            cur[...] = jnp.dot(h, wd_cur[...], preferred_element_type=F32)

        @pl.when(jnp.logical_and(mine, i == nu))
        def _():
            def body(rb, c):
                for u in range(DMA_UNROLL):
                    scatter_copy(i - 1, rb * DMA_UNROLL + u, prev, prev_sem).start()
                return c
            lax.fori_loop(0, MOE_TILE // DMA_UNROLL, body, 0)

            @pl.when(i == n_tiles)
            def _():
                wait_scatter(prev, prev_sem)

    weights.end_step(active, done)


def _moe_tiles(x, pair, tile_expert, n_used, next_expert, w_gate, w_up, w_down, layer):
    n_tok = x.shape[0] // ROW_SUBLANES
    n_tiles = pair.shape[0] // MOE_TILE
    hbm = pl.BlockSpec(memory_space=pl.ANY)

    def weight_bufs(k, n):
        chunk = k // W_CHUNKS
        return [pltpu.VMEM((k, n), BF16), pltpu.VMEM((k + chunk, n), BF16), pltpu.VMEM((2, chunk, n), F32)]

    return pl.pallas_call(
        functools.partial(_moe_tile_kernel, n_tok=n_tok, n_tiles=n_tiles, layer=layer),
        grid_spec=pltpu.PrefetchScalarGridSpec(
            num_scalar_prefetch=4,
            grid=(n_tiles + 1,),
            in_specs=[hbm, hbm, hbm, hbm],
            out_specs=hbm,
            scratch_shapes=weight_bufs(D_MODEL, D_EXPERT) + weight_bufs(D_MODEL, D_EXPERT)
            + weight_bufs(D_EXPERT, D_MODEL) + [
                pltpu.VMEM((MOE_TILE * ROW_SUBLANES, LANES), jnp.uint32),
                pltpu.VMEM((MOE_TILE, D_MODEL), BF16),
                pltpu.VMEM((MOE_TILE, D_MODEL), F32), pltpu.VMEM((MOE_TILE, D_MODEL), F32),
                pltpu.SemaphoreType.DMA(()), pltpu.SemaphoreType.DMA((2,)),
                pltpu.SemaphoreType.DMA(()), pltpu.SemaphoreType.DMA((2,)),
                pltpu.SMEM((4,), jnp.int32)],
        ),
        out_shape=jax.ShapeDtypeStruct((2 * n_tok + MOE_TILE, D_MODEL), F32),
        compiler_params=_params(("arbitrary",), 56),
        name=f"moe_tiles_{layer}",
    )(pair, tile_expert, n_used, next_expert, x, w_gate, w_up, w_down)


def _combine(y0_ref, y1_ref, res_ref, gate_ref, g_ref, beta_ref):
    gate = gate_ref[...]
    ffn = gate[:, 0:1] * y0_ref[...] + gate[:, 1:2] * y1_ref[...]
    return _layer_norm(ALPHA * res_ref[...] + ffn, g_ref[...], beta_ref[...])


def _combine_specs(n):
    nt = n // TOK_TILE
    row = lambda i: (i, 0)
    const = lambda i: (0, 0)
    return [pl.BlockSpec((TOK_TILE, D_MODEL), row),
            pl.BlockSpec((TOK_TILE, D_MODEL), lambda i: (i + nt, 0)),
            pl.BlockSpec((TOK_TILE, D_MODEL), row),
            pl.BlockSpec((TOK_TILE, 2), row),
            pl.BlockSpec((1, D_MODEL), const),
            pl.BlockSpec((1, D_MODEL), const)]


def _combine_split_kernel(y0_ref, y1_ref, res_ref, gate_ref, g_ref, beta_ref, prompt_ref, sample_ref):
    x = _combine(y0_ref, y1_ref, res_ref, gate_ref, g_ref, beta_ref)
    is_sample = pl.program_id(0) == pl.num_programs(0) - 1

    @pl.when(jnp.logical_not(is_sample))
    def _():
        prompt_ref[...] = x

    @pl.when(is_sample)
    def _():
        sample_ref[...] = x


def _combine_split(ys, res, gates_col, g, beta, *, name):
    n = res.shape[0]
    tm = TOK_TILE
    nt = n // tm
    return pl.pallas_call(
        _combine_split_kernel,
        grid=(nt,),
        in_specs=_combine_specs(n),
        out_specs=[pl.BlockSpec((tm, D_MODEL), lambda i: (jnp.minimum(i, nt - 2), 0)),
                   pl.BlockSpec((tm, D_MODEL), lambda i: (0, 0))],
        out_shape=[jax.ShapeDtypeStruct((n - tm, D_MODEL), F32),
                   jax.ShapeDtypeStruct((tm, D_MODEL), F32)],
        compiler_params=_params(("arbitrary",), 40),
        name=name,
    )(ys, ys, res, gates_col, g, beta)


def _load_weight(w_hbm, wbf_ref, stage_ref, sems):
    rows = stage_ref.shape[1]
    n_chunks = wbf_ref.shape[0] // rows

    def chunk_copy(c):
        return pltpu.make_async_copy(w_hbm.at[pl.ds(c * rows, rows), :], stage_ref.at[c % 2],
                                     sems.at[c % 2])

    chunk_copy(0).start()
    for c in range(n_chunks):
        if c + 1 < n_chunks:
            chunk_copy(c + 1).start()
        chunk_copy(c).wait()
        wbf_ref[c * rows:(c + 1) * rows, :] = stage_ref[c % 2].astype(BF16)


def _combine_qkv_kernel(y0_ref, y1_ref, res_ref, gate_ref, g_ref, beta_ref, wq_hbm, wkv_hbm,
                        x_ref, q_ref, kv_ref, wq_bf, wkv_bf, stage_q, stage_kv, sems, *, wq_index):
    @pl.when(pl.program_id(0) == 0)
    def _():
        wq = wq_hbm
        for k in wq_index:
            wq = wq.at[k]
        _load_weight(wq, wq_bf, stage_q, sems)
        _load_weight(wkv_hbm, wkv_bf, stage_kv, sems)

    x = _combine(y0_ref, y1_ref, res_ref, gate_ref, g_ref, beta_ref)
    x_ref[...] = x
    x_bf = x.astype(BF16)
    q = jnp.dot(x_bf, wq_bf[...], preferred_element_type=F32) * (HEAD_DIM ** -0.5)
    q_ref[...] = q.astype(BF16)
    kv_ref[...] = jnp.dot(x_bf, wkv_bf[...], preferred_element_type=F32)


def _combine_qkv(ys, res, gates_col, g, beta, w_q, wq_index, w_kv, *, name):
    n = res.shape[0]
    tm = TOK_TILE
    row = lambda i: (i, 0)
    hbm = pl.BlockSpec(memory_space=pl.ANY)
    return pl.pallas_call(
        functools.partial(_combine_qkv_kernel, wq_index=wq_index),
        grid=(n // tm,),
        in_specs=_combine_specs(n) + [hbm, hbm],
        out_specs=[pl.BlockSpec((tm, D_MODEL), row),
                   pl.BlockSpec((tm, D_MODEL), row),
                   pl.BlockSpec((tm, 2 * KV_DIM), row)],
        out_shape=[jax.ShapeDtypeStruct((n, D_MODEL), F32),
                   jax.ShapeDtypeStruct((n, D_MODEL), BF16),
                   jax.ShapeDtypeStruct((n, 2 * KV_DIM), F32)],
        scratch_shapes=[pltpu.VMEM((D_MODEL, D_MODEL), BF16),
                        pltpu.VMEM((D_MODEL, 2 * KV_DIM), BF16),
                        pltpu.VMEM((2, CAST_ROWS, D_MODEL), F32),
                        pltpu.VMEM((2, CAST_ROWS, 2 * KV_DIM), F32),
                        pltpu.SemaphoreType.DMA((2,))],
        compiler_params=_params(("arbitrary",), 52),
        name=name,
    )(ys, ys, res, gates_col, g, beta, w_q, w_kv)


def _moe_block(x_rows, e_idx, w_gate, w_up, w_down, layer):
    n = x_rows.shape[0] // ROW_SUBLANES
    n_tiles = -(-(2 * n + N_EXPERTS * (MOE_TILE - 1)) // MOE_TILE)
    pos, tile_expert, n_used, next_expert = _plan(e_idx, name=f"moe_plan_{layer}")
    pair = _invert(pos, n_tiles * MOE_TILE, name=f"moe_invert_{layer}")
    return _moe_tiles(x_rows, pair, tile_expert, n_used, next_expert, w_gate, w_up, w_down, layer)


def _sigmoid(x):
    return 0.5 * jnp.tanh(0.5 * x) + 0.5


def _log_sigmoid(x):
    return -(jnp.maximum(-x, 0.0) + jnp.log1p(jnp.exp(-jnp.abs(x))))


def _lru_gate_blocks(xcs, blocks, wrg_bf, wig_bf, brg_ref, big_ref, lam_ref):
    xbs = [xc.astype(BF16) for xc in xcs]
    r_lin = [jnp.dot(xb, wrg_bf[n], preferred_element_type=F32) for xb, n in zip(xbs, blocks)]
    i_lin = [jnp.dot(xb, wig_bf[n], preferred_element_type=F32) for xb, n in zip(xbs, blocks)]
    out = []
    for xc, n, rl, il in zip(xcs, blocks, r_lin, i_lin):
        cols = slice(n * LRU_BLOCK, (n + 1) * LRU_BLOCK)
        r = _sigmoid(rl + brg_ref[:, cols])
        i = _sigmoid(il + big_ref[:, cols])
        log_a = LRU_C * r * _log_sigmoid(lam_ref[:, cols])
        a = jnp.exp(log_a)
        u = xc * i * jnp.sqrt(-jnp.tanh(log_a) * (a * a + 1.0))
        out.append((a, u))
    return out


def _lru_gate_block(xc, n, wrg_bf, wig_bf, brg_ref, big_ref, lam_ref):
    return _lru_gate_blocks([xc], [n], wrg_bf, wig_bf, brg_ref, big_ref, lam_ref)[0]


def _cast_gate_weights(wrg_ref, wig_ref, wrg_bf, wig_bf):
    for n in range(LRU_BLOCKS):
        wrg_bf[n] = wrg_ref[n].astype(BF16)
        wig_bf[n] = wig_ref[n].astype(BF16)


def _lru_prompt_kernel(xb_ref, yb_ref, cw_ref, cb_ref, wrg_ref, wig_ref, brg_ref, big_ref, lam_ref,
                       m_ref, conv_ref, hlast_ref, xs, tail, a_s, u_s, hs_t, h_s, wrg_bf, wig_bf):
    b = pl.program_id(0)
    j = pl.program_id(1)
    tt = m_ref.shape[0]
    seg_len = tt // SEGS
    taps = CONV_WIDTH - 1
    head = SEGS * taps

    @pl.when(jnp.logical_and(b == 0, j == 0))
    def _():
        _cast_gate_weights(wrg_ref, wig_ref, wrg_bf, wig_bf)

    @pl.when(j == 0)
    def _():
        tail[...] = jnp.zeros_like(tail)
        h_s[...] = jnp.zeros_like(h_s)

    for q in range(seg_len):
        xs[head + SEGS * q:head + SEGS * (q + 1), :] = jnp.concatenate(
            [xb_ref[pl.ds(_chunk_row(q, c), SEGS, stride=CHUNK_SEG_ROWS), :] for c in range(CHUNKS)],
            axis=1)
    sub = lax.broadcasted_iota(jnp.int32, (SEGS, D_MODEL), 0)
    for k in range(taps):
        last = head + SEGS * (seg_len - taps + k)
        joined = jnp.where(sub == SEGS - 1, tail[SEGS * k:SEGS * (k + 1), :], xs[last:last + SEGS, :])
        xs[SEGS * k:SEGS * (k + 1), :] = pltpu.roll(joined, 1, axis=0)
    tail[...] = xs[head + SEGS * (seg_len - taps):head + SEGS * seg_len, :]

    for first_block in range(0, LRU_BLOCKS, GATE_BLOCKS):
        blocks = range(first_block, first_block + GATE_BLOCKS)
        xcs = []
        for n in blocks:
            cols = slice(n * LRU_BLOCK, (n + 1) * LRU_BLOCK)
            xc = cb_ref[:, cols] + cw_ref[0:1, cols] * xs[0:tt, cols]
            for k in range(1, CONV_WIDTH):
                xc = xc + cw_ref[k:k + 1, cols] * xs[SEGS * k:SEGS * k + tt, cols]
            xcs.append(xc)
        gates = _lru_gate_blocks(xcs, blocks, wrg_bf, wig_bf, brg_ref, big_ref, lam_ref)
        for n, (a, u) in zip(blocks, gates):
            cols = slice(n * LRU_BLOCK, (n + 1) * LRU_BLOCK)
            a_s[:, cols] = a
            u_s[:, cols] = u

    def scan_body(q, carry):
        h, prod = carry
        rows = pl.ds(pl.multiple_of(q * SEGS, SEGS), SEGS)
        a = a_s[rows, :]
        h = a * h + u_s[rows, :]
        prod = a * prod
        u_s[rows, :] = h
        a_s[rows, :] = prod
        return h, prod

    h_end, prod_end = lax.fori_loop(
        0, seg_len, scan_body,
        (jnp.zeros((SEGS, D_MODEL), F32), jnp.ones((SEGS, D_MODEL), F32)))
    state = h_s[...]
    entering = []
    for s in range(SEGS):
        entering.append(state)
        state = h_end[s:s + 1, :] + prod_end[s:s + 1, :] * state
    h_s[...] = state
    enter = jnp.concatenate(entering, axis=0)

    def fix_body(q, carry):
        rows = pl.ds(pl.multiple_of(q * SEGS, SEGS), SEGS)
        h = u_s[rows, :] + a_s[rows, :] * enter
        for c in range(CHUNKS):
            hs_t[pl.ds(q * CHUNK_TOK_ROWS + c, SEGS, stride=CHUNK_SEG_ROWS), :] = h[:, c * LANES:(c + 1) * LANES]
        return carry

    lax.fori_loop(0, seg_len, fix_body, 0)
    for s in range(SEGS):
        rows = slice(s * seg_len, (s + 1) * seg_len)
        hs = _chunk_rows_load(hs_t, s * seg_len, seg_len)
        m_ref[rows, :] = (hs * yb_ref[rows, :].astype(F32)).astype(BF16)

    @pl.when(j == pl.num_programs(1) - 1)
    def _():
        for k in range(taps):
            conv_ref[k:k + 1, :] = tail[SEGS * k + SEGS - 1:SEGS * (k + 1), :]
        hlast_ref[...] = state


def _lru_prompt(xb, yb, batch, seq, cw, cb, wrg, wig, brg, big, lam):
    tt = TOK_TILE
    nj = seq // tt
    row = lambda b, j: (b * nj + j, 0)
    const2 = lambda b, j: (0, 0)
    const3 = lambda b, j: (0, 0, 0)
    return pl.pallas_call(
        _lru_prompt_kernel,
        grid=(batch, nj),
        in_specs=[
            pl.BlockSpec((CHUNK_TILE_ROWS, LANES), row),
            pl.BlockSpec((tt, D_MODEL), row),
            pl.BlockSpec((CONV_WIDTH, D_MODEL), const2),
            pl.BlockSpec((1, D_MODEL), const2),
            pl.BlockSpec((LRU_BLOCKS, LRU_BLOCK, LRU_BLOCK), const3),
            pl.BlockSpec((LRU_BLOCKS, LRU_BLOCK, LRU_BLOCK), const3),
            pl.BlockSpec((1, D_MODEL), const2),
            pl.BlockSpec((1, D_MODEL), const2),
            pl.BlockSpec((1, D_MODEL), const2),
        ],
        out_specs=[
            pl.BlockSpec((tt, D_MODEL), row),
            pl.BlockSpec((None, CONV_WIDTH - 1, D_MODEL), lambda b, j: (b, 0, 0)),
            pl.BlockSpec((None, 1, D_MODEL), lambda b, j: (b, 0, 0)),
        ],
        out_shape=[
            jax.ShapeDtypeStruct((batch * seq, D_MODEL), BF16),
            jax.ShapeDtypeStruct((batch, CONV_WIDTH - 1, D_MODEL), F32),
            jax.ShapeDtypeStruct((batch, 1, D_MODEL), F32),
        ],
        scratch_shapes=[
            pltpu.VMEM((tt + SEGS * (CONV_WIDTH - 1), D_MODEL), F32),
            pltpu.VMEM((SEGS * (CONV_WIDTH - 1), D_MODEL), F32),
            pltpu.VMEM((tt, D_MODEL), F32),
            pltpu.VMEM((tt, D_MODEL), F32),
            pltpu.VMEM((CHUNK_TILE_ROWS, LANES), F32),
            pltpu.VMEM((1, D_MODEL), F32),
            pltpu.VMEM((LRU_BLOCKS, LRU_BLOCK, LRU_BLOCK), BF16),
            pltpu.VMEM((LRU_BLOCKS, LRU_BLOCK, LRU_BLOCK), BF16),
        ],
        compiler_params=_params(("arbitrary", "arbitrary"), 40),
        name="lru_prompt",
    )(xb, yb, cw, cb, wrg, wig, brg, big, lam)


def _lru_sample_kernel(xb_ref, yb_ref, cs_ref, h0_ref, cw_ref, cb_ref, wrg_ref, wig_ref,
                       brg_ref, big_ref, lam_ref, m_ref, conv_ref, hlast_ref, wrg_bf, wig_bf, *, steps):
    batch = h0_ref.shape[0]
    _cast_gate_weights(wrg_ref, wig_ref, wrg_bf, wig_bf)
    m_ref[steps * batch:, :] = jnp.zeros((m_ref.shape[0] - steps * batch, D_MODEL), BF16)

    def slab(t, cols):
        if t < CONV_WIDTH - 1:
            return cs_ref[t, :, cols]
        t -= CONV_WIDTH - 1
        first, stop, _ = cols.indices(D_MODEL)
        return _chunk_rows_load(xb_ref, t * batch, batch, range(first // LANES, stop // LANES))

    for n in range(LRU_BLOCKS):
        cols = slice(n * LRU_BLOCK, (n + 1) * LRU_BLOCK)
        h = h0_ref[:, cols]
        for t in range(steps):
            xc = cb_ref[:, cols] + cw_ref[0:1, cols] * slab(t, cols)
            for k in range(1, CONV_WIDTH):
                xc = xc + cw_ref[k:k + 1, cols] * slab(t + k, cols)
            a, u = _lru_gate_block(xc, n, wrg_bf, wig_bf, brg_ref, big_ref, lam_ref)
            h = a * h + u
            rows = slice(t * batch, (t + 1) * batch)
            m_ref[rows, cols] = (h * yb_ref[rows, cols].astype(F32)).astype(BF16)
        hlast_ref[:, cols] = h
    for k in range(CONV_WIDTH - 1):
        conv_ref[k] = slab(steps + k, slice(None))


def _lru_sample(xb, yb, tile, steps, conv_state, h0, cw, cb, wrg, wig, brg, big, lam):
    batch = h0.shape[0]
    tok = pl.BlockSpec((TOK_TILE, D_MODEL), lambda i: (tile, 0))
    tok_chunks = pl.BlockSpec((CHUNK_TILE_ROWS, LANES), lambda i: (tile, 0))
    full = lambda a: pl.BlockSpec(a.shape, lambda i: (0,) * a.ndim)
    small = (conv_state, h0, cw, cb, wrg, wig, brg, big, lam)
    return pl.pallas_call(
        functools.partial(_lru_sample_kernel, steps=steps),
        grid=(1,),
        in_specs=[tok_chunks, tok] + [full(a) for a in small],
        out_specs=[
            pl.BlockSpec((TOK_TILE, D_MODEL), lambda i: (0, 0)),
            pl.BlockSpec((CONV_WIDTH - 1, batch, D_MODEL), lambda i: (0, 0, 0)),
            pl.BlockSpec((batch, D_MODEL), lambda i: (0, 0)),
        ],
        out_shape=[
            jax.ShapeDtypeStruct((TOK_TILE, D_MODEL), BF16),
            jax.ShapeDtypeStruct((CONV_WIDTH - 1, batch, D_MODEL), F32),
            jax.ShapeDtypeStruct((batch, D_MODEL), F32),
        ],
        scratch_shapes=[
            pltpu.VMEM((LRU_BLOCKS, LRU_BLOCK, LRU_BLOCK), BF16),
            pltpu.VMEM((LRU_BLOCKS, LRU_BLOCK, LRU_BLOCK), BF16),
        ],
        compiler_params=_params(("arbitrary",), 32),
        name="lru_sample",
    )(xb, yb, *small)


def _rel_bucket(dist):
    n = jnp.maximum(dist, 0)
    max_exact = N_BUCKETS // 2
    nf = jnp.maximum(n, 1).astype(F32)
    large = max_exact + (jnp.log(nf / max_exact) / math.log(MAX_DISTANCE / max_exact)
                         * (N_BUCKETS - max_exact)).astype(jnp.int32)
    large = jnp.minimum(large, N_BUCKETS - 1)
    return jnp.where(n < max_exact, n, large)


def _masked_buckets(dist):
    valid = (dist >= 0) & (dist < WINDOW)
    return jnp.where(valid, _rel_bucket(dist), -1).astype(jnp.int32)


def _build_bias(bucket, tab_ref, head):
    def body(bi, acc):
        return jnp.where(bucket == bi, tab_ref[bi * N_HEADS + head], acc)
    return lax.fori_loop(0, N_BUCKETS, body, jnp.full(bucket.shape, NEG_INF, F32))


def _softmax_pv(s, sink, v):
    m = jnp.maximum(jnp.max(s, axis=-1, keepdims=True), sink)
    p = jnp.exp(s - m)
    den = jnp.sum(p, axis=-1, keepdims=True) + jnp.exp(sink - m)
    return jnp.dot(p.astype(BF16), v, preferred_element_type=F32) / den


def _attn_prompt_kernel(q_ref, kvp_ref, kvc_ref, bucket_ref, tab_ref, sink_ref, o_ref, bias_s):
    b = pl.program_id(0)
    n = pl.program_id(1)

    @pl.when(jnp.logical_and(b == 0, n == 0))
    def _():
        bucket = bucket_ref[...]

        col = lax.broadcasted_iota(jnp.int32, (WINDOW, 2 * WINDOW), 1)

        def head_body(h, c):
            bias = _build_bias(bucket, tab_ref, h)
            sink = sink_ref[h]
            g = h // GROUP
            r0 = pl.multiple_of((h % GROUP) * WINDOW, WINDOW)
            bias_s[0, g, pl.ds(r0, WINDOW), :] = jnp.where(col == 0, sink, bias)
            bias_s[1, g, pl.ds(r0, WINDOW), :] = jnp.where(
                col == 0, sink, jnp.where(col < WINDOW, NEG_INF, bias))
            return c

        lax.fori_loop(0, N_HEADS, head_body, 0)

    first = (n == 0).astype(jnp.int32)
    row = lax.broadcasted_iota(jnp.int32, kvp_ref.shape, 0)
    kv_prev = jnp.where(row == 0, 0.0, kvp_ref[...])
    kv = jnp.concatenate([kv_prev, kvc_ref[...]], axis=0).astype(BF16)
    ones = jnp.ones((2 * WINDOW, 2 * HEAD_DIM), BF16)
    lane = lax.broadcasted_iota(jnp.int32, (WINDOW, 2 * HEAD_DIM), 1)
    def scores(idx):
        g, pair = divmod(idx, GROUP // 2)
        h0 = g * GROUP + 2 * pair
        kg = kv[:, g * HEAD_DIM:(g + 1) * HEAD_DIM]
        qp = jnp.concatenate([q_ref[:, h * HEAD_DIM:(h + 1) * HEAD_DIM] for h in (h0, h0 + 1)], axis=0)
        s = lax.dot_general(qp, kg, (((1,), (1,)), ((), ())), preferred_element_type=F32)
        return s + bias_s[first, g, 2 * pair * WINDOW:(2 * pair + 2) * WINDOW, :]

    def finish(idx, o_ext):
        h0 = 2 * idx
        o = o_ext[:, :2 * HEAD_DIM] * (1.0 / o_ext[:, 2 * HEAD_DIM:])
        o_ref[:, h0 * HEAD_DIM:(h0 + 2) * HEAD_DIM] = jnp.where(
            lane < HEAD_DIM, o[:WINDOW], o[WINDOW:]).astype(BF16)

    n_pairs = N_HEADS // 2

    def values(idx, p):
        g = idx // (GROUP // 2)
        vg = kv[:, KV_DIM + g * HEAD_DIM:KV_DIM + (g + 1) * HEAD_DIM]
        v_ext = jnp.concatenate([vg, vg, ones], axis=1)
        return jnp.dot(p, v_ext, preferred_element_type=F32)

    for first_pair in range(0, n_pairs, PAIR_BLOCK):
        block = range(first_pair, first_pair + PAIR_BLOCK)
        ss = [scores(idx) for idx in block]
        ms = [jnp.max(s, axis=-1, keepdims=True) for s in ss]
        ps = [jnp.exp(s - m).astype(BF16) for s, m in zip(ss, ms)]
        os_ = [values(idx, p) for idx, p in zip(block, ps)]
        for idx, o_ext in zip(block, os_):
            finish(idx, o_ext)


def _attn_prompt(q, kv, batch, seq, bucket, tab, sinks):
    nb = seq // WINDOW
    smem = pl.BlockSpec(memory_space=pltpu.SMEM)
    return pl.pallas_call(
        _attn_prompt_kernel,
        grid=(batch, nb),
        in_specs=[
            pl.BlockSpec((WINDOW, D_MODEL), lambda b, n: (b * nb + n, 0)),
            pl.BlockSpec((WINDOW, 2 * KV_DIM), lambda b, n: (jnp.maximum(b * nb + n - 1, 0), 0)),
            pl.BlockSpec((WINDOW, 2 * KV_DIM), lambda b, n: (b * nb + n, 0)),
            pl.BlockSpec((WINDOW, 2 * WINDOW), lambda b, n: (0, 0)),
            smem, smem,
        ],
        out_specs=pl.BlockSpec((WINDOW, D_MODEL), lambda b, n: (b * nb + n, 0)),
        out_shape=jax.ShapeDtypeStruct((batch * seq, D_MODEL), BF16),
        scratch_shapes=[pltpu.VMEM((2, N_KV_HEADS, GROUP * WINDOW, 2 * WINDOW), F32)],
        compiler_params=_params(("arbitrary", "arbitrary"), 32),
        name="attn_prompt",
    )(q, kv, kv, bucket, tab, sinks)


def _attn_sample_kernel(q_ref, k_ref, v_ref, bucket_ref, tab_ref, sink_ref, o_ref, bias_s):
    steps = q_ref.shape[0]

    @pl.when(pl.program_id(0) == 0)
    def _():
        bucket = bucket_ref[...]

        def head_body(h, c):
            bias_s[h] = _build_bias(bucket, tab_ref, h)
            return c

        lax.fori_loop(0, N_HEADS, head_body, 0)

    rows = lax.broadcasted_iota(jnp.int32, (GROUP * steps, 1), 0)
    k = k_ref[...].astype(BF16)
    v = v_ref[...].astype(BF16)
    groups = range(N_KV_HEADS)

    def heads_of(g):
        return range(g * GROUP, (g + 1) * GROUP)

    scores, sinks = [], []
    for g in groups:
        qg = jnp.concatenate([q_ref[:, h * HEAD_DIM:(h + 1) * HEAD_DIM] for h in heads_of(g)], axis=0)
        bias = jnp.concatenate([bias_s[h] for h in heads_of(g)], axis=0)
        sink = jnp.full((GROUP * steps, 1), sink_ref[g * GROUP], F32)
        for hh in range(1, GROUP):
            sink = jnp.where(rows >= hh * steps, sink_ref[g * GROUP + hh], sink)
        kg = k[:, g * HEAD_DIM:(g + 1) * HEAD_DIM]
        scores.append(lax.dot_general(qg, kg, (((1,), (1,)), ((), ())),
                                      preferred_element_type=F32) + bias)
        sinks.append(sink)
    outs = [_softmax_pv(scores[g], sinks[g], v[:, g * HEAD_DIM:(g + 1) * HEAD_DIM]) for g in groups]
    for g in groups:
        for hh, h in enumerate(heads_of(g)):
            o_ref[:, h * HEAD_DIM:(h + 1) * HEAD_DIM] = outs[g][hh * steps:(hh + 1) * steps].astype(BF16)


def _attn_sample(q, k_all, v_all, bucket, tab, sinks):
    batch, steps, _ = q.shape
    lk = k_all.shape[1]
    smem = pl.BlockSpec(memory_space=pltpu.SMEM)
    return pl.pallas_call(
        _attn_sample_kernel,
        grid=(batch,),
        in_specs=[
            pl.BlockSpec((None, steps, D_MODEL), lambda b: (b, 0, 0)),
            pl.BlockSpec((None, lk, KV_DIM), lambda b: (b, 0, 0)),
            pl.BlockSpec((None, lk, KV_DIM), lambda b: (b, 0, 0)),
            pl.BlockSpec((steps, lk), lambda b: (0, 0)),
            smem, smem,
        ],
        out_specs=pl.BlockSpec((None, steps, D_MODEL), lambda b: (b, 0, 0)),
        out_shape=jax.ShapeDtypeStruct((batch, steps, D_MODEL), BF16),
        scratch_shapes=[pltpu.VMEM((N_HEADS, steps, lk), F32)],
        compiler_params=_params(("arbitrary",), 32),
        name="attn_sample",
    )(q, k_all, v_all, bucket, tab, sinks)


def kernel(x_prompt, x_sample, state_conv, state_rnn, cache_k_win, cache_v_win, ln_g, ln_b, lru_w_x, lru_b_x, lru_w_y, lru_b_y, lru_conv_w, lru_conv_b, lru_w_rg, lru_b_rg, lru_w_ig, lru_b_ig, lru_lam, lru_w_out, lru_b_out, attn_w_kv, attn_w_q, attn_w_o, attn_sinks, rel_bias, moe_w_router, moe_b_router, moe_w_gate, moe_w_up, moe_w_down):
    bp, seq, _ = x_prompt.shape
    bs, steps, _ = x_sample.shape
    n_p = bp * seq
    n_s = bs * steps

    assert n_p % TOK_TILE == 0 and n_s <= TOK_TILE
    sample_tile = n_p // TOK_TILE

    def pad_tile(rows):
        return jnp.pad(rows, ((0, TOK_TILE - n_s), (0, 0)))

    x0 = (x_prompt.reshape(n_p, D_MODEL),
          pad_tile(x_sample.transpose(1, 0, 2).reshape(n_s, D_MODEL)))
    wr_t = moe_w_router.T
    br = moe_b_router.reshape(N_EXPERTS, 1)
    vec = lambda a: a.reshape(1, -1)

    xb, yb = _lru_in(x0, lru_w_x, vec(lru_b_x[0]), lru_w_y, vec(lru_b_y[0]), 0)
    lru_args = (lru_conv_w[0], vec(lru_conv_b[0]), lru_w_rg[0], lru_w_ig[0],
                vec(lru_b_rg[0]), vec(lru_b_ig[0]), vec(lru_lam[0]))
    m_p, conv_p, rnn_p = _lru_prompt(xb, yb, bp, seq, *lru_args)
    m_s, conv_s, rnn_s = _lru_sample(xb, yb, sample_tile, steps,
                                     state_conv[0].transpose(1, 0, 2), state_rnn[0], *lru_args)
    x1, x1_rows, e_idx, gates = _proj_ln((m_p, m_s), lru_w_out, (0,), vec(lru_b_out[0]), x0,
                                vec(ln_g[0, 0]), vec(ln_b[0, 0]), wr_t, br, name="lru_out_ln")
    ys = _moe_block(x1_rows, e_idx, moe_w_gate, moe_w_up, moe_w_down, 0)

    x2, q, kv = _combine_qkv(ys, x1, gates.T, vec(ln_g[0, 1]), vec(ln_b[0, 1]),
                             attn_w_q, (0,), attn_w_kv, name="moe_combine_qkv")
    tab = rel_bias.reshape(-1)
    sinks = attn_sinks[0]
    qi = jnp.arange(WINDOW)[:, None]
    kj = jnp.arange(2 * WINDOW)[None, :]
    o_p = _attn_prompt(q, kv, bp, seq, _masked_buckets(qi + WINDOW - kj), tab, sinks)
    kv_s = kv[n_p:n_p + n_s].reshape(steps, bs, 2, KV_DIM).transpose(2, 1, 0, 3)
    k_all = jnp.concatenate([cache_k_win.reshape(bs, WINDOW, KV_DIM), kv_s[0]], axis=1)
    v_all = jnp.concatenate([cache_v_win.reshape(bs, WINDOW, KV_DIM), kv_s[1]], axis=1)
    dist_s = jnp.arange(steps)[:, None] + WINDOW - jnp.arange(WINDOW + steps)[None, :]
    q_s = q[n_p:n_p + n_s].reshape(steps, bs, D_MODEL).transpose(1, 0, 2)
    o_s = _attn_sample(q_s, k_all, v_all, _masked_buckets(dist_s), tab, sinks)
    o_s = pad_tile(o_s.transpose(1, 0, 2).reshape(n_s, D_MODEL))
    x3, x3_rows, e_idx, gates = _proj_ln((o_p, o_s), attn_w_o, (0,), jnp.zeros((1, D_MODEL), F32), x2,
                                vec(ln_g[1, 0]), vec(ln_b[1, 0]), wr_t, br, name="attn_out_ln")
    ys = _moe_block(x3_rows, e_idx, moe_w_gate, moe_w_up, moe_w_down, 1)
    y_p, y_s = _combine_split(ys, x3, gates.T, vec(ln_g[1, 1]), vec(ln_b[1, 1]), name="moe_combine_1")

    y_prompt = y_p.reshape(bp, seq, D_MODEL)
    y_sample = y_s[:n_s].reshape(steps, bs, D_MODEL).transpose(1, 0, 2)
    kv_p = jnp.stack([kv[(b + 1) * seq - WINDOW:(b + 1) * seq] for b in range(bp)])
    kv_p = kv_p.reshape(bp, WINDOW, 2, N_KV_HEADS, HEAD_DIM)
    k_win_s = k_all[:, steps:].reshape(bs, WINDOW, N_KV_HEADS, HEAD_DIM)
    v_win_s = v_all[:, steps:].reshape(bs, WINDOW, N_KV_HEADS, HEAD_DIM)
    return (y_prompt, y_sample,
            conv_p[None], rnn_p.reshape(1, bp, D_MODEL),
            kv_p[:, :, 0], kv_p[:, :, 1],
            conv_s.transpose(1, 0, 2)[None], rnn_s[None],
            k_win_s, v_win_s)
```

```python
import functools
import math

import jax
import jax.numpy as jnp
from jax import lax
from jax.experimental import pallas as pl
from jax.experimental.pallas import tpu as pltpu

D_MODEL = 2048
DEPTH = 2
LRU_BLOCKS = 8
LRU_BLOCK = D_MODEL // LRU_BLOCKS
CONV_WIDTH = 4
LRU_C = 8.0
N_HEADS = 32
HEAD_DIM = 64
N_KV_HEADS = 8
GROUP = N_HEADS // N_KV_HEADS
KV_DIM = N_KV_HEADS * HEAD_DIM
WINDOW = 128
N_BUCKETS = 32
MAX_DISTANCE = 128
N_EXPERTS = 16
N_GROUPS = 4
EXPERTS_PER_GROUP = N_EXPERTS // N_GROUPS
D_EXPERT = 1024
ALPHA = (2 * DEPTH) ** 0.25
LN_EPS = 1e-5

LANES = 128
SEGS = 8
CHUNKS = D_MODEL // LANES
ROW_SUBLANES = D_MODEL // (2 * LANES)
MOE_TILE = 256
TOK_TILE = 256
CHUNK_TOK_ROWS = CHUNKS + 4
CHUNK_SEG_TOKS = TOK_TILE // SEGS
CHUNK_SEG_ROWS = CHUNK_SEG_TOKS * CHUNK_TOK_ROWS + 4
CHUNK_TILE_ROWS = SEGS * CHUNK_SEG_ROWS
DMA_UNROLL = 8
SCALAR_UNROLL = 32
PLAN_UNROLL = 4
PAIR_BLOCK = 4
GATE_BLOCKS = 4
W_CHUNKS = 4
CAST_ROWS = 256
BF16 = jnp.bfloat16
F32 = jnp.float32
NEG_INF = float("-inf")


def _params(sem, vmem_mb):
    return pltpu.CompilerParams(dimension_semantics=sem, vmem_limit_bytes=vmem_mb * 1024 * 1024)


def _cast_rows(src_ref, dst_ref):
    n = src_ref.shape[0] // CAST_ROWS

    def body(i, c):
        r = pl.multiple_of(i * CAST_ROWS, CAST_ROWS)
        dst_ref[pl.ds(r, CAST_ROWS), :] = src_ref[pl.ds(r, CAST_ROWS), :].astype(BF16)
        return c

    lax.fori_loop(0, n, body, 0)


def _layer_norm(z, g, b):
    mu = jnp.mean(z, axis=-1, keepdims=True)
    zc = z - mu
    var = jnp.mean(zc * zc, axis=-1, keepdims=True)
    return zc * lax.rsqrt(var + LN_EPS) * g + b


def _chunk_row(tok, chunk):
    seg, t = divmod(tok, CHUNK_SEG_TOKS)
    return seg * CHUNK_SEG_ROWS + t * CHUNK_TOK_ROWS + chunk


def _chunk_rows_store(ref, y):
    ref[...] = jnp.zeros_like(ref)
    for seg in range(SEGS):
        rows = slice(seg * CHUNK_SEG_TOKS, (seg + 1) * CHUNK_SEG_TOKS)
        for c in range(CHUNKS):
            dst = pl.ds(_chunk_row(seg * CHUNK_SEG_TOKS, c), CHUNK_SEG_TOKS, stride=CHUNK_TOK_ROWS)
            ref[dst, :] = y[rows, c * LANES:(c + 1) * LANES]


def _chunk_rows_load(ref, tok0, count, chunks=range(CHUNKS)):
    assert tok0 // CHUNK_SEG_TOKS == (tok0 + count - 1) // CHUNK_SEG_TOKS
    return jnp.concatenate(
        [ref[pl.ds(_chunk_row(tok0, c), count, stride=CHUNK_TOK_ROWS), :] for c in chunks], axis=1)


def _store_packed_rows(x_bf, rows_ref):
    n = x_bf.shape[0]
    bits = pltpu.bitcast(x_bf.astype(F32), jnp.uint32)
    packed = bits[:, D_MODEL // 2:] | (bits[:, :D_MODEL // 2] >> 16)
    for c in range(ROW_SUBLANES):
        rows_ref[pl.ds(c, n, stride=ROW_SUBLANES), :] = packed[:, c * LANES:(c + 1) * LANES]


def _load_packed_rows(rows_ref, x_bf_ref):
    n = x_bf_ref.shape[0]
    for c in range(ROW_SUBLANES):
        words = rows_ref[pl.ds(c, n, stride=ROW_SUBLANES), :]
        low = pltpu.bitcast(words << 16, F32).astype(BF16)
        high = pltpu.bitcast(words & jnp.uint32(0xFFFF0000), F32).astype(BF16)
        x_bf_ref[:, c * LANES:(c + 1) * LANES] = low
        x_bf_ref[:, D_MODEL // 2 + c * LANES:D_MODEL // 2 + (c + 1) * LANES] = high


def _tok_operands(x, tile_of=lambda i: i):
    if isinstance(x, tuple):
        xp, xs = x
        d = xp.shape[1]
        last_p = xp.shape[0] // TOK_TILE - 1
        specs = [pl.BlockSpec((TOK_TILE, d), lambda i, *_: (jnp.minimum(tile_of(i), last_p), 0)),
                 pl.BlockSpec((TOK_TILE, d), lambda i, *_: (0, 0))]
        return [xp, xs], specs, last_p + 2
    return ([x], [pl.BlockSpec((TOK_TILE, x.shape[1]), lambda i, *_: (tile_of(i), 0))],
            x.shape[0] // TOK_TILE)


def _tok_load(refs, is_sample=None):
    if len(refs) == 1:
        return refs[0][...]
    if is_sample is None:
        is_sample = pl.program_id(0) == pl.num_programs(0) - 1
    return jnp.where(is_sample, refs[1][...], refs[0][...])


def _lru_in_kernel(*refs, n_x, layer):
    x_refs = refs[:n_x]
    wx_hbm, wy_hbm, bx_ref, by_ref, xb_ref, yb_ref, wx_bf, wy_bf, stage, sems = refs[n_x:]

    @pl.when(pl.program_id(0) == 0)
    def _():
        _load_weight(wx_hbm.at[layer], wx_bf, stage, sems)
        _load_weight(wy_hbm.at[layer], wy_bf, stage, sems)

    x = _tok_load(x_refs).astype(BF16)
    _chunk_rows_store(xb_ref, jnp.dot(x, wx_bf[...], preferred_element_type=F32) + bx_ref[...])
    y = jnp.dot(x, wy_bf[...], preferred_element_type=F32) + by_ref[...]
    yb_ref[...] = jax.nn.gelu(y).astype(BF16)


def _lru_in(x, w_x, b_x, w_y, b_y, layer):
    arrays, specs, nt = _tok_operands(x)
    hbm = pl.BlockSpec(memory_space=pl.ANY)
    vec_spec = pl.BlockSpec((1, D_MODEL), lambda i: (0, 0))
    return pl.pallas_call(
        functools.partial(_lru_in_kernel, n_x=len(arrays), layer=layer),
        grid=(nt,),
        in_specs=specs + [hbm, hbm, vec_spec, vec_spec],
        out_specs=[pl.BlockSpec((CHUNK_TILE_ROWS, LANES), lambda i: (i, 0)),
                   pl.BlockSpec((TOK_TILE, D_MODEL), lambda i: (i, 0))],
        out_shape=[jax.ShapeDtypeStruct((nt * CHUNK_TILE_ROWS, LANES), F32),
                   jax.ShapeDtypeStruct((nt * TOK_TILE, D_MODEL), BF16)],
        scratch_shapes=[pltpu.VMEM((D_MODEL, D_MODEL), BF16), pltpu.VMEM((D_MODEL, D_MODEL), BF16),
                        pltpu.VMEM((2, CAST_ROWS, D_MODEL), F32), pltpu.SemaphoreType.DMA((2,))],
        compiler_params=_params(("arbitrary",), 48),
        name="lru_in",
    )(*arrays, w_x, w_y, b_x, b_y)


def _route(logits_t, b_router):
    aff = jax.nn.sigmoid(logits_t)
    sel = aff + b_router
    srow = [sel[e:e + 1, :] for e in range(N_EXPERTS)]
    arow = [aff[e:e + 1, :] for e in range(N_EXPERTS)]

    def top2_sum(v):
        pairs = [v[i] + v[j] for i in range(4) for j in range(i + 1, 4)]
        return functools.reduce(jnp.maximum, pairs)

    scores = [top2_sum(srow[4 * g:4 * g + 4]) for g in range(N_GROUPS)]
    best = scores[0]
    gi = jnp.zeros_like(best, dtype=jnp.int32)
    for g in range(1, N_GROUPS):
        upd = scores[g] > best
        best = jnp.where(upd, scores[g], best)
        gi = jnp.where(upd, g, gi)

    def pick_group(rows, j):
        out = rows[j]
        for g in range(1, N_GROUPS):
            out = jnp.where(gi == g, rows[4 * g + j], out)
        return out

    v = [pick_group(srow, j) for j in range(EXPERTS_PER_GROUP)]
    a = [pick_group(arow, j) for j in range(EXPERTS_PER_GROUP)]

    m1, i1 = v[0], jnp.zeros_like(gi)
    for j in range(1, EXPERTS_PER_GROUP):
        upd = v[j] > m1
        m1 = jnp.where(upd, v[j], m1)
        i1 = jnp.where(upd, j, i1)
    m2 = jnp.full_like(m1, NEG_INF)
    i2 = jnp.zeros_like(gi)
    for j in range(EXPERTS_PER_GROUP):
        cand = jnp.where(i1 == j, NEG_INF, v[j])
        upd = cand > m2
        m2 = jnp.where(upd, cand, m2)
        i2 = jnp.where(upd, j, i2)

    def pick_idx(rows, idx):
        out = rows[0]
        for j in range(1, EXPERTS_PER_GROUP):
            out = jnp.where(idx == j, rows[j], out)
        return out

    a1 = pick_idx(a, i1)
    a2 = pick_idx(a, i2)
    tot = a1 + a2
    e_idx = jnp.concatenate([gi * EXPERTS_PER_GROUP + i1, gi * EXPERTS_PER_GROUP + i2], axis=0)
    gates = jnp.concatenate([a1 / tot, a2 / tot], axis=0)
    return e_idx, gates


def _proj_ln_kernel(*refs, n_m, n_res):
    m_refs = refs[:n_m]
    w_ref, b_ref = refs[n_m:n_m + 2]
    res_refs = refs[n_m + 2:n_m + 2 + n_res]
    (g_ref, beta_ref, wr_ref, br_ref, x_ref, xrow_ref, e_ref, gate_ref,
     wbf_ref, ya, yb) = refs[n_m + 2 + n_res:]
    i = pl.program_id(0)
    n_tiles = pl.num_programs(0) - 1

    @pl.when(i == 0)
    def _():
        _cast_rows(w_ref, wbf_ref)
        yb[...] = jnp.zeros_like(yb)

    for parity, (cur, prev) in enumerate(((ya, yb), (yb, ya))):
        @pl.when(i % 2 == parity)
        def _():
            cur[...] = jnp.dot(_tok_load(m_refs, i >= n_tiles - 1), wbf_ref[...],
                               preferred_element_type=F32)
            y = prev[...] + b_ref[...]
            x = _layer_norm(ALPHA * _tok_load(res_refs, i == n_tiles) + y, g_ref[...], beta_ref[...])
            x_ref[...] = x
            x_bf = x.astype(BF16)
            _store_packed_rows(x_bf, xrow_ref)
            logits_t = lax.dot_general(wr_ref[...].astype(BF16), x_bf,
                                       (((1,), (1,)), ((), ())), preferred_element_type=F32)
            e_idx, gates = _route(logits_t, br_ref[...])
            e_ref[...] = e_idx
            gate_ref[...] = gates


def _proj_ln(m, w, w_index, b, res, g, beta, wr_t, br, *, name):
    nt = _tok_operands(m)[2]
    m_arrays, m_specs, _ = _tok_operands(m, lambda i: jnp.minimum(i, nt - 1))
    res_arrays, res_specs, _ = _tok_operands(res, lambda i: jnp.maximum(i - 1, 0))
    k = w.shape[-2]
    tm = TOK_TILE
    n = nt * tm
    row = lambda i: (jnp.maximum(i - 1, 0), 0)
    const = lambda i: (0, 0)
    x, x_rows, e_idx, gates = pl.pallas_call(
        functools.partial(_proj_ln_kernel, n_m=len(m_arrays), n_res=len(res_arrays)),
        grid=(nt + 1,),
        in_specs=m_specs + [
            pl.BlockSpec((None,) * len(w_index) + (k, D_MODEL), lambda i: w_index + (0, 0),
                         pipeline_mode=pl.Buffered(1)),
            pl.BlockSpec((1, D_MODEL), const),
        ] + res_specs + [
            pl.BlockSpec((1, D_MODEL), const),
            pl.BlockSpec((1, D_MODEL), const),
            pl.BlockSpec((N_EXPERTS, D_MODEL), const),
            pl.BlockSpec((N_EXPERTS, 1), const),
        ],
        out_specs=[
            pl.BlockSpec((tm, D_MODEL), row),
            pl.BlockSpec((tm * ROW_SUBLANES, LANES), row),
            pl.BlockSpec((None, 2, tm), lambda i: (jnp.maximum(i - 1, 0), 0, 0)),
            pl.BlockSpec((None, 2, tm), lambda i: (jnp.maximum(i - 1, 0), 0, 0)),
        ],
        out_shape=[
            jax.ShapeDtypeStruct((n, D_MODEL), F32),
            jax.ShapeDtypeStruct((n * ROW_SUBLANES, LANES), jnp.uint32),
            jax.ShapeDtypeStruct((nt, 2, tm), jnp.int32),
            jax.ShapeDtypeStruct((nt, 2, tm), F32),
        ],
        scratch_shapes=[pltpu.VMEM((k, D_MODEL), BF16),
                        pltpu.VMEM((tm, D_MODEL), F32), pltpu.VMEM((tm, D_MODEL), F32)],
        compiler_params=_params(("arbitrary",), 52),
        name=name,
    )(*m_arrays, w, b, *res_arrays, g, beta, wr_t, br)
    e_idx = e_idx.transpose(1, 0, 2).reshape(2, n)
    gates = gates.transpose(1, 0, 2).reshape(2, n)
    return x, x_rows, e_idx, gates


def _plan_kernel(e_ref, pos_ref, meta_ref, rank_ref):
    nrow = e_ref.shape[0]
    ri = lax.broadcasted_iota(jnp.int32, (LANES, LANES), 0)
    ci = lax.broadcasted_iota(jnp.int32, (LANES, LANES), 1)
    tri = jnp.where(ri <= ci, 1.0, 0.0).astype(BF16)
    sub = lax.broadcasted_iota(jnp.int32, (N_EXPERTS, LANES), 0)

    def count_body(b, base):
        rows = [b * PLAN_UNROLL + u for u in range(PLAN_UNROLL)]
        onehots = [sub == e_ref[pl.ds(r, 1), :] for r in rows]
        locs = [jnp.dot(jnp.where(oh, 1.0, 0.0).astype(BF16), tri, preferred_element_type=F32)
                for oh in onehots]
        for r, onehot, loc in zip(rows, onehots, locs):
            rank_ref[pl.ds(r, 1), :] = jnp.sum(jnp.where(onehot, base + loc - 1.0, 0.0),
                                               axis=0, keepdims=True)
            base = base + jnp.broadcast_to(loc[:, LANES - 1:LANES], (N_EXPERTS, LANES))
        return base

    count = lax.fori_loop(0, nrow // PLAN_UNROLL, count_body, jnp.zeros((N_EXPERTS, LANES), F32))
    ntile = jnp.floor((count + (MOE_TILE - 1.0)) * (1.0 / MOE_TILE))
    offs = []
    acc = jnp.zeros((1, LANES), F32)
    for e in range(N_EXPERTS):
        offs.append(acc)
        acc = acc + ntile[e:e + 1, :]
    tile_off = jnp.concatenate(offs, axis=0)
    tile_end = tile_off + ntile
    lane = lax.broadcasted_iota(jnp.int32, (N_EXPERTS, LANES), 1).astype(F32)
    tile_expert = jnp.sum(jnp.where(tile_end <= lane, 1.0, 0.0), axis=0, keepdims=True)
    tile_expert = jnp.minimum(tile_expert, N_EXPERTS - 1.0)
    own = jnp.logical_and(tile_off <= lane, lane < tile_end)
    run_end = jnp.sum(jnp.where(own, tile_end, 0.0), axis=0, keepdims=True)
    next_expert = jnp.sum(jnp.where(tile_end <= run_end, 1.0, 0.0), axis=0, keepdims=True)
    has_next = jnp.logical_and(lane[0:1, :] < acc, run_end < acc)
    next_expert = jnp.where(has_next, next_expert, -1.0)
    meta = jnp.concatenate([tile_expert, acc, next_expert, jnp.zeros((5, LANES), F32)], axis=0)
    meta_ref[...] = meta.astype(jnp.int32)
    row_off = tile_off * float(MOE_TILE)

    def pos_body(r, c):
        onehot = sub == e_ref[pl.ds(r, 1), :]
        p = jnp.sum(jnp.where(onehot, row_off, 0.0), axis=0, keepdims=True) + rank_ref[pl.ds(r, 1), :]
        pos_ref[pl.ds(r, 1), :] = p.astype(jnp.int32)
        return c

    lax.fori_loop(0, nrow, pos_body, 0)


def _plan(e_idx, *, name):
    n2 = e_idx.shape[0] * e_idx.shape[1]
    assert n2 % (LANES * PLAN_UNROLL) == 0
    e2d = e_idx.reshape(n2 // LANES, LANES)
    pos, meta = pl.pallas_call(
        _plan_kernel,
        out_shape=[jax.ShapeDtypeStruct(e2d.shape, jnp.int32),
                   jax.ShapeDtypeStruct((8, LANES), jnp.int32)],
        scratch_shapes=[pltpu.VMEM(e2d.shape, F32)],
        name=name,
    )(e2d)
    return pos.reshape(n2), meta[0], meta[1, :1], meta[2]


def _invert_kernel(pos_ref, pair_ref):
    n_rows = pair_ref.shape[0]
    n_pairs = pos_ref.shape[0]

    def fill_body(b, c):
        for u in range(SCALAR_UNROLL):
            pair_ref[b * SCALAR_UNROLL + u] = -1
        return c

    def pair_body(b, c):
        rows = [pos_ref[b * SCALAR_UNROLL + u] for u in range(SCALAR_UNROLL)]
        for u in range(SCALAR_UNROLL):
            pair_ref[rows[u]] = b * SCALAR_UNROLL + u
        return c

    lax.fori_loop(0, n_rows // SCALAR_UNROLL, fill_body, 0)
    lax.fori_loop(0, n_pairs // SCALAR_UNROLL, pair_body, 0)


def _invert(pos, n_rows, *, name):
    return pl.pallas_call(
        _invert_kernel,
        grid_spec=pltpu.PrefetchScalarGridSpec(
            num_scalar_prefetch=1,
            grid=(1,),
            in_specs=[],
            out_specs=pl.BlockSpec(memory_space=pltpu.SMEM),
        ),
        out_shape=jax.ShapeDtypeStruct((n_rows,), jnp.int32),
        name=name,
    )(pos)


def _expert_changed(te_ref, i):
    return jnp.logical_or(i == 0, te_ref[i] != te_ref[jnp.maximum(i - 1, 0)])


class _ExpertWeights:
    def __init__(self, mats, layer, st, sems):
        self.mats, self.layer, self.st, self.sems = mats, layer, st, sems

    def _copies(self, expert, c):
        out = []
        for w_hbm, stage, _, _ in self.mats:
            rows = stage.shape[1]
            src = w_hbm.at[self.layer, expert, pl.ds(pl.multiple_of(c * rows, rows), rows), :]
            out.append(pltpu.make_async_copy(src, stage.at[c % 2], self.sems.at[c % 2]))
        return out

    def _start(self, expert, c):
        for cp in self._copies(expert, c):
            cp.start()
        self.st[2] = c + 1

    def _convert(self, c_src, c_dst):
        for _, stage, w_next, _ in self.mats:
            rows = stage.shape[1]
            dst = pl.ds(pl.multiple_of(c_dst * rows, rows), rows)
            w_next[dst, :] = stage[c_src % 2].astype(BF16)

    def reset(self):
        for _, stage, _, _ in self.mats:
            stage[...] = jnp.zeros_like(stage)
        self.st[1] = 0
        self.st[2] = 0

    def switch_to(self, expert):
        st = self.st

        def body(c, carry):
            @pl.when(c >= st[2])
            def _():
                self._start(expert, c)

            @pl.when(jnp.logical_and(c + 1 < W_CHUNKS, c + 1 >= st[2]))
            def _():
                self._start(expert, c + 1)

            for cp in self._copies(expert, c):
                cp.wait()
            self._convert(c, c)
            return carry

        lax.fori_loop(st[1], W_CHUNKS, body, 0)
        for _, _, w_next, w_cur in self.mats:
            _copy_rows(w_next, w_cur)
        st[1] = 0
        st[2] = 0

    def begin_step(self, next_expert):
        st = self.st
        done, issued = st[1], st[2]
        has_next = next_expert >= 0
        active = jnp.logical_and(has_next, done < issued)

        @pl.when(jnp.logical_and(has_next, jnp.logical_and(issued < W_CHUNKS, issued < done + 2)))
        def _():
            self._start(next_expert, issued)

        @pl.when(active)
        def _():
            for cp in self._copies(next_expert, done):
                cp.wait()

        return active, done

    def convert_step(self, active, done):
        self._convert(jnp.where(active, done, done + 1), jnp.where(active, done, W_CHUNKS))

    def end_step(self, active, done):
        @pl.when(active)
        def _():
            self.st[1] = done + 1


def _copy_rows(src_ref, dst_ref):
    n = dst_ref.shape[0] // CAST_ROWS

    def body(i, c):
        r = pl.multiple_of(i * CAST_ROWS, CAST_ROWS)
        dst_ref[pl.ds(r, CAST_ROWS), :] = src_ref[pl.ds(r, CAST_ROWS), :]
        return c

    lax.fori_loop(0, n, body, 0)


def _moe_tile_kernel(pair_ref, te_ref, nu_ref, nxt_ref, x_hbm, wg_hbm, wu_hbm, wd_hbm, out_hbm,
                     wg_cur, wg_next, wg_stage, wu_cur, wu_next, wu_stage, wd_cur, wd_next, wd_stage,
                     x_rows, x_bf, ya, yb, gsem, ssems, tsem, wsems, st, *, n_tok, n_tiles, layer):
    i = pl.program_id(0)
    nu = nu_ref[0]
    running = i < nu
    trash = 2 * n_tok
    weights = _ExpertWeights([(wg_hbm, wg_stage, wg_next, wg_cur), (wu_hbm, wu_stage, wu_next, wu_cur),
                              (wd_hbm, wd_stage, wd_next, wd_cur)], layer, st, wsems)

    def gather_copy(tile, r):
        p = pair_ref[tile * MOE_TILE + r]
        tok = jnp.where(p >= n_tok, p - n_tok, jnp.maximum(p, 0))
        src = pl.ds(pl.multiple_of(tok * ROW_SUBLANES, ROW_SUBLANES), ROW_SUBLANES)
        return pltpu.make_async_copy(x_hbm.at[src, :], x_rows.at[pl.ds(r * ROW_SUBLANES, ROW_SUBLANES), :],
                                     gsem)

    def scatter_copy(tile, r, buf, sem):
        p = pair_ref[tile * MOE_TILE + r]
        dst = jnp.where(p < 0, trash + r, p)
        return pltpu.make_async_copy(buf.at[pl.ds(r, 1), :], out_hbm.at[pl.ds(dst, 1), :], sem)

    def wait_scatter(buf, sem):
        pltpu.make_async_copy(buf, out_hbm.at[pl.ds(0, MOE_TILE), :], sem).wait()

    @pl.when(i == 0)
    def _():
        def body(rb, c):
            for u in range(DMA_UNROLL):
                gather_copy(0, rb * DMA_UNROLL + u).start()
            return c
        lax.fori_loop(0, MOE_TILE // DMA_UNROLL, body, 0)
        yb[...] = jnp.zeros_like(yb)
        fill = pltpu.make_async_copy(yb, out_hbm.at[pl.ds(trash, MOE_TILE), :], tsem)
        fill.start()
        fill.wait()
        weights.reset()

    @pl.when(i <= nu)
    def _():
        pltpu.make_async_copy(x_hbm.at[pl.ds(0, MOE_TILE * ROW_SUBLANES), :], x_rows, gsem).wait()

    @pl.when(jnp.logical_and(running, _expert_changed(te_ref, i)))
    def _():
        weights.switch_to(te_ref[i])

    active, done = weights.begin_step(jnp.where(running, nxt_ref[i], -1))

    for parity, (cur, prev) in enumerate(((ya, yb), (yb, ya))):
        cur_sem, prev_sem = ssems.at[parity], ssems.at[1 - parity]
        mine = i % 2 == parity

        @pl.when(jnp.logical_and(mine, jnp.logical_and(i >= 1, i - 1 <= nu)))
        def _():
            wait_scatter(cur, cur_sem)

        @pl.when(jnp.logical_and(mine, running))
        def _():
            _load_packed_rows(x_rows, x_bf)
            next_tile = jnp.minimum(i + 1, n_tiles - 1)
            prev_tile = jnp.maximum(i - 1, 0)
            for r in range(MOE_TILE):
                gather_copy(next_tile, r).start(priority=r % 2)
            for r in range(MOE_TILE):
                scatter_copy(prev_tile, r, prev, prev_sem).start(priority=r % 2)
            weights.convert_step(active, done)
            x = x_bf[...]
            a = jnp.dot(x, wg_cur[...], preferred_element_type=F32)
            b = jnp.dot(x, wu_cur[...], preferred_element_type=F32)
            h = (jax.nn.silu(a) * b).astype(BF16)
            cur[...] = jnp.dot(h, wd_cur[...], preferred_element_type=F32)

        @pl.when(jnp.logical_and(mine, i == nu))
        def _():
            def body(rb, c):
                for u in range(DMA_UNROLL):
                    scatter_copy(i - 1, rb * DMA_UNROLL + u, prev, prev_sem).start()
                return c
            lax.fori_loop(0, MOE_TILE // DMA_UNROLL, body, 0)

            @pl.when(i == n_tiles)
            def _():
                wait_scatter(prev, prev_sem)

    weights.end_step(active, done)


def _moe_tiles(x, pair, tile_expert, n_used, next_expert, w_gate, w_up, w_down, layer):
    n_tok = x.shape[0] // ROW_SUBLANES
    n_tiles = pair.shape[0] // MOE_TILE
    hbm = pl.BlockSpec(memory_space=pl.ANY)

    def weight_bufs(k, n):
        chunk = k // W_CHUNKS
        return [pltpu.VMEM((k, n), BF16), pltpu.VMEM((k + chunk, n), BF16), pltpu.VMEM((2, chunk, n), F32)]

    return pl.pallas_call(
        functools.partial(_moe_tile_kernel, n_tok=n_tok, n_tiles=n_tiles, layer=layer),
        grid_spec=pltpu.PrefetchScalarGridSpec(
            num_scalar_prefetch=4,
            grid=(n_tiles + 1,),
            in_specs=[hbm, hbm, hbm, hbm],
            out_specs=hbm,
            scratch_shapes=weight_bufs(D_MODEL, D_EXPERT) + weight_bufs(D_MODEL, D_EXPERT)
            + weight_bufs(D_EXPERT, D_MODEL) + [
                pltpu.VMEM((MOE_TILE * ROW_SUBLANES, LANES), jnp.uint32),
                pltpu.VMEM((MOE_TILE, D_MODEL), BF16),
                pltpu.VMEM((MOE_TILE, D_MODEL), F32), pltpu.VMEM((MOE_TILE, D_MODEL), F32),
                pltpu.SemaphoreType.DMA(()), pltpu.SemaphoreType.DMA((2,)),
                pltpu.SemaphoreType.DMA(()), pltpu.SemaphoreType.DMA((2,)),
                pltpu.SMEM((4,), jnp.int32)],
        ),
        out_shape=jax.ShapeDtypeStruct((2 * n_tok + MOE_TILE, D_MODEL), F32),
        compiler_params=_params(("arbitrary",), 56),
        name=f"moe_tiles_{layer}",
    )(pair, tile_expert, n_used, next_expert, x, w_gate, w_up, w_down)


def _combine(y0_ref, y1_ref, res_ref, gate_ref, g_ref, beta_ref):
    gate = gate_ref[...]
    ffn = gate[:, 0:1] * y0_ref[...] + gate[:, 1:2] * y1_ref[...]
    return _layer_norm(ALPHA * res_ref[...] + ffn, g_ref[...], beta_ref[...])


def _combine_specs(n):
    nt = n // TOK_TILE
    row = lambda i: (i, 0)
    const = lambda i: (0, 0)
    return [pl.BlockSpec((TOK_TILE, D_MODEL), row),
            pl.BlockSpec((TOK_TILE, D_MODEL), lambda i: (i + nt, 0)),
            pl.BlockSpec((TOK_TILE, D_MODEL), row),
            pl.BlockSpec((TOK_TILE, 2), row),
            pl.BlockSpec((1, D_MODEL), const),
            pl.BlockSpec((1, D_MODEL), const)]


def _combine_split_kernel(y0_ref, y1_ref, res_ref, gate_ref, g_ref, beta_ref, prompt_ref, sample_ref):
    x = _combine(y0_ref, y1_ref, res_ref, gate_ref, g_ref, beta_ref)
    is_sample = pl.program_id(0) == pl.num_programs(0) - 1

    @pl.when(jnp.logical_not(is_sample))
    def _():
        prompt_ref[...] = x

    @pl.when(is_sample)
    def _():
        sample_ref[...] = x


def _combine_split(ys, res, gates_col, g, beta, *, name):
    n = res.shape[0]
    tm = TOK_TILE
    nt = n // tm
    return pl.pallas_call(
        _combine_split_kernel,
        grid=(nt,),
        in_specs=_combine_specs(n),
        out_specs=[pl.BlockSpec((tm, D_MODEL), lambda i: (jnp.minimum(i, nt - 2), 0)),
                   pl.BlockSpec((tm, D_MODEL), lambda i: (0, 0))],
        out_shape=[jax.ShapeDtypeStruct((n - tm, D_MODEL), F32),
                   jax.ShapeDtypeStruct((tm, D_MODEL), F32)],
        compiler_params=_params(("arbitrary",), 40),
        name=name,
    )(ys, ys, res, gates_col, g, beta)


def _load_weight(w_hbm, wbf_ref, stage_ref, sems):
    rows = stage_ref.shape[1]
    n_chunks = wbf_ref.shape[0] // rows

    def chunk_copy(c):
        return pltpu.make_async_copy(w_hbm.at[pl.ds(c * rows, rows), :], stage_ref.at[c % 2],
                                     sems.at[c % 2])

    chunk_copy(0).start()
    for c in range(n_chunks):
        if c + 1 < n_chunks:
            chunk_copy(c + 1).start()
        chunk_copy(c).wait()
        wbf_ref[c * rows:(c + 1) * rows, :] = stage_ref[c % 2].astype(BF16)


def _combine_qkv_kernel(y0_ref, y1_ref, res_ref, gate_ref, g_ref, beta_ref, wq_hbm, wkv_hbm,
                        x_ref, q_ref, kv_ref, wq_bf, wkv_bf, stage_q, stage_kv, sems, *, wq_index):
    @pl.when(pl.program_id(0) == 0)
    def _():
        wq = wq_hbm
        for k in wq_index:
            wq = wq.at[k]
        _load_weight(wq, wq_bf, stage_q, sems)
        _load_weight(wkv_hbm, wkv_bf, stage_kv, sems)

    x = _combine(y0_ref, y1_ref, res_ref, gate_ref, g_ref, beta_ref)
    x_ref[...] = x
    x_bf = x.astype(BF16)
    q = jnp.dot(x_bf, wq_bf[...], preferred_element_type=F32) * (HEAD_DIM ** -0.5)
    q_ref[...] = q.astype(BF16)
    kv_ref[...] = jnp.dot(x_bf, wkv_bf[...], preferred_element_type=F32)


def _combine_qkv(ys, res, gates_col, g, beta, w_q, wq_index, w_kv, *, name):
    n = res.shape[0]
    tm = TOK_TILE
    row = lambda i: (i, 0)
    hbm = pl.BlockSpec(memory_space=pl.ANY)
    return pl.pallas_call(
        functools.partial(_combine_qkv_kernel, wq_index=wq_index),
        grid=(n // tm,),
        in_specs=_combine_specs(n) + [hbm, hbm],
        out_specs=[pl.BlockSpec((tm, D_MODEL), row),
                   pl.BlockSpec((tm, D_MODEL), row),
                   pl.BlockSpec((tm, 2 * KV_DIM), row)],
        out_shape=[jax.ShapeDtypeStruct((n, D_MODEL), F32),
                   jax.ShapeDtypeStruct((n, D_MODEL), BF16),
                   jax.ShapeDtypeStruct((n, 2 * KV_DIM), F32)],
        scratch_shapes=[pltpu.VMEM((D_MODEL, D_MODEL), BF16),
                        pltpu.VMEM((D_MODEL, 2 * KV_DIM), BF16),
                        pltpu.VMEM((2, CAST_ROWS, D_MODEL), F32),
                        pltpu.VMEM((2, CAST_ROWS, 2 * KV_DIM), F32),
                        pltpu.SemaphoreType.DMA((2,))],
        compiler_params=_params(("arbitrary",), 52),
        name=name,
    )(ys, ys, res, gates_col, g, beta, w_q, w_kv)


def _moe_block(x_rows, e_idx, w_gate, w_up, w_down, layer):
    n = x_rows.shape[0] // ROW_SUBLANES
    n_tiles = -(-(2 * n + N_EXPERTS * (MOE_TILE - 1)) // MOE_TILE)
    pos, tile_expert, n_used, next_expert = _plan(e_idx, name=f"moe_plan_{layer}")
    pair = _invert(pos, n_tiles * MOE_TILE, name=f"moe_invert_{layer}")
    return _moe_tiles(x_rows, pair, tile_expert, n_used, next_expert, w_gate, w_up, w_down, layer)


def _sigmoid_of_half(half_x):
    return 0.5 * jnp.tanh(half_x) + 0.5


def _log_sigmoid(x):
    return -(jnp.maximum(-x, 0.0) + jnp.log1p(jnp.exp(-jnp.abs(x))))


def _lru_gate_blocks(xcs, blocks, wrg_bf, wig_bf, brg_ref, big_ref, lam_ref):
    xbs = [xc.astype(BF16) for xc in xcs]
    r_lin = [jnp.dot(xb, wrg_bf[n], preferred_element_type=F32) for xb, n in zip(xbs, blocks)]
    i_lin = [jnp.dot(xb, wig_bf[n], preferred_element_type=F32) for xb, n in zip(xbs, blocks)]
    out = []
    for xc, n, rl, il in zip(xcs, blocks, r_lin, i_lin):
        cols = slice(n * LRU_BLOCK, (n + 1) * LRU_BLOCK)
        r = _sigmoid_of_half(rl + 0.5 * brg_ref[:, cols])
        i = _sigmoid_of_half(il + 0.5 * big_ref[:, cols])
        log_a = r * (LRU_C * _log_sigmoid(lam_ref[:, cols]))
        a = jnp.exp(log_a)
        u = xc * i * jnp.sqrt(-jnp.tanh(log_a) * (a * a + 1.0))
        out.append((a, u))
    return out


def _lru_gate_block(xc, n, wrg_bf, wig_bf, brg_ref, big_ref, lam_ref):
    return _lru_gate_blocks([xc], [n], wrg_bf, wig_bf, brg_ref, big_ref, lam_ref)[0]


def _cast_gate_weights(wrg_ref, wig_ref, wrg_bf, wig_bf):
    for n in range(LRU_BLOCKS):
        wrg_bf[n] = (0.5 * wrg_ref[n]).astype(BF16)
        wig_bf[n] = (0.5 * wig_ref[n]).astype(BF16)


def _lru_prompt_kernel(xb_ref, yb_ref, cw_ref, cb_ref, wrg_ref, wig_ref, brg_ref, big_ref, lam_ref,
                       m_ref, conv_ref, hlast_ref, xs, tail, a_s, u_s, hs_t, h_s, wrg_bf, wig_bf):
    b = pl.program_id(0)
    j = pl.program_id(1)
    tt = m_ref.shape[0]
    seg_len = tt // SEGS
    taps = CONV_WIDTH - 1
    head = SEGS * taps

    @pl.when(jnp.logical_and(b == 0, j == 0))
    def _():
        _cast_gate_weights(wrg_ref, wig_ref, wrg_bf, wig_bf)

    @pl.when(j == 0)
    def _():
        tail[...] = jnp.zeros_like(tail)
        h_s[...] = jnp.zeros_like(h_s)

    for q in range(seg_len):
        xs[head + SEGS * q:head + SEGS * (q + 1), :] = jnp.concatenate(
            [xb_ref[pl.ds(_chunk_row(q, c), SEGS, stride=CHUNK_SEG_ROWS), :] for c in range(CHUNKS)],
            axis=1)
    sub = lax.broadcasted_iota(jnp.int32, (SEGS, D_MODEL), 0)
    for k in range(taps):
        last = head + SEGS * (seg_len - taps + k)
        joined = jnp.where(sub == SEGS - 1, tail[SEGS * k:SEGS * (k + 1), :], xs[last:last + SEGS, :])
        xs[SEGS * k:SEGS * (k + 1), :] = pltpu.roll(joined, 1, axis=0)
    tail[...] = xs[head + SEGS * (seg_len - taps):head + SEGS * seg_len, :]

    for first_block in range(0, LRU_BLOCKS, GATE_BLOCKS):
        blocks = range(first_block, first_block + GATE_BLOCKS)
        xcs = []
        for n in blocks:
            cols = slice(n * LRU_BLOCK, (n + 1) * LRU_BLOCK)
            xc = cb_ref[:, cols] + cw_ref[0:1, cols] * xs[0:tt, cols]
            for k in range(1, CONV_WIDTH):
                xc = xc + cw_ref[k:k + 1, cols] * xs[SEGS * k:SEGS * k + tt, cols]
            xcs.append(xc)
        gates = _lru_gate_blocks(xcs, blocks, wrg_bf, wig_bf, brg_ref, big_ref, lam_ref)
        for n, (a, u) in zip(blocks, gates):
            cols = slice(n * LRU_BLOCK, (n + 1) * LRU_BLOCK)
            a_s[:, cols] = a
            u_s[:, cols] = u

    def scan_body(q, carry):
        h, prod = carry
        rows = pl.ds(pl.multiple_of(q * SEGS, SEGS), SEGS)
        a = a_s[rows, :]
        h = a * h + u_s[rows, :]
        prod = a * prod
        u_s[rows, :] = h
        a_s[rows, :] = prod
        return h, prod

    h_end, prod_end = lax.fori_loop(
        0, seg_len, scan_body,
        (jnp.zeros((SEGS, D_MODEL), F32), jnp.ones((SEGS, D_MODEL), F32)))
    state = h_s[...]
    entering = []
    for s in range(SEGS):
        entering.append(state)
        state = h_end[s:s + 1, :] + prod_end[s:s + 1, :] * state
    h_s[...] = state
    enter = jnp.concatenate(entering, axis=0)

    def fix_body(q, carry):
        rows = pl.ds(pl.multiple_of(q * SEGS, SEGS), SEGS)
        h = u_s[rows, :] + a_s[rows, :] * enter
        for c in range(CHUNKS):
            hs_t[pl.ds(q * CHUNK_TOK_ROWS + c, SEGS, stride=CHUNK_SEG_ROWS), :] = h[:, c * LANES:(c + 1) * LANES]
        return carry

    lax.fori_loop(0, seg_len, fix_body, 0)
    for s in range(SEGS):
        rows = slice(s * seg_len, (s + 1) * seg_len)
        hs = _chunk_rows_load(hs_t, s * seg_len, seg_len)
        m_ref[rows, :] = (hs * yb_ref[rows, :].astype(F32)).astype(BF16)

    @pl.when(j == pl.num_programs(1) - 1)
    def _():
        for k in range(taps):
            conv_ref[k:k + 1, :] = tail[SEGS * k + SEGS - 1:SEGS * (k + 1), :]
        hlast_ref[...] = state


def _lru_prompt(xb, yb, batch, seq, cw, cb, wrg, wig, brg, big, lam):
    tt = TOK_TILE
    nj = seq // tt
    row = lambda b, j: (b * nj + j, 0)
    const2 = lambda b, j: (0, 0)
    const3 = lambda b, j: (0, 0, 0)
    return pl.pallas_call(
        _lru_prompt_kernel,
        grid=(batch, nj),
        in_specs=[
            pl.BlockSpec((CHUNK_TILE_ROWS, LANES), row),
            pl.BlockSpec((tt, D_MODEL), row),
            pl.BlockSpec((CONV_WIDTH, D_MODEL), const2),
            pl.BlockSpec((1, D_MODEL), const2),
            pl.BlockSpec((LRU_BLOCKS, LRU_BLOCK, LRU_BLOCK), const3),
            pl.BlockSpec((LRU_BLOCKS, LRU_BLOCK, LRU_BLOCK), const3),
            pl.BlockSpec((1, D_MODEL), const2),
            pl.BlockSpec((1, D_MODEL), const2),
            pl.BlockSpec((1, D_MODEL), const2),
        ],
        out_specs=[
            pl.BlockSpec((tt, D_MODEL), row),
            pl.BlockSpec((None, CONV_WIDTH - 1, D_MODEL), lambda b, j: (b, 0, 0)),
            pl.BlockSpec((None, 1, D_MODEL), lambda b, j: (b, 0, 0)),
        ],
        out_shape=[
            jax.ShapeDtypeStruct((batch * seq, D_MODEL), BF16),
            jax.ShapeDtypeStruct((batch, CONV_WIDTH - 1, D_MODEL), F32),
            jax.ShapeDtypeStruct((batch, 1, D_MODEL), F32),
        ],
        scratch_shapes=[
            pltpu.VMEM((tt + SEGS * (CONV_WIDTH - 1), D_MODEL), F32),
            pltpu.VMEM((SEGS * (CONV_WIDTH - 1), D_MODEL), F32),
            pltpu.VMEM((tt, D_MODEL), F32),
            pltpu.VMEM((tt, D_MODEL), F32),
            pltpu.VMEM((CHUNK_TILE_ROWS, LANES), F32),
            pltpu.VMEM((1, D_MODEL), F32),
            pltpu.VMEM((LRU_BLOCKS, LRU_BLOCK, LRU_BLOCK), BF16),
            pltpu.VMEM((LRU_BLOCKS, LRU_BLOCK, LRU_BLOCK), BF16),
        ],
        compiler_params=_params(("arbitrary", "arbitrary"), 40),
        name="lru_prompt",
    )(xb, yb, cw, cb, wrg, wig, brg, big, lam)


def _lru_sample_kernel(xb_ref, yb_ref, cs_ref, h0_ref, cw_ref, cb_ref, wrg_ref, wig_ref,
                       brg_ref, big_ref, lam_ref, m_ref, conv_ref, hlast_ref, wrg_bf, wig_bf, *, steps):
    batch = h0_ref.shape[0]
    _cast_gate_weights(wrg_ref, wig_ref, wrg_bf, wig_bf)
    m_ref[steps * batch:, :] = jnp.zeros((m_ref.shape[0] - steps * batch, D_MODEL), BF16)

    def slab(t, cols):
        if t < CONV_WIDTH - 1:
            return cs_ref[t, :, cols]
        t -= CONV_WIDTH - 1
        first, stop, _ = cols.indices(D_MODEL)
        return _chunk_rows_load(xb_ref, t * batch, batch, range(first // LANES, stop // LANES))

    for n in range(LRU_BLOCKS):
        cols = slice(n * LRU_BLOCK, (n + 1) * LRU_BLOCK)
        h = h0_ref[:, cols]
        for t in range(steps):
            xc = cb_ref[:, cols] + cw_ref[0:1, cols] * slab(t, cols)
            for k in range(1, CONV_WIDTH):
                xc = xc + cw_ref[k:k + 1, cols] * slab(t + k, cols)
            a, u = _lru_gate_block(xc, n, wrg_bf, wig_bf, brg_ref, big_ref, lam_ref)
            h = a * h + u
            rows = slice(t * batch, (t + 1) * batch)
            m_ref[rows, cols] = (h * yb_ref[rows, cols].astype(F32)).astype(BF16)
        hlast_ref[:, cols] = h
    for k in range(CONV_WIDTH - 1):
        conv_ref[k] = slab(steps + k, slice(None))


def _lru_sample(xb, yb, tile, steps, conv_state, h0, cw, cb, wrg, wig, brg, big, lam):
    batch = h0.shape[0]
    tok = pl.BlockSpec((TOK_TILE, D_MODEL), lambda i: (tile, 0))
    tok_chunks = pl.BlockSpec((CHUNK_TILE_ROWS, LANES), lambda i: (tile, 0))
    full = lambda a: pl.BlockSpec(a.shape, lambda i: (0,) * a.ndim)
    small = (conv_state, h0, cw, cb, wrg, wig, brg, big, lam)
    return pl.pallas_call(
        functools.partial(_lru_sample_kernel, steps=steps),
        grid=(1,),
        in_specs=[tok_chunks, tok] + [full(a) for a in small],
        out_specs=[
            pl.BlockSpec((TOK_TILE, D_MODEL), lambda i: (0, 0)),
            pl.BlockSpec((CONV_WIDTH - 1, batch, D_MODEL), lambda i: (0, 0, 0)),
            pl.BlockSpec((batch, D_MODEL), lambda i: (0, 0)),
        ],
        out_shape=[
            jax.ShapeDtypeStruct((TOK_TILE, D_MODEL), BF16),
            jax.ShapeDtypeStruct((CONV_WIDTH - 1, batch, D_MODEL), F32),
            jax.ShapeDtypeStruct((batch, D_MODEL), F32),
        ],
        scratch_shapes=[
            pltpu.VMEM((LRU_BLOCKS, LRU_BLOCK, LRU_BLOCK), BF16),
            pltpu.VMEM((LRU_BLOCKS, LRU_BLOCK, LRU_BLOCK), BF16),
        ],
        compiler_params=_params(("arbitrary",), 32),
        name="lru_sample",
    )(xb, yb, *small)


def _rel_bucket(dist):
    n = jnp.maximum(dist, 0)
    max_exact = N_BUCKETS // 2
    nf = jnp.maximum(n, 1).astype(F32)
    large = max_exact + (jnp.log(nf / max_exact) / math.log(MAX_DISTANCE / max_exact)
                         * (N_BUCKETS - max_exact)).astype(jnp.int32)
    large = jnp.minimum(large, N_BUCKETS - 1)
    return jnp.where(n < max_exact, n, large)


def _masked_buckets(dist):
    valid = (dist >= 0) & (dist < WINDOW)
    return jnp.where(valid, _rel_bucket(dist), -1).astype(jnp.int32)


def _build_bias(bucket, tab_ref, head):
    def body(bi, acc):
        return jnp.where(bucket == bi, tab_ref[bi * N_HEADS + head], acc)
    return lax.fori_loop(0, N_BUCKETS, body, jnp.full(bucket.shape, NEG_INF, F32))


def _softmax_pv(s, sink, v):
    m = jnp.maximum(jnp.max(s, axis=-1, keepdims=True), sink)
    p = jnp.exp(s - m)
    den = jnp.sum(p, axis=-1, keepdims=True) + jnp.exp(sink - m)
    return jnp.dot(p.astype(BF16), v, preferred_element_type=F32) / den


def _attn_prompt_kernel(q_ref, kvp_ref, kvc_ref, bucket_ref, tab_ref, sink_ref, o_ref, bias_s):
    b = pl.program_id(0)
    n = pl.program_id(1)

    @pl.when(jnp.logical_and(b == 0, n == 0))
    def _():
        bucket = bucket_ref[...]

        col = lax.broadcasted_iota(jnp.int32, (WINDOW, 2 * WINDOW), 1)

        def head_body(h, c):
            by_offset = jnp.broadcast_to(_build_bias(bucket, tab_ref, h), (WINDOW, 2 * WINDOW))
            bias = pltpu.roll(by_offset, 0, axis=1, stride=1, stride_axis=0)
            sink = sink_ref[h]
            g = h // GROUP
            r0 = pl.multiple_of((h % GROUP) * WINDOW, WINDOW)
            bias_s[0, g, pl.ds(r0, WINDOW), :] = jnp.where(col == 0, sink, bias)
            bias_s[1, g, pl.ds(r0, WINDOW), :] = jnp.where(
                col == 0, sink, jnp.where(col < WINDOW, NEG_INF, bias))
            return c

        lax.fori_loop(0, N_HEADS, head_body, 0)

    first = (n == 0).astype(jnp.int32)
    row = lax.broadcasted_iota(jnp.int32, kvp_ref.shape, 0)
    kv_prev = jnp.where(row == 0, 0.0, kvp_ref[...])
    kv = jnp.concatenate([kv_prev, kvc_ref[...]], axis=0).astype(BF16)
    ones = jnp.ones((2 * WINDOW, 2 * HEAD_DIM), BF16)
    lane = lax.broadcasted_iota(jnp.int32, (WINDOW, 2 * HEAD_DIM), 1)
    def scores(idx):
        g, pair = divmod(idx, GROUP // 2)
        h0 = g * GROUP + 2 * pair
        kg = kv[:, g * HEAD_DIM:(g + 1) * HEAD_DIM]
        qp = jnp.concatenate([q_ref[:, h * HEAD_DIM:(h + 1) * HEAD_DIM] for h in (h0, h0 + 1)], axis=0)
        s = lax.dot_general(qp, kg, (((1,), (1,)), ((), ())), preferred_element_type=F32)
        return s + bias_s[first, g, 2 * pair * WINDOW:(2 * pair + 2) * WINDOW, :]

    def finish(idx, o_ext):
        h0 = 2 * idx
        o = o_ext[:, :2 * HEAD_DIM] * (1.0 / o_ext[:, 2 * HEAD_DIM:])
        o_ref[:, h0 * HEAD_DIM:(h0 + 2) * HEAD_DIM] = jnp.where(
            lane < HEAD_DIM, o[:WINDOW], o[WINDOW:]).astype(BF16)

    n_pairs = N_HEADS // 2

    def values(idx, p):
        g = idx // (GROUP // 2)
        vg = kv[:, KV_DIM + g * HEAD_DIM:KV_DIM + (g + 1) * HEAD_DIM]
        v_ext = jnp.concatenate([vg, vg, ones], axis=1)
        return jnp.dot(p, v_ext, preferred_element_type=F32)

    for first_pair in range(0, n_pairs, PAIR_BLOCK):
        block = range(first_pair, first_pair + PAIR_BLOCK)
        ss = [scores(idx) for idx in block]
        ms = [jnp.max(s, axis=-1, keepdims=True) for s in ss]
        ps = [jnp.exp(s - m).astype(BF16) for s, m in zip(ss, ms)]
        os_ = [values(idx, p) for idx, p in zip(block, ps)]
        for idx, o_ext in zip(block, os_):
            finish(idx, o_ext)


def _attn_prompt(q, kv, batch, seq, bucket, tab, sinks):
    nb = seq // WINDOW
    smem = pl.BlockSpec(memory_space=pltpu.SMEM)
    return pl.pallas_call(
        _attn_prompt_kernel,
        grid=(batch, nb),
        in_specs=[
            pl.BlockSpec((WINDOW, D_MODEL), lambda b, n: (b * nb + n, 0)),
            pl.BlockSpec((WINDOW, 2 * KV_DIM), lambda b, n: (jnp.maximum(b * nb + n - 1, 0), 0)),
            pl.BlockSpec((WINDOW, 2 * KV_DIM), lambda b, n: (b * nb + n, 0)),
            pl.BlockSpec((1, 2 * WINDOW), lambda b, n: (0, 0)),
            smem, smem,
        ],
        out_specs=pl.BlockSpec((WINDOW, D_MODEL), lambda b, n: (b * nb + n, 0)),
        out_shape=jax.ShapeDtypeStruct((batch * seq, D_MODEL), BF16),
        scratch_shapes=[pltpu.VMEM((2, N_KV_HEADS, GROUP * WINDOW, 2 * WINDOW), F32)],
        compiler_params=_params(("arbitrary", "arbitrary"), 32),
        name="attn_prompt",
    )(q, kv, kv, bucket, tab, sinks)


def _attn_sample_kernel(q_ref, k_ref, v_ref, bucket_ref, tab_ref, sink_ref, o_ref, bias_s):
    steps = q_ref.shape[0]

    @pl.when(pl.program_id(0) == 0)
    def _():
        bucket = bucket_ref[...]

        def head_body(h, c):
            bias_s[h] = _build_bias(bucket, tab_ref, h)
            return c

        lax.fori_loop(0, N_HEADS, head_body, 0)

    rows = lax.broadcasted_iota(jnp.int32, (GROUP * steps, 1), 0)
    k = k_ref[...].astype(BF16)
    v = v_ref[...].astype(BF16)
    groups = range(N_KV_HEADS)

    def heads_of(g):
        return range(g * GROUP, (g + 1) * GROUP)

    scores, sinks = [], []
    for g in groups:
        qg = jnp.concatenate([q_ref[:, h * HEAD_DIM:(h + 1) * HEAD_DIM] for h in heads_of(g)], axis=0)
        bias = jnp.concatenate([bias_s[h] for h in heads_of(g)], axis=0)
        sink = jnp.full((GROUP * steps, 1), sink_ref[g * GROUP], F32)
        for hh in range(1, GROUP):
            sink = jnp.where(rows >= hh * steps, sink_ref[g * GROUP + hh], sink)
        kg = k[:, g * HEAD_DIM:(g + 1) * HEAD_DIM]
        scores.append(lax.dot_general(qg, kg, (((1,), (1,)), ((), ())),
                                      preferred_element_type=F32) + bias)
        sinks.append(sink)
    outs = [_softmax_pv(scores[g], sinks[g], v[:, g * HEAD_DIM:(g + 1) * HEAD_DIM]) for g in groups]
    for g in groups:
        for hh, h in enumerate(heads_of(g)):
            o_ref[:, h * HEAD_DIM:(h + 1) * HEAD_DIM] = outs[g][hh * steps:(hh + 1) * steps].astype(BF16)


def _attn_sample(q, k_all, v_all, bucket, tab, sinks):
    batch, steps, _ = q.shape
    lk = k_all.shape[1]
    smem = pl.BlockSpec(memory_space=pltpu.SMEM)
    return pl.pallas_call(
        _attn_sample_kernel,
        grid=(batch,),
        in_specs=[
            pl.BlockSpec((None, steps, D_MODEL), lambda b: (b, 0, 0)),
            pl.BlockSpec((None, lk, KV_DIM), lambda b: (b, 0, 0)),
            pl.BlockSpec((None, lk, KV_DIM), lambda b: (b, 0, 0)),
            pl.BlockSpec((steps, lk), lambda b: (0, 0)),
            smem, smem,
        ],
        out_specs=pl.BlockSpec((None, steps, D_MODEL), lambda b: (b, 0, 0)),
        out_shape=jax.ShapeDtypeStruct((batch, steps, D_MODEL), BF16),
        scratch_shapes=[pltpu.VMEM((N_HEADS, steps, lk), F32)],
        compiler_params=_params(("arbitrary",), 32),
        name="attn_sample",
    )(q, k_all, v_all, bucket, tab, sinks)


def kernel(x_prompt, x_sample, state_conv, state_rnn, cache_k_win, cache_v_win, ln_g, ln_b, lru_w_x, lru_b_x, lru_w_y, lru_b_y, lru_conv_w, lru_conv_b, lru_w_rg, lru_b_rg, lru_w_ig, lru_b_ig, lru_lam, lru_w_out, lru_b_out, attn_w_kv, attn_w_q, attn_w_o, attn_sinks, rel_bias, moe_w_router, moe_b_router, moe_w_gate, moe_w_up, moe_w_down):
    bp, seq, _ = x_prompt.shape
    bs, steps, _ = x_sample.shape
    n_p = bp * seq
    n_s = bs * steps

    assert n_p % TOK_TILE == 0 and n_s <= TOK_TILE
    sample_tile = n_p // TOK_TILE

    def pad_tile(rows):
        return jnp.pad(rows, ((0, TOK_TILE - n_s), (0, 0)))

    x0 = (x_prompt.reshape(n_p, D_MODEL),
          pad_tile(x_sample.transpose(1, 0, 2).reshape(n_s, D_MODEL)))
    wr_t = moe_w_router.T
    br = moe_b_router.reshape(N_EXPERTS, 1)
    vec = lambda a: a.reshape(1, -1)

    xb, yb = _lru_in(x0, lru_w_x, vec(lru_b_x[0]), lru_w_y, vec(lru_b_y[0]), 0)
    lru_args = (lru_conv_w[0], vec(lru_conv_b[0]), lru_w_rg[0], lru_w_ig[0],
                vec(lru_b_rg[0]), vec(lru_b_ig[0]), vec(lru_lam[0]))
    m_p, conv_p, rnn_p = _lru_prompt(xb, yb, bp, seq, *lru_args)
    m_s, conv_s, rnn_s = _lru_sample(xb, yb, sample_tile, steps,
                                     state_conv[0].transpose(1, 0, 2), state_rnn[0], *lru_args)
    x1, x1_rows, e_idx, gates = _proj_ln((m_p, m_s), lru_w_out, (0,), vec(lru_b_out[0]), x0,
                                vec(ln_g[0, 0]), vec(ln_b[0, 0]), wr_t, br, name="lru_out_ln")
    ys = _moe_block(x1_rows, e_idx, moe_w_gate, moe_w_up, moe_w_down, 0)

    x2, q, kv = _combine_qkv(ys, x1, gates.T, vec(ln_g[0, 1]), vec(ln_b[0, 1]),
                             attn_w_q, (0,), attn_w_kv, name="moe_combine_qkv")
    tab = rel_bias.reshape(-1)
    sinks = attn_sinks[0]
    offsets = jnp.arange(2 * WINDOW)[None, :]
    o_p = _attn_prompt(q, kv, bp, seq, _masked_buckets(WINDOW - offsets), tab, sinks)
    kv_s = kv[n_p:n_p + n_s].reshape(steps, bs, 2, KV_DIM).transpose(2, 1, 0, 3)
    k_all = jnp.concatenate([cache_k_win.reshape(bs, WINDOW, KV_DIM), kv_s[0]], axis=1)
    v_all = jnp.concatenate([cache_v_win.reshape(bs, WINDOW, KV_DIM), kv_s[1]], axis=1)
    dist_s = jnp.arange(steps)[:, None] + WINDOW - jnp.arange(WINDOW + steps)[None, :]
    q_s = q[n_p:n_p + n_s].reshape(steps, bs, D_MODEL).transpose(1, 0, 2)
    o_s = _attn_sample(q_s, k_all, v_all, _masked_buckets(dist_s), tab, sinks)
    o_s = pad_tile(o_s.transpose(1, 0, 2).reshape(n_s, D_MODEL))
    x3, x3_rows, e_idx, gates = _proj_ln((o_p, o_s), attn_w_o, (0,), jnp.zeros((1, D_MODEL), F32), x2,
                                vec(ln_g[1, 0]), vec(ln_b[1, 0]), wr_t, br, name="attn_out_ln")
    ys = _moe_block(x3_rows, e_idx, moe_w_gate, moe_w_up, moe_w_down, 1)
    y_p, y_s = _combine_split(ys, x3, gates.T, vec(ln_g[1, 1]), vec(ln_b[1, 1]), name="moe_combine_1")

    y_prompt = y_p.reshape(bp, seq, D_MODEL)
    y_sample = y_s[:n_s].reshape(steps, bs, D_MODEL).transpose(1, 0, 2)
    kv_p = jnp.stack([kv[(b + 1) * seq - WINDOW:(b + 1) * seq] for b in range(bp)])
    kv_p = kv_p.reshape(bp, WINDOW, 2, N_KV_HEADS, HEAD_DIM)
    k_win_s = k_all[:, steps:].reshape(bs, WINDOW, N_KV_HEADS, HEAD_DIM)
    v_win_s = v_all[:, steps:].reshape(bs, WINDOW, N_KV_HEADS, HEAD_DIM)
    return (y_prompt, y_sample,
            conv_p[None], rnn_p.reshape(1, bp, D_MODEL),
            kv_p[:, :, 0], kv_p[:, :, 1],
            conv_s.transpose(1, 0, 2)[None], rnn_s[None],
            k_win_s, v_win_s)
```

```python
import functools
import math

import jax
import jax.numpy as jnp
from jax import lax
from jax.experimental import pallas as pl
from jax.experimental.pallas import tpu as pltpu

D_MODEL = 2048
DEPTH = 2
LRU_BLOCKS = 8
LRU_BLOCK = D_MODEL // LRU_BLOCKS
CONV_WIDTH = 4
LRU_C = 8.0
N_HEADS = 32
HEAD_DIM = 64
N_KV_HEADS = 8
GROUP = N_HEADS // N_KV_HEADS
KV_DIM = N_KV_HEADS * HEAD_DIM
WINDOW = 128
N_BUCKETS = 32
MAX_DISTANCE = 128
N_EXPERTS = 16
N_GROUPS = 4
EXPERTS_PER_GROUP = N_EXPERTS // N_GROUPS
D_EXPERT = 1024
ALPHA = (2 * DEPTH) ** 0.25
LN_EPS = 1e-5

LANES = 128
SEGS = 8
CHUNKS = D_MODEL // LANES
ROW_SUBLANES = D_MODEL // (2 * LANES)
MOE_TILE = 256
TOK_TILE = 256
CHUNK_TOK_ROWS = CHUNKS + 4
CHUNK_SEG_TOKS = TOK_TILE // SEGS
CHUNK_SEG_ROWS = CHUNK_SEG_TOKS * CHUNK_TOK_ROWS + 4
CHUNK_TILE_ROWS = SEGS * CHUNK_SEG_ROWS
DMA_UNROLL = 8
SCALAR_UNROLL = 32
PLAN_UNROLL = 4
PAIR_BLOCK = 4
GATE_BLOCKS = 4
W_CHUNKS = 4
CAST_ROWS = 256
BF16 = jnp.bfloat16
F32 = jnp.float32
NEG_INF = float("-inf")


def _params(sem, vmem_mb):
    return pltpu.CompilerParams(dimension_semantics=sem, vmem_limit_bytes=vmem_mb * 1024 * 1024)


def _cast_rows(src_ref, dst_ref):
    n = src_ref.shape[0] // CAST_ROWS

    def body(i, c):
        r = pl.multiple_of(i * CAST_ROWS, CAST_ROWS)
        dst_ref[pl.ds(r, CAST_ROWS), :] = src_ref[pl.ds(r, CAST_ROWS), :].astype(BF16)
        return c

    lax.fori_loop(0, n, body, 0)


def _layer_norm(z, g, b):
    mu = jnp.mean(z, axis=-1, keepdims=True)
    zc = z - mu
    var = jnp.mean(zc * zc, axis=-1, keepdims=True)
    return zc * lax.rsqrt(var + LN_EPS) * g + b


def _chunk_row(tok, chunk):
    seg, t = divmod(tok, CHUNK_SEG_TOKS)
    return seg * CHUNK_SEG_ROWS + t * CHUNK_TOK_ROWS + chunk


def _chunk_rows_store(ref, y):
    ref[...] = jnp.zeros_like(ref)
    for seg in range(SEGS):
        rows = slice(seg * CHUNK_SEG_TOKS, (seg + 1) * CHUNK_SEG_TOKS)
        for c in range(CHUNKS):
            dst = pl.ds(_chunk_row(seg * CHUNK_SEG_TOKS, c), CHUNK_SEG_TOKS, stride=CHUNK_TOK_ROWS)
            ref[dst, :] = y[rows, c * LANES:(c + 1) * LANES]


def _chunk_rows_load(ref, tok0, count, chunks=range(CHUNKS)):
    assert tok0 // CHUNK_SEG_TOKS == (tok0 + count - 1) // CHUNK_SEG_TOKS
    return jnp.concatenate(
        [ref[pl.ds(_chunk_row(tok0, c), count, stride=CHUNK_TOK_ROWS), :] for c in chunks], axis=1)


def _store_packed_rows(x_bf, rows_ref):
    n = x_bf.shape[0]
    bits = pltpu.bitcast(x_bf.astype(F32), jnp.uint32)
    packed = bits[:, D_MODEL // 2:] | (bits[:, :D_MODEL // 2] >> 16)
    for c in range(ROW_SUBLANES):
        rows_ref[pl.ds(c, n, stride=ROW_SUBLANES), :] = packed[:, c * LANES:(c + 1) * LANES]


def _load_packed_rows(rows_ref, x_bf_ref):
    n = x_bf_ref.shape[0]
    for c in range(ROW_SUBLANES):
        words = rows_ref[pl.ds(c, n, stride=ROW_SUBLANES), :]
        low = pltpu.bitcast(words << 16, F32).astype(BF16)
        high = pltpu.bitcast(words & jnp.uint32(0xFFFF0000), F32).astype(BF16)
        x_bf_ref[:, c * LANES:(c + 1) * LANES] = low
        x_bf_ref[:, D_MODEL // 2 + c * LANES:D_MODEL // 2 + (c + 1) * LANES] = high


def _tok_operands(x, tile_of=lambda i: i):
    if isinstance(x, tuple):
        xp, xs = x
        d = xp.shape[1]
        last_p = xp.shape[0] // TOK_TILE - 1
        specs = [pl.BlockSpec((TOK_TILE, d), lambda i, *_: (jnp.minimum(tile_of(i), last_p), 0)),
                 pl.BlockSpec((TOK_TILE, d), lambda i, *_: (0, 0))]
        return [xp, xs], specs, last_p + 2
    return ([x], [pl.BlockSpec((TOK_TILE, x.shape[1]), lambda i, *_: (tile_of(i), 0))],
            x.shape[0] // TOK_TILE)


def _tok_load(refs, is_sample=None):
    if len(refs) == 1:
        return refs[0][...]
    if is_sample is None:
        is_sample = pl.program_id(0) == pl.num_programs(0) - 1
    return jnp.where(is_sample, refs[1][...], refs[0][...])


def _lru_in_kernel(*refs, n_x, layer):
    x_refs = refs[:n_x]
    wx_hbm, wy_hbm, bx_ref, by_ref, xb_ref, yb_ref, wx_bf, wy_bf, stage, sems = refs[n_x:]

    @pl.when(pl.program_id(0) == 0)
    def _():
        _load_weight(wx_hbm.at[layer], wx_bf, stage, sems)
        _load_weight(wy_hbm.at[layer], wy_bf, stage, sems)

    x = _tok_load(x_refs).astype(BF16)
    _chunk_rows_store(xb_ref, jnp.dot(x, wx_bf[...], preferred_element_type=F32) + bx_ref[...])
    y = jnp.dot(x, wy_bf[...], preferred_element_type=F32) + by_ref[...]
    yb_ref[...] = jax.nn.gelu(y).astype(BF16)


def _lru_in(x, w_x, b_x, w_y, b_y, layer):
    arrays, specs, nt = _tok_operands(x)
    hbm = pl.BlockSpec(memory_space=pl.ANY)
    vec_spec = pl.BlockSpec((1, D_MODEL), lambda i: (0, 0))
    return pl.pallas_call(
        functools.partial(_lru_in_kernel, n_x=len(arrays), layer=layer),
        grid=(nt,),
        in_specs=specs + [hbm, hbm, vec_spec, vec_spec],
        out_specs=[pl.BlockSpec((CHUNK_TILE_ROWS, LANES), lambda i: (i, 0)),
                   pl.BlockSpec((TOK_TILE, D_MODEL), lambda i: (i, 0))],
        out_shape=[jax.ShapeDtypeStruct((nt * CHUNK_TILE_ROWS, LANES), F32),
                   jax.ShapeDtypeStruct((nt * TOK_TILE, D_MODEL), BF16)],
        scratch_shapes=[pltpu.VMEM((D_MODEL, D_MODEL), BF16), pltpu.VMEM((D_MODEL, D_MODEL), BF16),
                        pltpu.VMEM((2, CAST_ROWS, D_MODEL), F32), pltpu.SemaphoreType.DMA((2,))],
        compiler_params=_params(("arbitrary",), 48),
        name="lru_in",
    )(*arrays, w_x, w_y, b_x, b_y)


def _route(logits_t, b_router):
    aff = jax.nn.sigmoid(logits_t)
    sel = aff + b_router
    srow = [sel[e:e + 1, :] for e in range(N_EXPERTS)]
    arow = [aff[e:e + 1, :] for e in range(N_EXPERTS)]

    def top2_sum(v):
        pairs = [v[i] + v[j] for i in range(4) for j in range(i + 1, 4)]
        return functools.reduce(jnp.maximum, pairs)

    scores = [top2_sum(srow[4 * g:4 * g + 4]) for g in range(N_GROUPS)]
    best = scores[0]
    gi = jnp.zeros_like(best, dtype=jnp.int32)
    for g in range(1, N_GROUPS):
        upd = scores[g] > best
        best = jnp.where(upd, scores[g], best)
        gi = jnp.where(upd, g, gi)

    def pick_group(rows, j):
        out = rows[j]
        for g in range(1, N_GROUPS):
            out = jnp.where(gi == g, rows[4 * g + j], out)
        return out

    v = [pick_group(srow, j) for j in range(EXPERTS_PER_GROUP)]
    a = [pick_group(arow, j) for j in range(EXPERTS_PER_GROUP)]

    m1, i1 = v[0], jnp.zeros_like(gi)
    for j in range(1, EXPERTS_PER_GROUP):
        upd = v[j] > m1
        m1 = jnp.where(upd, v[j], m1)
        i1 = jnp.where(upd, j, i1)
    m2 = jnp.full_like(m1, NEG_INF)
    i2 = jnp.zeros_like(gi)
    for j in range(EXPERTS_PER_GROUP):
        cand = jnp.where(i1 == j, NEG_INF, v[j])
        upd = cand > m2
        m2 = jnp.where(upd, cand, m2)
        i2 = jnp.where(upd, j, i2)

    def pick_idx(rows, idx):
        out = rows[0]
        for j in range(1, EXPERTS_PER_GROUP):
            out = jnp.where(idx == j, rows[j], out)
        return out

    a1 = pick_idx(a, i1)
    a2 = pick_idx(a, i2)
    tot = a1 + a2
    e_idx = jnp.concatenate([gi * EXPERTS_PER_GROUP + i1, gi * EXPERTS_PER_GROUP + i2], axis=0)
    gates = jnp.concatenate([a1 / tot, a2 / tot], axis=0)
    return e_idx, gates


def _proj_ln_kernel(*refs, n_m, n_res):
    m_refs = refs[:n_m]
    w_ref, b_ref = refs[n_m:n_m + 2]
    res_refs = refs[n_m + 2:n_m + 2 + n_res]
    (g_ref, beta_ref, wr_ref, br_ref, x_ref, xrow_ref, e_ref, gate_ref,
     wbf_ref, ya, yb) = refs[n_m + 2 + n_res:]
    i = pl.program_id(0)
    n_tiles = pl.num_programs(0) - 1

    @pl.when(i == 0)
    def _():
        _cast_rows(w_ref, wbf_ref)
        yb[...] = jnp.zeros_like(yb)

    for parity, (cur, prev) in enumerate(((ya, yb), (yb, ya))):
        @pl.when(i % 2 == parity)
        def _():
            cur[...] = jnp.dot(_tok_load(m_refs, i >= n_tiles - 1), wbf_ref[...],
                               preferred_element_type=F32)
            y = prev[...] + b_ref[...]
            x = _layer_norm(ALPHA * _tok_load(res_refs, i == n_tiles) + y, g_ref[...], beta_ref[...])
            x_ref[...] = x
            x_bf = x.astype(BF16)
            _store_packed_rows(x_bf, xrow_ref)
            logits_t = lax.dot_general(wr_ref[...].astype(BF16), x_bf,
                                       (((1,), (1,)), ((), ())), preferred_element_type=F32)
            e_idx, gates = _route(logits_t, br_ref[...])
            e_ref[...] = e_idx
            gate_ref[...] = gates


def _proj_ln(m, w, w_index, b, res, g, beta, wr_t, br, *, name):
    nt = _tok_operands(m)[2]
    m_arrays, m_specs, _ = _tok_operands(m, lambda i: jnp.minimum(i, nt - 1))
    res_arrays, res_specs, _ = _tok_operands(res, lambda i: jnp.maximum(i - 1, 0))
    k = w.shape[-2]
    tm = TOK_TILE
    n = nt * tm
    row = lambda i: (jnp.maximum(i - 1, 0), 0)
    const = lambda i: (0, 0)
    x, x_rows, e_idx, gates = pl.pallas_call(
        functools.partial(_proj_ln_kernel, n_m=len(m_arrays), n_res=len(res_arrays)),
        grid=(nt + 1,),
        in_specs=m_specs + [
            pl.BlockSpec((None,) * len(w_index) + (k, D_MODEL), lambda i: w_index + (0, 0),
                         pipeline_mode=pl.Buffered(1)),
            pl.BlockSpec((1, D_MODEL), const),
        ] + res_specs + [
            pl.BlockSpec((1, D_MODEL), const),
            pl.BlockSpec((1, D_MODEL), const),
            pl.BlockSpec((N_EXPERTS, D_MODEL), const),
            pl.BlockSpec((N_EXPERTS, 1), const),
        ],
        out_specs=[
            pl.BlockSpec((tm, D_MODEL), row),
            pl.BlockSpec((tm * ROW_SUBLANES, LANES), row),
            pl.BlockSpec((None, 2, tm), lambda i: (jnp.maximum(i - 1, 0), 0, 0)),
            pl.BlockSpec((None, 2, tm), lambda i: (jnp.maximum(i - 1, 0), 0, 0)),
        ],
        out_shape=[
            jax.ShapeDtypeStruct((n, D_MODEL), F32),
            jax.ShapeDtypeStruct((n * ROW_SUBLANES, LANES), jnp.uint32),
            jax.ShapeDtypeStruct((nt, 2, tm), jnp.int32),
            jax.ShapeDtypeStruct((nt, 2, tm), F32),
        ],
        scratch_shapes=[pltpu.VMEM((k, D_MODEL), BF16),
                        pltpu.VMEM((tm, D_MODEL), F32), pltpu.VMEM((tm, D_MODEL), F32)],
        compiler_params=_params(("arbitrary",), 52),
        name=name,
    )(*m_arrays, w, b, *res_arrays, g, beta, wr_t, br)
    e_idx = e_idx.transpose(1, 0, 2).reshape(2, n)
    gates = gates.transpose(1, 0, 2).reshape(2, n)
    return x, x_rows, e_idx, gates


def _plan_kernel(e_ref, pos_ref, meta_ref, rank_ref):
    nrow = e_ref.shape[0]
    ri = lax.broadcasted_iota(jnp.int32, (LANES, LANES), 0)
    ci = lax.broadcasted_iota(jnp.int32, (LANES, LANES), 1)
    tri = jnp.where(ri <= ci, 1.0, 0.0).astype(BF16)
    sub = lax.broadcasted_iota(jnp.int32, (N_EXPERTS, LANES), 0)

    def count_body(b, base):
        rows = [b * PLAN_UNROLL + u for u in range(PLAN_UNROLL)]
        onehots = [sub == e_ref[pl.ds(r, 1), :] for r in rows]
        locs = [jnp.dot(jnp.where(oh, 1.0, 0.0).astype(BF16), tri, preferred_element_type=F32)
                for oh in onehots]
        for r, onehot, loc in zip(rows, onehots, locs):
            rank_ref[pl.ds(r, 1), :] = jnp.sum(jnp.where(onehot, base + loc - 1.0, 0.0),
                                               axis=0, keepdims=True)
            base = base + jnp.broadcast_to(loc[:, LANES - 1:LANES], (N_EXPERTS, LANES))
        return base

    count = lax.fori_loop(0, nrow // PLAN_UNROLL, count_body, jnp.zeros((N_EXPERTS, LANES), F32))
    ntile = jnp.floor((count + (MOE_TILE - 1.0)) * (1.0 / MOE_TILE))
    offs = []
    acc = jnp.zeros((1, LANES), F32)
    for e in range(N_EXPERTS):
        offs.append(acc)
        acc = acc + ntile[e:e + 1, :]
    tile_off = jnp.concatenate(offs, axis=0)
    tile_end = tile_off + ntile
    lane = lax.broadcasted_iota(jnp.int32, (N_EXPERTS, LANES), 1).astype(F32)
    tile_expert = jnp.sum(jnp.where(tile_end <= lane, 1.0, 0.0), axis=0, keepdims=True)
    tile_expert = jnp.minimum(tile_expert, N_EXPERTS - 1.0)
    own = jnp.logical_and(tile_off <= lane, lane < tile_end)
    run_end = jnp.sum(jnp.where(own, tile_end, 0.0), axis=0, keepdims=True)
    next_expert = jnp.sum(jnp.where(tile_end <= run_end, 1.0, 0.0), axis=0, keepdims=True)
    has_next = jnp.logical_and(lane[0:1, :] < acc, run_end < acc)
    next_expert = jnp.where(has_next, next_expert, -1.0)
    meta = jnp.concatenate([tile_expert, acc, next_expert, jnp.zeros((5, LANES), F32)], axis=0)
    meta_ref[...] = meta.astype(jnp.int32)
    row_off = tile_off * float(MOE_TILE)

    def pos_body(r, c):
        onehot = sub == e_ref[pl.ds(r, 1), :]
        p = jnp.sum(jnp.where(onehot, row_off, 0.0), axis=0, keepdims=True) + rank_ref[pl.ds(r, 1), :]
        pos_ref[pl.ds(r, 1), :] = p.astype(jnp.int32)
        return c

    lax.fori_loop(0, nrow, pos_body, 0)


def _plan(e_idx, *, name):
    n2 = e_idx.shape[0] * e_idx.shape[1]
    assert n2 % (LANES * PLAN_UNROLL) == 0
    e2d = e_idx.reshape(n2 // LANES, LANES)
    pos, meta = pl.pallas_call(
        _plan_kernel,
        out_shape=[jax.ShapeDtypeStruct(e2d.shape, jnp.int32),
                   jax.ShapeDtypeStruct((8, LANES), jnp.int32)],
        scratch_shapes=[pltpu.VMEM(e2d.shape, F32)],
        name=name,
    )(e2d)
    return pos.reshape(n2), meta[0], meta[1, :1], meta[2]


def _invert_kernel(pos_ref, pair_ref):
    n_rows = pair_ref.shape[0]
    n_pairs = pos_ref.shape[0]

    def fill_body(b, c):
        for u in range(SCALAR_UNROLL):
            pair_ref[b * SCALAR_UNROLL + u] = -1
        return c

    def pair_body(b, c):
        rows = [pos_ref[b * SCALAR_UNROLL + u] for u in range(SCALAR_UNROLL)]
        for u in range(SCALAR_UNROLL):
            pair_ref[rows[u]] = b * SCALAR_UNROLL + u
        return c

    lax.fori_loop(0, n_rows // SCALAR_UNROLL, fill_body, 0)
    lax.fori_loop(0, n_pairs // SCALAR_UNROLL, pair_body, 0)


def _invert(pos, n_rows, *, name):
    return pl.pallas_call(
        _invert_kernel,
        grid_spec=pltpu.PrefetchScalarGridSpec(
            num_scalar_prefetch=1,
            grid=(1,),
            in_specs=[],
            out_specs=pl.BlockSpec(memory_space=pltpu.SMEM),
        ),
        out_shape=jax.ShapeDtypeStruct((n_rows,), jnp.int32),
        name=name,
    )(pos)


def _expert_changed(te_ref, i):
    return jnp.logical_or(i == 0, te_ref[i] != te_ref[jnp.maximum(i - 1, 0)])


class _ExpertWeights:
    def __init__(self, mats, layer, st, sems):
        self.mats, self.layer, self.st, self.sems = mats, layer, st, sems

    def _copies(self, expert, c):
        out = []
        for w_hbm, stage, _, _ in self.mats:
            rows = stage.shape[1]
            src = w_hbm.at[self.layer, expert, pl.ds(pl.multiple_of(c * rows, rows), rows), :]
            out.append(pltpu.make_async_copy(src, stage.at[c % 2], self.sems.at[c % 2]))
        return out

    def _start(self, expert, c):
        for cp in self._copies(expert, c):
            cp.start()
        self.st[2] = c + 1

    def _convert(self, c_src, c_dst):
        for _, stage, w_next, _ in self.mats:
            rows = stage.shape[1]
            dst = pl.ds(pl.multiple_of(c_dst * rows, rows), rows)
            w_next[dst, :] = stage[c_src % 2].astype(BF16)

    def reset(self):
        for _, stage, _, _ in self.mats:
            stage[...] = jnp.zeros_like(stage)
        self.st[1] = 0
        self.st[2] = 0

    def switch_to(self, expert):
        st = self.st

        def body(c, carry):
            @pl.when(c >= st[2])
            def _():
                self._start(expert, c)

            @pl.when(jnp.logical_and(c + 1 < W_CHUNKS, c + 1 >= st[2]))
            def _():
                self._start(expert, c + 1)

            for cp in self._copies(expert, c):
                cp.wait()
            self._convert(c, c)
            return carry

        lax.fori_loop(st[1], W_CHUNKS, body, 0)
        for _, _, w_next, w_cur in self.mats:
            _copy_rows(w_next, w_cur)
        st[1] = 0
        st[2] = 0

    def begin_step(self, next_expert):
        st = self.st
        done, issued = st[1], st[2]
        has_next = next_expert >= 0
        active = jnp.logical_and(has_next, done < issued)

        @pl.when(jnp.logical_and(has_next, jnp.logical_and(issued < W_CHUNKS, issued < done + 2)))
        def _():
            self._start(next_expert, issued)

        @pl.when(active)
        def _():
            for cp in self._copies(next_expert, done):
                cp.wait()

        return active, done

    def convert_step(self, active, done):
        self._convert(jnp.where(active, done, done + 1), jnp.where(active, done, W_CHUNKS))

    def end_step(self, active, done):
        @pl.when(active)
        def _():
            self.st[1] = done + 1


def _copy_rows(src_ref, dst_ref):
    n = dst_ref.shape[0] // CAST_ROWS

    def body(i, c):
        r = pl.multiple_of(i * CAST_ROWS, CAST_ROWS)
        dst_ref[pl.ds(r, CAST_ROWS), :] = src_ref[pl.ds(r, CAST_ROWS), :]
        return c

    lax.fori_loop(0, n, body, 0)


def _moe_tile_kernel(pair_ref, te_ref, nu_ref, nxt_ref, x_hbm, wg_hbm, wu_hbm, wd_hbm, out_hbm,
                     wg_cur, wg_next, wg_stage, wu_cur, wu_next, wu_stage, wd_cur, wd_next, wd_stage,
                     x_rows, x_bf, ya, yb, gsem, ssems, tsem, wsems, st, *, n_tok, n_tiles, layer):
    i = pl.program_id(0)
    nu = nu_ref[0]
    running = i < nu
    trash = 2 * n_tok
    weights = _ExpertWeights([(wg_hbm, wg_stage, wg_next, wg_cur), (wu_hbm, wu_stage, wu_next, wu_cur),
                              (wd_hbm, wd_stage, wd_next, wd_cur)], layer, st, wsems)

    def gather_copy(tile, r):
        p = pair_ref[tile * MOE_TILE + r]
        tok = jnp.where(p >= n_tok, p - n_tok, jnp.maximum(p, 0))
        src = pl.ds(pl.multiple_of(tok * ROW_SUBLANES, ROW_SUBLANES), ROW_SUBLANES)
        return pltpu.make_async_copy(x_hbm.at[src, :], x_rows.at[pl.ds(r * ROW_SUBLANES, ROW_SUBLANES), :],
                                     gsem)

    def scatter_copy(tile, r, buf, sem):
        p = pair_ref[tile * MOE_TILE + r]
        dst = jnp.where(p < 0, trash + r, p)
        return pltpu.make_async_copy(buf.at[pl.ds(r, 1), :], out_hbm.at[pl.ds(dst, 1), :], sem)

    def wait_scatter(buf, sem):
        pltpu.make_async_copy(buf, out_hbm.at[pl.ds(0, MOE_TILE), :], sem).wait()

    @pl.when(i == 0)
    def _():
        def body(rb, c):
            for u in range(DMA_UNROLL):
                gather_copy(0, rb * DMA_UNROLL + u).start()
            return c
        lax.fori_loop(0, MOE_TILE // DMA_UNROLL, body, 0)
        yb[...] = jnp.zeros_like(yb)
        fill = pltpu.make_async_copy(yb, out_hbm.at[pl.ds(trash, MOE_TILE), :], tsem)
        fill.start()
        fill.wait()
        weights.reset()

    @pl.when(i <= nu)
    def _():
        pltpu.make_async_copy(x_hbm.at[pl.ds(0, MOE_TILE * ROW_SUBLANES), :], x_rows, gsem).wait()

    @pl.when(jnp.logical_and(running, _expert_changed(te_ref, i)))
    def _():
        weights.switch_to(te_ref[i])

    active, done = weights.begin_step(jnp.where(running, nxt_ref[i], -1))

    for parity, (cur, prev) in enumerate(((ya, yb), (yb, ya))):
        cur_sem, prev_sem = ssems.at[parity], ssems.at[1 - parity]
        mine = i % 2 == parity

        @pl.when(jnp.logical_and(mine, jnp.logical_and(i >= 1, i - 1 <= nu)))
        def _():
            wait_scatter(cur, cur_sem)

        @pl.when(jnp.logical_and(mine, running))
        def _():
            _load_packed_rows(x_rows, x_bf)
            next_tile = jnp.minimum(i + 1, n_tiles - 1)
            prev_tile = jnp.maximum(i - 1, 0)
            for r in range(MOE_TILE):
                gather_copy(next_tile, r).start(priority=r % 2)
            for r in range(MOE_TILE):
                scatter_copy(prev_tile, r, prev, prev_sem).start(priority=r % 2)
            weights.convert_step(active, done)
            x = x_bf[...]
            a = jnp.dot(x, wg_cur[...], preferred_element_type=F32)
            b = jnp.dot(x, wu_cur[...], preferred_element_type=F32)
            h = (jax.nn.silu(a) * b).astype(BF16)
            cur[...] = jnp.dot(h, wd_cur[...], preferred_element_type=F32)

        @pl.when(jnp.logical_and(mine, i == nu))
        def _():
            def body(rb, c):
                for u in range(DMA_UNROLL):
                    scatter_copy(i - 1, rb * DMA_UNROLL + u, prev, prev_sem).start()
                return c
            lax.fori_loop(0, MOE_TILE // DMA_UNROLL, body, 0)

            @pl.when(i == n_tiles)
            def _():
                wait_scatter(prev, prev_sem)

    weights.end_step(active, done)


def _moe_tiles(x, pair, tile_expert, n_used, next_expert, w_gate, w_up, w_down, layer):
    n_tok = x.shape[0] // ROW_SUBLANES
    n_tiles = pair.shape[0] // MOE_TILE
    hbm = pl.BlockSpec(memory_space=pl.ANY)

    def weight_bufs(k, n):
        chunk = k // W_CHUNKS
        return [pltpu.VMEM((k, n), BF16), pltpu.VMEM((k + chunk, n), BF16), pltpu.VMEM((2, chunk, n), F32)]

    return pl.pallas_call(
        functools.partial(_moe_tile_kernel, n_tok=n_tok, n_tiles=n_tiles, layer=layer),
        grid_spec=pltpu.PrefetchScalarGridSpec(
            num_scalar_prefetch=4,
            grid=(n_tiles + 1,),
            in_specs=[hbm, hbm, hbm, hbm],
            out_specs=hbm,
            scratch_shapes=weight_bufs(D_MODEL, D_EXPERT) + weight_bufs(D_MODEL, D_EXPERT)
            + weight_bufs(D_EXPERT, D_MODEL) + [
                pltpu.VMEM((MOE_TILE * ROW_SUBLANES, LANES), jnp.uint32),
                pltpu.VMEM((MOE_TILE, D_MODEL), BF16),
                pltpu.VMEM((MOE_TILE, D_MODEL), F32), pltpu.VMEM((MOE_TILE, D_MODEL), F32),
                pltpu.SemaphoreType.DMA(()), pltpu.SemaphoreType.DMA((2,)),
                pltpu.SemaphoreType.DMA(()), pltpu.SemaphoreType.DMA((2,)),
                pltpu.SMEM((4,), jnp.int32)],
        ),
        out_shape=jax.ShapeDtypeStruct((2 * n_tok + MOE_TILE, D_MODEL), F32),
        compiler_params=_params(("arbitrary",), 56),
        name=f"moe_tiles_{layer}",
    )(pair, tile_expert, n_used, next_expert, x, w_gate, w_up, w_down)


def _combine(y0_ref, y1_ref, res_ref, gate_ref, g_ref, beta_ref):
    gate = gate_ref[...]
    ffn = gate[:, 0:1] * y0_ref[...] + gate[:, 1:2] * y1_ref[...]
    return _layer_norm(ALPHA * res_ref[...] + ffn, g_ref[...], beta_ref[...])


def _combine_specs(n):
    nt = n // TOK_TILE
    row = lambda i: (i, 0)
    const = lambda i: (0, 0)
    return [pl.BlockSpec((TOK_TILE, D_MODEL), row),
            pl.BlockSpec((TOK_TILE, D_MODEL), lambda i: (i + nt, 0)),
            pl.BlockSpec((TOK_TILE, D_MODEL), row),
            pl.BlockSpec((TOK_TILE, 2), row),
            pl.BlockSpec((1, D_MODEL), const),
            pl.BlockSpec((1, D_MODEL), const)]


def _combine_split_kernel(y0_ref, y1_ref, res_ref, gate_ref, g_ref, beta_ref, prompt_ref, sample_ref):
    x = _combine(y0_ref, y1_ref, res_ref, gate_ref, g_ref, beta_ref)
    is_sample = pl.program_id(0) == pl.num_programs(0) - 1

    @pl.when(jnp.logical_not(is_sample))
    def _():
        prompt_ref[...] = x

    @pl.when(is_sample)
    def _():
        sample_ref[...] = x


def _combine_split(ys, res, gates_col, g, beta, *, name):
    n = res.shape[0]
    tm = TOK_TILE
    nt = n // tm
    return pl.pallas_call(
        _combine_split_kernel,
        grid=(nt,),
        in_specs=_combine_specs(n),
        out_specs=[pl.BlockSpec((tm, D_MODEL), lambda i: (jnp.minimum(i, nt - 2), 0)),
                   pl.BlockSpec((tm, D_MODEL), lambda i: (0, 0))],
        out_shape=[jax.ShapeDtypeStruct((n - tm, D_MODEL), F32),
                   jax.ShapeDtypeStruct((tm, D_MODEL), F32)],
        compiler_params=_params(("arbitrary",), 40),
        name=name,
    )(ys, ys, res, gates_col, g, beta)


def _load_weight(w_hbm, wbf_ref, stage_ref, sems):
    rows = stage_ref.shape[1]
    n_chunks = wbf_ref.shape[0] // rows

    def chunk_copy(c):
        return pltpu.make_async_copy(w_hbm.at[pl.ds(c * rows, rows), :], stage_ref.at[c % 2],
                                     sems.at[c % 2])

    chunk_copy(0).start()
    for c in range(n_chunks):
        if c + 1 < n_chunks:
            chunk_copy(c + 1).start()
        chunk_copy(c).wait()
        wbf_ref[c * rows:(c + 1) * rows, :] = stage_ref[c % 2].astype(BF16)


def _combine_qkv_kernel(y0_ref, y1_ref, res_ref, gate_ref, g_ref, beta_ref, wq_hbm, wkv_hbm,
                        x_ref, q_ref, kv_ref, wq_bf, wkv_bf, stage_q, stage_kv, sems, *, wq_index):
    @pl.when(pl.program_id(0) == 0)
    def _():
        wq = wq_hbm
        for k in wq_index:
            wq = wq.at[k]
        _load_weight(wq, wq_bf, stage_q, sems)
        _load_weight(wkv_hbm, wkv_bf, stage_kv, sems)

    x = _combine(y0_ref, y1_ref, res_ref, gate_ref, g_ref, beta_ref)
    x_ref[...] = x
    x_bf = x.astype(BF16)
    q = jnp.dot(x_bf, wq_bf[...], preferred_element_type=F32) * (HEAD_DIM ** -0.5)
    q_ref[...] = q.astype(BF16)
    kv_ref[...] = jnp.dot(x_bf, wkv_bf[...], preferred_element_type=F32)


def _combine_qkv(ys, res, gates_col, g, beta, w_q, wq_index, w_kv, *, name):
    n = res.shape[0]
    tm = TOK_TILE
    row = lambda i: (i, 0)
    hbm = pl.BlockSpec(memory_space=pl.ANY)
    return pl.pallas_call(
        functools.partial(_combine_qkv_kernel, wq_index=wq_index),
        grid=(n // tm,),
        in_specs=_combine_specs(n) + [hbm, hbm],
        out_specs=[pl.BlockSpec((tm, D_MODEL), row),
                   pl.BlockSpec((tm, D_MODEL), row),
                   pl.BlockSpec((tm, 2 * KV_DIM), row)],
        out_shape=[jax.ShapeDtypeStruct((n, D_MODEL), F32),
                   jax.ShapeDtypeStruct((n, D_MODEL), BF16),
                   jax.ShapeDtypeStruct((n, 2 * KV_DIM), F32)],
        scratch_shapes=[pltpu.VMEM((D_MODEL, D_MODEL), BF16),
                        pltpu.VMEM((D_MODEL, 2 * KV_DIM), BF16),
                        pltpu.VMEM((2, CAST_ROWS, D_MODEL), F32),
                        pltpu.VMEM((2, CAST_ROWS, 2 * KV_DIM), F32),
                        pltpu.SemaphoreType.DMA((2,))],
        compiler_params=_params(("arbitrary",), 52),
        name=name,
    )(ys, ys, res, gates_col, g, beta, w_q, w_kv)


def _moe_block(x_rows, e_idx, w_gate, w_up, w_down, layer):
    n = x_rows.shape[0] // ROW_SUBLANES
    n_tiles = -(-(2 * n + N_EXPERTS * (MOE_TILE - 1)) // MOE_TILE)
    pos, tile_expert, n_used, next_expert = _plan(e_idx, name=f"moe_plan_{layer}")
    pair = _invert(pos, n_tiles * MOE_TILE, name=f"moe_invert_{layer}")
    return _moe_tiles(x_rows, pair, tile_expert, n_used, next_expert, w_gate, w_up, w_down, layer)


def _sigmoid_of_half(half_x):
    return 0.5 * jnp.tanh(half_x) + 0.5


def _log_sigmoid(x):
    return -(jnp.maximum(-x, 0.0) + jnp.log1p(jnp.exp(-jnp.abs(x))))


def _lru_gate_blocks(xcs, blocks, wrg_bf, wig_bf, brg_ref, big_ref, lam_ref):
    xbs = [xc.astype(BF16) for xc in xcs]
    r_lin = [jnp.dot(xb, wrg_bf[n], preferred_element_type=F32) for xb, n in zip(xbs, blocks)]
    i_lin = [jnp.dot(xb, wig_bf[n], preferred_element_type=F32) for xb, n in zip(xbs, blocks)]
    out = []
    for xc, n, rl, il in zip(xcs, blocks, r_lin, i_lin):
        cols = slice(n * LRU_BLOCK, (n + 1) * LRU_BLOCK)
        r = _sigmoid_of_half(rl + 0.5 * brg_ref[:, cols])
        i = _sigmoid_of_half(il + 0.5 * big_ref[:, cols])
        log_a = r * (LRU_C * _log_sigmoid(lam_ref[:, cols]))
        a = jnp.exp(log_a)
        u = xc * i * jnp.sqrt(-jnp.tanh(log_a) * (a * a + 1.0))
        out.append((a, u))
    return out


def _lru_gate_block(xc, n, wrg_bf, wig_bf, brg_ref, big_ref, lam_ref):
    return _lru_gate_blocks([xc], [n], wrg_bf, wig_bf, brg_ref, big_ref, lam_ref)[0]


def _cast_gate_weights(wrg_ref, wig_ref, wrg_bf, wig_bf):
    for n in range(LRU_BLOCKS):
        wrg_bf[n] = (0.5 * wrg_ref[n]).astype(BF16)
        wig_bf[n] = (0.5 * wig_ref[n]).astype(BF16)


def _lru_prompt_kernel(xb_ref, yb_ref, cw_ref, cb_ref, wrg_ref, wig_ref, brg_ref, big_ref, lam_ref,
                       m_ref, conv_ref, hlast_ref, xs, tail, a_s, u_s, hs_t, h_s, wrg_bf, wig_bf):
    b = pl.program_id(0)
    j = pl.program_id(1)
    tt = m_ref.shape[0]
    seg_len = tt // SEGS
    taps = CONV_WIDTH - 1
    head = SEGS * taps

    @pl.when(jnp.logical_and(b == 0, j == 0))
    def _():
        _cast_gate_weights(wrg_ref, wig_ref, wrg_bf, wig_bf)

    @pl.when(j == 0)
    def _():
        tail[...] = jnp.zeros_like(tail)
        h_s[...] = jnp.zeros_like(h_s)

    for q in range(seg_len):
        xs[head + SEGS * q:head + SEGS * (q + 1), :] = jnp.concatenate(
            [xb_ref[pl.ds(_chunk_row(q, c), SEGS, stride=CHUNK_SEG_ROWS), :] for c in range(CHUNKS)],
            axis=1)
    sub = lax.broadcasted_iota(jnp.int32, (SEGS, D_MODEL), 0)
    for k in range(taps):
        last = head + SEGS * (seg_len - taps + k)
        joined = jnp.where(sub == SEGS - 1, tail[SEGS * k:SEGS * (k + 1), :], xs[last:last + SEGS, :])
        xs[SEGS * k:SEGS * (k + 1), :] = pltpu.roll(joined, 1, axis=0)
    tail[...] = xs[head + SEGS * (seg_len - taps):head + SEGS * seg_len, :]

    for first_block in range(0, LRU_BLOCKS, GATE_BLOCKS):
        blocks = range(first_block, first_block + GATE_BLOCKS)
        xcs = []
        for n in blocks:
            cols = slice(n * LRU_BLOCK, (n + 1) * LRU_BLOCK)
            xc = cb_ref[:, cols] + cw_ref[0:1, cols] * xs[0:tt, cols]
            for k in range(1, CONV_WIDTH):
                xc = xc + cw_ref[k:k + 1, cols] * xs[SEGS * k:SEGS * k + tt, cols]
            xcs.append(xc)
        gates = _lru_gate_blocks(xcs, blocks, wrg_bf, wig_bf, brg_ref, big_ref, lam_ref)
        for n, (a, u) in zip(blocks, gates):
            cols = slice(n * LRU_BLOCK, (n + 1) * LRU_BLOCK)
            a_s[:, cols] = a
            u_s[:, cols] = u

    def scan_body(q, carry):
        h, prod = carry
        rows = pl.ds(pl.multiple_of(q * SEGS, SEGS), SEGS)
        a = a_s[rows, :]
        h = a * h + u_s[rows, :]
        prod = a * prod
        u_s[rows, :] = h
        a_s[rows, :] = prod
        return h, prod

    h_end, prod_end = lax.fori_loop(
        0, seg_len, scan_body,
        (jnp.zeros((SEGS, D_MODEL), F32), jnp.ones((SEGS, D_MODEL), F32)))
    state = h_s[...]
    entering = []
    for s in range(SEGS):
        entering.append(state)
        state = h_end[s:s + 1, :] + prod_end[s:s + 1, :] * state
    h_s[...] = state
    enter = jnp.concatenate(entering, axis=0)

    def fix_body(q, carry):
        rows = pl.ds(pl.multiple_of(q * SEGS, SEGS), SEGS)
        h = u_s[rows, :] + a_s[rows, :] * enter
        for c in range(CHUNKS):
            hs_t[pl.ds(q * CHUNK_TOK_ROWS + c, SEGS, stride=CHUNK_SEG_ROWS), :] = h[:, c * LANES:(c + 1) * LANES]
        return carry

    lax.fori_loop(0, seg_len, fix_body, 0)
    for s in range(SEGS):
        rows = slice(s * seg_len, (s + 1) * seg_len)
        hs = _chunk_rows_load(hs_t, s * seg_len, seg_len)
        m_ref[rows, :] = (hs * yb_ref[rows, :].astype(F32)).astype(BF16)

    @pl.when(j == pl.num_programs(1) - 1)
    def _():
        for k in range(taps):
            conv_ref[k:k + 1, :] = tail[SEGS * k + SEGS - 1:SEGS * (k + 1), :]
        hlast_ref[...] = state


def _lru_prompt(xb, yb, batch, seq, cw, cb, wrg, wig, brg, big, lam):
    tt = TOK_TILE
    nj = seq // tt
    row = lambda b, j: (b * nj + j, 0)
    const2 = lambda b, j: (0, 0)
    const3 = lambda b, j: (0, 0, 0)
    return pl.pallas_call(
        _lru_prompt_kernel,
        grid=(batch, nj),
        in_specs=[
            pl.BlockSpec((CHUNK_TILE_ROWS, LANES), row),
            pl.BlockSpec((tt, D_MODEL), row),
            pl.BlockSpec((CONV_WIDTH, D_MODEL), const2),
            pl.BlockSpec((1, D_MODEL), const2),
            pl.BlockSpec((LRU_BLOCKS, LRU_BLOCK, LRU_BLOCK), const3),
            pl.BlockSpec((LRU_BLOCKS, LRU_BLOCK, LRU_BLOCK), const3),
            pl.BlockSpec((1, D_MODEL), const2),
            pl.BlockSpec((1, D_MODEL), const2),
            pl.BlockSpec((1, D_MODEL), const2),
        ],
        out_specs=[
            pl.BlockSpec((tt, D_MODEL), row),
            pl.BlockSpec((None, CONV_WIDTH - 1, D_MODEL), lambda b, j: (b, 0, 0)),
            pl.BlockSpec((None, 1, D_MODEL), lambda b, j: (b, 0, 0)),
        ],
        out_shape=[
            jax.ShapeDtypeStruct((batch * seq, D_MODEL), BF16),
            jax.ShapeDtypeStruct((batch, CONV_WIDTH - 1, D_MODEL), F32),
            jax.ShapeDtypeStruct((batch, 1, D_MODEL), F32),
        ],
        scratch_shapes=[
            pltpu.VMEM((tt + SEGS * (CONV_WIDTH - 1), D_MODEL), F32),
            pltpu.VMEM((SEGS * (CONV_WIDTH - 1), D_MODEL), F32),
            pltpu.VMEM((tt, D_MODEL), F32),
            pltpu.VMEM((tt, D_MODEL), F32),
            pltpu.VMEM((CHUNK_TILE_ROWS, LANES), F32),
            pltpu.VMEM((1, D_MODEL), F32),
            pltpu.VMEM((LRU_BLOCKS, LRU_BLOCK, LRU_BLOCK), BF16),
            pltpu.VMEM((LRU_BLOCKS, LRU_BLOCK, LRU_BLOCK), BF16),
        ],
        compiler_params=_params(("arbitrary", "arbitrary"), 40),
        name="lru_prompt",
    )(xb, yb, cw, cb, wrg, wig, brg, big, lam)


def _lru_sample_kernel(xb_ref, yb_ref, cs_ref, h0_ref, cw_ref, cb_ref, wrg_ref, wig_ref,
                       brg_ref, big_ref, lam_ref, m_ref, conv_ref, hlast_ref, wrg_bf, wig_bf, *, steps):
    batch = h0_ref.shape[0]
    _cast_gate_weights(wrg_ref, wig_ref, wrg_bf, wig_bf)
    m_ref[steps * batch:, :] = jnp.zeros((m_ref.shape[0] - steps * batch, D_MODEL), BF16)

    def slab(t, cols):
        if t < CONV_WIDTH - 1:
            return cs_ref[t, :, cols]
        t -= CONV_WIDTH - 1
        first, stop, _ = cols.indices(D_MODEL)
        return _chunk_rows_load(xb_ref, t * batch, batch, range(first // LANES, stop // LANES))

    for n in range(LRU_BLOCKS):
        cols = slice(n * LRU_BLOCK, (n + 1) * LRU_BLOCK)
        h = h0_ref[:, cols]
        for t in range(steps):
            xc = cb_ref[:, cols] + cw_ref[0:1, cols] * slab(t, cols)
            for k in range(1, CONV_WIDTH):
                xc = xc + cw_ref[k:k + 1, cols] * slab(t + k, cols)
            a, u = _lru_gate_block(xc, n, wrg_bf, wig_bf, brg_ref, big_ref, lam_ref)
            h = a * h + u
            rows = slice(t * batch, (t + 1) * batch)
            m_ref[rows, cols] = (h * yb_ref[rows, cols].astype(F32)).astype(BF16)
        hlast_ref[:, cols] = h
    for k in range(CONV_WIDTH - 1):
        conv_ref[k] = slab(steps + k, slice(None))


def _lru_sample(xb, yb, tile, steps, conv_state, h0, cw, cb, wrg, wig, brg, big, lam):
    batch = h0.shape[0]
    tok = pl.BlockSpec((TOK_TILE, D_MODEL), lambda i: (tile, 0))
    tok_chunks = pl.BlockSpec((CHUNK_TILE_ROWS, LANES), lambda i: (tile, 0))
    full = lambda a: pl.BlockSpec(a.shape, lambda i: (0,) * a.ndim)
    small = (conv_state, h0, cw, cb, wrg, wig, brg, big, lam)
    return pl.pallas_call(
        functools.partial(_lru_sample_kernel, steps=steps),
        grid=(1,),
        in_specs=[tok_chunks, tok] + [full(a) for a in small],
        out_specs=[
            pl.BlockSpec((TOK_TILE, D_MODEL), lambda i: (0, 0)),
            pl.BlockSpec((CONV_WIDTH - 1, batch, D_MODEL), lambda i: (0, 0, 0)),
            pl.BlockSpec((batch, D_MODEL), lambda i: (0, 0)),
        ],
        out_shape=[
            jax.ShapeDtypeStruct((TOK_TILE, D_MODEL), BF16),
            jax.ShapeDtypeStruct((CONV_WIDTH - 1, batch, D_MODEL), F32),
            jax.ShapeDtypeStruct((batch, D_MODEL), F32),
        ],
        scratch_shapes=[
            pltpu.VMEM((LRU_BLOCKS, LRU_BLOCK, LRU_BLOCK), BF16),
            pltpu.VMEM((LRU_BLOCKS, LRU_BLOCK, LRU_BLOCK), BF16),
        ],
        compiler_params=_params(("arbitrary",), 32),
        name="lru_sample",
    )(xb, yb, *small)


def _rel_bucket(dist):
    n = jnp.maximum(dist, 0)
    max_exact = N_BUCKETS // 2
    nf = jnp.maximum(n, 1).astype(F32)
    large = max_exact + (jnp.log(nf / max_exact) / math.log(MAX_DISTANCE / max_exact)
                         * (N_BUCKETS - max_exact)).astype(jnp.int32)
    large = jnp.minimum(large, N_BUCKETS - 1)
    return jnp.where(n < max_exact, n, large)


def _masked_buckets(dist):
    valid = (dist >= 0) & (dist < WINDOW)
    return jnp.where(valid, _rel_bucket(dist), -1).astype(jnp.int32)


def _build_bias(bucket, tab_ref, head):
    def body(bi, acc):
        return jnp.where(bucket == bi, tab_ref[bi * N_HEADS + head], acc)
    return lax.fori_loop(0, N_BUCKETS, body, jnp.full(bucket.shape, NEG_INF, F32))


def _softmax_pv(s, sink, v):
    m = jnp.maximum(jnp.max(s, axis=-1, keepdims=True), sink)
    p = jnp.exp(s - m)
    den = jnp.sum(p, axis=-1, keepdims=True) + jnp.exp(sink - m)
    return jnp.dot(p.astype(BF16), v, preferred_element_type=F32) / den


def _attn_prompt_kernel(q_ref, kvp_ref, kvc_ref, bucket_ref, tab_ref, sink_ref, o_ref, bias_s):
    b = pl.program_id(0)
    n = pl.program_id(1)

    @pl.when(jnp.logical_and(b == 0, n == 0))
    def _():
        bucket = bucket_ref[...]

        col = lax.broadcasted_iota(jnp.int32, (WINDOW, 2 * WINDOW), 1)

        def head_body(h, c):
            by_offset = jnp.broadcast_to(_build_bias(bucket, tab_ref, h), (WINDOW, 2 * WINDOW))
            bias = pltpu.roll(by_offset, 0, axis=1, stride=1, stride_axis=0)
            sink = sink_ref[h]
            g = h // GROUP
            r0 = pl.multiple_of((h % GROUP) * WINDOW, WINDOW)
            bias_s[0, g, pl.ds(r0, WINDOW), :] = jnp.where(col == 0, sink, bias)
            bias_s[1, g, pl.ds(r0, WINDOW), :] = jnp.where(
                col == 0, sink, jnp.where(col < WINDOW, NEG_INF, bias))
            return c

        lax.fori_loop(0, N_HEADS, head_body, 0)

    first = (n == 0).astype(jnp.int32)
    row = lax.broadcasted_iota(jnp.int32, kvp_ref.shape, 0)
    kv_prev = jnp.where(row == 0, 0.0, kvp_ref[...])
    kv = jnp.concatenate([kv_prev, kvc_ref[...]], axis=0).astype(BF16)
    ones = jnp.ones((2 * WINDOW, 2 * HEAD_DIM), BF16)
    lane = lax.broadcasted_iota(jnp.int32, (WINDOW, 2 * HEAD_DIM), 1)
    def scores(idx):
        g, pair = divmod(idx, GROUP // 2)
        h0 = g * GROUP + 2 * pair
        kg = kv[:, g * HEAD_DIM:(g + 1) * HEAD_DIM]
        qp = jnp.concatenate([q_ref[:, h * HEAD_DIM:(h + 1) * HEAD_DIM] for h in (h0, h0 + 1)], axis=0)
        s = lax.dot_general(qp, kg, (((1,), (1,)), ((), ())), preferred_element_type=F32)
        return s + bias_s[first, g, 2 * pair * WINDOW:(2 * pair + 2) * WINDOW, :]

    def finish(idx, o_ext):
        h0 = 2 * idx
        o = o_ext[:, :2 * HEAD_DIM] * (1.0 / o_ext[:, 2 * HEAD_DIM:])
        o_ref[:, h0 * HEAD_DIM:(h0 + 2) * HEAD_DIM] = jnp.where(
            lane < HEAD_DIM, o[:WINDOW], o[WINDOW:]).astype(BF16)

    n_pairs = N_HEADS // 2

    def values(idx, p):
        g = idx // (GROUP // 2)
        vg = kv[:, KV_DIM + g * HEAD_DIM:KV_DIM + (g + 1) * HEAD_DIM]
        v_ext = jnp.concatenate([vg, vg, ones], axis=1)
        return jnp.dot(p, v_ext, preferred_element_type=F32)

    for first_pair in range(0, n_pairs, PAIR_BLOCK):
        block = range(first_pair, first_pair + PAIR_BLOCK)
        ss = [scores(idx) for idx in block]
        ms = [jnp.max(s, axis=-1, keepdims=True) for s in ss]
        ps = [jnp.exp(s - m).astype(BF16) for s, m in zip(ss, ms)]
        os_ = [values(idx, p) for idx, p in zip(block, ps)]
        for idx, o_ext in zip(block, os_):
            finish(idx, o_ext)


def _attn_prompt(q, kv, batch, seq, bucket, tab, sinks):
    nb = seq // WINDOW
    smem = pl.BlockSpec(memory_space=pltpu.SMEM)
    return pl.pallas_call(
        _attn_prompt_kernel,
        grid=(batch, nb),
        in_specs=[
            pl.BlockSpec((WINDOW, D_MODEL), lambda b, n: (b * nb + n, 0)),
            pl.BlockSpec((WINDOW, 2 * KV_DIM), lambda b, n: (jnp.maximum(b * nb + n - 1, 0), 0)),
            pl.BlockSpec((WINDOW, 2 * KV_DIM), lambda b, n: (b * nb + n, 0)),
            pl.BlockSpec((1, 2 * WINDOW), lambda b, n: (0, 0)),
            smem, smem,
        ],
        out_specs=pl.BlockSpec((WINDOW, D_MODEL), lambda b, n: (b * nb + n, 0)),
        out_shape=jax.ShapeDtypeStruct((batch * seq, D_MODEL), BF16),
        scratch_shapes=[pltpu.VMEM((2, N_KV_HEADS, GROUP * WINDOW, 2 * WINDOW), F32)],
        compiler_params=_params(("arbitrary", "arbitrary"), 32),
        name="attn_prompt",
    )(q, kv, kv, bucket, tab, sinks)


def _attn_sample_kernel(q_ref, ck_ref, cv_ref, kn_ref, vn_ref, bucket_ref, tab_ref, sink_ref,
                        o_ref, kwin_ref, vwin_ref, bias_s):
    steps = q_ref.shape[0]
    k_all = jnp.concatenate([ck_ref[...], kn_ref[...]], axis=0)
    v_all = jnp.concatenate([cv_ref[...], vn_ref[...]], axis=0)
    kwin_ref[...] = k_all[steps:, :]
    vwin_ref[...] = v_all[steps:, :]

    @pl.when(pl.program_id(0) == 0)
    def _():
        bucket = bucket_ref[...]

        def head_body(h, c):
            bias_s[h] = _build_bias(bucket, tab_ref, h)
            return c

        lax.fori_loop(0, N_HEADS, head_body, 0)

    rows = lax.broadcasted_iota(jnp.int32, (GROUP * steps, 1), 0)
    k = k_all.astype(BF16)
    v = v_all.astype(BF16)
    groups = range(N_KV_HEADS)

    def heads_of(g):
        return range(g * GROUP, (g + 1) * GROUP)

    scores, sinks = [], []
    for g in groups:
        qg = jnp.concatenate([q_ref[:, h * HEAD_DIM:(h + 1) * HEAD_DIM] for h in heads_of(g)], axis=0)
        bias = jnp.concatenate([bias_s[h] for h in heads_of(g)], axis=0)
        sink = jnp.full((GROUP * steps, 1), sink_ref[g * GROUP], F32)
        for hh in range(1, GROUP):
            sink = jnp.where(rows >= hh * steps, sink_ref[g * GROUP + hh], sink)
        kg = k[:, g * HEAD_DIM:(g + 1) * HEAD_DIM]
        scores.append(lax.dot_general(qg, kg, (((1,), (1,)), ((), ())),
                                      preferred_element_type=F32) + bias)
        sinks.append(sink)
    outs = [_softmax_pv(scores[g], sinks[g], v[:, g * HEAD_DIM:(g + 1) * HEAD_DIM]) for g in groups]
    for g in groups:
        for hh, h in enumerate(heads_of(g)):
            o_ref[:, h * HEAD_DIM:(h + 1) * HEAD_DIM] = outs[g][hh * steps:(hh + 1) * steps].astype(BF16)


def _attn_sample(q, cache_k, cache_v, k_new, v_new, bucket, tab, sinks):
    batch, steps, _ = q.shape
    lk = WINDOW + steps
    smem = pl.BlockSpec(memory_space=pltpu.SMEM)
    per_seq = lambda rows, cols: pl.BlockSpec((None, rows, cols), lambda b: (b, 0, 0))
    return pl.pallas_call(
        _attn_sample_kernel,
        grid=(batch,),
        in_specs=[
            per_seq(steps, D_MODEL),
            per_seq(WINDOW, KV_DIM), per_seq(WINDOW, KV_DIM),
            per_seq(steps, KV_DIM), per_seq(steps, KV_DIM),
            pl.BlockSpec((steps, lk), lambda b: (0, 0)),
            smem, smem,
        ],
        out_specs=[per_seq(steps, D_MODEL), per_seq(WINDOW, KV_DIM), per_seq(WINDOW, KV_DIM)],
        out_shape=[jax.ShapeDtypeStruct((batch, steps, D_MODEL), BF16),
                   jax.ShapeDtypeStruct((batch, WINDOW, KV_DIM), F32),
                   jax.ShapeDtypeStruct((batch, WINDOW, KV_DIM), F32)],
        scratch_shapes=[pltpu.VMEM((N_HEADS, steps, lk), F32)],
        compiler_params=_params(("arbitrary",), 32),
        name="attn_sample",
    )(q, cache_k, cache_v, k_new, v_new, bucket, tab, sinks)


def kernel(x_prompt, x_sample, state_conv, state_rnn, cache_k_win, cache_v_win, ln_g, ln_b, lru_w_x, lru_b_x, lru_w_y, lru_b_y, lru_conv_w, lru_conv_b, lru_w_rg, lru_b_rg, lru_w_ig, lru_b_ig, lru_lam, lru_w_out, lru_b_out, attn_w_kv, attn_w_q, attn_w_o, attn_sinks, rel_bias, moe_w_router, moe_b_router, moe_w_gate, moe_w_up, moe_w_down):
    bp, seq, _ = x_prompt.shape
    bs, steps, _ = x_sample.shape
    n_p = bp * seq
    n_s = bs * steps

    assert n_p % TOK_TILE == 0 and n_s <= TOK_TILE
    sample_tile = n_p // TOK_TILE

    def pad_tile(rows):
        return jnp.pad(rows, ((0, TOK_TILE - n_s), (0, 0)))

    x0 = (x_prompt.reshape(n_p, D_MODEL),
          pad_tile(x_sample.transpose(1, 0, 2).reshape(n_s, D_MODEL)))
    wr_t = moe_w_router.T
    br = moe_b_router.reshape(N_EXPERTS, 1)
    vec = lambda a: a.reshape(1, -1)

    xb, yb = _lru_in(x0, lru_w_x, vec(lru_b_x[0]), lru_w_y, vec(lru_b_y[0]), 0)
    lru_args = (lru_conv_w[0], vec(lru_conv_b[0]), lru_w_rg[0], lru_w_ig[0],
                vec(lru_b_rg[0]), vec(lru_b_ig[0]), vec(lru_lam[0]))
    m_p, conv_p, rnn_p = _lru_prompt(xb, yb, bp, seq, *lru_args)
    m_s, conv_s, rnn_s = _lru_sample(xb, yb, sample_tile, steps,
                                     state_conv[0].transpose(1, 0, 2), state_rnn[0], *lru_args)
    x1, x1_rows, e_idx, gates = _proj_ln((m_p, m_s), lru_w_out, (0,), vec(lru_b_out[0]), x0,
                                vec(ln_g[0, 0]), vec(ln_b[0, 0]), wr_t, br, name="lru_out_ln")
    ys = _moe_block(x1_rows, e_idx, moe_w_gate, moe_w_up, moe_w_down, 0)

    x2, q, kv = _combine_qkv(ys, x1, gates.T, vec(ln_g[0, 1]), vec(ln_b[0, 1]),
                             attn_w_q, (0,), attn_w_kv, name="moe_combine_qkv")
    tab = rel_bias.reshape(-1)
    sinks = attn_sinks[0]
    offsets = jnp.arange(2 * WINDOW)[None, :]
    o_p = _attn_prompt(q, kv, bp, seq, _masked_buckets(WINDOW - offsets), tab, sinks)
    kv_s = kv[n_p:n_p + n_s].reshape(steps, bs, 2, KV_DIM).transpose(2, 1, 0, 3)
    dist_s = jnp.arange(steps)[:, None] + WINDOW - jnp.arange(WINDOW + steps)[None, :]
    q_s = q[n_p:n_p + n_s].reshape(steps, bs, D_MODEL).transpose(1, 0, 2)
    o_s, k_win_s, v_win_s = _attn_sample(
        q_s, cache_k_win.reshape(bs, WINDOW, KV_DIM), cache_v_win.reshape(bs, WINDOW, KV_DIM),
        kv_s[0], kv_s[1], _masked_buckets(dist_s), tab, sinks)
    o_s = pad_tile(o_s.transpose(1, 0, 2).reshape(n_s, D_MODEL))
    x3, x3_rows, e_idx, gates = _proj_ln((o_p, o_s), attn_w_o, (0,), jnp.zeros((1, D_MODEL), F32), x2,
                                vec(ln_g[1, 0]), vec(ln_b[1, 0]), wr_t, br, name="attn_out_ln")
    ys = _moe_block(x3_rows, e_idx, moe_w_gate, moe_w_up, moe_w_down, 1)
    y_p, y_s = _combine_split(ys, x3, gates.T, vec(ln_g[1, 1]), vec(ln_b[1, 1]), name="moe_combine_1")

    y_prompt = y_p.reshape(bp, seq, D_MODEL)
    y_sample = y_s[:n_s].reshape(steps, bs, D_MODEL).transpose(1, 0, 2)
    kv_p = jnp.stack([kv[(b + 1) * seq - WINDOW:(b + 1) * seq] for b in range(bp)])
    kv_p = kv_p.reshape(bp, WINDOW, 2, N_KV_HEADS, HEAD_DIM)
    k_win_s = k_win_s.reshape(bs, WINDOW, N_KV_HEADS, HEAD_DIM)
    v_win_s = v_win_s.reshape(bs, WINDOW, N_KV_HEADS, HEAD_DIM)
    return (y_prompt, y_sample,
            conv_p[None], rnn_p.reshape(1, bp, D_MODEL),
            kv_p[:, :, 0], kv_p[:, :, 1],
            conv_s.transpose(1, 0, 2)[None], rnn_s[None],
            k_win_s, v_win_s)
```

```python
import functools
import math

import jax
import jax.numpy as jnp
from jax import lax
from jax.experimental import pallas as pl
from jax.experimental.pallas import tpu as pltpu

D_MODEL = 2048
DEPTH = 2
LRU_BLOCKS = 8
LRU_BLOCK = D_MODEL // LRU_BLOCKS
CONV_WIDTH = 4
LRU_C = 8.0
N_HEADS = 32
HEAD_DIM = 64
N_KV_HEADS = 8
GROUP = N_HEADS // N_KV_HEADS
KV_DIM = N_KV_HEADS * HEAD_DIM
WINDOW = 128
N_BUCKETS = 32
MAX_DISTANCE = 128
N_EXPERTS = 16
N_GROUPS = 4
EXPERTS_PER_GROUP = N_EXPERTS // N_GROUPS
D_EXPERT = 1024
ALPHA = (2 * DEPTH) ** 0.25
LN_EPS = 1e-5

LANES = 128
SEGS = 8
CHUNKS = D_MODEL // LANES
ROW_SUBLANES = D_MODEL // (2 * LANES)
MOE_TILE = 256
TOK_TILE = 256
CHUNK_TOK_ROWS = CHUNKS + 4
CHUNK_SEG_TOKS = TOK_TILE // SEGS
CHUNK_SEG_ROWS = CHUNK_SEG_TOKS * CHUNK_TOK_ROWS + 4
CHUNK_TILE_ROWS = SEGS * CHUNK_SEG_ROWS
DMA_UNROLL = 8
SCALAR_UNROLL = 32
PAIR_UNROLL = 16
PLAN_UNROLL = 4
PAIR_BLOCK = 4
GATE_BLOCKS = 4
W_CHUNKS = 4
CAST_ROWS = 256
BF16 = jnp.bfloat16
F32 = jnp.float32
NEG_INF = float("-inf")


def _params(sem, vmem_mb):
    return pltpu.CompilerParams(dimension_semantics=sem, vmem_limit_bytes=vmem_mb * 1024 * 1024)


def _cast_rows(src_ref, dst_ref):
    n = src_ref.shape[0] // CAST_ROWS

    def body(i, c):
        r = pl.multiple_of(i * CAST_ROWS, CAST_ROWS)
        dst_ref[pl.ds(r, CAST_ROWS), :] = src_ref[pl.ds(r, CAST_ROWS), :].astype(BF16)
        return c

    lax.fori_loop(0, n, body, 0)


def _layer_norm(z, g, b):
    mu = jnp.mean(z, axis=-1, keepdims=True)
    zc = z - mu
    var = jnp.mean(zc * zc, axis=-1, keepdims=True)
    return zc * lax.rsqrt(var + LN_EPS) * g + b


def _chunk_row(tok, chunk):
    seg, t = divmod(tok, CHUNK_SEG_TOKS)
    return seg * CHUNK_SEG_ROWS + t * CHUNK_TOK_ROWS + chunk


def _chunk_rows_store(ref, y):
    ref[...] = jnp.zeros_like(ref)
    for seg in range(SEGS):
        rows = slice(seg * CHUNK_SEG_TOKS, (seg + 1) * CHUNK_SEG_TOKS)
        for c in range(CHUNKS):
            dst = pl.ds(_chunk_row(seg * CHUNK_SEG_TOKS, c), CHUNK_SEG_TOKS, stride=CHUNK_TOK_ROWS)
            ref[dst, :] = y[rows, c * LANES:(c + 1) * LANES]


def _chunk_rows_load(ref, tok0, count, chunks=range(CHUNKS)):
    assert tok0 // CHUNK_SEG_TOKS == (tok0 + count - 1) // CHUNK_SEG_TOKS
    return jnp.concatenate(
        [ref[pl.ds(_chunk_row(tok0, c), count, stride=CHUNK_TOK_ROWS), :] for c in chunks], axis=1)


def _store_packed_rows(x_bf, rows_ref):
    n = x_bf.shape[0]
    bits = pltpu.bitcast(x_bf.astype(F32), jnp.uint32)
    packed = bits[:, D_MODEL // 2:] | (bits[:, :D_MODEL // 2] >> 16)
    for c in range(ROW_SUBLANES):
        rows_ref[pl.ds(c, n, stride=ROW_SUBLANES), :] = packed[:, c * LANES:(c + 1) * LANES]


def _load_packed_rows(rows_ref, x_bf_ref):
    n = x_bf_ref.shape[0]
    for c in range(ROW_SUBLANES):
        words = rows_ref[pl.ds(c, n, stride=ROW_SUBLANES), :]
        low = pltpu.bitcast(words << 16, F32).astype(BF16)
        high = pltpu.bitcast(words & jnp.uint32(0xFFFF0000), F32).astype(BF16)
        x_bf_ref[:, c * LANES:(c + 1) * LANES] = low
        x_bf_ref[:, D_MODEL // 2 + c * LANES:D_MODEL // 2 + (c + 1) * LANES] = high


def _tok_operands(x, tile_of=lambda i: i):
    if isinstance(x, tuple):
        xp, xs = x
        d = xp.shape[1]
        last_p = xp.shape[0] // TOK_TILE - 1
        specs = [pl.BlockSpec((TOK_TILE, d), lambda i, *_: (jnp.minimum(tile_of(i), last_p), 0)),
                 pl.BlockSpec((TOK_TILE, d), lambda i, *_: (0, 0))]
        return [xp, xs], specs, last_p + 2
    return ([x], [pl.BlockSpec((TOK_TILE, x.shape[1]), lambda i, *_: (tile_of(i), 0))],
            x.shape[0] // TOK_TILE)


def _tok_load(refs, is_sample=None):
    if len(refs) == 1:
        return refs[0][...]
    if is_sample is None:
        is_sample = pl.program_id(0) == pl.num_programs(0) - 1
    return jnp.where(is_sample, refs[1][...], refs[0][...])


def _lru_in_kernel(*refs, n_x, layer):
    x_refs = refs[:n_x]
    wx_hbm, wy_hbm, bx_ref, by_ref, xb_ref, yb_ref, wx_bf, wy_bf, stage, sems = refs[n_x:]

    @pl.when(pl.program_id(0) == 0)
    def _():
        _load_weight(wx_hbm.at[layer], wx_bf, stage, sems)
        _load_weight(wy_hbm.at[layer], wy_bf, stage, sems)

    x = _tok_load(x_refs).astype(BF16)
    _chunk_rows_store(xb_ref, jnp.dot(x, wx_bf[...], preferred_element_type=F32) + bx_ref[...])
    y = jnp.dot(x, wy_bf[...], preferred_element_type=F32) + by_ref[...]
    yb_ref[...] = jax.nn.gelu(y).astype(BF16)


def _lru_in(x, w_x, b_x, w_y, b_y, layer):
    arrays, specs, nt = _tok_operands(x)
    hbm = pl.BlockSpec(memory_space=pl.ANY)
    vec_spec = pl.BlockSpec((1, D_MODEL), lambda i: (0, 0))
    return pl.pallas_call(
        functools.partial(_lru_in_kernel, n_x=len(arrays), layer=layer),
        grid=(nt,),
        in_specs=specs + [hbm, hbm, vec_spec, vec_spec],
        out_specs=[pl.BlockSpec((CHUNK_TILE_ROWS, LANES), lambda i: (i, 0)),
                   pl.BlockSpec((TOK_TILE, D_MODEL), lambda i: (i, 0))],
        out_shape=[jax.ShapeDtypeStruct((nt * CHUNK_TILE_ROWS, LANES), F32),
                   jax.ShapeDtypeStruct((nt * TOK_TILE, D_MODEL), BF16)],
        scratch_shapes=[pltpu.VMEM((D_MODEL, D_MODEL), BF16), pltpu.VMEM((D_MODEL, D_MODEL), BF16),
                        pltpu.VMEM((2, CAST_ROWS, D_MODEL), F32), pltpu.SemaphoreType.DMA((2,))],
        compiler_params=_params(("arbitrary",), 48),
        name="lru_in",
    )(*arrays, w_x, w_y, b_x, b_y)


def _route(logits_t, b_router):
    aff = jax.nn.sigmoid(logits_t)
    sel = aff + b_router
    srow = [sel[e:e + 1, :] for e in range(N_EXPERTS)]
    arow = [aff[e:e + 1, :] for e in range(N_EXPERTS)]

    def top2_sum(v):
        pairs = [v[i] + v[j] for i in range(4) for j in range(i + 1, 4)]
        return functools.reduce(jnp.maximum, pairs)

    scores = [top2_sum(srow[4 * g:4 * g + 4]) for g in range(N_GROUPS)]
    best = scores[0]
    gi = jnp.zeros_like(best, dtype=jnp.int32)
    for g in range(1, N_GROUPS):
        upd = scores[g] > best
        best = jnp.where(upd, scores[g], best)
        gi = jnp.where(upd, g, gi)

    def pick_group(rows, j):
        out = rows[j]
        for g in range(1, N_GROUPS):
            out = jnp.where(gi == g, rows[4 * g + j], out)
        return out

    v = [pick_group(srow, j) for j in range(EXPERTS_PER_GROUP)]
    a = [pick_group(arow, j) for j in range(EXPERTS_PER_GROUP)]

    m1, i1 = v[0], jnp.zeros_like(gi)
    for j in range(1, EXPERTS_PER_GROUP):
        upd = v[j] > m1
        m1 = jnp.where(upd, v[j], m1)
        i1 = jnp.where(upd, j, i1)
    m2 = jnp.full_like(m1, NEG_INF)
    i2 = jnp.zeros_like(gi)
    for j in range(EXPERTS_PER_GROUP):
        cand = jnp.where(i1 == j, NEG_INF, v[j])
        upd = cand > m2
        m2 = jnp.where(upd, cand, m2)
        i2 = jnp.where(upd, j, i2)

    def pick_idx(rows, idx):
        out = rows[0]
        for j in range(1, EXPERTS_PER_GROUP):
            out = jnp.where(idx == j, rows[j], out)
        return out

    a1 = pick_idx(a, i1)
    a2 = pick_idx(a, i2)
    tot = a1 + a2
    e_idx = jnp.concatenate([gi * EXPERTS_PER_GROUP + i1, gi * EXPERTS_PER_GROUP + i2], axis=0)
    gates = jnp.concatenate([a1 / tot, a2 / tot], axis=0)
    return e_idx, gates


def _proj_ln_kernel(*refs, n_m, n_res):
    m_refs = refs[:n_m]
    w_ref, b_ref = refs[n_m:n_m + 2]
    res_refs = refs[n_m + 2:n_m + 2 + n_res]
    (g_ref, beta_ref, wr_ref, br_ref, x_ref, xrow_ref, e_ref, gate_ref,
     wbf_ref, ya, yb) = refs[n_m + 2 + n_res:]
    i = pl.program_id(0)
    n_tiles = pl.num_programs(0) - 1

    @pl.when(i == 0)
    def _():
        _cast_rows(w_ref, wbf_ref)
        yb[...] = jnp.zeros_like(yb)

    for parity, (cur, prev) in enumerate(((ya, yb), (yb, ya))):
        @pl.when(i % 2 == parity)
        def _():
            cur[...] = jnp.dot(_tok_load(m_refs, i >= n_tiles - 1), wbf_ref[...],
                               preferred_element_type=F32)
            y = prev[...] + b_ref[...]
            x = _layer_norm(ALPHA * _tok_load(res_refs, i == n_tiles) + y, g_ref[...], beta_ref[...])
            x_ref[...] = x
            x_bf = x.astype(BF16)
            _store_packed_rows(x_bf, xrow_ref)
            logits_t = lax.dot_general(wr_ref[...].astype(BF16), x_bf,
                                       (((1,), (1,)), ((), ())), preferred_element_type=F32)
            e_idx, gates = _route(logits_t, br_ref[...])
            e_ref[...] = e_idx
            gate_ref[...] = gates


def _proj_ln(m, w, w_index, b, res, g, beta, wr_t, br, *, name):
    nt = _tok_operands(m)[2]
    m_arrays, m_specs, _ = _tok_operands(m, lambda i: jnp.minimum(i, nt - 1))
    res_arrays, res_specs, _ = _tok_operands(res, lambda i: jnp.maximum(i - 1, 0))
    k = w.shape[-2]
    tm = TOK_TILE
    n = nt * tm
    row = lambda i: (jnp.maximum(i - 1, 0), 0)
    const = lambda i: (0, 0)
    x, x_rows, e_idx, gates = pl.pallas_call(
        functools.partial(_proj_ln_kernel, n_m=len(m_arrays), n_res=len(res_arrays)),
        grid=(nt + 1,),
        in_specs=m_specs + [
            pl.BlockSpec((None,) * len(w_index) + (k, D_MODEL), lambda i: w_index + (0, 0),
                         pipeline_mode=pl.Buffered(1)),
            pl.BlockSpec((1, D_MODEL), const),
        ] + res_specs + [
            pl.BlockSpec((1, D_MODEL), const),
            pl.BlockSpec((1, D_MODEL), const),
            pl.BlockSpec((N_EXPERTS, D_MODEL), const),
            pl.BlockSpec((N_EXPERTS, 1), const),
        ],
        out_specs=[
            pl.BlockSpec((tm, D_MODEL), row),
            pl.BlockSpec((tm * ROW_SUBLANES, LANES), row),
            pl.BlockSpec((None, 2, tm), lambda i: (jnp.maximum(i - 1, 0), 0, 0)),
            pl.BlockSpec((None, 2, tm), lambda i: (jnp.maximum(i - 1, 0), 0, 0)),
        ],
        out_shape=[
            jax.ShapeDtypeStruct((n, D_MODEL), F32),
            jax.ShapeDtypeStruct((n * ROW_SUBLANES, LANES), jnp.uint32),
            jax.ShapeDtypeStruct((nt, 2, tm), jnp.int32),
            jax.ShapeDtypeStruct((nt, 2, tm), F32),
        ],
        scratch_shapes=[pltpu.VMEM((k, D_MODEL), BF16),
                        pltpu.VMEM((tm, D_MODEL), F32), pltpu.VMEM((tm, D_MODEL), F32)],
        compiler_params=_params(("arbitrary",), 52),
        name=name,
    )(*m_arrays, w, b, *res_arrays, g, beta, wr_t, br)
    e_idx = e_idx.transpose(1, 0, 2).reshape(2, n)
    gates = gates.transpose(1, 0, 2).reshape(2, n)
    return x, x_rows, e_idx, gates


def _plan_kernel(e_ref, pos_ref, meta_ref, rank_ref):
    nrow = e_ref.shape[0]
    ri = lax.broadcasted_iota(jnp.int32, (LANES, LANES), 0)
    ci = lax.broadcasted_iota(jnp.int32, (LANES, LANES), 1)
    tri = jnp.where(ri <= ci, 1.0, 0.0).astype(BF16)
    sub = lax.broadcasted_iota(jnp.int32, (N_EXPERTS, LANES), 0)

    def count_body(b, base):
        rows = [b * PLAN_UNROLL + u for u in range(PLAN_UNROLL)]
        onehots = [sub == e_ref[pl.ds(r, 1), :] for r in rows]
        locs = [jnp.dot(jnp.where(oh, 1.0, 0.0).astype(BF16), tri, preferred_element_type=F32)
                for oh in onehots]
        for r, onehot, loc in zip(rows, onehots, locs):
            rank_ref[pl.ds(r, 1), :] = jnp.sum(jnp.where(onehot, base + loc - 1.0, 0.0),
                                               axis=0, keepdims=True)
            base = base + jnp.broadcast_to(loc[:, LANES - 1:LANES], (N_EXPERTS, LANES))
        return base

    count = lax.fori_loop(0, nrow // PLAN_UNROLL, count_body, jnp.zeros((N_EXPERTS, LANES), F32))
    ntile = jnp.floor((count + (MOE_TILE - 1.0)) * (1.0 / MOE_TILE))
    offs = []
    acc = jnp.zeros((1, LANES), F32)
    for e in range(N_EXPERTS):
        offs.append(acc)
        acc = acc + ntile[e:e + 1, :]
    tile_off = jnp.concatenate(offs, axis=0)
    tile_end = tile_off + ntile
    lane = lax.broadcasted_iota(jnp.int32, (N_EXPERTS, LANES), 1).astype(F32)
    tile_expert = jnp.sum(jnp.where(tile_end <= lane, 1.0, 0.0), axis=0, keepdims=True)
    tile_expert = jnp.minimum(tile_expert, N_EXPERTS - 1.0)
    own = jnp.logical_and(tile_off <= lane, lane < tile_end)
    run_end = jnp.sum(jnp.where(own, tile_end, 0.0), axis=0, keepdims=True)
    next_expert = jnp.sum(jnp.where(tile_end <= run_end, 1.0, 0.0), axis=0, keepdims=True)
    has_next = jnp.logical_and(lane[0:1, :] < acc, run_end < acc)
    next_expert = jnp.where(has_next, next_expert, -1.0)
    meta = jnp.concatenate([tile_expert, acc, next_expert, jnp.zeros((5, LANES), F32)], axis=0)
    meta_ref[...] = meta.astype(jnp.int32)
    row_off = tile_off * float(MOE_TILE)

    def pos_body(r, c):
        onehot = sub == e_ref[pl.ds(r, 1), :]
        p = jnp.sum(jnp.where(onehot, row_off, 0.0), axis=0, keepdims=True) + rank_ref[pl.ds(r, 1), :]
        pos_ref[pl.ds(r, 1), :] = p.astype(jnp.int32)
        return c

    lax.fori_loop(0, nrow, pos_body, 0)


def _plan(e_idx, *, name):
    n2 = e_idx.shape[0] * e_idx.shape[1]
    assert n2 % (LANES * PLAN_UNROLL) == 0
    e2d = e_idx.reshape(n2 // LANES, LANES)
    pos, meta = pl.pallas_call(
        _plan_kernel,
        out_shape=[jax.ShapeDtypeStruct(e2d.shape, jnp.int32),
                   jax.ShapeDtypeStruct((8, LANES), jnp.int32)],
        scratch_shapes=[pltpu.VMEM(e2d.shape, F32)],
        name=name,
    )(e2d)
    return pos.reshape(n2), meta[0], meta[1, :1], meta[2]


def _invert_kernel(pos_ref, pair_ref):
    n_rows = pair_ref.shape[0]
    n_pairs = pos_ref.shape[0]

    def fill_body(b, c):
        for u in range(SCALAR_UNROLL):
            pair_ref[b * SCALAR_UNROLL + u] = -1
        return c

    def pair_body(b, c):
        rows = [pos_ref[b * PAIR_UNROLL + u] for u in range(PAIR_UNROLL)]
        for u in range(PAIR_UNROLL):
            pair_ref[rows[u]] = b * PAIR_UNROLL + u
        return c

    lax.fori_loop(0, n_rows // SCALAR_UNROLL, fill_body, 0)
    lax.fori_loop(0, n_pairs // PAIR_UNROLL, pair_body, 0)


def _invert(pos, n_rows, *, name):
    return pl.pallas_call(
        _invert_kernel,
        grid_spec=pltpu.PrefetchScalarGridSpec(
            num_scalar_prefetch=1,
            grid=(1,),
            in_specs=[],
            out_specs=pl.BlockSpec(memory_space=pltpu.SMEM),
        ),
        out_shape=jax.ShapeDtypeStruct((n_rows,), jnp.int32),
        name=name,
    )(pos)


def _expert_changed(te_ref, i):
    return jnp.logical_or(i == 0, te_ref[i] != te_ref[jnp.maximum(i - 1, 0)])


class _ExpertWeights:
    def __init__(self, mats, layer, st, sems):
        self.mats, self.layer, self.st, self.sems = mats, layer, st, sems

    def _copies(self, expert, c):
        out = []
        for w_hbm, stage, _, _ in self.mats:
            rows = stage.shape[1]
            src = w_hbm.at[self.layer, expert, pl.ds(pl.multiple_of(c * rows, rows), rows), :]
            out.append(pltpu.make_async_copy(src, stage.at[c % 2], self.sems.at[c % 2]))
        return out

    def _start(self, expert, c):
        for cp in self._copies(expert, c):
            cp.start()
        self.st[2] = c + 1

    def _convert(self, c_src, c_dst):
        for _, stage, w_next, _ in self.mats:
            rows = stage.shape[1]
            dst = pl.ds(pl.multiple_of(c_dst * rows, rows), rows)
            w_next[dst, :] = stage[c_src % 2].astype(BF16)

    def reset(self):
        for _, stage, _, _ in self.mats:
            stage[...] = jnp.zeros_like(stage)
        self.st[1] = 0
        self.st[2] = 0

    def switch_to(self, expert):
        st = self.st

        def body(c, carry):
            @pl.when(c >= st[2])
            def _():
                self._start(expert, c)

            @pl.when(jnp.logical_and(c + 1 < W_CHUNKS, c + 1 >= st[2]))
            def _():
                self._start(expert, c + 1)

            for cp in self._copies(expert, c):
                cp.wait()
            self._convert(c, c)
            return carry

        lax.fori_loop(st[1], W_CHUNKS, body, 0)
        for _, _, w_next, w_cur in self.mats:
            _copy_rows(w_next, w_cur)
        st[1] = 0
        st[2] = 0

    def begin_step(self, next_expert):
        st = self.st
        done, issued = st[1], st[2]
        has_next = next_expert >= 0
        active = jnp.logical_and(has_next, done < issued)

        @pl.when(jnp.logical_and(has_next, jnp.logical_and(issued < W_CHUNKS, issued < done + 2)))
        def _():
            self._start(next_expert, issued)

        @pl.when(active)
        def _():
            for cp in self._copies(next_expert, done):
                cp.wait()

        return active, done

    def convert_step(self, active, done):
        self._convert(jnp.where(active, done, done + 1), jnp.where(active, done, W_CHUNKS))

    def end_step(self, active, done):
        @pl.when(active)
        def _():
            self.st[1] = done + 1


def _copy_rows(src_ref, dst_ref):
    n = dst_ref.shape[0] // CAST_ROWS

    def body(i, c):
        r = pl.multiple_of(i * CAST_ROWS, CAST_ROWS)
        dst_ref[pl.ds(r, CAST_ROWS), :] = src_ref[pl.ds(r, CAST_ROWS), :]
        return c

    lax.fori_loop(0, n, body, 0)


def _moe_tile_kernel(pair_ref, te_ref, nu_ref, nxt_ref, x_hbm, wg_hbm, wu_hbm, wd_hbm, out_hbm,
                     wg_cur, wg_next, wg_stage, wu_cur, wu_next, wu_stage, wd_cur, wd_next, wd_stage,
                     x_rows, x_bf, ya, yb, gsem, ssems, tsem, wsems, st, *, n_tok, n_tiles, layer):
    i = pl.program_id(0)
    nu = nu_ref[0]
    running = i < nu
    trash = 2 * n_tok
    weights = _ExpertWeights([(wg_hbm, wg_stage, wg_next, wg_cur), (wu_hbm, wu_stage, wu_next, wu_cur),
                              (wd_hbm, wd_stage, wd_next, wd_cur)], layer, st, wsems)

    def gather_copy(tile, r):
        p = pair_ref[tile * MOE_TILE + r]
        tok = jnp.where(p >= n_tok, p - n_tok, jnp.maximum(p, 0))
        src = pl.ds(pl.multiple_of(tok * ROW_SUBLANES, ROW_SUBLANES), ROW_SUBLANES)
        return pltpu.make_async_copy(x_hbm.at[src, :], x_rows.at[pl.ds(r * ROW_SUBLANES, ROW_SUBLANES), :],
                                     gsem)

    def scatter_copy(tile, r, buf, sem):
        p = pair_ref[tile * MOE_TILE + r]
        dst = jnp.where(p < 0, trash + r, p)
        return pltpu.make_async_copy(buf.at[pl.ds(r, 1), :], out_hbm.at[pl.ds(dst, 1), :], sem)

    def wait_scatter(buf, sem):
        pltpu.make_async_copy(buf, out_hbm.at[pl.ds(0, MOE_TILE), :], sem).wait()

    @pl.when(i == 0)
    def _():
        def body(rb, c):
            for u in range(DMA_UNROLL):
                gather_copy(0, rb * DMA_UNROLL + u).start()
            return c
        lax.fori_loop(0, MOE_TILE // DMA_UNROLL, body, 0)
        yb[...] = jnp.zeros_like(yb)
        fill = pltpu.make_async_copy(yb, out_hbm.at[pl.ds(trash, MOE_TILE), :], tsem)
        fill.start()
        fill.wait()
        weights.reset()

    @pl.when(i <= nu)
    def _():
        pltpu.make_async_copy(x_hbm.at[pl.ds(0, MOE_TILE * ROW_SUBLANES), :], x_rows, gsem).wait()

    @pl.when(jnp.logical_and(running, _expert_changed(te_ref, i)))
    def _():
        weights.switch_to(te_ref[i])

    active, done = weights.begin_step(jnp.where(running, nxt_ref[i], -1))

    for parity, (cur, prev) in enumerate(((ya, yb), (yb, ya))):
        cur_sem, prev_sem = ssems.at[parity], ssems.at[1 - parity]
        mine = i % 2 == parity

        @pl.when(jnp.logical_and(mine, jnp.logical_and(i >= 1, i - 1 <= nu)))
        def _():
            wait_scatter(cur, cur_sem)

        @pl.when(jnp.logical_and(mine, running))
        def _():
            _load_packed_rows(x_rows, x_bf)
            next_tile = jnp.minimum(i + 1, n_tiles - 1)
            prev_tile = jnp.maximum(i - 1, 0)
            for r in range(MOE_TILE):
                gather_copy(next_tile, r).start(priority=r % 2)
            for r in range(MOE_TILE):
                scatter_copy(prev_tile, r, prev, prev_sem).start(priority=r % 2)
            weights.convert_step(active, done)
            x = x_bf[...]
            a = jnp.dot(x, wg_cur[...], preferred_element_type=F32)
            b = jnp.dot(x, wu_cur[...], preferred_element_type=F32)
            h = (jax.nn.silu(a) * b).astype(BF16)
            cur[...] = jnp.dot(h, wd_cur[...], preferred_element_type=F32)

        @pl.when(jnp.logical_and(mine, i == nu))
        def _():
            def body(rb, c):
                for u in range(DMA_UNROLL):
                    scatter_copy(i - 1, rb * DMA_UNROLL + u, prev, prev_sem).start()
                return c
            lax.fori_loop(0, MOE_TILE // DMA_UNROLL, body, 0)

            @pl.when(i == n_tiles)
            def _():
                wait_scatter(prev, prev_sem)

    weights.end_step(active, done)


def _moe_tiles(x, pair, tile_expert, n_used, next_expert, w_gate, w_up, w_down, layer):
    n_tok = x.shape[0] // ROW_SUBLANES
    n_tiles = pair.shape[0] // MOE_TILE
    hbm = pl.BlockSpec(memory_space=pl.ANY)

    def weight_bufs(k, n):
        chunk = k // W_CHUNKS
        return [pltpu.VMEM((k, n), BF16), pltpu.VMEM((k + chunk, n), BF16), pltpu.VMEM((2, chunk, n), F32)]

    return pl.pallas_call(
        functools.partial(_moe_tile_kernel, n_tok=n_tok, n_tiles=n_tiles, layer=layer),
        grid_spec=pltpu.PrefetchScalarGridSpec(
            num_scalar_prefetch=4,
            grid=(n_tiles + 1,),
            in_specs=[hbm, hbm, hbm, hbm],
            out_specs=hbm,
            scratch_shapes=weight_bufs(D_MODEL, D_EXPERT) + weight_bufs(D_MODEL, D_EXPERT)
            + weight_bufs(D_EXPERT, D_MODEL) + [
                pltpu.VMEM((MOE_TILE * ROW_SUBLANES, LANES), jnp.uint32),
                pltpu.VMEM((MOE_TILE, D_MODEL), BF16),
                pltpu.VMEM((MOE_TILE, D_MODEL), F32), pltpu.VMEM((MOE_TILE, D_MODEL), F32),
                pltpu.SemaphoreType.DMA(()), pltpu.SemaphoreType.DMA((2,)),
                pltpu.SemaphoreType.DMA(()), pltpu.SemaphoreType.DMA((2,)),
                pltpu.SMEM((4,), jnp.int32)],
        ),
        out_shape=jax.ShapeDtypeStruct((2 * n_tok + MOE_TILE, D_MODEL), F32),
        compiler_params=_params(("arbitrary",), 56),
        name=f"moe_tiles_{layer}",
    )(pair, tile_expert, n_used, next_expert, x, w_gate, w_up, w_down)


def _combine(y0_ref, y1_ref, res_ref, gate_ref, g_ref, beta_ref):
    gate = gate_ref[...]
    ffn = gate[:, 0:1] * y0_ref[...] + gate[:, 1:2] * y1_ref[...]
    return _layer_norm(ALPHA * res_ref[...] + ffn, g_ref[...], beta_ref[...])


def _combine_specs(n):
    nt = n // TOK_TILE
    row = lambda i: (i, 0)
    const = lambda i: (0, 0)
    return [pl.BlockSpec((TOK_TILE, D_MODEL), row),
            pl.BlockSpec((TOK_TILE, D_MODEL), lambda i: (i + nt, 0)),
            pl.BlockSpec((TOK_TILE, D_MODEL), row),
            pl.BlockSpec((TOK_TILE, 2), row),
            pl.BlockSpec((1, D_MODEL), const),
            pl.BlockSpec((1, D_MODEL), const)]


def _combine_split_kernel(y0_ref, y1_ref, res_ref, gate_ref, g_ref, beta_ref, prompt_ref, sample_ref):
    x = _combine(y0_ref, y1_ref, res_ref, gate_ref, g_ref, beta_ref)
    is_sample = pl.program_id(0) == pl.num_programs(0) - 1

    @pl.when(jnp.logical_not(is_sample))
    def _():
        prompt_ref[...] = x

    @pl.when(is_sample)
    def _():
        sample_ref[...] = x


def _combine_split(ys, res, gates_col, g, beta, *, name):
    n = res.shape[0]
    tm = TOK_TILE
    nt = n // tm
    return pl.pallas_call(
        _combine_split_kernel,
        grid=(nt,),
        in_specs=_combine_specs(n),
        out_specs=[pl.BlockSpec((tm, D_MODEL), lambda i: (jnp.minimum(i, nt - 2), 0)),
                   pl.BlockSpec((tm, D_MODEL), lambda i: (0, 0))],
        out_shape=[jax.ShapeDtypeStruct((n - tm, D_MODEL), F32),
                   jax.ShapeDtypeStruct((tm, D_MODEL), F32)],
        compiler_params=_params(("arbitrary",), 40),
        name=name,
    )(ys, ys, res, gates_col, g, beta)


def _load_weight(w_hbm, wbf_ref, stage_ref, sems):
    rows = stage_ref.shape[1]
    n_chunks = wbf_ref.shape[0] // rows

    def chunk_copy(c):
        return pltpu.make_async_copy(w_hbm.at[pl.ds(c * rows, rows), :], stage_ref.at[c % 2],
                                     sems.at[c % 2])

    chunk_copy(0).start()
    for c in range(n_chunks):
        if c + 1 < n_chunks:
            chunk_copy(c + 1).start()
        chunk_copy(c).wait()
        wbf_ref[c * rows:(c + 1) * rows, :] = stage_ref[c % 2].astype(BF16)


def _combine_qkv_kernel(y0_ref, y1_ref, res_ref, gate_ref, g_ref, beta_ref, wq_hbm, wkv_hbm,
                        x_ref, q_ref, kv_ref, wq_bf, wkv_bf, stage_q, stage_kv, sems, *, wq_index):
    @pl.when(pl.program_id(0) == 0)
    def _():
        wq = wq_hbm
        for k in wq_index:
            wq = wq.at[k]
        _load_weight(wq, wq_bf, stage_q, sems)
        _load_weight(wkv_hbm, wkv_bf, stage_kv, sems)

    x = _combine(y0_ref, y1_ref, res_ref, gate_ref, g_ref, beta_ref)
    x_ref[...] = x
    x_bf = x.astype(BF16)
    q = jnp.dot(x_bf, wq_bf[...], preferred_element_type=F32) * (HEAD_DIM ** -0.5)
    q_ref[...] = q.astype(BF16)
    kv_ref[...] = jnp.dot(x_bf, wkv_bf[...], preferred_element_type=F32)


def _combine_qkv(ys, res, gates_col, g, beta, w_q, wq_index, w_kv, *, name):
    n = res.shape[0]
    tm = TOK_TILE
    row = lambda i: (i, 0)
    hbm = pl.BlockSpec(memory_space=pl.ANY)
    return pl.pallas_call(
        functools.partial(_combine_qkv_kernel, wq_index=wq_index),
        grid=(n // tm,),
        in_specs=_combine_specs(n) + [hbm, hbm],
        out_specs=[pl.BlockSpec((tm, D_MODEL), row),
                   pl.BlockSpec((tm, D_MODEL), row),
                   pl.BlockSpec((tm, 2 * KV_DIM), row)],
        out_shape=[jax.ShapeDtypeStruct((n, D_MODEL), F32),
                   jax.ShapeDtypeStruct((n, D_MODEL), BF16),
                   jax.ShapeDtypeStruct((n, 2 * KV_DIM), F32)],
        scratch_shapes=[pltpu.VMEM((D_MODEL, D_MODEL), BF16),
                        pltpu.VMEM((D_MODEL, 2 * KV_DIM), BF16),
                        pltpu.VMEM((2, CAST_ROWS, D_MODEL), F32),
                        pltpu.VMEM((2, CAST_ROWS, 2 * KV_DIM), F32),
                        pltpu.SemaphoreType.DMA((2,))],
        compiler_params=_params(("arbitrary",), 52),
        name=name,
    )(ys, ys, res, gates_col, g, beta, w_q, w_kv)


def _moe_block(x_rows, e_idx, w_gate, w_up, w_down, layer):
    n = x_rows.shape[0] // ROW_SUBLANES
    n_tiles = -(-(2 * n + N_EXPERTS * (MOE_TILE - 1)) // MOE_TILE)
    pos, tile_expert, n_used, next_expert = _plan(e_idx, name=f"moe_plan_{layer}")
    pair = _invert(pos, n_tiles * MOE_TILE, name=f"moe_invert_{layer}")
    return _moe_tiles(x_rows, pair, tile_expert, n_used, next_expert, w_gate, w_up, w_down, layer)


def _sigmoid_of_half(half_x):
    return 0.5 * jnp.tanh(half_x) + 0.5


def _log_sigmoid(x):
    return -(jnp.maximum(-x, 0.0) + jnp.log1p(jnp.exp(-jnp.abs(x))))


def _lru_gate_blocks(xcs, blocks, wrg_bf, wig_bf, brg_ref, big_ref, lam_ref):
    xbs = [xc.astype(BF16) for xc in xcs]
    r_lin = [jnp.dot(xb, wrg_bf[n], preferred_element_type=F32) for xb, n in zip(xbs, blocks)]
    i_lin = [jnp.dot(xb, wig_bf[n], preferred_element_type=F32) for xb, n in zip(xbs, blocks)]
    out = []
    for xc, n, rl, il in zip(xcs, blocks, r_lin, i_lin):
        cols = slice(n * LRU_BLOCK, (n + 1) * LRU_BLOCK)
        r = _sigmoid_of_half(rl + 0.5 * brg_ref[:, cols])
        i = _sigmoid_of_half(il + 0.5 * big_ref[:, cols])
        log_a = r * (LRU_C * _log_sigmoid(lam_ref[:, cols]))
        a = jnp.exp(log_a)
        u = xc * i * jnp.sqrt(-jnp.tanh(log_a) * (a * a + 1.0))
        out.append((a, u))
    return out


def _lru_gate_block(xc, n, wrg_bf, wig_bf, brg_ref, big_ref, lam_ref):
    return _lru_gate_blocks([xc], [n], wrg_bf, wig_bf, brg_ref, big_ref, lam_ref)[0]


def _cast_gate_weights(wrg_ref, wig_ref, wrg_bf, wig_bf):
    for n in range(LRU_BLOCKS):
        wrg_bf[n] = (0.5 * wrg_ref[n]).astype(BF16)
        wig_bf[n] = (0.5 * wig_ref[n]).astype(BF16)


def _lru_prompt_kernel(xb_ref, yb_ref, cw_ref, cb_ref, wrg_ref, wig_ref, brg_ref, big_ref, lam_ref,
                       m_ref, conv_ref, hlast_ref, xs, tail, a_s, u_s, hs_t, h_s, wrg_bf, wig_bf):
    b = pl.program_id(0)
    j = pl.program_id(1)
    tt = m_ref.shape[0]
    seg_len = tt // SEGS
    taps = CONV_WIDTH - 1
    head = SEGS * taps

    @pl.when(jnp.logical_and(b == 0, j == 0))
    def _():
        _cast_gate_weights(wrg_ref, wig_ref, wrg_bf, wig_bf)

    @pl.when(j == 0)
    def _():
        tail[...] = jnp.zeros_like(tail)
        h_s[...] = jnp.zeros_like(h_s)

    for q in range(seg_len):
        xs[head + SEGS * q:head + SEGS * (q + 1), :] = jnp.concatenate(
            [xb_ref[pl.ds(_chunk_row(q, c), SEGS, stride=CHUNK_SEG_ROWS), :] for c in range(CHUNKS)],
            axis=1)
    sub = lax.broadcasted_iota(jnp.int32, (SEGS, D_MODEL), 0)
    for k in range(taps):
        last = head + SEGS * (seg_len - taps + k)
        joined = jnp.where(sub == SEGS - 1, tail[SEGS * k:SEGS * (k + 1), :], xs[last:last + SEGS, :])
        xs[SEGS * k:SEGS * (k + 1), :] = pltpu.roll(joined, 1, axis=0)
    tail[...] = xs[head + SEGS * (seg_len - taps):head + SEGS * seg_len, :]

    for first_block in range(0, LRU_BLOCKS, GATE_BLOCKS):
        blocks = range(first_block, first_block + GATE_BLOCKS)
        xcs = []
        for n in blocks:
            cols = slice(n * LRU_BLOCK, (n + 1) * LRU_BLOCK)
            xc = cb_ref[:, cols] + cw_ref[0:1, cols] * xs[0:tt, cols]
            for k in range(1, CONV_WIDTH):
                xc = xc + cw_ref[k:k + 1, cols] * xs[SEGS * k:SEGS * k + tt, cols]
            xcs.append(xc)
        gates = _lru_gate_blocks(xcs, blocks, wrg_bf, wig_bf, brg_ref, big_ref, lam_ref)
        for n, (a, u) in zip(blocks, gates):
            cols = slice(n * LRU_BLOCK, (n + 1) * LRU_BLOCK)
            a_s[:, cols] = a
            u_s[:, cols] = u

    def scan_body(q, carry):
        h, prod = carry
        rows = pl.ds(pl.multiple_of(q * SEGS, SEGS), SEGS)
        a = a_s[rows, :]
        h = a * h + u_s[rows, :]
        prod = a * prod
        u_s[rows, :] = h
        a_s[rows, :] = prod
        return h, prod

    h_end, prod_end = lax.fori_loop(
        0, seg_len, scan_body,
        (jnp.zeros((SEGS, D_MODEL), F32), jnp.ones((SEGS, D_MODEL), F32)))
    state = h_s[...]
    entering = []
    for s in range(SEGS):
        entering.append(state)
        state = h_end[s:s + 1, :] + prod_end[s:s + 1, :] * state
    h_s[...] = state
    enter = jnp.concatenate(entering, axis=0)

    def fix_body(q, carry):
        rows = pl.ds(pl.multiple_of(q * SEGS, SEGS), SEGS)
        h = u_s[rows, :] + a_s[rows, :] * enter
        for c in range(CHUNKS):
            hs_t[pl.ds(q * CHUNK_TOK_ROWS + c, SEGS, stride=CHUNK_SEG_ROWS), :] = h[:, c * LANES:(c + 1) * LANES]
        return carry

    lax.fori_loop(0, seg_len, fix_body, 0)
    for s in range(SEGS):
        rows = slice(s * seg_len, (s + 1) * seg_len)
        hs = _chunk_rows_load(hs_t, s * seg_len, seg_len)
        m_ref[rows, :] = (hs * yb_ref[rows, :].astype(F32)).astype(BF16)

    @pl.when(j == pl.num_programs(1) - 1)
    def _():
        for k in range(taps):
            conv_ref[k:k + 1, :] = tail[SEGS * k + SEGS - 1:SEGS * (k + 1), :]
        hlast_ref[...] = state


def _lru_prompt(xb, yb, batch, seq, cw, cb, wrg, wig, brg, big, lam):
    tt = TOK_TILE
    nj = seq // tt
    row = lambda b, j: (b * nj + j, 0)
    const2 = lambda b, j: (0, 0)
    const3 = lambda b, j: (0, 0, 0)
    return pl.pallas_call(
        _lru_prompt_kernel,
        grid=(batch, nj),
        in_specs=[
            pl.BlockSpec((CHUNK_TILE_ROWS, LANES), row),
            pl.BlockSpec((tt, D_MODEL), row),
            pl.BlockSpec((CONV_WIDTH, D_MODEL), const2),
            pl.BlockSpec((1, D_MODEL), const2),
            pl.BlockSpec((LRU_BLOCKS, LRU_BLOCK, LRU_BLOCK), const3),
            pl.BlockSpec((LRU_BLOCKS, LRU_BLOCK, LRU_BLOCK), const3),
            pl.BlockSpec((1, D_MODEL), const2),
            pl.BlockSpec((1, D_MODEL), const2),
            pl.BlockSpec((1, D_MODEL), const2),
        ],
        out_specs=[
            pl.BlockSpec((tt, D_MODEL), row),
            pl.BlockSpec((None, CONV_WIDTH - 1, D_MODEL), lambda b, j: (b, 0, 0)),
            pl.BlockSpec((None, 1, D_MODEL), lambda b, j: (b, 0, 0)),
        ],
        out_shape=[
            jax.ShapeDtypeStruct((batch * seq, D_MODEL), BF16),
            jax.ShapeDtypeStruct((batch, CONV_WIDTH - 1, D_MODEL), F32),
            jax.ShapeDtypeStruct((batch, 1, D_MODEL), F32),
        ],
        scratch_shapes=[
            pltpu.VMEM((tt + SEGS * (CONV_WIDTH - 1), D_MODEL), F32),
            pltpu.VMEM((SEGS * (CONV_WIDTH - 1), D_MODEL), F32),
            pltpu.VMEM((tt, D_MODEL), F32),
            pltpu.VMEM((tt, D_MODEL), F32),
            pltpu.VMEM((CHUNK_TILE_ROWS, LANES), F32),
            pltpu.VMEM((1, D_MODEL), F32),
            pltpu.VMEM((LRU_BLOCKS, LRU_BLOCK, LRU_BLOCK), BF16),
            pltpu.VMEM((LRU_BLOCKS, LRU_BLOCK, LRU_BLOCK), BF16),
        ],
        compiler_params=_params(("arbitrary", "arbitrary"), 40),
        name="lru_prompt",
    )(xb, yb, cw, cb, wrg, wig, brg, big, lam)


def _lru_sample_kernel(xb_ref, yb_ref, cs_ref, h0_ref, cw_ref, cb_ref, wrg_ref, wig_ref,
                       brg_ref, big_ref, lam_ref, m_ref, conv_ref, hlast_ref, wrg_bf, wig_bf, *, steps):
    batch = h0_ref.shape[0]
    _cast_gate_weights(wrg_ref, wig_ref, wrg_bf, wig_bf)
    m_ref[steps * batch:, :] = jnp.zeros((m_ref.shape[0] - steps * batch, D_MODEL), BF16)

    def slab(t, cols):
        if t < CONV_WIDTH - 1:
            return cs_ref[t, :, cols]
        t -= CONV_WIDTH - 1
        first, stop, _ = cols.indices(D_MODEL)
        return _chunk_rows_load(xb_ref, t * batch, batch, range(first // LANES, stop // LANES))

    for n in range(LRU_BLOCKS):
        cols = slice(n * LRU_BLOCK, (n + 1) * LRU_BLOCK)
        h = h0_ref[:, cols]
        for t in range(steps):
            xc = cb_ref[:, cols] + cw_ref[0:1, cols] * slab(t, cols)
            for k in range(1, CONV_WIDTH):
                xc = xc + cw_ref[k:k + 1, cols] * slab(t + k, cols)
            a, u = _lru_gate_block(xc, n, wrg_bf, wig_bf, brg_ref, big_ref, lam_ref)
            h = a * h + u
            rows = slice(t * batch, (t + 1) * batch)
            m_ref[rows, cols] = (h * yb_ref[rows, cols].astype(F32)).astype(BF16)
        hlast_ref[:, cols] = h
    for k in range(CONV_WIDTH - 1):
        conv_ref[k] = slab(steps + k, slice(None))


def _lru_sample(xb, yb, tile, steps, conv_state, h0, cw, cb, wrg, wig, brg, big, lam):
    batch = h0.shape[0]
    tok = pl.BlockSpec((TOK_TILE, D_MODEL), lambda i: (tile, 0))
    tok_chunks = pl.BlockSpec((CHUNK_TILE_ROWS, LANES), lambda i: (tile, 0))
    full = lambda a: pl.BlockSpec(a.shape, lambda i: (0,) * a.ndim)
    small = (conv_state, h0, cw, cb, wrg, wig, brg, big, lam)
    return pl.pallas_call(
        functools.partial(_lru_sample_kernel, steps=steps),
        grid=(1,),
        in_specs=[tok_chunks, tok] + [full(a) for a in small],
        out_specs=[
            pl.BlockSpec((TOK_TILE, D_MODEL), lambda i: (0, 0)),
            pl.BlockSpec((CONV_WIDTH - 1, batch, D_MODEL), lambda i: (0, 0, 0)),
            pl.BlockSpec((batch, D_MODEL), lambda i: (0, 0)),
        ],
        out_shape=[
            jax.ShapeDtypeStruct((TOK_TILE, D_MODEL), BF16),
            jax.ShapeDtypeStruct((CONV_WIDTH - 1, batch, D_MODEL), F32),
            jax.ShapeDtypeStruct((batch, D_MODEL), F32),
        ],
        scratch_shapes=[
            pltpu.VMEM((LRU_BLOCKS, LRU_BLOCK, LRU_BLOCK), BF16),
            pltpu.VMEM((LRU_BLOCKS, LRU_BLOCK, LRU_BLOCK), BF16),
        ],
        compiler_params=_params(("arbitrary",), 32),
        name="lru_sample",
    )(xb, yb, *small)


def _rel_bucket(dist):
    n = jnp.maximum(dist, 0)
    max_exact = N_BUCKETS // 2
    nf = jnp.maximum(n, 1).astype(F32)
    large = max_exact + (jnp.log(nf / max_exact) / math.log(MAX_DISTANCE / max_exact)
                         * (N_BUCKETS - max_exact)).astype(jnp.int32)
    large = jnp.minimum(large, N_BUCKETS - 1)
    return jnp.where(n < max_exact, n, large)


def _masked_buckets(dist):
    valid = (dist >= 0) & (dist < WINDOW)
    return jnp.where(valid, _rel_bucket(dist), -1).astype(jnp.int32)


def _build_bias(bucket, tab_ref, head):
    def body(bi, acc):
        return jnp.where(bucket == bi, tab_ref[bi * N_HEADS + head], acc)
    return lax.fori_loop(0, N_BUCKETS, body, jnp.full(bucket.shape, NEG_INF, F32))


def _softmax_pv(s, sink, v):
    m = jnp.maximum(jnp.max(s, axis=-1, keepdims=True), sink)
    p = jnp.exp(s - m)
    den = jnp.sum(p, axis=-1, keepdims=True) + jnp.exp(sink - m)
    return jnp.dot(p.astype(BF16), v, preferred_element_type=F32) / den


def _attn_prompt_kernel(q_ref, kvp_ref, kvc_ref, bucket_ref, tab_ref, sink_ref, o_ref, bias_s):
    b = pl.program_id(0)
    n = pl.program_id(1)

    @pl.when(jnp.logical_and(b == 0, n == 0))
    def _():
        bucket = bucket_ref[...]

        col = lax.broadcasted_iota(jnp.int32, (WINDOW, 2 * WINDOW), 1)

        def head_body(h, c):
            by_offset = jnp.broadcast_to(_build_bias(bucket, tab_ref, h), (WINDOW, 2 * WINDOW))
            bias = pltpu.roll(by_offset, 0, axis=1, stride=1, stride_axis=0)
            sink = sink_ref[h]
            g = h // GROUP
            r0 = pl.multiple_of((h % GROUP) * WINDOW, WINDOW)
            bias_s[0, g, pl.ds(r0, WINDOW), :] = jnp.where(col == 0, sink, bias)
            bias_s[1, g, pl.ds(r0, WINDOW), :] = jnp.where(
                col == 0, sink, jnp.where(col < WINDOW, NEG_INF, bias))
            return c

        lax.fori_loop(0, N_HEADS, head_body, 0)

    first = (n == 0).astype(jnp.int32)
    row = lax.broadcasted_iota(jnp.int32, kvp_ref.shape, 0)
    kv_prev = jnp.where(row == 0, 0.0, kvp_ref[...])
    kv = jnp.concatenate([kv_prev, kvc_ref[...]], axis=0).astype(BF16)
    ones = jnp.ones((2 * WINDOW, 2 * HEAD_DIM), BF16)
    lane = lax.broadcasted_iota(jnp.int32, (WINDOW, 2 * HEAD_DIM), 1)
    def scores(idx):
        g, pair = divmod(idx, GROUP // 2)
        h0 = g * GROUP + 2 * pair
        kg = kv[:, g * HEAD_DIM:(g + 1) * HEAD_DIM]
        qp = jnp.concatenate([q_ref[:, h * HEAD_DIM:(h + 1) * HEAD_DIM] for h in (h0, h0 + 1)], axis=0)
        s = lax.dot_general(qp, kg, (((1,), (1,)), ((), ())), preferred_element_type=F32)
        return s + bias_s[first, g, 2 * pair * WINDOW:(2 * pair + 2) * WINDOW, :]

    def finish(idx, o_ext):
        h0 = 2 * idx
        o = o_ext[:, :2 * HEAD_DIM] * (1.0 / o_ext[:, 2 * HEAD_DIM:])
        o_ref[:, h0 * HEAD_DIM:(h0 + 2) * HEAD_DIM] = jnp.where(
            lane < HEAD_DIM, o[:WINDOW], o[WINDOW:]).astype(BF16)

    n_pairs = N_HEADS // 2

    def values(idx, p):
        g = idx // (GROUP // 2)
        vg = kv[:, KV_DIM + g * HEAD_DIM:KV_DIM + (g + 1) * HEAD_DIM]
        v_ext = jnp.concatenate([vg, vg, ones], axis=1)
        return jnp.dot(p, v_ext, preferred_element_type=F32)

    for first_pair in range(0, n_pairs, PAIR_BLOCK):
        block = range(first_pair, first_pair + PAIR_BLOCK)
        ss = [scores(idx) for idx in block]
        ms = [jnp.max(s, axis=-1, keepdims=True) for s in ss]
        ps = [jnp.exp(s - m).astype(BF16) for s, m in zip(ss, ms)]
        os_ = [values(idx, p) for idx, p in zip(block, ps)]
        for idx, o_ext in zip(block, os_):
            finish(idx, o_ext)


def _attn_prompt(q, kv, batch, seq, bucket, tab, sinks):
    nb = seq // WINDOW
    smem = pl.BlockSpec(memory_space=pltpu.SMEM)
    return pl.pallas_call(
        _attn_prompt_kernel,
        grid=(batch, nb),
        in_specs=[
            pl.BlockSpec((WINDOW, D_MODEL), lambda b, n: (b * nb + n, 0)),
            pl.BlockSpec((WINDOW, 2 * KV_DIM), lambda b, n: (jnp.maximum(b * nb + n - 1, 0), 0)),
            pl.BlockSpec((WINDOW, 2 * KV_DIM), lambda b, n: (b * nb + n, 0)),
            pl.BlockSpec((1, 2 * WINDOW), lambda b, n: (0, 0)),
            smem, smem,
        ],
        out_specs=pl.BlockSpec((WINDOW, D_MODEL), lambda b, n: (b * nb + n, 0)),
        out_shape=jax.ShapeDtypeStruct((batch * seq, D_MODEL), BF16),
        scratch_shapes=[pltpu.VMEM((2, N_KV_HEADS, GROUP * WINDOW, 2 * WINDOW), F32)],
        compiler_params=_params(("arbitrary", "arbitrary"), 32),
        name="attn_prompt",
    )(q, kv, kv, bucket, tab, sinks)


def _attn_sample_kernel(q_ref, ck_ref, cv_ref, kn_ref, vn_ref, bucket_ref, tab_ref, sink_ref,
                        o_ref, kwin_ref, vwin_ref, bias_s):
    steps = q_ref.shape[0]
    k_all = jnp.concatenate([ck_ref[...], kn_ref[...]], axis=0)
    v_all = jnp.concatenate([cv_ref[...], vn_ref[...]], axis=0)
    kwin_ref[...] = k_all[steps:, :]
    vwin_ref[...] = v_all[steps:, :]

    @pl.when(pl.program_id(0) == 0)
    def _():
        bucket = bucket_ref[...]

        def head_body(h, c):
            bias_s[h] = _build_bias(bucket, tab_ref, h)
            return c

        lax.fori_loop(0, N_HEADS, head_body, 0)

    rows = lax.broadcasted_iota(jnp.int32, (GROUP * steps, 1), 0)
    k = k_all.astype(BF16)
    v = v_all.astype(BF16)
    groups = range(N_KV_HEADS)

    def heads_of(g):
        return range(g * GROUP, (g + 1) * GROUP)

    scores, sinks = [], []
    for g in groups:
        qg = jnp.concatenate([q_ref[:, h * HEAD_DIM:(h + 1) * HEAD_DIM] for h in heads_of(g)], axis=0)
        bias = jnp.concatenate([bias_s[h] for h in heads_of(g)], axis=0)
        sink = jnp.full((GROUP * steps, 1), sink_ref[g * GROUP], F32)
        for hh in range(1, GROUP):
            sink = jnp.where(rows >= hh * steps, sink_ref[g * GROUP + hh], sink)
        kg = k[:, g * HEAD_DIM:(g + 1) * HEAD_DIM]
        scores.append(lax.dot_general(qg, kg, (((1,), (1,)), ((), ())),
                                      preferred_element_type=F32) + bias)
        sinks.append(sink)
    outs = [_softmax_pv(scores[g], sinks[g], v[:, g * HEAD_DIM:(g + 1) * HEAD_DIM]) for g in groups]
    for g in groups:
        for hh, h in enumerate(heads_of(g)):
            o_ref[:, h * HEAD_DIM:(h + 1) * HEAD_DIM] = outs[g][hh * steps:(hh + 1) * steps].astype(BF16)


def _attn_sample(q, cache_k, cache_v, k_new, v_new, bucket, tab, sinks):
    batch, steps, _ = q.shape
    lk = WINDOW + steps
    smem = pl.BlockSpec(memory_space=pltpu.SMEM)
    per_seq = lambda rows, cols: pl.BlockSpec((None, rows, cols), lambda b: (b, 0, 0))
    return pl.pallas_call(
        _attn_sample_kernel,
        grid=(batch,),
        in_specs=[
            per_seq(steps, D_MODEL),
            per_seq(WINDOW, KV_DIM), per_seq(WINDOW, KV_DIM),
            per_seq(steps, KV_DIM), per_seq(steps, KV_DIM),
            pl.BlockSpec((steps, lk), lambda b: (0, 0)),
            smem, smem,
        ],
        out_specs=[per_seq(steps, D_MODEL), per_seq(WINDOW, KV_DIM), per_seq(WINDOW, KV_DIM)],
        out_shape=[jax.ShapeDtypeStruct((batch, steps, D_MODEL), BF16),
                   jax.ShapeDtypeStruct((batch, WINDOW, KV_DIM), F32),
                   jax.ShapeDtypeStruct((batch, WINDOW, KV_DIM), F32)],
        scratch_shapes=[pltpu.VMEM((N_HEADS, steps, lk), F32)],
        compiler_params=_params(("arbitrary",), 32),
        name="attn_sample",
    )(q, cache_k, cache_v, k_new, v_new, bucket, tab, sinks)


def kernel(x_prompt, x_sample, state_conv, state_rnn, cache_k_win, cache_v_win, ln_g, ln_b, lru_w_x, lru_b_x, lru_w_y, lru_b_y, lru_conv_w, lru_conv_b, lru_w_rg, lru_b_rg, lru_w_ig, lru_b_ig, lru_lam, lru_w_out, lru_b_out, attn_w_kv, attn_w_q, attn_w_o, attn_sinks, rel_bias, moe_w_router, moe_b_router, moe_w_gate, moe_w_up, moe_w_down):
    bp, seq, _ = x_prompt.shape
    bs, steps, _ = x_sample.shape
    n_p = bp * seq
    n_s = bs * steps

    assert n_p % TOK_TILE == 0 and n_s <= TOK_TILE
    sample_tile = n_p // TOK_TILE

    def pad_tile(rows):
        return jnp.pad(rows, ((0, TOK_TILE - n_s), (0, 0)))

    x0 = (x_prompt.reshape(n_p, D_MODEL),
          pad_tile(x_sample.transpose(1, 0, 2).reshape(n_s, D_MODEL)))
    wr_t = moe_w_router.T
    br = moe_b_router.reshape(N_EXPERTS, 1)
    vec = lambda a: a.reshape(1, -1)

    xb, yb = _lru_in(x0, lru_w_x, vec(lru_b_x[0]), lru_w_y, vec(lru_b_y[0]), 0)
    lru_args = (lru_conv_w[0], vec(lru_conv_b[0]), lru_w_rg[0], lru_w_ig[0],
                vec(lru_b_rg[0]), vec(lru_b_ig[0]), vec(lru_lam[0]))
    m_p, conv_p, rnn_p = _lru_prompt(xb, yb, bp, seq, *lru_args)
    m_s, conv_s, rnn_s = _lru_sample(xb, yb, sample_tile, steps,
                                     state_conv[0].transpose(1, 0, 2), state_rnn[0], *lru_args)
    x1, x1_rows, e_idx, gates = _proj_ln((m_p, m_s), lru_w_out, (0,), vec(lru_b_out[0]), x0,
                                vec(ln_g[0, 0]), vec(ln_b[0, 0]), wr_t, br, name="lru_out_ln")
    ys = _moe_block(x1_rows, e_idx, moe_w_gate, moe_w_up, moe_w_down, 0)

    x2, q, kv = _combine_qkv(ys, x1, gates.T, vec(ln_g[0, 1]), vec(ln_b[0, 1]),
                             attn_w_q, (0,), attn_w_kv, name="moe_combine_qkv")
    tab = rel_bias.reshape(-1)
    sinks = attn_sinks[0]
    offsets = jnp.arange(2 * WINDOW)[None, :]
    o_p = _attn_prompt(q, kv, bp, seq, _masked_buckets(WINDOW - offsets), tab, sinks)
    kv_s = kv[n_p:n_p + n_s].reshape(steps, bs, 2, KV_DIM).transpose(2, 1, 0, 3)
    dist_s = jnp.arange(steps)[:, None] + WINDOW - jnp.arange(WINDOW + steps)[None, :]
    q_s = q[n_p:n_p + n_s].reshape(steps, bs, D_MODEL).transpose(1, 0, 2)
    o_s, k_win_s, v_win_s = _attn_sample(
        q_s, cache_k_win.reshape(bs, WINDOW, KV_DIM), cache_v_win.reshape(bs, WINDOW, KV_DIM),
        kv_s[0], kv_s[1], _masked_buckets(dist_s), tab, sinks)
    o_s = pad_tile(o_s.transpose(1, 0, 2).reshape(n_s, D_MODEL))
    x3, x3_rows, e_idx, gates = _proj_ln((o_p, o_s), attn_w_o, (0,), jnp.zeros((1, D_MODEL), F32), x2,
                                vec(ln_g[1, 0]), vec(ln_b[1, 0]), wr_t, br, name="attn_out_ln")
    ys = _moe_block(x3_rows, e_idx, moe_w_gate, moe_w_up, moe_w_down, 1)
    y_p, y_s = _combine_split(ys, x3, gates.T, vec(ln_g[1, 1]), vec(ln_b[1, 1]), name="moe_combine_1")

    y_prompt = y_p.reshape(bp, seq, D_MODEL)
    y_sample = y_s[:n_s].reshape(steps, bs, D_MODEL).transpose(1, 0, 2)
    kv_p = jnp.stack([kv[(b + 1) * seq - WINDOW:(b + 1) * seq] for b in range(bp)])
    kv_p = kv_p.reshape(bp, WINDOW, 2, N_KV_HEADS, HEAD_DIM)
    k_win_s = k_win_s.reshape(bs, WINDOW, N_KV_HEADS, HEAD_DIM)
    v_win_s = v_win_s.reshape(bs, WINDOW, N_KV_HEADS, HEAD_DIM)
    return (y_prompt, y_sample,
            conv_p[None], rnn_p.reshape(1, bp, D_MODEL),
            kv_p[:, :, 0], kv_p[:, :, 1],
            conv_s.transpose(1, 0, 2)[None], rnn_s[None],
            k_win_s, v_win_s)
```

```python
import functools
import math

import jax
import jax.numpy as jnp
from jax import lax
from jax.experimental import pallas as pl
from jax.experimental.pallas import tpu as pltpu

D_MODEL = 2048
DEPTH = 2
LRU_BLOCKS = 8
LRU_BLOCK = D_MODEL // LRU_BLOCKS
CONV_WIDTH = 4
LRU_C = 8.0
N_HEADS = 32
HEAD_DIM = 64
N_KV_HEADS = 8
GROUP = N_HEADS // N_KV_HEADS
KV_DIM = N_KV_HEADS * HEAD_DIM
WINDOW = 128
N_BUCKETS = 32
MAX_DISTANCE = 128
N_EXPERTS = 16
N_GROUPS = 4
EXPERTS_PER_GROUP = N_EXPERTS // N_GROUPS
D_EXPERT = 1024
ALPHA = (2 * DEPTH) ** 0.25
LN_EPS = 1e-5

LANES = 128
SEGS = 8
CHUNKS = D_MODEL // LANES
ROW_SUBLANES = D_MODEL // (2 * LANES)
MOE_TILE = 256
TOK_TILE = 256
CHUNK_TOK_ROWS = CHUNKS + 4
CHUNK_SEG_TOKS = TOK_TILE // SEGS
CHUNK_SEG_ROWS = CHUNK_SEG_TOKS * CHUNK_TOK_ROWS + 4
CHUNK_TILE_ROWS = SEGS * CHUNK_SEG_ROWS
DMA_UNROLL = 8
SCALAR_UNROLL = 32
PAIR_UNROLL = 16
PLAN_UNROLL = 4
PAIR_BLOCK = 4
GATE_BLOCKS = 4
W_CHUNKS = 4
CAST_ROWS = 256
BF16 = jnp.bfloat16
F32 = jnp.float32
NEG_INF = float("-inf")


def _params(sem, vmem_mb):
    return pltpu.CompilerParams(dimension_semantics=sem, vmem_limit_bytes=vmem_mb * 1024 * 1024)


def _cast_rows(src_ref, dst_ref):
    n = src_ref.shape[0] // CAST_ROWS

    def body(i, c):
        r = pl.multiple_of(i * CAST_ROWS, CAST_ROWS)
        dst_ref[pl.ds(r, CAST_ROWS), :] = src_ref[pl.ds(r, CAST_ROWS), :].astype(BF16)
        return c

    lax.fori_loop(0, n, body, 0)


def _layer_norm(z, g, b):
    mu = jnp.mean(z, axis=-1, keepdims=True)
    zc = z - mu
    var = jnp.mean(zc * zc, axis=-1, keepdims=True)
    return zc * lax.rsqrt(var + LN_EPS) * g + b


def _chunk_row(tok, chunk):
    seg, t = divmod(tok, CHUNK_SEG_TOKS)
    return seg * CHUNK_SEG_ROWS + t * CHUNK_TOK_ROWS + chunk


def _chunk_rows_store(ref, y):
    ref[...] = jnp.zeros_like(ref)
    for seg in range(SEGS):
        rows = slice(seg * CHUNK_SEG_TOKS, (seg + 1) * CHUNK_SEG_TOKS)
        for c in range(CHUNKS):
            dst = pl.ds(_chunk_row(seg * CHUNK_SEG_TOKS, c), CHUNK_SEG_TOKS, stride=CHUNK_TOK_ROWS)
            ref[dst, :] = y[rows, c * LANES:(c + 1) * LANES]


def _chunk_rows_load(ref, tok0, count, chunks=range(CHUNKS)):
    assert tok0 // CHUNK_SEG_TOKS == (tok0 + count - 1) // CHUNK_SEG_TOKS
    return jnp.concatenate(
        [ref[pl.ds(_chunk_row(tok0, c), count, stride=CHUNK_TOK_ROWS), :] for c in chunks], axis=1)


def _store_packed_rows(x_bf, rows_ref):
    n = x_bf.shape[0]
    bits = pltpu.bitcast(x_bf.astype(F32), jnp.uint32)
    packed = bits[:, D_MODEL // 2:] | (bits[:, :D_MODEL // 2] >> 16)
    for c in range(ROW_SUBLANES):
        rows_ref[pl.ds(c, n, stride=ROW_SUBLANES), :] = packed[:, c * LANES:(c + 1) * LANES]


def _load_packed_rows(rows_ref, x_bf_ref):
    n = x_bf_ref.shape[0]
    for c in range(ROW_SUBLANES):
        words = rows_ref[pl.ds(c, n, stride=ROW_SUBLANES), :]
        low = pltpu.bitcast(words << 16, F32).astype(BF16)
        high = pltpu.bitcast(words & jnp.uint32(0xFFFF0000), F32).astype(BF16)
        x_bf_ref[:, c * LANES:(c + 1) * LANES] = low
        x_bf_ref[:, D_MODEL // 2 + c * LANES:D_MODEL // 2 + (c + 1) * LANES] = high


def _tok_operands(x, tile_of=lambda i: i):
    if isinstance(x, tuple):
        xp, xs = x
        d = xp.shape[1]
        last_p = xp.shape[0] // TOK_TILE - 1
        specs = [pl.BlockSpec((TOK_TILE, d), lambda i, *_: (jnp.minimum(tile_of(i), last_p), 0)),
                 pl.BlockSpec((TOK_TILE, d), lambda i, *_: (0, 0))]
        return [xp, xs], specs, last_p + 2
    return ([x], [pl.BlockSpec((TOK_TILE, x.shape[1]), lambda i, *_: (tile_of(i), 0))],
            x.shape[0] // TOK_TILE)


def _tok_load(refs, is_sample=None):
    if len(refs) == 1:
        return refs[0][...]
    if is_sample is None:
        is_sample = pl.program_id(0) == pl.num_programs(0) - 1
    return jnp.where(is_sample, refs[1][...], refs[0][...])


def _lru_in_kernel(*refs, n_x, layer):
    x_refs = refs[:n_x]
    wx_hbm, wy_hbm, bx_ref, by_ref, xb_ref, yb_ref, wx_bf, wy_bf, stage, sems = refs[n_x:]

    @pl.when(pl.program_id(0) == 0)
    def _():
        _load_weight(wx_hbm.at[layer], wx_bf, stage, sems)
        _load_weight(wy_hbm.at[layer], wy_bf, stage, sems)

    x = _tok_load(x_refs).astype(BF16)
    _chunk_rows_store(xb_ref, jnp.dot(x, wx_bf[...], preferred_element_type=F32) + bx_ref[...])
    y = jnp.dot(x, wy_bf[...], preferred_element_type=F32) + by_ref[...]
    yb_ref[...] = jax.nn.gelu(y).astype(BF16)


def _lru_in(x, w_x, b_x, w_y, b_y, layer):
    arrays, specs, nt = _tok_operands(x)
    hbm = pl.BlockSpec(memory_space=pl.ANY)
    vec_spec = pl.BlockSpec((1, D_MODEL), lambda i: (0, 0))
    return pl.pallas_call(
        functools.partial(_lru_in_kernel, n_x=len(arrays), layer=layer),
        grid=(nt,),
        in_specs=specs + [hbm, hbm, vec_spec, vec_spec],
        out_specs=[pl.BlockSpec((CHUNK_TILE_ROWS, LANES), lambda i: (i, 0)),
                   pl.BlockSpec((TOK_TILE, D_MODEL), lambda i: (i, 0))],
        out_shape=[jax.ShapeDtypeStruct((nt * CHUNK_TILE_ROWS, LANES), F32),
                   jax.ShapeDtypeStruct((nt * TOK_TILE, D_MODEL), BF16)],
        scratch_shapes=[pltpu.VMEM((D_MODEL, D_MODEL), BF16), pltpu.VMEM((D_MODEL, D_MODEL), BF16),
                        pltpu.VMEM((2, CAST_ROWS, D_MODEL), F32), pltpu.SemaphoreType.DMA((2,))],
        compiler_params=_params(("arbitrary",), 48),
        name="lru_in",
    )(*arrays, w_x, w_y, b_x, b_y)


def _route(logits_t, b_router):
    aff = jax.nn.sigmoid(logits_t)
    sel = aff + b_router
    srow = [sel[e:e + 1, :] for e in range(N_EXPERTS)]
    arow = [aff[e:e + 1, :] for e in range(N_EXPERTS)]

    def top2_sum(v):
        pairs = [v[i] + v[j] for i in range(4) for j in range(i + 1, 4)]
        return functools.reduce(jnp.maximum, pairs)

    scores = [top2_sum(srow[4 * g:4 * g + 4]) for g in range(N_GROUPS)]
    best = scores[0]
    gi = jnp.zeros_like(best, dtype=jnp.int32)
    for g in range(1, N_GROUPS):
        upd = scores[g] > best
        best = jnp.where(upd, scores[g], best)
        gi = jnp.where(upd, g, gi)

    def pick_group(rows, j):
        out = rows[j]
        for g in range(1, N_GROUPS):
            out = jnp.where(gi == g, rows[4 * g + j], out)
        return out

    v = [pick_group(srow, j) for j in range(EXPERTS_PER_GROUP)]
    a = [pick_group(arow, j) for j in range(EXPERTS_PER_GROUP)]

    m1, i1 = v[0], jnp.zeros_like(gi)
    for j in range(1, EXPERTS_PER_GROUP):
        upd = v[j] > m1
        m1 = jnp.where(upd, v[j], m1)
        i1 = jnp.where(upd, j, i1)
    m2 = jnp.full_like(m1, NEG_INF)
    i2 = jnp.zeros_like(gi)
    for j in range(EXPERTS_PER_GROUP):
        cand = jnp.where(i1 == j, NEG_INF, v[j])
        upd = cand > m2
        m2 = jnp.where(upd, cand, m2)
        i2 = jnp.where(upd, j, i2)

    def pick_idx(rows, idx):
        out = rows[0]
        for j in range(1, EXPERTS_PER_GROUP):
            out = jnp.where(idx == j, rows[j], out)
        return out

    a1 = pick_idx(a, i1)
    a2 = pick_idx(a, i2)
    tot = a1 + a2
    e_idx = jnp.concatenate([gi * EXPERTS_PER_GROUP + i1, gi * EXPERTS_PER_GROUP + i2], axis=0)
    gates = jnp.concatenate([a1 / tot, a2 / tot], axis=0)
    return e_idx, gates


def _proj_ln_kernel(*refs, n_m, n_res):
    m_refs = refs[:n_m]
    w_ref, b_ref = refs[n_m:n_m + 2]
    res_refs = refs[n_m + 2:n_m + 2 + n_res]
    (g_ref, beta_ref, wr_ref, br_ref, x_ref, xrow_ref, e_ref, gate_ref,
     wbf_ref, ya, yb) = refs[n_m + 2 + n_res:]
    i = pl.program_id(0)
    n_tiles = pl.num_programs(0) - 1

    @pl.when(i == 0)
    def _():
        _cast_rows(w_ref, wbf_ref)
        yb[...] = jnp.zeros_like(yb)

    for parity, (cur, prev) in enumerate(((ya, yb), (yb, ya))):
        @pl.when(i % 2 == parity)
        def _():
            cur[...] = jnp.dot(_tok_load(m_refs, i >= n_tiles - 1), wbf_ref[...],
                               preferred_element_type=F32)
            y = prev[...] + b_ref[...]
            x = _layer_norm(ALPHA * _tok_load(res_refs, i == n_tiles) + y, g_ref[...], beta_ref[...])
            x_ref[...] = x
            x_bf = x.astype(BF16)
            _store_packed_rows(x_bf, xrow_ref)
            logits_t = lax.dot_general(wr_ref[...].astype(BF16), x_bf,
                                       (((1,), (1,)), ((), ())), preferred_element_type=F32)
            e_idx, gates = _route(logits_t, br_ref[...])
            e_ref[...] = e_idx
            gate_ref[...] = gates


def _proj_ln(m, w, w_index, b, res, g, beta, wr_t, br, *, name):
    nt = _tok_operands(m)[2]
    m_arrays, m_specs, _ = _tok_operands(m, lambda i: jnp.minimum(i, nt - 1))
    res_arrays, res_specs, _ = _tok_operands(res, lambda i: jnp.maximum(i - 1, 0))
    k = w.shape[-2]
    tm = TOK_TILE
    n = nt * tm
    row = lambda i: (jnp.maximum(i - 1, 0), 0)
    const = lambda i: (0, 0)
    x, x_rows, e_idx, gates = pl.pallas_call(
        functools.partial(_proj_ln_kernel, n_m=len(m_arrays), n_res=len(res_arrays)),
        grid=(nt + 1,),
        in_specs=m_specs + [
            pl.BlockSpec((None,) * len(w_index) + (k, D_MODEL), lambda i: w_index + (0, 0),
                         pipeline_mode=pl.Buffered(1)),
            pl.BlockSpec((1, D_MODEL), const),
        ] + res_specs + [
            pl.BlockSpec((1, D_MODEL), const),
            pl.BlockSpec((1, D_MODEL), const),
            pl.BlockSpec((N_EXPERTS, D_MODEL), const),
            pl.BlockSpec((N_EXPERTS, 1), const),
        ],
        out_specs=[
            pl.BlockSpec((tm, D_MODEL), row),
            pl.BlockSpec((tm * ROW_SUBLANES, LANES), row),
            pl.BlockSpec((None, 2, tm), lambda i: (jnp.maximum(i - 1, 0), 0, 0)),
            pl.BlockSpec((None, 2, tm), lambda i: (jnp.maximum(i - 1, 0), 0, 0)),
        ],
        out_shape=[
            jax.ShapeDtypeStruct((n, D_MODEL), F32),
            jax.ShapeDtypeStruct((n * ROW_SUBLANES, LANES), jnp.uint32),
            jax.ShapeDtypeStruct((nt, 2, tm), jnp.int32),
            jax.ShapeDtypeStruct((nt, 2, tm), F32),
        ],
        scratch_shapes=[pltpu.VMEM((k, D_MODEL), BF16),
                        pltpu.VMEM((tm, D_MODEL), F32), pltpu.VMEM((tm, D_MODEL), F32)],
        compiler_params=_params(("arbitrary",), 52),
        name=name,
    )(*m_arrays, w, b, *res_arrays, g, beta, wr_t, br)
    e_idx = e_idx.transpose(1, 0, 2).reshape(2, n)
    gates = gates.transpose(1, 0, 2).reshape(2, n)
    return x, x_rows, e_idx, gates


def _plan_kernel(e_ref, pos_ref, meta_ref, rank_ref):
    nrow = e_ref.shape[0]
    ri = lax.broadcasted_iota(jnp.int32, (LANES, LANES), 0)
    ci = lax.broadcasted_iota(jnp.int32, (LANES, LANES), 1)
    tri = jnp.where(ri <= ci, 1.0, 0.0).astype(BF16)
    sub = lax.broadcasted_iota(jnp.int32, (N_EXPERTS, LANES), 0)

    def count_body(b, base):
        rows = [b * PLAN_UNROLL + u for u in range(PLAN_UNROLL)]
        onehots = [sub == e_ref[pl.ds(r, 1), :] for r in rows]
        locs = [jnp.dot(jnp.where(oh, 1.0, 0.0).astype(BF16), tri, preferred_element_type=F32)
                for oh in onehots]
        for r, onehot, loc in zip(rows, onehots, locs):
            rank_ref[pl.ds(r, 1), :] = jnp.sum(jnp.where(onehot, base + loc - 1.0, 0.0),
                                               axis=0, keepdims=True)
            base = base + jnp.broadcast_to(loc[:, LANES - 1:LANES], (N_EXPERTS, LANES))
        return base

    count = lax.fori_loop(0, nrow // PLAN_UNROLL, count_body, jnp.zeros((N_EXPERTS, LANES), F32))
    ntile = jnp.floor((count + (MOE_TILE - 1.0)) * (1.0 / MOE_TILE))
    offs = []
    acc = jnp.zeros((1, LANES), F32)
    for e in range(N_EXPERTS):
        offs.append(acc)
        acc = acc + ntile[e:e + 1, :]
    tile_off = jnp.concatenate(offs, axis=0)
    tile_end = tile_off + ntile
    lane = lax.broadcasted_iota(jnp.int32, (N_EXPERTS, LANES), 1).astype(F32)
    tile_expert = jnp.sum(jnp.where(tile_end <= lane, 1.0, 0.0), axis=0, keepdims=True)
    tile_expert = jnp.minimum(tile_expert, N_EXPERTS - 1.0)
    own = jnp.logical_and(tile_off <= lane, lane < tile_end)
    run_end = jnp.sum(jnp.where(own, tile_end, 0.0), axis=0, keepdims=True)
    next_expert = jnp.sum(jnp.where(tile_end <= run_end, 1.0, 0.0), axis=0, keepdims=True)
    has_next = jnp.logical_and(lane[0:1, :] < acc, run_end < acc)
    next_expert = jnp.where(has_next, next_expert, -1.0)
    meta = jnp.concatenate([tile_expert, acc, next_expert, jnp.zeros((5, LANES), F32)], axis=0)
    meta_ref[...] = meta.astype(jnp.int32)
    row_off = tile_off * float(MOE_TILE)

    def pos_body(r, c):
        onehot = sub == e_ref[pl.ds(r, 1), :]
        p = jnp.sum(jnp.where(onehot, row_off, 0.0), axis=0, keepdims=True) + rank_ref[pl.ds(r, 1), :]
        pos_ref[pl.ds(r, 1), :] = p.astype(jnp.int32)
        return c

    lax.fori_loop(0, nrow, pos_body, 0)


def _plan(e_idx, *, name):
    n2 = e_idx.shape[0] * e_idx.shape[1]
    assert n2 % (LANES * PLAN_UNROLL) == 0
    e2d = e_idx.reshape(n2 // LANES, LANES)
    pos, meta = pl.pallas_call(
        _plan_kernel,
        out_shape=[jax.ShapeDtypeStruct(e2d.shape, jnp.int32),
                   jax.ShapeDtypeStruct((8, LANES), jnp.int32)],
        scratch_shapes=[pltpu.VMEM(e2d.shape, F32)],
        name=name,
    )(e2d)
    return pos.reshape(n2), meta[0], meta[1, :1], meta[2]


def _invert_kernel(pos_ref, te_ref, nu_ref, pair_ref):
    n_rows = pair_ref.shape[0]
    n_pairs = pos_ref.shape[0]
    last_used = nu_ref[0] - 1

    def fill_body(b, c):
        tile = (b * SCALAR_UNROLL) // MOE_TILE

        @pl.when(jnp.logical_or(tile >= last_used, te_ref[tile + 1] != te_ref[tile]))
        def _():
            for u in range(SCALAR_UNROLL):
                pair_ref[b * SCALAR_UNROLL + u] = -1
        return c

    def pair_body(b, c):
        rows = [pos_ref[b * PAIR_UNROLL + u] for u in range(PAIR_UNROLL)]
        for u in range(PAIR_UNROLL):
            pair_ref[rows[u]] = b * PAIR_UNROLL + u
        return c

    lax.fori_loop(0, n_rows // SCALAR_UNROLL, fill_body, 0)
    lax.fori_loop(0, n_pairs // PAIR_UNROLL, pair_body, 0)


def _invert(pos, tile_expert, n_used, n_rows, *, name):
    return pl.pallas_call(
        _invert_kernel,
        grid_spec=pltpu.PrefetchScalarGridSpec(
            num_scalar_prefetch=3,
            grid=(1,),
            in_specs=[],
            out_specs=pl.BlockSpec(memory_space=pltpu.SMEM),
        ),
        out_shape=jax.ShapeDtypeStruct((n_rows,), jnp.int32),
        name=name,
    )(pos, tile_expert, n_used)


def _expert_changed(te_ref, i):
    return jnp.logical_or(i == 0, te_ref[i] != te_ref[jnp.maximum(i - 1, 0)])


class _ExpertWeights:
    def __init__(self, mats, layer, st, sems):
        self.mats, self.layer, self.st, self.sems = mats, layer, st, sems

    def _copies(self, expert, c):
        out = []
        for w_hbm, stage, _, _ in self.mats:
            rows = stage.shape[1]
            src = w_hbm.at[self.layer, expert, pl.ds(pl.multiple_of(c * rows, rows), rows), :]
            out.append(pltpu.make_async_copy(src, stage.at[c % 2], self.sems.at[c % 2]))
        return out

    def _start(self, expert, c):
        for cp in self._copies(expert, c):
            cp.start()
        self.st[2] = c + 1

    def _convert(self, c_src, c_dst):
        for _, stage, w_next, _ in self.mats:
            rows = stage.shape[1]
            dst = pl.ds(pl.multiple_of(c_dst * rows, rows), rows)
            w_next[dst, :] = stage[c_src % 2].astype(BF16)

    def reset(self):
        for _, stage, _, _ in self.mats:
            stage[...] = jnp.zeros_like(stage)
        self.st[1] = 0
        self.st[2] = 0

    def switch_to(self, expert):
        st = self.st

        def body(c, carry):
            @pl.when(c >= st[2])
            def _():
                self._start(expert, c)

            @pl.when(jnp.logical_and(c + 1 < W_CHUNKS, c + 1 >= st[2]))
            def _():
                self._start(expert, c + 1)

            for cp in self._copies(expert, c):
                cp.wait()
            self._convert(c, c)
            return carry

        lax.fori_loop(st[1], W_CHUNKS, body, 0)
        for _, _, w_next, w_cur in self.mats:
            _copy_rows(w_next, w_cur)
        st[1] = 0
        st[2] = 0

    def begin_step(self, next_expert):
        st = self.st
        done, issued = st[1], st[2]
        has_next = next_expert >= 0
        active = jnp.logical_and(has_next, done < issued)

        @pl.when(jnp.logical_and(has_next, jnp.logical_and(issued < W_CHUNKS, issued < done + 2)))
        def _():
            self._start(next_expert, issued)

        @pl.when(active)
        def _():
            for cp in self._copies(next_expert, done):
                cp.wait()

        return active, done

    def convert_step(self, active, done):
        self._convert(jnp.where(active, done, done + 1), jnp.where(active, done, W_CHUNKS))

    def end_step(self, active, done):
        @pl.when(active)
        def _():
            self.st[1] = done + 1


def _copy_rows(src_ref, dst_ref):
    n = dst_ref.shape[0] // CAST_ROWS

    def body(i, c):
        r = pl.multiple_of(i * CAST_ROWS, CAST_ROWS)
        dst_ref[pl.ds(r, CAST_ROWS), :] = src_ref[pl.ds(r, CAST_ROWS), :]
        return c

    lax.fori_loop(0, n, body, 0)


def _moe_tile_kernel(pair_ref, te_ref, nu_ref, nxt_ref, x_hbm, wg_hbm, wu_hbm, wd_hbm, out_hbm,
                     wg_cur, wg_next, wg_stage, wu_cur, wu_next, wu_stage, wd_cur, wd_next, wd_stage,
                     x_rows, x_bf, ya, yb, gsem, ssems, tsem, wsems, st, *, n_tok, n_tiles, layer):
    i = pl.program_id(0)
    nu = nu_ref[0]
    running = i < nu
    trash = 2 * n_tok
    weights = _ExpertWeights([(wg_hbm, wg_stage, wg_next, wg_cur), (wu_hbm, wu_stage, wu_next, wu_cur),
                              (wd_hbm, wd_stage, wd_next, wd_cur)], layer, st, wsems)

    def gather_copy(tile, r):
        p = pair_ref[tile * MOE_TILE + r]
        tok = jnp.where(p >= n_tok, p - n_tok, jnp.maximum(p, 0))
        src = pl.ds(pl.multiple_of(tok * ROW_SUBLANES, ROW_SUBLANES), ROW_SUBLANES)
        return pltpu.make_async_copy(x_hbm.at[src, :], x_rows.at[pl.ds(r * ROW_SUBLANES, ROW_SUBLANES), :],
                                     gsem)

    def scatter_copy(tile, r, buf, sem):
        p = pair_ref[tile * MOE_TILE + r]
        dst = jnp.where(p < 0, trash + r, p)
        return pltpu.make_async_copy(buf.at[pl.ds(r, 1), :], out_hbm.at[pl.ds(dst, 1), :], sem)

    def wait_scatter(buf, sem):
        pltpu.make_async_copy(buf, out_hbm.at[pl.ds(0, MOE_TILE), :], sem).wait()

    @pl.when(i == 0)
    def _():
        def body(rb, c):
            for u in range(DMA_UNROLL):
                gather_copy(0, rb * DMA_UNROLL + u).start()
            return c
        lax.fori_loop(0, MOE_TILE // DMA_UNROLL, body, 0)
        yb[...] = jnp.zeros_like(yb)
        fill = pltpu.make_async_copy(yb, out_hbm.at[pl.ds(trash, MOE_TILE), :], tsem)
        fill.start()
        fill.wait()
        weights.reset()

    @pl.when(i <= nu)
    def _():
        pltpu.make_async_copy(x_hbm.at[pl.ds(0, MOE_TILE * ROW_SUBLANES), :], x_rows, gsem).wait()

    @pl.when(jnp.logical_and(running, _expert_changed(te_ref, i)))
    def _():
        weights.switch_to(te_ref[i])

    active, done = weights.begin_step(jnp.where(running, nxt_ref[i], -1))

    for parity, (cur, prev) in enumerate(((ya, yb), (yb, ya))):
        cur_sem, prev_sem = ssems.at[parity], ssems.at[1 - parity]
        mine = i % 2 == parity

        @pl.when(jnp.logical_and(mine, jnp.logical_and(i >= 1, i - 1 <= nu)))
        def _():
            wait_scatter(cur, cur_sem)

        @pl.when(jnp.logical_and(mine, running))
        def _():
            _load_packed_rows(x_rows, x_bf)
            next_tile = jnp.minimum(i + 1, n_tiles - 1)
            prev_tile = jnp.maximum(i - 1, 0)
            for r in range(MOE_TILE):
                gather_copy(next_tile, r).start(priority=r % 2)
            for r in range(MOE_TILE):
                scatter_copy(prev_tile, r, prev, prev_sem).start(priority=r % 2)
            weights.convert_step(active, done)
            x = x_bf[...]
            a = jnp.dot(x, wg_cur[...], preferred_element_type=F32)
            b = jnp.dot(x, wu_cur[...], preferred_element_type=F32)
            h = (jax.nn.silu(a) * b).astype(BF16)
            cur[...] = jnp.dot(h, wd_cur[...], preferred_element_type=F32)

        @pl.when(jnp.logical_and(mine, i == nu))
        def _():
            def body(rb, c):
                for u in range(DMA_UNROLL):
                    scatter_copy(i - 1, rb * DMA_UNROLL + u, prev, prev_sem).start()
                return c
            lax.fori_loop(0, MOE_TILE // DMA_UNROLL, body, 0)

            @pl.when(i == n_tiles)
            def _():
                wait_scatter(prev, prev_sem)

    weights.end_step(active, done)


def _moe_tiles(x, pair, tile_expert, n_used, next_expert, w_gate, w_up, w_down, layer):
    n_tok = x.shape[0] // ROW_SUBLANES
    n_tiles = pair.shape[0] // MOE_TILE
    hbm = pl.BlockSpec(memory_space=pl.ANY)

    def weight_bufs(k, n):
        chunk = k // W_CHUNKS
        return [pltpu.VMEM((k, n), BF16), pltpu.VMEM((k + chunk, n), BF16), pltpu.VMEM((2, chunk, n), F32)]

    return pl.pallas_call(
        functools.partial(_moe_tile_kernel, n_tok=n_tok, n_tiles=n_tiles, layer=layer),
        grid_spec=pltpu.PrefetchScalarGridSpec(
            num_scalar_prefetch=4,
            grid=(n_tiles + 1,),
            in_specs=[hbm, hbm, hbm, hbm],
            out_specs=hbm,
            scratch_shapes=weight_bufs(D_MODEL, D_EXPERT) + weight_bufs(D_MODEL, D_EXPERT)
            + weight_bufs(D_EXPERT, D_MODEL) + [
                pltpu.VMEM((MOE_TILE * ROW_SUBLANES, LANES), jnp.uint32),
                pltpu.VMEM((MOE_TILE, D_MODEL), BF16),
                pltpu.VMEM((MOE_TILE, D_MODEL), F32), pltpu.VMEM((MOE_TILE, D_MODEL), F32),
                pltpu.SemaphoreType.DMA(()), pltpu.SemaphoreType.DMA((2,)),
                pltpu.SemaphoreType.DMA(()), pltpu.SemaphoreType.DMA((2,)),
                pltpu.SMEM((4,), jnp.int32)],
        ),
        out_shape=jax.ShapeDtypeStruct((2 * n_tok + MOE_TILE, D_MODEL), F32),
        compiler_params=_params(("arbitrary",), 56),
        name=f"moe_tiles_{layer}",
    )(pair, tile_expert, n_used, next_expert, x, w_gate, w_up, w_down)


def _combine(y0_ref, y1_ref, res_ref, gate_ref, g_ref, beta_ref):
    gate = gate_ref[...]
    ffn = gate[:, 0:1] * y0_ref[...] + gate[:, 1:2] * y1_ref[...]
    return _layer_norm(ALPHA * res_ref[...] + ffn, g_ref[...], beta_ref[...])


def _combine_specs(n):
    nt = n // TOK_TILE
    row = lambda i: (i, 0)
    const = lambda i: (0, 0)
    return [pl.BlockSpec((TOK_TILE, D_MODEL), row),
            pl.BlockSpec((TOK_TILE, D_MODEL), lambda i: (i + nt, 0)),
            pl.BlockSpec((TOK_TILE, D_MODEL), row),
            pl.BlockSpec((TOK_TILE, 2), row),
            pl.BlockSpec((1, D_MODEL), const),
            pl.BlockSpec((1, D_MODEL), const)]


def _combine_split_kernel(y0_ref, y1_ref, res_ref, gate_ref, g_ref, beta_ref, prompt_ref, sample_ref):
    x = _combine(y0_ref, y1_ref, res_ref, gate_ref, g_ref, beta_ref)
    is_sample = pl.program_id(0) == pl.num_programs(0) - 1

    @pl.when(jnp.logical_not(is_sample))
    def _():
        prompt_ref[...] = x

    @pl.when(is_sample)
    def _():
        sample_ref[...] = x


def _combine_split(ys, res, gates_col, g, beta, *, name):
    n = res.shape[0]
    tm = TOK_TILE
    nt = n // tm
    return pl.pallas_call(
        _combine_split_kernel,
        grid=(nt,),
        in_specs=_combine_specs(n),
        out_specs=[pl.BlockSpec((tm, D_MODEL), lambda i: (jnp.minimum(i, nt - 2), 0)),
                   pl.BlockSpec((tm, D_MODEL), lambda i: (0, 0))],
        out_shape=[jax.ShapeDtypeStruct((n - tm, D_MODEL), F32),
                   jax.ShapeDtypeStruct((tm, D_MODEL), F32)],
        compiler_params=_params(("arbitrary",), 40),
        name=name,
    )(ys, ys, res, gates_col, g, beta)


def _load_weight(w_hbm, wbf_ref, stage_ref, sems):
    rows = stage_ref.shape[1]
    n_chunks = wbf_ref.shape[0] // rows

    def chunk_copy(c):
        return pltpu.make_async_copy(w_hbm.at[pl.ds(c * rows, rows), :], stage_ref.at[c % 2],
                                     sems.at[c % 2])

    chunk_copy(0).start()
    for c in range(n_chunks):
        if c + 1 < n_chunks:
            chunk_copy(c + 1).start()
        chunk_copy(c).wait()
        wbf_ref[c * rows:(c + 1) * rows, :] = stage_ref[c % 2].astype(BF16)


def _combine_qkv_kernel(y0_ref, y1_ref, res_ref, gate_ref, g_ref, beta_ref, wq_hbm, wkv_hbm,
                        x_ref, q_ref, kv_ref, wq_bf, wkv_bf, stage_q, stage_kv, sems, *, wq_index):
    @pl.when(pl.program_id(0) == 0)
    def _():
        wq = wq_hbm
        for k in wq_index:
            wq = wq.at[k]
        _load_weight(wq, wq_bf, stage_q, sems)
        _load_weight(wkv_hbm, wkv_bf, stage_kv, sems)

    x = _combine(y0_ref, y1_ref, res_ref, gate_ref, g_ref, beta_ref)
    x_ref[...] = x
    x_bf = x.astype(BF16)
    q = jnp.dot(x_bf, wq_bf[...], preferred_element_type=F32) * (HEAD_DIM ** -0.5)
    q_ref[...] = q.astype(BF16)
    kv_ref[...] = jnp.dot(x_bf, wkv_bf[...], preferred_element_type=F32)


def _combine_qkv(ys, res, gates_col, g, beta, w_q, wq_index, w_kv, *, name):
    n = res.shape[0]
    tm = TOK_TILE
    row = lambda i: (i, 0)
    hbm = pl.BlockSpec(memory_space=pl.ANY)
    return pl.pallas_call(
        functools.partial(_combine_qkv_kernel, wq_index=wq_index),
        grid=(n // tm,),
        in_specs=_combine_specs(n) + [hbm, hbm],
        out_specs=[pl.BlockSpec((tm, D_MODEL), row),
                   pl.BlockSpec((tm, D_MODEL), row),
                   pl.BlockSpec((tm, 2 * KV_DIM), row)],
        out_shape=[jax.ShapeDtypeStruct((n, D_MODEL), F32),
                   jax.ShapeDtypeStruct((n, D_MODEL), BF16),
                   jax.ShapeDtypeStruct((n, 2 * KV_DIM), F32)],
        scratch_shapes=[pltpu.VMEM((D_MODEL, D_MODEL), BF16),
                        pltpu.VMEM((D_MODEL, 2 * KV_DIM), BF16),
                        pltpu.VMEM((2, CAST_ROWS, D_MODEL), F32),
                        pltpu.VMEM((2, CAST_ROWS, 2 * KV_DIM), F32),
                        pltpu.SemaphoreType.DMA((2,))],
        compiler_params=_params(("arbitrary",), 52),
        name=name,
    )(ys, ys, res, gates_col, g, beta, w_q, w_kv)


def _moe_block(x_rows, e_idx, w_gate, w_up, w_down, layer):
    n = x_rows.shape[0] // ROW_SUBLANES
    n_tiles = -(-(2 * n + N_EXPERTS * (MOE_TILE - 1)) // MOE_TILE)
    pos, tile_expert, n_used, next_expert = _plan(e_idx, name=f"moe_plan_{layer}")
    pair = _invert(pos, tile_expert, n_used, n_tiles * MOE_TILE, name=f"moe_invert_{layer}")
    return _moe_tiles(x_rows, pair, tile_expert, n_used, next_expert, w_gate, w_up, w_down, layer)


def _sigmoid_of_half(half_x):
    return 0.5 * jnp.tanh(half_x) + 0.5


def _log_sigmoid(x):
    return -(jnp.maximum(-x, 0.0) + jnp.log1p(jnp.exp(-jnp.abs(x))))


def _lru_gate_blocks(xcs, blocks, wrg_bf, wig_bf, brg_ref, big_ref, lam_ref):
    xbs = [xc.astype(BF16) for xc in xcs]
    r_lin = [jnp.dot(xb, wrg_bf[n], preferred_element_type=F32) for xb, n in zip(xbs, blocks)]
    i_lin = [jnp.dot(xb, wig_bf[n], preferred_element_type=F32) for xb, n in zip(xbs, blocks)]
    out = []
    for xc, n, rl, il in zip(xcs, blocks, r_lin, i_lin):
        cols = slice(n * LRU_BLOCK, (n + 1) * LRU_BLOCK)
        r = _sigmoid_of_half(rl + 0.5 * brg_ref[:, cols])
        i = _sigmoid_of_half(il + 0.5 * big_ref[:, cols])
        log_a = r * (LRU_C * _log_sigmoid(lam_ref[:, cols]))
        a = jnp.exp(log_a)
        u = xc * i * jnp.sqrt(-jnp.tanh(log_a) * (a * a + 1.0))
        out.append((a, u))
    return out


def _lru_gate_block(xc, n, wrg_bf, wig_bf, brg_ref, big_ref, lam_ref):
    return _lru_gate_blocks([xc], [n], wrg_bf, wig_bf, brg_ref, big_ref, lam_ref)[0]


def _cast_gate_weights(wrg_ref, wig_ref, wrg_bf, wig_bf):
    for n in range(LRU_BLOCKS):
        wrg_bf[n] = (0.5 * wrg_ref[n]).astype(BF16)
        wig_bf[n] = (0.5 * wig_ref[n]).astype(BF16)


def _lru_prompt_kernel(xb_ref, yb_ref, cw_ref, cb_ref, wrg_ref, wig_ref, brg_ref, big_ref, lam_ref,
                       m_ref, conv_ref, hlast_ref, xs, tail, a_s, u_s, hs_t, h_s, wrg_bf, wig_bf):
    b = pl.program_id(0)
    j = pl.program_id(1)
    tt = m_ref.shape[0]
    seg_len = tt // SEGS
    taps = CONV_WIDTH - 1
    head = SEGS * taps

    @pl.when(jnp.logical_and(b == 0, j == 0))
    def _():
        _cast_gate_weights(wrg_ref, wig_ref, wrg_bf, wig_bf)

    @pl.when(j == 0)
    def _():
        tail[...] = jnp.zeros_like(tail)
        h_s[...] = jnp.zeros_like(h_s)

    for q in range(seg_len):
        xs[head + SEGS * q:head + SEGS * (q + 1), :] = jnp.concatenate(
            [xb_ref[pl.ds(_chunk_row(q, c), SEGS, stride=CHUNK_SEG_ROWS), :] for c in range(CHUNKS)],
            axis=1)
    sub = lax.broadcasted_iota(jnp.int32, (SEGS, D_MODEL), 0)
    for k in range(taps):
        last = head + SEGS * (seg_len - taps + k)
        joined = jnp.where(sub == SEGS - 1, tail[SEGS * k:SEGS * (k + 1), :], xs[last:last + SEGS, :])
        xs[SEGS * k:SEGS * (k + 1), :] = pltpu.roll(joined, 1, axis=0)
    tail[...] = xs[head + SEGS * (seg_len - taps):head + SEGS * seg_len, :]

    for first_block in range(0, LRU_BLOCKS, GATE_BLOCKS):
        blocks = range(first_block, first_block + GATE_BLOCKS)
        xcs = []
        for n in blocks:
            cols = slice(n * LRU_BLOCK, (n + 1) * LRU_BLOCK)
            xc = cb_ref[:, cols] + cw_ref[0:1, cols] * xs[0:tt, cols]
            for k in range(1, CONV_WIDTH):
                xc = xc + cw_ref[k:k + 1, cols] * xs[SEGS * k:SEGS * k + tt, cols]
            xcs.append(xc)
        gates = _lru_gate_blocks(xcs, blocks, wrg_bf, wig_bf, brg_ref, big_ref, lam_ref)
        for n, (a, u) in zip(blocks, gates):
            cols = slice(n * LRU_BLOCK, (n + 1) * LRU_BLOCK)
            a_s[:, cols] = a
            u_s[:, cols] = u

    def scan_body(q, carry):
        h, prod = carry
        rows = pl.ds(pl.multiple_of(q * SEGS, SEGS), SEGS)
        a = a_s[rows, :]
        h = a * h + u_s[rows, :]
        prod = a * prod
        u_s[rows, :] = h
        a_s[rows, :] = prod
        return h, prod

    h_end, prod_end = lax.fori_loop(
        0, seg_len, scan_body,
        (jnp.zeros((SEGS, D_MODEL), F32), jnp.ones((SEGS, D_MODEL), F32)))
    state = h_s[...]
    entering = []
    for s in range(SEGS):
        entering.append(state)
        state = h_end[s:s + 1, :] + prod_end[s:s + 1, :] * state
    h_s[...] = state
    enter = jnp.concatenate(entering, axis=0)

    def fix_body(q, carry):
        rows = pl.ds(pl.multiple_of(q * SEGS, SEGS), SEGS)
        h = u_s[rows, :] + a_s[rows, :] * enter
        for c in range(CHUNKS):
            hs_t[pl.ds(q * CHUNK_TOK_ROWS + c, SEGS, stride=CHUNK_SEG_ROWS), :] = h[:, c * LANES:(c + 1) * LANES]
        return carry

    lax.fori_loop(0, seg_len, fix_body, 0)
    for s in range(SEGS):
        rows = slice(s * seg_len, (s + 1) * seg_len)
        hs = _chunk_rows_load(hs_t, s * seg_len, seg_len)
        m_ref[rows, :] = (hs * yb_ref[rows, :].astype(F32)).astype(BF16)

    @pl.when(j == pl.num_programs(1) - 1)
    def _():
        for k in range(taps):
            conv_ref[k:k + 1, :] = tail[SEGS * k + SEGS - 1:SEGS * (k + 1), :]
        hlast_ref[...] = state


def _lru_prompt(xb, yb, batch, seq, cw, cb, wrg, wig, brg, big, lam):
    tt = TOK_TILE
    nj = seq // tt
    row = lambda b, j: (b * nj + j, 0)
    const2 = lambda b, j: (0, 0)
    const3 = lambda b, j: (0, 0, 0)
    return pl.pallas_call(
        _lru_prompt_kernel,
        grid=(batch, nj),
        in_specs=[
            pl.BlockSpec((CHUNK_TILE_ROWS, LANES), row),
            pl.BlockSpec((tt, D_MODEL), row),
            pl.BlockSpec((CONV_WIDTH, D_MODEL), const2),
            pl.BlockSpec((1, D_MODEL), const2),
            pl.BlockSpec((LRU_BLOCKS, LRU_BLOCK, LRU_BLOCK), const3),
            pl.BlockSpec((LRU_BLOCKS, LRU_BLOCK, LRU_BLOCK), const3),
            pl.BlockSpec((1, D_MODEL), const2),
            pl.BlockSpec((1, D_MODEL), const2),
            pl.BlockSpec((1, D_MODEL), const2),
        ],
        out_specs=[
            pl.BlockSpec((tt, D_MODEL), row),
            pl.BlockSpec((None, CONV_WIDTH - 1, D_MODEL), lambda b, j: (b, 0, 0)),
            pl.BlockSpec((None, 1, D_MODEL), lambda b, j: (b, 0, 0)),
        ],
        out_shape=[
            jax.ShapeDtypeStruct((batch * seq, D_MODEL), BF16),
            jax.ShapeDtypeStruct((batch, CONV_WIDTH - 1, D_MODEL), F32),
            jax.ShapeDtypeStruct((batch, 1, D_MODEL), F32),
        ],
        scratch_shapes=[
            pltpu.VMEM((tt + SEGS * (CONV_WIDTH - 1), D_MODEL), F32),
            pltpu.VMEM((SEGS * (CONV_WIDTH - 1), D_MODEL), F32),
            pltpu.VMEM((tt, D_MODEL), F32),
            pltpu.VMEM((tt, D_MODEL), F32),
            pltpu.VMEM((CHUNK_TILE_ROWS, LANES), F32),
            pltpu.VMEM((1, D_MODEL), F32),
            pltpu.VMEM((LRU_BLOCKS, LRU_BLOCK, LRU_BLOCK), BF16),
            pltpu.VMEM((LRU_BLOCKS, LRU_BLOCK, LRU_BLOCK), BF16),
        ],
        compiler_params=_params(("arbitrary", "arbitrary"), 40),
        name="lru_prompt",
    )(xb, yb, cw, cb, wrg, wig, brg, big, lam)


def _lru_sample_kernel(xb_ref, yb_ref, cs_ref, h0_ref, cw_ref, cb_ref, wrg_ref, wig_ref,
                       brg_ref, big_ref, lam_ref, m_ref, conv_ref, hlast_ref, wrg_bf, wig_bf, *, steps):
    batch = h0_ref.shape[0]
    _cast_gate_weights(wrg_ref, wig_ref, wrg_bf, wig_bf)
    m_ref[steps * batch:, :] = jnp.zeros((m_ref.shape[0] - steps * batch, D_MODEL), BF16)

    def slab(t, cols):
        if t < CONV_WIDTH - 1:
            return cs_ref[t, :, cols]
        t -= CONV_WIDTH - 1
        first, stop, _ = cols.indices(D_MODEL)
        return _chunk_rows_load(xb_ref, t * batch, batch, range(first // LANES, stop // LANES))

    for n in range(LRU_BLOCKS):
        cols = slice(n * LRU_BLOCK, (n + 1) * LRU_BLOCK)
        h = h0_ref[:, cols]
        for t in range(steps):
            xc = cb_ref[:, cols] + cw_ref[0:1, cols] * slab(t, cols)
            for k in range(1, CONV_WIDTH):
                xc = xc + cw_ref[k:k + 1, cols] * slab(t + k, cols)
            a, u = _lru_gate_block(xc, n, wrg_bf, wig_bf, brg_ref, big_ref, lam_ref)
            h = a * h + u
            rows = slice(t * batch, (t + 1) * batch)
            m_ref[rows, cols] = (h * yb_ref[rows, cols].astype(F32)).astype(BF16)
        hlast_ref[:, cols] = h
    for k in range(CONV_WIDTH - 1):
        conv_ref[k] = slab(steps + k, slice(None))


def _lru_sample(xb, yb, tile, steps, conv_state, h0, cw, cb, wrg, wig, brg, big, lam):
    batch = h0.shape[0]
    tok = pl.BlockSpec((TOK_TILE, D_MODEL), lambda i: (tile, 0))
    tok_chunks = pl.BlockSpec((CHUNK_TILE_ROWS, LANES), lambda i: (tile, 0))
    full = lambda a: pl.BlockSpec(a.shape, lambda i: (0,) * a.ndim)
    small = (conv_state, h0, cw, cb, wrg, wig, brg, big, lam)
    return pl.pallas_call(
        functools.partial(_lru_sample_kernel, steps=steps),
        grid=(1,),
        in_specs=[tok_chunks, tok] + [full(a) for a in small],
        out_specs=[
            pl.BlockSpec((TOK_TILE, D_MODEL), lambda i: (0, 0)),
            pl.BlockSpec((CONV_WIDTH - 1, batch, D_MODEL), lambda i: (0, 0, 0)),
            pl.BlockSpec((batch, D_MODEL), lambda i: (0, 0)),
        ],
        out_shape=[
            jax.ShapeDtypeStruct((TOK_TILE, D_MODEL), BF16),
            jax.ShapeDtypeStruct((CONV_WIDTH - 1, batch, D_MODEL), F32),
            jax.ShapeDtypeStruct((batch, D_MODEL), F32),
        ],
        scratch_shapes=[
            pltpu.VMEM((LRU_BLOCKS, LRU_BLOCK, LRU_BLOCK), BF16),
            pltpu.VMEM((LRU_BLOCKS, LRU_BLOCK, LRU_BLOCK), BF16),
        ],
        compiler_params=_params(("arbitrary",), 32),
        name="lru_sample",
    )(xb, yb, *small)


def _rel_bucket(dist):
    n = jnp.maximum(dist, 0)
    max_exact = N_BUCKETS // 2
    nf = jnp.maximum(n, 1).astype(F32)
    large = max_exact + (jnp.log(nf / max_exact) / math.log(MAX_DISTANCE / max_exact)
                         * (N_BUCKETS - max_exact)).astype(jnp.int32)
    large = jnp.minimum(large, N_BUCKETS - 1)
    return jnp.where(n < max_exact, n, large)


def _masked_buckets(dist):
    valid = (dist >= 0) & (dist < WINDOW)
    return jnp.where(valid, _rel_bucket(dist), -1).astype(jnp.int32)


def _build_bias(bucket, tab_ref, head):
    def body(bi, acc):
        return jnp.where(bucket == bi, tab_ref[bi * N_HEADS + head], acc)
    return lax.fori_loop(0, N_BUCKETS, body, jnp.full(bucket.shape, NEG_INF, F32))


def _softmax_pv(s, sink, v):
    m = jnp.maximum(jnp.max(s, axis=-1, keepdims=True), sink)
    p = jnp.exp(s - m)
    den = jnp.sum(p, axis=-1, keepdims=True) + jnp.exp(sink - m)
    return jnp.dot(p.astype(BF16), v, preferred_element_type=F32) / den


def _attn_prompt_kernel(q_ref, kvp_ref, kvc_ref, bucket_ref, tab_ref, sink_ref, o_ref, bias_s):
    b = pl.program_id(0)
    n = pl.program_id(1)

    @pl.when(jnp.logical_and(b == 0, n == 0))
    def _():
        bucket = bucket_ref[...]

        col = lax.broadcasted_iota(jnp.int32, (WINDOW, 2 * WINDOW), 1)

        def head_body(h, c):
            by_offset = jnp.broadcast_to(_build_bias(bucket, tab_ref, h), (WINDOW, 2 * WINDOW))
            bias = pltpu.roll(by_offset, 0, axis=1, stride=1, stride_axis=0)
            sink = sink_ref[h]
            g = h // GROUP
            r0 = pl.multiple_of((h % GROUP) * WINDOW, WINDOW)
            bias_s[0, g, pl.ds(r0, WINDOW), :] = jnp.where(col == 0, sink, bias)
            bias_s[1, g, pl.ds(r0, WINDOW), :] = jnp.where(
                col == 0, sink, jnp.where(col < WINDOW, NEG_INF, bias))
            return c

        lax.fori_loop(0, N_HEADS, head_body, 0)

    first = (n == 0).astype(jnp.int32)
    row = lax.broadcasted_iota(jnp.int32, kvp_ref.shape, 0)
    kv_prev = jnp.where(row == 0, 0.0, kvp_ref[...])
    kv = jnp.concatenate([kv_prev, kvc_ref[...]], axis=0).astype(BF16)
    ones = jnp.ones((2 * WINDOW, 2 * HEAD_DIM), BF16)
    lane = lax.broadcasted_iota(jnp.int32, (WINDOW, 2 * HEAD_DIM), 1)
    def scores(idx):
        g, pair = divmod(idx, GROUP // 2)
        h0 = g * GROUP + 2 * pair
        kg = kv[:, g * HEAD_DIM:(g + 1) * HEAD_DIM]
        qp = jnp.concatenate([q_ref[:, h * HEAD_DIM:(h + 1) * HEAD_DIM] for h in (h0, h0 + 1)], axis=0)
        s = lax.dot_general(qp, kg, (((1,), (1,)), ((), ())), preferred_element_type=F32)
        return s + bias_s[first, g, 2 * pair * WINDOW:(2 * pair + 2) * WINDOW, :]

    def finish(idx, o_ext):
        h0 = 2 * idx
        o = o_ext[:, :2 * HEAD_DIM] * (1.0 / o_ext[:, 2 * HEAD_DIM:])
        o_ref[:, h0 * HEAD_DIM:(h0 + 2) * HEAD_DIM] = jnp.where(
            lane < HEAD_DIM, o[:WINDOW], o[WINDOW:]).astype(BF16)

    n_pairs = N_HEADS // 2

    def values(idx, p):
        g = idx // (GROUP // 2)
        vg = kv[:, KV_DIM + g * HEAD_DIM:KV_DIM + (g + 1) * HEAD_DIM]
        v_ext = jnp.concatenate([vg, vg, ones], axis=1)
        return jnp.dot(p, v_ext, preferred_element_type=F32)

    for first_pair in range(0, n_pairs, PAIR_BLOCK):
        block = range(first_pair, first_pair + PAIR_BLOCK)
        ss = [scores(idx) for idx in block]
        ms = [jnp.max(s, axis=-1, keepdims=True) for s in ss]
        ps = [jnp.exp(s - m).astype(BF16) for s, m in zip(ss, ms)]
        os_ = [values(idx, p) for idx, p in zip(block, ps)]
        for idx, o_ext in zip(block, os_):
            finish(idx, o_ext)


def _attn_prompt(q, kv, batch, seq, bucket, tab, sinks):
    nb = seq // WINDOW
    smem = pl.BlockSpec(memory_space=pltpu.SMEM)
    return pl.pallas_call(
        _attn_prompt_kernel,
        grid=(batch, nb),
        in_specs=[
            pl.BlockSpec((WINDOW, D_MODEL), lambda b, n: (b * nb + n, 0)),
            pl.BlockSpec((WINDOW, 2 * KV_DIM), lambda b, n: (jnp.maximum(b * nb + n - 1, 0), 0)),
            pl.BlockSpec((WINDOW, 2 * KV_DIM), lambda b, n: (b * nb + n, 0)),
            pl.BlockSpec((1, 2 * WINDOW), lambda b, n: (0, 0)),
            smem, smem,
        ],
        out_specs=pl.BlockSpec((WINDOW, D_MODEL), lambda b, n: (b * nb + n, 0)),
        out_shape=jax.ShapeDtypeStruct((batch * seq, D_MODEL), BF16),
        scratch_shapes=[pltpu.VMEM((2, N_KV_HEADS, GROUP * WINDOW, 2 * WINDOW), F32)],
        compiler_params=_params(("arbitrary", "arbitrary"), 32),
        name="attn_prompt",
    )(q, kv, kv, bucket, tab, sinks)


def _attn_sample_kernel(q_ref, ck_ref, cv_ref, kn_ref, vn_ref, bucket_ref, tab_ref, sink_ref,
                        o_ref, kwin_ref, vwin_ref, bias_s):
    steps = q_ref.shape[0]
    k_all = jnp.concatenate([ck_ref[...], kn_ref[...]], axis=0)
    v_all = jnp.concatenate([cv_ref[...], vn_ref[...]], axis=0)
    kwin_ref[...] = k_all[steps:, :]
    vwin_ref[...] = v_all[steps:, :]

    @pl.when(pl.program_id(0) == 0)
    def _():
        bucket = bucket_ref[...]

        def head_body(h, c):
            bias_s[h] = _build_bias(bucket, tab_ref, h)
            return c

        lax.fori_loop(0, N_HEADS, head_body, 0)

    rows = lax.broadcasted_iota(jnp.int32, (GROUP * steps, 1), 0)
    k = k_all.astype(BF16)
    v = v_all.astype(BF16)
    groups = range(N_KV_HEADS)

    def heads_of(g):
        return range(g * GROUP, (g + 1) * GROUP)

    scores, sinks = [], []
    for g in groups:
        qg = jnp.concatenate([q_ref[:, h * HEAD_DIM:(h + 1) * HEAD_DIM] for h in heads_of(g)], axis=0)
        bias = jnp.concatenate([bias_s[h] for h in heads_of(g)], axis=0)
        sink = jnp.full((GROUP * steps, 1), sink_ref[g * GROUP], F32)
        for hh in range(1, GROUP):
            sink = jnp.where(rows >= hh * steps, sink_ref[g * GROUP + hh], sink)
        kg = k[:, g * HEAD_DIM:(g + 1) * HEAD_DIM]
        scores.append(lax.dot_general(qg, kg, (((1,), (1,)), ((), ())),
                                      preferred_element_type=F32) + bias)
        sinks.append(sink)
    outs = [_softmax_pv(scores[g], sinks[g], v[:, g * HEAD_DIM:(g + 1) * HEAD_DIM]) for g in groups]
    for g in groups:
        for hh, h in enumerate(heads_of(g)):
            o_ref[:, h * HEAD_DIM:(h + 1) * HEAD_DIM] = outs[g][hh * steps:(hh + 1) * steps].astype(BF16)


def _attn_sample(q, cache_k, cache_v, k_new, v_new, bucket, tab, sinks):
    batch, steps, _ = q.shape
    lk = WINDOW + steps
    smem = pl.BlockSpec(memory_space=pltpu.SMEM)
    per_seq = lambda rows, cols: pl.BlockSpec((None, rows, cols), lambda b: (b, 0, 0))
    return pl.pallas_call(
        _attn_sample_kernel,
        grid=(batch,),
        in_specs=[
            per_seq(steps, D_MODEL),
            per_seq(WINDOW, KV_DIM), per_seq(WINDOW, KV_DIM),
            per_seq(steps, KV_DIM), per_seq(steps, KV_DIM),
            pl.BlockSpec((steps, lk), lambda b: (0, 0)),
            smem, smem,
        ],
        out_specs=[per_seq(steps, D_MODEL), per_seq(WINDOW, KV_DIM), per_seq(WINDOW, KV_DIM)],
        out_shape=[jax.ShapeDtypeStruct((batch, steps, D_MODEL), BF16),
                   jax.ShapeDtypeStruct((batch, WINDOW, KV_DIM), F32),
                   jax.ShapeDtypeStruct((batch, WINDOW, KV_DIM), F32)],
        scratch_shapes=[pltpu.VMEM((N_HEADS, steps, lk), F32)],
        compiler_params=_params(("arbitrary",), 32),
        name="attn_sample",
    )(q, cache_k, cache_v, k_new, v_new, bucket, tab, sinks)


def kernel(x_prompt, x_sample, state_conv, state_rnn, cache_k_win, cache_v_win, ln_g, ln_b, lru_w_x, lru_b_x, lru_w_y, lru_b_y, lru_conv_w, lru_conv_b, lru_w_rg, lru_b_rg, lru_w_ig, lru_b_ig, lru_lam, lru_w_out, lru_b_out, attn_w_kv, attn_w_q, attn_w_o, attn_sinks, rel_bias, moe_w_router, moe_b_router, moe_w_gate, moe_w_up, moe_w_down):
    bp, seq, _ = x_prompt.shape
    bs, steps, _ = x_sample.shape
    n_p = bp * seq
    n_s = bs * steps

    assert n_p % TOK_TILE == 0 and n_s <= TOK_TILE
    sample_tile = n_p // TOK_TILE

    def pad_tile(rows):
        return jnp.pad(rows, ((0, TOK_TILE - n_s), (0, 0)))

    x0 = (x_prompt.reshape(n_p, D_MODEL),
          pad_tile(x_sample.transpose(1, 0, 2).reshape(n_s, D_MODEL)))
    wr_t = moe_w_router.T
    br = moe_b_router.reshape(N_EXPERTS, 1)
    vec = lambda a: a.reshape(1, -1)

    xb, yb = _lru_in(x0, lru_w_x, vec(lru_b_x[0]), lru_w_y, vec(lru_b_y[0]), 0)
    lru_args = (lru_conv_w[0], vec(lru_conv_b[0]), lru_w_rg[0], lru_w_ig[0],
                vec(lru_b_rg[0]), vec(lru_b_ig[0]), vec(lru_lam[0]))
    m_p, conv_p, rnn_p = _lru_prompt(xb, yb, bp, seq, *lru_args)
    m_s, conv_s, rnn_s = _lru_sample(xb, yb, sample_tile, steps,
                                     state_conv[0].transpose(1, 0, 2), state_rnn[0], *lru_args)
    x1, x1_rows, e_idx, gates = _proj_ln((m_p, m_s), lru_w_out, (0,), vec(lru_b_out[0]), x0,
                                vec(ln_g[0, 0]), vec(ln_b[0, 0]), wr_t, br, name="lru_out_ln")
    ys = _moe_block(x1_rows, e_idx, moe_w_gate, moe_w_up, moe_w_down, 0)

    x2, q, kv = _combine_qkv(ys, x1, gates.T, vec(ln_g[0, 1]), vec(ln_b[0, 1]),
                             attn_w_q, (0,), attn_w_kv, name="moe_combine_qkv")
    tab = rel_bias.reshape(-1)
    sinks = attn_sinks[0]
    offsets = jnp.arange(2 * WINDOW)[None, :]
    o_p = _attn_prompt(q, kv, bp, seq, _masked_buckets(WINDOW - offsets), tab, sinks)
    kv_s = kv[n_p:n_p + n_s].reshape(steps, bs, 2, KV_DIM).transpose(2, 1, 0, 3)
    dist_s = jnp.arange(steps)[:, None] + WINDOW - jnp.arange(WINDOW + steps)[None, :]
    q_s = q[n_p:n_p + n_s].reshape(steps, bs, D_MODEL).transpose(1, 0, 2)
    o_s, k_win_s, v_win_s = _attn_sample(
        q_s, cache_k_win.reshape(bs, WINDOW, KV_DIM), cache_v_win.reshape(bs, WINDOW, KV_DIM),
        kv_s[0], kv_s[1], _masked_buckets(dist_s), tab, sinks)
    o_s = pad_tile(o_s.transpose(1, 0, 2).reshape(n_s, D_MODEL))
    x3, x3_rows, e_idx, gates = _proj_ln((o_p, o_s), attn_w_o, (0,), jnp.zeros((1, D_MODEL), F32), x2,
                                vec(ln_g[1, 0]), vec(ln_b[1, 0]), wr_t, br, name="attn_out_ln")
    ys = _moe_block(x3_rows, e_idx, moe_w_gate, moe_w_up, moe_w_down, 1)
    y_p, y_s = _combine_split(ys, x3, gates.T, vec(ln_g[1, 1]), vec(ln_b[1, 1]), name="moe_combine_1")

    y_prompt = y_p.reshape(bp, seq, D_MODEL)
    y_sample = y_s[:n_s].reshape(steps, bs, D_MODEL).transpose(1, 0, 2)
    kv_p = jnp.stack([kv[(b + 1) * seq - WINDOW:(b + 1) * seq] for b in range(bp)])
    kv_p = kv_p.reshape(bp, WINDOW, 2, N_KV_HEADS, HEAD_DIM)
    k_win_s = k_win_s.reshape(bs, WINDOW, N_KV_HEADS, HEAD_DIM)
    v_win_s = v_win_s.reshape(bs, WINDOW, N_KV_HEADS, HEAD_DIM)
    return (y_prompt, y_sample,
            conv_p[None], rnn_p.reshape(1, bp, D_MODEL),
            kv_p[:, :, 0], kv_p[:, :, 1],
            conv_s.transpose(1, 0, 2)[None], rnn_s[None],
            k_win_s, v_win_s)
```

```python
import functools
import math

import jax
import jax.numpy as jnp
from jax import lax
from jax.experimental import pallas as pl
from jax.experimental.pallas import tpu as pltpu

D_MODEL = 2048
DEPTH = 2
LRU_BLOCKS = 8
LRU_BLOCK = D_MODEL // LRU_BLOCKS
CONV_WIDTH = 4
LRU_C = 8.0
N_HEADS = 32
HEAD_DIM = 64
N_KV_HEADS = 8
GROUP = N_HEADS // N_KV_HEADS
KV_DIM = N_KV_HEADS * HEAD_DIM
WINDOW = 128
N_BUCKETS = 32
MAX_DISTANCE = 128
N_EXPERTS = 16
N_GROUPS = 4
EXPERTS_PER_GROUP = N_EXPERTS // N_GROUPS
D_EXPERT = 1024
ALPHA = (2 * DEPTH) ** 0.25
LN_EPS = 1e-5

LANES = 128
SEGS = 8
CHUNKS = D_MODEL // LANES
ROW_SUBLANES = D_MODEL // (2 * LANES)
MOE_TILE = 256
TOK_TILE = 256
CHUNK_TOK_ROWS = CHUNKS + 4
CHUNK_SEG_TOKS = TOK_TILE // SEGS
CHUNK_SEG_ROWS = CHUNK_SEG_TOKS * CHUNK_TOK_ROWS + 4
CHUNK_TILE_ROWS = SEGS * CHUNK_SEG_ROWS
DMA_UNROLL = 8
SCALAR_UNROLL = 32
PAIR_UNROLL = 16
PLAN_UNROLL = 4
PAIR_BLOCK = 4
GATE_BLOCKS = 4
W_CHUNKS = 4
CAST_ROWS = 256
BF16 = jnp.bfloat16
F32 = jnp.float32
NEG_INF = float("-inf")


def _params(sem, vmem_mb):
    return pltpu.CompilerParams(dimension_semantics=sem, vmem_limit_bytes=vmem_mb * 1024 * 1024)


def _cast_rows(src_ref, dst_ref):
    n = src_ref.shape[0] // CAST_ROWS

    def body(i, c):
        r = pl.multiple_of(i * CAST_ROWS, CAST_ROWS)
        dst_ref[pl.ds(r, CAST_ROWS), :] = src_ref[pl.ds(r, CAST_ROWS), :].astype(BF16)
        return c

    lax.fori_loop(0, n, body, 0)


def _layer_norm(z, g, b):
    mu = jnp.mean(z, axis=-1, keepdims=True)
    zc = z - mu
    var = jnp.mean(zc * zc, axis=-1, keepdims=True)
    return zc * lax.rsqrt(var + LN_EPS) * g + b


def _chunk_row(tok, chunk):
    seg, t = divmod(tok, CHUNK_SEG_TOKS)
    return seg * CHUNK_SEG_ROWS + t * CHUNK_TOK_ROWS + chunk


def _chunk_rows_store(ref, y):
    ref[...] = jnp.zeros_like(ref)
    for seg in range(SEGS):
        rows = slice(seg * CHUNK_SEG_TOKS, (seg + 1) * CHUNK_SEG_TOKS)
        for c in range(CHUNKS):
            dst = pl.ds(_chunk_row(seg * CHUNK_SEG_TOKS, c), CHUNK_SEG_TOKS, stride=CHUNK_TOK_ROWS)
            ref[dst, :] = y[rows, c * LANES:(c + 1) * LANES]


def _chunk_rows_load(ref, tok0, count, chunks=range(CHUNKS)):
    assert tok0 // CHUNK_SEG_TOKS == (tok0 + count - 1) // CHUNK_SEG_TOKS
    return jnp.concatenate(
        [ref[pl.ds(_chunk_row(tok0, c), count, stride=CHUNK_TOK_ROWS), :] for c in chunks], axis=1)


def _store_packed_rows(x_bf, rows_ref):
    n = x_bf.shape[0]
    bits = pltpu.bitcast(x_bf.astype(F32), jnp.uint32)
    packed = bits[:, D_MODEL // 2:] | (bits[:, :D_MODEL // 2] >> 16)
    for c in range(ROW_SUBLANES):
        rows_ref[pl.ds(c, n, stride=ROW_SUBLANES), :] = packed[:, c * LANES:(c + 1) * LANES]


def _load_packed_rows(rows_ref, x_bf_ref):
    n = x_bf_ref.shape[0]
    for c in range(ROW_SUBLANES):
        words = rows_ref[pl.ds(c, n, stride=ROW_SUBLANES), :]
        low = pltpu.bitcast(words << 16, F32).astype(BF16)
        high = pltpu.bitcast(words & jnp.uint32(0xFFFF0000), F32).astype(BF16)
        x_bf_ref[:, c * LANES:(c + 1) * LANES] = low
        x_bf_ref[:, D_MODEL // 2 + c * LANES:D_MODEL // 2 + (c + 1) * LANES] = high


def _tok_operands(x, tile_of=lambda i: i):
    if isinstance(x, tuple):
        xp, xs = x
        d = xp.shape[1]
        last_p = xp.shape[0] // TOK_TILE - 1
        specs = [pl.BlockSpec((TOK_TILE, d), lambda i, *_: (jnp.minimum(tile_of(i), last_p), 0)),
                 pl.BlockSpec((TOK_TILE, d), lambda i, *_: (0, 0))]
        return [xp, xs], specs, last_p + 2
    return ([x], [pl.BlockSpec((TOK_TILE, x.shape[1]), lambda i, *_: (tile_of(i), 0))],
            x.shape[0] // TOK_TILE)


def _tok_load(refs, is_sample=None):
    if len(refs) == 1:
        return refs[0][...]
    if is_sample is None:
        is_sample = pl.program_id(0) == pl.num_programs(0) - 1
    return jnp.where(is_sample, refs[1][...], refs[0][...])


def _lru_in_kernel(*refs, n_x, layer):
    x_refs = refs[:n_x]
    wx_hbm, wy_hbm, bx_ref, by_ref, xb_ref, yb_ref, wx_bf, wy_bf, stage, sems = refs[n_x:]

    @pl.when(pl.program_id(0) == 0)
    def _():
        _load_weight(wx_hbm.at[layer], wx_bf, stage, sems)
        _load_weight(wy_hbm.at[layer], wy_bf, stage, sems)

    x = _tok_load(x_refs).astype(BF16)
    _chunk_rows_store(xb_ref, jnp.dot(x, wx_bf[...], preferred_element_type=F32) + bx_ref[...])
    y = jnp.dot(x, wy_bf[...], preferred_element_type=F32) + by_ref[...]
    yb_ref[...] = jax.nn.gelu(y).astype(BF16)


def _lru_in(x, w_x, b_x, w_y, b_y, layer):
    arrays, specs, nt = _tok_operands(x)
    hbm = pl.BlockSpec(memory_space=pl.ANY)
    vec_spec = pl.BlockSpec((1, D_MODEL), lambda i: (0, 0))
    return pl.pallas_call(
        functools.partial(_lru_in_kernel, n_x=len(arrays), layer=layer),
        grid=(nt,),
        in_specs=specs + [hbm, hbm, vec_spec, vec_spec],
        out_specs=[pl.BlockSpec((CHUNK_TILE_ROWS, LANES), lambda i: (i, 0)),
                   pl.BlockSpec((TOK_TILE, D_MODEL), lambda i: (i, 0))],
        out_shape=[jax.ShapeDtypeStruct((nt * CHUNK_TILE_ROWS, LANES), F32),
                   jax.ShapeDtypeStruct((nt * TOK_TILE, D_MODEL), BF16)],
        scratch_shapes=[pltpu.VMEM((D_MODEL, D_MODEL), BF16), pltpu.VMEM((D_MODEL, D_MODEL), BF16),
                        pltpu.VMEM((2, CAST_ROWS, D_MODEL), F32), pltpu.SemaphoreType.DMA((2,))],
        compiler_params=_params(("arbitrary",), 48),
        name="lru_in",
    )(*arrays, w_x, w_y, b_x, b_y)


def _route(logits_t, b_router):
    aff = jax.nn.sigmoid(logits_t)
    sel = aff + b_router
    srow = [sel[e:e + 1, :] for e in range(N_EXPERTS)]
    arow = [aff[e:e + 1, :] for e in range(N_EXPERTS)]

    def top2_sum(v):
        pairs = [v[i] + v[j] for i in range(4) for j in range(i + 1, 4)]
        return functools.reduce(jnp.maximum, pairs)

    scores = [top2_sum(srow[4 * g:4 * g + 4]) for g in range(N_GROUPS)]
    best = scores[0]
    gi = jnp.zeros_like(best, dtype=jnp.int32)
    for g in range(1, N_GROUPS):
        upd = scores[g] > best
        best = jnp.where(upd, scores[g], best)
        gi = jnp.where(upd, g, gi)

    def pick_group(rows, j):
        out = rows[j]
        for g in range(1, N_GROUPS):
            out = jnp.where(gi == g, rows[4 * g + j], out)
        return out

    v = [pick_group(srow, j) for j in range(EXPERTS_PER_GROUP)]
    a = [pick_group(arow, j) for j in range(EXPERTS_PER_GROUP)]

    m1, i1 = v[0], jnp.zeros_like(gi)
    for j in range(1, EXPERTS_PER_GROUP):
        upd = v[j] > m1
        m1 = jnp.where(upd, v[j], m1)
        i1 = jnp.where(upd, j, i1)
    m2 = jnp.full_like(m1, NEG_INF)
    i2 = jnp.zeros_like(gi)
    for j in range(EXPERTS_PER_GROUP):
        cand = jnp.where(i1 == j, NEG_INF, v[j])
        upd = cand > m2
        m2 = jnp.where(upd, cand, m2)
        i2 = jnp.where(upd, j, i2)

    def pick_idx(rows, idx):
        out = rows[0]
        for j in range(1, EXPERTS_PER_GROUP):
            out = jnp.where(idx == j, rows[j], out)
        return out

    a1 = pick_idx(a, i1)
    a2 = pick_idx(a, i2)
    tot = a1 + a2
    e_idx = jnp.concatenate([gi * EXPERTS_PER_GROUP + i1, gi * EXPERTS_PER_GROUP + i2], axis=0)
    gates = jnp.concatenate([a1 / tot, a2 / tot], axis=0)
    return e_idx, gates


def _proj_ln_kernel(*refs, n_m, n_res):
    m_refs = refs[:n_m]
    w_ref, b_ref = refs[n_m:n_m + 2]
    res_refs = refs[n_m + 2:n_m + 2 + n_res]
    (g_ref, beta_ref, wr_ref, br_ref, x_ref, xrow_ref, e_ref, gate_ref,
     wbf_ref, ya, yb) = refs[n_m + 2 + n_res:]
    i = pl.program_id(0)
    n_tiles = pl.num_programs(0) - 1

    @pl.when(i == 0)
    def _():
        _cast_rows(w_ref, wbf_ref)
        yb[...] = jnp.zeros_like(yb)

    for parity, (cur, prev) in enumerate(((ya, yb), (yb, ya))):
        @pl.when(i % 2 == parity)
        def _():
            cur[...] = jnp.dot(_tok_load(m_refs, i >= n_tiles - 1), wbf_ref[...],
                               preferred_element_type=F32)
            y = prev[...] + b_ref[...]
            x = _layer_norm(ALPHA * _tok_load(res_refs, i == n_tiles) + y, g_ref[...], beta_ref[...])
            x_ref[...] = x
            x_bf = x.astype(BF16)
            _store_packed_rows(x_bf, xrow_ref)
            logits_t = lax.dot_general(wr_ref[...].astype(BF16), x_bf,
                                       (((1,), (1,)), ((), ())), preferred_element_type=F32)
            e_idx, gates = _route(logits_t, br_ref[...])
            e_ref[...] = e_idx
            gate_ref[...] = gates


def _proj_ln(m, w, w_index, b, res, g, beta, wr_t, br, *, name):
    nt = _tok_operands(m)[2]
    m_arrays, m_specs, _ = _tok_operands(m, lambda i: jnp.minimum(i, nt - 1))
    res_arrays, res_specs, _ = _tok_operands(res, lambda i: jnp.maximum(i - 1, 0))
    k = w.shape[-2]
    tm = TOK_TILE
    n = nt * tm
    row = lambda i: (jnp.maximum(i - 1, 0), 0)
    const = lambda i: (0, 0)
    x, x_rows, e_idx, gates = pl.pallas_call(
        functools.partial(_proj_ln_kernel, n_m=len(m_arrays), n_res=len(res_arrays)),
        grid=(nt + 1,),
        in_specs=m_specs + [
            pl.BlockSpec((None,) * len(w_index) + (k, D_MODEL), lambda i: w_index + (0, 0),
                         pipeline_mode=pl.Buffered(1)),
            pl.BlockSpec((1, D_MODEL), const),
        ] + res_specs + [
            pl.BlockSpec((1, D_MODEL), const),
            pl.BlockSpec((1, D_MODEL), const),
            pl.BlockSpec((N_EXPERTS, D_MODEL), const),
            pl.BlockSpec((N_EXPERTS, 1), const),
        ],
        out_specs=[
            pl.BlockSpec((tm, D_MODEL), row),
            pl.BlockSpec((tm * ROW_SUBLANES, LANES), row),
            pl.BlockSpec((None, 2, tm), lambda i: (jnp.maximum(i - 1, 0), 0, 0)),
            pl.BlockSpec((None, 2, tm), lambda i: (jnp.maximum(i - 1, 0), 0, 0)),
        ],
        out_shape=[
            jax.ShapeDtypeStruct((n, D_MODEL), F32),
            jax.ShapeDtypeStruct((n * ROW_SUBLANES, LANES), jnp.uint32),
            jax.ShapeDtypeStruct((nt, 2, tm), jnp.int32),
            jax.ShapeDtypeStruct((nt, 2, tm), F32),
        ],
        scratch_shapes=[pltpu.VMEM((k, D_MODEL), BF16),
                        pltpu.VMEM((tm, D_MODEL), F32), pltpu.VMEM((tm, D_MODEL), F32)],
        compiler_params=_params(("arbitrary",), 52),
        name=name,
    )(*m_arrays, w, b, *res_arrays, g, beta, wr_t, br)
    e_idx = e_idx.transpose(1, 0, 2).reshape(2, n)
    gates = gates.transpose(1, 0, 2).reshape(2, n)
    return x, x_rows, e_idx, gates


def _plan_kernel(e_ref, pos_ref, meta_ref, rank_ref):
    nrow = e_ref.shape[0]
    ri = lax.broadcasted_iota(jnp.int32, (LANES, LANES), 0)
    ci = lax.broadcasted_iota(jnp.int32, (LANES, LANES), 1)
    tri = jnp.where(ri <= ci, 1.0, 0.0).astype(BF16)
    sub = lax.broadcasted_iota(jnp.int32, (N_EXPERTS, LANES), 0)

    def count_body(b, base):
        rows = [b * PLAN_UNROLL + u for u in range(PLAN_UNROLL)]
        onehots = [sub == e_ref[pl.ds(r, 1), :] for r in rows]
        locs = [jnp.dot(jnp.where(oh, 1.0, 0.0).astype(BF16), tri, preferred_element_type=F32)
                for oh in onehots]
        for r, onehot, loc in zip(rows, onehots, locs):
            rank_ref[pl.ds(r, 1), :] = jnp.sum(jnp.where(onehot, base + loc - 1.0, 0.0),
                                               axis=0, keepdims=True)
            base = base + jnp.broadcast_to(loc[:, LANES - 1:LANES], (N_EXPERTS, LANES))
        return base

    count = lax.fori_loop(0, nrow // PLAN_UNROLL, count_body, jnp.zeros((N_EXPERTS, LANES), F32))
    ntile = jnp.floor((count + (MOE_TILE - 1.0)) * (1.0 / MOE_TILE))
    offs = []
    acc = jnp.zeros((1, LANES), F32)
    for e in range(N_EXPERTS):
        offs.append(acc)
        acc = acc + ntile[e:e + 1, :]
    tile_off = jnp.concatenate(offs, axis=0)
    tile_end = tile_off + ntile
    lane = lax.broadcasted_iota(jnp.int32, (N_EXPERTS, LANES), 1).astype(F32)
    tile_expert = jnp.sum(jnp.where(tile_end <= lane, 1.0, 0.0), axis=0, keepdims=True)
    tile_expert = jnp.minimum(tile_expert, N_EXPERTS - 1.0)
    own = jnp.logical_and(tile_off <= lane, lane < tile_end)
    run_end = jnp.sum(jnp.where(own, tile_end, 0.0), axis=0, keepdims=True)
    next_expert = jnp.sum(jnp.where(tile_end <= run_end, 1.0, 0.0), axis=0, keepdims=True)
    has_next = jnp.logical_and(lane[0:1, :] < acc, run_end < acc)
    next_expert = jnp.where(has_next, next_expert, -1.0)
    meta = jnp.concatenate([tile_expert, acc, next_expert, jnp.zeros((5, LANES), F32)], axis=0)
    meta_ref[...] = meta.astype(jnp.int32)
    row_off = tile_off * float(MOE_TILE)

    def pos_body(r, c):
        onehot = sub == e_ref[pl.ds(r, 1), :]
        p = jnp.sum(jnp.where(onehot, row_off, 0.0), axis=0, keepdims=True) + rank_ref[pl.ds(r, 1), :]
        pos_ref[pl.ds(r, 1), :] = p.astype(jnp.int32)
        return c

    lax.fori_loop(0, nrow, pos_body, 0)


def _plan(e_idx, *, name):
    n2 = e_idx.shape[0] * e_idx.shape[1]
    assert n2 % (LANES * PLAN_UNROLL) == 0
    e2d = e_idx.reshape(n2 // LANES, LANES)
    pos, meta = pl.pallas_call(
        _plan_kernel,
        out_shape=[jax.ShapeDtypeStruct(e2d.shape, jnp.int32),
                   jax.ShapeDtypeStruct((8, LANES), jnp.int32)],
        scratch_shapes=[pltpu.VMEM(e2d.shape, F32)],
        name=name,
    )(e2d)
    return pos.reshape(n2), meta[0], meta[1, :1], meta[2]


def _invert_kernel(pos_ref, pair_ref):
    n_rows = pair_ref.shape[0]
    n_pairs = pos_ref.shape[0]

    def fill_body(b, c):
        for u in range(SCALAR_UNROLL):
            pair_ref[b * SCALAR_UNROLL + u] = -1
        return c

    def pair_body(b, c):
        rows = [pos_ref[b * PAIR_UNROLL + u] for u in range(PAIR_UNROLL)]
        for u in range(PAIR_UNROLL):
            pair_ref[rows[u]] = b * PAIR_UNROLL + u
        return c

    lax.fori_loop(0, n_rows // SCALAR_UNROLL, fill_body, 0)
    lax.fori_loop(0, n_pairs // PAIR_UNROLL, pair_body, 0)


def _invert(pos, n_rows, *, name):
    return pl.pallas_call(
        _invert_kernel,
        grid_spec=pltpu.PrefetchScalarGridSpec(
            num_scalar_prefetch=1,
            grid=(1,),
            in_specs=[],
            out_specs=pl.BlockSpec(memory_space=pltpu.SMEM),
        ),
        out_shape=jax.ShapeDtypeStruct((n_rows,), jnp.int32),
        name=name,
    )(pos)


def _expert_changed(te_ref, i):
    return jnp.logical_or(i == 0, te_ref[i] != te_ref[jnp.maximum(i - 1, 0)])


class _ExpertWeights:
    def __init__(self, mats, layer, st, sems):
        self.mats, self.layer, self.st, self.sems = mats, layer, st, sems

    def _copies(self, expert, c):
        out = []
        for w_hbm, stage, _, _ in self.mats:
            rows = stage.shape[1]
            src = w_hbm.at[self.layer, expert, pl.ds(pl.multiple_of(c * rows, rows), rows), :]
            out.append(pltpu.make_async_copy(src, stage.at[c % 2], self.sems.at[c % 2]))
        return out

    def _start(self, expert, c):
        for cp in self._copies(expert, c):
            cp.start()
        self.st[2] = c + 1

    def _convert(self, c_src, c_dst):
        for _, stage, w_next, _ in self.mats:
            rows = stage.shape[1]
            dst = pl.ds(pl.multiple_of(c_dst * rows, rows), rows)
            w_next[dst, :] = stage[c_src % 2].astype(BF16)

    def reset(self):
        for _, stage, _, _ in self.mats:
            stage[...] = jnp.zeros_like(stage)
        self.st[1] = 0
        self.st[2] = 0

    def switch_to(self, expert):
        st = self.st

        def body(c, carry):
            @pl.when(c >= st[2])
            def _():
                self._start(expert, c)

            @pl.when(jnp.logical_and(c + 1 < W_CHUNKS, c + 1 >= st[2]))
            def _():
                self._start(expert, c + 1)

            for cp in self._copies(expert, c):
                cp.wait()
            self._convert(c, c)
            return carry

        lax.fori_loop(st[1], W_CHUNKS, body, 0)
        for _, _, w_next, w_cur in self.mats:
            _copy_rows(w_next, w_cur)
        st[1] = 0
        st[2] = 0

    def begin_step(self, next_expert):
        st = self.st
        done, issued = st[1], st[2]
        has_next = next_expert >= 0
        active = jnp.logical_and(has_next, done < issued)

        @pl.when(jnp.logical_and(has_next, jnp.logical_and(issued < W_CHUNKS, issued < done + 2)))
        def _():
            self._start(next_expert, issued)

        @pl.when(active)
        def _():
            for cp in self._copies(next_expert, done):
                cp.wait()

        return active, done

    def convert_step(self, active, done):
        self._convert(jnp.where(active, done, done + 1), jnp.where(active, done, W_CHUNKS))

    def end_step(self, active, done):
        @pl.when(active)
        def _():
            self.st[1] = done + 1


def _copy_rows(src_ref, dst_ref):
    n = dst_ref.shape[0] // CAST_ROWS

    def body(i, c):
        r = pl.multiple_of(i * CAST_ROWS, CAST_ROWS)
        dst_ref[pl.ds(r, CAST_ROWS), :] = src_ref[pl.ds(r, CAST_ROWS), :]
        return c

    lax.fori_loop(0, n, body, 0)


def _moe_tile_kernel(pair_ref, te_ref, nu_ref, nxt_ref, x_hbm, wg_hbm, wu_hbm, wd_hbm, out_hbm,
                     wg_cur, wg_next, wg_stage, wu_cur, wu_next, wu_stage, wd_cur, wd_next, wd_stage,
                     x_rows, x_bf, ya, yb, gsem, ssems, tsem, wsems, st, *, n_tok, n_tiles, layer):
    i = pl.program_id(0)
    nu = nu_ref[0]
    running = i < nu
    trash = 2 * n_tok
    weights = _ExpertWeights([(wg_hbm, wg_stage, wg_next, wg_cur), (wu_hbm, wu_stage, wu_next, wu_cur),
                              (wd_hbm, wd_stage, wd_next, wd_cur)], layer, st, wsems)

    def gather_copy(tile, r):
        p = pair_ref[tile * MOE_TILE + r]
        tok = jnp.where(p >= n_tok, p - n_tok, jnp.maximum(p, 0))
        src = pl.ds(pl.multiple_of(tok * ROW_SUBLANES, ROW_SUBLANES), ROW_SUBLANES)
        return pltpu.make_async_copy(x_hbm.at[src, :], x_rows.at[pl.ds(r * ROW_SUBLANES, ROW_SUBLANES), :],
                                     gsem)

    def scatter_copy(tile, r, buf, sem):
        p = pair_ref[tile * MOE_TILE + r]
        dst = jnp.where(p < 0, trash + r, p)
        return pltpu.make_async_copy(buf.at[pl.ds(r, 1), :], out_hbm.at[pl.ds(dst, 1), :], sem)

    def wait_scatter(buf, sem):
        pltpu.make_async_copy(buf, out_hbm.at[pl.ds(0, MOE_TILE), :], sem).wait()

    @pl.when(i == 0)
    def _():
        def body(rb, c):
            for u in range(DMA_UNROLL):
                gather_copy(0, rb * DMA_UNROLL + u).start()
            return c
        lax.fori_loop(0, MOE_TILE // DMA_UNROLL, body, 0)
        yb[...] = jnp.zeros_like(yb)
        fill = pltpu.make_async_copy(yb, out_hbm.at[pl.ds(trash, MOE_TILE), :], tsem)
        fill.start()
        fill.wait()
        weights.reset()

    @pl.when(i <= nu)
    def _():
        pltpu.make_async_copy(x_hbm.at[pl.ds(0, MOE_TILE * ROW_SUBLANES), :], x_rows, gsem).wait()

    @pl.when(jnp.logical_and(running, _expert_changed(te_ref, i)))
    def _():
        weights.switch_to(te_ref[i])

    active, done = weights.begin_step(jnp.where(running, nxt_ref[i], -1))

    for parity, (cur, prev) in enumerate(((ya, yb), (yb, ya))):
        cur_sem, prev_sem = ssems.at[parity], ssems.at[1 - parity]
        mine = i % 2 == parity

        @pl.when(jnp.logical_and(mine, jnp.logical_and(i >= 1, i - 1 <= nu)))
        def _():
            wait_scatter(cur, cur_sem)

        @pl.when(jnp.logical_and(mine, running))
        def _():
            _load_packed_rows(x_rows, x_bf)
            next_tile = jnp.minimum(i + 1, n_tiles - 1)
            prev_tile = jnp.maximum(i - 1, 0)
            for r in range(MOE_TILE):
                gather_copy(next_tile, r).start(priority=0)
            for r in range(MOE_TILE):
                scatter_copy(prev_tile, r, prev, prev_sem).start(priority=1)
            weights.convert_step(active, done)
            x = x_bf[...]
            a = jnp.dot(x, wg_cur[...], preferred_element_type=F32)
            b = jnp.dot(x, wu_cur[...], preferred_element_type=F32)
            h = (jax.nn.silu(a) * b).astype(BF16)
            cur[...] = jnp.dot(h, wd_cur[...], preferred_element_type=F32)

        @pl.when(jnp.logical_and(mine, i == nu))
        def _():
            def body(rb, c):
                for u in range(DMA_UNROLL):
                    scatter_copy(i - 1, rb * DMA_UNROLL + u, prev, prev_sem).start()
                return c
            lax.fori_loop(0, MOE_TILE // DMA_UNROLL, body, 0)

            @pl.when(i == n_tiles)
            def _():
                wait_scatter(prev, prev_sem)

    weights.end_step(active, done)


def _moe_tiles(x, pair, tile_expert, n_used, next_expert, w_gate, w_up, w_down, layer):
    n_tok = x.shape[0] // ROW_SUBLANES
    n_tiles = pair.shape[0] // MOE_TILE
    hbm = pl.BlockSpec(memory_space=pl.ANY)

    def weight_bufs(k, n):
        chunk = k // W_CHUNKS
        return [pltpu.VMEM((k, n), BF16), pltpu.VMEM((k + chunk, n), BF16), pltpu.VMEM((2, chunk, n), F32)]

    return pl.pallas_call(
        functools.partial(_moe_tile_kernel, n_tok=n_tok, n_tiles=n_tiles, layer=layer),
        grid_spec=pltpu.PrefetchScalarGridSpec(
            num_scalar_prefetch=4,
            grid=(n_tiles + 1,),
            in_specs=[hbm, hbm, hbm, hbm],
            out_specs=hbm,
            scratch_shapes=weight_bufs(D_MODEL, D_EXPERT) + weight_bufs(D_MODEL, D_EXPERT)
            + weight_bufs(D_EXPERT, D_MODEL) + [
                pltpu.VMEM((MOE_TILE * ROW_SUBLANES, LANES), jnp.uint32),
                pltpu.VMEM((MOE_TILE, D_MODEL), BF16),
                pltpu.VMEM((MOE_TILE, D_MODEL), F32), pltpu.VMEM((MOE_TILE, D_MODEL), F32),
                pltpu.SemaphoreType.DMA(()), pltpu.SemaphoreType.DMA((2,)),
                pltpu.SemaphoreType.DMA(()), pltpu.SemaphoreType.DMA((2,)),
                pltpu.SMEM((4,), jnp.int32)],
        ),
        out_shape=jax.ShapeDtypeStruct((2 * n_tok + MOE_TILE, D_MODEL), F32),
        compiler_params=_params(("arbitrary",), 56),
        name=f"moe_tiles_{layer}",
    )(pair, tile_expert, n_used, next_expert, x, w_gate, w_up, w_down)


def _combine(y0_ref, y1_ref, res_ref, gate_ref, g_ref, beta_ref):
    gate = gate_ref[...]
    ffn = gate[:, 0:1] * y0_ref[...] + gate[:, 1:2] * y1_ref[...]
    return _layer_norm(ALPHA * res_ref[...] + ffn, g_ref[...], beta_ref[...])


def _combine_specs(n):
    nt = n // TOK_TILE
    row = lambda i: (i, 0)
    const = lambda i: (0, 0)
    return [pl.BlockSpec((TOK_TILE, D_MODEL), row),
            pl.BlockSpec((TOK_TILE, D_MODEL), lambda i: (i + nt, 0)),
            pl.BlockSpec((TOK_TILE, D_MODEL), row),
            pl.BlockSpec((TOK_TILE, 2), row),
            pl.BlockSpec((1, D_MODEL), const),
            pl.BlockSpec((1, D_MODEL), const)]


def _combine_split_kernel(y0_ref, y1_ref, res_ref, gate_ref, g_ref, beta_ref, prompt_ref, sample_ref):
    x = _combine(y0_ref, y1_ref, res_ref, gate_ref, g_ref, beta_ref)
    is_sample = pl.program_id(0) == pl.num_programs(0) - 1

    @pl.when(jnp.logical_not(is_sample))
    def _():
        prompt_ref[...] = x

    @pl.when(is_sample)
    def _():
        sample_ref[...] = x


def _combine_split(ys, res, gates_col, g, beta, *, name):
    n = res.shape[0]
    tm = TOK_TILE
    nt = n // tm
    return pl.pallas_call(
        _combine_split_kernel,
        grid=(nt,),
        in_specs=_combine_specs(n),
        out_specs=[pl.BlockSpec((tm, D_MODEL), lambda i: (jnp.minimum(i, nt - 2), 0)),
                   pl.BlockSpec((tm, D_MODEL), lambda i: (0, 0))],
        out_shape=[jax.ShapeDtypeStruct((n - tm, D_MODEL), F32),
                   jax.ShapeDtypeStruct((tm, D_MODEL), F32)],
        compiler_params=_params(("arbitrary",), 40),
        name=name,
    )(ys, ys, res, gates_col, g, beta)


def _load_weight(w_hbm, wbf_ref, stage_ref, sems):
    rows = stage_ref.shape[1]
    n_chunks = wbf_ref.shape[0] // rows

    def chunk_copy(c):
        return pltpu.make_async_copy(w_hbm.at[pl.ds(c * rows, rows), :], stage_ref.at[c % 2],
                                     sems.at[c % 2])

    chunk_copy(0).start()
    for c in range(n_chunks):
        if c + 1 < n_chunks:
            chunk_copy(c + 1).start()
        chunk_copy(c).wait()
        wbf_ref[c * rows:(c + 1) * rows, :] = stage_ref[c % 2].astype(BF16)


def _combine_qkv_kernel(y0_ref, y1_ref, res_ref, gate_ref, g_ref, beta_ref, wq_hbm, wkv_hbm,
                        x_ref, q_ref, kv_ref, wq_bf, wkv_bf, stage_q, stage_kv, sems, *, wq_index):
    @pl.when(pl.program_id(0) == 0)
    def _():
        wq = wq_hbm
        for k in wq_index:
            wq = wq.at[k]
        _load_weight(wq, wq_bf, stage_q, sems)
        _load_weight(wkv_hbm, wkv_bf, stage_kv, sems)

    x = _combine(y0_ref, y1_ref, res_ref, gate_ref, g_ref, beta_ref)
    x_ref[...] = x
    x_bf = x.astype(BF16)
    q = jnp.dot(x_bf, wq_bf[...], preferred_element_type=F32) * (HEAD_DIM ** -0.5)
    q_ref[...] = q.astype(BF16)
    kv_ref[...] = jnp.dot(x_bf, wkv_bf[...], preferred_element_type=F32)


def _combine_qkv(ys, res, gates_col, g, beta, w_q, wq_index, w_kv, *, name):
    n = res.shape[0]
    tm = TOK_TILE
    row = lambda i: (i, 0)
    hbm = pl.BlockSpec(memory_space=pl.ANY)
    return pl.pallas_call(
        functools.partial(_combine_qkv_kernel, wq_index=wq_index),
        grid=(n // tm,),
        in_specs=_combine_specs(n) + [hbm, hbm],
        out_specs=[pl.BlockSpec((tm, D_MODEL), row),
                   pl.BlockSpec((tm, D_MODEL), row),
                   pl.BlockSpec((tm, 2 * KV_DIM), row)],
        out_shape=[jax.ShapeDtypeStruct((n, D_MODEL), F32),
                   jax.ShapeDtypeStruct((n, D_MODEL), BF16),
                   jax.ShapeDtypeStruct((n, 2 * KV_DIM), F32)],
        scratch_shapes=[pltpu.VMEM((D_MODEL, D_MODEL), BF16),
                        pltpu.VMEM((D_MODEL, 2 * KV_DIM), BF16),
                        pltpu.VMEM((2, CAST_ROWS, D_MODEL), F32),
                        pltpu.VMEM((2, CAST_ROWS, 2 * KV_DIM), F32),
                        pltpu.SemaphoreType.DMA((2,))],
        compiler_params=_params(("arbitrary",), 52),
        name=name,
    )(ys, ys, res, gates_col, g, beta, w_q, w_kv)


def _moe_block(x_rows, e_idx, w_gate, w_up, w_down, layer):
    n = x_rows.shape[0] // ROW_SUBLANES
    n_tiles = -(-(2 * n + N_EXPERTS * (MOE_TILE - 1)) // MOE_TILE)
    pos, tile_expert, n_used, next_expert = _plan(e_idx, name=f"moe_plan_{layer}")
    pair = _invert(pos, n_tiles * MOE_TILE, name=f"moe_invert_{layer}")
    return _moe_tiles(x_rows, pair, tile_expert, n_used, next_expert, w_gate, w_up, w_down, layer)


def _sigmoid_of_half(half_x):
    return 0.5 * jnp.tanh(half_x) + 0.5


def _log_sigmoid(x):
    return -(jnp.maximum(-x, 0.0) + jnp.log1p(jnp.exp(-jnp.abs(x))))


def _lru_gate_blocks(xcs, blocks, wrg_bf, wig_bf, brg_ref, big_ref, lam_ref):
    xbs = [xc.astype(BF16) for xc in xcs]
    r_lin = [jnp.dot(xb, wrg_bf[n], preferred_element_type=F32) for xb, n in zip(xbs, blocks)]
    i_lin = [jnp.dot(xb, wig_bf[n], preferred_element_type=F32) for xb, n in zip(xbs, blocks)]
    out = []
    for xc, n, rl, il in zip(xcs, blocks, r_lin, i_lin):
        cols = slice(n * LRU_BLOCK, (n + 1) * LRU_BLOCK)
        r = _sigmoid_of_half(rl + 0.5 * brg_ref[:, cols])
        i = _sigmoid_of_half(il + 0.5 * big_ref[:, cols])
        log_a = r * (LRU_C * _log_sigmoid(lam_ref[:, cols]))
        a = jnp.exp(log_a)
        u = xc * i * jnp.sqrt(-jnp.tanh(log_a) * (a * a + 1.0))
        out.append((a, u))
    return out


def _lru_gate_block(xc, n, wrg_bf, wig_bf, brg_ref, big_ref, lam_ref):
    return _lru_gate_blocks([xc], [n], wrg_bf, wig_bf, brg_ref, big_ref, lam_ref)[0]


def _cast_gate_weights(wrg_ref, wig_ref, wrg_bf, wig_bf):
    for n in range(LRU_BLOCKS):
        wrg_bf[n] = (0.5 * wrg_ref[n]).astype(BF16)
        wig_bf[n] = (0.5 * wig_ref[n]).astype(BF16)


def _lru_prompt_kernel(xb_ref, yb_ref, cw_ref, cb_ref, wrg_ref, wig_ref, brg_ref, big_ref, lam_ref,
                       m_ref, conv_ref, hlast_ref, xs, tail, a_s, u_s, hs_t, h_s, wrg_bf, wig_bf):
    b = pl.program_id(0)
    j = pl.program_id(1)
    tt = m_ref.shape[0]
    seg_len = tt // SEGS
    taps = CONV_WIDTH - 1
    head = SEGS * taps

    @pl.when(jnp.logical_and(b == 0, j == 0))
    def _():
        _cast_gate_weights(wrg_ref, wig_ref, wrg_bf, wig_bf)

    @pl.when(j == 0)
    def _():
        tail[...] = jnp.zeros_like(tail)
        h_s[...] = jnp.zeros_like(h_s)

    for q in range(seg_len):
        xs[head + SEGS * q:head + SEGS * (q + 1), :] = jnp.concatenate(
            [xb_ref[pl.ds(_chunk_row(q, c), SEGS, stride=CHUNK_SEG_ROWS), :] for c in range(CHUNKS)],
            axis=1)
    sub = lax.broadcasted_iota(jnp.int32, (SEGS, D_MODEL), 0)
    for k in range(taps):
        last = head + SEGS * (seg_len - taps + k)
        joined = jnp.where(sub == SEGS - 1, tail[SEGS * k:SEGS * (k + 1), :], xs[last:last + SEGS, :])
        xs[SEGS * k:SEGS * (k + 1), :] = pltpu.roll(joined, 1, axis=0)
    tail[...] = xs[head + SEGS * (seg_len - taps):head + SEGS * seg_len, :]

    for first_block in range(0, LRU_BLOCKS, GATE_BLOCKS):
        blocks = range(first_block, first_block + GATE_BLOCKS)
        xcs = []
        for n in blocks:
            cols = slice(n * LRU_BLOCK, (n + 1) * LRU_BLOCK)
            xc = cb_ref[:, cols] + cw_ref[0:1, cols] * xs[0:tt, cols]
            for k in range(1, CONV_WIDTH):
                xc = xc + cw_ref[k:k + 1, cols] * xs[SEGS * k:SEGS * k + tt, cols]
            xcs.append(xc)
        gates = _lru_gate_blocks(xcs, blocks, wrg_bf, wig_bf, brg_ref, big_ref, lam_ref)
        for n, (a, u) in zip(blocks, gates):
            cols = slice(n * LRU_BLOCK, (n + 1) * LRU_BLOCK)
            a_s[:, cols] = a
            u_s[:, cols] = u

    def scan_body(q, carry):
        h, prod = carry
        rows = pl.ds(pl.multiple_of(q * SEGS, SEGS), SEGS)
        a = a_s[rows, :]
        h = a * h + u_s[rows, :]
        prod = a * prod
        u_s[rows, :] = h
        a_s[rows, :] = prod
        return h, prod

    h_end, prod_end = lax.fori_loop(
        0, seg_len, scan_body,
        (jnp.zeros((SEGS, D_MODEL), F32), jnp.ones((SEGS, D_MODEL), F32)))
    state = h_s[...]
    entering = []
    for s in range(SEGS):
        entering.append(state)
        state = h_end[s:s + 1, :] + prod_end[s:s + 1, :] * state
    h_s[...] = state
    enter = jnp.concatenate(entering, axis=0)

    def fix_body(q, carry):
        rows = pl.ds(pl.multiple_of(q * SEGS, SEGS), SEGS)
        h = u_s[rows, :] + a_s[rows, :] * enter
        for c in range(CHUNKS):
            hs_t[pl.ds(q * CHUNK_TOK_ROWS + c, SEGS, stride=CHUNK_SEG_ROWS), :] = h[:, c * LANES:(c + 1) * LANES]
        return carry

    lax.fori_loop(0, seg_len, fix_body, 0)
    for s in range(SEGS):
        rows = slice(s * seg_len, (s + 1) * seg_len)
        hs = _chunk_rows_load(hs_t, s * seg_len, seg_len)
        m_ref[rows, :] = (hs * yb_ref[rows, :].astype(F32)).astype(BF16)

    @pl.when(j == pl.num_programs(1) - 1)
    def _():
        for k in range(taps):
            conv_ref[k:k + 1, :] = tail[SEGS * k + SEGS - 1:SEGS * (k + 1), :]
        hlast_ref[...] = state


def _lru_prompt(xb, yb, batch, seq, cw, cb, wrg, wig, brg, big, lam):
    tt = TOK_TILE
    nj = seq // tt
    row = lambda b, j: (b * nj + j, 0)
    const2 = lambda b, j: (0, 0)
    const3 = lambda b, j: (0, 0, 0)
    return pl.pallas_call(
        _lru_prompt_kernel,
        grid=(batch, nj),
        in_specs=[
            pl.BlockSpec((CHUNK_TILE_ROWS, LANES), row),
            pl.BlockSpec((tt, D_MODEL), row),
            pl.BlockSpec((CONV_WIDTH, D_MODEL), const2),
            pl.BlockSpec((1, D_MODEL), const2),
            pl.BlockSpec((LRU_BLOCKS, LRU_BLOCK, LRU_BLOCK), const3),
            pl.BlockSpec((LRU_BLOCKS, LRU_BLOCK, LRU_BLOCK), const3),
            pl.BlockSpec((1, D_MODEL), const2),
            pl.BlockSpec((1, D_MODEL), const2),
            pl.BlockSpec((1, D_MODEL), const2),
        ],
        out_specs=[
            pl.BlockSpec((tt, D_MODEL), row),
            pl.BlockSpec((None, CONV_WIDTH - 1, D_MODEL), lambda b, j: (b, 0, 0)),
            pl.BlockSpec((None, 1, D_MODEL), lambda b, j: (b, 0, 0)),
        ],
        out_shape=[
            jax.ShapeDtypeStruct((batch * seq, D_MODEL), BF16),
            jax.ShapeDtypeStruct((batch, CONV_WIDTH - 1, D_MODEL), F32),
            jax.ShapeDtypeStruct((batch, 1, D_MODEL), F32),
        ],
        scratch_shapes=[
            pltpu.VMEM((tt + SEGS * (CONV_WIDTH - 1), D_MODEL), F32),
            pltpu.VMEM((SEGS * (CONV_WIDTH - 1), D_MODEL), F32),
            pltpu.VMEM((tt, D_MODEL), F32),
            pltpu.VMEM((tt, D_MODEL), F32),
            pltpu.VMEM((CHUNK_TILE_ROWS, LANES), F32),
            pltpu.VMEM((1, D_MODEL), F32),
            pltpu.VMEM((LRU_BLOCKS, LRU_BLOCK, LRU_BLOCK), BF16),
            pltpu.VMEM((LRU_BLOCKS, LRU_BLOCK, LRU_BLOCK), BF16),
        ],
        compiler_params=_params(("arbitrary", "arbitrary"), 40),
        name="lru_prompt",
    )(xb, yb, cw, cb, wrg, wig, brg, big, lam)


def _lru_sample_kernel(xb_ref, yb_ref, cs_ref, h0_ref, cw_ref, cb_ref, wrg_ref, wig_ref,
                       brg_ref, big_ref, lam_ref, m_ref, conv_ref, hlast_ref, wrg_bf, wig_bf, *, steps):
    batch = h0_ref.shape[0]
    _cast_gate_weights(wrg_ref, wig_ref, wrg_bf, wig_bf)
    m_ref[steps * batch:, :] = jnp.zeros((m_ref.shape[0] - steps * batch, D_MODEL), BF16)

    def slab(t, cols):
        if t < CONV_WIDTH - 1:
            return cs_ref[t, :, cols]
        t -= CONV_WIDTH - 1
        first, stop, _ = cols.indices(D_MODEL)
        return _chunk_rows_load(xb_ref, t * batch, batch, range(first // LANES, stop // LANES))

    for n in range(LRU_BLOCKS):
        cols = slice(n * LRU_BLOCK, (n + 1) * LRU_BLOCK)
        h = h0_ref[:, cols]
        for t in range(steps):
            xc = cb_ref[:, cols] + cw_ref[0:1, cols] * slab(t, cols)
            for k in range(1, CONV_WIDTH):
                xc = xc + cw_ref[k:k + 1, cols] * slab(t + k, cols)
            a, u = _lru_gate_block(xc, n, wrg_bf, wig_bf, brg_ref, big_ref, lam_ref)
            h = a * h + u
            rows = slice(t * batch, (t + 1) * batch)
            m_ref[rows, cols] = (h * yb_ref[rows, cols].astype(F32)).astype(BF16)
        hlast_ref[:, cols] = h
    for k in range(CONV_WIDTH - 1):
        conv_ref[k] = slab(steps + k, slice(None))


def _lru_sample(xb, yb, tile, steps, conv_state, h0, cw, cb, wrg, wig, brg, big, lam):
    batch = h0.shape[0]
    tok = pl.BlockSpec((TOK_TILE, D_MODEL), lambda i: (tile, 0))
    tok_chunks = pl.BlockSpec((CHUNK_TILE_ROWS, LANES), lambda i: (tile, 0))
    full = lambda a: pl.BlockSpec(a.shape, lambda i: (0,) * a.ndim)
    small = (conv_state, h0, cw, cb, wrg, wig, brg, big, lam)
    return pl.pallas_call(
        functools.partial(_lru_sample_kernel, steps=steps),
        grid=(1,),
        in_specs=[tok_chunks, tok] + [full(a) for a in small],
        out_specs=[
            pl.BlockSpec((TOK_TILE, D_MODEL), lambda i: (0, 0)),
            pl.BlockSpec((CONV_WIDTH - 1, batch, D_MODEL), lambda i: (0, 0, 0)),
            pl.BlockSpec((batch, D_MODEL), lambda i: (0, 0)),
        ],
        out_shape=[
            jax.ShapeDtypeStruct((TOK_TILE, D_MODEL), BF16),
            jax.ShapeDtypeStruct((CONV_WIDTH - 1, batch, D_MODEL), F32),
            jax.ShapeDtypeStruct((batch, D_MODEL), F32),
        ],
        scratch_shapes=[
            pltpu.VMEM((LRU_BLOCKS, LRU_BLOCK, LRU_BLOCK), BF16),
            pltpu.VMEM((LRU_BLOCKS, LRU_BLOCK, LRU_BLOCK), BF16),
        ],
        compiler_params=_params(("arbitrary",), 32),
        name="lru_sample",
    )(xb, yb, *small)


def _rel_bucket(dist):
    n = jnp.maximum(dist, 0)
    max_exact = N_BUCKETS // 2
    nf = jnp.maximum(n, 1).astype(F32)
    large = max_exact + (jnp.log(nf / max_exact) / math.log(MAX_DISTANCE / max_exact)
                         * (N_BUCKETS - max_exact)).astype(jnp.int32)
    large = jnp.minimum(large, N_BUCKETS - 1)
    return jnp.where(n < max_exact, n, large)


def _masked_buckets(dist):
    valid = (dist >= 0) & (dist < WINDOW)
    return jnp.where(valid, _rel_bucket(dist), -1).astype(jnp.int32)


def _build_bias(bucket, tab_ref, head):
    def body(bi, acc):
        return jnp.where(bucket == bi, tab_ref[bi * N_HEADS + head], acc)
    return lax.fori_loop(0, N_BUCKETS, body, jnp.full(bucket.shape, NEG_INF, F32))


def _softmax_pv(s, sink, v):
    m = jnp.maximum(jnp.max(s, axis=-1, keepdims=True), sink)
    p = jnp.exp(s - m)
    den = jnp.sum(p, axis=-1, keepdims=True) + jnp.exp(sink - m)
    return jnp.dot(p.astype(BF16), v, preferred_element_type=F32) / den


def _attn_prompt_kernel(q_ref, kvp_ref, kvc_ref, bucket_ref, tab_ref, sink_ref, o_ref, bias_s):
    b = pl.program_id(0)
    n = pl.program_id(1)

    @pl.when(jnp.logical_and(b == 0, n == 0))
    def _():
        bucket = bucket_ref[...]

        col = lax.broadcasted_iota(jnp.int32, (WINDOW, 2 * WINDOW), 1)

        def head_body(h, c):
            by_offset = jnp.broadcast_to(_build_bias(bucket, tab_ref, h), (WINDOW, 2 * WINDOW))
            bias = pltpu.roll(by_offset, 0, axis=1, stride=1, stride_axis=0)
            sink = sink_ref[h]
            g = h // GROUP
            r0 = pl.multiple_of((h % GROUP) * WINDOW, WINDOW)
            bias_s[0, g, pl.ds(r0, WINDOW), :] = jnp.where(col == 0, sink, bias)
            bias_s[1, g, pl.ds(r0, WINDOW), :] = jnp.where(
                col == 0, sink, jnp.where(col < WINDOW, NEG_INF, bias))
            return c

        lax.fori_loop(0, N_HEADS, head_body, 0)

    first = (n == 0).astype(jnp.int32)
    row = lax.broadcasted_iota(jnp.int32, kvp_ref.shape, 0)
    kv_prev = jnp.where(row == 0, 0.0, kvp_ref[...])
    kv = jnp.concatenate([kv_prev, kvc_ref[...]], axis=0).astype(BF16)
    ones = jnp.ones((2 * WINDOW, 2 * HEAD_DIM), BF16)
    lane = lax.broadcasted_iota(jnp.int32, (WINDOW, 2 * HEAD_DIM), 1)
    def scores(idx):
        g, pair = divmod(idx, GROUP // 2)
        h0 = g * GROUP + 2 * pair
        kg = kv[:, g * HEAD_DIM:(g + 1) * HEAD_DIM]
        qp = jnp.concatenate([q_ref[:, h * HEAD_DIM:(h + 1) * HEAD_DIM] for h in (h0, h0 + 1)], axis=0)
        s = lax.dot_general(qp, kg, (((1,), (1,)), ((), ())), preferred_element_type=F32)
        return s + bias_s[first, g, 2 * pair * WINDOW:(2 * pair + 2) * WINDOW, :]

    def finish(idx, o_ext):
        h0 = 2 * idx
        o = o_ext[:, :2 * HEAD_DIM] * (1.0 / o_ext[:, 2 * HEAD_DIM:])
        o_ref[:, h0 * HEAD_DIM:(h0 + 2) * HEAD_DIM] = jnp.where(
            lane < HEAD_DIM, o[:WINDOW], o[WINDOW:]).astype(BF16)

    n_pairs = N_HEADS // 2

    def values(idx, p):
        g = idx // (GROUP // 2)
        vg = kv[:, KV_DIM + g * HEAD_DIM:KV_DIM + (g + 1) * HEAD_DIM]
        v_ext = jnp.concatenate([vg, vg, ones], axis=1)
        return jnp.dot(p, v_ext, preferred_element_type=F32)

    for first_pair in range(0, n_pairs, PAIR_BLOCK):
        block = range(first_pair, first_pair + PAIR_BLOCK)
        ss = [scores(idx) for idx in block]
        ms = [jnp.max(s, axis=-1, keepdims=True) for s in ss]
        ps = [jnp.exp(s - m).astype(BF16) for s, m in zip(ss, ms)]
        os_ = [values(idx, p) for idx, p in zip(block, ps)]
        for idx, o_ext in zip(block, os_):
            finish(idx, o_ext)


def _attn_prompt(q, kv, batch, seq, bucket, tab, sinks):
    nb = seq // WINDOW
    smem = pl.BlockSpec(memory_space=pltpu.SMEM)
    return pl.pallas_call(
        _attn_prompt_kernel,
        grid=(batch, nb),
        in_specs=[
            pl.BlockSpec((WINDOW, D_MODEL), lambda b, n: (b * nb + n, 0)),
            pl.BlockSpec((WINDOW, 2 * KV_DIM), lambda b, n: (jnp.maximum(b * nb + n - 1, 0), 0)),
            pl.BlockSpec((WINDOW, 2 * KV_DIM), lambda b, n: (b * nb + n, 0)),
            pl.BlockSpec((1, 2 * WINDOW), lambda b, n: (0, 0)),
            smem, smem,
        ],
        out_specs=pl.BlockSpec((WINDOW, D_MODEL), lambda b, n: (b * nb + n, 0)),
        out_shape=jax.ShapeDtypeStruct((batch * seq, D_MODEL), BF16),
        scratch_shapes=[pltpu.VMEM((2, N_KV_HEADS, GROUP * WINDOW, 2 * WINDOW), F32)],
        compiler_params=_params(("arbitrary", "arbitrary"), 32),
        name="attn_prompt",
    )(q, kv, kv, bucket, tab, sinks)


def _attn_sample_kernel(q_ref, ck_ref, cv_ref, kn_ref, vn_ref, bucket_ref, tab_ref, sink_ref,
                        o_ref, kwin_ref, vwin_ref, bias_s):
    steps = q_ref.shape[0]
    k_all = jnp.concatenate([ck_ref[...], kn_ref[...]], axis=0)
    v_all = jnp.concatenate([cv_ref[...], vn_ref[...]], axis=0)
    kwin_ref[...] = k_all[steps:, :]
    vwin_ref[...] = v_all[steps:, :]

    @pl.when(pl.program_id(0) == 0)
    def _():
        bucket = bucket_ref[...]

        def head_body(h, c):
            bias_s[h] = _build_bias(bucket, tab_ref, h)
            return c

        lax.fori_loop(0, N_HEADS, head_body, 0)

    rows = lax.broadcasted_iota(jnp.int32, (GROUP * steps, 1), 0)
    k = k_all.astype(BF16)
    v = v_all.astype(BF16)
    groups = range(N_KV_HEADS)

    def heads_of(g):
        return range(g * GROUP, (g + 1) * GROUP)

    scores, sinks = [], []
    for g in groups:
        qg = jnp.concatenate([q_ref[:, h * HEAD_DIM:(h + 1) * HEAD_DIM] for h in heads_of(g)], axis=0)
        bias = jnp.concatenate([bias_s[h] for h in heads_of(g)], axis=0)
        sink = jnp.full((GROUP * steps, 1), sink_ref[g * GROUP], F32)
        for hh in range(1, GROUP):
            sink = jnp.where(rows >= hh * steps, sink_ref[g * GROUP + hh], sink)
        kg = k[:, g * HEAD_DIM:(g + 1) * HEAD_DIM]
        scores.append(lax.dot_general(qg, kg, (((1,), (1,)), ((), ())),
                                      preferred_element_type=F32) + bias)
        sinks.append(sink)
    outs = [_softmax_pv(scores[g], sinks[g], v[:, g * HEAD_DIM:(g + 1) * HEAD_DIM]) for g in groups]
    for g in groups:
        for hh, h in enumerate(heads_of(g)):
            o_ref[:, h * HEAD_DIM:(h + 1) * HEAD_DIM] = outs[g][hh * steps:(hh + 1) * steps].astype(BF16)


def _attn_sample(q, cache_k, cache_v, k_new, v_new, bucket, tab, sinks):
    batch, steps, _ = q.shape
    lk = WINDOW + steps
    smem = pl.BlockSpec(memory_space=pltpu.SMEM)
    per_seq = lambda rows, cols: pl.BlockSpec((None, rows, cols), lambda b: (b, 0, 0))
    return pl.pallas_call(
        _attn_sample_kernel,
        grid=(batch,),
        in_specs=[
            per_seq(steps, D_MODEL),
            per_seq(WINDOW, KV_DIM), per_seq(WINDOW, KV_DIM),
            per_seq(steps, KV_DIM), per_seq(steps, KV_DIM),
            pl.BlockSpec((steps, lk), lambda b: (0, 0)),
            smem, smem,
        ],
        out_specs=[per_seq(steps, D_MODEL), per_seq(WINDOW, KV_DIM), per_seq(WINDOW, KV_DIM)],
        out_shape=[jax.ShapeDtypeStruct((batch, steps, D_MODEL), BF16),
                   jax.ShapeDtypeStruct((batch, WINDOW, KV_DIM), F32),
                   jax.ShapeDtypeStruct((batch, WINDOW, KV_DIM), F32)],
        scratch_shapes=[pltpu.VMEM((N_HEADS, steps, lk), F32)],
        compiler_params=_params(("arbitrary",), 32),
        name="attn_sample",
    )(q, cache_k, cache_v, k_new, v_new, bucket, tab, sinks)


def kernel(x_prompt, x_sample, state_conv, state_rnn, cache_k_win, cache_v_win, ln_g, ln_b, lru_w_x, lru_b_x, lru_w_y, lru_b_y, lru_conv_w, lru_conv_b, lru_w_rg, lru_b_rg, lru_w_ig, lru_b_ig, lru_lam, lru_w_out, lru_b_out, attn_w_kv, attn_w_q, attn_w_o, attn_sinks, rel_bias, moe_w_router, moe_b_router, moe_w_gate, moe_w_up, moe_w_down):
    bp, seq, _ = x_prompt.shape
    bs, steps, _ = x_sample.shape
    n_p = bp * seq
    n_s = bs * steps

    assert n_p % TOK_TILE == 0 and n_s <= TOK_TILE
    sample_tile = n_p // TOK_TILE

    def pad_tile(rows):
        return jnp.pad(rows, ((0, TOK_TILE - n_s), (0, 0)))

    x0 = (x_prompt.reshape(n_p, D_MODEL),
          pad_tile(x_sample.transpose(1, 0, 2).reshape(n_s, D_MODEL)))
    wr_t = moe_w_router.T
    br = moe_b_router.reshape(N_EXPERTS, 1)
    vec = lambda a: a.reshape(1, -1)

    xb, yb = _lru_in(x0, lru_w_x, vec(lru_b_x[0]), lru_w_y, vec(lru_b_y[0]), 0)
    lru_args = (lru_conv_w[0], vec(lru_conv_b[0]), lru_w_rg[0], lru_w_ig[0],
                vec(lru_b_rg[0]), vec(lru_b_ig[0]), vec(lru_lam[0]))
    m_p, conv_p, rnn_p = _lru_prompt(xb, yb, bp, seq, *lru_args)
    m_s, conv_s, rnn_s = _lru_sample(xb, yb, sample_tile, steps,
                                     state_conv[0].transpose(1, 0, 2), state_rnn[0], *lru_args)
    x1, x1_rows, e_idx, gates = _proj_ln((m_p, m_s), lru_w_out, (0,), vec(lru_b_out[0]), x0,
                                vec(ln_g[0, 0]), vec(ln_b[0, 0]), wr_t, br, name="lru_out_ln")
    ys = _moe_block(x1_rows, e_idx, moe_w_gate, moe_w_up, moe_w_down, 0)

    x2, q, kv = _combine_qkv(ys, x1, gates.T, vec(ln_g[0, 1]), vec(ln_b[0, 1]),
                             attn_w_q, (0,), attn_w_kv, name="moe_combine_qkv")
    tab = rel_bias.reshape(-1)
    sinks = attn_sinks[0]
    offsets = jnp.arange(2 * WINDOW)[None, :]
    o_p = _attn_prompt(q, kv, bp, seq, _masked_buckets(WINDOW - offsets), tab, sinks)
    kv_s = kv[n_p:n_p + n_s].reshape(steps, bs, 2, KV_DIM).transpose(2, 1, 0, 3)
    dist_s = jnp.arange(steps)[:, None] + WINDOW - jnp.arange(WINDOW + steps)[None, :]
    q_s = q[n_p:n_p + n_s].reshape(steps, bs, D_MODEL).transpose(1, 0, 2)
    o_s, k_win_s, v_win_s = _attn_sample(
        q_s, cache_k_win.reshape(bs, WINDOW, KV_DIM), cache_v_win.reshape(bs, WINDOW, KV_DIM),
        kv_s[0], kv_s[1], _masked_buckets(dist_s), tab, sinks)
    o_s = pad_tile(o_s.transpose(1, 0, 2).reshape(n_s, D_MODEL))
    x3, x3_rows, e_idx, gates = _proj_ln((o_p, o_s), attn_w_o, (0,), jnp.zeros((1, D_MODEL), F32), x2,
                                vec(ln_g[1, 0]), vec(ln_b[1, 0]), wr_t, br, name="attn_out_ln")
    ys = _moe_block(x3_rows, e_idx, moe_w_gate, moe_w_up, moe_w_down, 1)
    y_p, y_s = _combine_split(ys, x3, gates.T, vec(ln_g[1, 1]), vec(ln_b[1, 1]), name="moe_combine_1")

    y_prompt = y_p.reshape(bp, seq, D_MODEL)
    y_sample = y_s[:n_s].reshape(steps, bs, D_MODEL).transpose(1, 0, 2)
    kv_p = jnp.stack([kv[(b + 1) * seq - WINDOW:(b + 1) * seq] for b in range(bp)])
    kv_p = kv_p.reshape(bp, WINDOW, 2, N_KV_HEADS, HEAD_DIM)
    k_win_s = k_win_s.reshape(bs, WINDOW, N_KV_HEADS, HEAD_DIM)
    v_win_s = v_win_s.reshape(bs, WINDOW, N_KV_HEADS, HEAD_DIM)
    return (y_prompt, y_sample,
            conv_p[None], rnn_p.reshape(1, bp, D_MODEL),
            kv_p[:, :, 0], kv_p[:, :, 1],
            conv_s.transpose(1, 0, 2)[None], rnn_s[None],
            k_win_s, v_win_s)
```

```python
import functools
import math

import jax
import jax.numpy as jnp
from jax import lax
from jax.experimental import pallas as pl
from jax.experimental.pallas import tpu as pltpu

D_MODEL = 2048
DEPTH = 2
LRU_BLOCKS = 8
LRU_BLOCK = D_MODEL // LRU_BLOCKS
CONV_WIDTH = 4
LRU_C = 8.0
N_HEADS = 32
HEAD_DIM = 64
N_KV_HEADS = 8
GROUP = N_HEADS // N_KV_HEADS
KV_DIM = N_KV_HEADS * HEAD_DIM
WINDOW = 128
N_BUCKETS = 32
MAX_DISTANCE = 128
N_EXPERTS = 16
N_GROUPS = 4
EXPERTS_PER_GROUP = N_EXPERTS // N_GROUPS
D_EXPERT = 1024
ALPHA = (2 * DEPTH) ** 0.25
LN_EPS = 1e-5

LANES = 128
SEGS = 8
CHUNKS = D_MODEL // LANES
ROW_SUBLANES = D_MODEL // (2 * LANES)
MOE_TILE = 256
TOK_TILE = 256
CHUNK_TOK_ROWS = CHUNKS + 4
CHUNK_SEG_TOKS = TOK_TILE // SEGS
CHUNK_SEG_ROWS = CHUNK_SEG_TOKS * CHUNK_TOK_ROWS + 4
CHUNK_TILE_ROWS = SEGS * CHUNK_SEG_ROWS
DMA_UNROLL = 8
SCALAR_UNROLL = 32
PAIR_UNROLL = 16
PLAN_UNROLL = 4
PAIR_BLOCK = 4
GATE_BLOCKS = 4
W_CHUNKS = 4
RING_SLOTS = 3
CAST_ROWS = 256
BF16 = jnp.bfloat16
F32 = jnp.float32
NEG_INF = float("-inf")


def _params(sem, vmem_mb):
    return pltpu.CompilerParams(dimension_semantics=sem, vmem_limit_bytes=vmem_mb * 1024 * 1024)


def _cast_rows(src_ref, dst_ref):
    n = src_ref.shape[0] // CAST_ROWS

    def body(i, c):
        r = pl.multiple_of(i * CAST_ROWS, CAST_ROWS)
        dst_ref[pl.ds(r, CAST_ROWS), :] = src_ref[pl.ds(r, CAST_ROWS), :].astype(BF16)
        return c

    lax.fori_loop(0, n, body, 0)


def _layer_norm(z, g, b):
    mu = jnp.mean(z, axis=-1, keepdims=True)
    zc = z - mu
    var = jnp.mean(zc * zc, axis=-1, keepdims=True)
    return zc * lax.rsqrt(var + LN_EPS) * g + b


def _chunk_row(tok, chunk):
    seg, t = divmod(tok, CHUNK_SEG_TOKS)
    return seg * CHUNK_SEG_ROWS + t * CHUNK_TOK_ROWS + chunk


def _chunk_rows_store(ref, y):
    ref[...] = jnp.zeros_like(ref)
    for seg in range(SEGS):
        rows = slice(seg * CHUNK_SEG_TOKS, (seg + 1) * CHUNK_SEG_TOKS)
        for c in range(CHUNKS):
            dst = pl.ds(_chunk_row(seg * CHUNK_SEG_TOKS, c), CHUNK_SEG_TOKS, stride=CHUNK_TOK_ROWS)
            ref[dst, :] = y[rows, c * LANES:(c + 1) * LANES]


def _chunk_rows_load(ref, tok0, count, chunks=range(CHUNKS)):
    assert tok0 // CHUNK_SEG_TOKS == (tok0 + count - 1) // CHUNK_SEG_TOKS
    return jnp.concatenate(
        [ref[pl.ds(_chunk_row(tok0, c), count, stride=CHUNK_TOK_ROWS), :] for c in chunks], axis=1)


def _store_packed_rows(x_bf, rows_ref):
    n = x_bf.shape[0]
    bits = pltpu.bitcast(x_bf.astype(F32), jnp.uint32)
    packed = bits[:, D_MODEL // 2:] | (bits[:, :D_MODEL // 2] >> 16)
    for c in range(ROW_SUBLANES):
        rows_ref[pl.ds(c, n, stride=ROW_SUBLANES), :] = packed[:, c * LANES:(c + 1) * LANES]


def _load_packed_rows(rows_ref, x_bf_ref):
    n = x_bf_ref.shape[0]
    for c in range(ROW_SUBLANES):
        words = rows_ref[pl.ds(c, n, stride=ROW_SUBLANES), :]
        low = pltpu.bitcast(words << 16, F32).astype(BF16)
        high = pltpu.bitcast(words & jnp.uint32(0xFFFF0000), F32).astype(BF16)
        x_bf_ref[:, c * LANES:(c + 1) * LANES] = low
        x_bf_ref[:, D_MODEL // 2 + c * LANES:D_MODEL // 2 + (c + 1) * LANES] = high


def _tok_operands(x, tile_of=lambda i: i):
    if isinstance(x, tuple):
        xp, xs = x
        d = xp.shape[1]
        last_p = xp.shape[0] // TOK_TILE - 1
        specs = [pl.BlockSpec((TOK_TILE, d), lambda i, *_: (jnp.minimum(tile_of(i), last_p), 0)),
                 pl.BlockSpec((TOK_TILE, d), lambda i, *_: (0, 0))]
        return [xp, xs], specs, last_p + 2
    return ([x], [pl.BlockSpec((TOK_TILE, x.shape[1]), lambda i, *_: (tile_of(i), 0))],
            x.shape[0] // TOK_TILE)


def _tok_load(refs, is_sample=None):
    if len(refs) == 1:
        return refs[0][...]
    if is_sample is None:
        is_sample = pl.program_id(0) == pl.num_programs(0) - 1
    return jnp.where(is_sample, refs[1][...], refs[0][...])


def _lru_in_kernel(*refs, n_x, layer):
    x_refs = refs[:n_x]
    wx_hbm, wy_hbm, bx_ref, by_ref, xb_ref, yb_ref, wx_bf, wy_bf, stage, sems = refs[n_x:]

    @pl.when(pl.program_id(0) == 0)
    def _():
        _load_weight(wx_hbm.at[layer], wx_bf, stage, sems)
        _load_weight(wy_hbm.at[layer], wy_bf, stage, sems)

    x = _tok_load(x_refs).astype(BF16)
    _chunk_rows_store(xb_ref, jnp.dot(x, wx_bf[...], preferred_element_type=F32) + bx_ref[...])
    y = jnp.dot(x, wy_bf[...], preferred_element_type=F32) + by_ref[...]
    yb_ref[...] = jax.nn.gelu(y).astype(BF16)


def _lru_in(x, w_x, b_x, w_y, b_y, layer):
    arrays, specs, nt = _tok_operands(x)
    hbm = pl.BlockSpec(memory_space=pl.ANY)
    vec_spec = pl.BlockSpec((1, D_MODEL), lambda i: (0, 0))
    return pl.pallas_call(
        functools.partial(_lru_in_kernel, n_x=len(arrays), layer=layer),
        grid=(nt,),
        in_specs=specs + [hbm, hbm, vec_spec, vec_spec],
        out_specs=[pl.BlockSpec((CHUNK_TILE_ROWS, LANES), lambda i: (i, 0)),
                   pl.BlockSpec((TOK_TILE, D_MODEL), lambda i: (i, 0))],
        out_shape=[jax.ShapeDtypeStruct((nt * CHUNK_TILE_ROWS, LANES), F32),
                   jax.ShapeDtypeStruct((nt * TOK_TILE, D_MODEL), BF16)],
        scratch_shapes=[pltpu.VMEM((D_MODEL, D_MODEL), BF16), pltpu.VMEM((D_MODEL, D_MODEL), BF16),
                        pltpu.VMEM((2, CAST_ROWS, D_MODEL), F32), pltpu.SemaphoreType.DMA((2,))],
        compiler_params=_params(("arbitrary",), 48),
        name="lru_in",
    )(*arrays, w_x, w_y, b_x, b_y)


def _route(logits_t, b_router):
    aff = jax.nn.sigmoid(logits_t)
    sel = aff + b_router
    srow = [sel[e:e + 1, :] for e in range(N_EXPERTS)]
    arow = [aff[e:e + 1, :] for e in range(N_EXPERTS)]

    def top2_sum(v):
        pairs = [v[i] + v[j] for i in range(4) for j in range(i + 1, 4)]
        return functools.reduce(jnp.maximum, pairs)

    scores = [top2_sum(srow[4 * g:4 * g + 4]) for g in range(N_GROUPS)]
    best = scores[0]
    gi = jnp.zeros_like(best, dtype=jnp.int32)
    for g in range(1, N_GROUPS):
        upd = scores[g] > best
        best = jnp.where(upd, scores[g], best)
        gi = jnp.where(upd, g, gi)

    def pick_group(rows, j):
        out = rows[j]
        for g in range(1, N_GROUPS):
            out = jnp.where(gi == g, rows[4 * g + j], out)
        return out

    v = [pick_group(srow, j) for j in range(EXPERTS_PER_GROUP)]
    a = [pick_group(arow, j) for j in range(EXPERTS_PER_GROUP)]

    m1, i1 = v[0], jnp.zeros_like(gi)
    for j in range(1, EXPERTS_PER_GROUP):
        upd = v[j] > m1
        m1 = jnp.where(upd, v[j], m1)
        i1 = jnp.where(upd, j, i1)
    m2 = jnp.full_like(m1, NEG_INF)
    i2 = jnp.zeros_like(gi)
    for j in range(EXPERTS_PER_GROUP):
        cand = jnp.where(i1 == j, NEG_INF, v[j])
        upd = cand > m2
        m2 = jnp.where(upd, cand, m2)
        i2 = jnp.where(upd, j, i2)

    def pick_idx(rows, idx):
        out = rows[0]
        for j in range(1, EXPERTS_PER_GROUP):
            out = jnp.where(idx == j, rows[j], out)
        return out

    a1 = pick_idx(a, i1)
    a2 = pick_idx(a, i2)
    tot = a1 + a2
    e_idx = jnp.concatenate([gi * EXPERTS_PER_GROUP + i1, gi * EXPERTS_PER_GROUP + i2], axis=0)
    gates = jnp.concatenate([a1 / tot, a2 / tot], axis=0)
    return e_idx, gates


def _proj_ln_kernel(*refs, n_m, n_res):
    m_refs = refs[:n_m]
    w_ref, b_ref = refs[n_m:n_m + 2]
    res_refs = refs[n_m + 2:n_m + 2 + n_res]
    (g_ref, beta_ref, wr_ref, br_ref, x_ref, xrow_ref, e_ref, gate_ref,
     wbf_ref, ya, yb) = refs[n_m + 2 + n_res:]
    i = pl.program_id(0)
    n_tiles = pl.num_programs(0) - 1

    @pl.when(i == 0)
    def _():
        _cast_rows(w_ref, wbf_ref)
        yb[...] = jnp.zeros_like(yb)

    for parity, (cur, prev) in enumerate(((ya, yb), (yb, ya))):
        @pl.when(i % 2 == parity)
        def _():
            cur[...] = jnp.dot(_tok_load(m_refs, i >= n_tiles - 1), wbf_ref[...],
                               preferred_element_type=F32)
            y = prev[...] + b_ref[...]
            x = _layer_norm(ALPHA * _tok_load(res_refs, i == n_tiles) + y, g_ref[...], beta_ref[...])
            x_ref[...] = x
            x_bf = x.astype(BF16)
            _store_packed_rows(x_bf, xrow_ref)
            logits_t = lax.dot_general(wr_ref[...].astype(BF16), x_bf,
                                       (((1,), (1,)), ((), ())), preferred_element_type=F32)
            e_idx, gates = _route(logits_t, br_ref[...])
            e_ref[...] = e_idx
            gate_ref[...] = gates


def _proj_ln(m, w, w_index, b, res, g, beta, wr_t, br, *, name):
    nt = _tok_operands(m)[2]
    m_arrays, m_specs, _ = _tok_operands(m, lambda i: jnp.minimum(i, nt - 1))
    res_arrays, res_specs, _ = _tok_operands(res, lambda i: jnp.maximum(i - 1, 0))
    k = w.shape[-2]
    tm = TOK_TILE
    n = nt * tm
    row = lambda i: (jnp.maximum(i - 1, 0), 0)
    const = lambda i: (0, 0)
    x, x_rows, e_idx, gates = pl.pallas_call(
        functools.partial(_proj_ln_kernel, n_m=len(m_arrays), n_res=len(res_arrays)),
        grid=(nt + 1,),
        in_specs=m_specs + [
            pl.BlockSpec((None,) * len(w_index) + (k, D_MODEL), lambda i: w_index + (0, 0),
                         pipeline_mode=pl.Buffered(1)),
            pl.BlockSpec((1, D_MODEL), const),
        ] + res_specs + [
            pl.BlockSpec((1, D_MODEL), const),
            pl.BlockSpec((1, D_MODEL), const),
            pl.BlockSpec((N_EXPERTS, D_MODEL), const),
            pl.BlockSpec((N_EXPERTS, 1), const),
        ],
        out_specs=[
            pl.BlockSpec((tm, D_MODEL), row),
            pl.BlockSpec((tm * ROW_SUBLANES, LANES), row),
            pl.BlockSpec((None, 2, tm), lambda i: (jnp.maximum(i - 1, 0), 0, 0)),
            pl.BlockSpec((None, 2, tm), lambda i: (jnp.maximum(i - 1, 0), 0, 0)),
        ],
        out_shape=[
            jax.ShapeDtypeStruct((n, D_MODEL), F32),
            jax.ShapeDtypeStruct((n * ROW_SUBLANES, LANES), jnp.uint32),
            jax.ShapeDtypeStruct((nt, 2, tm), jnp.int32),
            jax.ShapeDtypeStruct((nt, 2, tm), F32),
        ],
        scratch_shapes=[pltpu.VMEM((k, D_MODEL), BF16),
                        pltpu.VMEM((tm, D_MODEL), F32), pltpu.VMEM((tm, D_MODEL), F32)],
        compiler_params=_params(("arbitrary",), 52),
        name=name,
    )(*m_arrays, w, b, *res_arrays, g, beta, wr_t, br)
    e_idx = e_idx.transpose(1, 0, 2).reshape(2, n)
    gates = gates.transpose(1, 0, 2).reshape(2, n)
    return x, x_rows, e_idx, gates


def _plan_kernel(e_ref, pos_ref, meta_ref, rank_ref):
    nrow = e_ref.shape[0]
    ri = lax.broadcasted_iota(jnp.int32, (LANES, LANES), 0)
    ci = lax.broadcasted_iota(jnp.int32, (LANES, LANES), 1)
    tri = jnp.where(ri <= ci, 1.0, 0.0).astype(BF16)
    sub = lax.broadcasted_iota(jnp.int32, (N_EXPERTS, LANES), 0)

    def count_body(b, base):
        rows = [b * PLAN_UNROLL + u for u in range(PLAN_UNROLL)]
        onehots = [sub == e_ref[pl.ds(r, 1), :] for r in rows]
        locs = [jnp.dot(jnp.where(oh, 1.0, 0.0).astype(BF16), tri, preferred_element_type=F32)
                for oh in onehots]
        for r, onehot, loc in zip(rows, onehots, locs):
            rank_ref[pl.ds(r, 1), :] = jnp.sum(jnp.where(onehot, base + loc - 1.0, 0.0),
                                               axis=0, keepdims=True)
            base = base + jnp.broadcast_to(loc[:, LANES - 1:LANES], (N_EXPERTS, LANES))
        return base

    count = lax.fori_loop(0, nrow // PLAN_UNROLL, count_body, jnp.zeros((N_EXPERTS, LANES), F32))
    ntile = jnp.floor((count + (MOE_TILE - 1.0)) * (1.0 / MOE_TILE))
    offs = []
    acc = jnp.zeros((1, LANES), F32)
    for e in range(N_EXPERTS):
        offs.append(acc)
        acc = acc + ntile[e:e + 1, :]
    tile_off = jnp.concatenate(offs, axis=0)
    tile_end = tile_off + ntile
    lane = lax.broadcasted_iota(jnp.int32, (N_EXPERTS, LANES), 1).astype(F32)
    tile_expert = jnp.sum(jnp.where(tile_end <= lane, 1.0, 0.0), axis=0, keepdims=True)
    tile_expert = jnp.minimum(tile_expert, N_EXPERTS - 1.0)
    own = jnp.logical_and(tile_off <= lane, lane < tile_end)
    run_end = jnp.sum(jnp.where(own, tile_end, 0.0), axis=0, keepdims=True)
    next_expert = jnp.sum(jnp.where(tile_end <= run_end, 1.0, 0.0), axis=0, keepdims=True)
    has_next = jnp.logical_and(lane[0:1, :] < acc, run_end < acc)
    next_expert = jnp.where(has_next, next_expert, -1.0)
    meta = jnp.concatenate([tile_expert, acc, next_expert, jnp.zeros((5, LANES), F32)], axis=0)
    meta_ref[...] = meta.astype(jnp.int32)
    row_off = tile_off * float(MOE_TILE)

    def pos_body(r, c):
        onehot = sub == e_ref[pl.ds(r, 1), :]
        p = jnp.sum(jnp.where(onehot, row_off, 0.0), axis=0, keepdims=True) + rank_ref[pl.ds(r, 1), :]
        pos_ref[pl.ds(r, 1), :] = p.astype(jnp.int32)
        return c

    lax.fori_loop(0, nrow, pos_body, 0)


def _plan(e_idx, *, name):
    n2 = e_idx.shape[0] * e_idx.shape[1]
    assert n2 % (LANES * PLAN_UNROLL) == 0
    e2d = e_idx.reshape(n2 // LANES, LANES)
    pos, meta = pl.pallas_call(
        _plan_kernel,
        out_shape=[jax.ShapeDtypeStruct(e2d.shape, jnp.int32),
                   jax.ShapeDtypeStruct((8, LANES), jnp.int32)],
        scratch_shapes=[pltpu.VMEM(e2d.shape, F32)],
        name=name,
    )(e2d)
    return pos.reshape(n2), meta[0], meta[1, :1], meta[2]


def _invert_kernel(pos_ref, pair_ref):
    n_rows = pair_ref.shape[0]
    n_pairs = pos_ref.shape[0]

    def fill_body(b, c):
        for u in range(SCALAR_UNROLL):
            pair_ref[b * SCALAR_UNROLL + u] = -1
        return c

    def pair_body(b, c):
        rows = [pos_ref[b * PAIR_UNROLL + u] for u in range(PAIR_UNROLL)]
        for u in range(PAIR_UNROLL):
            pair_ref[rows[u]] = b * PAIR_UNROLL + u
        return c

    lax.fori_loop(0, n_rows // SCALAR_UNROLL, fill_body, 0)
    lax.fori_loop(0, n_pairs // PAIR_UNROLL, pair_body, 0)


def _invert(pos, n_rows, *, name):
    return pl.pallas_call(
        _invert_kernel,
        grid_spec=pltpu.PrefetchScalarGridSpec(
            num_scalar_prefetch=1,
            grid=(1,),
            in_specs=[],
            out_specs=pl.BlockSpec(memory_space=pltpu.SMEM),
        ),
        out_shape=jax.ShapeDtypeStruct((n_rows,), jnp.int32),
        name=name,
    )(pos)


def _expert_changed(te_ref, i):
    return jnp.logical_or(i == 0, te_ref[i] != te_ref[jnp.maximum(i - 1, 0)])


class _ExpertWeights:
    def __init__(self, mats, layer, st, sems):
        self.mats, self.layer, self.st, self.sems = mats, layer, st, sems

    def _copies(self, expert, c):
        out = []
        for w_hbm, stage, _, _ in self.mats:
            rows = stage.shape[1]
            src = w_hbm.at[self.layer, expert, pl.ds(pl.multiple_of(c * rows, rows), rows), :]
            out.append(pltpu.make_async_copy(src, stage.at[c % 2], self.sems.at[c % 2]))
        return out

    def _start(self, expert, c):
        for cp in self._copies(expert, c):
            cp.start()
        self.st[2] = c + 1

    def _convert(self, c_src, c_dst):
        for _, stage, w_next, _ in self.mats:
            rows = stage.shape[1]
            dst = pl.ds(pl.multiple_of(c_dst * rows, rows), rows)
            w_next[dst, :] = stage[c_src % 2].astype(BF16)

    def reset(self):
        for _, stage, _, _ in self.mats:
            stage[...] = jnp.zeros_like(stage)
        self.st[1] = 0
        self.st[2] = 0

    def switch_to(self, expert):
        st = self.st

        def body(c, carry):
            @pl.when(c >= st[2])
            def _():
                self._start(expert, c)

            @pl.when(jnp.logical_and(c + 1 < W_CHUNKS, c + 1 >= st[2]))
            def _():
                self._start(expert, c + 1)

            for cp in self._copies(expert, c):
                cp.wait()
            self._convert(c, c)
            return carry

        lax.fori_loop(st[1], W_CHUNKS, body, 0)
        for _, _, w_next, w_cur in self.mats:
            _copy_rows(w_next, w_cur)
        st[1] = 0
        st[2] = 0

    def begin_step(self, next_expert):
        st = self.st
        done, issued = st[1], st[2]
        has_next = next_expert >= 0
        active = jnp.logical_and(has_next, done < issued)

        @pl.when(jnp.logical_and(has_next, jnp.logical_and(issued < W_CHUNKS, issued < done + 2)))
        def _():
            self._start(next_expert, issued)

        @pl.when(active)
        def _():
            for cp in self._copies(next_expert, done):
                cp.wait()

        return active, done

    def convert_step(self, active, done):
        self._convert(jnp.where(active, done, done + 1), jnp.where(active, done, W_CHUNKS))

    def end_step(self, active, done):
        @pl.when(active)
        def _():
            self.st[1] = done + 1


def _copy_rows(src_ref, dst_ref):
    n = dst_ref.shape[0] // CAST_ROWS

    def body(i, c):
        r = pl.multiple_of(i * CAST_ROWS, CAST_ROWS)
        dst_ref[pl.ds(r, CAST_ROWS), :] = src_ref[pl.ds(r, CAST_ROWS), :]
        return c

    lax.fori_loop(0, n, body, 0)


def _moe_tile_kernel(pair_ref, te_ref, nu_ref, nxt_ref, x_hbm, wg_hbm, wu_hbm, wd_hbm, out_hbm,
                     wg_cur, wg_next, wg_stage, wu_cur, wu_next, wu_stage, wd_cur, wd_next, wd_stage,
                     x_rows, x_bf, ya, yb, gsem, ssems, tsem, wsems, st, *, n_tok, n_tiles, layer):
    i = pl.program_id(0)
    nu = nu_ref[0]
    running = i < nu
    trash = 2 * n_tok
    weights = _ExpertWeights([(wg_hbm, wg_stage, wg_next, wg_cur), (wu_hbm, wu_stage, wu_next, wu_cur),
                              (wd_hbm, wd_stage, wd_next, wd_cur)], layer, st, wsems)

    def gather_copy(tile, r):
        p = pair_ref[tile * MOE_TILE + r]
        tok = jnp.where(p >= n_tok, p - n_tok, jnp.maximum(p, 0))
        src = pl.ds(pl.multiple_of(tok * ROW_SUBLANES, ROW_SUBLANES), ROW_SUBLANES)
        return pltpu.make_async_copy(x_hbm.at[src, :], x_rows.at[pl.ds(r * ROW_SUBLANES, ROW_SUBLANES), :],
                                     gsem)

    def scatter_copy(tile, r, buf, sem):
        p = pair_ref[tile * MOE_TILE + r]
        dst = jnp.where(p < 0, trash + r, p)
        return pltpu.make_async_copy(buf.at[pl.ds(r, 1), :], out_hbm.at[pl.ds(dst, 1), :], sem)

    def wait_scatter(buf, sem):
        pltpu.make_async_copy(buf, out_hbm.at[pl.ds(0, MOE_TILE), :], sem).wait()

    @pl.when(i == 0)
    def _():
        def body(rb, c):
            for u in range(DMA_UNROLL):
                gather_copy(0, rb * DMA_UNROLL + u).start()
            return c
        lax.fori_loop(0, MOE_TILE // DMA_UNROLL, body, 0)
        yb[...] = jnp.zeros_like(yb)
        fill = pltpu.make_async_copy(yb, out_hbm.at[pl.ds(trash, MOE_TILE), :], tsem)
        fill.start()
        fill.wait()
        weights.reset()

    @pl.when(i <= nu)
    def _():
        pltpu.make_async_copy(x_hbm.at[pl.ds(0, MOE_TILE * ROW_SUBLANES), :], x_rows, gsem).wait()

    @pl.when(jnp.logical_and(running, _expert_changed(te_ref, i)))
    def _():
        weights.switch_to(te_ref[i])

    active, done = weights.begin_step(jnp.where(running, nxt_ref[i], -1))

    for parity, (cur, prev) in enumerate(((ya, yb), (yb, ya))):
        cur_sem, prev_sem = ssems.at[parity], ssems.at[1 - parity]
        mine = i % 2 == parity

        @pl.when(jnp.logical_and(mine, jnp.logical_and(i >= 1, i - 1 <= nu)))
        def _():
            wait_scatter(cur, cur_sem)

        @pl.when(jnp.logical_and(mine, running))
        def _():
            _load_packed_rows(x_rows, x_bf)
            next_tile = jnp.minimum(i + 1, n_tiles - 1)
            prev_tile = jnp.maximum(i - 1, 0)
            for r in range(MOE_TILE):
                gather_copy(next_tile, r).start(priority=r % 2)
            for r in range(MOE_TILE):
                scatter_copy(prev_tile, r, prev, prev_sem).start(priority=r % 2)
            weights.convert_step(active, done)
            x = x_bf[...]
            a = jnp.dot(x, wg_cur[...], preferred_element_type=F32)
            b = jnp.dot(x, wu_cur[...], preferred_element_type=F32)
            h = (jax.nn.silu(a) * b).astype(BF16)
            cur[...] = jnp.dot(h, wd_cur[...], preferred_element_type=F32)

        @pl.when(jnp.logical_and(mine, i == nu))
        def _():
            def body(rb, c):
                for u in range(DMA_UNROLL):
                    scatter_copy(i - 1, rb * DMA_UNROLL + u, prev, prev_sem).start()
                return c
            lax.fori_loop(0, MOE_TILE // DMA_UNROLL, body, 0)

            @pl.when(i == n_tiles)
            def _():
                wait_scatter(prev, prev_sem)

    weights.end_step(active, done)


def _moe_tiles(x, pair, tile_expert, n_used, next_expert, w_gate, w_up, w_down, layer):
    n_tok = x.shape[0] // ROW_SUBLANES
    n_tiles = pair.shape[0] // MOE_TILE
    hbm = pl.BlockSpec(memory_space=pl.ANY)

    def weight_bufs(k, n):
        chunk = k // W_CHUNKS
        return [pltpu.VMEM((k, n), BF16), pltpu.VMEM((k + chunk, n), BF16), pltpu.VMEM((2, chunk, n), F32)]

    return pl.pallas_call(
        functools.partial(_moe_tile_kernel, n_tok=n_tok, n_tiles=n_tiles, layer=layer),
        grid_spec=pltpu.PrefetchScalarGridSpec(
            num_scalar_prefetch=4,
            grid=(n_tiles + 1,),
            in_specs=[hbm, hbm, hbm, hbm],
            out_specs=hbm,
            scratch_shapes=weight_bufs(D_MODEL, D_EXPERT) + weight_bufs(D_MODEL, D_EXPERT)
            + weight_bufs(D_EXPERT, D_MODEL) + [
                pltpu.VMEM((MOE_TILE * ROW_SUBLANES, LANES), jnp.uint32),
                pltpu.VMEM((MOE_TILE, D_MODEL), BF16),
                pltpu.VMEM((MOE_TILE, D_MODEL), F32), pltpu.VMEM((MOE_TILE, D_MODEL), F32),
                pltpu.SemaphoreType.DMA(()), pltpu.SemaphoreType.DMA((2,)),
                pltpu.SemaphoreType.DMA(()), pltpu.SemaphoreType.DMA((2,)),
                pltpu.SMEM((4,), jnp.int32)],
        ),
        out_shape=jax.ShapeDtypeStruct((2 * n_tok + MOE_TILE, D_MODEL), F32),
        compiler_params=_params(("arbitrary",), 56),
        name=f"moe_tiles_{layer}",
    )(pair, tile_expert, n_used, next_expert, x, w_gate, w_up, w_down)


def _combine(y0_ref, y1_ref, res_ref, gate_ref, g_ref, beta_ref):
    gate = gate_ref[...]
    ffn = gate[:, 0:1] * y0_ref[...] + gate[:, 1:2] * y1_ref[...]
    return _layer_norm(ALPHA * res_ref[...] + ffn, g_ref[...], beta_ref[...])


def _combine_specs(n):
    nt = n // TOK_TILE
    row = lambda i: (i, 0)
    const = lambda i: (0, 0)
    return [pl.BlockSpec((TOK_TILE, D_MODEL), row),
            pl.BlockSpec((TOK_TILE, D_MODEL), lambda i: (i + nt, 0)),
            pl.BlockSpec((TOK_TILE, D_MODEL), row),
            pl.BlockSpec((TOK_TILE, 2), row),
            pl.BlockSpec((1, D_MODEL), const),
            pl.BlockSpec((1, D_MODEL), const)]


def _combine_split_kernel(ys_hbm, res_hbm, gate_ref, g_ref, beta_ref, prompt_ref, sample_ref,
                          y0_buf, y1_buf, res_buf, sems):
    i = pl.program_id(0)
    nt = pl.num_programs(0)

    def copies(step):
        slot = step % RING_SLOTS
        r0 = pl.multiple_of(step * TOK_TILE, TOK_TILE)
        r1 = pl.multiple_of((step + nt) * TOK_TILE, TOK_TILE)
        sem = sems.at[slot]
        return [pltpu.make_async_copy(ys_hbm.at[pl.ds(r0, TOK_TILE), :], y0_buf.at[slot], sem),
                pltpu.make_async_copy(ys_hbm.at[pl.ds(r1, TOK_TILE), :], y1_buf.at[slot], sem),
                pltpu.make_async_copy(res_hbm.at[pl.ds(r0, TOK_TILE), :], res_buf.at[slot], sem)]

    @pl.when(i == 0)
    def _():
        for step in range(RING_SLOTS - 1):
            for cp in copies(step):
                cp.start()

    @pl.when(i + RING_SLOTS - 1 < nt)
    def _():
        for cp in copies(i + RING_SLOTS - 1):
            cp.start()

    for cp in copies(i):
        cp.wait()
    slot = i % RING_SLOTS
    gate = gate_ref[...]
    ffn = gate[:, 0:1] * y0_buf[slot] + gate[:, 1:2] * y1_buf[slot]
    x = _layer_norm(ALPHA * res_buf[slot] + ffn, g_ref[...], beta_ref[...])
    is_sample = i == nt - 1

    @pl.when(jnp.logical_not(is_sample))
    def _():
        prompt_ref[...] = x

    @pl.when(is_sample)
    def _():
        sample_ref[...] = x


def _combine_split(ys, res, gates_col, g, beta, *, name):
    n = res.shape[0]
    tm = TOK_TILE
    nt = n // tm
    assert nt >= RING_SLOTS
    hbm = pl.BlockSpec(memory_space=pl.ANY)
    const = lambda i: (0, 0)
    ring = pltpu.VMEM((RING_SLOTS, tm, D_MODEL), F32)
    return pl.pallas_call(
        _combine_split_kernel,
        grid=(nt,),
        in_specs=[hbm, hbm, pl.BlockSpec((tm, 2), lambda i: (i, 0)),
                  pl.BlockSpec((1, D_MODEL), const), pl.BlockSpec((1, D_MODEL), const)],
        out_specs=[pl.BlockSpec((tm, D_MODEL), lambda i: (jnp.minimum(i, nt - 2), 0)),
                   pl.BlockSpec((tm, D_MODEL), const)],
        out_shape=[jax.ShapeDtypeStruct((n - tm, D_MODEL), F32),
                   jax.ShapeDtypeStruct((tm, D_MODEL), F32)],
        scratch_shapes=[ring, ring, ring, pltpu.SemaphoreType.DMA((RING_SLOTS,))],
        compiler_params=_params(("arbitrary",), 40),
        name=name,
    )(ys, res, gates_col, g, beta)


def _load_weight(w_hbm, wbf_ref, stage_ref, sems):
    rows = stage_ref.shape[1]
    n_chunks = wbf_ref.shape[0] // rows

    def chunk_copy(c):
        return pltpu.make_async_copy(w_hbm.at[pl.ds(c * rows, rows), :], stage_ref.at[c % 2],
                                     sems.at[c % 2])

    chunk_copy(0).start()
    for c in range(n_chunks):
        if c + 1 < n_chunks:
            chunk_copy(c + 1).start()
        chunk_copy(c).wait()
        wbf_ref[c * rows:(c + 1) * rows, :] = stage_ref[c % 2].astype(BF16)


def _combine_qkv_kernel(y0_ref, y1_ref, res_ref, gate_ref, g_ref, beta_ref, wq_hbm, wkv_hbm,
                        x_ref, q_ref, kv_ref, wq_bf, wkv_bf, stage_q, stage_kv, sems, *, wq_index):
    @pl.when(pl.program_id(0) == 0)
    def _():
        wq = wq_hbm
        for k in wq_index:
            wq = wq.at[k]
        _load_weight(wq, wq_bf, stage_q, sems)
        _load_weight(wkv_hbm, wkv_bf, stage_kv, sems)

    x = _combine(y0_ref, y1_ref, res_ref, gate_ref, g_ref, beta_ref)
    x_ref[...] = x
    x_bf = x.astype(BF16)
    q = jnp.dot(x_bf, wq_bf[...], preferred_element_type=F32) * (HEAD_DIM ** -0.5)
    q_ref[...] = q.astype(BF16)
    kv_ref[...] = jnp.dot(x_bf, wkv_bf[...], preferred_element_type=F32)


def _combine_qkv(ys, res, gates_col, g, beta, w_q, wq_index, w_kv, *, name):
    n = res.shape[0]
    tm = TOK_TILE
    row = lambda i: (i, 0)
    hbm = pl.BlockSpec(memory_space=pl.ANY)
    return pl.pallas_call(
        functools.partial(_combine_qkv_kernel, wq_index=wq_index),
        grid=(n // tm,),
        in_specs=_combine_specs(n) + [hbm, hbm],
        out_specs=[pl.BlockSpec((tm, D_MODEL), row),
                   pl.BlockSpec((tm, D_MODEL), row),
                   pl.BlockSpec((tm, 2 * KV_DIM), row)],
        out_shape=[jax.ShapeDtypeStruct((n, D_MODEL), F32),
                   jax.ShapeDtypeStruct((n, D_MODEL), BF16),
                   jax.ShapeDtypeStruct((n, 2 * KV_DIM), F32)],
        scratch_shapes=[pltpu.VMEM((D_MODEL, D_MODEL), BF16),
                        pltpu.VMEM((D_MODEL, 2 * KV_DIM), BF16),
                        pltpu.VMEM((2, CAST_ROWS, D_MODEL), F32),
                        pltpu.VMEM((2, CAST_ROWS, 2 * KV_DIM), F32),
                        pltpu.SemaphoreType.DMA((2,))],
        compiler_params=_params(("arbitrary",), 52),
        name=name,
    )(ys, ys, res, gates_col, g, beta, w_q, w_kv)


def _moe_block(x_rows, e_idx, w_gate, w_up, w_down, layer):
    n = x_rows.shape[0] // ROW_SUBLANES
    n_tiles = -(-(2 * n + N_EXPERTS * (MOE_TILE - 1)) // MOE_TILE)
    pos, tile_expert, n_used, next_expert = _plan(e_idx, name=f"moe_plan_{layer}")
    pair = _invert(pos, n_tiles * MOE_TILE, name=f"moe_invert_{layer}")
    return _moe_tiles(x_rows, pair, tile_expert, n_used, next_expert, w_gate, w_up, w_down, layer)


def _sigmoid_of_half(half_x):
    return 0.5 * jnp.tanh(half_x) + 0.5


def _log_sigmoid(x):
    return -(jnp.maximum(-x, 0.0) + jnp.log1p(jnp.exp(-jnp.abs(x))))


def _lru_gate_blocks(xcs, blocks, wrg_bf, wig_bf, brg_ref, big_ref, lam_ref):
    xbs = [xc.astype(BF16) for xc in xcs]
    r_lin = [jnp.dot(xb, wrg_bf[n], preferred_element_type=F32) for xb, n in zip(xbs, blocks)]
    i_lin = [jnp.dot(xb, wig_bf[n], preferred_element_type=F32) for xb, n in zip(xbs, blocks)]
    out = []
    for xc, n, rl, il in zip(xcs, blocks, r_lin, i_lin):
        cols = slice(n * LRU_BLOCK, (n + 1) * LRU_BLOCK)
        r = _sigmoid_of_half(rl + 0.5 * brg_ref[:, cols])
        i = _sigmoid_of_half(il + 0.5 * big_ref[:, cols])
        log_a = r * (LRU_C * _log_sigmoid(lam_ref[:, cols]))
        a = jnp.exp(log_a)
        u = xc * i * jnp.sqrt(-jnp.tanh(log_a) * (a * a + 1.0))
        out.append((a, u))
    return out


def _lru_gate_block(xc, n, wrg_bf, wig_bf, brg_ref, big_ref, lam_ref):
    return _lru_gate_blocks([xc], [n], wrg_bf, wig_bf, brg_ref, big_ref, lam_ref)[0]


def _cast_gate_weights(wrg_ref, wig_ref, wrg_bf, wig_bf):
    for n in range(LRU_BLOCKS):
        wrg_bf[n] = (0.5 * wrg_ref[n]).astype(BF16)
        wig_bf[n] = (0.5 * wig_ref[n]).astype(BF16)


def _lru_prompt_kernel(xb_ref, yb_ref, cw_ref, cb_ref, wrg_ref, wig_ref, brg_ref, big_ref, lam_ref,
                       m_ref, conv_ref, hlast_ref, xs, tail, a_s, u_s, hs_t, h_s, wrg_bf, wig_bf):
    b = pl.program_id(0)
    j = pl.program_id(1)
    tt = m_ref.shape[0]
    seg_len = tt // SEGS
    taps = CONV_WIDTH - 1
    head = SEGS * taps

    @pl.when(jnp.logical_and(b == 0, j == 0))
    def _():
        _cast_gate_weights(wrg_ref, wig_ref, wrg_bf, wig_bf)

    @pl.when(j == 0)
    def _():
        tail[...] = jnp.zeros_like(tail)
        h_s[...] = jnp.zeros_like(h_s)

    for q in range(seg_len):
        xs[head + SEGS * q:head + SEGS * (q + 1), :] = jnp.concatenate(
            [xb_ref[pl.ds(_chunk_row(q, c), SEGS, stride=CHUNK_SEG_ROWS), :] for c in range(CHUNKS)],
            axis=1)
    sub = lax.broadcasted_iota(jnp.int32, (SEGS, D_MODEL), 0)
    for k in range(taps):
        last = head + SEGS * (seg_len - taps + k)
        joined = jnp.where(sub == SEGS - 1, tail[SEGS * k:SEGS * (k + 1), :], xs[last:last + SEGS, :])
        xs[SEGS * k:SEGS * (k + 1), :] = pltpu.roll(joined, 1, axis=0)
    tail[...] = xs[head + SEGS * (seg_len - taps):head + SEGS * seg_len, :]

    for first_block in range(0, LRU_BLOCKS, GATE_BLOCKS):
        blocks = range(first_block, first_block + GATE_BLOCKS)
        xcs = []
        for n in blocks:
            cols = slice(n * LRU_BLOCK, (n + 1) * LRU_BLOCK)
            xc = cb_ref[:, cols] + cw_ref[0:1, cols] * xs[0:tt, cols]
            for k in range(1, CONV_WIDTH):
                xc = xc + cw_ref[k:k + 1, cols] * xs[SEGS * k:SEGS * k + tt, cols]
            xcs.append(xc)
        gates = _lru_gate_blocks(xcs, blocks, wrg_bf, wig_bf, brg_ref, big_ref, lam_ref)
        for n, (a, u) in zip(blocks, gates):
            cols = slice(n * LRU_BLOCK, (n + 1) * LRU_BLOCK)
            a_s[:, cols] = a
            u_s[:, cols] = u

    def scan_body(q, carry):
        h, prod = carry
        rows = pl.ds(pl.multiple_of(q * SEGS, SEGS), SEGS)
        a = a_s[rows, :]
        h = a * h + u_s[rows, :]
        prod = a * prod
        u_s[rows, :] = h
        a_s[rows, :] = prod
        return h, prod

    h_end, prod_end = lax.fori_loop(
        0, seg_len, scan_body,
        (jnp.zeros((SEGS, D_MODEL), F32), jnp.ones((SEGS, D_MODEL), F32)))
    state = h_s[...]
    entering = []
    for s in range(SEGS):
        entering.append(state)
        state = h_end[s:s + 1, :] + prod_end[s:s + 1, :] * state
    h_s[...] = state
    enter = jnp.concatenate(entering, axis=0)

    def fix_body(q, carry):
        rows = pl.ds(pl.multiple_of(q * SEGS, SEGS), SEGS)
        h = u_s[rows, :] + a_s[rows, :] * enter
        for c in range(CHUNKS):
            hs_t[pl.ds(q * CHUNK_TOK_ROWS + c, SEGS, stride=CHUNK_SEG_ROWS), :] = h[:, c * LANES:(c + 1) * LANES]
        return carry

    lax.fori_loop(0, seg_len, fix_body, 0)
    for s in range(SEGS):
        rows = slice(s * seg_len, (s + 1) * seg_len)
        hs = _chunk_rows_load(hs_t, s * seg_len, seg_len)
        m_ref[rows, :] = (hs * yb_ref[rows, :].astype(F32)).astype(BF16)

    @pl.when(j == pl.num_programs(1) - 1)
    def _():
        for k in range(taps):
            conv_ref[k:k + 1, :] = tail[SEGS * k + SEGS - 1:SEGS * (k + 1), :]
        hlast_ref[...] = state


def _lru_prompt(xb, yb, batch, seq, cw, cb, wrg, wig, brg, big, lam):
    tt = TOK_TILE
    nj = seq // tt
    row = lambda b, j: (b * nj + j, 0)
    const2 = lambda b, j: (0, 0)
    const3 = lambda b, j: (0, 0, 0)
    return pl.pallas_call(
        _lru_prompt_kernel,
        grid=(batch, nj),
        in_specs=[
            pl.BlockSpec((CHUNK_TILE_ROWS, LANES), row),
            pl.BlockSpec((tt, D_MODEL), row),
            pl.BlockSpec((CONV_WIDTH, D_MODEL), const2),
            pl.BlockSpec((1, D_MODEL), const2),
            pl.BlockSpec((LRU_BLOCKS, LRU_BLOCK, LRU_BLOCK), const3),
            pl.BlockSpec((LRU_BLOCKS, LRU_BLOCK, LRU_BLOCK), const3),
            pl.BlockSpec((1, D_MODEL), const2),
            pl.BlockSpec((1, D_MODEL), const2),
            pl.BlockSpec((1, D_MODEL), const2),
        ],
        out_specs=[
            pl.BlockSpec((tt, D_MODEL), row),
            pl.BlockSpec((None, CONV_WIDTH - 1, D_MODEL), lambda b, j: (b, 0, 0)),
            pl.BlockSpec((None, 1, D_MODEL), lambda b, j: (b, 0, 0)),
        ],
        out_shape=[
            jax.ShapeDtypeStruct((batch * seq, D_MODEL), BF16),
            jax.ShapeDtypeStruct((batch, CONV_WIDTH - 1, D_MODEL), F32),
            jax.ShapeDtypeStruct((batch, 1, D_MODEL), F32),
        ],
        scratch_shapes=[
            pltpu.VMEM((tt + SEGS * (CONV_WIDTH - 1), D_MODEL), F32),
            pltpu.VMEM((SEGS * (CONV_WIDTH - 1), D_MODEL), F32),
            pltpu.VMEM((tt, D_MODEL), F32),
            pltpu.VMEM((tt, D_MODEL), F32),
            pltpu.VMEM((CHUNK_TILE_ROWS, LANES), F32),
            pltpu.VMEM((1, D_MODEL), F32),
            pltpu.VMEM((LRU_BLOCKS, LRU_BLOCK, LRU_BLOCK), BF16),
            pltpu.VMEM((LRU_BLOCKS, LRU_BLOCK, LRU_BLOCK), BF16),
        ],
        compiler_params=_params(("arbitrary", "arbitrary"), 40),
        name="lru_prompt",
    )(xb, yb, cw, cb, wrg, wig, brg, big, lam)


def _lru_sample_kernel(xb_ref, yb_ref, cs_ref, h0_ref, cw_ref, cb_ref, wrg_ref, wig_ref,
                       brg_ref, big_ref, lam_ref, m_ref, conv_ref, hlast_ref, wrg_bf, wig_bf, *, steps):
    batch = h0_ref.shape[0]
    _cast_gate_weights(wrg_ref, wig_ref, wrg_bf, wig_bf)
    m_ref[steps * batch:, :] = jnp.zeros((m_ref.shape[0] - steps * batch, D_MODEL), BF16)

    def slab(t, cols):
        if t < CONV_WIDTH - 1:
            return cs_ref[t, :, cols]
        t -= CONV_WIDTH - 1
        first, stop, _ = cols.indices(D_MODEL)
        return _chunk_rows_load(xb_ref, t * batch, batch, range(first // LANES, stop // LANES))

    for n in range(LRU_BLOCKS):
        cols = slice(n * LRU_BLOCK, (n + 1) * LRU_BLOCK)
        h = h0_ref[:, cols]
        for t in range(steps):
            xc = cb_ref[:, cols] + cw_ref[0:1, cols] * slab(t, cols)
            for k in range(1, CONV_WIDTH):
                xc = xc + cw_ref[k:k + 1, cols] * slab(t + k, cols)
            a, u = _lru_gate_block(xc, n, wrg_bf, wig_bf, brg_ref, big_ref, lam_ref)
            h = a * h + u
            rows = slice(t * batch, (t + 1) * batch)
            m_ref[rows, cols] = (h * yb_ref[rows, cols].astype(F32)).astype(BF16)
        hlast_ref[:, cols] = h
    for k in range(CONV_WIDTH - 1):
        conv_ref[k] = slab(steps + k, slice(None))


def _lru_sample(xb, yb, tile, steps, conv_state, h0, cw, cb, wrg, wig, brg, big, lam):
    batch = h0.shape[0]
    tok = pl.BlockSpec((TOK_TILE, D_MODEL), lambda i: (tile, 0))
    tok_chunks = pl.BlockSpec((CHUNK_TILE_ROWS, LANES), lambda i: (tile, 0))
    full = lambda a: pl.BlockSpec(a.shape, lambda i: (0,) * a.ndim)
    small = (conv_state, h0, cw, cb, wrg, wig, brg, big, lam)
    return pl.pallas_call(
        functools.partial(_lru_sample_kernel, steps=steps),
        grid=(1,),
        in_specs=[tok_chunks, tok] + [full(a) for a in small],
        out_specs=[
            pl.BlockSpec((TOK_TILE, D_MODEL), lambda i: (0, 0)),
            pl.BlockSpec((CONV_WIDTH - 1, batch, D_MODEL), lambda i: (0, 0, 0)),
            pl.BlockSpec((batch, D_MODEL), lambda i: (0, 0)),
        ],
        out_shape=[
            jax.ShapeDtypeStruct((TOK_TILE, D_MODEL), BF16),
            jax.ShapeDtypeStruct((CONV_WIDTH - 1, batch, D_MODEL), F32),
            jax.ShapeDtypeStruct((batch, D_MODEL), F32),
        ],
        scratch_shapes=[
            pltpu.VMEM((LRU_BLOCKS, LRU_BLOCK, LRU_BLOCK), BF16),
            pltpu.VMEM((LRU_BLOCKS, LRU_BLOCK, LRU_BLOCK), BF16),
        ],
        compiler_params=_params(("arbitrary",), 32),
        name="lru_sample",
    )(xb, yb, *small)


def _rel_bucket(dist):
    n = jnp.maximum(dist, 0)
    max_exact = N_BUCKETS // 2
    nf = jnp.maximum(n, 1).astype(F32)
    large = max_exact + (jnp.log(nf / max_exact) / math.log(MAX_DISTANCE / max_exact)
                         * (N_BUCKETS - max_exact)).astype(jnp.int32)
    large = jnp.minimum(large, N_BUCKETS - 1)
    return jnp.where(n < max_exact, n, large)


def _masked_buckets(dist):
    valid = (dist >= 0) & (dist < WINDOW)
    return jnp.where(valid, _rel_bucket(dist), -1).astype(jnp.int32)


def _build_bias(bucket, tab_ref, head):
    def body(bi, acc):
        return jnp.where(bucket == bi, tab_ref[bi * N_HEADS + head], acc)
    return lax.fori_loop(0, N_BUCKETS, body, jnp.full(bucket.shape, NEG_INF, F32))


def _softmax_pv(s, sink, v):
    m = jnp.maximum(jnp.max(s, axis=-1, keepdims=True), sink)
    p = jnp.exp(s - m)
    den = jnp.sum(p, axis=-1, keepdims=True) + jnp.exp(sink - m)
    return jnp.dot(p.astype(BF16), v, preferred_element_type=F32) / den


def _attn_prompt_kernel(q_ref, kvp_ref, kvc_ref, bucket_ref, tab_ref, sink_ref, o_ref, bias_s):
    b = pl.program_id(0)
    n = pl.program_id(1)

    @pl.when(jnp.logical_and(b == 0, n == 0))
    def _():
        bucket = bucket_ref[...]

        col = lax.broadcasted_iota(jnp.int32, (WINDOW, 2 * WINDOW), 1)

        def head_body(h, c):
            by_offset = jnp.broadcast_to(_build_bias(bucket, tab_ref, h), (WINDOW, 2 * WINDOW))
            bias = pltpu.roll(by_offset, 0, axis=1, stride=1, stride_axis=0)
            sink = sink_ref[h]
            g = h // GROUP
            r0 = pl.multiple_of((h % GROUP) * WINDOW, WINDOW)
            bias_s[0, g, pl.ds(r0, WINDOW), :] = jnp.where(col == 0, sink, bias)
            bias_s[1, g, pl.ds(r0, WINDOW), :] = jnp.where(
                col == 0, sink, jnp.where(col < WINDOW, NEG_INF, bias))
            return c

        lax.fori_loop(0, N_HEADS, head_body, 0)

    first = (n == 0).astype(jnp.int32)
    row = lax.broadcasted_iota(jnp.int32, kvp_ref.shape, 0)
    kv_prev = jnp.where(row == 0, 0.0, kvp_ref[...])
    kv = jnp.concatenate([kv_prev, kvc_ref[...]], axis=0).astype(BF16)
    ones = jnp.ones((2 * WINDOW, 2 * HEAD_DIM), BF16)
    lane = lax.broadcasted_iota(jnp.int32, (WINDOW, 2 * HEAD_DIM), 1)
    def scores(idx):
        g, pair = divmod(idx, GROUP // 2)
        h0 = g * GROUP + 2 * pair
        kg = kv[:, g * HEAD_DIM:(g + 1) * HEAD_DIM]
        qp = jnp.concatenate([q_ref[:, h * HEAD_DIM:(h + 1) * HEAD_DIM] for h in (h0, h0 + 1)], axis=0)
        s = lax.dot_general(qp, kg, (((1,), (1,)), ((), ())), preferred_element_type=F32)
        return s + bias_s[first, g, 2 * pair * WINDOW:(2 * pair + 2) * WINDOW, :]

    def finish(idx, o_ext):
        h0 = 2 * idx
        o = o_ext[:, :2 * HEAD_DIM] * (1.0 / o_ext[:, 2 * HEAD_DIM:])
        o_ref[:, h0 * HEAD_DIM:(h0 + 2) * HEAD_DIM] = jnp.where(
            lane < HEAD_DIM, o[:WINDOW], o[WINDOW:]).astype(BF16)

    n_pairs = N_HEADS // 2

    def values(idx, p):
        g = idx // (GROUP // 2)
        vg = kv[:, KV_DIM + g * HEAD_DIM:KV_DIM + (g + 1) * HEAD_DIM]
        v_ext = jnp.concatenate([vg, vg, ones], axis=1)
        return jnp.dot(p, v_ext, preferred_element_type=F32)

    for first_pair in range(0, n_pairs, PAIR_BLOCK):
        block = range(first_pair, first_pair + PAIR_BLOCK)
        ss = [scores(idx) for idx in block]
        ms = [jnp.max(s, axis=-1, keepdims=True) for s in ss]
        ps = [jnp.exp(s - m).astype(BF16) for s, m in zip(ss, ms)]
        os_ = [values(idx, p) for idx, p in zip(block, ps)]
        for idx, o_ext in zip(block, os_):
            finish(idx, o_ext)


def _attn_prompt(q, kv, batch, seq, bucket, tab, sinks):
    nb = seq // WINDOW
    smem = pl.BlockSpec(memory_space=pltpu.SMEM)
    return pl.pallas_call(
        _attn_prompt_kernel,
        grid=(batch, nb),
        in_specs=[
            pl.BlockSpec((WINDOW, D_MODEL), lambda b, n: (b * nb + n, 0)),
            pl.BlockSpec((WINDOW, 2 * KV_DIM), lambda b, n: (jnp.maximum(b * nb + n - 1, 0), 0)),
            pl.BlockSpec((WINDOW, 2 * KV_DIM), lambda b, n: (b * nb + n, 0)),
            pl.BlockSpec((1, 2 * WINDOW), lambda b, n: (0, 0)),
            smem, smem,
        ],
        out_specs=pl.BlockSpec((WINDOW, D_MODEL), lambda b, n: (b * nb + n, 0)),
        out_shape=jax.ShapeDtypeStruct((batch * seq, D_MODEL), BF16),
        scratch_shapes=[pltpu.VMEM((2, N_KV_HEADS, GROUP * WINDOW, 2 * WINDOW), F32)],
        compiler_params=_params(("arbitrary", "arbitrary"), 32),
        name="attn_prompt",
    )(q, kv, kv, bucket, tab, sinks)


def _attn_sample_kernel(q_ref, ck_ref, cv_ref, kn_ref, vn_ref, bucket_ref, tab_ref, sink_ref,
                        o_ref, kwin_ref, vwin_ref, bias_s):
    steps = q_ref.shape[0]
    k_all = jnp.concatenate([ck_ref[...], kn_ref[...]], axis=0)
    v_all = jnp.concatenate([cv_ref[...], vn_ref[...]], axis=0)
    kwin_ref[...] = k_all[steps:, :]
    vwin_ref[...] = v_all[steps:, :]

    @pl.when(pl.program_id(0) == 0)
    def _():
        bucket = bucket_ref[...]

        def head_body(h, c):
            bias_s[h] = _build_bias(bucket, tab_ref, h)
            return c

        lax.fori_loop(0, N_HEADS, head_body, 0)

    rows = lax.broadcasted_iota(jnp.int32, (GROUP * steps, 1), 0)
    k = k_all.astype(BF16)
    v = v_all.astype(BF16)
    groups = range(N_KV_HEADS)

    def heads_of(g):
        return range(g * GROUP, (g + 1) * GROUP)

    scores, sinks = [], []
    for g in groups:
        qg = jnp.concatenate([q_ref[:, h * HEAD_DIM:(h + 1) * HEAD_DIM] for h in heads_of(g)], axis=0)
        bias = jnp.concatenate([bias_s[h] for h in heads_of(g)], axis=0)
        sink = jnp.full((GROUP * steps, 1), sink_ref[g * GROUP], F32)
        for hh in range(1, GROUP):
            sink = jnp.where(rows >= hh * steps, sink_ref[g * GROUP + hh], sink)
        kg = k[:, g * HEAD_DIM:(g + 1) * HEAD_DIM]
        scores.append(lax.dot_general(qg, kg, (((1,), (1,)), ((), ())),
                                      preferred_element_type=F32) + bias)
        sinks.append(sink)
    outs = [_softmax_pv(scores[g], sinks[g], v[:, g * HEAD_DIM:(g + 1) * HEAD_DIM]) for g in groups]
    for g in groups:
        for hh, h in enumerate(heads_of(g)):
            o_ref[:, h * HEAD_DIM:(h + 1) * HEAD_DIM] = outs[g][hh * steps:(hh + 1) * steps].astype(BF16)


def _attn_sample(q, cache_k, cache_v, k_new, v_new, bucket, tab, sinks):
    batch, steps, _ = q.shape
    lk = WINDOW + steps
    smem = pl.BlockSpec(memory_space=pltpu.SMEM)
    per_seq = lambda rows, cols: pl.BlockSpec((None, rows, cols), lambda b: (b, 0, 0))
    return pl.pallas_call(
        _attn_sample_kernel,
        grid=(batch,),
        in_specs=[
            per_seq(steps, D_MODEL),
            per_seq(WINDOW, KV_DIM), per_seq(WINDOW, KV_DIM),
            per_seq(steps, KV_DIM), per_seq(steps, KV_DIM),
            pl.BlockSpec((steps, lk), lambda b: (0, 0)),
            smem, smem,
        ],
        out_specs=[per_seq(steps, D_MODEL), per_seq(WINDOW, KV_DIM), per_seq(WINDOW, KV_DIM)],
        out_shape=[jax.ShapeDtypeStruct((batch, steps, D_MODEL), BF16),
                   jax.ShapeDtypeStruct((batch, WINDOW, KV_DIM), F32),
                   jax.ShapeDtypeStruct((batch, WINDOW, KV_DIM), F32)],
        scratch_shapes=[pltpu.VMEM((N_HEADS, steps, lk), F32)],
        compiler_params=_params(("arbitrary",), 32),
        name="attn_sample",
    )(q, cache_k, cache_v, k_new, v_new, bucket, tab, sinks)


def kernel(x_prompt, x_sample, state_conv, state_rnn, cache_k_win, cache_v_win, ln_g, ln_b, lru_w_x, lru_b_x, lru_w_y, lru_b_y, lru_conv_w, lru_conv_b, lru_w_rg, lru_b_rg, lru_w_ig, lru_b_ig, lru_lam, lru_w_out, lru_b_out, attn_w_kv, attn_w_q, attn_w_o, attn_sinks, rel_bias, moe_w_router, moe_b_router, moe_w_gate, moe_w_up, moe_w_down):
    bp, seq, _ = x_prompt.shape
    bs, steps, _ = x_sample.shape
    n_p = bp * seq
    n_s = bs * steps

    assert n_p % TOK_TILE == 0 and n_s <= TOK_TILE
    sample_tile = n_p // TOK_TILE

    def pad_tile(rows):
        return jnp.pad(rows, ((0, TOK_TILE - n_s), (0, 0)))

    x0 = (x_prompt.reshape(n_p, D_MODEL),
          pad_tile(x_sample.transpose(1, 0, 2).reshape(n_s, D_MODEL)))
    wr_t = moe_w_router.T
    br = moe_b_router.reshape(N_EXPERTS, 1)
    vec = lambda a: a.reshape(1, -1)

    xb, yb = _lru_in(x0, lru_w_x, vec(lru_b_x[0]), lru_w_y, vec(lru_b_y[0]), 0)
    lru_args = (lru_conv_w[0], vec(lru_conv_b[0]), lru_w_rg[0], lru_w_ig[0],
                vec(lru_b_rg[0]), vec(lru_b_ig[0]), vec(lru_lam[0]))
    m_p, conv_p, rnn_p = _lru_prompt(xb, yb, bp, seq, *lru_args)
    m_s, conv_s, rnn_s = _lru_sample(xb, yb, sample_tile, steps,
                                     state_conv[0].transpose(1, 0, 2), state_rnn[0], *lru_args)
    x1, x1_rows, e_idx, gates = _proj_ln((m_p, m_s), lru_w_out, (0,), vec(lru_b_out[0]), x0,
                                vec(ln_g[0, 0]), vec(ln_b[0, 0]), wr_t, br, name="lru_out_ln")
    ys = _moe_block(x1_rows, e_idx, moe_w_gate, moe_w_up, moe_w_down, 0)

    x2, q, kv = _combine_qkv(ys, x1, gates.T, vec(ln_g[0, 1]), vec(ln_b[0, 1]),
                             attn_w_q, (0,), attn_w_kv, name="moe_combine_qkv")
    tab = rel_bias.reshape(-1)
    sinks = attn_sinks[0]
    offsets = jnp.arange(2 * WINDOW)[None, :]
    o_p = _attn_prompt(q, kv, bp, seq, _masked_buckets(WINDOW - offsets), tab, sinks)
    kv_s = kv[n_p:n_p + n_s].reshape(steps, bs, 2, KV_DIM).transpose(2, 1, 0, 3)
    dist_s = jnp.arange(steps)[:, None] + WINDOW - jnp.arange(WINDOW + steps)[None, :]
    q_s = q[n_p:n_p + n_s].reshape(steps, bs, D_MODEL).transpose(1, 0, 2)
    o_s, k_win_s, v_win_s = _attn_sample(
        q_s, cache_k_win.reshape(bs, WINDOW, KV_DIM), cache_v_win.reshape(bs, WINDOW, KV_DIM),
        kv_s[0], kv_s[1], _masked_buckets(dist_s), tab, sinks)
    o_s = pad_tile(o_s.transpose(1, 0, 2).reshape(n_s, D_MODEL))
    x3, x3_rows, e_idx, gates = _proj_ln((o_p, o_s), attn_w_o, (0,), jnp.zeros((1, D_MODEL), F32), x2,
                                vec(ln_g[1, 0]), vec(ln_b[1, 0]), wr_t, br, name="attn_out_ln")
    ys = _moe_block(x3_rows, e_idx, moe_w_gate, moe_w_up, moe_w_down, 1)
    y_p, y_s = _combine_split(ys, x3, gates.T, vec(ln_g[1, 1]), vec(ln_b[1, 1]), name="moe_combine_1")

    y_prompt = y_p.reshape(bp, seq, D_MODEL)
    y_sample = y_s[:n_s].reshape(steps, bs, D_MODEL).transpose(1, 0, 2)
    kv_p = jnp.stack([kv[(b + 1) * seq - WINDOW:(b + 1) * seq] for b in range(bp)])
    kv_p = kv_p.reshape(bp, WINDOW, 2, N_KV_HEADS, HEAD_DIM)
    k_win_s = k_win_s.reshape(bs, WINDOW, N_KV_HEADS, HEAD_DIM)
    v_win_s = v_win_s.reshape(bs, WINDOW, N_KV_HEADS, HEAD_DIM)
    return (y_prompt, y_sample,
            conv_p[None], rnn_p.reshape(1, bp, D_MODEL),
            kv_p[:, :, 0], kv_p[:, :, 1],
            conv_s.transpose(1, 0, 2)[None], rnn_s[None],
            k_win_s, v_win_s)
```
